```python
import math
import jax, jax.numpy as jnp
from jax import lax
import numpy as np

D_MODEL = 1024
BATCH = 8
SEQ = 4096
DEPTH = 2

SSM_GROUP_CH = 16
SSM_GROUPS = D_MODEL // 32
SSM_WIDTH = SSM_GROUPS * SSM_GROUP_CH
SSM_STATE = 64
DT_MIN = 1e-3
DT_MAX = 1e-1
EIG_CLIP = 1e-4
HEAD_DIM = 64
ATTN_HEADS = D_MODEL // 128
ATTN_WIDTH = ATTN_HEADS * HEAD_DIM
Q_BLOCK = 128
N_IN = SSM_WIDTH + 3 * ATTN_WIDTH + ATTN_HEADS + 2 * D_MODEL
D_FF = ((8 * D_MODEL + 3 * 256 - 1) // (3 * 256)) * 256
N_MOD = 6
RMS_EPS = 1e-6

kernel_name = "hybrid_s5_fox_gated_block"


def rmsnorm(x, g):
    xf = x.astype(jnp.float32)
    r = lax.rsqrt(jnp.mean(xf * xf, axis=-1, keepdims=True) + RMS_EPS)
    return (xf * r * g.astype(jnp.float32)).astype(x.dtype)


def _linear_recurrence(e1, e2):
    a1, b1 = e1
    a2, b2 = e2
    return a1 * a2, a2 * b1 + b2


def s5_branch(u, lam_re, lam_im, log_dt, b_re, b_im, c_re, c_im, d_skip, w_glu, b_glu):
    dtype = u.dtype
    bsz, s, _ = u.shape
    f32 = jnp.float32
    uf = u.astype(f32).reshape(bsz, s, SSM_GROUPS, SSM_GROUP_CH)
    lam = lax.complex(jnp.minimum(lam_re.astype(f32), -EIG_CLIP), lam_im.astype(f32))
    dt = jnp.exp(log_dt.astype(f32))[:, None]
    lam_bar = jnp.exp(lam * dt)
    b = lax.complex(b_re.astype(f32), b_im.astype(f32))
    b_bar = ((lam_bar - 1.0) / lam)[..., None] * b
    bu = jnp.einsum('bsgh,gph->bsgp', uf, b_bar)
    a = jnp.broadcast_to(lam_bar, bu.shape)
    _, states = lax.associative_scan(_linear_recurrence, (a, bu), axis=1)
    cm = lax.complex(c_re.astype(f32), c_im.astype(f32))
    y = jnp.real(jnp.einsum('bsgp,ghp->bsgh', states, cm))
    y = y + d_skip.astype(f32).reshape(SSM_GROUPS, SSM_GROUP_CH) * uf
    y = y.reshape(bsz, s, SSM_WIDTH).astype(dtype)
    z = jax.nn.gelu(y)
    return z * jax.nn.sigmoid(z @ w_glu + b_glu)


def forgetting_attention(q, k, v, f_logit, b_f):
    bsz, s, _ = q.shape
    nb = s // Q_BLOCK
    f32 = jnp.float32
    q = q.reshape(bsz, s, ATTN_HEADS, HEAD_DIM).transpose(0, 2, 1, 3)
    k = k.reshape(bsz, s, ATTN_HEADS, HEAD_DIM).transpose(0, 2, 1, 3)
    v = v.reshape(bsz, s, ATTN_HEADS, HEAD_DIM).transpose(0, 2, 1, 3)
    log_f = jax.nn.log_sigmoid(f_logit.astype(f32) + b_f.astype(f32))
    cum = jnp.cumsum(log_f, axis=1).transpose(0, 2, 1)
    q_blocks = q.reshape(bsz, ATTN_HEADS, nb, Q_BLOCK, HEAD_DIM).transpose(2, 0, 1, 3, 4)
    cum_blocks = cum.reshape(bsz, ATTN_HEADS, nb, Q_BLOCK).transpose(2, 0, 1, 3)
    k_pos = jnp.arange(s)
    scale = HEAD_DIM ** -0.5

    def one_block(args):
        qb, cb, i = args
        logits = jnp.einsum('bhqd,bhkd->bhqk', qb, k).astype(f32) * scale
        logits = logits + cb[..., None] - cum[:, :, None, :]
        q_pos = i * Q_BLOCK + jnp.arange(Q_BLOCK)
        logits = jnp.where(k_pos[None, :] <= q_pos[:, None], logits, -jnp.inf)
        p = jax.nn.softmax(logits, axis=-1).astype(v.dtype)
        return jnp.einsum('bhqk,bhkd->bhqd', p, v)

    out = lax.map(one_block, (q_blocks, cum_blocks, jnp.arange(nb)))
    return out.transpose(1, 0, 3, 2, 4).reshape(bsz, s, ATTN_WIDTH)


def _fwd_setup_inputs(seed: int = 0) -> dict:
    key = jax.random.key(seed)
    ks = jax.random.split(key, 32)
    f32 = jnp.float32
    nrm = lambda k, shape, s: jax.random.normal(k, shape, f32) * s
    L, D, G, P, H = DEPTH, D_MODEL, SSM_GROUPS, SSM_STATE, SSM_GROUP_CH
    lam_im0 = jnp.pi * jnp.arange(P, dtype=f32)
    return {
        "x": nrm(ks[0], (BATCH, SEQ, D), 1.0),
        "c": nrm(ks[1], (BATCH, D), 1.0),
        "w_ada": nrm(ks[2], (L, D, N_MOD * D), 0.5 * D ** -0.5),
        "b_ada": nrm(ks[3], (L, N_MOD * D), 0.02),
        "g_pre_mix": 1.0 + nrm(ks[4], (L, D), 0.02),
        "g_post_mix": 1.0 + nrm(ks[5], (L, D), 0.02),
        "g_pre_ffn": 1.0 + nrm(ks[6], (L, D), 0.02),
        "g_post_ffn": 1.0 + nrm(ks[7], (L, D), 0.02),
        "w_in": nrm(ks[8], (L, D, N_IN), D ** -0.5),
        "lam_re": -0.5 + nrm(ks[9], (L, G, P), 0.01),
        "lam_im": lam_im0 + nrm(ks[10], (L, G, P), 0.01),
        "log_dt": jax.random.uniform(ks[11], (L, G), f32, math.log(DT_MIN), math.log(DT_MAX)),
        "b_re": nrm(ks[12], (L, G, P, H), (2 * H) ** -0.5),
        "b_im": nrm(ks[13], (L, G, P, H), (2 * H) ** -0.5),
        "c_re": nrm(ks[14], (L, G, H, P), (2 * P) ** -0.5),
        "c_im": nrm(ks[15], (L, G, H, P), (2 * P) ** -0.5),
        "d_skip": nrm(ks[16], (L, SSM_WIDTH), 1.0),
        "w_glu": nrm(ks[17], (L, SSM_WIDTH, SSM_WIDTH), SSM_WIDTH ** -0.5),
        "b_glu": nrm(ks[18], (L, SSM_WIDTH), 0.02),
        "b_f": jax.random.uniform(ks[19], (L, ATTN_HEADS), f32, 1.0, 5.0),
        "w_pa": nrm(ks[20], (L, SSM_WIDTH, D), SSM_WIDTH ** -0.5),
        "w_pb": nrm(ks[21], (L, ATTN_WIDTH, D), ATTN_WIDTH ** -0.5),
        "w_o": nrm(ks[22], (L, D, D), D ** -0.5),
        "w_ffn_gate": nrm(ks[23], (L, D, D_FF), D ** -0.5),
        "w_ffn_up": nrm(ks[24], (L, D, D_FF), D ** -0.5),
        "w_ffn_down": nrm(ks[25], (L, D_FF, D), D_FF ** -0.5),
    }


def _fwd_reference(x, c, w_ada, b_ada, g_pre_mix, g_post_mix, g_pre_ffn, g_post_ffn, w_in,
              lam_re, lam_im, log_dt, b_re, b_im, c_re, c_im, d_skip, w_glu, b_glu, b_f,
              w_pa, w_pb, w_o, w_ffn_gate, w_ffn_up, w_ffn_down):
    split_at = np.cumsum([SSM_WIDTH, ATTN_WIDTH, ATTN_WIDTH, ATTN_WIDTH, ATTN_HEADS, D_MODEL]).tolist()
    cond = jax.nn.silu(c)
    for l in range(DEPTH):
        mod = cond @ w_ada[l] + b_ada[l]
        shift_m, scale_m, gate_m, shift_f, scale_f, gate_f = jnp.split(mod[:, None, :], N_MOD, axis=-1)

        h = rmsnorm(x, g_pre_mix[l]) * (1.0 + scale_m) + shift_m
        proj = h @ w_in[l]
        u_ssm, q, k, v, f_logit, g_a, g_b = jnp.split(proj, split_at, axis=-1)
        y_ssm = s5_branch(u_ssm, lam_re[l], lam_im[l], log_dt[l], b_re[l], b_im[l],
                          c_re[l], c_im[l], d_skip[l], w_glu[l], b_glu[l])
        y_att = forgetting_attention(q, k, v, f_logit, b_f[l])
        merged = jax.nn.sigmoid(g_a) * (y_ssm @ w_pa[l]) + jax.nn.sigmoid(g_b) * (y_att @ w_pb[l])
        y = merged @ w_o[l]
        x = x + gate_m * rmsnorm(y, g_post_mix[l])

        h = rmsnorm(x, g_pre_ffn[l]) * (1.0 + scale_f) + shift_f
        y = (jax.nn.silu(h @ w_ffn_gate[l]) * (h @ w_ffn_up[l])) @ w_ffn_down[l]
        x = x + gate_f * rmsnorm(y, g_post_ffn[l])
    return x


import jax as _jax
import jax.numpy as _jnp

TWIN_FORMAT = 'train_step'
FWD_PARAMS = ['x', 'c', 'w_ada', 'b_ada', 'g_pre_mix', 'g_post_mix', 'g_pre_ffn', 'g_post_ffn', 'w_in', 'lam_re', 'lam_im', 'log_dt', 'b_re', 'b_im', 'c_re', 'c_im', 'd_skip', 'w_glu', 'b_glu', 'b_f', 'w_pa', 'w_pb', 'w_o', 'w_ffn_gate', 'w_ffn_up', 'w_ffn_down']
TWIN_WEIGHTS = ['w_ada', 'b_ada', 'g_pre_mix', 'g_post_mix', 'g_pre_ffn', 'g_post_ffn', 'w_in', 'lam_re', 'lam_im', 'log_dt', 'b_re', 'b_im', 'c_re', 'c_im', 'd_skip', 'w_glu', 'b_glu', 'b_f', 'w_pa', 'w_pb', 'w_o', 'w_ffn_gate', 'w_ffn_up', 'w_ffn_down']
TWIN_DIFF_INPUT = 'x'
TWIN_INPUTS = ['x', 'c', 'w_ada', 'b_ada', 'g_pre_mix', 'g_post_mix', 'g_pre_ffn', 'g_post_ffn', 'w_in', 'lam_re', 'lam_im', 'log_dt', 'b_re', 'b_im', 'c_re', 'c_im', 'd_skip', 'w_glu', 'b_glu', 'b_f', 'w_pa', 'w_pb', 'w_o', 'w_ffn_gate', 'w_ffn_up', 'w_ffn_down', 'loss_target', 'm_w_ada', 'm_b_ada', 'm_g_pre_mix', 'm_g_post_mix', 'm_g_pre_ffn', 'm_g_post_ffn', 'm_w_in', 'm_lam_re', 'm_lam_im', 'm_log_dt', 'm_b_re', 'm_b_im', 'm_c_re', 'm_c_im', 'm_d_skip', 'm_w_glu', 'm_b_glu', 'm_b_f', 'm_w_pa', 'm_w_pb', 'm_w_o', 'm_w_ffn_gate', 'm_w_ffn_up', 'm_w_ffn_down', 'v_w_ada', 'v_b_ada', 'v_g_pre_mix', 'v_g_post_mix', 'v_g_pre_ffn', 'v_g_post_ffn', 'v_w_in', 'v_lam_re', 'v_lam_im', 'v_log_dt', 'v_b_re', 'v_b_im', 'v_c_re', 'v_c_im', 'v_d_skip', 'v_w_glu', 'v_b_glu', 'v_b_f', 'v_w_pa', 'v_w_pb', 'v_w_o', 'v_w_ffn_gate', 'v_w_ffn_up', 'v_w_ffn_down']
TWIN_OUTPUTS = ['loss', 'grad_x', 'grad_w_ada', 'grad_b_ada', 'grad_g_pre_mix', 'grad_g_post_mix', 'grad_g_pre_ffn', 'grad_g_post_ffn', 'grad_w_in', 'grad_lam_re', 'grad_lam_im', 'grad_log_dt', 'grad_b_re', 'grad_b_im', 'grad_c_re', 'grad_c_im', 'grad_d_skip', 'grad_w_glu', 'grad_b_glu', 'grad_b_f', 'grad_w_pa', 'grad_w_pb', 'grad_w_o', 'grad_w_ffn_gate', 'grad_w_ffn_up', 'grad_w_ffn_down', 'delta_w_ada', 'delta_b_ada', 'delta_g_pre_mix', 'delta_g_post_mix', 'delta_g_pre_ffn', 'delta_g_post_ffn', 'delta_w_in', 'delta_lam_re', 'delta_lam_im', 'delta_log_dt', 'delta_b_re', 'delta_b_im', 'delta_c_re', 'delta_c_im', 'delta_d_skip', 'delta_w_glu', 'delta_b_glu', 'delta_b_f', 'delta_w_pa', 'delta_w_pb', 'delta_w_o', 'delta_w_ffn_gate', 'delta_w_ffn_up', 'delta_w_ffn_down', 'new_m_w_ada', 'new_m_b_ada', 'new_m_g_pre_mix', 'new_m_g_post_mix', 'new_m_g_pre_ffn', 'new_m_g_post_ffn', 'new_m_w_in', 'new_m_lam_re', 'new_m_lam_im', 'new_m_log_dt', 'new_m_b_re', 'new_m_b_im', 'new_m_c_re', 'new_m_c_im', 'new_m_d_skip', 'new_m_w_glu', 'new_m_b_glu', 'new_m_b_f', 'new_m_w_pa', 'new_m_w_pb', 'new_m_w_o', 'new_m_w_ffn_gate', 'new_m_w_ffn_up', 'new_m_w_ffn_down', 'new_v_w_ada', 'new_v_b_ada', 'new_v_g_pre_mix', 'new_v_g_post_mix', 'new_v_g_pre_ffn', 'new_v_g_post_ffn', 'new_v_w_in', 'new_v_lam_re', 'new_v_lam_im', 'new_v_log_dt', 'new_v_b_re', 'new_v_b_im', 'new_v_c_re', 'new_v_c_im', 'new_v_d_skip', 'new_v_w_glu', 'new_v_b_glu', 'new_v_b_f', 'new_v_w_pa', 'new_v_w_pb', 'new_v_w_o', 'new_v_w_ffn_gate', 'new_v_w_ffn_up', 'new_v_w_ffn_down']
TWIN_LEAF_KINDS = {'loss': 'loss', 'grad_x': 'grad_x', 'grad_w_ada': 'grad_w', 'grad_b_ada': 'grad_w', 'grad_g_pre_mix': 'grad_w', 'grad_g_post_mix': 'grad_w', 'grad_g_pre_ffn': 'grad_w', 'grad_g_post_ffn': 'grad_w', 'grad_w_in': 'grad_w', 'grad_lam_re': 'grad_w', 'grad_lam_im': 'grad_w', 'grad_log_dt': 'grad_w', 'grad_b_re': 'grad_w', 'grad_b_im': 'grad_w', 'grad_c_re': 'grad_w', 'grad_c_im': 'grad_w', 'grad_d_skip': 'grad_w', 'grad_w_glu': 'grad_w', 'grad_b_glu': 'grad_w', 'grad_b_f': 'grad_w', 'grad_w_pa': 'grad_w', 'grad_w_pb': 'grad_w', 'grad_w_o': 'grad_w', 'grad_w_ffn_gate': 'grad_w', 'grad_w_ffn_up': 'grad_w', 'grad_w_ffn_down': 'grad_w', 'delta_w_ada': 'delta_w', 'delta_b_ada': 'delta_w', 'delta_g_pre_mix': 'delta_w', 'delta_g_post_mix': 'delta_w', 'delta_g_pre_ffn': 'delta_w', 'delta_g_post_ffn': 'delta_w', 'delta_w_in': 'delta_w', 'delta_lam_re': 'delta_w', 'delta_lam_im': 'delta_w', 'delta_log_dt': 'delta_w', 'delta_b_re': 'delta_w', 'delta_b_im': 'delta_w', 'delta_c_re': 'delta_w', 'delta_c_im': 'delta_w', 'delta_d_skip': 'delta_w', 'delta_w_glu': 'delta_w', 'delta_b_glu': 'delta_w', 'delta_b_f': 'delta_w', 'delta_w_pa': 'delta_w', 'delta_w_pb': 'delta_w', 'delta_w_o': 'delta_w', 'delta_w_ffn_gate': 'delta_w', 'delta_w_ffn_up': 'delta_w', 'delta_w_ffn_down': 'delta_w', 'new_m_w_ada': 'new_m', 'new_m_b_ada': 'new_m', 'new_m_g_pre_mix': 'new_m', 'new_m_g_post_mix': 'new_m', 'new_m_g_pre_ffn': 'new_m', 'new_m_g_post_ffn': 'new_m', 'new_m_w_in': 'new_m', 'new_m_lam_re': 'new_m', 'new_m_lam_im': 'new_m', 'new_m_log_dt': 'new_m', 'new_m_b_re': 'new_m', 'new_m_b_im': 'new_m', 'new_m_c_re': 'new_m', 'new_m_c_im': 'new_m', 'new_m_d_skip': 'new_m', 'new_m_w_glu': 'new_m', 'new_m_b_glu': 'new_m', 'new_m_b_f': 'new_m', 'new_m_w_pa': 'new_m', 'new_m_w_pb': 'new_m', 'new_m_w_o': 'new_m', 'new_m_w_ffn_gate': 'new_m', 'new_m_w_ffn_up': 'new_m', 'new_m_w_ffn_down': 'new_m', 'new_v_w_ada': 'new_v', 'new_v_b_ada': 'new_v', 'new_v_g_pre_mix': 'new_v', 'new_v_g_post_mix': 'new_v', 'new_v_g_pre_ffn': 'new_v', 'new_v_g_post_ffn': 'new_v', 'new_v_w_in': 'new_v', 'new_v_lam_re': 'new_v', 'new_v_lam_im': 'new_v', 'new_v_log_dt': 'new_v', 'new_v_b_re': 'new_v', 'new_v_b_im': 'new_v', 'new_v_c_re': 'new_v', 'new_v_c_im': 'new_v', 'new_v_d_skip': 'new_v', 'new_v_w_glu': 'new_v', 'new_v_b_glu': 'new_v', 'new_v_b_f': 'new_v', 'new_v_w_pa': 'new_v', 'new_v_w_pb': 'new_v', 'new_v_w_o': 'new_v', 'new_v_w_ffn_gate': 'new_v', 'new_v_w_ffn_up': 'new_v', 'new_v_w_ffn_down': 'new_v'}


def _forward(args):
    return _fwd_reference(*[args[k] for k in FWD_PARAMS])


def _output_shape():
    out = _jax.eval_shape(lambda: _forward(_fwd_setup_inputs(0)))
    return out.shape, out.dtype

N_MICROBATCH = 1
ADAM_LR = 0.001
ADAM_B1 = 0.9
ADAM_B2 = 0.999
ADAM_EPS = 1e-08
ADAM_WD = 0.01
ADAM_STEP = 10
PER_EXAMPLE_BATCH_AXIS = {'x': 0, 'c': 0, 'loss_target': 0}
SHARED_INPUTS = []
_WEIGHT_DTYPES = {'w_ada': _jnp.float32, 'b_ada': _jnp.float32, 'g_pre_mix': _jnp.float32, 'g_post_mix': _jnp.float32, 'g_pre_ffn': _jnp.float32, 'g_post_ffn': _jnp.float32, 'w_in': _jnp.float32, 'lam_re': _jnp.float32, 'lam_im': _jnp.float32, 'log_dt': _jnp.float32, 'b_re': _jnp.float32, 'b_im': _jnp.float32, 'c_re': _jnp.float32, 'c_im': _jnp.float32, 'd_skip': _jnp.float32, 'w_glu': _jnp.float32, 'b_glu': _jnp.float32, 'b_f': _jnp.float32, 'w_pa': _jnp.float32, 'w_pb': _jnp.float32, 'w_o': _jnp.float32, 'w_ffn_gate': _jnp.float32, 'w_ffn_up': _jnp.float32, 'w_ffn_down': _jnp.float32}
MOMENT_SCALE = {'w_ada': 2.019112e+00, 'b_ada': 3.672092e+00, 'g_pre_mix': 2.460508e-01, 'g_post_mix': 4.075187e+00, 'g_pre_ffn': 1.719607e-01, 'g_post_ffn': 3.725776e+00, 'w_in': 3.600789e-01, 'lam_re': 2.758544e-02, 'lam_im': 2.163918e-02, 'log_dt': 1.742827e+00, 'b_re': 1.965738e-02, 'b_im': 2.126220e-02, 'c_re': 4.280280e-02, 'c_im': 4.110333e-02, 'd_skip': 6.731371e-01, 'w_glu': 9.512354e-02, 'b_glu': 2.699911e-01, 'b_f': 5.273694e-01, 'w_pa': 4.300918e-01, 'w_pb': 7.485377e-01, 'w_o': 8.720870e-01, 'w_ffn_gate': 8.007105e-02, 'w_ffn_up': 1.069171e-01, 'w_ffn_down': 1.817811e-01}


def _to_microbatches(a, axis):
    t = _jnp.moveaxis(a, axis, 0)
    t = t.reshape((N_MICROBATCH, t.shape[0] // N_MICROBATCH) + t.shape[1:])
    return _jnp.moveaxis(t, 1, axis + 1)


def setup_inputs(seed: int = 0) -> dict:
    inp = _fwd_setup_inputs(seed)
    key = _jax.random.fold_in(_jax.random.key(seed), 7919)
    shape, _ = _output_shape()
    out = dict(inp)
    out["loss_target"] = _jax.random.normal(_jax.random.fold_in(key, 0), shape, _jnp.float32)
    for i, name in enumerate(TWIN_WEIGHTS):
        w = inp[name].astype(_jnp.float32)
        if MOMENT_SCALE is None:
            s = _jnp.sqrt(_jnp.mean(_jnp.square(w)) + 1e-30)
        else:
            s = MOMENT_SCALE[name]
        km, kv = _jax.random.split(_jax.random.fold_in(key, i + 1))
        out[name] = w
        out["m_" + name] = s * _jax.random.normal(km, w.shape, _jnp.float32)
        out["v_" + name] = (s * s) * _jax.random.uniform(kv, w.shape, _jnp.float32, 0.5, 1.5)
    if N_MICROBATCH > 1:
        for name, axis in PER_EXAMPLE_BATCH_AXIS.items():
            out[name] = _to_microbatches(out[name], axis)
    return {'x': out['x'], 'c': out['c'], 'w_ada': out['w_ada'], 'b_ada': out['b_ada'], 'g_pre_mix': out['g_pre_mix'], 'g_post_mix': out['g_post_mix'], 'g_pre_ffn': out['g_pre_ffn'], 'g_post_ffn': out['g_post_ffn'], 'w_in': out['w_in'], 'lam_re': out['lam_re'], 'lam_im': out['lam_im'], 'log_dt': out['log_dt'], 'b_re': out['b_re'], 'b_im': out['b_im'], 'c_re': out['c_re'], 'c_im': out['c_im'], 'd_skip': out['d_skip'], 'w_glu': out['w_glu'], 'b_glu': out['b_glu'], 'b_f': out['b_f'], 'w_pa': out['w_pa'], 'w_pb': out['w_pb'], 'w_o': out['w_o'], 'w_ffn_gate': out['w_ffn_gate'], 'w_ffn_up': out['w_ffn_up'], 'w_ffn_down': out['w_ffn_down'], 'loss_target': out['loss_target'], 'm_w_ada': out['m_w_ada'], 'm_b_ada': out['m_b_ada'], 'm_g_pre_mix': out['m_g_pre_mix'], 'm_g_post_mix': out['m_g_post_mix'], 'm_g_pre_ffn': out['m_g_pre_ffn'], 'm_g_post_ffn': out['m_g_post_ffn'], 'm_w_in': out['m_w_in'], 'm_lam_re': out['m_lam_re'], 'm_lam_im': out['m_lam_im'], 'm_log_dt': out['m_log_dt'], 'm_b_re': out['m_b_re'], 'm_b_im': out['m_b_im'], 'm_c_re': out['m_c_re'], 'm_c_im': out['m_c_im'], 'm_d_skip': out['m_d_skip'], 'm_w_glu': out['m_w_glu'], 'm_b_glu': out['m_b_glu'], 'm_b_f': out['m_b_f'], 'm_w_pa': out['m_w_pa'], 'm_w_pb': out['m_w_pb'], 'm_w_o': out['m_w_o'], 'm_w_ffn_gate': out['m_w_ffn_gate'], 'm_w_ffn_up': out['m_w_ffn_up'], 'm_w_ffn_down': out['m_w_ffn_down'], 'v_w_ada': out['v_w_ada'], 'v_b_ada': out['v_b_ada'], 'v_g_pre_mix': out['v_g_pre_mix'], 'v_g_post_mix': out['v_g_post_mix'], 'v_g_pre_ffn': out['v_g_pre_ffn'], 'v_g_post_ffn': out['v_g_post_ffn'], 'v_w_in': out['v_w_in'], 'v_lam_re': out['v_lam_re'], 'v_lam_im': out['v_lam_im'], 'v_log_dt': out['v_log_dt'], 'v_b_re': out['v_b_re'], 'v_b_im': out['v_b_im'], 'v_c_re': out['v_c_re'], 'v_c_im': out['v_c_im'], 'v_d_skip': out['v_d_skip'], 'v_w_glu': out['v_w_glu'], 'v_b_glu': out['v_b_glu'], 'v_b_f': out['v_b_f'], 'v_w_pa': out['v_w_pa'], 'v_w_pb': out['v_w_pb'], 'v_w_o': out['v_w_o'], 'v_w_ffn_gate': out['v_w_ffn_gate'], 'v_w_ffn_up': out['v_w_ffn_up'], 'v_w_ffn_down': out['v_w_ffn_down']}


def _loss(weights, diff, rest, loss_target):
    with _jax.named_scope("forward"):
        args = {**rest, TWIN_DIFF_INPUT: diff, **{k: w.astype(_WEIGHT_DTYPES[k]) for k, w in weights.items()}}
        y = _forward(args)
    with _jax.named_scope("loss_head"):
        err = _jnp.square(y.astype(_jnp.float32) - loss_target)
        return 0.5 * _jnp.sum(_jnp.mean(err, axis=-1)) if err.ndim else 0.5 * err


def _adamw(w, g, m, v):
    m = ADAM_B1 * m + (1.0 - ADAM_B1) * g
    v = ADAM_B2 * v + (1.0 - ADAM_B2) * _jnp.square(g)
    m_hat = m / (1.0 - ADAM_B1 ** ADAM_STEP)
    v_hat = v / (1.0 - ADAM_B2 ** ADAM_STEP)
    delta = -ADAM_LR * (m_hat / (_jnp.sqrt(v_hat) + ADAM_EPS) + ADAM_WD * w)
    return delta, m, v


def reference(x, c, w_ada, b_ada, g_pre_mix, g_post_mix, g_pre_ffn, g_post_ffn, w_in, lam_re, lam_im, log_dt, b_re, b_im, c_re, c_im, d_skip, w_glu, b_glu, b_f, w_pa, w_pb, w_o, w_ffn_gate, w_ffn_up, w_ffn_down, loss_target, m_w_ada, m_b_ada, m_g_pre_mix, m_g_post_mix, m_g_pre_ffn, m_g_post_ffn, m_w_in, m_lam_re, m_lam_im, m_log_dt, m_b_re, m_b_im, m_c_re, m_c_im, m_d_skip, m_w_glu, m_b_glu, m_b_f, m_w_pa, m_w_pb, m_w_o, m_w_ffn_gate, m_w_ffn_up, m_w_ffn_down, v_w_ada, v_b_ada, v_g_pre_mix, v_g_post_mix, v_g_pre_ffn, v_g_post_ffn, v_w_in, v_lam_re, v_lam_im, v_log_dt, v_b_re, v_b_im, v_c_re, v_c_im, v_d_skip, v_w_glu, v_b_glu, v_b_f, v_w_pa, v_w_pb, v_w_o, v_w_ffn_gate, v_w_ffn_up, v_w_ffn_down):
    given = dict(x=x, c=c, w_ada=w_ada, b_ada=b_ada, g_pre_mix=g_pre_mix, g_post_mix=g_post_mix, g_pre_ffn=g_pre_ffn, g_post_ffn=g_post_ffn, w_in=w_in, lam_re=lam_re, lam_im=lam_im, log_dt=log_dt, b_re=b_re, b_im=b_im, c_re=c_re, c_im=c_im, d_skip=d_skip, w_glu=w_glu, b_glu=b_glu, b_f=b_f, w_pa=w_pa, w_pb=w_pb, w_o=w_o, w_ffn_gate=w_ffn_gate, w_ffn_up=w_ffn_up, w_ffn_down=w_ffn_down, loss_target=loss_target, m_w_ada=m_w_ada, m_b_ada=m_b_ada, m_g_pre_mix=m_g_pre_mix, m_g_post_mix=m_g_post_mix, m_g_pre_ffn=m_g_pre_ffn, m_g_post_ffn=m_g_post_ffn, m_w_in=m_w_in, m_lam_re=m_lam_re, m_lam_im=m_lam_im, m_log_dt=m_log_dt, m_b_re=m_b_re, m_b_im=m_b_im, m_c_re=m_c_re, m_c_im=m_c_im, m_d_skip=m_d_skip, m_w_glu=m_w_glu, m_b_glu=m_b_glu, m_b_f=m_b_f, m_w_pa=m_w_pa, m_w_pb=m_w_pb, m_w_o=m_w_o, m_w_ffn_gate=m_w_ffn_gate, m_w_ffn_up=m_w_ffn_up, m_w_ffn_down=m_w_ffn_down, v_w_ada=v_w_ada, v_b_ada=v_b_ada, v_g_pre_mix=v_g_pre_mix, v_g_post_mix=v_g_post_mix, v_g_pre_ffn=v_g_pre_ffn, v_g_post_ffn=v_g_post_ffn, v_w_in=v_w_in, v_lam_re=v_lam_re, v_lam_im=v_lam_im, v_log_dt=v_log_dt, v_b_re=v_b_re, v_b_im=v_b_im, v_c_re=v_c_re, v_c_im=v_c_im, v_d_skip=v_d_skip, v_w_glu=v_w_glu, v_b_glu=v_b_glu, v_b_f=v_b_f, v_w_pa=v_w_pa, v_w_pb=v_w_pb, v_w_o=v_w_o, v_w_ffn_gate=v_w_ffn_gate, v_w_ffn_up=v_w_ffn_up, v_w_ffn_down=v_w_ffn_down)
    weights = {n: given[n] for n in TWIN_WEIGHTS}
    shared = {n: given[n] for n in SHARED_INPUTS}
    per_example = {n: given[n] for n in ['x', 'c']}
    grad_fn = _jax.value_and_grad(_loss, argnums=(0, 1))

    def one_microbatch(ex, loss_target):
        ex = dict(ex)
        diff = ex.pop(TWIN_DIFF_INPUT)
        return grad_fn(weights, diff, {**shared, **ex}, loss_target)

    if N_MICROBATCH == 1:
        loss, (grad_w, grad_x) = one_microbatch(per_example, given["loss_target"])
    else:
        def body(carry, xs):
            loss_sum, grad_sum = carry
            l_k, (gw_k, gx_k) = one_microbatch(xs[0], xs[1])
            with _jax.named_scope("update"):
                return (loss_sum + l_k, _jax.tree.map(_jnp.add, grad_sum, gw_k)), gx_k

        init = (_jnp.zeros((), _jnp.float32), _jax.tree.map(_jnp.zeros_like, weights))
        (loss, grad_w), grad_x = _jax.lax.scan(body, init, (per_example, given["loss_target"]))
    with _jax.named_scope("update"):
        delta_w, new_m, new_v = {}, {}, {}
        for n in TWIN_WEIGHTS:
            delta_w[n], new_m[n], new_v[n] = _adamw(weights[n], grad_w[n], given["m_" + n], given["v_" + n])
    return (loss, grad_x, *[grad_w[n] for n in TWIN_WEIGHTS], *[delta_w[n] for n in TWIN_WEIGHTS],
            *[new_m[n] for n in TWIN_WEIGHTS], *[new_v[n] for n in TWIN_WEIGHTS])
```

```python
import functools
import math

import jax
import jax.numpy as jnp
from jax import lax
from jax.experimental import pallas as pl
from jax.experimental.pallas import tpu as pltpu

F32 = jnp.float32
BF16 = jnp.bfloat16
MESH = pl.DeviceIdType.MESH

RMS_EPS = 1e-6
EIG_CLIP = 1e-4
ADAM_LR, ADAM_B1, ADAM_B2, ADAM_EPS, ADAM_WD, ADAM_STEP = 0.001, 0.9, 0.999, 1e-08, 0.01, 10

LANES = 128
SUBLANES = 8
VMEM_LIMIT = 56 * 1024 * 1024
S5_ROWS = 256
S5_CHUNK = 256
ATT_BLOCK = 256
FFN_TILE = 256
COMM_LANES = 1024
N_CHIPS = 4
N_DEV = 8

NN = (((1,), (0,)), ((), ()))
NT = (((1,), (1,)), ((), ()))
TN = (((0,), (0,)), ((), ()))
_DN = {"nn": NN, "nt": NT, "tn": TN}


def _cparams(**kw):
    return pltpu.CompilerParams(vmem_limit_bytes=VMEM_LIMIT, **kw)


def _pick(dim, target):
    best, t = None, LANES
    while t <= min(dim, target):
        if dim % t == 0:
            best = t
        t += LANES
    return best or dim


def _sigmoid(x):
    return 1.0 / (1.0 + jnp.exp(-x))


def _dot(a, b, dn):
    return lax.dot_general(a, b, dn, preferred_element_type=F32)


def _mm(name, a, b, mode, out_shapes, out_specs, epilogue, extra=(), extra_specs=(),
        tm=512, tn=512, tk=512, a_fn=None):
    if mode == "nn":
        (m, kd), (_, n) = a.shape, b.shape
    elif mode == "nt":
        (m, kd), (n, _) = a.shape, b.shape
    else:
        (kd, m), (_, n) = a.shape, b.shape
    tm, tn, tk = _pick(m, tm), _pick(n, tn), _pick(kd, tk)
    nk = kd // tk
    n_extra, n_out = len(extra), len(out_shapes)

    def body(*refs):
        a_ref, b_ref = refs[0], refs[1]
        extra_refs = refs[2:2 + n_extra]
        out_refs = refs[2 + n_extra:2 + n_extra + n_out]
        acc = refs[-1]
        k = pl.program_id(2)

        @pl.when(k == 0)
        def _():
            acc[...] = jnp.zeros_like(acc)

        av = a_ref[...]
        if a_fn is not None:
            av = a_fn(av.astype(F32))
        acc[...] += _dot(av.astype(BF16), b_ref[...].astype(BF16), _DN[mode])

        @pl.when(k == nk - 1)
        def _():
            epilogue(acc[...], extra_refs, out_refs)

    if mode == "tn":
        a_spec = pl.BlockSpec((tk, tm), lambda i, j, k: (k, i))
    else:
        a_spec = pl.BlockSpec((tm, tk), lambda i, j, k: (i, k))
    if mode == "nt":
        b_spec = pl.BlockSpec((tn, tk), lambda i, j, k: (j, k))
    else:
        b_spec = pl.BlockSpec((tk, tn), lambda i, j, k: (k, j))
    res = pl.pallas_call(
        body, name=name, grid=(m // tm, n // tn, nk),
        in_specs=[a_spec, b_spec, *extra_specs],
        out_specs=list(out_specs), out_shape=list(out_shapes),
        scratch_shapes=[pltpu.VMEM((tm, tn), F32)],
        compiler_params=_cparams(),
    )(a, b, *extra)
    return res, (tm, tn, tk)


def _mm_plain(name, a, b, mode, out_dtype, add=None, a_fn=None, tm=512, tn=512, tk=512):
    if mode == "nn":
        m, n = a.shape[0], b.shape[1]
    elif mode == "nt":
        m, n = a.shape[0], b.shape[0]
    else:
        m, n = a.shape[1], b.shape[1]
    tm_, tn_ = _pick(m, tm), _pick(n, tn)
    spec = pl.BlockSpec((tm_, tn_), lambda i, j, k: (i, j))

    def epilogue(acc, extra_refs, out_refs):
        if add is not None:
            acc = acc + extra_refs[0][...]
        out_refs[0][...] = acc.astype(out_dtype)

    extra = () if add is None else (add,)
    (out,), _ = _mm(name, a, b, mode, [jax.ShapeDtypeStruct((m, n), out_dtype)], [spec], epilogue,
                    extra=extra, extra_specs=[spec] * len(extra), tm=tm, tn=tn, tk=tk, a_fn=a_fn)
    return out


def _row_tile(s, d):
    return _pick(s, max(SUBLANES, (1 << 20) // (4 * d)))


def _prenorm_fwd(name, x, g, scale, shift):
    s, d = x.shape
    tr = _row_tile(s, d)

    def body(x_ref, g_ref, sc_ref, sh_ref, h_ref):
        xv = x_ref[...]
        r = lax.rsqrt(jnp.mean(xv * xv, axis=-1, keepdims=True) + RMS_EPS)
        h_ref[...] = ((xv * r * g_ref[...]) * (1.0 + sc_ref[...]) + sh_ref[...]).astype(BF16)

    row = pl.BlockSpec((tr, d), lambda i: (i, 0))
    vec = pl.BlockSpec((1, d), lambda i: (0, 0))
    return pl.pallas_call(body, name=name, grid=(s // tr,), in_specs=[row, vec, vec, vec], out_specs=row,
                          out_shape=jax.ShapeDtypeStruct((s, d), BF16), compiler_params=_cparams())(x, g, scale, shift)


def _prenorm_bwd(name, dh, x, g, scale, dx_res):
    s, d = x.shape
    tr = _row_tile(s, d)

    def body(dh_ref, x_ref, g_ref, sc_ref, dxr_ref, dx_ref, sums_ref):
        @pl.when(pl.program_id(0) == 0)
        def _():
            sums_ref[...] = jnp.zeros_like(sums_ref)

        xv, dhv, gv = x_ref[...], dh_ref[...].astype(F32), g_ref[...]
        r = lax.rsqrt(jnp.mean(xv * xv, axis=-1, keepdims=True) + RMS_EPS)
        xhat = xv * r
        dxn = dhv * (1.0 + sc_ref[...])
        dxhat = dxn * gv
        dx = r * (dxhat - xhat * jnp.mean(dxhat * xhat, axis=-1, keepdims=True))
        dx_ref[...] = dxr_ref[...] + dx
        sums_ref[0:1, :] += jnp.sum(dhv * (xhat * gv), axis=0, keepdims=True)
        sums_ref[1:2, :] += jnp.sum(dhv, axis=0, keepdims=True)
        sums_ref[2:3, :] += jnp.sum(dxn * xhat, axis=0, keepdims=True)

    row = pl.BlockSpec((tr, d), lambda i: (i, 0))
    vec = pl.BlockSpec((1, d), lambda i: (0, 0))
    acc = pl.BlockSpec((SUBLANES, d), lambda i: (0, 0))
    return pl.pallas_call(
        body, name=name, grid=(s // tr,), in_specs=[row, row, vec, vec, row], out_specs=[row, acc],
        out_shape=[jax.ShapeDtypeStruct((s, d), F32), jax.ShapeDtypeStruct((SUBLANES, d), F32)],
        compiler_params=_cparams())(dh, x, g, scale, dx_res)


def _postnorm_bwd(name, dxn, y, g, gate):
    s, d = y.shape
    tr = _row_tile(s, d)

    def body(dx_ref, y_ref, g_ref, gt_ref, dy_ref, sums_ref):
        @pl.when(pl.program_id(0) == 0)
        def _():
            sums_ref[...] = jnp.zeros_like(sums_ref)

        yv, dxv, gv = y_ref[...], dx_ref[...], g_ref[...]
        r = lax.rsqrt(jnp.mean(yv * yv, axis=-1, keepdims=True) + RMS_EPS)
        yhat = yv * r
        dn = dxv * gt_ref[...]
        dyhat = dn * gv
        dy_ref[...] = (r * (dyhat - yhat * jnp.mean(dyhat * yhat, axis=-1, keepdims=True))).astype(BF16)
        sums_ref[0:1, :] += jnp.sum(dxv * (yhat * gv), axis=0, keepdims=True)
        sums_ref[1:2, :] += jnp.sum(dn * yhat, axis=0, keepdims=True)

    row = pl.BlockSpec((tr, d), lambda i: (i, 0))
    vec = pl.BlockSpec((1, d), lambda i: (0, 0))
    acc = pl.BlockSpec((SUBLANES, d), lambda i: (0, 0))
    return pl.pallas_call(
        body, name=name, grid=(s // tr,), in_specs=[row, row, vec, vec], out_specs=[row, acc],
        out_shape=[jax.ShapeDtypeStruct((s, d), BF16), jax.ShapeDtypeStruct((SUBLANES, d), F32)],
        compiler_params=_cparams())(dxn, y, g, gate)


def _loss_grad(name, y, target):
    s, d = y.shape
    tr = _row_tile(s, d)

    def body(y_ref, t_ref, dy_ref, loss_ref):
        @pl.when(pl.program_id(0) == 0)
        def _():
            loss_ref[...] = jnp.zeros_like(loss_ref)

        err = y_ref[...] - t_ref[...]
        dy_ref[...] = err * (1.0 / d)
        part = jnp.sum(jnp.sum(err * err, axis=-1, keepdims=True), axis=0, keepdims=True) * (0.5 / d)
        loss_ref[...] += jnp.broadcast_to(part, loss_ref.shape)

    row = pl.BlockSpec((tr, d), lambda i: (i, 0))
    acc = pl.BlockSpec((SUBLANES, LANES), lambda i: (0, 0))
    return pl.pallas_call(
        body, name=name, grid=(s // tr,), in_specs=[row, row], out_specs=[row, acc],
        out_shape=[jax.ShapeDtypeStruct((s, d), F32), jax.ShapeDtypeStruct((SUBLANES, LANES), F32)],
        compiler_params=_cparams())(y, target)


def _gelu(y):
    c = math.sqrt(2.0 / math.pi)
    return 0.5 * y * (1.0 + jnp.tanh(c * (y + 0.044715 * (y * y * y))))


def _gelu_grad(y):
    c = math.sqrt(2.0 / math.pi)
    th = jnp.tanh(c * (y + 0.044715 * (y * y * y)))
    return 0.5 * (1.0 + th) + 0.5 * y * (1.0 - th * th) * c * (1.0 + 3.0 * 0.044715 * (y * y))


def _scan_rows(x_ref, row0, n_groups, ns2, tab_ref, carry_ref, reverse, after_group=None, extra_init=None):
    wc = min(S5_CHUNK, ns2)
    shifts = (1, 2, 4)
    for c0 in range(0, ns2, wc):
        re = slice(c0, c0 + wc)
        im = slice(ns2 + c0, ns2 + c0 + wc)
        tabs = [tab_ref[k, :, re] for k in range(8)]

        def group(i, carry, re=re, im=im, tabs=tabs, c0=c0):
            cr, ci, extra = carry
            g = (n_groups - 1 - i) if reverse else i
            r0 = pl.multiple_of(row0 + g * SUBLANES, SUBLANES)
            br = x_ref[pl.ds(r0, SUBLANES), re]
            bi = x_ref[pl.ds(r0, SUBLANES), im]
            for lvl, k in enumerate(shifts):
                mr, mi = tabs[2 * lvl], tabs[2 * lvl + 1]
                sh = (SUBLANES - k) if reverse else k
                sr = pltpu.roll(br, sh, 0)
                si = pltpu.roll(bi, sh, 0)
                br, bi = br + mr * sr - mi * si, bi + mr * si + mi * sr
            apr, api = tabs[6], tabs[7]
            xr = br + apr * cr - api * ci
            xi = bi + apr * ci + api * cr
            x_ref[pl.ds(r0, SUBLANES), re] = xr
            x_ref[pl.ds(r0, SUBLANES), im] = xi
            if after_group is not None:
                extra = after_group(c0, r0, xr, xi, extra)
            if reverse:
                return xr[0:1, :], xi[0:1, :], extra
            return xr[SUBLANES - 1:SUBLANES, :], xi[SUBLANES - 1:SUBLANES, :], extra

        init_extra = extra_init(wc) if extra_init is not None else 0
        cr, ci, extra = lax.fori_loop(0, n_groups, group, (carry_ref[0:1, re], carry_ref[0:1, im], init_extra))
        carry_ref[0:1, re] = cr
        carry_ref[0:1, im] = ci
        if after_group is not None:
            after_group(c0, None, None, None, extra)


def _s5_fwd(name, u, b_blk, c_blk, tab_f, dskip, w_glu, b_glu):
    s, w = u.shape
    nkb = w // LANES
    ns2 = b_blk.shape[2] // 2 * nkb
    half = ns2 // nkb
    t = min(S5_ROWS, s)
    nblk = s // t

    def body(u_ref, b_ref, c_ref, tab_ref, ds_ref, wg_ref, bg_ref, y_ref, ys_ref, cs_ref, xs, carry):
        @pl.when(pl.program_id(0) == 0)
        def _():
            carry[...] = jnp.zeros_like(carry)

        cs_ref[0] = carry[...]
        for kb in range(nkb):
            bu = _dot(u_ref[:, kb * LANES:(kb + 1) * LANES], b_ref[kb], NN)
            xs[:, kb * half:(kb + 1) * half] = bu[:, :half]
            xs[:, ns2 + kb * half:ns2 + (kb + 1) * half] = bu[:, half:]
        _scan_rows(xs, 0, t // SUBLANES, ns2, tab_ref, carry, reverse=False)
        for kb in range(nkb):
            cols = slice(kb * LANES, (kb + 1) * LANES)
            yk = _dot(xs[:, kb * half:(kb + 1) * half].astype(BF16), c_ref[kb, :half, :], NN)
            yk += _dot(xs[:, ns2 + kb * half:ns2 + (kb + 1) * half].astype(BF16), c_ref[kb, half:, :], NN)
            y_ref[:, cols] = yk + ds_ref[:, cols] * u_ref[:, cols].astype(F32)
        z = _gelu(y_ref[...])
        gate = _sigmoid(_dot(z.astype(BF16), wg_ref[...], NN) + bg_ref[...])
        ys_ref[...] = (z * gate).astype(BF16)

    row = pl.BlockSpec((t, w), lambda i: (i, 0))
    full = lambda shape: pl.BlockSpec(shape, lambda i: (0,) * len(shape))
    return pl.pallas_call(
        body, name=name, grid=(nblk,),
        in_specs=[row, full(b_blk.shape), full(c_blk.shape), full(tab_f.shape), full(dskip.shape),
                  full(w_glu.shape), full(b_glu.shape)],
        out_specs=[row, row, pl.BlockSpec((1, 1, 2 * ns2), lambda i: (i, 0, 0))],
        out_shape=[jax.ShapeDtypeStruct((s, w), F32), jax.ShapeDtypeStruct((s, w), BF16),
                   jax.ShapeDtypeStruct((nblk, 1, 2 * ns2), F32)],
        scratch_shapes=[pltpu.VMEM((t, 2 * ns2), F32), pltpu.VMEM((1, 2 * ns2), F32)],
        compiler_params=_cparams(),
    )(u, b_blk, c_blk, tab_f, dskip, w_glu, b_glu)


def _s5_bwd(name, u, dys, y, carries, b_blk, c_blk, tab_f, tab_r, dskip, w_glu, b_glu):
    s, w = u.shape
    nkb = w // LANES
    ns2 = b_blk.shape[2] // 2 * nkb
    half = ns2 // nkb
    t = min(S5_ROWS, s)
    nblk = s // t
    ng = t // SUBLANES

    def body(u_ref, dys_ref, y_ref, cs_ref, b_ref, c_ref, tabf_ref, tabr_ref, ds_ref, wg_ref, bg_ref,
             du_ref, db_ref, dc_ref, da_ref, dwg_ref, vec_ref, xs, gs, dyv, fcarry, gcarry):
        @pl.when(pl.program_id(0) == 0)
        def _():
            db_ref[...] = jnp.zeros_like(db_ref)
            dc_ref[...] = jnp.zeros_like(dc_ref)
            da_ref[...] = jnp.zeros_like(da_ref)
            dwg_ref[...] = jnp.zeros_like(dwg_ref)
            vec_ref[...] = jnp.zeros_like(vec_ref)
            gcarry[...] = jnp.zeros_like(gcarry)

        yv = y_ref[...]
        z = _gelu(yv)
        zb = z.astype(BF16)
        gate = _sigmoid(_dot(zb, wg_ref[...], NN) + bg_ref[...])
        dout = dys_ref[...].astype(F32)
        dt = dout * z * gate * (1.0 - gate)
        dtb = dt.astype(BF16)
        dz = dout * gate + _dot(dtb, wg_ref[...], NT)
        dy = dz * _gelu_grad(yv)
        dyv[...] = dy
        dwg_ref[...] += _dot(zb, dtb, TN)
        vec_ref[0:1, :] += jnp.sum(dt, axis=0, keepdims=True)
        vec_ref[1:2, :] += jnp.sum(dy * u_ref[...].astype(F32), axis=0, keepdims=True)

        fcarry[...] = cs_ref[0]
        xs[0:SUBLANES, :] = jnp.broadcast_to(cs_ref[0], (SUBLANES, 2 * ns2))
        for kb in range(nkb):
            bu = _dot(u_ref[:, kb * LANES:(kb + 1) * LANES], b_ref[kb], NN)
            xs[SUBLANES:, kb * half:(kb + 1) * half] = bu[:, :half]
            xs[SUBLANES:, ns2 + kb * half:ns2 + (kb + 1) * half] = bu[:, half:]
        _scan_rows(xs, SUBLANES, ng, ns2, tabf_ref, fcarry, reverse=False)

        for kb in range(nkb):
            dyk = dyv[:, kb * LANES:(kb + 1) * LANES].astype(BF16)
            re = slice(kb * half, (kb + 1) * half)
            im = slice(ns2 + kb * half, ns2 + (kb + 1) * half)
            gs[:, re] = _dot(dyk, c_ref[kb, :half, :], NT)
            gs[:, im] = _dot(dyk, c_ref[kb, half:, :], NT)
            dc_ref[kb, :half, :] += _dot(xs[SUBLANES:, re].astype(BF16), dyk, TN)
            dc_ref[kb, half:, :] += _dot(xs[SUBLANES:, im].astype(BF16), dyk, TN)

        row_is_first = lax.broadcasted_iota(jnp.int32, (SUBLANES, min(S5_CHUNK, ns2)), 0) == 0

        def fold(c0, r0, gr, gi, acc):
            wc = min(S5_CHUNK, ns2)
            re = slice(c0, c0 + wc)
            im = slice(ns2 + c0, ns2 + c0 + wc)
            if r0 is None:
                da_ref[:, re] += acc[0]
                da_ref[:, im] += acc[1]
                return acc
            cur_r = xs[pl.ds(r0 + SUBLANES, SUBLANES), re]
            cur_i = xs[pl.ds(r0 + SUBLANES, SUBLANES), im]
            prv_r = xs[pl.ds(r0, SUBLANES), re]
            prv_i = xs[pl.ds(r0, SUBLANES), im]
            xpr = jnp.where(row_is_first, prv_r[SUBLANES - 1:SUBLANES, :], pltpu.roll(cur_r, 1, 0))
            xpi = jnp.where(row_is_first, prv_i[SUBLANES - 1:SUBLANES, :], pltpu.roll(cur_i, 1, 0))
            return acc[0] + gr * xpr + gi * xpi, acc[1] - gr * xpi + gi * xpr

        zero2 = lambda wc: (jnp.zeros((SUBLANES, wc), F32), jnp.zeros((SUBLANES, wc), F32))
        _scan_rows(gs, 0, ng, ns2, tabr_ref, gcarry, reverse=True, after_group=fold, extra_init=zero2)

        for kb in range(nkb):
            cols = slice(kb * LANES, (kb + 1) * LANES)
            re = slice(kb * half, (kb + 1) * half)
            im = slice(ns2 + kb * half, ns2 + (kb + 1) * half)
            uk = u_ref[:, cols]
            gr = gs[:, re].astype(BF16)
            gi = gs[:, im].astype(BF16)
            db_ref[kb, :, :half] += _dot(uk, gr, TN)
            db_ref[kb, :, half:] += _dot(uk, gi, TN)
            duk = _dot(gr, b_ref[kb, :, :half], NT) + _dot(gi, b_ref[kb, :, half:], NT)
            du_ref[:, cols] = (duk + ds_ref[:, cols] * dyv[:, cols]).astype(BF16)

    rev = lambda i: (nblk - 1 - i, 0)
    row = pl.BlockSpec((t, w), rev)
    full = lambda shape: pl.BlockSpec(shape, lambda i: (0,) * len(shape))
    return pl.pallas_call(
        body, name=name, grid=(nblk,),
        in_specs=[row, row, row, pl.BlockSpec((1, 1, 2 * ns2), lambda i: (nblk - 1 - i, 0, 0)),
                  full(b_blk.shape), full(c_blk.shape), full(tab_f.shape), full(tab_r.shape),
                  full(dskip.shape), full(w_glu.shape), full(b_glu.shape)],
        out_specs=[row, full(b_blk.shape), full(c_blk.shape), full((SUBLANES, 2 * ns2)), full((w, w)),
                   full((SUBLANES, w))],
        out_shape=[jax.ShapeDtypeStruct((s, w), BF16), jax.ShapeDtypeStruct(b_blk.shape, F32),
                   jax.ShapeDtypeStruct(c_blk.shape, F32), jax.ShapeDtypeStruct((SUBLANES, 2 * ns2), F32),
                   jax.ShapeDtypeStruct((w, w), F32), jax.ShapeDtypeStruct((SUBLANES, w), F32)],
        scratch_shapes=[pltpu.VMEM((t + SUBLANES, 2 * ns2), F32), pltpu.VMEM((t, 2 * ns2), F32),
                        pltpu.VMEM((t, w), F32), pltpu.VMEM((1, 2 * ns2), F32), pltpu.VMEM((1, 2 * ns2), F32)],
        compiler_params=_cparams(),
    )(u, dys, y, carries, b_blk, c_blk, tab_f, tab_r, dskip, w_glu, b_glu)


def _log_sigmoid(x):
    return jnp.minimum(x, 0.0) - jnp.log(1.0 + jnp.exp(-jnp.abs(x)))


def _cum_fwd(name, f_t, b_f):
    h, s = f_t.shape
    tc = _pick(s, 512)
    nb = s // tc

    def body(f_ref, b_ref, c_ref, carry):
        @pl.when(pl.program_id(0) == 0)
        def _():
            carry[...] = jnp.zeros_like(carry)

        lf = _log_sigmoid(f_ref[...] + b_ref[...])
        upper = (lax.broadcasted_iota(jnp.int32, (tc, tc), 0) <= lax.broadcasted_iota(jnp.int32, (tc, tc), 1))
        cum = lax.dot_general(lf, upper.astype(F32), NN, precision=lax.Precision.HIGHEST,
                              preferred_element_type=F32) + carry[...]
        c_ref[...] = cum
        carry[...] += jnp.sum(lf, axis=1, keepdims=True)

    blk = pl.BlockSpec((h, tc), lambda i: (0, i))
    return pl.pallas_call(body, name=name, grid=(nb,), in_specs=[blk, pl.BlockSpec((h, 1), lambda i: (0, 0))],
                          out_specs=blk, out_shape=jax.ShapeDtypeStruct((h, s), F32),
                          scratch_shapes=[pltpu.VMEM((h, 1), F32)], compiler_params=_cparams())(f_t, b_f)


def _cum_bwd(name, dcq, dck, f_t, b_f):
    h, s = f_t.shape
    tc = _pick(s, 512)
    nb = s // tc

    def body(dcq_ref, dck_ref, f_ref, b_ref, df_ref, db_ref, carry):
        @pl.when(pl.program_id(0) == 0)
        def _():
            carry[...] = jnp.zeros_like(carry)
            db_ref[...] = jnp.zeros_like(db_ref)

        dc = dcq_ref[...] + dck_ref[...]
        lower = (lax.broadcasted_iota(jnp.int32, (tc, tc), 0) >= lax.broadcasted_iota(jnp.int32, (tc, tc), 1))
        dlf = lax.dot_general(dc, lower.astype(F32), NN, precision=lax.Precision.HIGHEST,
                              preferred_element_type=F32) + carry[...]
        carry[...] += jnp.sum(dc, axis=1, keepdims=True)
        df = dlf * _sigmoid(-(f_ref[...] + b_ref[...]))
        df_ref[...] = df
        db_ref[...] += jnp.broadcast_to(jnp.sum(df, axis=1, keepdims=True), db_ref.shape)

    blk = pl.BlockSpec((h, tc), lambda i: (0, nb - 1 - i))
    return pl.pallas_call(
        body, name=name, grid=(nb,), in_specs=[blk, blk, blk, pl.BlockSpec((h, 1), lambda i: (0, 0))],
        out_specs=[blk, pl.BlockSpec((h, LANES), lambda i: (0, 0))],
        out_shape=[jax.ShapeDtypeStruct((h, s), F32), jax.ShapeDtypeStruct((h, LANES), F32)],
        scratch_shapes=[pltpu.VMEM((h, 1), F32)], compiler_params=_cparams())(dcq, dck, f_t, b_f)


def _attn_fwd(name, q, k, v, cq, ck):
    h, s, dh = q.shape
    t = min(ATT_BLOCK, s)
    nq = s // t
    scale = dh ** -0.5

    def body(q_ref, k_ref, v_ref, cq_ref, ck_ref, o_ref, lse_ref):
        i = pl.program_id(1)
        qv = q_ref[0]
        cqv = cq_ref[0]
        causal = (lax.broadcasted_iota(jnp.int32, (t, t), 1) <= lax.broadcasted_iota(jnp.int32, (t, t), 0))

        def step(j, carry, diagonal):
            m, l, acc = carry
            r0 = pl.multiple_of(j * t, t)
            sc = _dot(qv, k_ref[0, pl.ds(r0, t), :], NT) * scale + (cqv - ck_ref[0, j])
            if diagonal:
                sc = jnp.where(causal, sc, -1e30)
            m_new = jnp.maximum(m, jnp.max(sc, axis=1, keepdims=True))
            p = jnp.exp(sc - m_new)
            alpha = jnp.exp(m - m_new)
            l = alpha * l + jnp.sum(p, axis=1, keepdims=True)
            acc = alpha * acc + _dot(p.astype(BF16), v_ref[0, pl.ds(r0, t), :], NN)
            return m_new, l, acc

        init = (jnp.full((t, 1), -1e30, F32), jnp.zeros((t, 1), F32), jnp.zeros((t, dh), F32))
        carry = lax.fori_loop(0, i, lambda j, c: step(j, c, False), init)
        m, l, acc = step(i, carry, True)
        o_ref[0] = (acc / l).astype(BF16)
        lse_ref[0] = m + jnp.log(l)

    qs = pl.BlockSpec((1, t, dh), lambda hh, i: (hh, i, 0))
    kv = pl.BlockSpec((1, s, dh), lambda hh, i: (hh, 0, 0))
    col = pl.BlockSpec((1, t, 1), lambda hh, i: (hh, i, 0))
    return pl.pallas_call(
        body, name=name, grid=(h, nq),
        in_specs=[qs, kv, kv, col, pl.BlockSpec((1, nq, 1, t), lambda hh, i: (hh, 0, 0, 0))],
        out_specs=[qs, col],
        out_shape=[jax.ShapeDtypeStruct((h, s, dh), BF16), jax.ShapeDtypeStruct((h, s, 1), F32)],
        compiler_params=_cparams(),
    )(q, k, v, cq, ck)


def _attn_bwd(name, q, k, v, o, do, lse, cq, ck):
    h, s, dh = q.shape
    t = min(ATT_BLOCK, s)
    nk = s // t
    scale = dh ** -0.5

    def body(q_ref, k_ref, v_ref, o_ref, do_ref, lse_ref, cq_ref, ck_ref,
             dq_ref, dk_ref, dv_ref, dcq_ref, dck_ref, delta, dk_acc, dv_acc, dc_acc):
        j = pl.program_id(1)

        @pl.when(j == 0)
        def _():
            dq_ref[...] = jnp.zeros_like(dq_ref)
            dcq_ref[...] = jnp.zeros_like(dcq_ref)

            def fill(i, _):
                r0 = pl.multiple_of(i * t, t)
                prod = do_ref[0, pl.ds(r0, t), :].astype(F32) * o_ref[0, pl.ds(r0, t), :].astype(F32)
                delta[pl.ds(r0, t), :] = jnp.sum(prod, axis=1, keepdims=True)
                return 0

            lax.fori_loop(0, nk, fill, 0)

        kv_, vv = k_ref[0], v_ref[0]
        ckv = ck_ref[0, 0]
        dk_acc[...] = jnp.zeros_like(dk_acc)
        dv_acc[...] = jnp.zeros_like(dv_acc)
        dc_acc[...] = jnp.zeros_like(dc_acc)
        causal = (lax.broadcasted_iota(jnp.int32, (t, t), 1) <= lax.broadcasted_iota(jnp.int32, (t, t), 0))

        def step(i, diagonal):
            r0 = pl.multiple_of(i * t, t)
            qi = q_ref[0, pl.ds(r0, t), :]
            doi = do_ref[0, pl.ds(r0, t), :]
            sc = _dot(qi, kv_, NT) * scale + (cq_ref[0, pl.ds(r0, t), :] - ckv)
            p = jnp.exp(sc - lse_ref[0, pl.ds(r0, t), :])
            if diagonal:
                p = jnp.where(causal, p, 0.0)
            dp = _dot(doi, vv, NT)
            ds = p * (dp - delta[pl.ds(r0, t), :])
            dsb = ds.astype(BF16)
            dv_acc[...] += _dot(p.astype(BF16), doi, TN)
            dk_acc[...] += _dot(dsb, qi, TN)
            dq_ref[0, pl.ds(r0, t), :] += _dot(dsb, kv_, NN) * scale
            dc_acc[...] -= jnp.sum(ds, axis=0, keepdims=True)
            dcq_ref[0, pl.ds(r0, t), :] += jnp.sum(ds, axis=1, keepdims=True)

        step(j, True)

        def rest(i, _):
            step(i, False)
            return 0

        lax.fori_loop(j + 1, nk, rest, 0)
        dk_ref[0] = (dk_acc[...] * scale).astype(BF16)
        dv_ref[0] = dv_acc[...].astype(BF16)
        dck_ref[0, 0] = dc_acc[...]

    whole = pl.BlockSpec((1, s, dh), lambda hh, j: (hh, 0, 0))
    blk = pl.BlockSpec((1, t, dh), lambda hh, j: (hh, j, 0))
    col = pl.BlockSpec((1, s, 1), lambda hh, j: (hh, 0, 0))
    ckb = pl.BlockSpec((1, 1, 1, t), lambda hh, j: (hh, j, 0, 0))
    return pl.pallas_call(
        body, name=name, grid=(h, nk),
        in_specs=[whole, blk, blk, whole, whole, col, col, ckb],
        out_specs=[whole, blk, blk, col, ckb],
        out_shape=[jax.ShapeDtypeStruct((h, s, dh), F32), jax.ShapeDtypeStruct((h, s, dh), BF16),
                   jax.ShapeDtypeStruct((h, s, dh), BF16), jax.ShapeDtypeStruct((h, s, 1), F32),
                   jax.ShapeDtypeStruct((h, nk, 1, t), F32)],
        scratch_shapes=[pltpu.VMEM((s, 1), F32), pltpu.VMEM((t, dh), F32), pltpu.VMEM((t, dh), F32),
                        pltpu.VMEM((1, t), F32)],
        compiler_params=_cparams(),
    )(q, k, v, o, do, lse, cq, ck)


def _adamw(name, w, g, m, v):
    r, c = w.shape
    tr = _pick8(r, max(SUBLANES, (1 << 20) // (4 * c)))

    def body(w_ref, g_ref, m_ref, v_ref, d_ref, mo_ref, vo_ref):
        gv = g_ref[...]
        m2 = ADAM_B1 * m_ref[...] + (1.0 - ADAM_B1) * gv
        v2 = ADAM_B2 * v_ref[...] + (1.0 - ADAM_B2) * (gv * gv)
        m_hat = m2 / (1.0 - ADAM_B1 ** ADAM_STEP)
        v_hat = v2 / (1.0 - ADAM_B2 ** ADAM_STEP)
        d_ref[...] = -ADAM_LR * (m_hat / (jnp.sqrt(v_hat) + ADAM_EPS) + ADAM_WD * w_ref[...])
        mo_ref[...] = m2
        vo_ref[...] = v2

    blk = pl.BlockSpec((tr, c), lambda i: (i, 0))
    sh = jax.ShapeDtypeStruct((r, c), F32)
    return pl.pallas_call(body, name=name, grid=(r // tr,), in_specs=[blk] * 4, out_specs=[blk] * 3,
                          out_shape=[sh, sh, sh], compiler_params=_cparams())(w, g, m, v)


def _pick8(dim, target):
    best, t = None, SUBLANES
    while t <= min(dim, target):
        if dim % t == 0:
            best = t
        t += SUBLANES
    return best or dim


def _sum_blocks(name, x, out_dtype):
    n, r, c = x.shape
    tr = _pick8(r, max(SUBLANES, (1 << 19) // (4 * c)))

    def body(x_ref, o_ref):
        acc = x_ref[0].astype(F32)
        for i in range(1, n):
            acc = acc + x_ref[i].astype(F32)
        o_ref[...] = acc.astype(out_dtype)

    return pl.pallas_call(body, name=name, grid=(r // tr,),
                          in_specs=[pl.BlockSpec((n, tr, c), lambda i: (0, i, 0))],
                          out_specs=pl.BlockSpec((tr, c), lambda i: (i, 0)),
                          out_shape=jax.ShapeDtypeStruct((r, c), out_dtype), compiler_params=_cparams())(x)


def _add_mine(name, gbuf, recv, core):
    n, _, r, c = gbuf.shape
    tr = _pick8(r, 256)

    def body(core_ref, g_ref, r_ref, o_ref):
        o_ref[...] = (g_ref[:, 0].astype(F32) + r_ref[...].astype(F32)).astype(BF16)

    grid_spec = pltpu.PrefetchScalarGridSpec(
        num_scalar_prefetch=1, grid=(r // tr,),
        in_specs=[pl.BlockSpec((n, 1, tr, c), lambda i, core_ref: (0, core_ref[0], i, 0)),
                  pl.BlockSpec((n, tr, c), lambda i, core_ref: (0, i, 0))],
        out_specs=pl.BlockSpec((n, tr, c), lambda i, core_ref: (0, i, 0)))
    return pl.pallas_call(body, name=name, grid_spec=grid_spec,
                          out_shape=jax.ShapeDtypeStruct((n, r, c), BF16), compiler_params=_cparams())(core, gbuf, recv)


def _all_gather(name, x_shard):
    m_per, n = x_shard.shape

    def body(x_ref, out_ref, send_sems, recv_sems, local_sem):
        x, y, c = lax.axis_index("x"), lax.axis_index("y"), lax.axis_index("c")
        me, sibling = (x, y, c), (x, y, 1 - c)
        chips = [(1 - x, y), (x, 1 - y), (1 - x, 1 - y)]

        def rows(px, py, pc):
            return out_ref.at[pl.ds((4 * px + 2 * py + pc) * m_per, m_per), :]

        def copy(k, block, to, src=None):
            return pltpu.make_async_remote_copy(
                src_ref=rows(*block) if src is None else src, dst_ref=rows(*block),
                send_sem=send_sems.at[k], recv_sem=recv_sems.at[k], device_id=to, device_id_type=MESH)

        mine = pltpu.make_async_copy(x_ref, rows(*me), local_sem)
        mine.start()
        first = [copy(0, me, sibling, src=x_ref)]
        first += [copy(1 + j, me, (*chip, c), src=x_ref) for j, chip in enumerate(chips)]
        for cp in first:
            cp.start()
        passed = [copy(4 + j, (*chip, c), sibling) for j, chip in enumerate(chips)]
        for j, chip in enumerate(chips):
            copy(1 + j, (*chip, c), me).wait_recv()
            passed[j].start()
        copy(0, sibling, me).wait_recv()
        for j, chip in enumerate(chips):
            copy(4 + j, (*chip, 1 - c), me).wait_recv()
        for cp in first + passed:
            cp.wait_send()
        mine.wait()

    return pl.pallas_call(
        body, name=name, out_shape=jax.ShapeDtypeStruct((N_DEV * m_per, n), x_shard.dtype),
        in_specs=[pl.BlockSpec(memory_space=pl.ANY)], out_specs=pl.BlockSpec(memory_space=pl.ANY),
        scratch_shapes=[pltpu.SemaphoreType.DMA((7,)), pltpu.SemaphoreType.DMA((7,)), pltpu.SemaphoreType.DMA],
    )(x_shard)


def _swap_halves(name, gbuf):
    n, _, r, c_ = gbuf.shape

    def body(g_ref, out_ref, send_sem, recv_sem):
        x, y, c = lax.axis_index("x"), lax.axis_index("y"), lax.axis_index("c")
        cp = pltpu.make_async_remote_copy(src_ref=g_ref.at[:, 1 - c], dst_ref=out_ref, send_sem=send_sem,
                                          recv_sem=recv_sem, device_id=(x, y, 1 - c), device_id_type=MESH)
        cp.start()
        cp.wait()

    return pl.pallas_call(
        body, name=name, out_shape=jax.ShapeDtypeStruct((n, r, c_), gbuf.dtype),
        in_specs=[pl.BlockSpec(memory_space=pl.ANY)], out_specs=pl.BlockSpec(memory_space=pl.ANY),
        scratch_shapes=[pltpu.SemaphoreType.DMA, pltpu.SemaphoreType.DMA],
    )(gbuf)


def _chip_exchange(name, part):
    n, r, c_ = part.shape

    def body(p_ref, out_ref, send_sems, recv_sems, local_sem):
        x, y, c = lax.axis_index("x"), lax.axis_index("y"), lax.axis_index("c")
        my_chip = 2 * x + y
        chips = [(1 - x, y), (x, 1 - y), (1 - x, 1 - y)]
        mine = pltpu.make_async_copy(p_ref.at[my_chip], out_ref.at[my_chip], local_sem)
        mine.start()
        copies = [pltpu.make_async_remote_copy(
            src_ref=p_ref.at[2 * cx + cy], dst_ref=out_ref.at[my_chip], send_sem=send_sems.at[j],
            recv_sem=recv_sems.at[j], device_id=(cx, cy, c), device_id_type=MESH)
            for j, (cx, cy) in enumerate(chips)]
        for cp in copies:
            cp.start()
        for cp in copies:
            cp.wait()
        mine.wait()

    return pl.pallas_call(
        body, name=name, out_shape=jax.ShapeDtypeStruct((n, r, c_), part.dtype),
        in_specs=[pl.BlockSpec(memory_space=pl.ANY)], out_specs=pl.BlockSpec(memory_space=pl.ANY),
        scratch_shapes=[pltpu.SemaphoreType.DMA((3,)), pltpu.SemaphoreType.DMA((3,)), pltpu.SemaphoreType.DMA],
    )(part)


def _share_halves(name, half):
    r, c_ = half.shape

    def body(h_ref, out_ref, send_sem, recv_sem, local_sem):
        x, y, c = lax.axis_index("x"), lax.axis_index("y"), lax.axis_index("c")
        mine = pltpu.make_async_copy(h_ref, out_ref.at[c], local_sem)
        mine.start()
        cp = pltpu.make_async_remote_copy(src_ref=h_ref, dst_ref=out_ref.at[c], send_sem=send_sem,
                                          recv_sem=recv_sem, device_id=(x, y, 1 - c), device_id_type=MESH)
        cp.start()
        cp.wait()
        mine.wait()

    return pl.pallas_call(
        body, name=name, out_shape=jax.ShapeDtypeStruct((2, r, c_), half.dtype),
        in_specs=[pl.BlockSpec(memory_space=pl.ANY)], out_specs=pl.BlockSpec(memory_space=pl.ANY),
        scratch_shapes=[pltpu.SemaphoreType.DMA, pltpu.SemaphoreType.DMA, pltpu.SemaphoreType.DMA],
    )(half)


def _pack(arrays, cols, row_multiple, dtype):
    flat = jnp.concatenate([a.reshape(-1).astype(dtype) for a in arrays])
    unit = cols * row_multiple
    total = -(-flat.shape[0] // unit) * unit
    return jnp.pad(flat, (0, total - flat.shape[0])).reshape(total // cols, cols)


def _unpack(buf, shapes):
    flat, out, off = buf.reshape(-1), [], 0
    for sh in shapes:
        n = math.prod(sh)
        out.append(flat[off:off + n].reshape(sh))
        off += n
    return out


def _discretize(lam_re, lam_im, log_dt, b_re, b_im):
    lam = lax.complex(jnp.minimum(lam_re, -EIG_CLIP), lam_im)
    dt = jnp.exp(log_dt)[:, None]
    lam_bar = jnp.exp(lam * dt)
    b_bar = ((lam_bar - 1.0) / lam)[..., None] * lax.complex(b_re, b_im)
    return jnp.real(lam_bar), jnp.imag(lam_bar), jnp.real(b_bar), jnp.imag(b_bar)


def _scan_tables(ar, ai):
    a = lax.complex(ar, ai)
    pw = [a]
    for _ in range(7):
        pw.append(pw[-1] * a)
    rows = jnp.arange(SUBLANES)[:, None]

    def build(p, reverse):
        tabs = []
        for k in (1, 2, 4):
            keep = (rows <= SUBLANES - 1 - k) if reverse else (rows >= k)
            tk = jnp.where(keep, p[k - 1][None, :], 0.0)
            tabs += [jnp.real(tk), jnp.imag(tk)]
        stack = jnp.stack(p[::-1] if reverse else p)
        tabs += [jnp.real(stack), jnp.imag(stack)]
        return jnp.stack(tabs).astype(F32)

    return build(pw, False), build([jnp.conj(p) for p in pw], True)


def _block_diag(per_group, groups_per_block):
    g, a, b = per_group.shape
    x = per_group.reshape(g // groups_per_block, groups_per_block, a, b)
    eye = jnp.eye(groups_per_block, dtype=per_group.dtype)
    out = x[:, :, :, None, :] * eye[None, :, None, :, None]
    return out.reshape(g // groups_per_block, groups_per_block * a, groups_per_block * b)


def _block_diag_extract(dense, groups_per_block, a, b):
    nkb = dense.shape[0]
    x = dense.reshape(nkb, groups_per_block, a, groups_per_block, b)
    idx = jnp.arange(groups_per_block)
    return x[:, idx, :, idx, :].transpose(1, 0, 2, 3).reshape(nkb * groups_per_block, a, b)


def _interleave(a, b, tile):
    k, f = a.shape
    return jnp.stack([a.reshape(k, f // tile, tile), b.reshape(k, f // tile, tile)], axis=2).reshape(k, 2 * f)


def _deinterleave(ab, tile):
    k, f2 = ab.shape
    x = ab.reshape(k, f2 // (2 * tile), 2, tile)
    return x[:, :, 0, :].reshape(k, f2 // 2), x[:, :, 1, :].reshape(k, f2 // 2)


def _layer_fwd(tag, x, mod, p):
    s, d = x.shape
    w_ssm, w_att = p["w_glu"].shape[0], p["w_pb"].shape[0]
    heads = p["b_f"].shape[0]
    dh = w_att // heads
    row = lambda v: v.reshape(1, -1)
    sv = {}

    h = _prenorm_fwd(f"prenorm_mix_{tag}", x, row(p["g_pre_mix"]), row(mod[1]), row(mod[0]))
    uqkv = _mm_plain(f"proj_main_{tag}", h, p["w_main"], "nn", BF16, tn=1024)
    fg = _mm_plain(f"proj_gate_{tag}", h, p["w_gates"], "nn", F32, tn=1024)
    u = uqkv[:, :w_ssm]
    f_t = fg[:, :heads].T
    g_a, g_b = fg[:, LANES:LANES + d], fg[:, LANES + d:]

    y_s5, ys, carries = _s5_fwd(f"s5_fwd_{tag}", u, p["b_blk"], p["c_blk"], p["tab_f"], row(p["d_skip"]),
                                p["w_glu"], row(p["b_glu"]))

    heads_major = lambda a: a.reshape(s, heads, dh).transpose(1, 0, 2)
    q = heads_major(uqkv[:, w_ssm:w_ssm + w_att])
    k = heads_major(uqkv[:, w_ssm + w_att:w_ssm + 2 * w_att])
    v = heads_major(uqkv[:, w_ssm + 2 * w_att:])
    cum = _cum_fwd(f"cum_fwd_{tag}", f_t, p["b_f"].reshape(heads, 1))
    t = min(ATT_BLOCK, s)
    cq, ck = cum.reshape(heads, s, 1), cum.reshape(heads, s // t, 1, t)
    o, lse = _attn_fwd(f"attn_fwd_{tag}", q, k, v, cq, ck)
    ya = o.transpose(1, 0, 2).reshape(s, w_att)

    tile = pl.BlockSpec((_pick(s, 512), _pick(d, 512)), lambda i, j, k_: (i, j))

    def merge(acc, extra_refs, out_refs):
        ya_ref, wpb_ref, ga_ref, gb_ref = extra_refs
        a_ref, b_ref, m_ref = out_refs
        bv = _dot(ya_ref[...], wpb_ref[...], NN)
        a_ref[...] = acc.astype(BF16)
        b_ref[...] = bv.astype(BF16)
        m_ref[...] = (_sigmoid(ga_ref[...]) * acc + _sigmoid(gb_ref[...]) * bv).astype(BF16)

    sd_bf = jax.ShapeDtypeStruct((s, d), BF16)
    (pa, pb, merged), _ = _mm(
        f"merge_{tag}", ys, p["w_pa"], "nn", [sd_bf] * 3, [tile] * 3, merge,
        extra=(ya, p["w_pb"], g_a, g_b),
        extra_specs=[pl.BlockSpec((_pick(s, 512), w_att), lambda i, j, k_: (i, 0)),
                     pl.BlockSpec((w_att, _pick(d, 512)), lambda i, j, k_: (0, j)), tile, tile],
        tk=w_ssm)

    x1, y_mix = _mm_postnorm(f"out_proj_{tag}", merged, p["w_o"], x, row(mod[2]), row(p["g_post_mix"]))

    h2 = _prenorm_fwd(f"prenorm_ffn_{tag}", x1, row(p["g_pre_ffn"]), row(mod[4]), row(mod[3]))
    f_dim = p["w_down"].shape[0]
    tm = _pick(s, 512)

    def swiglu(acc, extra_refs, out_refs):
        ab_ref, hid_ref = out_refs
        av, bv = acc[:, :FFN_TILE], acc[:, FFN_TILE:]
        ab_ref[...] = acc.astype(BF16)
        hid_ref[...] = (av * _sigmoid(av) * bv).astype(BF16)

    (ab, hidden), _ = _mm(
        f"ffn_up_{tag}", h2, p["w_gu"], "nn",
        [jax.ShapeDtypeStruct((s, 2 * f_dim), BF16), jax.ShapeDtypeStruct((s, f_dim), BF16)],
        [pl.BlockSpec((tm, 2 * FFN_TILE), lambda i, j, k_: (i, j)), pl.BlockSpec((tm, FFN_TILE), lambda i, j, k_: (i, j))],
        swiglu, tm=tm, tn=2 * FFN_TILE, tk=1024)
    x2, y_ffn = _mm_postnorm(f"ffn_down_{tag}", hidden, p["w_down"], x1, row(mod[5]), row(p["g_post_ffn"]))

    sv.update(x=x, h=h, uqkv=uqkv, f_t=f_t, g_a=g_a, g_b=g_b, y_s5=y_s5, ys=ys, carries=carries, q=q, k=k, v=v,
              cq=cq, ck=ck, o=o, lse=lse, ya=ya, pa=pa, pb=pb, merged=merged, x1=x1, y_mix=y_mix, h2=h2, ab=ab,
              hidden=hidden, y_ffn=y_ffn)
    return x2, sv


def _mm_postnorm(name, a, w, x, gate, g):
    s, d = x.shape
    tm = _pick(s, 256)
    rowspec = pl.BlockSpec((tm, d), lambda i, j, k: (i, 0))
    vec = pl.BlockSpec((1, d), lambda i, j, k: (0, 0))

    def epilogue(acc, extra_refs, out_refs):
        x_ref, gate_ref, g_ref = extra_refs
        r = lax.rsqrt(jnp.mean(acc * acc, axis=-1, keepdims=True) + RMS_EPS)
        out_refs[0][...] = x_ref[...] + gate_ref[...] * (acc * r * g_ref[...])
        out_refs[1][...] = acc

    sd = jax.ShapeDtypeStruct((s, d), F32)
    (xn, y), _ = _mm(name, a, w, "nn", [sd, sd], [rowspec, rowspec], epilogue, extra=(x, gate, g),
                     extra_specs=[rowspec, vec, vec], tm=tm, tn=d, tk=1536)
    return xn, y


def _layer_bwd(tag, dx2, mod, p, sv):
    s, d = dx2.shape
    w_ssm, w_att = p["w_glu"].shape[0], p["w_pb"].shape[0]
    heads = p["b_f"].shape[0]
    dh = w_att // heads
    row = lambda v: v.reshape(1, -1)
    gr = {}

    dy_ffn, sums = _postnorm_bwd(f"postnorm_bwd_ffn_{tag}", dx2, sv["y_ffn"], row(p["g_post_ffn"]), row(mod[5]))
    d_gate_f, gr["g_post_ffn"] = sums[0], sums[1]
    gr["w_down"] = _mm_plain(f"dw_down_{tag}", sv["hidden"], dy_ffn, "tn", BF16, tk=1024)
    tm = _pick(s, 512)

    def swiglu_bwd(acc, extra_refs, out_refs):
        abv = extra_refs[0][...].astype(F32)
        av, bv = abv[:, :FFN_TILE], abv[:, FFN_TILE:]
        sg = _sigmoid(av)
        da = acc * bv * (sg * (1.0 + av * (1.0 - sg)))
        db = acc * (av * sg)
        out_refs[0][:, :FFN_TILE] = da.astype(BF16)
        out_refs[0][:, FFN_TILE:] = db.astype(BF16)

    ab_spec = pl.BlockSpec((tm, 2 * FFN_TILE), lambda i, j, k_: (i, j))
    (dab,), _ = _mm(f"ffn_down_bwd_{tag}", dy_ffn, p["w_down"], "nt",
                    [jax.ShapeDtypeStruct(sv["ab"].shape, BF16)], [ab_spec], swiglu_bwd,
                    extra=(sv["ab"],), extra_specs=[ab_spec], tm=tm, tn=FFN_TILE, tk=1024)
    gr["w_gu"] = _mm_plain(f"dw_gu_{tag}", sv["h2"], dab, "tn", BF16, tk=1024)
    dh2 = _mm_plain(f"dh_ffn_{tag}", dab, p["w_gu"], "nt", F32, tn=1024, tk=1024)
    dx1, sums = _prenorm_bwd(f"prenorm_bwd_ffn_{tag}", dh2, sv["x1"], row(p["g_pre_ffn"]), row(mod[4]), dx2)
    d_scale_f, d_shift_f, gr["g_pre_ffn"] = sums[0], sums[1], sums[2]

    dy_mix, sums = _postnorm_bwd(f"postnorm_bwd_mix_{tag}", dx1, sv["y_mix"], row(p["g_post_mix"]), row(mod[2]))
    d_gate_m, gr["g_post_mix"] = sums[0], sums[1]
    gr["w_o"] = _mm_plain(f"dw_o_{tag}", sv["merged"], dy_mix, "tn", BF16, tk=1024)

    tile = pl.BlockSpec((_pick(s, 512), _pick(d, 512)), lambda i, j, k_: (i, j))

    def merge_bwd(acc, extra_refs, out_refs):
        a_ref, b_ref, ga_ref, gb_ref = extra_refs
        sa, sb = _sigmoid(ga_ref[...]), _sigmoid(gb_ref[...])
        out_refs[0][...] = (acc * sa).astype(BF16)
        out_refs[1][...] = (acc * sb).astype(BF16)
        out_refs[2][...] = (acc * a_ref[...].astype(F32) * sa * (1.0 - sa)).astype(BF16)
        out_refs[3][...] = (acc * b_ref[...].astype(F32) * sb * (1.0 - sb)).astype(BF16)

    sd_bf = jax.ShapeDtypeStruct((s, d), BF16)
    (d_pa, d_pb, d_ga, d_gb), _ = _mm(f"out_proj_bwd_{tag}", dy_mix, p["w_o"], "nt", [sd_bf] * 4, [tile] * 4, merge_bwd,
                                      extra=(sv["pa"], sv["pb"], sv["g_a"], sv["g_b"]), extra_specs=[tile] * 4, tk=1024)
    gr["w_pa"] = _mm_plain(f"dw_pa_{tag}", sv["ys"], d_pa, "tn", BF16, tk=1024)
    gr["w_pb"] = _mm_plain(f"dw_pb_{tag}", sv["ya"], d_pb, "tn", BF16, tk=1024)
    d_ys = _mm_plain(f"d_ys_{tag}", d_pa, p["w_pa"], "nt", BF16, tk=1024)
    d_ya = _mm_plain(f"d_ya_{tag}", d_pb, p["w_pb"], "nt", BF16, tk=1024)

    do = d_ya.reshape(s, heads, dh).transpose(1, 0, 2)
    dq, dk, dv, dcq, dck = _attn_bwd(f"attn_bwd_{tag}", sv["q"], sv["k"], sv["v"], sv["o"], do, sv["lse"], sv["cq"], sv["ck"])
    d_f_t, d_bf = _cum_bwd(f"cum_bwd_{tag}", dcq.reshape(heads, s), dck.reshape(heads, s), sv["f_t"],
                           p["b_f"].reshape(heads, 1))
    gr["b_f"] = d_bf[:, 0]
    tokens_major = lambda a: a.transpose(1, 0, 2).reshape(s, w_att).astype(BF16)

    u = sv["uqkv"][:, :w_ssm]
    du, d_bblk, d_cblk, d_abar, gr["w_glu"], vec = _s5_bwd(
        f"s5_bwd_{tag}", u, d_ys, sv["y_s5"], sv["carries"], p["b_blk"], p["c_blk"], p["tab_f"], p["tab_r"],
        row(p["d_skip"]), p["w_glu"], row(p["b_glu"]))
    gr["b_glu"], gr["d_skip"] = vec[0], vec[1]
    gr["b_blk"], gr["c_blk"], gr["a_bar"] = d_bblk, d_cblk, d_abar

    d_main = jnp.concatenate([du, tokens_major(dq), tokens_major(dk), tokens_major(dv)], axis=1)
    d_f = jnp.pad(d_f_t.T, ((0, 0), (0, LANES - heads))).astype(BF16)
    d_gates = jnp.concatenate([d_f, d_ga, d_gb], axis=1)
    gr["w_main"] = _mm_plain(f"dw_main_{tag}", sv["h"], d_main, "tn", BF16, tk=1024)
    gr["w_gates"] = _mm_plain(f"dw_gates_{tag}", sv["h"], d_gates, "tn", BF16, tk=1024)
    dh_a = _mm_plain(f"dh_main_{tag}", d_main, p["w_main"], "nt", F32, tn=1024, tk=1024)
    dh1 = _mm_plain(f"dh_gates_{tag}", d_gates, p["w_gates"], "nt", F32, add=dh_a, tn=1024, tk=1024)
    dx0, sums = _prenorm_bwd(f"prenorm_bwd_mix_{tag}", dh1, sv["x"], row(p["g_pre_mix"]), row(mod[1]), dx1)
    d_scale_m, d_shift_m, gr["g_pre_mix"] = sums[0], sums[1], sums[2]

    d_mod = jnp.stack([d_shift_m, d_scale_m, d_gate_m, d_shift_f, d_scale_f, d_gate_f])
    return dx0, d_mod, gr


BIG = ("w_in", "w_glu", "w_pa", "w_pb", "w_o", "w_ffn_gate", "w_ffn_up", "w_ffn_down")
BIG_SHARD_AXIS = {"w_in": 2, "w_glu": 1, "w_pa": 2, "w_pb": 2, "w_o": 1, "w_ffn_gate": 2, "w_ffn_up": 2, "w_ffn_down": 1}
SMALL = ("b_ada", "g_pre_mix", "g_post_mix", "g_pre_ffn", "g_post_ffn", "lam_re", "lam_im", "log_dt", "b_re", "b_im",
         "c_re", "c_im", "d_skip", "b_glu", "b_f")
WEIGHTS = ("w_ada", "b_ada", "g_pre_mix", "g_post_mix", "g_pre_ffn", "g_post_ffn", "w_in", "lam_re", "lam_im", "log_dt",
           "b_re", "b_im", "c_re", "c_im", "d_skip", "w_glu", "b_glu", "b_f", "w_pa", "w_pb", "w_o", "w_ffn_gate",
           "w_ffn_up", "w_ffn_down")


def _prepare_layer(full, small, l):
    w_in = full["w_in"][l]
    d = w_in.shape[0]
    heads = small["b_f"].shape[1]
    n_groups, n_state, group_ch = small["b_re"].shape[1:]
    w_ssm = n_groups * group_ch
    w_att = full["w_pb"].shape[1]
    n_main = w_ssm + 3 * w_att
    gpb = LANES // group_ch
    p = {}
    p["w_main"] = w_in[:, :n_main]
    p["w_gates"] = jnp.concatenate(
        [w_in[:, n_main:n_main + heads], jnp.zeros((d, LANES - heads), BF16), w_in[:, n_main + heads:]], axis=1)
    p["w_glu"], p["w_pa"], p["w_pb"], p["w_o"] = (full[n][l] for n in ("w_glu", "w_pa", "w_pb", "w_o"))
    p["w_gu"] = _interleave(full["w_ffn_gate"][l], full["w_ffn_up"][l], FFN_TILE)
    p["w_down"] = full["w_ffn_down"][l]
    for n in ("g_pre_mix", "g_post_mix", "g_pre_ffn", "g_post_ffn", "d_skip", "b_glu", "b_f"):
        p[n] = small[n][l]
    ar, ai, br, bi = _discretize(small["lam_re"][l], small["lam_im"][l], small["log_dt"][l], small["b_re"][l], small["b_im"][l])
    p["tab_f"], p["tab_r"] = _scan_tables(ar.reshape(-1), ai.reshape(-1))
    bre = _block_diag(br.transpose(0, 2, 1), gpb)
    bim = _block_diag(bi.transpose(0, 2, 1), gpb)
    p["b_blk"] = jnp.concatenate([bre, bim], axis=2).astype(BF16)
    cre = _block_diag(small["c_re"][l].transpose(0, 2, 1), gpb)
    cim = _block_diag(small["c_im"][l].transpose(0, 2, 1), gpb)
    p["c_blk"] = jnp.concatenate([cre, -cim], axis=1).astype(BF16)
    return p


def _small_grads_from_partials(gr, small, l):
    n_groups, n_state, group_ch = small["b_re"].shape[1:]
    gpb = LANES // group_ch
    half = gpb * n_state
    ns2 = n_groups * n_state
    d_abar = jnp.sum(gr["a_bar"], axis=0)
    dar, dai = d_abar[:ns2].reshape(n_groups, n_state), d_abar[ns2:].reshape(n_groups, n_state)
    dbr = _block_diag_extract(gr["b_blk"][:, :, :half], gpb, group_ch, n_state).transpose(0, 2, 1)
    dbi = _block_diag_extract(gr["b_blk"][:, :, half:], gpb, group_ch, n_state).transpose(0, 2, 1)
    args = (small["lam_re"][l], small["lam_im"][l], small["log_dt"][l], small["b_re"][l], small["b_im"][l])
    _, vjp = jax.vjp(_discretize, *args)
    d_lam_re, d_lam_im, d_log_dt, d_b_re, d_b_im = vjp((dar, dai, dbr, dbi))
    d_c_re = _block_diag_extract(gr["c_blk"][:, :half, :], gpb, n_state, group_ch).transpose(0, 2, 1)
    d_c_im = -_block_diag_extract(gr["c_blk"][:, half:, :], gpb, n_state, group_ch).transpose(0, 2, 1)
    return dict(lam_re=d_lam_re, lam_im=d_lam_im, log_dt=d_log_dt, b_re=d_b_re, b_im=d_b_im, c_re=d_c_re, c_im=d_c_im)


def _shard_of(a, axis, j):
    n = a.shape[axis] // N_CHIPS
    return lax.slice_in_dim(a, j * n, (j + 1) * n, axis=axis)


def _fwd_bwd(xs, target, mods, layers, heads):
    depth = len(layers)
    saved = []
    act = xs
    for l in range(depth):
        act, sv = _layer_fwd(str(l), act, mods[l], layers[l])
        saved.append(sv)
    dx, loss_blk = _loss_grad("loss", act, target)
    grads, d_mods = [None] * depth, [None] * depth
    for l in reversed(range(depth)):
        dx, d_mods[l], grads[l] = _layer_bwd(str(l), dx, mods[l], layers[l], saved[l])
    full_grads = {n: [] for n in BIG}
    for l in range(depth):
        g = grads[l]
        full_grads["w_in"].append(jnp.concatenate([g["w_main"], g["w_gates"][:, :heads], g["w_gates"][:, LANES:]], axis=1))
        dg, du_ = _deinterleave(g["w_gu"], FFN_TILE)
        full_grads["w_ffn_gate"].append(dg)
        full_grads["w_ffn_up"].append(du_)
        full_grads["w_ffn_down"].append(g["w_down"])
        full_grads["w_glu"].append(g["w_glu"].astype(BF16))
        for n in ("w_pa", "w_pb", "w_o"):
            full_grads[n].append(g[n])
    stacked = {n: jnp.stack(full_grads[n]) for n in BIG}
    return loss_blk, dx, d_mods, grads, stacked


def kernel(x, c, w_ada, b_ada, g_pre_mix, g_post_mix, g_pre_ffn, g_post_ffn, w_in, lam_re, lam_im, log_dt, b_re, b_im, c_re, c_im, d_skip, w_glu, b_glu, b_f, w_pa, w_pb, w_o, w_ffn_gate, w_ffn_up, w_ffn_down, loss_target, m_w_ada, m_b_ada, m_g_pre_mix, m_g_post_mix, m_g_pre_ffn, m_g_post_ffn, m_w_in, m_lam_re, m_lam_im, m_log_dt, m_b_re, m_b_im, m_c_re, m_c_im, m_d_skip, m_w_glu, m_b_glu, m_b_f, m_w_pa, m_w_pb, m_w_o, m_w_ffn_gate, m_w_ffn_up, m_w_ffn_down, v_w_ada, v_b_ada, v_g_pre_mix, v_g_post_mix, v_g_pre_ffn, v_g_post_ffn, v_w_in, v_lam_re, v_lam_im, v_log_dt, v_b_re, v_b_im, v_c_re, v_c_im, v_d_skip, v_w_glu, v_b_glu, v_b_f, v_w_pa, v_w_pb, v_w_o, v_w_ffn_gate, v_w_ffn_up, v_w_ffn_down):
    local = dict(locals())
    weights = {n: local[n] for n in WEIGHTS}
    moments_m = {n: local["m_" + n] for n in WEIGHTS}
    moments_v = {n: local["v_" + n] for n in WEIGHTS}
    depth, d = g_pre_mix.shape
    n_mod = w_ada.shape[2] * N_CHIPS // d
    mx, my, mc = lax.axis_index("x"), lax.axis_index("y"), lax.axis_index("c")
    my_chip = 2 * mx + my
    my_dev = 4 * mx + 2 * my + mc
    xs = x[0]

    shard_shapes = [weights[n].shape for n in BIG]
    wflat = _pack([weights[n] for n in BIG], COMM_LANES, 32, BF16)
    rh = wflat.shape[0] // 2
    mine = lax.dynamic_slice_in_dim(wflat, mc * rh, rh, axis=0)
    gathered = _all_gather("gather_weights", mine).reshape(N_CHIPS, 2 * rh, COMM_LANES)
    per_chip = [_unpack(gathered[j], shard_shapes) for j in range(N_CHIPS)]
    full = {n: jnp.concatenate([per_chip[j][i] for j in range(N_CHIPS)], axis=BIG_SHARD_AXIS[n]) for i, n in enumerate(BIG)}
    small = {n: weights[n] for n in SMALL}
    layers = [_prepare_layer(full, small, l) for l in range(depth)]

    c_pad = jnp.pad(c, ((0, SUBLANES - 1), (0, 0)))
    c_all = _all_gather("gather_cond", c_pad).reshape(N_DEV, SUBLANES, d)[:, 0, :]
    silu = lambda v: v * _sigmoid(v)
    n_cols = w_ada.shape[2]
    mod_shard = []
    for l in range(depth):
        bias = lax.dynamic_slice_in_dim(b_ada[l], my_chip * n_cols, n_cols)
        mod_shard.append(_mm_plain(f"ada_{l}", c_all, w_ada[l], "nn", F32, add=jnp.broadcast_to(bias, (N_DEV, n_cols)),
                                   a_fn=silu, tm=N_DEV, tn=512, tk=1024))
    mod_block = jnp.concatenate(mod_shard, axis=1)
    mod_all = _all_gather("gather_mod", mod_block).reshape(N_DEV, N_DEV, depth, n_cols)
    mod_rows = lax.dynamic_index_in_dim(mod_all[0::2], my_dev, axis=1, keepdims=False)
    mods = [mod_rows[:, l, :].reshape(n_mod, d) for l in range(depth)]

    loss_blk, dx, d_mods, grads, stacked = _fwd_bwd(xs, loss_target[0], mods, layers, b_f.shape[1])
    loss = lax.psum(loss_blk[0, 0], ("x", "y", "c"))
    grad_x = dx[None]

    gbuf = jnp.stack([_pack([_shard_of(stacked[n], BIG_SHARD_AXIS[n], j) for n in BIG], COMM_LANES, 32, BF16)
                      for j in range(N_CHIPS)]).reshape(N_CHIPS, 2, rh, COMM_LANES)
    from_sibling = _swap_halves("grads_swap_cores", gbuf)
    chip_part = _add_mine("grads_add_cores", gbuf, from_sibling, mc.astype(jnp.int32).reshape(1))
    from_chips = _chip_exchange("grads_exchange_chips", chip_part)
    my_half = _sum_blocks("grads_sum_chips", from_chips, F32)
    shard_flat = _share_halves("grads_share_cores", my_half).reshape(2 * rh, COMM_LANES)
    big_grads = dict(zip(BIG, _unpack(shard_flat, shard_shapes)))

    partial_names = ("g_pre_mix", "g_post_mix", "g_pre_ffn", "g_post_ffn", "d_skip", "b_glu", "b_f", "a_bar", "b_blk", "c_blk")
    contrib = []
    for l in range(depth):
        contrib.append(d_mods[l])
        contrib += [grads[l][n] for n in partial_names]
    contrib_shapes = [a.shape for a in contrib]
    block = _pack(contrib, LANES, SUBLANES, F32)
    rows = block.shape[0]
    all_blocks = _all_gather("gather_small_grads", block).reshape(N_DEV, rows, LANES)
    summed = _unpack(_sum_blocks("sum_small_grads", all_blocks, F32), contrib_shapes)
    per_layer = len(partial_names) + 1
    small_grads = {n: [] for n in SMALL}
    d_mod_all = []
    for l in range(depth):
        part = summed[l * per_layer:(l + 1) * per_layer]
        small_grads["b_ada"].append(part[0].reshape(-1))
        gl = dict(zip(partial_names, part[1:]))
        for n in ("g_pre_mix", "g_post_mix", "g_pre_ffn", "g_post_ffn", "d_skip", "b_glu", "b_f"):
            small_grads[n].append(gl[n])
        for n, gval in _small_grads_from_partials(gl, small, l).items():
            small_grads[n].append(gval)
        each = [_unpack(all_blocks[dev], contrib_shapes)[l * per_layer] for dev in range(N_DEV)]
        d_mod_all.append(jnp.stack(each).reshape(N_DEV, n_mod * d))
    small_grads = {n: jnp.stack(v) for n, v in small_grads.items()}

    g_w_ada = []
    for l in range(depth):
        cols = lax.dynamic_slice_in_dim(d_mod_all[l], my_chip * n_cols, n_cols, axis=1)
        g_w_ada.append(_mm_plain(f"dw_ada_{l}", c_all, cols, "tn", F32, a_fn=silu, tm=512, tn=512, tk=N_DEV))
    all_grads = dict(big_grads)
    all_grads.update(small_grads)
    all_grads["w_ada"] = jnp.stack(g_w_ada)

    delta, new_m, new_v = {}, {}, {}
    for n in ("w_ada",) + BIG:
        shape = weights[n].shape
        two_d = lambda a: a.reshape(-1, shape[-1])
        dl, nm, nv = _adamw(f"adamw_{n}", two_d(weights[n]), two_d(all_grads[n]), two_d(moments_m[n]), two_d(moments_v[n]))
        delta[n], new_m[n], new_v[n] = dl.reshape(shape), nm.reshape(shape), nv.reshape(shape)
    small_shapes = [weights[n].shape for n in SMALL]
    packed = [_pack([src[n] for n in SMALL], LANES, SUBLANES, F32) for src in (weights, all_grads, moments_m, moments_v)]
    outs = _adamw("adamw_small", *packed)
    for dst, buf in zip((delta, new_m, new_v), outs):
        dst.update(dict(zip(SMALL, _unpack(buf, small_shapes))))

    return (loss, grad_x, *[all_grads[n] for n in WEIGHTS], *[delta[n] for n in WEIGHTS],
            *[new_m[n] for n in WEIGHTS], *[new_v[n] for n in WEIGHTS])
```

```python
import functools
import math

import jax
import jax.numpy as jnp
from jax import lax
from jax.experimental import pallas as pl
from jax.experimental.pallas import tpu as pltpu

F32 = jnp.float32
BF16 = jnp.bfloat16
MESH = pl.DeviceIdType.MESH

RMS_EPS = 1e-6
EIG_CLIP = 1e-4
ADAM_LR, ADAM_B1, ADAM_B2, ADAM_EPS, ADAM_WD, ADAM_STEP = 0.001, 0.9, 0.999, 1e-08, 0.01, 10

LANES = 128
SUBLANES = 8
VMEM_LIMIT = 56 * 1024 * 1024
S5_ROWS = 256
S5_CHUNK = 256
ATT_BLOCK = 256
FFN_TILE = 256
F_PAD = 256
COMM_LANES = 1024
N_CHIPS = 4
N_DEV = 8

NN = (((1,), (0,)), ((), ()))
NT = (((1,), (1,)), ((), ()))
TN = (((0,), (0,)), ((), ()))
_DN = {"nn": NN, "nt": NT, "tn": TN}


def _cparams(**kw):
    return pltpu.CompilerParams(vmem_limit_bytes=VMEM_LIMIT, **kw)


def _pick(dim, target):
    best, t = None, LANES
    while t <= min(dim, target):
        if dim % t == 0:
            best = t
        t += LANES
    return best or dim


def _sigmoid(x):
    return 1.0 / (1.0 + jnp.exp(-x))


def _dot(a, b, dn):
    return lax.dot_general(a, b, dn, preferred_element_type=F32)


def _mm(name, a, b, mode, out_shapes, out_specs, epilogue, extra=(), extra_specs=(),
        tm=512, tn=512, tk=512, a_fn=None):
    if mode == "nn":
        (m, kd), (_, n) = a.shape, b.shape
    elif mode == "nt":
        (m, kd), (n, _) = a.shape, b.shape
    else:
        (kd, m), (_, n) = a.shape, b.shape
    tm, tn, tk = _pick(m, tm), _pick(n, tn), _pick(kd, tk)
    nk = kd // tk
    n_extra, n_out = len(extra), len(out_shapes)

    def body(*refs):
        a_ref, b_ref = refs[0], refs[1]
        extra_refs = refs[2:2 + n_extra]
        out_refs = refs[2 + n_extra:2 + n_extra + n_out]
        acc = refs[-1]
        k = pl.program_id(2)

        @pl.when(k == 0)
        def _():
            acc[...] = jnp.zeros_like(acc)

        av = a_ref[...]
        if a_fn is not None:
            av = a_fn(av.astype(F32))
        acc[...] += _dot(av.astype(BF16), b_ref[...].astype(BF16), _DN[mode])

        @pl.when(k == nk - 1)
        def _():
            epilogue(acc[...], extra_refs, out_refs)

    if mode == "tn":
        a_spec = pl.BlockSpec((tk, tm), lambda i, j, k: (k, i))
    else:
        a_spec = pl.BlockSpec((tm, tk), lambda i, j, k: (i, k))
    if mode == "nt":
        b_spec = pl.BlockSpec((tn, tk), lambda i, j, k: (j, k))
    else:
        b_spec = pl.BlockSpec((tk, tn), lambda i, j, k: (k, j))
    res = pl.pallas_call(
        body, name=name, grid=(m // tm, n // tn, nk),
        in_specs=[a_spec, b_spec, *extra_specs],
        out_specs=list(out_specs), out_shape=list(out_shapes),
        scratch_shapes=[pltpu.VMEM((tm, tn), F32)],
        compiler_params=_cparams(),
    )(a, b, *extra)
    return res, (tm, tn, tk)


def _mm_plain(name, a, b, mode, out_dtype, add=None, a_fn=None, tm=512, tn=512, tk=512):
    if mode == "nn":
        m, n = a.shape[0], b.shape[1]
    elif mode == "nt":
        m, n = a.shape[0], b.shape[0]
    else:
        m, n = a.shape[1], b.shape[1]
    tm_, tn_ = _pick(m, tm), _pick(n, tn)
    spec = pl.BlockSpec((tm_, tn_), lambda i, j, k: (i, j))

    def epilogue(acc, extra_refs, out_refs):
        if add is not None:
            acc = acc + extra_refs[0][...]
        out_refs[0][...] = acc.astype(out_dtype)

    extra = () if add is None else (add,)
    (out,), _ = _mm(name, a, b, mode, [jax.ShapeDtypeStruct((m, n), out_dtype)], [spec], epilogue,
                    extra=extra, extra_specs=[spec] * len(extra), tm=tm, tn=tn, tk=tk, a_fn=a_fn)
    return out


def _row_tile(s, d):
    return _pick(s, max(SUBLANES, (1 << 20) // (4 * d)))


def _prenorm_fwd(name, x, g, scale, shift):
    s, d = x.shape
    tr = _row_tile(s, d)

    def body(x_ref, g_ref, sc_ref, sh_ref, h_ref):
        xv = x_ref[...]
        r = lax.rsqrt(jnp.mean(xv * xv, axis=-1, keepdims=True) + RMS_EPS)
        h_ref[...] = ((xv * r * g_ref[...]) * (1.0 + sc_ref[...]) + sh_ref[...]).astype(BF16)

    row = pl.BlockSpec((tr, d), lambda i: (i, 0))
    vec = pl.BlockSpec((1, d), lambda i: (0, 0))
    return pl.pallas_call(body, name=name, grid=(s // tr,), in_specs=[row, vec, vec, vec], out_specs=row,
                          out_shape=jax.ShapeDtypeStruct((s, d), BF16), compiler_params=_cparams())(x, g, scale, shift)


def _prenorm_bwd(name, dh, x, g, scale, dx_res):
    s, d = x.shape
    tr = _row_tile(s, d)

    def body(dh_ref, x_ref, g_ref, sc_ref, dxr_ref, dx_ref, sums_ref):
        @pl.when(pl.program_id(0) == 0)
        def _():
            sums_ref[...] = jnp.zeros_like(sums_ref)

        xv, dhv, gv = x_ref[...], dh_ref[...].astype(F32), g_ref[...]
        r = lax.rsqrt(jnp.mean(xv * xv, axis=-1, keepdims=True) + RMS_EPS)
        xhat = xv * r
        dxn = dhv * (1.0 + sc_ref[...])
        dxhat = dxn * gv
        dx = r * (dxhat - xhat * jnp.mean(dxhat * xhat, axis=-1, keepdims=True))
        dx_ref[...] = dxr_ref[...] + dx
        sums_ref[0:1, :] += jnp.sum(dhv * (xhat * gv), axis=0, keepdims=True)
        sums_ref[1:2, :] += jnp.sum(dhv, axis=0, keepdims=True)
        sums_ref[2:3, :] += jnp.sum(dxn * xhat, axis=0, keepdims=True)

    row = pl.BlockSpec((tr, d), lambda i: (i, 0))
    vec = pl.BlockSpec((1, d), lambda i: (0, 0))
    acc = pl.BlockSpec((SUBLANES, d), lambda i: (0, 0))
    return pl.pallas_call(
        body, name=name, grid=(s // tr,), in_specs=[row, row, vec, vec, row], out_specs=[row, acc],
        out_shape=[jax.ShapeDtypeStruct((s, d), F32), jax.ShapeDtypeStruct((SUBLANES, d), F32)],
        compiler_params=_cparams())(dh, x, g, scale, dx_res)


def _postnorm_bwd(name, dxn, y, g, gate):
    s, d = y.shape
    tr = _row_tile(s, d)

    def body(dx_ref, y_ref, g_ref, gt_ref, dy_ref, sums_ref):
        @pl.when(pl.program_id(0) == 0)
        def _():
            sums_ref[...] = jnp.zeros_like(sums_ref)

        yv, dxv, gv = y_ref[...], dx_ref[...], g_ref[...]
        r = lax.rsqrt(jnp.mean(yv * yv, axis=-1, keepdims=True) + RMS_EPS)
        yhat = yv * r
        dn = dxv * gt_ref[...]
        dyhat = dn * gv
        dy_ref[...] = (r * (dyhat - yhat * jnp.mean(dyhat * yhat, axis=-1, keepdims=True))).astype(BF16)
        sums_ref[0:1, :] += jnp.sum(dxv * (yhat * gv), axis=0, keepdims=True)
        sums_ref[1:2, :] += jnp.sum(dn * yhat, axis=0, keepdims=True)

    row = pl.BlockSpec((tr, d), lambda i: (i, 0))
    vec = pl.BlockSpec((1, d), lambda i: (0, 0))
    acc = pl.BlockSpec((SUBLANES, d), lambda i: (0, 0))
    return pl.pallas_call(
        body, name=name, grid=(s // tr,), in_specs=[row, row, vec, vec], out_specs=[row, acc],
        out_shape=[jax.ShapeDtypeStruct((s, d), BF16), jax.ShapeDtypeStruct((SUBLANES, d), F32)],
        compiler_params=_cparams())(dxn, y, g, gate)


def _loss_grad(name, y, target):
    s, d = y.shape
    tr = _row_tile(s, d)

    def body(y_ref, t_ref, dy_ref, loss_ref):
        @pl.when(pl.program_id(0) == 0)
        def _():
            loss_ref[...] = jnp.zeros_like(loss_ref)

        err = y_ref[...] - t_ref[...]
        dy_ref[...] = err * (1.0 / d)
        part = jnp.sum(jnp.sum(err * err, axis=-1, keepdims=True), axis=0, keepdims=True) * (0.5 / d)
        loss_ref[...] += jnp.broadcast_to(part, loss_ref.shape)

    row = pl.BlockSpec((tr, d), lambda i: (i, 0))
    acc = pl.BlockSpec((SUBLANES, LANES), lambda i: (0, 0))
    return pl.pallas_call(
        body, name=name, grid=(s // tr,), in_specs=[row, row], out_specs=[row, acc],
        out_shape=[jax.ShapeDtypeStruct((s, d), F32), jax.ShapeDtypeStruct((SUBLANES, LANES), F32)],
        compiler_params=_cparams())(y, target)


def _gelu(y):
    c = math.sqrt(2.0 / math.pi)
    return 0.5 * y * (1.0 + jnp.tanh(c * (y + 0.044715 * (y * y * y))))


def _gelu_grad(y):
    c = math.sqrt(2.0 / math.pi)
    th = jnp.tanh(c * (y + 0.044715 * (y * y * y)))
    return 0.5 * (1.0 + th) + 0.5 * y * (1.0 - th * th) * c * (1.0 + 3.0 * 0.044715 * (y * y))


def _scan_rows(x_ref, row0, n_groups, ns2, tab_ref, carry_ref, reverse, after_group=None, extra_init=None):
    wc = min(S5_CHUNK, ns2)
    shifts = (1, 2, 4)
    for c0 in range(0, ns2, wc):
        re = slice(c0, c0 + wc)
        im = slice(ns2 + c0, ns2 + c0 + wc)
        tabs = [tab_ref[k, :, re] for k in range(8)]

        def group(i, carry, re=re, im=im, tabs=tabs, c0=c0):
            cr, ci, extra = carry
            g = (n_groups - 1 - i) if reverse else i
            r0 = pl.multiple_of(row0 + g * SUBLANES, SUBLANES)
            br = x_ref[pl.ds(r0, SUBLANES), re]
            bi = x_ref[pl.ds(r0, SUBLANES), im]
            for lvl, k in enumerate(shifts):
                mr, mi = tabs[2 * lvl], tabs[2 * lvl + 1]
                sh = (SUBLANES - k) if reverse else k
                sr = pltpu.roll(br, sh, 0)
                si = pltpu.roll(bi, sh, 0)
                br, bi = br + mr * sr - mi * si, bi + mr * si + mi * sr
            apr, api = tabs[6], tabs[7]
            xr = br + apr * cr - api * ci
            xi = bi + apr * ci + api * cr
            x_ref[pl.ds(r0, SUBLANES), re] = xr
            x_ref[pl.ds(r0, SUBLANES), im] = xi
            if after_group is not None:
                extra = after_group(c0, r0, xr, xi, extra)
            if reverse:
                return xr[0:1, :], xi[0:1, :], extra
            return xr[SUBLANES - 1:SUBLANES, :], xi[SUBLANES - 1:SUBLANES, :], extra

        init_extra = extra_init(wc) if extra_init is not None else 0
        cr, ci, extra = lax.fori_loop(0, n_groups, group, (carry_ref[0:1, re], carry_ref[0:1, im], init_extra))
        carry_ref[0:1, re] = cr
        carry_ref[0:1, im] = ci
        if after_group is not None:
            after_group(c0, None, None, None, extra)


def _s5_fwd(name, u, b_blk, c_blk, tab_f, dskip, w_glu, b_glu):
    s, w = u.shape
    nkb = w // LANES
    ns2 = b_blk.shape[2] // 2 * nkb
    half = ns2 // nkb
    t = min(S5_ROWS, s)
    nblk = s // t

    def body(u_ref, b_ref, c_ref, tab_ref, ds_ref, wg_ref, bg_ref, y_ref, ys_ref, cs_ref, xs, carry):
        @pl.when(pl.program_id(0) == 0)
        def _():
            carry[...] = jnp.zeros_like(carry)

        cs_ref[0] = carry[...]
        for kb in range(nkb):
            bu = _dot(u_ref[:, kb * LANES:(kb + 1) * LANES], b_ref[kb], NN)
            xs[:, kb * half:(kb + 1) * half] = bu[:, :half]
            xs[:, ns2 + kb * half:ns2 + (kb + 1) * half] = bu[:, half:]
        _scan_rows(xs, 0, t // SUBLANES, ns2, tab_ref, carry, reverse=False)
        for kb in range(nkb):
            cols = slice(kb * LANES, (kb + 1) * LANES)
            yk = _dot(xs[:, kb * half:(kb + 1) * half].astype(BF16), c_ref[kb, :half, :], NN)
            yk += _dot(xs[:, ns2 + kb * half:ns2 + (kb + 1) * half].astype(BF16), c_ref[kb, half:, :], NN)
            y_ref[:, cols] = yk + ds_ref[:, cols] * u_ref[:, cols].astype(F32)
        z = _gelu(y_ref[...])
        gate = _sigmoid(_dot(z.astype(BF16), wg_ref[...], NN) + bg_ref[...])
        ys_ref[...] = (z * gate).astype(BF16)

    row = pl.BlockSpec((t, w), lambda i: (i, 0))
    full = lambda shape: pl.BlockSpec(shape, lambda i: (0,) * len(shape))
    return pl.pallas_call(
        body, name=name, grid=(nblk,),
        in_specs=[row, full(b_blk.shape), full(c_blk.shape), full(tab_f.shape), full(dskip.shape),
                  full(w_glu.shape), full(b_glu.shape)],
        out_specs=[row, row, pl.BlockSpec((1, 1, 2 * ns2), lambda i: (i, 0, 0))],
        out_shape=[jax.ShapeDtypeStruct((s, w), F32), jax.ShapeDtypeStruct((s, w), BF16),
                   jax.ShapeDtypeStruct((nblk, 1, 2 * ns2), F32)],
        scratch_shapes=[pltpu.VMEM((t, 2 * ns2), F32), pltpu.VMEM((1, 2 * ns2), F32)],
        compiler_params=_cparams(),
    )(u, b_blk, c_blk, tab_f, dskip, w_glu, b_glu)


def _s5_bwd(name, u, dys, y, carries, b_blk, c_blk, tab_f, tab_r, dskip, w_glu, b_glu):
    s, w = u.shape
    nkb = w // LANES
    ns2 = b_blk.shape[2] // 2 * nkb
    half = ns2 // nkb
    t = min(S5_ROWS, s)
    nblk = s // t
    ng = t // SUBLANES

    def body(u_ref, dys_ref, y_ref, cs_ref, b_ref, c_ref, tabf_ref, tabr_ref, ds_ref, wg_ref, bg_ref,
             du_ref, db_ref, dc_ref, da_ref, dwg_ref, vec_ref, xs, gs, dyv, fcarry, gcarry):
        @pl.when(pl.program_id(0) == 0)
        def _():
            db_ref[...] = jnp.zeros_like(db_ref)
            dc_ref[...] = jnp.zeros_like(dc_ref)
            da_ref[...] = jnp.zeros_like(da_ref)
            dwg_ref[...] = jnp.zeros_like(dwg_ref)
            vec_ref[...] = jnp.zeros_like(vec_ref)
            gcarry[...] = jnp.zeros_like(gcarry)

        yv = y_ref[...]
        z = _gelu(yv)
        zb = z.astype(BF16)
        gate = _sigmoid(_dot(zb, wg_ref[...], NN) + bg_ref[...])
        dout = dys_ref[...].astype(F32)
        dt = dout * z * gate * (1.0 - gate)
        dtb = dt.astype(BF16)
        dz = dout * gate + _dot(dtb, wg_ref[...], NT)
        dy = dz * _gelu_grad(yv)
        dyv[...] = dy
        dwg_ref[...] += _dot(zb, dtb, TN)
        vec_ref[0:1, :] += jnp.sum(dt, axis=0, keepdims=True)
        vec_ref[1:2, :] += jnp.sum(dy * u_ref[...].astype(F32), axis=0, keepdims=True)

        fcarry[...] = cs_ref[0]
        xs[0:SUBLANES, :] = jnp.broadcast_to(cs_ref[0], (SUBLANES, 2 * ns2))
        for kb in range(nkb):
            bu = _dot(u_ref[:, kb * LANES:(kb + 1) * LANES], b_ref[kb], NN)
            xs[SUBLANES:, kb * half:(kb + 1) * half] = bu[:, :half]
            xs[SUBLANES:, ns2 + kb * half:ns2 + (kb + 1) * half] = bu[:, half:]
        _scan_rows(xs, SUBLANES, ng, ns2, tabf_ref, fcarry, reverse=False)

        for kb in range(nkb):
            dyk = dyv[:, kb * LANES:(kb + 1) * LANES].astype(BF16)
            re = slice(kb * half, (kb + 1) * half)
            im = slice(ns2 + kb * half, ns2 + (kb + 1) * half)
            gs[:, re] = _dot(dyk, c_ref[kb, :half, :], NT)
            gs[:, im] = _dot(dyk, c_ref[kb, half:, :], NT)
            dc_ref[kb, :half, :] += _dot(xs[SUBLANES:, re].astype(BF16), dyk, TN)
            dc_ref[kb, half:, :] += _dot(xs[SUBLANES:, im].astype(BF16), dyk, TN)

        row_is_first = lax.broadcasted_iota(jnp.int32, (SUBLANES, min(S5_CHUNK, ns2)), 0) == 0

        def fold(c0, r0, gr, gi, acc):
            wc = min(S5_CHUNK, ns2)
            re = slice(c0, c0 + wc)
            im = slice(ns2 + c0, ns2 + c0 + wc)
            if r0 is None:
                da_ref[:, re] += acc[0]
                da_ref[:, im] += acc[1]
                return acc
            cur_r = xs[pl.ds(r0 + SUBLANES, SUBLANES), re]
            cur_i = xs[pl.ds(r0 + SUBLANES, SUBLANES), im]
            prv_r = xs[pl.ds(r0, SUBLANES), re]
            prv_i = xs[pl.ds(r0, SUBLANES), im]
            xpr = jnp.where(row_is_first, prv_r[SUBLANES - 1:SUBLANES, :], pltpu.roll(cur_r, 1, 0))
            xpi = jnp.where(row_is_first, prv_i[SUBLANES - 1:SUBLANES, :], pltpu.roll(cur_i, 1, 0))
            return acc[0] + gr * xpr + gi * xpi, acc[1] - gr * xpi + gi * xpr

        zero2 = lambda wc: (jnp.zeros((SUBLANES, wc), F32), jnp.zeros((SUBLANES, wc), F32))
        _scan_rows(gs, 0, ng, ns2, tabr_ref, gcarry, reverse=True, after_group=fold, extra_init=zero2)

        for kb in range(nkb):
            cols = slice(kb * LANES, (kb + 1) * LANES)
            re = slice(kb * half, (kb + 1) * half)
            im = slice(ns2 + kb * half, ns2 + (kb + 1) * half)
            uk = u_ref[:, cols]
            gr = gs[:, re].astype(BF16)
            gi = gs[:, im].astype(BF16)
            db_ref[kb, :, :half] += _dot(uk, gr, TN)
            db_ref[kb, :, half:] += _dot(uk, gi, TN)
            duk = _dot(gr, b_ref[kb, :, :half], NT) + _dot(gi, b_ref[kb, :, half:], NT)
            du_ref[:, cols] = (duk + ds_ref[:, cols] * dyv[:, cols]).astype(BF16)

    rev = lambda i: (nblk - 1 - i, 0)
    row = pl.BlockSpec((t, w), rev)
    full = lambda shape: pl.BlockSpec(shape, lambda i: (0,) * len(shape))
    return pl.pallas_call(
        body, name=name, grid=(nblk,),
        in_specs=[row, row, row, pl.BlockSpec((1, 1, 2 * ns2), lambda i: (nblk - 1 - i, 0, 0)),
                  full(b_blk.shape), full(c_blk.shape), full(tab_f.shape), full(tab_r.shape),
                  full(dskip.shape), full(w_glu.shape), full(b_glu.shape)],
        out_specs=[row, full(b_blk.shape), full(c_blk.shape), full((SUBLANES, 2 * ns2)), full((w, w)),
                   full((SUBLANES, w))],
        out_shape=[jax.ShapeDtypeStruct((s, w), BF16), jax.ShapeDtypeStruct(b_blk.shape, F32),
                   jax.ShapeDtypeStruct(c_blk.shape, F32), jax.ShapeDtypeStruct((SUBLANES, 2 * ns2), F32),
                   jax.ShapeDtypeStruct((w, w), F32), jax.ShapeDtypeStruct((SUBLANES, w), F32)],
        scratch_shapes=[pltpu.VMEM((t + SUBLANES, 2 * ns2), F32), pltpu.VMEM((t, 2 * ns2), F32),
                        pltpu.VMEM((t, w), F32), pltpu.VMEM((1, 2 * ns2), F32), pltpu.VMEM((1, 2 * ns2), F32)],
        compiler_params=_cparams(),
    )(u, dys, y, carries, b_blk, c_blk, tab_f, tab_r, dskip, w_glu, b_glu)


def _log_sigmoid(x):
    return jnp.minimum(x, 0.0) - jnp.log(1.0 + jnp.exp(-jnp.abs(x)))


def _cum_fwd(name, f_t, b_f):
    h, s = f_t.shape
    tc = _pick(s, 512)
    nb = s // tc

    def body(f_ref, b_ref, c_ref, carry):
        @pl.when(pl.program_id(0) == 0)
        def _():
            carry[...] = jnp.zeros_like(carry)

        lf = _log_sigmoid(f_ref[...] + b_ref[...])
        upper = (lax.broadcasted_iota(jnp.int32, (tc, tc), 0) <= lax.broadcasted_iota(jnp.int32, (tc, tc), 1))
        cum = lax.dot_general(lf, upper.astype(F32), NN, precision=lax.Precision.HIGHEST,
                              preferred_element_type=F32) + carry[...]
        c_ref[...] = cum
        carry[...] += jnp.sum(lf, axis=1, keepdims=True)

    blk = pl.BlockSpec((h, tc), lambda i: (0, i))
    return pl.pallas_call(body, name=name, grid=(nb,), in_specs=[blk, pl.BlockSpec((h, 1), lambda i: (0, 0))],
                          out_specs=blk, out_shape=jax.ShapeDtypeStruct((h, s), F32),
                          scratch_shapes=[pltpu.VMEM((h, 1), F32)], compiler_params=_cparams())(f_t, b_f)


def _cum_bwd(name, dcq, dck, f_t, b_f):
    h, s = f_t.shape
    tc = _pick(s, 512)
    nb = s // tc

    def body(dcq_ref, dck_ref, f_ref, b_ref, df_ref, db_ref, carry):
        @pl.when(pl.program_id(0) == 0)
        def _():
            carry[...] = jnp.zeros_like(carry)
            db_ref[...] = jnp.zeros_like(db_ref)

        dc = dcq_ref[...] + dck_ref[...]
        lower = (lax.broadcasted_iota(jnp.int32, (tc, tc), 0) >= lax.broadcasted_iota(jnp.int32, (tc, tc), 1))
        dlf = lax.dot_general(dc, lower.astype(F32), NN, precision=lax.Precision.HIGHEST,
                              preferred_element_type=F32) + carry[...]
        carry[...] += jnp.sum(dc, axis=1, keepdims=True)
        df = dlf * _sigmoid(-(f_ref[...] + b_ref[...]))
        df_ref[...] = df
        db_ref[...] += jnp.broadcast_to(jnp.sum(df, axis=1, keepdims=True), db_ref.shape)

    blk = pl.BlockSpec((h, tc), lambda i: (0, nb - 1 - i))
    return pl.pallas_call(
        body, name=name, grid=(nb,), in_specs=[blk, blk, blk, pl.BlockSpec((h, 1), lambda i: (0, 0))],
        out_specs=[blk, pl.BlockSpec((h, LANES), lambda i: (0, 0))],
        out_shape=[jax.ShapeDtypeStruct((h, s), F32), jax.ShapeDtypeStruct((h, LANES), F32)],
        scratch_shapes=[pltpu.VMEM((h, 1), F32)], compiler_params=_cparams())(dcq, dck, f_t, b_f)


def _attn_fwd(name, q, k, v, cq, ck):
    h, s, dh = q.shape
    t = min(ATT_BLOCK, s)
    nq = s // t
    scale = dh ** -0.5

    def body(q_ref, k_ref, v_ref, cq_ref, ck_ref, o_ref, lse_ref):
        i = pl.program_id(1)
        qv = q_ref[0]
        cqv = cq_ref[0]
        causal = (lax.broadcasted_iota(jnp.int32, (t, t), 1) <= lax.broadcasted_iota(jnp.int32, (t, t), 0))

        def step(j, carry, diagonal):
            m, l, acc = carry
            r0 = pl.multiple_of(j * t, t)
            sc = _dot(qv, k_ref[0, pl.ds(r0, t), :], NT) * scale + (cqv - ck_ref[0, j])
            if diagonal:
                sc = jnp.where(causal, sc, -1e30)
            m_new = jnp.maximum(m, jnp.max(sc, axis=1, keepdims=True))
            p = jnp.exp(sc - m_new)
            alpha = jnp.exp(m - m_new)
            l = alpha * l + jnp.sum(p, axis=1, keepdims=True)
            acc = alpha * acc + _dot(p.astype(BF16), v_ref[0, pl.ds(r0, t), :], NN)
            return m_new, l, acc

        init = (jnp.full((t, 1), -1e30, F32), jnp.zeros((t, 1), F32), jnp.zeros((t, dh), F32))
        carry = lax.fori_loop(0, i, lambda j, c: step(j, c, False), init)
        m, l, acc = step(i, carry, True)
        o_ref[0] = (acc / l).astype(BF16)
        lse_ref[0] = m + jnp.log(l)

    qs = pl.BlockSpec((1, t, dh), lambda hh, i: (hh, i, 0))
    kv = pl.BlockSpec((1, s, dh), lambda hh, i: (hh, 0, 0))
    col = pl.BlockSpec((1, t, 1), lambda hh, i: (hh, i, 0))
    return pl.pallas_call(
        body, name=name, grid=(h, nq),
        in_specs=[qs, kv, kv, col, pl.BlockSpec((1, nq, 1, t), lambda hh, i: (hh, 0, 0, 0))],
        out_specs=[qs, col],
        out_shape=[jax.ShapeDtypeStruct((h, s, dh), BF16), jax.ShapeDtypeStruct((h, s, 1), F32)],
        compiler_params=_cparams(),
    )(q, k, v, cq, ck)


def _attn_bwd(name, q, k, v, o, do, lse, cq, ck):
    h, s, dh = q.shape
    t = min(ATT_BLOCK, s)
    nk = s // t
    scale = dh ** -0.5

    def body(q_ref, k_ref, v_ref, o_ref, do_ref, lse_ref, cq_ref, ck_ref,
             dq_ref, dk_ref, dv_ref, dcq_ref, dck_ref, delta, dk_acc, dv_acc, dc_acc):
        j = pl.program_id(1)

        @pl.when(j == 0)
        def _():
            dq_ref[...] = jnp.zeros_like(dq_ref)
            dcq_ref[...] = jnp.zeros_like(dcq_ref)

            def fill(i, _):
                r0 = pl.multiple_of(i * t, t)
                prod = do_ref[0, pl.ds(r0, t), :].astype(F32) * o_ref[0, pl.ds(r0, t), :].astype(F32)
                delta[pl.ds(r0, t), :] = jnp.sum(prod, axis=1, keepdims=True)
                return 0

            lax.fori_loop(0, nk, fill, 0)

        kv_, vv = k_ref[0], v_ref[0]
        ckv = ck_ref[0, 0]
        dk_acc[...] = jnp.zeros_like(dk_acc)
        dv_acc[...] = jnp.zeros_like(dv_acc)
        dc_acc[...] = jnp.zeros_like(dc_acc)
        causal = (lax.broadcasted_iota(jnp.int32, (t, t), 1) <= lax.broadcasted_iota(jnp.int32, (t, t), 0))

        def step(i, diagonal):
            r0 = pl.multiple_of(i * t, t)
            qi = q_ref[0, pl.ds(r0, t), :]
            doi = do_ref[0, pl.ds(r0, t), :]
            sc = _dot(qi, kv_, NT) * scale + (cq_ref[0, pl.ds(r0, t), :] - ckv)
            p = jnp.exp(sc - lse_ref[0, pl.ds(r0, t), :])
            if diagonal:
                p = jnp.where(causal, p, 0.0)
            dp = _dot(doi, vv, NT)
            ds = p * (dp - delta[pl.ds(r0, t), :])
            dsb = ds.astype(BF16)
            dv_acc[...] += _dot(p.astype(BF16), doi, TN)
            dk_acc[...] += _dot(dsb, qi, TN)
            dq_ref[0, pl.ds(r0, t), :] += _dot(dsb, kv_, NN) * scale
            dc_acc[...] -= jnp.sum(ds, axis=0, keepdims=True)
            dcq_ref[0, pl.ds(r0, t), :] += jnp.sum(ds, axis=1, keepdims=True)

        step(j, True)

        def rest(i, _):
            step(i, False)
            return 0

        lax.fori_loop(j + 1, nk, rest, 0)
        dk_ref[0] = (dk_acc[...] * scale).astype(BF16)
        dv_ref[0] = dv_acc[...].astype(BF16)
        dck_ref[0, 0] = dc_acc[...]

    whole = pl.BlockSpec((1, s, dh), lambda hh, j: (hh, 0, 0))
    blk = pl.BlockSpec((1, t, dh), lambda hh, j: (hh, j, 0))
    col = pl.BlockSpec((1, s, 1), lambda hh, j: (hh, 0, 0))
    ckb = pl.BlockSpec((1, 1, 1, t), lambda hh, j: (hh, j, 0, 0))
    return pl.pallas_call(
        body, name=name, grid=(h, nk),
        in_specs=[whole, blk, blk, whole, whole, col, col, ckb],
        out_specs=[whole, blk, blk, col, ckb],
        out_shape=[jax.ShapeDtypeStruct((h, s, dh), F32), jax.ShapeDtypeStruct((h, s, dh), BF16),
                   jax.ShapeDtypeStruct((h, s, dh), BF16), jax.ShapeDtypeStruct((h, s, 1), F32),
                   jax.ShapeDtypeStruct((h, nk, 1, t), F32)],
        scratch_shapes=[pltpu.VMEM((s, 1), F32), pltpu.VMEM((t, dh), F32), pltpu.VMEM((t, dh), F32),
                        pltpu.VMEM((1, t), F32)],
        compiler_params=_cparams(),
    )(q, k, v, o, do, lse, cq, ck)


def _adamw(name, w, g, m, v):
    r, c = w.shape
    tr = _pick8(r, max(SUBLANES, (1 << 20) // (4 * c)))

    def body(w_ref, g_ref, m_ref, v_ref, d_ref, mo_ref, vo_ref):
        gv = g_ref[...]
        m2 = ADAM_B1 * m_ref[...] + (1.0 - ADAM_B1) * gv
        v2 = ADAM_B2 * v_ref[...] + (1.0 - ADAM_B2) * (gv * gv)
        m_hat = m2 / (1.0 - ADAM_B1 ** ADAM_STEP)
        v_hat = v2 / (1.0 - ADAM_B2 ** ADAM_STEP)
        d_ref[...] = -ADAM_LR * (m_hat / (jnp.sqrt(v_hat) + ADAM_EPS) + ADAM_WD * w_ref[...])
        mo_ref[...] = m2
        vo_ref[...] = v2

    blk = pl.BlockSpec((tr, c), lambda i: (i, 0))
    sh = jax.ShapeDtypeStruct((r, c), F32)
    return pl.pallas_call(body, name=name, grid=(r // tr,), in_specs=[blk] * 4, out_specs=[blk] * 3,
                          out_shape=[sh, sh, sh], compiler_params=_cparams())(w, g, m, v)


def _pick8(dim, target):
    best, t = None, SUBLANES
    while t <= min(dim, target):
        if dim % t == 0:
            best = t
        t += SUBLANES
    return best or dim


def _sum_blocks(name, x, out_dtype):
    n, r, c = x.shape
    tr = _pick8(r, max(SUBLANES, (1 << 19) // (4 * c)))

    def body(x_ref, o_ref):
        acc = x_ref[0].astype(F32)
        for i in range(1, n):
            acc = acc + x_ref[i].astype(F32)
        o_ref[...] = acc.astype(out_dtype)

    return pl.pallas_call(body, name=name, grid=(r // tr,),
                          in_specs=[pl.BlockSpec((n, tr, c), lambda i: (0, i, 0))],
                          out_specs=pl.BlockSpec((tr, c), lambda i: (i, 0)),
                          out_shape=jax.ShapeDtypeStruct((r, c), out_dtype), compiler_params=_cparams())(x)


def _add_mine(name, gbuf, recv, core):
    n, _, r, c = gbuf.shape
    tr = _pick8(r, 256)

    def body(core_ref, g_ref, r_ref, o_ref):
        o_ref[...] = (g_ref[:, 0].astype(F32) + r_ref[...].astype(F32)).astype(BF16)

    grid_spec = pltpu.PrefetchScalarGridSpec(
        num_scalar_prefetch=1, grid=(r // tr,),
        in_specs=[pl.BlockSpec((n, 1, tr, c), lambda i, core_ref: (0, core_ref[0], i, 0)),
                  pl.BlockSpec((n, tr, c), lambda i, core_ref: (0, i, 0))],
        out_specs=pl.BlockSpec((n, tr, c), lambda i, core_ref: (0, i, 0)))
    return pl.pallas_call(body, name=name, grid_spec=grid_spec,
                          out_shape=jax.ShapeDtypeStruct((n, r, c), BF16), compiler_params=_cparams())(core, gbuf, recv)


def _all_gather(name, x_shard):
    m_per, n = x_shard.shape

    def body(x_ref, out_ref, send_sems, recv_sems):
        x, y, c = lax.axis_index("x"), lax.axis_index("y"), lax.axis_index("c")
        me, sibling = (x, y, c), (x, y, 1 - c)
        chips = [(1 - x, y), (x, 1 - y), (1 - x, 1 - y)]

        def rows(px, py, pc):
            return out_ref.at[pl.ds((4 * px + 2 * py + pc) * m_per, m_per), :]

        def copy(k, block, to, src=None):
            return pltpu.make_async_remote_copy(
                src_ref=rows(*block) if src is None else src, dst_ref=rows(*block),
                send_sem=send_sems.at[k], recv_sem=recv_sems.at[k], device_id=to, device_id_type=MESH)

        first = [copy(0, me, sibling, src=x_ref)]
        first += [copy(1 + j, me, (*chip, c), src=x_ref) for j, chip in enumerate(chips)]
        for cp in first:
            cp.start()
        passed = [copy(4 + j, (*chip, c), sibling) for j, chip in enumerate(chips)]
        for j, chip in enumerate(chips):
            copy(1 + j, (*chip, c), me).wait_recv()
            passed[j].start()
        copy(0, sibling, me).wait_recv()
        for j, chip in enumerate(chips):
            copy(4 + j, (*chip, 1 - c), me).wait_recv()
        for cp in first + passed:
            cp.wait_send()

    out = pl.pallas_call(
        body, name=name, out_shape=jax.ShapeDtypeStruct((N_DEV * m_per, n), x_shard.dtype),
        in_specs=[pl.BlockSpec(memory_space=pl.ANY)], out_specs=pl.BlockSpec(memory_space=pl.ANY),
        scratch_shapes=[pltpu.SemaphoreType.DMA((7,)), pltpu.SemaphoreType.DMA((7,))],
    )(x_shard)
    my_dev = 4 * lax.axis_index("x") + 2 * lax.axis_index("y") + lax.axis_index("c")
    return lax.dynamic_update_slice(out, x_shard, (my_dev * m_per, 0))


def _swap_halves(name, gbuf):
    n, _, r, c_ = gbuf.shape

    def body(g_ref, out_ref, send_sem, recv_sem):
        x, y, c = lax.axis_index("x"), lax.axis_index("y"), lax.axis_index("c")
        cp = pltpu.make_async_remote_copy(src_ref=g_ref.at[:, 1 - c], dst_ref=out_ref, send_sem=send_sem,
                                          recv_sem=recv_sem, device_id=(x, y, 1 - c), device_id_type=MESH)
        cp.start()
        cp.wait()

    return pl.pallas_call(
        body, name=name, out_shape=jax.ShapeDtypeStruct((n, r, c_), gbuf.dtype),
        in_specs=[pl.BlockSpec(memory_space=pl.ANY)], out_specs=pl.BlockSpec(memory_space=pl.ANY),
        scratch_shapes=[pltpu.SemaphoreType.DMA, pltpu.SemaphoreType.DMA],
    )(gbuf)


def _chip_exchange(name, part):
    n, r, c_ = part.shape

    def body(p_ref, out_ref, send_sems, recv_sems):
        x, y, c = lax.axis_index("x"), lax.axis_index("y"), lax.axis_index("c")
        my_chip = 2 * x + y
        chips = [(1 - x, y), (x, 1 - y), (1 - x, 1 - y)]
        copies = [pltpu.make_async_remote_copy(
            src_ref=p_ref.at[2 * cx + cy], dst_ref=out_ref.at[my_chip], send_sem=send_sems.at[j],
            recv_sem=recv_sems.at[j], device_id=(cx, cy, c), device_id_type=MESH)
            for j, (cx, cy) in enumerate(chips)]
        for cp in copies:
            cp.start()
        for cp in copies:
            cp.wait()

    out = pl.pallas_call(
        body, name=name, out_shape=jax.ShapeDtypeStruct((n, r, c_), part.dtype),
        in_specs=[pl.BlockSpec(memory_space=pl.ANY)], out_specs=pl.BlockSpec(memory_space=pl.ANY),
        scratch_shapes=[pltpu.SemaphoreType.DMA((3,)), pltpu.SemaphoreType.DMA((3,))],
    )(part)
    my_chip = 2 * lax.axis_index("x") + lax.axis_index("y")
    return lax.dynamic_update_slice(out, lax.dynamic_slice_in_dim(part, my_chip, 1, axis=0), (my_chip, 0, 0))


def _share_halves(name, half):
    r, c_ = half.shape

    def body(h_ref, out_ref, send_sem, recv_sem):
        x, y, c = lax.axis_index("x"), lax.axis_index("y"), lax.axis_index("c")
        cp = pltpu.make_async_remote_copy(src_ref=h_ref, dst_ref=out_ref.at[c], send_sem=send_sem,
                                          recv_sem=recv_sem, device_id=(x, y, 1 - c), device_id_type=MESH)
        cp.start()
        cp.wait()

    out = pl.pallas_call(
        body, name=name, out_shape=jax.ShapeDtypeStruct((2, r, c_), half.dtype),
        in_specs=[pl.BlockSpec(memory_space=pl.ANY)], out_specs=pl.BlockSpec(memory_space=pl.ANY),
        scratch_shapes=[pltpu.SemaphoreType.DMA, pltpu.SemaphoreType.DMA],
    )(half)
    return lax.dynamic_update_slice(out, half[None], (lax.axis_index("c"), 0, 0))


def _pack(arrays, cols, row_multiple, dtype):
    flat = jnp.concatenate([a.reshape(-1).astype(dtype) for a in arrays])
    unit = cols * row_multiple
    total = -(-flat.shape[0] // unit) * unit
    return jnp.pad(flat, (0, total - flat.shape[0])).reshape(total // cols, cols)


def _unpack(buf, shapes):
    flat, out, off = buf.reshape(-1), [], 0
    for sh in shapes:
        n = math.prod(sh)
        out.append(flat[off:off + n].reshape(sh))
        off += n
    return out


def _discretize(lam_re, lam_im, log_dt, b_re, b_im):
    lam = lax.complex(jnp.minimum(lam_re, -EIG_CLIP), lam_im)
    dt = jnp.exp(log_dt)[:, None]
    lam_bar = jnp.exp(lam * dt)
    b_bar = ((lam_bar - 1.0) / lam)[..., None] * lax.complex(b_re, b_im)
    return jnp.real(lam_bar), jnp.imag(lam_bar), jnp.real(b_bar), jnp.imag(b_bar)


def _scan_tables(ar, ai):
    a = lax.complex(ar, ai)
    pw = [a]
    for _ in range(7):
        pw.append(pw[-1] * a)
    rows = jnp.arange(SUBLANES)[:, None]

    def build(p, reverse):
        tabs = []
        for k in (1, 2, 4):
            keep = (rows <= SUBLANES - 1 - k) if reverse else (rows >= k)
            tk = jnp.where(keep, p[k - 1][None, :], 0.0)
            tabs += [jnp.real(tk), jnp.imag(tk)]
        stack = jnp.stack(p[::-1] if reverse else p)
        tabs += [jnp.real(stack), jnp.imag(stack)]
        return jnp.stack(tabs).astype(F32)

    return build(pw, False), build([jnp.conj(p) for p in pw], True)


def _block_diag(per_group, groups_per_block):
    g, a, b = per_group.shape
    x = per_group.reshape(g // groups_per_block, groups_per_block, a, b)
    eye = jnp.eye(groups_per_block, dtype=per_group.dtype)
    out = x[:, :, :, None, :] * eye[None, :, None, :, None]
    return out.reshape(g // groups_per_block, groups_per_block * a, groups_per_block * b)


def _block_diag_extract(dense, groups_per_block, a, b):
    nkb = dense.shape[0]
    x = dense.reshape(nkb, groups_per_block, a, groups_per_block, b)
    idx = jnp.arange(groups_per_block)
    return x[:, idx, :, idx, :].transpose(1, 0, 2, 3).reshape(nkb * groups_per_block, a, b)


def _interleave(a, b, tile):
    k, f = a.shape
    return jnp.stack([a.reshape(k, f // tile, tile), b.reshape(k, f // tile, tile)], axis=2).reshape(k, 2 * f)


def _deinterleave(ab, tile):
    k, f2 = ab.shape
    x = ab.reshape(k, f2 // (2 * tile), 2, tile)
    return x[:, :, 0, :].reshape(k, f2 // 2), x[:, :, 1, :].reshape(k, f2 // 2)


def _layer_fwd(tag, x, mod, p):
    s, d = x.shape
    w_ssm, w_att = p["w_glu"].shape[0], p["w_pb"].shape[0]
    heads = p["b_f"].shape[0]
    dh = w_att // heads
    row = lambda v: v.reshape(1, -1)
    sv = {}

    h = _prenorm_fwd(f"prenorm_mix_{tag}", x, row(p["g_pre_mix"]), row(mod[1]), row(mod[0]))
    uqkv = _mm_plain(f"proj_main_{tag}", h, p["w_main"], "nn", BF16, tn=1024)
    fg = _mm_plain(f"proj_gate_{tag}", h, p["w_gates"], "nn", F32, tn=1024)
    u = uqkv[:, :w_ssm]
    f_t = fg[:, :heads].T
    g_a, g_b = fg[:, F_PAD:F_PAD + d], fg[:, F_PAD + d:]

    y_s5, ys, carries = _s5_fwd(f"s5_fwd_{tag}", u, p["b_blk"], p["c_blk"], p["tab_f"], row(p["d_skip"]),
                                p["w_glu"], row(p["b_glu"]))

    heads_major = lambda a: a.reshape(s, heads, dh).transpose(1, 0, 2)
    q = heads_major(uqkv[:, w_ssm:w_ssm + w_att])
    k = heads_major(uqkv[:, w_ssm + w_att:w_ssm + 2 * w_att])
    v = heads_major(uqkv[:, w_ssm + 2 * w_att:])
    cum = _cum_fwd(f"cum_fwd_{tag}", f_t, p["b_f"].reshape(heads, 1))
    t = min(ATT_BLOCK, s)
    cq, ck = cum.reshape(heads, s, 1), cum.reshape(heads, s // t, 1, t)
    o, lse = _attn_fwd(f"attn_fwd_{tag}", q, k, v, cq, ck)
    ya = o.transpose(1, 0, 2).reshape(s, w_att)

    tile = pl.BlockSpec((_pick(s, 512), _pick(d, 512)), lambda i, j, k_: (i, j))

    def merge(acc, extra_refs, out_refs):
        ya_ref, wpb_ref, ga_ref, gb_ref = extra_refs
        a_ref, b_ref, m_ref = out_refs
        bv = _dot(ya_ref[...], wpb_ref[...], NN)
        a_ref[...] = acc.astype(BF16)
        b_ref[...] = bv.astype(BF16)
        m_ref[...] = (_sigmoid(ga_ref[...]) * acc + _sigmoid(gb_ref[...]) * bv).astype(BF16)

    sd_bf = jax.ShapeDtypeStruct((s, d), BF16)
    (pa, pb, merged), _ = _mm(
        f"merge_{tag}", ys, p["w_pa"], "nn", [sd_bf] * 3, [tile] * 3, merge,
        extra=(ya, p["w_pb"], g_a, g_b),
        extra_specs=[pl.BlockSpec((_pick(s, 512), w_att), lambda i, j, k_: (i, 0)),
                     pl.BlockSpec((w_att, _pick(d, 512)), lambda i, j, k_: (0, j)), tile, tile],
        tk=w_ssm)

    x1, y_mix = _mm_postnorm(f"out_proj_{tag}", merged, p["w_o"], x, row(mod[2]), row(p["g_post_mix"]))

    h2 = _prenorm_fwd(f"prenorm_ffn_{tag}", x1, row(p["g_pre_ffn"]), row(mod[4]), row(mod[3]))
    f_dim = p["w_down"].shape[0]
    tm = _pick(s, 512)

    def swiglu(acc, extra_refs, out_refs):
        ab_ref, hid_ref = out_refs
        av, bv = acc[:, :FFN_TILE], acc[:, FFN_TILE:]
        ab_ref[...] = acc.astype(BF16)
        hid_ref[...] = (av * _sigmoid(av) * bv).astype(BF16)

    (ab, hidden), _ = _mm(
        f"ffn_up_{tag}", h2, p["w_gu"], "nn",
        [jax.ShapeDtypeStruct((s, 2 * f_dim), BF16), jax.ShapeDtypeStruct((s, f_dim), BF16)],
        [pl.BlockSpec((tm, 2 * FFN_TILE), lambda i, j, k_: (i, j)), pl.BlockSpec((tm, FFN_TILE), lambda i, j, k_: (i, j))],
        swiglu, tm=tm, tn=2 * FFN_TILE, tk=1024)
    x2, y_ffn = _mm_postnorm(f"ffn_down_{tag}", hidden, p["w_down"], x1, row(mod[5]), row(p["g_post_ffn"]))

    sv.update(x=x, h=h, uqkv=uqkv, f_t=f_t, g_a=g_a, g_b=g_b, y_s5=y_s5, ys=ys, carries=carries, q=q, k=k, v=v,
              cq=cq, ck=ck, o=o, lse=lse, ya=ya, pa=pa, pb=pb, merged=merged, x1=x1, y_mix=y_mix, h2=h2, ab=ab,
              hidden=hidden, y_ffn=y_ffn)
    return x2, sv


def _mm_postnorm(name, a, w, x, gate, g):
    s, d = x.shape
    tm = _pick(s, 256)
    rowspec = pl.BlockSpec((tm, d), lambda i, j, k: (i, 0))
    vec = pl.BlockSpec((1, d), lambda i, j, k: (0, 0))

    def epilogue(acc, extra_refs, out_refs):
        x_ref, gate_ref, g_ref = extra_refs
        r = lax.rsqrt(jnp.mean(acc * acc, axis=-1, keepdims=True) + RMS_EPS)
        out_refs[0][...] = x_ref[...] + gate_ref[...] * (acc * r * g_ref[...])
        out_refs[1][...] = acc

    sd = jax.ShapeDtypeStruct((s, d), F32)
    (xn, y), _ = _mm(name, a, w, "nn", [sd, sd], [rowspec, rowspec], epilogue, extra=(x, gate, g),
                     extra_specs=[rowspec, vec, vec], tm=tm, tn=d, tk=1536)
    return xn, y


def _layer_bwd(tag, dx2, mod, p, sv):
    s, d = dx2.shape
    w_ssm, w_att = p["w_glu"].shape[0], p["w_pb"].shape[0]
    heads = p["b_f"].shape[0]
    dh = w_att // heads
    row = lambda v: v.reshape(1, -1)
    gr = {}

    dy_ffn, sums = _postnorm_bwd(f"postnorm_bwd_ffn_{tag}", dx2, sv["y_ffn"], row(p["g_post_ffn"]), row(mod[5]))
    d_gate_f, gr["g_post_ffn"] = sums[0], sums[1]
    gr["w_down"] = _mm_plain(f"dw_down_{tag}", sv["hidden"], dy_ffn, "tn", BF16, tk=1024)
    tm = _pick(s, 512)

    def swiglu_bwd(acc, extra_refs, out_refs):
        abv = extra_refs[0][...].astype(F32)
        av, bv = abv[:, :FFN_TILE], abv[:, FFN_TILE:]
        sg = _sigmoid(av)
        da = acc * bv * (sg * (1.0 + av * (1.0 - sg)))
        db = acc * (av * sg)
        out_refs[0][:, :FFN_TILE] = da.astype(BF16)
        out_refs[0][:, FFN_TILE:] = db.astype(BF16)

    ab_spec = pl.BlockSpec((tm, 2 * FFN_TILE), lambda i, j, k_: (i, j))
    (dab,), _ = _mm(f"ffn_down_bwd_{tag}", dy_ffn, p["w_down"], "nt",
                    [jax.ShapeDtypeStruct(sv["ab"].shape, BF16)], [ab_spec], swiglu_bwd,
                    extra=(sv["ab"],), extra_specs=[ab_spec], tm=tm, tn=FFN_TILE, tk=1024)
    gr["w_gu"] = _mm_plain(f"dw_gu_{tag}", sv["h2"], dab, "tn", BF16, tk=1024)
    dh2 = _mm_plain(f"dh_ffn_{tag}", dab, p["w_gu"], "nt", F32, tn=1024, tk=1024)
    dx1, sums = _prenorm_bwd(f"prenorm_bwd_ffn_{tag}", dh2, sv["x1"], row(p["g_pre_ffn"]), row(mod[4]), dx2)
    d_scale_f, d_shift_f, gr["g_pre_ffn"] = sums[0], sums[1], sums[2]

    dy_mix, sums = _postnorm_bwd(f"postnorm_bwd_mix_{tag}", dx1, sv["y_mix"], row(p["g_post_mix"]), row(mod[2]))
    d_gate_m, gr["g_post_mix"] = sums[0], sums[1]
    gr["w_o"] = _mm_plain(f"dw_o_{tag}", sv["merged"], dy_mix, "tn", BF16, tk=1024)

    tile = pl.BlockSpec((_pick(s, 512), _pick(d, 512)), lambda i, j, k_: (i, j))

    def merge_bwd(acc, extra_refs, out_refs):
        a_ref, b_ref, ga_ref, gb_ref = extra_refs
        sa, sb = _sigmoid(ga_ref[...]), _sigmoid(gb_ref[...])
        out_refs[0][...] = (acc * sa).astype(BF16)
        out_refs[1][...] = (acc * sb).astype(BF16)
        out_refs[2][...] = (acc * a_ref[...].astype(F32) * sa * (1.0 - sa)).astype(BF16)
        out_refs[3][...] = (acc * b_ref[...].astype(F32) * sb * (1.0 - sb)).astype(BF16)

    sd_bf = jax.ShapeDtypeStruct((s, d), BF16)
    (d_pa, d_pb, d_ga, d_gb), _ = _mm(f"out_proj_bwd_{tag}", dy_mix, p["w_o"], "nt", [sd_bf] * 4, [tile] * 4, merge_bwd,
                                      extra=(sv["pa"], sv["pb"], sv["g_a"], sv["g_b"]), extra_specs=[tile] * 4, tk=1024)
    gr["w_pa"] = _mm_plain(f"dw_pa_{tag}", sv["ys"], d_pa, "tn", BF16, tk=1024)
    gr["w_pb"] = _mm_plain(f"dw_pb_{tag}", sv["ya"], d_pb, "tn", BF16, tk=1024)
    d_ys = _mm_plain(f"d_ys_{tag}", d_pa, p["w_pa"], "nt", BF16, tk=1024)
    d_ya = _mm_plain(f"d_ya_{tag}", d_pb, p["w_pb"], "nt", BF16, tk=1024)

    do = d_ya.reshape(s, heads, dh).transpose(1, 0, 2)
    dq, dk, dv, dcq, dck = _attn_bwd(f"attn_bwd_{tag}", sv["q"], sv["k"], sv["v"], sv["o"], do, sv["lse"], sv["cq"], sv["ck"])
    d_f_t, d_bf = _cum_bwd(f"cum_bwd_{tag}", dcq.reshape(heads, s), dck.reshape(heads, s), sv["f_t"],
                           p["b_f"].reshape(heads, 1))
    gr["b_f"] = d_bf[:, 0]
    tokens_major = lambda a: a.transpose(1, 0, 2).reshape(s, w_att).astype(BF16)

    u = sv["uqkv"][:, :w_ssm]
    du, d_bblk, d_cblk, d_abar, gr["w_glu"], vec = _s5_bwd(
        f"s5_bwd_{tag}", u, d_ys, sv["y_s5"], sv["carries"], p["b_blk"], p["c_blk"], p["tab_f"], p["tab_r"],
        row(p["d_skip"]), p["w_glu"], row(p["b_glu"]))
    gr["b_glu"], gr["d_skip"] = vec[0], vec[1]
    gr["b_blk"], gr["c_blk"], gr["a_bar"] = d_bblk, d_cblk, d_abar

    d_main = jnp.concatenate([du, tokens_major(dq), tokens_major(dk), tokens_major(dv)], axis=1)
    d_f = jnp.pad(d_f_t.T, ((0, 0), (0, F_PAD - heads))).astype(BF16)
    d_gates = jnp.concatenate([d_f, d_ga, d_gb], axis=1)
    gr["w_main"] = _mm_plain(f"dw_main_{tag}", sv["h"], d_main, "tn", BF16, tk=1024)
    gr["w_gates"] = _mm_plain(f"dw_gates_{tag}", sv["h"], d_gates, "tn", BF16, tk=1024)
    dh_a = _mm_plain(f"dh_main_{tag}", d_main, p["w_main"], "nt", F32, tn=1024, tk=1024)
    dh1 = _mm_plain(f"dh_gates_{tag}", d_gates, p["w_gates"], "nt", F32, add=dh_a, tn=1024, tk=1024)
    dx0, sums = _prenorm_bwd(f"prenorm_bwd_mix_{tag}", dh1, sv["x"], row(p["g_pre_mix"]), row(mod[1]), dx1)
    d_scale_m, d_shift_m, gr["g_pre_mix"] = sums[0], sums[1], sums[2]

    d_mod = jnp.stack([d_shift_m, d_scale_m, d_gate_m, d_shift_f, d_scale_f, d_gate_f])
    return dx0, d_mod, gr


BIG = ("w_in", "w_glu", "w_pa", "w_pb", "w_o", "w_ffn_gate", "w_ffn_up", "w_ffn_down")
BIG_SHARD_AXIS = {"w_in": 2, "w_glu": 1, "w_pa": 2, "w_pb": 2, "w_o": 1, "w_ffn_gate": 2, "w_ffn_up": 2, "w_ffn_down": 1}
SMALL = ("b_ada", "g_pre_mix", "g_post_mix", "g_pre_ffn", "g_post_ffn", "lam_re", "lam_im", "log_dt", "b_re", "b_im",
         "c_re", "c_im", "d_skip", "b_glu", "b_f")
WEIGHTS = ("w_ada", "b_ada", "g_pre_mix", "g_post_mix", "g_pre_ffn", "g_post_ffn", "w_in", "lam_re", "lam_im", "log_dt",
           "b_re", "b_im", "c_re", "c_im", "d_skip", "w_glu", "b_glu", "b_f", "w_pa", "w_pb", "w_o", "w_ffn_gate",
           "w_ffn_up", "w_ffn_down")


def _prepare_layer(full, small, l):
    w_in = full["w_in"][l]
    d = w_in.shape[0]
    heads = small["b_f"].shape[1]
    n_groups, n_state, group_ch = small["b_re"].shape[1:]
    w_ssm = n_groups * group_ch
    w_att = full["w_pb"].shape[1]
    n_main = w_ssm + 3 * w_att
    gpb = LANES // group_ch
    p = {}
    p["w_main"] = w_in[:, :n_main]
    p["w_gates"] = jnp.concatenate(
        [w_in[:, n_main:n_main + heads], jnp.zeros((d, F_PAD - heads), BF16), w_in[:, n_main + heads:]], axis=1)
    p["w_glu"], p["w_pa"], p["w_pb"], p["w_o"] = (full[n][l] for n in ("w_glu", "w_pa", "w_pb", "w_o"))
    p["w_gu"] = _interleave(full["w_ffn_gate"][l], full["w_ffn_up"][l], FFN_TILE)
    p["w_down"] = full["w_ffn_down"][l]
    for n in ("g_pre_mix", "g_post_mix", "g_pre_ffn", "g_post_ffn", "d_skip", "b_glu", "b_f"):
        p[n] = small[n][l]
    ar, ai, br, bi = _discretize(small["lam_re"][l], small["lam_im"][l], small["log_dt"][l], small["b_re"][l], small["b_im"][l])
    p["tab_f"], p["tab_r"] = _scan_tables(ar.reshape(-1), ai.reshape(-1))
    bre = _block_diag(br.transpose(0, 2, 1), gpb)
    bim = _block_diag(bi.transpose(0, 2, 1), gpb)
    p["b_blk"] = jnp.concatenate([bre, bim], axis=2).astype(BF16)
    cre = _block_diag(small["c_re"][l].transpose(0, 2, 1), gpb)
    cim = _block_diag(small["c_im"][l].transpose(0, 2, 1), gpb)
    p["c_blk"] = jnp.concatenate([cre, -cim], axis=1).astype(BF16)
    return p


def _compact_partials(gr, n_state, group_ch):
    gpb = LANES // group_ch
    half = gpb * n_state
    out = dict(gr)
    out["bbar_re"] = _block_diag_extract(gr["b_blk"][:, :, :half], gpb, group_ch, n_state).transpose(0, 2, 1)
    out["bbar_im"] = _block_diag_extract(gr["b_blk"][:, :, half:], gpb, group_ch, n_state).transpose(0, 2, 1)
    out["c_re"] = _block_diag_extract(gr["c_blk"][:, :half, :], gpb, n_state, group_ch).transpose(0, 2, 1)
    out["c_im"] = -_block_diag_extract(gr["c_blk"][:, half:, :], gpb, n_state, group_ch).transpose(0, 2, 1)
    return out


def _small_grads_from_partials(gr, small, l):
    n_groups, n_state, _ = small["b_re"].shape[1:]
    ns2 = n_groups * n_state
    d_abar = jnp.sum(gr["a_bar"], axis=0)
    dar, dai = d_abar[:ns2].reshape(n_groups, n_state), d_abar[ns2:].reshape(n_groups, n_state)
    args = (small["lam_re"][l], small["lam_im"][l], small["log_dt"][l], small["b_re"][l], small["b_im"][l])
    _, vjp = jax.vjp(_discretize, *args)
    d_lam_re, d_lam_im, d_log_dt, d_b_re, d_b_im = vjp((dar, dai, gr["bbar_re"], gr["bbar_im"]))
    return dict(lam_re=d_lam_re, lam_im=d_lam_im, log_dt=d_log_dt, b_re=d_b_re, b_im=d_b_im,
                c_re=gr["c_re"], c_im=gr["c_im"])


def _shard_of(a, axis, j):
    n = a.shape[axis] // N_CHIPS
    return lax.slice_in_dim(a, j * n, (j + 1) * n, axis=axis)


def _fwd_bwd(xs, target, mods, layers, heads):
    depth = len(layers)
    saved = []
    act = xs
    for l in range(depth):
        act, sv = _layer_fwd(str(l), act, mods[l], layers[l])
        saved.append(sv)
    dx, loss_blk = _loss_grad("loss", act, target)
    grads, d_mods = [None] * depth, [None] * depth
    for l in reversed(range(depth)):
        dx, d_mods[l], grads[l] = _layer_bwd(str(l), dx, mods[l], layers[l], saved[l])
    full_grads = {n: [] for n in BIG}
    for l in range(depth):
        g = grads[l]
        full_grads["w_in"].append(jnp.concatenate([g["w_main"], g["w_gates"][:, :heads], g["w_gates"][:, F_PAD:]], axis=1))
        dg, du_ = _deinterleave(g["w_gu"], FFN_TILE)
        full_grads["w_ffn_gate"].append(dg)
        full_grads["w_ffn_up"].append(du_)
        full_grads["w_ffn_down"].append(g["w_down"])
        full_grads["w_glu"].append(g["w_glu"].astype(BF16))
        for n in ("w_pa", "w_pb", "w_o"):
            full_grads[n].append(g[n])
    stacked = {n: jnp.stack(full_grads[n]) for n in BIG}
    return loss_blk, dx, d_mods, grads, stacked


def kernel(x, c, w_ada, b_ada, g_pre_mix, g_post_mix, g_pre_ffn, g_post_ffn, w_in, lam_re, lam_im, log_dt, b_re, b_im, c_re, c_im, d_skip, w_glu, b_glu, b_f, w_pa, w_pb, w_o, w_ffn_gate, w_ffn_up, w_ffn_down, loss_target, m_w_ada, m_b_ada, m_g_pre_mix, m_g_post_mix, m_g_pre_ffn, m_g_post_ffn, m_w_in, m_lam_re, m_lam_im, m_log_dt, m_b_re, m_b_im, m_c_re, m_c_im, m_d_skip, m_w_glu, m_b_glu, m_b_f, m_w_pa, m_w_pb, m_w_o, m_w_ffn_gate, m_w_ffn_up, m_w_ffn_down, v_w_ada, v_b_ada, v_g_pre_mix, v_g_post_mix, v_g_pre_ffn, v_g_post_ffn, v_w_in, v_lam_re, v_lam_im, v_log_dt, v_b_re, v_b_im, v_c_re, v_c_im, v_d_skip, v_w_glu, v_b_glu, v_b_f, v_w_pa, v_w_pb, v_w_o, v_w_ffn_gate, v_w_ffn_up, v_w_ffn_down):
    local = dict(locals())
    weights = {n: local[n] for n in WEIGHTS}
    moments_m = {n: local["m_" + n] for n in WEIGHTS}
    moments_v = {n: local["v_" + n] for n in WEIGHTS}
    depth, d = g_pre_mix.shape
    n_mod = w_ada.shape[2] * N_CHIPS // d
    mx, my, mc = lax.axis_index("x"), lax.axis_index("y"), lax.axis_index("c")
    my_chip = 2 * mx + my
    my_dev = 4 * mx + 2 * my + mc
    xs = x[0]

    shard_shapes = [weights[n].shape for n in BIG]
    wflat = _pack([weights[n] for n in BIG], COMM_LANES, 32, BF16)
    rh = wflat.shape[0] // 2
    mine = lax.dynamic_slice_in_dim(wflat, mc * rh, rh, axis=0)
    gathered = _all_gather("gather_weights", mine).reshape(N_CHIPS, 2 * rh, COMM_LANES)
    per_chip = [_unpack(gathered[j], shard_shapes) for j in range(N_CHIPS)]
    full = {n: jnp.concatenate([per_chip[j][i] for j in range(N_CHIPS)], axis=BIG_SHARD_AXIS[n]) for i, n in enumerate(BIG)}
    small = {n: weights[n] for n in SMALL}
    layers = [_prepare_layer(full, small, l) for l in range(depth)]

    c_pad = jnp.pad(c, ((0, SUBLANES - 1), (0, 0)))
    c_all = _all_gather("gather_cond", c_pad).reshape(N_DEV, SUBLANES, d)[:, 0, :]
    silu = lambda v: v * _sigmoid(v)
    n_cols = w_ada.shape[2]
    mod_shard = []
    for l in range(depth):
        bias = lax.dynamic_slice_in_dim(b_ada[l], my_chip * n_cols, n_cols)
        mod_shard.append(_mm_plain(f"ada_{l}", c_all, w_ada[l], "nn", F32, add=jnp.broadcast_to(bias, (N_DEV, n_cols)),
                                   a_fn=silu, tm=N_DEV, tn=512, tk=1024))
    mod_block = jnp.concatenate(mod_shard, axis=1)
    mod_all = _all_gather("gather_mod", mod_block).reshape(N_DEV, N_DEV, depth, n_cols)
    mod_rows = lax.dynamic_index_in_dim(mod_all[0::2], my_dev, axis=1, keepdims=False)
    mods = [mod_rows[:, l, :].reshape(n_mod, d) for l in range(depth)]

    loss_blk, dx, d_mods, grads, stacked = _fwd_bwd(xs, loss_target[0], mods, layers, b_f.shape[1])
    loss = lax.psum(loss_blk[0, 0], ("x", "y", "c"))
    grad_x = dx[None]

    gbuf = jnp.stack([_pack([_shard_of(stacked[n], BIG_SHARD_AXIS[n], j) for n in BIG], COMM_LANES, 32, BF16)
                      for j in range(N_CHIPS)]).reshape(N_CHIPS, 2, rh, COMM_LANES)
    from_sibling = _swap_halves("grads_swap_cores", gbuf)
    chip_part = _add_mine("grads_add_cores", gbuf, from_sibling, mc.astype(jnp.int32).reshape(1))
    from_chips = _chip_exchange("grads_exchange_chips", chip_part)
    my_half = _sum_blocks("grads_sum_chips", from_chips, F32)
    shard_flat = _share_halves("grads_share_cores", my_half).reshape(2 * rh, COMM_LANES)
    big_grads = dict(zip(BIG, _unpack(shard_flat, shard_shapes)))

    partial_names = ("g_pre_mix", "g_post_mix", "g_pre_ffn", "g_post_ffn", "d_skip", "b_glu", "b_f", "a_bar",
                     "bbar_re", "bbar_im", "c_re", "c_im")
    n_state, group_ch = b_re.shape[2:]
    contrib = list(d_mods)
    for l in range(depth):
        compact = _compact_partials(grads[l], n_state, group_ch)
        contrib += [compact[n] for n in partial_names]
    contrib_shapes = [a.shape for a in contrib]
    block = _pack(contrib, LANES, SUBLANES, F32)
    rows = block.shape[0]
    all_blocks = _all_gather("gather_small_grads", block).reshape(N_DEV, rows, LANES)
    summed = _unpack(_sum_blocks("sum_small_grads", all_blocks, F32), contrib_shapes)
    per_layer = len(partial_names)
    small_grads = {n: [] for n in SMALL}
    d_mod_all = []
    for l in range(depth):
        small_grads["b_ada"].append(summed[l].reshape(-1))
        gl = dict(zip(partial_names, summed[depth + l * per_layer:depth + (l + 1) * per_layer]))
        for n in ("g_pre_mix", "g_post_mix", "g_pre_ffn", "g_post_ffn", "d_skip", "b_glu", "b_f"):
            small_grads[n].append(gl[n])
        for n, gval in _small_grads_from_partials(gl, small, l).items():
            small_grads[n].append(gval)
        d_mod_all.append(all_blocks.reshape(N_DEV, rows * LANES)[:, l * n_mod * d:(l + 1) * n_mod * d])
    small_grads = {n: jnp.stack(v) for n, v in small_grads.items()}

    g_w_ada = []
    for l in range(depth):
        cols = lax.dynamic_slice_in_dim(d_mod_all[l], my_chip * n_cols, n_cols, axis=1)
        g_w_ada.append(_mm_plain(f"dw_ada_{l}", c_all, cols, "tn", F32, a_fn=silu, tm=512, tn=512, tk=N_DEV))
    all_grads = dict(big_grads)
    all_grads.update(small_grads)
    all_grads["w_ada"] = jnp.stack(g_w_ada)

    delta, new_m, new_v = {}, {}, {}
    for n in ("w_ada",) + BIG:
        shape = weights[n].shape
        two_d = lambda a: a.reshape(-1, shape[-1])
        dl, nm, nv = _adamw(f"adamw_{n}", two_d(weights[n]), two_d(all_grads[n]), two_d(moments_m[n]), two_d(moments_v[n]))
        delta[n], new_m[n], new_v[n] = dl.reshape(shape), nm.reshape(shape), nv.reshape(shape)
    small_shapes = [weights[n].shape for n in SMALL]
    packed = [_pack([src[n] for n in SMALL], LANES, SUBLANES, F32) for src in (weights, all_grads, moments_m, moments_v)]
    outs = _adamw("adamw_small", *packed)
    for dst, buf in zip((delta, new_m, new_v), outs):
        dst.update(dict(zip(SMALL, _unpack(buf, small_shapes))))

    return (loss, grad_x, *[all_grads[n] for n in WEIGHTS], *[delta[n] for n in WEIGHTS],
            *[new_m[n] for n in WEIGHTS], *[new_v[n] for n in WEIGHTS])
```

```python
import functools
import math

import jax
import jax.numpy as jnp
from jax import lax
from jax.experimental import pallas as pl
from jax.experimental.pallas import tpu as pltpu

F32 = jnp.float32
BF16 = jnp.bfloat16
MESH = pl.DeviceIdType.MESH

RMS_EPS = 1e-6
EIG_CLIP = 1e-4
ADAM_LR, ADAM_B1, ADAM_B2, ADAM_EPS, ADAM_WD, ADAM_STEP = 0.001, 0.9, 0.999, 1e-08, 0.01, 10

LANES = 128
SUBLANES = 8
VMEM_LIMIT = 56 * 1024 * 1024
S5_ROWS = 256
S5_CHUNK = 256
ATT_BLOCK = 512
FFN_TILE = 256
F_PAD = 256
COMM_LANES = 1024
N_CHIPS = 4
N_DEV = 8

NN = (((1,), (0,)), ((), ()))
NT = (((1,), (1,)), ((), ()))
TN = (((0,), (0,)), ((), ()))
_DN = {"nn": NN, "nt": NT, "tn": TN}


def _cparams(**kw):
    return pltpu.CompilerParams(vmem_limit_bytes=VMEM_LIMIT, **kw)


def _pick(dim, target):
    best, t = None, LANES
    while t <= min(dim, target):
        if dim % t == 0:
            best = t
        t += LANES
    return best or dim


def _sigmoid(x):
    return 1.0 / (1.0 + jnp.exp(-x))


def _dot(a, b, dn):
    return lax.dot_general(a, b, dn, preferred_element_type=F32)


def _mm(name, a, b, mode, out_shapes, out_specs, epilogue, extra=(), extra_specs=(),
        tm=512, tn=512, tk=512, a_fn=None):
    if mode == "nn":
        (m, kd), (_, n) = a.shape, b.shape
    elif mode == "nt":
        (m, kd), (n, _) = a.shape, b.shape
    else:
        (kd, m), (_, n) = a.shape, b.shape
    tm, tn, tk = _pick(m, tm), _pick(n, tn), _pick(kd, tk)
    nk = kd // tk
    n_extra, n_out = len(extra), len(out_shapes)

    def body(*refs):
        a_ref, b_ref = refs[0], refs[1]
        extra_refs = refs[2:2 + n_extra]
        out_refs = refs[2 + n_extra:2 + n_extra + n_out]
        acc = refs[-1]
        k = pl.program_id(2)

        @pl.when(k == 0)
        def _():
            acc[...] = jnp.zeros_like(acc)

        av = a_ref[...]
        if a_fn is not None:
            av = a_fn(av.astype(F32))
        acc[...] += _dot(av.astype(BF16), b_ref[...].astype(BF16), _DN[mode])

        @pl.when(k == nk - 1)
        def _():
            epilogue(acc[...], extra_refs, out_refs)

    if mode == "tn":
        a_spec = pl.BlockSpec((tk, tm), lambda i, j, k: (k, i))
    else:
        a_spec = pl.BlockSpec((tm, tk), lambda i, j, k: (i, k))
    if mode == "nt":
        b_spec = pl.BlockSpec((tn, tk), lambda i, j, k: (j, k))
    else:
        b_spec = pl.BlockSpec((tk, tn), lambda i, j, k: (k, j))
    res = pl.pallas_call(
        body, name=name, grid=(m // tm, n // tn, nk),
        in_specs=[a_spec, b_spec, *extra_specs],
        out_specs=list(out_specs), out_shape=list(out_shapes),
        scratch_shapes=[pltpu.VMEM((tm, tn), F32)],
        compiler_params=_cparams(),
    )(a, b, *extra)
    return res, (tm, tn, tk)


def _mm_plain(name, a, b, mode, out_dtype, add=None, a_fn=None, tm=512, tn=512, tk=512):
    if mode == "nn":
        m, n = a.shape[0], b.shape[1]
    elif mode == "nt":
        m, n = a.shape[0], b.shape[0]
    else:
        m, n = a.shape[1], b.shape[1]
    tm_, tn_ = _pick(m, tm), _pick(n, tn)
    spec = pl.BlockSpec((tm_, tn_), lambda i, j, k: (i, j))

    def epilogue(acc, extra_refs, out_refs):
        if add is not None:
            acc = acc + extra_refs[0][...]
        out_refs[0][...] = acc.astype(out_dtype)

    extra = () if add is None else (add,)
    (out,), _ = _mm(name, a, b, mode, [jax.ShapeDtypeStruct((m, n), out_dtype)], [spec], epilogue,
                    extra=extra, extra_specs=[spec] * len(extra), tm=tm, tn=tn, tk=tk, a_fn=a_fn)
    return out


def _row_tile(s, d):
    return _pick(s, max(SUBLANES, (1 << 20) // (4 * d)))


def _prenorm_fwd(name, x, g, scale, shift):
    s, d = x.shape
    tr = _row_tile(s, d)

    def body(x_ref, g_ref, sc_ref, sh_ref, h_ref):
        xv = x_ref[...]
        r = lax.rsqrt(jnp.mean(xv * xv, axis=-1, keepdims=True) + RMS_EPS)
        h_ref[...] = ((xv * r * g_ref[...]) * (1.0 + sc_ref[...]) + sh_ref[...]).astype(BF16)

    row = pl.BlockSpec((tr, d), lambda i: (i, 0))
    vec = pl.BlockSpec((1, d), lambda i: (0, 0))
    return pl.pallas_call(body, name=name, grid=(s // tr,), in_specs=[row, vec, vec, vec], out_specs=row,
                          out_shape=jax.ShapeDtypeStruct((s, d), BF16), compiler_params=_cparams())(x, g, scale, shift)


def _prenorm_bwd(name, dh, x, g, scale, dx_res):
    s, d = x.shape
    tr = _row_tile(s, d)

    def body(dh_ref, x_ref, g_ref, sc_ref, dxr_ref, dx_ref, sums_ref):
        @pl.when(pl.program_id(0) == 0)
        def _():
            sums_ref[...] = jnp.zeros_like(sums_ref)

        xv, dhv, gv = x_ref[...], dh_ref[...].astype(F32), g_ref[...]
        r = lax.rsqrt(jnp.mean(xv * xv, axis=-1, keepdims=True) + RMS_EPS)
        xhat = xv * r
        dxn = dhv * (1.0 + sc_ref[...])
        dxhat = dxn * gv
        dx = r * (dxhat - xhat * jnp.mean(dxhat * xhat, axis=-1, keepdims=True))
        dx_ref[...] = dxr_ref[...] + dx
        sums_ref[0:1, :] += jnp.sum(dhv * (xhat * gv), axis=0, keepdims=True)
        sums_ref[1:2, :] += jnp.sum(dhv, axis=0, keepdims=True)
        sums_ref[2:3, :] += jnp.sum(dxn * xhat, axis=0, keepdims=True)

    row = pl.BlockSpec((tr, d), lambda i: (i, 0))
    vec = pl.BlockSpec((1, d), lambda i: (0, 0))
    acc = pl.BlockSpec((SUBLANES, d), lambda i: (0, 0))
    return pl.pallas_call(
        body, name=name, grid=(s // tr,), in_specs=[row, row, vec, vec, row], out_specs=[row, acc],
        out_shape=[jax.ShapeDtypeStruct((s, d), F32), jax.ShapeDtypeStruct((SUBLANES, d), F32)],
        compiler_params=_cparams())(dh, x, g, scale, dx_res)


def _postnorm_bwd(name, dxn, y, g, gate):
    s, d = y.shape
    tr = _row_tile(s, d)

    def body(dx_ref, y_ref, g_ref, gt_ref, dy_ref, sums_ref):
        @pl.when(pl.program_id(0) == 0)
        def _():
            sums_ref[...] = jnp.zeros_like(sums_ref)

        yv, dxv, gv = y_ref[...], dx_ref[...], g_ref[...]
        r = lax.rsqrt(jnp.mean(yv * yv, axis=-1, keepdims=True) + RMS_EPS)
        yhat = yv * r
        dn = dxv * gt_ref[...]
        dyhat = dn * gv
        dy_ref[...] = (r * (dyhat - yhat * jnp.mean(dyhat * yhat, axis=-1, keepdims=True))).astype(BF16)
        sums_ref[0:1, :] += jnp.sum(dxv * (yhat * gv), axis=0, keepdims=True)
        sums_ref[1:2, :] += jnp.sum(dn * yhat, axis=0, keepdims=True)

    row = pl.BlockSpec((tr, d), lambda i: (i, 0))
    vec = pl.BlockSpec((1, d), lambda i: (0, 0))
    acc = pl.BlockSpec((SUBLANES, d), lambda i: (0, 0))
    return pl.pallas_call(
        body, name=name, grid=(s // tr,), in_specs=[row, row, vec, vec], out_specs=[row, acc],
        out_shape=[jax.ShapeDtypeStruct((s, d), BF16), jax.ShapeDtypeStruct((SUBLANES, d), F32)],
        compiler_params=_cparams())(dxn, y, g, gate)


def _loss_grad(name, y, target):
    s, d = y.shape
    tr = _row_tile(s, d)

    def body(y_ref, t_ref, dy_ref, loss_ref):
        @pl.when(pl.program_id(0) == 0)
        def _():
            loss_ref[...] = jnp.zeros_like(loss_ref)

        err = y_ref[...] - t_ref[...]
        dy_ref[...] = err * (1.0 / d)
        part = jnp.sum(jnp.sum(err * err, axis=-1, keepdims=True), axis=0, keepdims=True) * (0.5 / d)
        loss_ref[...] += jnp.broadcast_to(part, loss_ref.shape)

    row = pl.BlockSpec((tr, d), lambda i: (i, 0))
    acc = pl.BlockSpec((SUBLANES, LANES), lambda i: (0, 0))
    return pl.pallas_call(
        body, name=name, grid=(s // tr,), in_specs=[row, row], out_specs=[row, acc],
        out_shape=[jax.ShapeDtypeStruct((s, d), F32), jax.ShapeDtypeStruct((SUBLANES, LANES), F32)],
        compiler_params=_cparams())(y, target)


def _gelu(y):
    c = math.sqrt(2.0 / math.pi)
    return 0.5 * y * (1.0 + jnp.tanh(c * (y + 0.044715 * (y * y * y))))


def _gelu_grad(y):
    c = math.sqrt(2.0 / math.pi)
    th = jnp.tanh(c * (y + 0.044715 * (y * y * y)))
    return 0.5 * (1.0 + th) + 0.5 * y * (1.0 - th * th) * c * (1.0 + 3.0 * 0.044715 * (y * y))


def _scan_rows(x_ref, row0, n_groups, ns2, tab_ref, carry_ref, reverse, after_group=None, extra_init=None):
    wc = min(S5_CHUNK, ns2)
    shifts = (1, 2, 4)
    for c0 in range(0, ns2, wc):
        re = slice(c0, c0 + wc)
        im = slice(ns2 + c0, ns2 + c0 + wc)
        tabs = [tab_ref[k, :, re] for k in range(8)]

        def group(i, carry, re=re, im=im, tabs=tabs, c0=c0):
            cr, ci, extra = carry
            g = (n_groups - 1 - i) if reverse else i
            r0 = pl.multiple_of(row0 + g * SUBLANES, SUBLANES)
            br = x_ref[pl.ds(r0, SUBLANES), re]
            bi = x_ref[pl.ds(r0, SUBLANES), im]
            for lvl, k in enumerate(shifts):
                mr, mi = tabs[2 * lvl], tabs[2 * lvl + 1]
                sh = (SUBLANES - k) if reverse else k
                sr = pltpu.roll(br, sh, 0)
                si = pltpu.roll(bi, sh, 0)
                br, bi = br + mr * sr - mi * si, bi + mr * si + mi * sr
            apr, api = tabs[6], tabs[7]
            xr = br + apr * cr - api * ci
            xi = bi + apr * ci + api * cr
            x_ref[pl.ds(r0, SUBLANES), re] = xr
            x_ref[pl.ds(r0, SUBLANES), im] = xi
            if after_group is not None:
                extra = after_group(c0, r0, xr, xi, extra)
            if reverse:
                return xr[0:1, :], xi[0:1, :], extra
            return xr[SUBLANES - 1:SUBLANES, :], xi[SUBLANES - 1:SUBLANES, :], extra

        init_extra = extra_init(wc) if extra_init is not None else 0
        cr, ci, extra = lax.fori_loop(0, n_groups, group, (carry_ref[0:1, re], carry_ref[0:1, im], init_extra))
        carry_ref[0:1, re] = cr
        carry_ref[0:1, im] = ci
        if after_group is not None:
            after_group(c0, None, None, None, extra)


def _s5_fwd(name, u, b_blk, c_blk, tab_f, dskip, w_glu, b_glu):
    s, w = u.shape
    nkb = w // LANES
    ns2 = b_blk.shape[2] // 2 * nkb
    half = ns2 // nkb
    t = min(S5_ROWS, s)
    nblk = s // t

    def body(u_ref, b_ref, c_ref, tab_ref, ds_ref, wg_ref, bg_ref, y_ref, ys_ref, cs_ref, xs, carry):
        @pl.when(pl.program_id(0) == 0)
        def _():
            carry[...] = jnp.zeros_like(carry)

        cs_ref[0] = carry[...]
        for kb in range(nkb):
            bu = _dot(u_ref[:, kb * LANES:(kb + 1) * LANES], b_ref[kb], NN)
            xs[:, kb * half:(kb + 1) * half] = bu[:, :half]
            xs[:, ns2 + kb * half:ns2 + (kb + 1) * half] = bu[:, half:]
        _scan_rows(xs, 0, t // SUBLANES, ns2, tab_ref, carry, reverse=False)
        for kb in range(nkb):
            cols = slice(kb * LANES, (kb + 1) * LANES)
            yk = _dot(xs[:, kb * half:(kb + 1) * half].astype(BF16), c_ref[kb, :half, :], NN)
            yk += _dot(xs[:, ns2 + kb * half:ns2 + (kb + 1) * half].astype(BF16), c_ref[kb, half:, :], NN)
            y_ref[:, cols] = yk + ds_ref[:, cols] * u_ref[:, cols].astype(F32)
        z = _gelu(y_ref[...])
        gate = _sigmoid(_dot(z.astype(BF16), wg_ref[...], NN) + bg_ref[...])
        ys_ref[...] = (z * gate).astype(BF16)

    row = pl.BlockSpec((t, w), lambda i: (i, 0))
    full = lambda shape: pl.BlockSpec(shape, lambda i: (0,) * len(shape))
    return pl.pallas_call(
        body, name=name, grid=(nblk,),
        in_specs=[row, full(b_blk.shape), full(c_blk.shape), full(tab_f.shape), full(dskip.shape),
                  full(w_glu.shape), full(b_glu.shape)],
        out_specs=[row, row, pl.BlockSpec((1, 1, 2 * ns2), lambda i: (i, 0, 0))],
        out_shape=[jax.ShapeDtypeStruct((s, w), F32), jax.ShapeDtypeStruct((s, w), BF16),
                   jax.ShapeDtypeStruct((nblk, 1, 2 * ns2), F32)],
        scratch_shapes=[pltpu.VMEM((t, 2 * ns2), F32), pltpu.VMEM((1, 2 * ns2), F32)],
        compiler_params=_cparams(),
    )(u, b_blk, c_blk, tab_f, dskip, w_glu, b_glu)


def _s5_bwd(name, u, dys, y, carries, b_blk, c_blk, tab_f, tab_r, dskip, w_glu, b_glu):
    s, w = u.shape
    nkb = w // LANES
    ns2 = b_blk.shape[2] // 2 * nkb
    half = ns2 // nkb
    t = min(S5_ROWS, s)
    nblk = s // t
    ng = t // SUBLANES

    def body(u_ref, dys_ref, y_ref, cs_ref, b_ref, c_ref, tabf_ref, tabr_ref, ds_ref, wg_ref, bg_ref,
             du_ref, db_ref, dc_ref, da_ref, dwg_ref, vec_ref, xs, gs, dyv, fcarry, gcarry):
        @pl.when(pl.program_id(0) == 0)
        def _():
            db_ref[...] = jnp.zeros_like(db_ref)
            dc_ref[...] = jnp.zeros_like(dc_ref)
            da_ref[...] = jnp.zeros_like(da_ref)
            dwg_ref[...] = jnp.zeros_like(dwg_ref)
            vec_ref[...] = jnp.zeros_like(vec_ref)
            gcarry[...] = jnp.zeros_like(gcarry)

        yv = y_ref[...]
        z = _gelu(yv)
        zb = z.astype(BF16)
        gate = _sigmoid(_dot(zb, wg_ref[...], NN) + bg_ref[...])
        dout = dys_ref[...].astype(F32)
        dt = dout * z * gate * (1.0 - gate)
        dtb = dt.astype(BF16)
        dz = dout * gate + _dot(dtb, wg_ref[...], NT)
        dy = dz * _gelu_grad(yv)
        dyv[...] = dy
        dwg_ref[...] += _dot(zb, dtb, TN)
        vec_ref[0:1, :] += jnp.sum(dt, axis=0, keepdims=True)
        vec_ref[1:2, :] += jnp.sum(dy * u_ref[...].astype(F32), axis=0, keepdims=True)

        fcarry[...] = cs_ref[0]
        xs[0:SUBLANES, :] = jnp.broadcast_to(cs_ref[0], (SUBLANES, 2 * ns2))
        for kb in range(nkb):
            bu = _dot(u_ref[:, kb * LANES:(kb + 1) * LANES], b_ref[kb], NN)
            xs[SUBLANES:, kb * half:(kb + 1) * half] = bu[:, :half]
            xs[SUBLANES:, ns2 + kb * half:ns2 + (kb + 1) * half] = bu[:, half:]
        _scan_rows(xs, SUBLANES, ng, ns2, tabf_ref, fcarry, reverse=False)

        for kb in range(nkb):
            dyk = dyv[:, kb * LANES:(kb + 1) * LANES].astype(BF16)
            re = slice(kb * half, (kb + 1) * half)
            im = slice(ns2 + kb * half, ns2 + (kb + 1) * half)
            gs[:, re] = _dot(dyk, c_ref[kb, :half, :], NT)
            gs[:, im] = _dot(dyk, c_ref[kb, half:, :], NT)
            dc_ref[kb, :half, :] += _dot(xs[SUBLANES:, re].astype(BF16), dyk, TN)
            dc_ref[kb, half:, :] += _dot(xs[SUBLANES:, im].astype(BF16), dyk, TN)

        row_is_first = lax.broadcasted_iota(jnp.int32, (SUBLANES, min(S5_CHUNK, ns2)), 0) == 0

        def fold(c0, r0, gr, gi, acc):
            wc = min(S5_CHUNK, ns2)
            re = slice(c0, c0 + wc)
            im = slice(ns2 + c0, ns2 + c0 + wc)
            if r0 is None:
                da_ref[:, re] += acc[0]
                da_ref[:, im] += acc[1]
                return acc
            cur_r = xs[pl.ds(r0 + SUBLANES, SUBLANES), re]
            cur_i = xs[pl.ds(r0 + SUBLANES, SUBLANES), im]
            prv_r = xs[pl.ds(r0, SUBLANES), re]
            prv_i = xs[pl.ds(r0, SUBLANES), im]
            xpr = jnp.where(row_is_first, prv_r[SUBLANES - 1:SUBLANES, :], pltpu.roll(cur_r, 1, 0))
            xpi = jnp.where(row_is_first, prv_i[SUBLANES - 1:SUBLANES, :], pltpu.roll(cur_i, 1, 0))
            return acc[0] + gr * xpr + gi * xpi, acc[1] - gr * xpi + gi * xpr

        zero2 = lambda wc: (jnp.zeros((SUBLANES, wc), F32), jnp.zeros((SUBLANES, wc), F32))
        _scan_rows(gs, 0, ng, ns2, tabr_ref, gcarry, reverse=True, after_group=fold, extra_init=zero2)

        for kb in range(nkb):
            cols = slice(kb * LANES, (kb + 1) * LANES)
            re = slice(kb * half, (kb + 1) * half)
            im = slice(ns2 + kb * half, ns2 + (kb + 1) * half)
            uk = u_ref[:, cols]
            gr = gs[:, re].astype(BF16)
            gi = gs[:, im].astype(BF16)
            db_ref[kb, :, :half] += _dot(uk, gr, TN)
            db_ref[kb, :, half:] += _dot(uk, gi, TN)
            duk = _dot(gr, b_ref[kb, :, :half], NT) + _dot(gi, b_ref[kb, :, half:], NT)
            du_ref[:, cols] = (duk + ds_ref[:, cols] * dyv[:, cols]).astype(BF16)

    rev = lambda i: (nblk - 1 - i, 0)
    row = pl.BlockSpec((t, w), rev)
    full = lambda shape: pl.BlockSpec(shape, lambda i: (0,) * len(shape))
    return pl.pallas_call(
        body, name=name, grid=(nblk,),
        in_specs=[row, row, row, pl.BlockSpec((1, 1, 2 * ns2), lambda i: (nblk - 1 - i, 0, 0)),
                  full(b_blk.shape), full(c_blk.shape), full(tab_f.shape), full(tab_r.shape),
                  full(dskip.shape), full(w_glu.shape), full(b_glu.shape)],
        out_specs=[row, full(b_blk.shape), full(c_blk.shape), full((SUBLANES, 2 * ns2)), full((w, w)),
                   full((SUBLANES, w))],
        out_shape=[jax.ShapeDtypeStruct((s, w), BF16), jax.ShapeDtypeStruct(b_blk.shape, F32),
                   jax.ShapeDtypeStruct(c_blk.shape, F32), jax.ShapeDtypeStruct((SUBLANES, 2 * ns2), F32),
                   jax.ShapeDtypeStruct((w, w), F32), jax.ShapeDtypeStruct((SUBLANES, w), F32)],
        scratch_shapes=[pltpu.VMEM((t + SUBLANES, 2 * ns2), F32), pltpu.VMEM((t, 2 * ns2), F32),
                        pltpu.VMEM((t, w), F32), pltpu.VMEM((1, 2 * ns2), F32), pltpu.VMEM((1, 2 * ns2), F32)],
        compiler_params=_cparams(),
    )(u, dys, y, carries, b_blk, c_blk, tab_f, tab_r, dskip, w_glu, b_glu)


def _log_sigmoid(x):
    return jnp.minimum(x, 0.0) - jnp.log(1.0 + jnp.exp(-jnp.abs(x)))


def _cum_fwd(name, f_t, b_f):
    h, s = f_t.shape
    tc = _pick(s, 512)
    nb = s // tc

    def body(f_ref, b_ref, c_ref, carry):
        @pl.when(pl.program_id(0) == 0)
        def _():
            carry[...] = jnp.zeros_like(carry)

        lf = _log_sigmoid(f_ref[...] + b_ref[...])
        upper = (lax.broadcasted_iota(jnp.int32, (tc, tc), 0) <= lax.broadcasted_iota(jnp.int32, (tc, tc), 1))
        cum = lax.dot_general(lf, upper.astype(F32), NN, precision=lax.Precision.HIGHEST,
                              preferred_element_type=F32) + carry[...]
        c_ref[...] = cum
        carry[...] += jnp.sum(lf, axis=1, keepdims=True)

    blk = pl.BlockSpec((h, tc), lambda i: (0, i))
    return pl.pallas_call(body, name=name, grid=(nb,), in_specs=[blk, pl.BlockSpec((h, 1), lambda i: (0, 0))],
                          out_specs=blk, out_shape=jax.ShapeDtypeStruct((h, s), F32),
                          scratch_shapes=[pltpu.VMEM((h, 1), F32)], compiler_params=_cparams())(f_t, b_f)


def _cum_bwd(name, dcq, dck, f_t, b_f):
    h, s = f_t.shape
    tc = _pick(s, 512)
    nb = s // tc

    def body(dcq_ref, dck_ref, f_ref, b_ref, df_ref, db_ref, carry):
        @pl.when(pl.program_id(0) == 0)
        def _():
            carry[...] = jnp.zeros_like(carry)
            db_ref[...] = jnp.zeros_like(db_ref)

        dc = dcq_ref[...] + dck_ref[...]
        lower = (lax.broadcasted_iota(jnp.int32, (tc, tc), 0) >= lax.broadcasted_iota(jnp.int32, (tc, tc), 1))
        dlf = lax.dot_general(dc, lower.astype(F32), NN, precision=lax.Precision.HIGHEST,
                              preferred_element_type=F32) + carry[...]
        carry[...] += jnp.sum(dc, axis=1, keepdims=True)
        df = dlf * _sigmoid(-(f_ref[...] + b_ref[...]))
        df_ref[...] = df
        db_ref[...] += jnp.broadcast_to(jnp.sum(df, axis=1, keepdims=True), db_ref.shape)

    blk = pl.BlockSpec((h, tc), lambda i: (0, nb - 1 - i))
    return pl.pallas_call(
        body, name=name, grid=(nb,), in_specs=[blk, blk, blk, pl.BlockSpec((h, 1), lambda i: (0, 0))],
        out_specs=[blk, pl.BlockSpec((h, LANES), lambda i: (0, 0))],
        out_shape=[jax.ShapeDtypeStruct((h, s), F32), jax.ShapeDtypeStruct((h, LANES), F32)],
        scratch_shapes=[pltpu.VMEM((h, 1), F32)], compiler_params=_cparams())(dcq, dck, f_t, b_f)


def _attn_fwd(name, qkv, q_blk, k_blk, v_blk, n_pairs, ck):
    s = qkv.shape[0]
    dh = LANES // 2
    t = min(ATT_BLOCK, s)
    nq = s // t
    scale = dh ** -0.5

    def body(q_ref, k_ref, v_ref, ck_ref, o_ref, lse_ref, m_s, acc_s):
        i = pl.program_id(1)
        low = lax.broadcasted_iota(jnp.int32, (1, LANES), 1) < dh
        qs = (q_ref[...].astype(F32) * scale).astype(BF16)
        zero = jnp.zeros_like(qs)
        qh = (jnp.where(low, qs, zero), jnp.where(low, zero, qs))
        m_s[...] = jnp.full(m_s.shape, -1e30, F32)
        acc_s[...] = jnp.zeros_like(acc_s)
        causal = (lax.broadcasted_iota(jnp.int32, (t, t), 1) <= lax.broadcasted_iota(jnp.int32, (t, t), 0))

        def step(j, diagonal):
            r0 = pl.multiple_of(j * t, t)
            kj = k_ref[pl.ds(r0, t), :]
            vj = v_ref[pl.ds(r0, t), :]
            one = jnp.ones_like(vj)
            vh = (jnp.where(low, vj, one), jnp.where(low, one, vj))
            for hd in range(2):
                sc = _dot(qh[hd], kj, NT) - ck_ref[hd, j]
                if diagonal:
                    sc = jnp.where(causal, sc, -1e30)
                m_old = m_s[hd]
                m_new = jnp.maximum(m_old, jnp.max(sc, axis=1, keepdims=True))
                p = jnp.exp(sc - m_new)
                acc_s[hd] = jnp.exp(m_old - m_new) * acc_s[hd] + _dot(p.astype(BF16), vh[hd], NN)
                m_s[hd] = m_new

        def full(j, _):
            step(j, False)
            return 0

        lax.fori_loop(0, i, full, 0)
        step(i, True)
        a0, a1 = acc_s[0], acc_s[1]
        o_ref[...] = jnp.where(low, a0 / pltpu.roll(a0, dh, 1), a1 / pltpu.roll(a1, dh, 1)).astype(BF16)
        lse_ref[0] = m_s[0] + jnp.log(a0[:, dh:dh + 1])
        lse_ref[1] = m_s[1] + jnp.log(a1[:, 0:1])

    return pl.pallas_call(
        body, name=name, grid=(n_pairs, nq),
        in_specs=[pl.BlockSpec((t, LANES), lambda hp, i: (i, q_blk + hp)),
                  pl.BlockSpec((s, LANES), lambda hp, i: (0, k_blk + hp)),
                  pl.BlockSpec((s, LANES), lambda hp, i: (0, v_blk + hp)),
                  pl.BlockSpec((2, nq, 1, t), lambda hp, i: (hp, 0, 0, 0))],
        out_specs=[pl.BlockSpec((t, LANES), lambda hp, i: (i, hp)), pl.BlockSpec((2, t, 1), lambda hp, i: (hp, i, 0))],
        out_shape=[jax.ShapeDtypeStruct((s, LANES * n_pairs), BF16), jax.ShapeDtypeStruct((2 * n_pairs, s, 1), F32)],
        scratch_shapes=[pltpu.VMEM((2, t, 1), F32), pltpu.VMEM((2, t, LANES), F32)],
        compiler_params=_cparams(),
    )(qkv, qkv, qkv, ck)


def _attn_bwd(name, qkv, q_blk, k_blk, v_blk, n_pairs, o, do, lse_rows, ck_cols):
    s = qkv.shape[0]
    dh = LANES // 2
    t = min(ATT_BLOCK, s)
    nk = s // t
    scale = dh ** -0.5

    def body(q_ref, k_ref, v_ref, o_ref, do_ref, lse_ref, ck_ref,
             dq_ref, dk_ref, dv_ref, dcq_ref, dck_ref, delta, dqt, dk_acc, dv_acc):
        j = pl.program_id(1)
        low = lax.broadcasted_iota(jnp.int32, (1, LANES), 1) < dh
        low_rows = lax.broadcasted_iota(jnp.int32, (LANES, 1), 0) < dh

        @pl.when(j == 0)
        def _():
            dqt[...] = jnp.zeros_like(dqt)
            sel = (jnp.broadcast_to(low, (SUBLANES, LANES)).astype(F32), jnp.broadcast_to(~low, (SUBLANES, LANES)).astype(F32))

            def fill(i, _):
                r0 = pl.multiple_of(i * t, t)
                prod = do_ref[pl.ds(r0, t), :].astype(F32) * o_ref[pl.ds(r0, t), :].astype(F32)
                for hd in range(2):
                    delta[hd, i] = lax.dot_general(sel[hd], prod, NT, precision=lax.Precision.HIGHEST,
                                                   preferred_element_type=F32)
                return 0

            lax.fori_loop(0, nk, fill, 0)

        kj, vj = k_ref[...], v_ref[...]
        zero, one = jnp.zeros_like(kj), jnp.ones_like(kj)
        kh = (jnp.where(low, kj, zero), jnp.where(low, zero, kj))
        vh = (jnp.where(low, vj, zero), jnp.where(low, zero, vj))
        kjt = kj.astype(F32).T.astype(BF16)
        one_t = jnp.ones_like(kjt)
        kht = (jnp.where(low_rows, kjt, one_t), jnp.where(low_rows, one_t, kjt))
        dk_acc[...] = jnp.zeros_like(dk_acc)
        dv_acc[...] = jnp.zeros_like(dv_acc)
        causal_t = (lax.broadcasted_iota(jnp.int32, (t, t), 0) <= lax.broadcasted_iota(jnp.int32, (t, t), 1))

        def step(i, diagonal):
            r0 = pl.multiple_of(i * t, t)
            qi = (q_ref[pl.ds(r0, t), :].astype(F32) * scale).astype(BF16)
            doi = do_ref[pl.ds(r0, t), :]
            qone, dzero = jnp.ones_like(qi), jnp.zeros_like(doi)
            qsel = (jnp.where(low, qi, qone), jnp.where(low, qone, qi))
            dosel = (jnp.where(low, doi, dzero), jnp.where(low, dzero, doi))
            for hd in range(2):
                st = _dot(kh[hd], qi, NT) - ck_ref[hd] - lse_ref[hd, i]
                pt = jnp.exp(st)
                if diagonal:
                    pt = jnp.where(causal_t, pt, 0.0)
                dst = pt * (_dot(vh[hd], doi, NT) - delta[hd, i, 0:1, :])
                dsb = dst.astype(BF16)
                dv_acc[...] += _dot(pt.astype(BF16), dosel[hd], NN)
                dk_acc[hd] += _dot(dsb, qsel[hd], NN)
                dqt[hd, i] += _dot(kht[hd], dsb, NN)

        step(j, True)

        def rest(i, _):
            step(i, False)
            return 0

        lax.fori_loop(j + 1, nk, rest, 0)
        dk_ref[...] = jnp.where(low, dk_acc[0], dk_acc[1]).astype(BF16)
        dv_ref[...] = dv_acc[...].astype(BF16)
        dck_ref[0] = -dk_acc[0][:, dh:dh + 1]
        dck_ref[1] = -dk_acc[1][:, 0:1]

        @pl.when(j == nk - 1)
        def _():
            def emit(i, _):
                r0 = pl.multiple_of(i * t, t)
                d0, d1 = dqt[0, i], dqt[1, i]
                dq_ref[pl.ds(r0, t), :] = (jnp.where(low_rows, d0, d1) * scale).T.astype(BF16)
                dcq_ref[0, i] = d0[dh:dh + 1, :]
                dcq_ref[1, i] = d1[0:1, :]
                return 0

            lax.fori_loop(0, nk, emit, 0)

    col_blk = lambda base: pl.BlockSpec((t, LANES), lambda hp, j: (j, base + hp))
    col_all = lambda base: pl.BlockSpec((s, LANES), lambda hp, j: (0, base + hp))
    rows_all = pl.BlockSpec((2, nk, 1, t), lambda hp, j: (hp, 0, 0, 0))
    return pl.pallas_call(
        body, name=name, grid=(n_pairs, nk),
        in_specs=[col_all(q_blk), col_blk(k_blk), col_blk(v_blk), col_all(0), col_all(0), rows_all,
                  pl.BlockSpec((2, t, 1), lambda hp, j: (hp, j, 0))],
        out_specs=[col_all(0), col_blk(0), col_blk(0), rows_all, pl.BlockSpec((2, t, 1), lambda hp, j: (hp, j, 0))],
        out_shape=[jax.ShapeDtypeStruct((s, LANES * n_pairs), BF16), jax.ShapeDtypeStruct((s, LANES * n_pairs), BF16),
                   jax.ShapeDtypeStruct((s, LANES * n_pairs), BF16), jax.ShapeDtypeStruct((2 * n_pairs, nk, 1, t), F32),
                   jax.ShapeDtypeStruct((2 * n_pairs, s, 1), F32)],
        scratch_shapes=[pltpu.VMEM((2, nk, SUBLANES, t), F32), pltpu.VMEM((2, nk, LANES, t), F32),
                        pltpu.VMEM((2, t, LANES), F32), pltpu.VMEM((t, LANES), F32)],
        compiler_params=_cparams(),
    )(qkv, qkv, qkv, o, do, lse_rows, ck_cols)


def _adamw(name, w, g, m, v):
    r, c = w.shape
    tr = _pick8(r, max(SUBLANES, (1 << 20) // (4 * c)))

    def body(w_ref, g_ref, m_ref, v_ref, d_ref, mo_ref, vo_ref):
        gv = g_ref[...]
        m2 = ADAM_B1 * m_ref[...] + (1.0 - ADAM_B1) * gv
        v2 = ADAM_B2 * v_ref[...] + (1.0 - ADAM_B2) * (gv * gv)
        m_hat = m2 / (1.0 - ADAM_B1 ** ADAM_STEP)
        v_hat = v2 / (1.0 - ADAM_B2 ** ADAM_STEP)
        d_ref[...] = -ADAM_LR * (m_hat / (jnp.sqrt(v_hat) + ADAM_EPS) + ADAM_WD * w_ref[...])
        mo_ref[...] = m2
        vo_ref[...] = v2

    blk = pl.BlockSpec((tr, c), lambda i: (i, 0))
    sh = jax.ShapeDtypeStruct((r, c), F32)
    return pl.pallas_call(body, name=name, grid=(r // tr,), in_specs=[blk] * 4, out_specs=[blk] * 3,
                          out_shape=[sh, sh, sh], compiler_params=_cparams())(w, g, m, v)


def _pick8(dim, target):
    best, t = None, SUBLANES
    while t <= min(dim, target):
        if dim % t == 0:
            best = t
        t += SUBLANES
    return best or dim


def _sum_blocks(name, x, out_dtype):
    n, r, c = x.shape
    tr = _pick8(r, max(SUBLANES, (1 << 19) // (4 * c)))

    def body(x_ref, o_ref):
        acc = x_ref[0].astype(F32)
        for i in range(1, n):
            acc = acc + x_ref[i].astype(F32)
        o_ref[...] = acc.astype(out_dtype)

    return pl.pallas_call(body, name=name, grid=(r // tr,),
                          in_specs=[pl.BlockSpec((n, tr, c), lambda i: (0, i, 0))],
                          out_specs=pl.BlockSpec((tr, c), lambda i: (i, 0)),
                          out_shape=jax.ShapeDtypeStruct((r, c), out_dtype), compiler_params=_cparams())(x)


def _add_mine(name, gbuf, recv, core):
    n, _, r, c = gbuf.shape
    tr = _pick8(r, 256)

    def body(core_ref, g_ref, r_ref, o_ref):
        o_ref[...] = (g_ref[:, 0].astype(F32) + r_ref[...].astype(F32)).astype(BF16)

    grid_spec = pltpu.PrefetchScalarGridSpec(
        num_scalar_prefetch=1, grid=(r // tr,),
        in_specs=[pl.BlockSpec((n, 1, tr, c), lambda i, core_ref: (0, core_ref[0], i, 0)),
                  pl.BlockSpec((n, tr, c), lambda i, core_ref: (0, i, 0))],
        out_specs=pl.BlockSpec((n, tr, c), lambda i, core_ref: (0, i, 0)))
    return pl.pallas_call(body, name=name, grid_spec=grid_spec,
                          out_shape=jax.ShapeDtypeStruct((n, r, c), BF16), compiler_params=_cparams())(core, gbuf, recv)


def _all_gather(name, x_shard):
    m_per, n = x_shard.shape

    def body(x_ref, out_ref, send_sems, recv_sems):
        x, y, c = lax.axis_index("x"), lax.axis_index("y"), lax.axis_index("c")
        me, sibling = (x, y, c), (x, y, 1 - c)
        chips = [(1 - x, y), (x, 1 - y), (1 - x, 1 - y)]

        def rows(px, py, pc):
            return out_ref.at[pl.ds((4 * px + 2 * py + pc) * m_per, m_per), :]

        def copy(k, block, to, src=None):
            return pltpu.make_async_remote_copy(
                src_ref=rows(*block) if src is None else src, dst_ref=rows(*block),
                send_sem=send_sems.at[k], recv_sem=recv_sems.at[k], device_id=to, device_id_type=MESH)

        first = [copy(0, me, sibling, src=x_ref)]
        first += [copy(1 + j, me, (*chip, c), src=x_ref) for j, chip in enumerate(chips)]
        for cp in first:
            cp.start()
        passed = [copy(4 + j, (*chip, c), sibling) for j, chip in enumerate(chips)]
        for j, chip in enumerate(chips):
            copy(1 + j, (*chip, c), me).wait_recv()
            passed[j].start()
        copy(0, sibling, me).wait_recv()
        for j, chip in enumerate(chips):
            copy(4 + j, (*chip, 1 - c), me).wait_recv()
        for cp in first + passed:
            cp.wait_send()

    out = pl.pallas_call(
        body, name=name, out_shape=jax.ShapeDtypeStruct((N_DEV * m_per, n), x_shard.dtype),
        in_specs=[pl.BlockSpec(memory_space=pl.ANY)], out_specs=pl.BlockSpec(memory_space=pl.ANY),
        scratch_shapes=[pltpu.SemaphoreType.DMA((7,)), pltpu.SemaphoreType.DMA((7,))],
    )(x_shard)
    my_dev = 4 * lax.axis_index("x") + 2 * lax.axis_index("y") + lax.axis_index("c")
    return lax.dynamic_update_slice(out, x_shard, (my_dev * m_per, 0))


def _swap_halves(name, gbuf):
    n, _, r, c_ = gbuf.shape

    def body(g_ref, out_ref, send_sem, recv_sem):
        x, y, c = lax.axis_index("x"), lax.axis_index("y"), lax.axis_index("c")
        cp = pltpu.make_async_remote_copy(src_ref=g_ref.at[:, 1 - c], dst_ref=out_ref, send_sem=send_sem,
                                          recv_sem=recv_sem, device_id=(x, y, 1 - c), device_id_type=MESH)
        cp.start()
        cp.wait()

    return pl.pallas_call(
        body, name=name, out_shape=jax.ShapeDtypeStruct((n, r, c_), gbuf.dtype),
        in_specs=[pl.BlockSpec(memory_space=pl.ANY)], out_specs=pl.BlockSpec(memory_space=pl.ANY),
        scratch_shapes=[pltpu.SemaphoreType.DMA, pltpu.SemaphoreType.DMA],
    )(gbuf)


def _chip_exchange(name, part):
    n, r, c_ = part.shape

    def body(p_ref, out_ref, send_sems, recv_sems):
        x, y, c = lax.axis_index("x"), lax.axis_index("y"), lax.axis_index("c")
        my_chip = 2 * x + y
        chips = [(1 - x, y), (x, 1 - y), (1 - x, 1 - y)]
        copies = [pltpu.make_async_remote_copy(
            src_ref=p_ref.at[2 * cx + cy], dst_ref=out_ref.at[my_chip], send_sem=send_sems.at[j],
            recv_sem=recv_sems.at[j], device_id=(cx, cy, c), device_id_type=MESH)
            for j, (cx, cy) in enumerate(chips)]
        for cp in copies:
            cp.start()
        for cp in copies:
            cp.wait()

    out = pl.pallas_call(
        body, name=name, out_shape=jax.ShapeDtypeStruct((n, r, c_), part.dtype),
        in_specs=[pl.BlockSpec(memory_space=pl.ANY)], out_specs=pl.BlockSpec(memory_space=pl.ANY),
        scratch_shapes=[pltpu.SemaphoreType.DMA((3,)), pltpu.SemaphoreType.DMA((3,))],
    )(part)
    my_chip = 2 * lax.axis_index("x") + lax.axis_index("y")
    return lax.dynamic_update_slice(out, lax.dynamic_slice_in_dim(part, my_chip, 1, axis=0), (my_chip, 0, 0))


def _share_halves(name, half):
    r, c_ = half.shape

    def body(h_ref, out_ref, send_sem, recv_sem):
        x, y, c = lax.axis_index("x"), lax.axis_index("y"), lax.axis_index("c")
        cp = pltpu.make_async_remote_copy(src_ref=h_ref, dst_ref=out_ref.at[c], send_sem=send_sem,
                                          recv_sem=recv_sem, device_id=(x, y, 1 - c), device_id_type=MESH)
        cp.start()
        cp.wait()

    out = pl.pallas_call(
        body, name=name, out_shape=jax.ShapeDtypeStruct((2, r, c_), half.dtype),
        in_specs=[pl.BlockSpec(memory_space=pl.ANY)], out_specs=pl.BlockSpec(memory_space=pl.ANY),
        scratch_shapes=[pltpu.SemaphoreType.DMA, pltpu.SemaphoreType.DMA],
    )(half)
    return lax.dynamic_update_slice(out, half[None], (lax.axis_index("c"), 0, 0))


def _pack(arrays, cols, row_multiple, dtype):
    flat = jnp.concatenate([a.reshape(-1).astype(dtype) for a in arrays])
    unit = cols * row_multiple
    total = -(-flat.shape[0] // unit) * unit
    return jnp.pad(flat, (0, total - flat.shape[0])).reshape(total // cols, cols)


def _unpack(buf, shapes):
    flat, out, off = buf.reshape(-1), [], 0
    for sh in shapes:
        n = math.prod(sh)
        out.append(flat[off:off + n].reshape(sh))
        off += n
    return out


def _discretize(lam_re, lam_im, log_dt, b_re, b_im):
    lam = lax.complex(jnp.minimum(lam_re, -EIG_CLIP), lam_im)
    dt = jnp.exp(log_dt)[:, None]
    lam_bar = jnp.exp(lam * dt)
    b_bar = ((lam_bar - 1.0) / lam)[..., None] * lax.complex(b_re, b_im)
    return jnp.real(lam_bar), jnp.imag(lam_bar), jnp.real(b_bar), jnp.imag(b_bar)


def _scan_tables(ar, ai):
    a = lax.complex(ar, ai)
    pw = [a]
    for _ in range(7):
        pw.append(pw[-1] * a)
    rows = jnp.arange(SUBLANES)[:, None]

    def build(p, reverse):
        tabs = []
        for k in (1, 2, 4):
            keep = (rows <= SUBLANES - 1 - k) if reverse else (rows >= k)
            tk = jnp.where(keep, p[k - 1][None, :], 0.0)
            tabs += [jnp.real(tk), jnp.imag(tk)]
        stack = jnp.stack(p[::-1] if reverse else p)
        tabs += [jnp.real(stack), jnp.imag(stack)]
        return jnp.stack(tabs).astype(F32)

    return build(pw, False), build([jnp.conj(p) for p in pw], True)


def _block_diag(per_group, groups_per_block):
    g, a, b = per_group.shape
    x = per_group.reshape(g // groups_per_block, groups_per_block, a, b)
    eye = jnp.eye(groups_per_block, dtype=per_group.dtype)
    out = x[:, :, :, None, :] * eye[None, :, None, :, None]
    return out.reshape(g // groups_per_block, groups_per_block * a, groups_per_block * b)


def _block_diag_extract(dense, groups_per_block, a, b):
    nkb = dense.shape[0]
    x = dense.reshape(nkb, groups_per_block, a, groups_per_block, b)
    idx = jnp.arange(groups_per_block)
    return x[:, idx, :, idx, :].transpose(1, 0, 2, 3).reshape(nkb * groups_per_block, a, b)


def _interleave(a, b, tile):
    k, f = a.shape
    return jnp.stack([a.reshape(k, f // tile, tile), b.reshape(k, f // tile, tile)], axis=2).reshape(k, 2 * f)


def _deinterleave(ab, tile):
    k, f2 = ab.shape
    x = ab.reshape(k, f2 // (2 * tile), 2, tile)
    return x[:, :, 0, :].reshape(k, f2 // 2), x[:, :, 1, :].reshape(k, f2 // 2)


def _layer_fwd(tag, x, mod, p):
    s, d = x.shape
    w_ssm, w_att = p["w_glu"].shape[0], p["w_pb"].shape[0]
    heads = p["b_f"].shape[0]
    dh = w_att // heads
    row = lambda v: v.reshape(1, -1)
    sv = {}

    h = _prenorm_fwd(f"prenorm_mix_{tag}", x, row(p["g_pre_mix"]), row(mod[1]), row(mod[0]))
    uqkv = _mm_plain(f"proj_main_{tag}", h, p["w_main"], "nn", BF16, tn=1024)
    fg = _mm_plain(f"proj_gate_{tag}", h, p["w_gates"], "nn", F32, tn=1024)
    u = uqkv[:, :w_ssm]
    f_t = fg[:, :heads].T
    g_a, g_b = fg[:, F_PAD:F_PAD + d], fg[:, F_PAD + d:]

    y_s5, ys, carries = _s5_fwd(f"s5_fwd_{tag}", u, p["b_blk"], p["c_blk"], p["tab_f"], row(p["d_skip"]),
                                p["w_glu"], row(p["b_glu"]))

    assert dh * 2 == LANES and w_ssm % LANES == 0 and w_att % LANES == 0
    n_pairs = w_att // LANES
    blocks = (w_ssm // LANES, w_ssm // LANES + n_pairs, w_ssm // LANES + 2 * n_pairs)
    cum = _cum_fwd(f"cum_fwd_{tag}", f_t, p["b_f"].reshape(heads, 1))
    t = min(ATT_BLOCK, s)
    ck_cols, ck_rows = cum.reshape(heads, s, 1), cum.reshape(heads, s // t, 1, t)
    ya, lse = _attn_fwd(f"attn_fwd_{tag}", uqkv, *blocks, n_pairs, ck_rows)

    tile = pl.BlockSpec((_pick(s, 512), _pick(d, 512)), lambda i, j, k_: (i, j))

    def merge(acc, extra_refs, out_refs):
        ya_ref, wpb_ref, ga_ref, gb_ref = extra_refs
        a_ref, b_ref, m_ref = out_refs
        bv = _dot(ya_ref[...], wpb_ref[...], NN)
        a_ref[...] = acc.astype(BF16)
        b_ref[...] = bv.astype(BF16)
        m_ref[...] = (_sigmoid(ga_ref[...]) * acc + _sigmoid(gb_ref[...]) * bv).astype(BF16)

    sd_bf = jax.ShapeDtypeStruct((s, d), BF16)
    (pa, pb, merged), _ = _mm(
        f"merge_{tag}", ys, p["w_pa"], "nn", [sd_bf] * 3, [tile] * 3, merge,
        extra=(ya, p["w_pb"], g_a, g_b),
        extra_specs=[pl.BlockSpec((_pick(s, 512), w_att), lambda i, j, k_: (i, 0)),
                     pl.BlockSpec((w_att, _pick(d, 512)), lambda i, j, k_: (0, j)), tile, tile],
        tk=w_ssm)

    x1, y_mix = _mm_postnorm(f"out_proj_{tag}", merged, p["w_o"], x, row(mod[2]), row(p["g_post_mix"]))

    h2 = _prenorm_fwd(f"prenorm_ffn_{tag}", x1, row(p["g_pre_ffn"]), row(mod[4]), row(mod[3]))
    f_dim = p["w_down"].shape[0]
    tm = _pick(s, 512)

    def swiglu(acc, extra_refs, out_refs):
        ab_ref, hid_ref = out_refs
        av, bv = acc[:, :FFN_TILE], acc[:, FFN_TILE:]
        ab_ref[...] = acc.astype(BF16)
        hid_ref[...] = (av * _sigmoid(av) * bv).astype(BF16)

    (ab, hidden), _ = _mm(
        f"ffn_up_{tag}", h2, p["w_gu"], "nn",
        [jax.ShapeDtypeStruct((s, 2 * f_dim), BF16), jax.ShapeDtypeStruct((s, f_dim), BF16)],
        [pl.BlockSpec((tm, 2 * FFN_TILE), lambda i, j, k_: (i, j)), pl.BlockSpec((tm, FFN_TILE), lambda i, j, k_: (i, j))],
        swiglu, tm=tm, tn=2 * FFN_TILE, tk=1024)
    x2, y_ffn = _mm_postnorm(f"ffn_down_{tag}", hidden, p["w_down"], x1, row(mod[5]), row(p["g_post_ffn"]))

    sv.update(x=x, h=h, uqkv=uqkv, f_t=f_t, g_a=g_a, g_b=g_b, y_s5=y_s5, ys=ys, carries=carries, blocks=blocks,
              ck_cols=ck_cols, lse_rows=lse.reshape(heads, s // t, 1, t), ya=ya, pa=pa, pb=pb, merged=merged, x1=x1, y_mix=y_mix, h2=h2, ab=ab,
              hidden=hidden, y_ffn=y_ffn)
    return x2, sv


def _mm_postnorm(name, a, w, x, gate, g):
    s, d = x.shape
    tm = _pick(s, 256)
    rowspec = pl.BlockSpec((tm, d), lambda i, j, k: (i, 0))
    vec = pl.BlockSpec((1, d), lambda i, j, k: (0, 0))

    def epilogue(acc, extra_refs, out_refs):
        x_ref, gate_ref, g_ref = extra_refs
        r = lax.rsqrt(jnp.mean(acc * acc, axis=-1, keepdims=True) + RMS_EPS)
        out_refs[0][...] = x_ref[...] + gate_ref[...] * (acc * r * g_ref[...])
        out_refs[1][...] = acc

    sd = jax.ShapeDtypeStruct((s, d), F32)
    (xn, y), _ = _mm(name, a, w, "nn", [sd, sd], [rowspec, rowspec], epilogue, extra=(x, gate, g),
                     extra_specs=[rowspec, vec, vec], tm=tm, tn=d, tk=1536)
    return xn, y


def _layer_bwd(tag, dx2, mod, p, sv):
    s, d = dx2.shape
    w_ssm, w_att = p["w_glu"].shape[0], p["w_pb"].shape[0]
    heads = p["b_f"].shape[0]
    dh = w_att // heads
    row = lambda v: v.reshape(1, -1)
    gr = {}

    dy_ffn, sums = _postnorm_bwd(f"postnorm_bwd_ffn_{tag}", dx2, sv["y_ffn"], row(p["g_post_ffn"]), row(mod[5]))
    d_gate_f, gr["g_post_ffn"] = sums[0], sums[1]
    gr["w_down"] = _mm_plain(f"dw_down_{tag}", sv["hidden"], dy_ffn, "tn", BF16, tk=1024)
    tm = _pick(s, 512)

    def swiglu_bwd(acc, extra_refs, out_refs):
        abv = extra_refs[0][...].astype(F32)
        av, bv = abv[:, :FFN_TILE], abv[:, FFN_TILE:]
        sg = _sigmoid(av)
        da = acc * bv * (sg * (1.0 + av * (1.0 - sg)))
        db = acc * (av * sg)
        out_refs[0][:, :FFN_TILE] = da.astype(BF16)
        out_refs[0][:, FFN_TILE:] = db.astype(BF16)

    ab_spec = pl.BlockSpec((tm, 2 * FFN_TILE), lambda i, j, k_: (i, j))
    (dab,), _ = _mm(f"ffn_down_bwd_{tag}", dy_ffn, p["w_down"], "nt",
                    [jax.ShapeDtypeStruct(sv["ab"].shape, BF16)], [ab_spec], swiglu_bwd,
                    extra=(sv["ab"],), extra_specs=[ab_spec], tm=tm, tn=FFN_TILE, tk=1024)
    gr["w_gu"] = _mm_plain(f"dw_gu_{tag}", sv["h2"], dab, "tn", BF16, tk=1024)
    dh2 = _mm_plain(f"dh_ffn_{tag}", dab, p["w_gu"], "nt", F32, tn=1024, tk=1024)
    dx1, sums = _prenorm_bwd(f"prenorm_bwd_ffn_{tag}", dh2, sv["x1"], row(p["g_pre_ffn"]), row(mod[4]), dx2)
    d_scale_f, d_shift_f, gr["g_pre_ffn"] = sums[0], sums[1], sums[2]

    dy_mix, sums = _postnorm_bwd(f"postnorm_bwd_mix_{tag}", dx1, sv["y_mix"], row(p["g_post_mix"]), row(mod[2]))
    d_gate_m, gr["g_post_mix"] = sums[0], sums[1]
    gr["w_o"] = _mm_plain(f"dw_o_{tag}", sv["merged"], dy_mix, "tn", BF16, tk=1024)

    tile = pl.BlockSpec((_pick(s, 512), _pick(d, 512)), lambda i, j, k_: (i, j))

    def merge_bwd(acc, extra_refs, out_refs):
        a_ref, b_ref, ga_ref, gb_ref = extra_refs
        sa, sb = _sigmoid(ga_ref[...]), _sigmoid(gb_ref[...])
        out_refs[0][...] = (acc * sa).astype(BF16)
        out_refs[1][...] = (acc * sb).astype(BF16)
        out_refs[2][...] = (acc * a_ref[...].astype(F32) * sa * (1.0 - sa)).astype(BF16)
        out_refs[3][...] = (acc * b_ref[...].astype(F32) * sb * (1.0 - sb)).astype(BF16)

    sd_bf = jax.ShapeDtypeStruct((s, d), BF16)
    (d_pa, d_pb, d_ga, d_gb), _ = _mm(f"out_proj_bwd_{tag}", dy_mix, p["w_o"], "nt", [sd_bf] * 4, [tile] * 4, merge_bwd,
                                      extra=(sv["pa"], sv["pb"], sv["g_a"], sv["g_b"]), extra_specs=[tile] * 4, tk=1024)
    gr["w_pa"] = _mm_plain(f"dw_pa_{tag}", sv["ys"], d_pa, "tn", BF16, tk=1024)
    gr["w_pb"] = _mm_plain(f"dw_pb_{tag}", sv["ya"], d_pb, "tn", BF16, tk=1024)
    d_ys = _mm_plain(f"d_ys_{tag}", d_pa, p["w_pa"], "nt", BF16, tk=1024)
    d_ya = _mm_plain(f"d_ya_{tag}", d_pb, p["w_pb"], "nt", BF16, tk=1024)

    dq, dk, dv, dcq, dck = _attn_bwd(f"attn_bwd_{tag}", sv["uqkv"], *sv["blocks"], w_att // LANES, sv["ya"], d_ya,
                                     sv["lse_rows"], sv["ck_cols"])
    d_f_t, d_bf = _cum_bwd(f"cum_bwd_{tag}", dcq.reshape(heads, s), dck.reshape(heads, s), sv["f_t"],
                           p["b_f"].reshape(heads, 1))
    gr["b_f"] = d_bf[:, 0]

    u = sv["uqkv"][:, :w_ssm]
    du, d_bblk, d_cblk, d_abar, gr["w_glu"], vec = _s5_bwd(
        f"s5_bwd_{tag}", u, d_ys, sv["y_s5"], sv["carries"], p["b_blk"], p["c_blk"], p["tab_f"], p["tab_r"],
        row(p["d_skip"]), p["w_glu"], row(p["b_glu"]))
    gr["b_glu"], gr["d_skip"] = vec[0], vec[1]
    gr["b_blk"], gr["c_blk"], gr["a_bar"] = d_bblk, d_cblk, d_abar

    d_main = jnp.concatenate([du, dq, dk, dv], axis=1)
    d_f = jnp.pad(d_f_t.T, ((0, 0), (0, F_PAD - heads))).astype(BF16)
    d_gates = jnp.concatenate([d_f, d_ga, d_gb], axis=1)
    gr["w_main"] = _mm_plain(f"dw_main_{tag}", sv["h"], d_main, "tn", BF16, tk=1024)
    gr["w_gates"] = _mm_plain(f"dw_gates_{tag}", sv["h"], d_gates, "tn", BF16, tk=1024)
    dh_a = _mm_plain(f"dh_main_{tag}", d_main, p["w_main"], "nt", F32, tn=1024, tk=1024)
    dh1 = _mm_plain(f"dh_gates_{tag}", d_gates, p["w_gates"], "nt", F32, add=dh_a, tn=1024, tk=1024)
    dx0, sums = _prenorm_bwd(f"prenorm_bwd_mix_{tag}", dh1, sv["x"], row(p["g_pre_mix"]), row(mod[1]), dx1)
    d_scale_m, d_shift_m, gr["g_pre_mix"] = sums[0], sums[1], sums[2]

    d_mod = jnp.stack([d_shift_m, d_scale_m, d_gate_m, d_shift_f, d_scale_f, d_gate_f])
    return dx0, d_mod, gr


BIG = ("w_in", "w_glu", "w_pa", "w_pb", "w_o", "w_ffn_gate", "w_ffn_up", "w_ffn_down")
BIG_SHARD_AXIS = {"w_in": 2, "w_glu": 1, "w_pa": 2, "w_pb": 2, "w_o": 1, "w_ffn_gate": 2, "w_ffn_up": 2, "w_ffn_down": 1}
SMALL = ("b_ada", "g_pre_mix", "g_post_mix", "g_pre_ffn", "g_post_ffn", "lam_re", "lam_im", "log_dt", "b_re", "b_im",
         "c_re", "c_im", "d_skip", "b_glu", "b_f")
WEIGHTS = ("w_ada", "b_ada", "g_pre_mix", "g_post_mix", "g_pre_ffn", "g_post_ffn", "w_in", "lam_re", "lam_im", "log_dt",
           "b_re", "b_im", "c_re", "c_im", "d_skip", "w_glu", "b_glu", "b_f", "w_pa", "w_pb", "w_o", "w_ffn_gate",
           "w_ffn_up", "w_ffn_down")


def _prepare_layer(full, small, l):
    w_in = full["w_in"][l]
    d = w_in.shape[0]
    heads = small["b_f"].shape[1]
    n_groups, n_state, group_ch = small["b_re"].shape[1:]
    w_ssm = n_groups * group_ch
    w_att = full["w_pb"].shape[1]
    n_main = w_ssm + 3 * w_att
    gpb = LANES // group_ch
    p = {}
    p["w_main"] = w_in[:, :n_main]
    p["w_gates"] = jnp.concatenate(
        [w_in[:, n_main:n_main + heads], jnp.zeros((d, F_PAD - heads), BF16), w_in[:, n_main + heads:]], axis=1)
    p["w_glu"], p["w_pa"], p["w_pb"], p["w_o"] = (full[n][l] for n in ("w_glu", "w_pa", "w_pb", "w_o"))
    p["w_gu"] = _interleave(full["w_ffn_gate"][l], full["w_ffn_up"][l], FFN_TILE)
    p["w_down"] = full["w_ffn_down"][l]
    for n in ("g_pre_mix", "g_post_mix", "g_pre_ffn", "g_post_ffn", "d_skip", "b_glu", "b_f"):
        p[n] = small[n][l]
    ar, ai, br, bi = _discretize(small["lam_re"][l], small["lam_im"][l], small["log_dt"][l], small["b_re"][l], small["b_im"][l])
    p["tab_f"], p["tab_r"] = _scan_tables(ar.reshape(-1), ai.reshape(-1))
    bre = _block_diag(br.transpose(0, 2, 1), gpb)
    bim = _block_diag(bi.transpose(0, 2, 1), gpb)
    p["b_blk"] = jnp.concatenate([bre, bim], axis=2).astype(BF16)
    cre = _block_diag(small["c_re"][l].transpose(0, 2, 1), gpb)
    cim = _block_diag(small["c_im"][l].transpose(0, 2, 1), gpb)
    p["c_blk"] = jnp.concatenate([cre, -cim], axis=1).astype(BF16)
    return p


def _compact_partials(gr, n_state, group_ch):
    gpb = LANES // group_ch
    half = gpb * n_state
    out = dict(gr)
    out["bbar_re"] = _block_diag_extract(gr["b_blk"][:, :, :half], gpb, group_ch, n_state).transpose(0, 2, 1)
    out["bbar_im"] = _block_diag_extract(gr["b_blk"][:, :, half:], gpb, group_ch, n_state).transpose(0, 2, 1)
    out["c_re"] = _block_diag_extract(gr["c_blk"][:, :half, :], gpb, n_state, group_ch).transpose(0, 2, 1)
    out["c_im"] = -_block_diag_extract(gr["c_blk"][:, half:, :], gpb, n_state, group_ch).transpose(0, 2, 1)
    return out


def _small_grads_from_partials(gr, small, l):
    n_groups, n_state, _ = small["b_re"].shape[1:]
    ns2 = n_groups * n_state
    d_abar = jnp.sum(gr["a_bar"], axis=0)
    dar, dai = d_abar[:ns2].reshape(n_groups, n_state), d_abar[ns2:].reshape(n_groups, n_state)
    args = (small["lam_re"][l], small["lam_im"][l], small["log_dt"][l], small["b_re"][l], small["b_im"][l])
    _, vjp = jax.vjp(_discretize, *args)
    d_lam_re, d_lam_im, d_log_dt, d_b_re, d_b_im = vjp((dar, dai, gr["bbar_re"], gr["bbar_im"]))
    return dict(lam_re=d_lam_re, lam_im=d_lam_im, log_dt=d_log_dt, b_re=d_b_re, b_im=d_b_im,
                c_re=gr["c_re"], c_im=gr["c_im"])


def _shard_of(a, axis, j):
    n = a.shape[axis] // N_CHIPS
    return lax.slice_in_dim(a, j * n, (j + 1) * n, axis=axis)


def _fwd_bwd(xs, target, mods, layers, heads):
    depth = len(layers)
    saved = []
    act = xs
    for l in range(depth):
        act, sv = _layer_fwd(str(l), act, mods[l], layers[l])
        saved.append(sv)
    dx, loss_blk = _loss_grad("loss", act, target)
    grads, d_mods = [None] * depth, [None] * depth
    for l in reversed(range(depth)):
        dx, d_mods[l], grads[l] = _layer_bwd(str(l), dx, mods[l], layers[l], saved[l])
    full_grads = {n: [] for n in BIG}
    for l in range(depth):
        g = grads[l]
        full_grads["w_in"].append(jnp.concatenate([g["w_main"], g["w_gates"][:, :heads], g["w_gates"][:, F_PAD:]], axis=1))
        dg, du_ = _deinterleave(g["w_gu"], FFN_TILE)
        full_grads["w_ffn_gate"].append(dg)
        full_grads["w_ffn_up"].append(du_)
        full_grads["w_ffn_down"].append(g["w_down"])
        full_grads["w_glu"].append(g["w_glu"].astype(BF16))
        for n in ("w_pa", "w_pb", "w_o"):
            full_grads[n].append(g[n])
    stacked = {n: jnp.stack(full_grads[n]) for n in BIG}
    return loss_blk, dx, d_mods, grads, stacked


def kernel(x, c, w_ada, b_ada, g_pre_mix, g_post_mix, g_pre_ffn, g_post_ffn, w_in, lam_re, lam_im, log_dt, b_re, b_im, c_re, c_im, d_skip, w_glu, b_glu, b_f, w_pa, w_pb, w_o, w_ffn_gate, w_ffn_up, w_ffn_down, loss_target, m_w_ada, m_b_ada, m_g_pre_mix, m_g_post_mix, m_g_pre_ffn, m_g_post_ffn, m_w_in, m_lam_re, m_lam_im, m_log_dt, m_b_re, m_b_im, m_c_re, m_c_im, m_d_skip, m_w_glu, m_b_glu, m_b_f, m_w_pa, m_w_pb, m_w_o, m_w_ffn_gate, m_w_ffn_up, m_w_ffn_down, v_w_ada, v_b_ada, v_g_pre_mix, v_g_post_mix, v_g_pre_ffn, v_g_post_ffn, v_w_in, v_lam_re, v_lam_im, v_log_dt, v_b_re, v_b_im, v_c_re, v_c_im, v_d_skip, v_w_glu, v_b_glu, v_b_f, v_w_pa, v_w_pb, v_w_o, v_w_ffn_gate, v_w_ffn_up, v_w_ffn_down):
    local = dict(locals())
    weights = {n: local[n] for n in WEIGHTS}
    moments_m = {n: local["m_" + n] for n in WEIGHTS}
    moments_v = {n: local["v_" + n] for n in WEIGHTS}
    depth, d = g_pre_mix.shape
    n_mod = w_ada.shape[2] * N_CHIPS // d
    mx, my, mc = lax.axis_index("x"), lax.axis_index("y"), lax.axis_index("c")
    my_chip = 2 * mx + my
    my_dev = 4 * mx + 2 * my + mc
    xs = x[0]

    shard_shapes = [weights[n].shape for n in BIG]
    wflat = _pack([weights[n] for n in BIG], COMM_LANES, 32, BF16)
    rh = wflat.shape[0] // 2
    mine = lax.dynamic_slice_in_dim(wflat, mc * rh, rh, axis=0)
    gathered = _all_gather("gather_weights", mine).reshape(N_CHIPS, 2 * rh, COMM_LANES)
    per_chip = [_unpack(gathered[j], shard_shapes) for j in range(N_CHIPS)]
    full = {n: jnp.concatenate([per_chip[j][i] for j in range(N_CHIPS)], axis=BIG_SHARD_AXIS[n]) for i, n in enumerate(BIG)}
    small = {n: weights[n] for n in SMALL}
    layers = [_prepare_layer(full, small, l) for l in range(depth)]

    c_pad = jnp.pad(c, ((0, SUBLANES - 1), (0, 0)))
    c_all = _all_gather("gather_cond", c_pad).reshape(N_DEV, SUBLANES, d)[:, 0, :]
    silu = lambda v: v * _sigmoid(v)
    n_cols = w_ada.shape[2]
    mod_shard = []
    for l in range(depth):
        bias = lax.dynamic_slice_in_dim(b_ada[l], my_chip * n_cols, n_cols)
        mod_shard.append(_mm_plain(f"ada_{l}", c_all, w_ada[l], "nn", F32, add=jnp.broadcast_to(bias, (N_DEV, n_cols)),
                                   a_fn=silu, tm=N_DEV, tn=512, tk=1024))
    mod_block = jnp.concatenate(mod_shard, axis=1)
    mod_all = _all_gather("gather_mod", mod_block).reshape(N_DEV, N_DEV, depth, n_cols)
    mod_rows = lax.dynamic_index_in_dim(mod_all[0::2], my_dev, axis=1, keepdims=False)
    mods = [mod_rows[:, l, :].reshape(n_mod, d) for l in range(depth)]

    loss_blk, dx, d_mods, grads, stacked = _fwd_bwd(xs, loss_target[0], mods, layers, b_f.shape[1])
    loss = lax.psum(loss_blk[0, 0], ("x", "y", "c"))
    grad_x = dx[None]

    gbuf = jnp.stack([_pack([_shard_of(stacked[n], BIG_SHARD_AXIS[n], j) for n in BIG], COMM_LANES, 32, BF16)
                      for j in range(N_CHIPS)]).reshape(N_CHIPS, 2, rh, COMM_LANES)
    from_sibling = _swap_halves("grads_swap_cores", gbuf)
    chip_part = _add_mine("grads_add_cores", gbuf, from_sibling, mc.astype(jnp.int32).reshape(1))
    from_chips = _chip_exchange("grads_exchange_chips", chip_part)
    my_half = _sum_blocks("grads_sum_chips", from_chips, F32)
    shard_flat = _share_halves("grads_share_cores", my_half).reshape(2 * rh, COMM_LANES)
    big_grads = dict(zip(BIG, _unpack(shard_flat, shard_shapes)))

    partial_names = ("g_pre_mix", "g_post_mix", "g_pre_ffn", "g_post_ffn", "d_skip", "b_glu", "b_f", "a_bar",
                     "bbar_re", "bbar_im", "c_re", "c_im")
    n_state, group_ch = b_re.shape[2:]
    contrib = list(d_mods)
    for l in range(depth):
        compact = _compact_partials(grads[l], n_state, group_ch)
        contrib += [compact[n] for n in partial_names]
    contrib_shapes = [a.shape for a in contrib]
    block = _pack(contrib, LANES, SUBLANES, F32)
    rows = block.shape[0]
    all_blocks = _all_gather("gather_small_grads", block).reshape(N_DEV, rows, LANES)
    summed = _unpack(_sum_blocks("sum_small_grads", all_blocks, F32), contrib_shapes)
    per_layer = len(partial_names)
    small_grads = {n: [] for n in SMALL}
    d_mod_all = []
    for l in range(depth):
        small_grads["b_ada"].append(summed[l].reshape(-1))
        gl = dict(zip(partial_names, summed[depth + l * per_layer:depth + (l + 1) * per_layer]))
        for n in ("g_pre_mix", "g_post_mix", "g_pre_ffn", "g_post_ffn", "d_skip", "b_glu", "b_f"):
            small_grads[n].append(gl[n])
        for n, gval in _small_grads_from_partials(gl, small, l).items():
            small_grads[n].append(gval)
        d_mod_all.append(all_blocks.reshape(N_DEV, rows * LANES)[:, l * n_mod * d:(l + 1) * n_mod * d])
    small_grads = {n: jnp.stack(v) for n, v in small_grads.items()}

    g_w_ada = []
    for l in range(depth):
        cols = lax.dynamic_slice_in_dim(d_mod_all[l], my_chip * n_cols, n_cols, axis=1)
        g_w_ada.append(_mm_plain(f"dw_ada_{l}", c_all, cols, "tn", F32, a_fn=silu, tm=512, tn=512, tk=N_DEV))
    all_grads = dict(big_grads)
    all_grads.update(small_grads)
    all_grads["w_ada"] = jnp.stack(g_w_ada)

    delta, new_m, new_v = {}, {}, {}
    for n in ("w_ada",) + BIG:
        shape = weights[n].shape
        two_d = lambda a: a.reshape(-1, shape[-1])
        dl, nm, nv = _adamw(f"adamw_{n}", two_d(weights[n]), two_d(all_grads[n]), two_d(moments_m[n]), two_d(moments_v[n]))
        delta[n], new_m[n], new_v[n] = dl.reshape(shape), nm.reshape(shape), nv.reshape(shape)
    small_shapes = [weights[n].shape for n in SMALL]
    packed = [_pack([src[n] for n in SMALL], LANES, SUBLANES, F32) for src in (weights, all_grads, moments_m, moments_v)]
    outs = _adamw("adamw_small", *packed)
    for dst, buf in zip((delta, new_m, new_v), outs):
        dst.update(dict(zip(SMALL, _unpack(buf, small_shapes))))

    return (loss, grad_x, *[all_grads[n] for n in WEIGHTS], *[delta[n] for n in WEIGHTS],
            *[new_m[n] for n in WEIGHTS], *[new_v[n] for n in WEIGHTS])
```

```python
import functools
import math

import jax
import jax.numpy as jnp
from jax import lax
from jax.experimental import pallas as pl
from jax.experimental.pallas import tpu as pltpu

F32 = jnp.float32
BF16 = jnp.bfloat16
MESH = pl.DeviceIdType.MESH

RMS_EPS = 1e-6
EIG_CLIP = 1e-4
ADAM_LR, ADAM_B1, ADAM_B2, ADAM_EPS, ADAM_WD, ADAM_STEP = 0.001, 0.9, 0.999, 1e-08, 0.01, 10

LANES = 128
SUBLANES = 8
VMEM_LIMIT = 56 * 1024 * 1024
S5_ROWS = 256
S5_CHUNK = 256
ATT_BLOCK = 512
F_PAD = 256
N_CHIPS = 4
N_DEV = 8

NN = (((1,), (0,)), ((), ()))
NT = (((1,), (1,)), ((), ()))
TN = (((0,), (0,)), ((), ()))
_DN = {"nn": NN, "nt": NT, "tn": TN}


def _cparams(**kw):
    return pltpu.CompilerParams(vmem_limit_bytes=VMEM_LIMIT, **kw)


def _pick(dim, target):
    best, t = None, LANES
    while t <= min(dim, target):
        if dim % t == 0:
            best = t
        t += LANES
    return best or dim


def _sigmoid(x):
    return 1.0 / (1.0 + jnp.exp(-x))


def _dot(a, b, dn):
    return lax.dot_general(a, b, dn, preferred_element_type=F32)


def _mm_raw(name, a, b, mode, grid, acc_shape, a_spec, b_spec, out_shapes, out_specs, epilogue,
            extra=(), extra_specs=(), a_fn=None):
    nk = grid[2]
    n_extra, n_out = len(extra), len(out_shapes)

    def body(*refs):
        a_ref, b_ref = refs[0], refs[1]
        extra_refs = refs[2:2 + n_extra]
        out_refs = refs[2 + n_extra:2 + n_extra + n_out]
        acc = refs[-1]
        k = pl.program_id(2)

        @pl.when(k == 0)
        def _():
            acc[...] = jnp.zeros_like(acc)

        av = a_ref[...]
        if a_fn is not None:
            av = a_fn(av.astype(F32))
        acc[...] += _dot(av.astype(BF16), b_ref[...].astype(BF16), _DN[mode])

        @pl.when(k == nk - 1)
        def _():
            epilogue(acc[...], extra_refs, out_refs)

    return pl.pallas_call(
        body, name=name, grid=grid,
        in_specs=[a_spec, b_spec, *extra_specs],
        out_specs=list(out_specs), out_shape=list(out_shapes),
        scratch_shapes=[pltpu.VMEM(acc_shape, F32)],
        compiler_params=_cparams(),
    )(a, b, *extra)


def _mm(name, a, b, mode, out_shapes, out_specs, epilogue, extra=(), extra_specs=(),
        tm=512, tn=512, tk=512, a_fn=None):
    if mode == "nn":
        (m, kd), (_, n) = a.shape, b.shape
    elif mode == "nt":
        (m, kd), (n, _) = a.shape, b.shape
    else:
        (kd, m), (_, n) = a.shape, b.shape
    tm, tn, tk = _pick(m, tm), _pick(n, tn), _pick(kd, tk)
    if mode == "tn":
        a_spec = pl.BlockSpec((tk, tm), lambda i, j, k: (k, i))
    else:
        a_spec = pl.BlockSpec((tm, tk), lambda i, j, k: (i, k))
    if mode == "nt":
        b_spec = pl.BlockSpec((tn, tk), lambda i, j, k: (j, k))
    else:
        b_spec = pl.BlockSpec((tk, tn), lambda i, j, k: (k, j))
    res = _mm_raw(name, a, b, mode, (m // tm, n // tn, kd // tk), (tm, tn), a_spec, b_spec, out_shapes, out_specs,
                  epilogue, extra=extra, extra_specs=extra_specs, a_fn=a_fn)
    return res, (tm, tn, tk)


def _store(dtype):
    def epilogue(acc, extra_refs, out_refs):
        out_refs[0][...] = acc.astype(dtype)
    return epilogue


def _mm_sum(name, m, n, tm, tn, pairs, out_dtype):
    offs, total = [], 0
    for pr in pairs:
        offs.append(total)
        total += pr[6]
    n_p = len(pairs)

    def body(*refs):
        o_ref, acc = refs[2 * n_p], refs[2 * n_p + 1]
        k = pl.program_id(2)

        @pl.when(k == 0)
        def _():
            acc[...] = jnp.zeros_like(acc)

        for p_ in range(n_p):
            @pl.when((k >= offs[p_]) & (k < offs[p_] + pairs[p_][6]))
            def _(p_=p_):
                acc[...] += _dot(refs[2 * p_][...].astype(BF16), refs[2 * p_ + 1][...].astype(BF16), NT)

        @pl.when(k == total - 1)
        def _():
            o_ref[...] = acc[...].astype(out_dtype)

    in_specs, operands = [], []
    for (a, a_block, a_index, b, b_block, b_index, steps), off in zip(pairs, offs):
        local = lambda k, off=off, steps=steps: jnp.clip(k - off, 0, steps - 1)
        in_specs.append(pl.BlockSpec(a_block, lambda i, j, k, f=a_index, local=local: f(i, local(k))))
        in_specs.append(pl.BlockSpec(b_block, lambda i, j, k, f=b_index, local=local: f(j, local(k))))
        operands += [a, b]
    return pl.pallas_call(
        body, name=name, grid=(m // tm, n // tn, total), in_specs=in_specs,
        out_specs=pl.BlockSpec((tm, tn), lambda i, j, k: (i, j)), out_shape=jax.ShapeDtypeStruct((m, n), out_dtype),
        scratch_shapes=[pltpu.VMEM((tm, tn), F32)], compiler_params=_cparams(),
    )(*operands)


def _ffn_up(name, h, wg, wu, l):
    s, d = h.shape
    nc, fs = wg.shape[0], wg.shape[3]
    tm, tk = _pick(s, 512), _pick(d, 512)
    nk = d // tk

    def body(h_ref, wg_ref, wu_ref, a_ref, b_ref, hid_ref, acc_g, acc_u):
        k = pl.program_id(2)

        @pl.when(k == 0)
        def _():
            acc_g[...] = jnp.zeros_like(acc_g)
            acc_u[...] = jnp.zeros_like(acc_u)

        hv = h_ref[...]
        acc_g[...] += _dot(hv, wg_ref[...], NN)
        acc_u[...] += _dot(hv, wu_ref[...], NN)

        @pl.when(k == nk - 1)
        def _():
            av, bv = acc_g[...], acc_u[...]
            a_ref[...] = av.astype(BF16)
            b_ref[...] = bv.astype(BF16)
            hid_ref[...] = (av * _sigmoid(av) * bv).astype(BF16)

    w_spec = pl.BlockSpec((None, None, tk, fs), lambda i, j, k: (j, l, k, 0))
    o_spec = pl.BlockSpec((None, tm, fs), lambda i, j, k: (j, i, 0))
    sh = jax.ShapeDtypeStruct((nc, s, fs), BF16)
    return pl.pallas_call(
        body, name=name, grid=(s // tm, nc, nk),
        in_specs=[pl.BlockSpec((tm, tk), lambda i, j, k: (i, k)), w_spec, w_spec],
        out_specs=[o_spec] * 3, out_shape=[sh] * 3,
        scratch_shapes=[pltpu.VMEM((tm, fs), F32), pltpu.VMEM((tm, fs), F32)], compiler_params=_cparams(),
    )(h, wg, wu)


def _mm_plain(name, a, b, mode, out_dtype, add=None, a_fn=None, tm=512, tn=512, tk=512):
    if mode == "nn":
        m, n = a.shape[0], b.shape[1]
    elif mode == "nt":
        m, n = a.shape[0], b.shape[0]
    else:
        m, n = a.shape[1], b.shape[1]
    tm_, tn_ = _pick(m, tm), _pick(n, tn)
    spec = pl.BlockSpec((tm_, tn_), lambda i, j, k: (i, j))

    def epilogue(acc, extra_refs, out_refs):
        if add is not None:
            acc = acc + extra_refs[0][...]
        out_refs[0][...] = acc.astype(out_dtype)

    extra = () if add is None else (add,)
    (out,), _ = _mm(name, a, b, mode, [jax.ShapeDtypeStruct((m, n), out_dtype)], [spec], epilogue,
                    extra=extra, extra_specs=[spec] * len(extra), tm=tm, tn=tn, tk=tk, a_fn=a_fn)
    return out


def _row_tile(s, d):
    return _pick(s, max(SUBLANES, (1 << 20) // (4 * d)))


def _prenorm_fwd(name, x, g, scale, shift):
    s, d = x.shape
    tr = _row_tile(s, d)

    def body(x_ref, g_ref, sc_ref, sh_ref, h_ref):
        xv = x_ref[...]
        r = lax.rsqrt(jnp.mean(xv * xv, axis=-1, keepdims=True) + RMS_EPS)
        h_ref[...] = ((xv * r * g_ref[...]) * (1.0 + sc_ref[...]) + sh_ref[...]).astype(BF16)

    row = pl.BlockSpec((tr, d), lambda i: (i, 0))
    vec = pl.BlockSpec((1, d), lambda i: (0, 0))
    return pl.pallas_call(body, name=name, grid=(s // tr,), in_specs=[row, vec, vec, vec], out_specs=row,
                          out_shape=jax.ShapeDtypeStruct((s, d), BF16), compiler_params=_cparams())(x, g, scale, shift)


def _prenorm_bwd(name, dh, x, g, scale, dx_res):
    s, d = x.shape
    tr = _row_tile(s, d)

    def body(dh_ref, x_ref, g_ref, sc_ref, dxr_ref, dx_ref, sums_ref):
        @pl.when(pl.program_id(0) == 0)
        def _():
            sums_ref[...] = jnp.zeros_like(sums_ref)

        xv, dhv, gv = x_ref[...], dh_ref[...].astype(F32), g_ref[...]
        r = lax.rsqrt(jnp.mean(xv * xv, axis=-1, keepdims=True) + RMS_EPS)
        xhat = xv * r
        dxn = dhv * (1.0 + sc_ref[...])
        dxhat = dxn * gv
        dx = r * (dxhat - xhat * jnp.mean(dxhat * xhat, axis=-1, keepdims=True))
        dx_ref[...] = dxr_ref[...] + dx
        sums_ref[0:1, :] += jnp.sum(dhv * (xhat * gv), axis=0, keepdims=True)
        sums_ref[1:2, :] += jnp.sum(dhv, axis=0, keepdims=True)
        sums_ref[2:3, :] += jnp.sum(dxn * xhat, axis=0, keepdims=True)

    row = pl.BlockSpec((tr, d), lambda i: (i, 0))
    vec = pl.BlockSpec((1, d), lambda i: (0, 0))
    acc = pl.BlockSpec((SUBLANES, d), lambda i: (0, 0))
    return pl.pallas_call(
        body, name=name, grid=(s // tr,), in_specs=[row, row, vec, vec, row], out_specs=[row, acc],
        out_shape=[jax.ShapeDtypeStruct((s, d), F32), jax.ShapeDtypeStruct((SUBLANES, d), F32)],
        compiler_params=_cparams())(dh, x, g, scale, dx_res)


def _postnorm_bwd(name, dxn, y, g, gate):
    s, d = y.shape
    tr = _row_tile(s, d)

    def body(dx_ref, y_ref, g_ref, gt_ref, dy_ref, sums_ref):
        @pl.when(pl.program_id(0) == 0)
        def _():
            sums_ref[...] = jnp.zeros_like(sums_ref)

        yv, dxv, gv = y_ref[...], dx_ref[...], g_ref[...]
        r = lax.rsqrt(jnp.mean(yv * yv, axis=-1, keepdims=True) + RMS_EPS)
        yhat = yv * r
        dn = dxv * gt_ref[...]
        dyhat = dn * gv
        dy_ref[...] = (r * (dyhat - yhat * jnp.mean(dyhat * yhat, axis=-1, keepdims=True))).astype(BF16)
        sums_ref[0:1, :] += jnp.sum(dxv * (yhat * gv), axis=0, keepdims=True)
        sums_ref[1:2, :] += jnp.sum(dn * yhat, axis=0, keepdims=True)

    row = pl.BlockSpec((tr, d), lambda i: (i, 0))
    vec = pl.BlockSpec((1, d), lambda i: (0, 0))
    acc = pl.BlockSpec((SUBLANES, d), lambda i: (0, 0))
    return pl.pallas_call(
        body, name=name, grid=(s // tr,), in_specs=[row, row, vec, vec], out_specs=[row, acc],
        out_shape=[jax.ShapeDtypeStruct((s, d), BF16), jax.ShapeDtypeStruct((SUBLANES, d), F32)],
        compiler_params=_cparams())(dxn, y, g, gate)


def _loss_grad(name, y, target):
    s, d = y.shape
    tr = _row_tile(s, d)

    def body(y_ref, t_ref, dy_ref, loss_ref):
        @pl.when(pl.program_id(0) == 0)
        def _():
            loss_ref[...] = jnp.zeros_like(loss_ref)

        err = y_ref[...] - t_ref[...]
        dy_ref[...] = err * (1.0 / d)
        part = jnp.sum(jnp.sum(err * err, axis=-1, keepdims=True), axis=0, keepdims=True) * (0.5 / d)
        loss_ref[...] += jnp.broadcast_to(part, loss_ref.shape)

    row = pl.BlockSpec((tr, d), lambda i: (i, 0))
    acc = pl.BlockSpec((SUBLANES, LANES), lambda i: (0, 0))
    return pl.pallas_call(
        body, name=name, grid=(s // tr,), in_specs=[row, row], out_specs=[row, acc],
        out_shape=[jax.ShapeDtypeStruct((s, d), F32), jax.ShapeDtypeStruct((SUBLANES, LANES), F32)],
        compiler_params=_cparams())(y, target)


def _gelu(y):
    c = math.sqrt(2.0 / math.pi)
    return 0.5 * y * (1.0 + jnp.tanh(c * (y + 0.044715 * (y * y * y))))


def _gelu_grad(y):
    c = math.sqrt(2.0 / math.pi)
    th = jnp.tanh(c * (y + 0.044715 * (y * y * y)))
    return 0.5 * (1.0 + th) + 0.5 * y * (1.0 - th * th) * c * (1.0 + 3.0 * 0.044715 * (y * y))


def _scan_rows(x_ref, row0, n_groups, ns2, tab_ref, carry_ref, reverse, after_group=None, extra_init=None):
    wc = min(S5_CHUNK, ns2)
    shifts = (1, 2, 4)
    for c0 in range(0, ns2, wc):
        re = slice(c0, c0 + wc)
        im = slice(ns2 + c0, ns2 + c0 + wc)
        tabs = [tab_ref[k, :, re] for k in range(8)]

        def group(i, carry, re=re, im=im, tabs=tabs, c0=c0):
            cr, ci, extra = carry
            g = (n_groups - 1 - i) if reverse else i
            r0 = pl.multiple_of(row0 + g * SUBLANES, SUBLANES)
            br = x_ref[pl.ds(r0, SUBLANES), re]
            bi = x_ref[pl.ds(r0, SUBLANES), im]
            for lvl, k in enumerate(shifts):
                mr, mi = tabs[2 * lvl], tabs[2 * lvl + 1]
                sh = (SUBLANES - k) if reverse else k
                sr = pltpu.roll(br, sh, 0)
                si = pltpu.roll(bi, sh, 0)
                br, bi = br + mr * sr - mi * si, bi + mr * si + mi * sr
            apr, api = tabs[6], tabs[7]
            xr = br + apr * cr - api * ci
            xi = bi + apr * ci + api * cr
            x_ref[pl.ds(r0, SUBLANES), re] = xr
            x_ref[pl.ds(r0, SUBLANES), im] = xi
            if after_group is not None:
                extra = after_group(c0, r0, xr, xi, extra)
            if reverse:
                return xr[0:1, :], xi[0:1, :], extra
            return xr[SUBLANES - 1:SUBLANES, :], xi[SUBLANES - 1:SUBLANES, :], extra

        init_extra = extra_init(wc) if extra_init is not None else 0
        cr, ci, extra = lax.fori_loop(0, n_groups, group, (carry_ref[0:1, re], carry_ref[0:1, im], init_extra))
        carry_ref[0:1, re] = cr
        carry_ref[0:1, im] = ci
        if after_group is not None:
            after_group(c0, None, None, None, extra)


def _s5_fwd(name, u, b_blk, c_blk, tab_f, dskip, w_glu, b_glu):
    s, w = u.shape[0], w_glu.shape[0]
    nkb = w // LANES
    ns2 = b_blk.shape[2] // 2 * nkb
    half = ns2 // nkb
    t = min(S5_ROWS, s)
    nblk = s // t

    def body(u_ref, b_ref, c_ref, tab_ref, ds_ref, wg_ref, bg_ref, y_ref, ys_ref, cs_ref, xs, carry):
        @pl.when(pl.program_id(0) == 0)
        def _():
            carry[...] = jnp.zeros_like(carry)

        cs_ref[0] = carry[...]
        for kb in range(nkb):
            bu = _dot(u_ref[:, kb * LANES:(kb + 1) * LANES], b_ref[kb], NN)
            xs[:, kb * half:(kb + 1) * half] = bu[:, :half]
            xs[:, ns2 + kb * half:ns2 + (kb + 1) * half] = bu[:, half:]
        _scan_rows(xs, 0, t // SUBLANES, ns2, tab_ref, carry, reverse=False)
        for kb in range(nkb):
            cols = slice(kb * LANES, (kb + 1) * LANES)
            yk = _dot(xs[:, kb * half:(kb + 1) * half].astype(BF16), c_ref[kb, :half, :], NN)
            yk += _dot(xs[:, ns2 + kb * half:ns2 + (kb + 1) * half].astype(BF16), c_ref[kb, half:, :], NN)
            y_ref[:, cols] = yk + ds_ref[:, cols] * u_ref[:, cols].astype(F32)
        z = _gelu(y_ref[...])
        gate = _sigmoid(_dot(z.astype(BF16), wg_ref[...], NN) + bg_ref[...])
        ys_ref[...] = (z * gate).astype(BF16)

    row = pl.BlockSpec((t, w), lambda i: (i, 0))
    full = lambda shape: pl.BlockSpec(shape, lambda i: (0,) * len(shape))
    return pl.pallas_call(
        body, name=name, grid=(nblk,),
        in_specs=[row, full(b_blk.shape), full(c_blk.shape), full(tab_f.shape), full(dskip.shape),
                  full(w_glu.shape), full(b_glu.shape)],
        out_specs=[row, row, pl.BlockSpec((1, 1, 2 * ns2), lambda i: (i, 0, 0))],
        out_shape=[jax.ShapeDtypeStruct((s, w), F32), jax.ShapeDtypeStruct((s, w), BF16),
                   jax.ShapeDtypeStruct((nblk, 1, 2 * ns2), F32)],
        scratch_shapes=[pltpu.VMEM((t, 2 * ns2), F32), pltpu.VMEM((1, 2 * ns2), F32)],
        compiler_params=_cparams(),
    )(u, b_blk, c_blk, tab_f, dskip, w_glu, b_glu)


def _s5_bwd(name, u, dys, y, carries, b_blk, c_blk, tab_f, tab_r, dskip, w_glu, b_glu):
    s, w = u.shape[0], w_glu.shape[0]
    nkb = w // LANES
    ns2 = b_blk.shape[2] // 2 * nkb
    half = ns2 // nkb
    t = min(S5_ROWS, s)
    nblk = s // t
    ng = t // SUBLANES

    def body(u_ref, dys_ref, y_ref, cs_ref, b_ref, c_ref, tabf_ref, tabr_ref, ds_ref, wg_ref, bg_ref,
             du_ref, db_ref, dc_ref, da_ref, dwg_ref, vec_ref, xs, gs, dyv, fcarry, gcarry):
        @pl.when(pl.program_id(0) == 0)
        def _():
            db_ref[...] = jnp.zeros_like(db_ref)
            dc_ref[...] = jnp.zeros_like(dc_ref)
            da_ref[...] = jnp.zeros_like(da_ref)
            dwg_ref[...] = jnp.zeros_like(dwg_ref)
            vec_ref[...] = jnp.zeros_like(vec_ref)
            gcarry[...] = jnp.zeros_like(gcarry)

        yv = y_ref[...]
        z = _gelu(yv)
        zb = z.astype(BF16)
        gate = _sigmoid(_dot(zb, wg_ref[...], NN) + bg_ref[...])
        dout = dys_ref[...].astype(F32)
        dt = dout * z * gate * (1.0 - gate)
        dtb = dt.astype(BF16)
        dz = dout * gate + _dot(dtb, wg_ref[...], NT)
        dy = dz * _gelu_grad(yv)
        dyv[...] = dy
        dwg_ref[...] += _dot(zb, dtb, TN)
        vec_ref[0:1, :] += jnp.sum(dt, axis=0, keepdims=True)
        vec_ref[1:2, :] += jnp.sum(dy * u_ref[...].astype(F32), axis=0, keepdims=True)

        fcarry[...] = cs_ref[0]
        xs[0:SUBLANES, :] = jnp.broadcast_to(cs_ref[0], (SUBLANES, 2 * ns2))
        for kb in range(nkb):
            bu = _dot(u_ref[:, kb * LANES:(kb + 1) * LANES], b_ref[kb], NN)
            xs[SUBLANES:, kb * half:(kb + 1) * half] = bu[:, :half]
            xs[SUBLANES:, ns2 + kb * half:ns2 + (kb + 1) * half] = bu[:, half:]
        _scan_rows(xs, SUBLANES, ng, ns2, tabf_ref, fcarry, reverse=False)

        for kb in range(nkb):
            dyk = dyv[:, kb * LANES:(kb + 1) * LANES].astype(BF16)
            re = slice(kb * half, (kb + 1) * half)
            im = slice(ns2 + kb * half, ns2 + (kb + 1) * half)
            gs[:, re] = _dot(dyk, c_ref[kb, :half, :], NT)
            gs[:, im] = _dot(dyk, c_ref[kb, half:, :], NT)
            dc_ref[kb, :half, :] += _dot(xs[SUBLANES:, re].astype(BF16), dyk, TN)
            dc_ref[kb, half:, :] += _dot(xs[SUBLANES:, im].astype(BF16), dyk, TN)

        row_is_first = lax.broadcasted_iota(jnp.int32, (SUBLANES, min(S5_CHUNK, ns2)), 0) == 0

        def fold(c0, r0, gr, gi, acc):
            wc = min(S5_CHUNK, ns2)
            re = slice(c0, c0 + wc)
            im = slice(ns2 + c0, ns2 + c0 + wc)
            if r0 is None:
                da_ref[:, re] += acc[0]
                da_ref[:, im] += acc[1]
                return acc
            cur_r = xs[pl.ds(r0 + SUBLANES, SUBLANES), re]
            cur_i = xs[pl.ds(r0 + SUBLANES, SUBLANES), im]
            prv_r = xs[pl.ds(r0, SUBLANES), re]
            prv_i = xs[pl.ds(r0, SUBLANES), im]
            xpr = jnp.where(row_is_first, prv_r[SUBLANES - 1:SUBLANES, :], pltpu.roll(cur_r, 1, 0))
            xpi = jnp.where(row_is_first, prv_i[SUBLANES - 1:SUBLANES, :], pltpu.roll(cur_i, 1, 0))
            return acc[0] + gr * xpr + gi * xpi, acc[1] - gr * xpi + gi * xpr

        zero2 = lambda wc: (jnp.zeros((SUBLANES, wc), F32), jnp.zeros((SUBLANES, wc), F32))
        _scan_rows(gs, 0, ng, ns2, tabr_ref, gcarry, reverse=True, after_group=fold, extra_init=zero2)

        for kb in range(nkb):
            cols = slice(kb * LANES, (kb + 1) * LANES)
            re = slice(kb * half, (kb + 1) * half)
            im = slice(ns2 + kb * half, ns2 + (kb + 1) * half)
            uk = u_ref[:, cols]
            gr = gs[:, re].astype(BF16)
            gi = gs[:, im].astype(BF16)
            db_ref[kb, :, :half] += _dot(uk, gr, TN)
            db_ref[kb, :, half:] += _dot(uk, gi, TN)
            duk = _dot(gr, b_ref[kb, :, :half], NT) + _dot(gi, b_ref[kb, :, half:], NT)
            du_ref[:, cols] = (duk + ds_ref[:, cols] * dyv[:, cols]).astype(BF16)

    rev = lambda i: (nblk - 1 - i, 0)
    row = pl.BlockSpec((t, w), rev)
    full = lambda shape: pl.BlockSpec(shape, lambda i: (0,) * len(shape))
    return pl.pallas_call(
        body, name=name, grid=(nblk,),
        in_specs=[row, row, row, pl.BlockSpec((1, 1, 2 * ns2), lambda i: (nblk - 1 - i, 0, 0)),
                  full(b_blk.shape), full(c_blk.shape), full(tab_f.shape), full(tab_r.shape),
                  full(dskip.shape), full(w_glu.shape), full(b_glu.shape)],
        out_specs=[row, full(b_blk.shape), full(c_blk.shape), full((SUBLANES, 2 * ns2)), full((w, w)),
                   full((SUBLANES, w))],
        out_shape=[jax.ShapeDtypeStruct((s, w), BF16), jax.ShapeDtypeStruct(b_blk.shape, F32),
                   jax.ShapeDtypeStruct(c_blk.shape, F32), jax.ShapeDtypeStruct((SUBLANES, 2 * ns2), F32),
                   jax.ShapeDtypeStruct((w, w), F32), jax.ShapeDtypeStruct((SUBLANES, w), F32)],
        scratch_shapes=[pltpu.VMEM((t + SUBLANES, 2 * ns2), F32), pltpu.VMEM((t, 2 * ns2), F32),
                        pltpu.VMEM((t, w), F32), pltpu.VMEM((1, 2 * ns2), F32), pltpu.VMEM((1, 2 * ns2), F32)],
        compiler_params=_cparams(),
    )(u, dys, y, carries, b_blk, c_blk, tab_f, tab_r, dskip, w_glu, b_glu)


def _log_sigmoid(x):
    return jnp.minimum(x, 0.0) - jnp.log(1.0 + jnp.exp(-jnp.abs(x)))


def _cum_fwd(name, f_t, b_f):
    h, s = f_t.shape
    tc = _pick(s, 512)
    nb = s // tc

    def body(f_ref, b_ref, c_ref, carry):
        @pl.when(pl.program_id(0) == 0)
        def _():
            carry[...] = jnp.zeros_like(carry)

        lf = _log_sigmoid(f_ref[...] + b_ref[...])
        upper = (lax.broadcasted_iota(jnp.int32, (tc, tc), 0) <= lax.broadcasted_iota(jnp.int32, (tc, tc), 1))
        cum = lax.dot_general(lf, upper.astype(F32), NN, precision=lax.Precision.HIGHEST,
                              preferred_element_type=F32) + carry[...]
        c_ref[...] = cum
        carry[...] += jnp.sum(lf, axis=1, keepdims=True)

    blk = pl.BlockSpec((h, tc), lambda i: (0, i))
    return pl.pallas_call(body, name=name, grid=(nb,), in_specs=[blk, pl.BlockSpec((h, 1), lambda i: (0, 0))],
                          out_specs=blk, out_shape=jax.ShapeDtypeStruct((h, s), F32),
                          scratch_shapes=[pltpu.VMEM((h, 1), F32)], compiler_params=_cparams())(f_t, b_f)


def _cum_bwd(name, dcq, dck, f_t, b_f):
    h, s = f_t.shape
    tc = _pick(s, 512)
    nb = s // tc

    def body(dcq_ref, dck_ref, f_ref, b_ref, df_ref, db_ref, carry):
        @pl.when(pl.program_id(0) == 0)
        def _():
            carry[...] = jnp.zeros_like(carry)
            db_ref[...] = jnp.zeros_like(db_ref)

        dc = dcq_ref[...] + dck_ref[...]
        lower = (lax.broadcasted_iota(jnp.int32, (tc, tc), 0) >= lax.broadcasted_iota(jnp.int32, (tc, tc), 1))
        dlf = lax.dot_general(dc, lower.astype(F32), NN, precision=lax.Precision.HIGHEST,
                              preferred_element_type=F32) + carry[...]
        carry[...] += jnp.sum(dc, axis=1, keepdims=True)
        df = dlf * _sigmoid(-(f_ref[...] + b_ref[...]))
        df_ref[...] = df
        db_ref[...] += jnp.broadcast_to(jnp.sum(df, axis=1, keepdims=True), db_ref.shape)

    blk = pl.BlockSpec((h, tc), lambda i: (0, nb - 1 - i))
    return pl.pallas_call(
        body, name=name, grid=(nb,), in_specs=[blk, blk, blk, pl.BlockSpec((h, 1), lambda i: (0, 0))],
        out_specs=[blk, pl.BlockSpec((h, LANES), lambda i: (0, 0))],
        out_shape=[jax.ShapeDtypeStruct((h, s), F32), jax.ShapeDtypeStruct((h, LANES), F32)],
        scratch_shapes=[pltpu.VMEM((h, 1), F32)], compiler_params=_cparams())(dcq, dck, f_t, b_f)


def _attn_fwd(name, qkv, q_blk, k_blk, v_blk, n_pairs, ck):
    s = qkv.shape[0]
    dh = LANES // 2
    t = min(ATT_BLOCK, s)
    nq = s // t
    scale = dh ** -0.5

    def body(q_ref, k_ref, v_ref, ck_ref, o_ref, lse_ref, m_s, acc_s):
        i = pl.program_id(1)
        low = lax.broadcasted_iota(jnp.int32, (1, LANES), 1) < dh
        qs = (q_ref[...].astype(F32) * scale).astype(BF16)
        zero = jnp.zeros_like(qs)
        qh = (jnp.where(low, qs, zero), jnp.where(low, zero, qs))
        m_s[...] = jnp.full(m_s.shape, -1e30, F32)
        acc_s[...] = jnp.zeros_like(acc_s)
        causal = (lax.broadcasted_iota(jnp.int32, (t, t), 1) <= lax.broadcasted_iota(jnp.int32, (t, t), 0))

        def step(j, diagonal):
            r0 = pl.multiple_of(j * t, t)
            kj = k_ref[pl.ds(r0, t), :]
            vj = v_ref[pl.ds(r0, t), :]
            one = jnp.ones_like(vj)
            vh = (jnp.where(low, vj, one), jnp.where(low, one, vj))
            for hd in range(2):
                sc = _dot(qh[hd], kj, NT) - ck_ref[hd, j]
                if diagonal:
                    sc = jnp.where(causal, sc, -1e30)
                m_old = m_s[hd]
                m_new = jnp.maximum(m_old, jnp.max(sc, axis=1, keepdims=True))
                p = jnp.exp(sc - m_new)
                acc_s[hd] = jnp.exp(m_old - m_new) * acc_s[hd] + _dot(p.astype(BF16), vh[hd], NN)
                m_s[hd] = m_new

        def full(j, _):
            step(j, False)
            return 0

        lax.fori_loop(0, i, full, 0)
        step(i, True)
        a0, a1 = acc_s[0], acc_s[1]
        o_ref[...] = jnp.where(low, a0 / pltpu.roll(a0, dh, 1), a1 / pltpu.roll(a1, dh, 1)).astype(BF16)
        lse_ref[0] = m_s[0] + jnp.log(a0[:, dh:dh + 1])
        lse_ref[1] = m_s[1] + jnp.log(a1[:, 0:1])

    return pl.pallas_call(
        body, name=name, grid=(n_pairs, nq),
        in_specs=[pl.BlockSpec((t, LANES), lambda hp, i: (i, q_blk + hp)),
                  pl.BlockSpec((s, LANES), lambda hp, i: (0, k_blk + hp)),
                  pl.BlockSpec((s, LANES), lambda hp, i: (0, v_blk + hp)),
                  pl.BlockSpec((2, nq, 1, t), lambda hp, i: (hp, 0, 0, 0))],
        out_specs=[pl.BlockSpec((t, LANES), lambda hp, i: (i, hp)), pl.BlockSpec((2, t, 1), lambda hp, i: (hp, i, 0))],
        out_shape=[jax.ShapeDtypeStruct((s, LANES * n_pairs), BF16), jax.ShapeDtypeStruct((2 * n_pairs, s, 1), F32)],
        scratch_shapes=[pltpu.VMEM((2, t, 1), F32), pltpu.VMEM((2, t, LANES), F32)],
        compiler_params=_cparams(),
    )(qkv, qkv, qkv, ck)


def _attn_bwd(name, qkv, q_blk, k_blk, v_blk, n_pairs, o, do, lse_rows, ck_cols):
    s = qkv.shape[0]
    dh = LANES // 2
    t = min(ATT_BLOCK, s)
    nk = s // t
    scale = dh ** -0.5

    def body(q_ref, k_ref, v_ref, o_ref, do_ref, lse_ref, ck_ref,
             dq_ref, dk_ref, dv_ref, dcq_ref, dck_ref, delta, dqt, dk_acc, dv_acc):
        j = pl.program_id(1)
        low = lax.broadcasted_iota(jnp.int32, (1, LANES), 1) < dh
        low_rows = lax.broadcasted_iota(jnp.int32, (LANES, 1), 0) < dh

        @pl.when(j == 0)
        def _():
            dqt[...] = jnp.zeros_like(dqt)
            sel = (jnp.broadcast_to(low, (SUBLANES, LANES)).astype(F32), jnp.broadcast_to(~low, (SUBLANES, LANES)).astype(F32))

            def fill(i, _):
                r0 = pl.multiple_of(i * t, t)
                prod = do_ref[pl.ds(r0, t), :].astype(F32) * o_ref[pl.ds(r0, t), :].astype(F32)
                for hd in range(2):
                    delta[hd, i] = lax.dot_general(sel[hd], prod, NT, precision=lax.Precision.HIGHEST,
                                                   preferred_element_type=F32)
                return 0

            lax.fori_loop(0, nk, fill, 0)

        kj, vj = k_ref[...], v_ref[...]
        zero, one = jnp.zeros_like(kj), jnp.ones_like(kj)
        kh = (jnp.where(low, kj, zero), jnp.where(low, zero, kj))
        vh = (jnp.where(low, vj, zero), jnp.where(low, zero, vj))
        kjt = kj.astype(F32).T.astype(BF16)
        one_t = jnp.ones_like(kjt)
        kht = (jnp.where(low_rows, kjt, one_t), jnp.where(low_rows, one_t, kjt))
        dk_acc[...] = jnp.zeros_like(dk_acc)
        dv_acc[...] = jnp.zeros_like(dv_acc)
        causal_t = (lax.broadcasted_iota(jnp.int32, (t, t), 0) <= lax.broadcasted_iota(jnp.int32, (t, t), 1))

        def step(i, diagonal):
            r0 = pl.multiple_of(i * t, t)
            qi = (q_ref[pl.ds(r0, t), :].astype(F32) * scale).astype(BF16)
            doi = do_ref[pl.ds(r0, t), :]
            qone, dzero = jnp.ones_like(qi), jnp.zeros_like(doi)
            qsel = (jnp.where(low, qi, qone), jnp.where(low, qone, qi))
            dosel = (jnp.where(low, doi, dzero), jnp.where(low, dzero, doi))
            for hd in range(2):
                st = _dot(kh[hd], qi, NT) - ck_ref[hd] - lse_ref[hd, i]
                pt = jnp.exp(st)
                if diagonal:
                    pt = jnp.where(causal_t, pt, 0.0)
                dst = pt * (_dot(vh[hd], doi, NT) - delta[hd, i, 0:1, :])
                dsb = dst.astype(BF16)
                dv_acc[...] += _dot(pt.astype(BF16), dosel[hd], NN)
                dk_acc[hd] += _dot(dsb, qsel[hd], NN)
                dqt[hd, i] += _dot(kht[hd], dsb, NN)

        step(j, True)

        def rest(i, _):
            step(i, False)
            return 0

        lax.fori_loop(j + 1, nk, rest, 0)
        dk_ref[...] = jnp.where(low, dk_acc[0], dk_acc[1]).astype(BF16)
        dv_ref[...] = dv_acc[...].astype(BF16)
        dck_ref[0] = -dk_acc[0][:, dh:dh + 1]
        dck_ref[1] = -dk_acc[1][:, 0:1]

        @pl.when(j == nk - 1)
        def _():
            def emit(i, _):
                r0 = pl.multiple_of(i * t, t)
                d0, d1 = dqt[0, i], dqt[1, i]
                dq_ref[pl.ds(r0, t), :] = (jnp.where(low_rows, d0, d1) * scale).T.astype(BF16)
                dcq_ref[0, i] = d0[dh:dh + 1, :]
                dcq_ref[1, i] = d1[0:1, :]
                return 0

            lax.fori_loop(0, nk, emit, 0)

    col_blk = lambda base: pl.BlockSpec((t, LANES), lambda hp, j: (j, base + hp))
    col_all = lambda base: pl.BlockSpec((s, LANES), lambda hp, j: (0, base + hp))
    rows_all = pl.BlockSpec((2, nk, 1, t), lambda hp, j: (hp, 0, 0, 0))
    return pl.pallas_call(
        body, name=name, grid=(n_pairs, nk),
        in_specs=[col_all(q_blk), col_blk(k_blk), col_blk(v_blk), col_all(0), col_all(0), rows_all,
                  pl.BlockSpec((2, t, 1), lambda hp, j: (hp, j, 0))],
        out_specs=[col_all(0), col_blk(0), col_blk(0), rows_all, pl.BlockSpec((2, t, 1), lambda hp, j: (hp, j, 0))],
        out_shape=[jax.ShapeDtypeStruct((s, LANES * n_pairs), BF16), jax.ShapeDtypeStruct((s, LANES * n_pairs), BF16),
                   jax.ShapeDtypeStruct((s, LANES * n_pairs), BF16), jax.ShapeDtypeStruct((2 * n_pairs, nk, 1, t), F32),
                   jax.ShapeDtypeStruct((2 * n_pairs, s, 1), F32)],
        scratch_shapes=[pltpu.VMEM((2, nk, SUBLANES, t), F32), pltpu.VMEM((2, nk, LANES, t), F32),
                        pltpu.VMEM((2, t, LANES), F32), pltpu.VMEM((t, LANES), F32)],
        compiler_params=_cparams(),
    )(qkv, qkv, qkv, o, do, lse_rows, ck_cols)


def _adamw(name, w, g, m, v):
    r, c = w.shape
    tr = _pick8(r, max(SUBLANES, (1 << 20) // (4 * c)))

    def body(w_ref, g_ref, m_ref, v_ref, d_ref, mo_ref, vo_ref):
        gv = g_ref[...]
        m2 = ADAM_B1 * m_ref[...] + (1.0 - ADAM_B1) * gv
        v2 = ADAM_B2 * v_ref[...] + (1.0 - ADAM_B2) * (gv * gv)
        m_hat = m2 / (1.0 - ADAM_B1 ** ADAM_STEP)
        v_hat = v2 / (1.0 - ADAM_B2 ** ADAM_STEP)
        d_ref[...] = -ADAM_LR * (m_hat / (jnp.sqrt(v_hat) + ADAM_EPS) + ADAM_WD * w_ref[...])
        mo_ref[...] = m2
        vo_ref[...] = v2

    blk = pl.BlockSpec((tr, c), lambda i: (i, 0))
    sh = jax.ShapeDtypeStruct((r, c), F32)
    return pl.pallas_call(body, name=name, grid=(r // tr,), in_specs=[blk] * 4, out_specs=[blk] * 3,
                          out_shape=[sh, sh, sh], compiler_params=_cparams())(w, g, m, v)


def _pick8(dim, target, mult=SUBLANES):
    best, t = None, mult
    while t <= min(dim, target):
        if dim % t == 0:
            best = t
        t += mult
    return best or dim


BF16_ROWS = 16


def _sum_blocks(name, x, out_dtype):
    n, r, c = x.shape
    tr = _pick8(r, max(BF16_ROWS, (1 << 19) // (4 * c)), BF16_ROWS)

    def body(x_ref, o_ref):
        acc = x_ref[0].astype(F32)
        for i in range(1, n):
            acc = acc + x_ref[i].astype(F32)
        o_ref[...] = acc.astype(out_dtype)

    return pl.pallas_call(body, name=name, grid=(r // tr,),
                          in_specs=[pl.BlockSpec((n, tr, c), lambda i: (0, i, 0))],
                          out_specs=pl.BlockSpec((tr, c), lambda i: (i, 0)),
                          out_shape=jax.ShapeDtypeStruct((r, c), out_dtype), compiler_params=_cparams())(x)


def _add_layer(name, grads, recv, core):
    _, n, r, c = grads.shape
    tr = _pick8(r, max(BF16_ROWS, (1 << 19) // (4 * c)), BF16_ROWS)

    def body(core_ref, g_ref, r_ref, o_ref):
        o_ref[...] = (g_ref[...].astype(F32) + r_ref[...].astype(F32)).astype(BF16)

    grid_spec = pltpu.PrefetchScalarGridSpec(
        num_scalar_prefetch=1, grid=(r // tr,),
        in_specs=[pl.BlockSpec((None, n, tr, c), lambda i, core_ref: (core_ref[0], 0, i, 0)),
                  pl.BlockSpec((n, tr, c), lambda i, core_ref: (0, i, 0))],
        out_specs=pl.BlockSpec((n, tr, c), lambda i, core_ref: (0, i, 0)))
    return pl.pallas_call(body, name=name, grid_spec=grid_spec,
                          out_shape=jax.ShapeDtypeStruct((n, r, c), BF16), compiler_params=_cparams())(core, grads, recv)


def _all_gather(name, x_shard):
    m_per, n = x_shard.shape

    def body(x_ref, out_ref, send_sems, recv_sems):
        x, y, c = lax.axis_index("x"), lax.axis_index("y"), lax.axis_index("c")
        me, sibling = (x, y, c), (x, y, 1 - c)
        chips = [(1 - x, y), (x, 1 - y), (1 - x, 1 - y)]

        def rows(px, py, pc):
            return out_ref.at[pl.ds((4 * px + 2 * py + pc) * m_per, m_per), :]

        def copy(k, block, to, src=None):
            return pltpu.make_async_remote_copy(
                src_ref=rows(*block) if src is None else src, dst_ref=rows(*block),
                send_sem=send_sems.at[k], recv_sem=recv_sems.at[k], device_id=to, device_id_type=MESH)

        first = [copy(0, me, sibling, src=x_ref)]
        first += [copy(1 + j, me, (*chip, c), src=x_ref) for j, chip in enumerate(chips)]
        for cp in first:
            cp.start()
        passed = [copy(4 + j, (*chip, c), sibling) for j, chip in enumerate(chips)]
        for j, chip in enumerate(chips):
            copy(1 + j, (*chip, c), me).wait_recv()
            passed[j].start()
        copy(0, sibling, me).wait_recv()
        for j, chip in enumerate(chips):
            copy(4 + j, (*chip, 1 - c), me).wait_recv()
        for cp in first + passed:
            cp.wait_send()

    out = pl.pallas_call(
        body, name=name, out_shape=jax.ShapeDtypeStruct((N_DEV * m_per, n), x_shard.dtype),
        in_specs=[pl.BlockSpec(memory_space=pl.ANY)], out_specs=pl.BlockSpec(memory_space=pl.ANY),
        scratch_shapes=[pltpu.SemaphoreType.DMA((7,)), pltpu.SemaphoreType.DMA((7,))],
    )(x_shard)
    my_dev = 4 * lax.axis_index("x") + 2 * lax.axis_index("y") + lax.axis_index("c")
    return lax.dynamic_update_slice(out, x_shard, (my_dev * m_per, 0))


def _hbm_call(name, body, operands, out_shapes, n_sems):
    return pl.pallas_call(
        body, name=name, out_shape=list(out_shapes),
        in_specs=[pl.BlockSpec(memory_space=pl.ANY)] * len(operands),
        out_specs=[pl.BlockSpec(memory_space=pl.ANY)] * len(out_shapes),
        scratch_shapes=[pltpu.SemaphoreType.DMA((n_sems,)), pltpu.SemaphoreType.DMA((n_sems,))],
    )(*operands)


def _put_own(out, own, index):
    start = tuple(index) + (0,) * own.ndim
    return lax.dynamic_update_slice(out, own.reshape((1,) * len(index) + own.shape), start)


def _gather_weights(name, shards):
    n_w = len(shards)

    def body(*refs):
        ins, outs = refs[:n_w], refs[n_w:2 * n_w]
        send_sems, recv_sems = refs[2 * n_w], refs[2 * n_w + 1]
        x, y, c = lax.axis_index("x"), lax.axis_index("y"), lax.axis_index("c")
        my_chip = 2 * x + y
        chips = [(1 - x, y), (x, 1 - y), (1 - x, 1 - y)]

        def copy(w, k, src, chip, layer, to):
            return pltpu.make_async_remote_copy(
                src_ref=src, dst_ref=outs[w].at[chip, layer], send_sem=send_sems.at[6 * w + k],
                recv_sem=recv_sems.at[6 * w + k], device_id=to, device_id_type=MESH)

        started = []
        for w in range(n_w):
            for k, (cx, cy) in enumerate(chips):
                started.append(copy(w, k, ins[w].at[c], my_chip, c, (cx, cy, c)))
                started[-1].start()
        for w in range(n_w):
            for k, (cx, cy) in enumerate(chips):
                chip = 2 * cx + cy
                copy(w, k, ins[w].at[c], chip, c, (cx, cy, c)).wait_recv()
                started.append(copy(w, 3 + k, outs[w].at[chip, c], chip, c, (x, y, 1 - c)))
                started[-1].start()
        for w in range(n_w):
            for k, (cx, cy) in enumerate(chips):
                copy(w, 3 + k, ins[w].at[c], 2 * cx + cy, 1 - c, (x, y, 1 - c)).wait_recv()
        for cp in started:
            cp.wait_send()

    outs = _hbm_call(name, body, shards, [jax.ShapeDtypeStruct((N_CHIPS,) + s.shape, s.dtype) for s in shards], 6 * n_w)
    my_chip = 2 * lax.axis_index("x") + lax.axis_index("y")
    return [_put_own(o, s, (my_chip,)) for o, s in zip(outs, shards)]


def _swap_layers(name, grads):
    n_w = len(grads)

    def body(*refs):
        ins, outs = refs[:n_w], refs[n_w:2 * n_w]
        send_sems, recv_sems = refs[2 * n_w], refs[2 * n_w + 1]
        x, y, c = lax.axis_index("x"), lax.axis_index("y"), lax.axis_index("c")
        copies = [pltpu.make_async_remote_copy(src_ref=ins[w].at[1 - c], dst_ref=outs[w], send_sem=send_sems.at[w],
                                               recv_sem=recv_sems.at[w], device_id=(x, y, 1 - c), device_id_type=MESH)
                  for w in range(n_w)]
        for cp in copies:
            cp.start()
        for cp in copies:
            cp.wait()

    return _hbm_call(name, body, grads, [jax.ShapeDtypeStruct(g.shape[1:], g.dtype) for g in grads], n_w)


def _chip_exchange(name, parts):
    n_w = len(parts)

    def body(*refs):
        ins, outs = refs[:n_w], refs[n_w:2 * n_w]
        send_sems, recv_sems = refs[2 * n_w], refs[2 * n_w + 1]
        x, y, c = lax.axis_index("x"), lax.axis_index("y"), lax.axis_index("c")
        my_chip = 2 * x + y
        chips = [(1 - x, y), (x, 1 - y), (1 - x, 1 - y)]
        copies = [pltpu.make_async_remote_copy(
            src_ref=ins[w].at[2 * cx + cy], dst_ref=outs[w].at[my_chip], send_sem=send_sems.at[3 * w + k],
            recv_sem=recv_sems.at[3 * w + k], device_id=(cx, cy, c), device_id_type=MESH)
            for w in range(n_w) for k, (cx, cy) in enumerate(chips)]
        for cp in copies:
            cp.start()
        for cp in copies:
            cp.wait()

    outs = _hbm_call(name, body, parts, [jax.ShapeDtypeStruct(p.shape, p.dtype) for p in parts], 3 * n_w)
    my_chip = 2 * lax.axis_index("x") + lax.axis_index("y")
    return [_put_own(o, lax.dynamic_index_in_dim(p, my_chip, 0, keepdims=False), (my_chip,)) for o, p in zip(outs, parts)]


def _share_layers(name, reduced):
    n_w = len(reduced)

    def body(*refs):
        ins, outs = refs[:n_w], refs[n_w:2 * n_w]
        send_sems, recv_sems = refs[2 * n_w], refs[2 * n_w + 1]
        x, y, c = lax.axis_index("x"), lax.axis_index("y"), lax.axis_index("c")
        copies = [pltpu.make_async_remote_copy(src_ref=ins[w], dst_ref=outs[w].at[c], send_sem=send_sems.at[w],
                                               recv_sem=recv_sems.at[w], device_id=(x, y, 1 - c), device_id_type=MESH)
                  for w in range(n_w)]
        for cp in copies:
            cp.start()
        for cp in copies:
            cp.wait()

    outs = _hbm_call(name, body, reduced, [jax.ShapeDtypeStruct((2,) + r.shape, r.dtype) for r in reduced], n_w)
    return [_put_own(o, r, (lax.axis_index("c"),)) for o, r in zip(outs, reduced)]


def _pack(arrays, cols, row_multiple, dtype):
    flat = jnp.concatenate([a.reshape(-1).astype(dtype) for a in arrays])
    unit = cols * row_multiple
    total = -(-flat.shape[0] // unit) * unit
    return jnp.pad(flat, (0, total - flat.shape[0])).reshape(total // cols, cols)


def _unpack(buf, shapes):
    flat, out, off = buf.reshape(-1), [], 0
    for sh in shapes:
        n = math.prod(sh)
        out.append(flat[off:off + n].reshape(sh))
        off += n
    return out


def _discretize(lam_re, lam_im, log_dt, b_re, b_im):
    lam = lax.complex(jnp.minimum(lam_re, -EIG_CLIP), lam_im)
    dt = jnp.exp(log_dt)[:, None]
    lam_bar = jnp.exp(lam * dt)
    b_bar = ((lam_bar - 1.0) / lam)[..., None] * lax.complex(b_re, b_im)
    return jnp.real(lam_bar), jnp.imag(lam_bar), jnp.real(b_bar), jnp.imag(b_bar)


def _scan_tables(ar, ai):
    a = lax.complex(ar, ai)
    pw = [a]
    for _ in range(7):
        pw.append(pw[-1] * a)
    rows = jnp.arange(SUBLANES)[:, None]

    def build(p, reverse):
        tabs = []
        for k in (1, 2, 4):
            keep = (rows <= SUBLANES - 1 - k) if reverse else (rows >= k)
            tk = jnp.where(keep, p[k - 1][None, :], 0.0)
            tabs += [jnp.real(tk), jnp.imag(tk)]
        stack = jnp.stack(p[::-1] if reverse else p)
        tabs += [jnp.real(stack), jnp.imag(stack)]
        return jnp.stack(tabs).astype(F32)

    return build(pw, False), build([jnp.conj(p) for p in pw], True)


def _block_diag(per_group, groups_per_block):
    g, a, b = per_group.shape
    x = per_group.reshape(g // groups_per_block, groups_per_block, a, b)
    eye = jnp.eye(groups_per_block, dtype=per_group.dtype)
    out = x[:, :, :, None, :] * eye[None, :, None, :, None]
    return out.reshape(g // groups_per_block, groups_per_block * a, groups_per_block * b)


def _block_diag_extract(dense, groups_per_block, a, b):
    nkb = dense.shape[0]
    x = dense.reshape(nkb, groups_per_block, a, groups_per_block, b)
    idx = jnp.arange(groups_per_block)
    return x[:, idx, :, idx, :].transpose(1, 0, 2, 3).reshape(nkb * groups_per_block, a, b)


def _layer_fwd(tag, l, x, mod, p, wts):
    s, d = x.shape
    w_ssm, w_att = p["w_glu"].shape[0], wts["w_pb"].shape[2]
    heads = p["b_f"].shape[0]
    dh = w_att // heads
    cs = d // N_CHIPS
    fs = wts["w_ffn_down"].shape[2]
    tm = _pick(s, 512)
    row = lambda v: v.reshape(1, -1)
    sv = {}

    h = _prenorm_fwd(f"prenorm_mix_{tag}", x, row(p["g_pre_mix"]), row(mod[1]), row(mod[0]))
    uqkv = _mm_plain(f"proj_main_{tag}", h, p["w_main"], "nn", BF16, tn=1024)
    fg = _mm_plain(f"proj_gate_{tag}", h, p["w_gates"], "nn", F32, tn=1024)
    f_t = fg[:, 2 * d:2 * d + heads].T

    y_s5, ys, carries = _s5_fwd(f"s5_fwd_{tag}", uqkv, p["b_blk"], p["c_blk"], p["tab_f"], row(p["d_skip"]),
                                p["w_glu"], row(p["b_glu"]))

    assert dh * 2 == LANES and w_ssm % LANES == 0 and w_att % LANES == 0
    n_pairs = w_att // LANES
    blocks = (w_ssm // LANES, w_ssm // LANES + n_pairs, w_ssm // LANES + 2 * n_pairs)
    cum = _cum_fwd(f"cum_fwd_{tag}", f_t, p["b_f"].reshape(heads, 1))
    t = min(ATT_BLOCK, s)
    ck_cols, ck_rows = cum.reshape(heads, s, 1), cum.reshape(heads, s // t, 1, t)
    ya, lse = _attn_fwd(f"attn_fwd_{tag}", uqkv, *blocks, n_pairs, ck_rows)

    tile = pl.BlockSpec((tm, cs), lambda i, j, k: (i, j))
    slab = lambda rows: pl.BlockSpec((None, None, rows, cs), lambda i, j, k: (j, l, 0, 0))

    def merge(acc, extra_refs, out_refs):
        ya_ref, wpb_ref, ga_ref, gb_ref = extra_refs
        a_ref, b_ref, m_ref = out_refs
        bv = _dot(ya_ref[...], wpb_ref[...], NN)
        a_ref[...] = acc.astype(BF16)
        b_ref[...] = bv.astype(BF16)
        m_ref[...] = (_sigmoid(ga_ref[...]) * acc + _sigmoid(gb_ref[...]) * bv).astype(BF16)

    sd_bf = jax.ShapeDtypeStruct((s, d), BF16)
    pa, pb, merged = _mm_raw(
        f"merge_{tag}", ys, wts["w_pa"], "nn", (s // tm, N_CHIPS, 1), (tm, cs),
        pl.BlockSpec((tm, w_ssm), lambda i, j, k: (i, 0)), slab(w_ssm), [sd_bf] * 3, [tile] * 3, merge,
        extra=(ya, wts["w_pb"], fg, fg),
        extra_specs=[pl.BlockSpec((tm, w_att), lambda i, j, k: (i, 0)), slab(w_att), tile,
                     pl.BlockSpec((tm, cs), lambda i, j, k: (i, j + N_CHIPS))])

    tm2 = _pick(s, 256)
    x1, y_mix = _mm_postnorm(
        f"out_proj_{tag}", merged, pl.BlockSpec((tm2, cs), lambda i, j, k: (i, k)), wts["w_o"],
        pl.BlockSpec((None, None, cs, d), lambda i, j, k: (k, l, 0, 0)), N_CHIPS, x, row(mod[2]), row(p["g_post_mix"]))

    h2 = _prenorm_fwd(f"prenorm_ffn_{tag}", x1, row(p["g_pre_ffn"]), row(mod[4]), row(mod[3]))
    a4, b4, hid4 = _ffn_up(f"ffn_up_{tag}", h2, wts["w_ffn_gate"], wts["w_ffn_up"], l)
    x2, y_ffn = _mm_postnorm(
        f"ffn_down_{tag}", hid4, pl.BlockSpec((None, tm2, fs), lambda i, j, k: (k, i, 0)), wts["w_ffn_down"],
        pl.BlockSpec((None, None, fs, d), lambda i, j, k: (k, l, 0, 0)), N_CHIPS, x1, row(mod[5]), row(p["g_post_ffn"]))

    sv.update(x=x, h=h, uqkv=uqkv, fg=fg, f_t=f_t, y_s5=y_s5, ys=ys, carries=carries, blocks=blocks,
              ck_cols=ck_cols, lse_rows=lse.reshape(heads, s // t, 1, t), ya=ya, pa=pa, pb=pb, merged=merged, x1=x1,
              y_mix=y_mix, h2=h2, a4=a4, b4=b4, hid4=hid4, y_ffn=y_ffn)
    return x2, sv


def _mm_postnorm(name, a, a_spec, w, w_spec, nk, x, gate, g):
    s, d = x.shape
    tm = _pick(s, 256)
    rowspec = pl.BlockSpec((tm, d), lambda i, j, k: (i, 0))
    vec = pl.BlockSpec((1, d), lambda i, j, k: (0, 0))

    def epilogue(acc, extra_refs, out_refs):
        x_ref, gate_ref, g_ref = extra_refs
        r = lax.rsqrt(jnp.mean(acc * acc, axis=-1, keepdims=True) + RMS_EPS)
        out_refs[0][...] = x_ref[...] + gate_ref[...] * (acc * r * g_ref[...])
        out_refs[1][...] = acc

    sd = jax.ShapeDtypeStruct((s, d), F32)
    return _mm_raw(name, a, w, "nn", (s // tm, 1, nk), (tm, d), a_spec, w_spec, [sd, sd], [rowspec, rowspec], epilogue,
                   extra=(x, gate, g), extra_specs=[rowspec, vec, vec])


def _layer_bwd(tag, l, dx2, mod, p, wts, sv):
    s, d = dx2.shape
    w_ssm, w_att = p["w_glu"].shape[0], wts["w_pb"].shape[2]
    heads = p["b_f"].shape[0]
    cs = d // N_CHIPS
    fs = wts["w_ffn_down"].shape[2]
    tm, tk, td = _pick(s, 512), _pick(s, 512), _pick(d, 512)
    row = lambda v: v.reshape(1, -1)
    gr = {}

    def dw_slabs(name, act, act_spec, rows, dy, dy_spec, cols, grid_mn, out_index):
        return _mm_raw(name, act, dy, "tn", grid_mn + (s // tk,), (rows, cols), act_spec, dy_spec,
                       [jax.ShapeDtypeStruct((N_CHIPS,) + out_index[1], BF16)],
                       [pl.BlockSpec((None, rows, cols), out_index[0])], _store(BF16))[0]

    dy_ffn, sums = _postnorm_bwd(f"postnorm_bwd_ffn_{tag}", dx2, sv["y_ffn"], row(p["g_post_ffn"]), row(mod[5]))
    d_gate_f, gr["g_post_ffn"] = sums[0], sums[1]
    gr["w_ffn_down"] = dw_slabs(f"dw_down_{tag}", sv["hid4"], pl.BlockSpec((None, tk, fs), lambda i, j, k: (i, k, 0)), fs,
                                dy_ffn, pl.BlockSpec((tk, d), lambda i, j, k: (k, 0)), d, (N_CHIPS, 1),
                                (lambda i, j, k: (i, 0, 0), (fs, d)))

    def swiglu_bwd(acc, extra_refs, out_refs):
        av, bv = extra_refs[0][...].astype(F32), extra_refs[1][...].astype(F32)
        sg = _sigmoid(av)
        out_refs[0][...] = (acc * bv * (sg * (1.0 + av * (1.0 - sg)))).astype(BF16)
        out_refs[1][...] = (acc * (av * sg)).astype(BF16)

    blk4 = pl.BlockSpec((None, tm, fs), lambda i, j, k: (j, i, 0))
    sh4 = jax.ShapeDtypeStruct((N_CHIPS, s, fs), BF16)
    da4, db4 = _mm_raw(f"ffn_down_bwd_{tag}", dy_ffn, wts["w_ffn_down"], "nt", (s // tm, N_CHIPS, 1), (tm, fs),
                       pl.BlockSpec((tm, d), lambda i, j, k: (i, 0)),
                       pl.BlockSpec((None, None, fs, d), lambda i, j, k: (j, l, 0, 0)),
                       [sh4, sh4], [blk4, blk4], swiglu_bwd, extra=(sv["a4"], sv["b4"]), extra_specs=[blk4, blk4])
    for n, act4 in (("w_ffn_gate", da4), ("w_ffn_up", db4)):
        gr[n] = dw_slabs(f"d{n}_{tag}", sv["h2"], pl.BlockSpec((tk, td), lambda i, j, k: (k, i)), td,
                         act4, pl.BlockSpec((None, tk, fs), lambda i, j, k: (j, k, 0)), fs, (d // td, N_CHIPS),
                         (lambda i, j, k: (j, i, 0), (d, fs)))
    pairs = [(act4, (None, tm, fs), lambda i, kk: (kk, i, 0), wts[n], (None, None, td, fs), lambda j, kk: (kk, l, j, 0),
              N_CHIPS) for n, act4 in (("w_ffn_gate", da4), ("w_ffn_up", db4))]
    dh2 = _mm_sum(f"dh_ffn_{tag}", s, d, tm, td, pairs, F32)
    dx1, sums = _prenorm_bwd(f"prenorm_bwd_ffn_{tag}", dh2, sv["x1"], row(p["g_pre_ffn"]), row(mod[4]), dx2)
    d_scale_f, d_shift_f, gr["g_pre_ffn"] = sums[0], sums[1], sums[2]

    dy_mix, sums = _postnorm_bwd(f"postnorm_bwd_mix_{tag}", dx1, sv["y_mix"], row(p["g_post_mix"]), row(mod[2]))
    d_gate_m, gr["g_post_mix"] = sums[0], sums[1]
    gr["w_o"] = dw_slabs(f"dw_o_{tag}", sv["merged"], pl.BlockSpec((tk, cs), lambda i, j, k: (k, i)), cs,
                         dy_mix, pl.BlockSpec((tk, d), lambda i, j, k: (k, 0)), d, (N_CHIPS, 1),
                         (lambda i, j, k: (i, 0, 0), (cs, d)))

    tile = pl.BlockSpec((tm, cs), lambda i, j, k: (i, j))

    def merge_bwd(acc, extra_refs, out_refs):
        a_ref, b_ref, ga_ref, gb_ref = extra_refs
        sa, sb = _sigmoid(ga_ref[...]), _sigmoid(gb_ref[...])
        out_refs[0][...] = (acc * sa).astype(BF16)
        out_refs[1][...] = (acc * sb).astype(BF16)
        out_refs[2][...] = (acc * a_ref[...].astype(F32) * sa * (1.0 - sa)).astype(BF16)
        out_refs[3][...] = (acc * b_ref[...].astype(F32) * sb * (1.0 - sb)).astype(BF16)

    sd_bf = jax.ShapeDtypeStruct((s, d), BF16)
    d_pa, d_pb, d_ga, d_gb = _mm_raw(
        f"out_proj_bwd_{tag}", dy_mix, wts["w_o"], "nt", (s // tm, N_CHIPS, 1), (tm, cs),
        pl.BlockSpec((tm, d), lambda i, j, k: (i, 0)), pl.BlockSpec((None, None, cs, d), lambda i, j, k: (j, l, 0, 0)),
        [sd_bf] * 4, [tile] * 4, merge_bwd, extra=(sv["pa"], sv["pb"], sv["fg"], sv["fg"]),
        extra_specs=[tile, tile, tile, pl.BlockSpec((tm, cs), lambda i, j, k: (i, j + N_CHIPS))])
    d_branch = {}
    for n, act, width, d_p in (("w_pa", sv["ys"], w_ssm, d_pa), ("w_pb", sv["ya"], w_att, d_pb)):
        gr[n] = dw_slabs(f"d{n}_{tag}", act, pl.BlockSpec((tk, width), lambda i, j, k: (k, 0)), width,
                         d_p, pl.BlockSpec((tk, cs), lambda i, j, k: (k, j)), cs, (1, N_CHIPS),
                         (lambda i, j, k: (j, 0, 0), (width, cs)))
        d_branch[n] = _mm_raw(
            f"d_in_{n}_{tag}", d_p, wts[n], "nt", (s // tm, 1, N_CHIPS), (tm, width),
            pl.BlockSpec((tm, cs), lambda i, j, k: (i, k)), pl.BlockSpec((None, None, width, cs), lambda i, j, k: (k, l, 0, 0)),
            [jax.ShapeDtypeStruct((s, width), BF16)], [pl.BlockSpec((tm, width), lambda i, j, k: (i, 0))], _store(BF16))[0]
    d_ys, d_ya = d_branch["w_pa"], d_branch["w_pb"]

    dq, dk, dv, dcq, dck = _attn_bwd(f"attn_bwd_{tag}", sv["uqkv"], *sv["blocks"], w_att // LANES, sv["ya"], d_ya,
                                     sv["lse_rows"], sv["ck_cols"])
    d_f_t, d_bf = _cum_bwd(f"cum_bwd_{tag}", dcq.reshape(heads, s), dck.reshape(heads, s), sv["f_t"],
                           p["b_f"].reshape(heads, 1))
    gr["b_f"] = d_bf[:, 0]

    du, d_bblk, d_cblk, d_abar, d_wglu, vec = _s5_bwd(
        f"s5_bwd_{tag}", sv["uqkv"], d_ys, sv["y_s5"], sv["carries"], p["b_blk"], p["c_blk"], p["tab_f"], p["tab_r"],
        row(p["d_skip"]), p["w_glu"], row(p["b_glu"]))
    gr["w_glu"] = d_wglu.astype(BF16).reshape(N_CHIPS, w_ssm // N_CHIPS, w_ssm)
    gr["b_glu"], gr["d_skip"] = vec[0], vec[1]
    gr["b_blk"], gr["c_blk"], gr["a_bar"] = d_bblk, d_cblk, d_abar

    d_f = jnp.pad(d_f_t.T, ((0, 0), (0, F_PAD - heads))).astype(BF16)
    assert w_ssm % w_att == 0 and (2 * d) % F_PAD == 0
    first = w_ssm // w_att
    main_pieces = [(du, w_ssm, 0), (dq, w_att, first), (dk, w_att, first + 1), (dv, w_att, first + 2)]
    dw = [_mm_plain(f"dw_in{n}_{tag}", sv["h"], piece, "tn", BF16, tk=1024)
          for n, piece in enumerate([du, dq, dk, dv, d_f, d_ga, d_gb])]
    w_in_grad = jnp.concatenate(dw[:4] + [dw[4][:, :heads], dw[5], dw[6]], axis=1)
    gr["w_in"] = w_in_grad.reshape(d, N_CHIPS, w_in_grad.shape[1] // N_CHIPS).transpose(1, 0, 2)
    pairs = [(piece, (tm, width), lambda i, kk: (i, 0), p["w_main"], (td, width), lambda j, kk, blk=blk: (j, blk), 1)
             for piece, width, blk in main_pieces]
    steps = d // td
    pairs += [(piece, (tm, td), lambda i, kk: (i, kk), p["w_gates"], (td, td), lambda j, kk, off=off: (j, off + kk), steps)
              for piece, off in ((d_ga, 0), (d_gb, steps))]
    pairs.append((d_f, (tm, F_PAD), lambda i, kk: (i, 0), p["w_gates"], (td, F_PAD), lambda j, kk: (j, 2 * d // F_PAD), 1))
    dh1 = _mm_sum(f"dh_mix_{tag}", s, d, tm, td, pairs, F32)
    dx0, sums = _prenorm_bwd(f"prenorm_bwd_mix_{tag}", dh1, sv["x"], row(p["g_pre_mix"]), row(mod[1]), dx1)
    d_scale_m, d_shift_m, gr["g_pre_mix"] = sums[0], sums[1], sums[2]

    d_mod = jnp.stack([d_shift_m, d_scale_m, d_gate_m, d_shift_f, d_scale_f, d_gate_f])
    return dx0, d_mod, gr


BIG = ("w_in", "w_glu", "w_pa", "w_pb", "w_o", "w_ffn_gate", "w_ffn_up", "w_ffn_down")
SMALL = ("b_ada", "g_pre_mix", "g_post_mix", "g_pre_ffn", "g_post_ffn", "lam_re", "lam_im", "log_dt", "b_re", "b_im",
         "c_re", "c_im", "d_skip", "b_glu", "b_f")
WEIGHTS = ("w_ada", "b_ada", "g_pre_mix", "g_post_mix", "g_pre_ffn", "g_post_ffn", "w_in", "lam_re", "lam_im", "log_dt",
           "b_re", "b_im", "c_re", "c_im", "d_skip", "w_glu", "b_glu", "b_f", "w_pa", "w_pb", "w_o", "w_ffn_gate",
           "w_ffn_up", "w_ffn_down")


def _prepare_layer(wts, small, l):
    w_in = jnp.concatenate([wts["w_in"][j, l] for j in range(N_CHIPS)], axis=1)
    d = w_in.shape[0]
    heads = small["b_f"].shape[1]
    n_groups, n_state, group_ch = small["b_re"].shape[1:]
    w_ssm = n_groups * group_ch
    w_att = wts["w_pb"].shape[2]
    n_main = w_ssm + 3 * w_att
    gpb = LANES // group_ch
    p = {}
    p["w_main"] = w_in[:, :n_main]
    p["w_gates"] = jnp.concatenate(
        [w_in[:, n_main + heads:], w_in[:, n_main:n_main + heads], jnp.zeros((d, F_PAD - heads), BF16)], axis=1)
    p["w_glu"] = wts["w_glu"][:, l].reshape(w_ssm, w_ssm)
    for n in ("g_pre_mix", "g_post_mix", "g_pre_ffn", "g_post_ffn", "d_skip", "b_glu", "b_f"):
        p[n] = small[n][l]
    ar, ai, br, bi = _discretize(small["lam_re"][l], small["lam_im"][l], small["log_dt"][l], small["b_re"][l], small["b_im"][l])
    p["tab_f"], p["tab_r"] = _scan_tables(ar.reshape(-1), ai.reshape(-1))
    bre = _block_diag(br.transpose(0, 2, 1), gpb)
    bim = _block_diag(bi.transpose(0, 2, 1), gpb)
    p["b_blk"] = jnp.concatenate([bre, bim], axis=2).astype(BF16)
    cre = _block_diag(small["c_re"][l].transpose(0, 2, 1), gpb)
    cim = _block_diag(small["c_im"][l].transpose(0, 2, 1), gpb)
    p["c_blk"] = jnp.concatenate([cre, -cim], axis=1).astype(BF16)
    return p


def _compact_partials(gr, n_state, group_ch):
    gpb = LANES // group_ch
    half = gpb * n_state
    out = dict(gr)
    out["bbar_re"] = _block_diag_extract(gr["b_blk"][:, :, :half], gpb, group_ch, n_state).transpose(0, 2, 1)
    out["bbar_im"] = _block_diag_extract(gr["b_blk"][:, :, half:], gpb, group_ch, n_state).transpose(0, 2, 1)
    out["c_re"] = _block_diag_extract(gr["c_blk"][:, :half, :], gpb, n_state, group_ch).transpose(0, 2, 1)
    out["c_im"] = -_block_diag_extract(gr["c_blk"][:, half:, :], gpb, n_state, group_ch).transpose(0, 2, 1)
    return out


def _small_grads_from_partials(gr, small, l):
    n_groups, n_state, _ = small["b_re"].shape[1:]
    ns2 = n_groups * n_state
    d_abar = jnp.sum(gr["a_bar"], axis=0)
    dar, dai = d_abar[:ns2].reshape(n_groups, n_state), d_abar[ns2:].reshape(n_groups, n_state)
    args = (small["lam_re"][l], small["lam_im"][l], small["log_dt"][l], small["b_re"][l], small["b_im"][l])
    _, vjp = jax.vjp(_discretize, *args)
    d_lam_re, d_lam_im, d_log_dt, d_b_re, d_b_im = vjp((dar, dai, gr["bbar_re"], gr["bbar_im"]))
    return dict(lam_re=d_lam_re, lam_im=d_lam_im, log_dt=d_log_dt, b_re=d_b_re, b_im=d_b_im,
                c_re=gr["c_re"], c_im=gr["c_im"])


def _fwd_bwd(xs, target, mods, layers, wts):
    depth = len(layers)
    saved = []
    act = xs
    for l in range(depth):
        act, sv = _layer_fwd(str(l), l, act, mods[l], layers[l], wts)
        saved.append(sv)
    dx, loss_blk = _loss_grad("loss", act, target)
    grads, d_mods = [None] * depth, [None] * depth
    for l in reversed(range(depth)):
        dx, d_mods[l], grads[l] = _layer_bwd(str(l), l, dx, mods[l], layers[l], wts, saved[l])
    stacked = {n: jnp.stack([grads[l][n] for l in range(depth)]) for n in BIG}
    return loss_blk, dx, d_mods, grads, stacked


def kernel(x, c, w_ada, b_ada, g_pre_mix, g_post_mix, g_pre_ffn, g_post_ffn, w_in, lam_re, lam_im, log_dt, b_re, b_im, c_re, c_im, d_skip, w_glu, b_glu, b_f, w_pa, w_pb, w_o, w_ffn_gate, w_ffn_up, w_ffn_down, loss_target, m_w_ada, m_b_ada, m_g_pre_mix, m_g_post_mix, m_g_pre_ffn, m_g_post_ffn, m_w_in, m_lam_re, m_lam_im, m_log_dt, m_b_re, m_b_im, m_c_re, m_c_im, m_d_skip, m_w_glu, m_b_glu, m_b_f, m_w_pa, m_w_pb, m_w_o, m_w_ffn_gate, m_w_ffn_up, m_w_ffn_down, v_w_ada, v_b_ada, v_g_pre_mix, v_g_post_mix, v_g_pre_ffn, v_g_post_ffn, v_w_in, v_lam_re, v_lam_im, v_log_dt, v_b_re, v_b_im, v_c_re, v_c_im, v_d_skip, v_w_glu, v_b_glu, v_b_f, v_w_pa, v_w_pb, v_w_o, v_w_ffn_gate, v_w_ffn_up, v_w_ffn_down):
    local = dict(locals())
    weights = {n: local[n] for n in WEIGHTS}
    moments_m = {n: local["m_" + n] for n in WEIGHTS}
    moments_v = {n: local["v_" + n] for n in WEIGHTS}
    depth, d = g_pre_mix.shape
    n_mod = w_ada.shape[2] * N_CHIPS // d
    mx, my, mc = lax.axis_index("x"), lax.axis_index("y"), lax.axis_index("c")
    my_chip = 2 * mx + my
    my_dev = 4 * mx + 2 * my + mc
    xs = x[0]

    assert depth == 2, "each core of a chip moves and reduces one layer"
    wts = dict(zip(BIG, _gather_weights("gather_weights", [weights[n].astype(BF16) for n in BIG])))
    small = {n: weights[n] for n in SMALL}
    layers = [_prepare_layer(wts, small, l) for l in range(depth)]

    c_pad = jnp.pad(c, ((0, SUBLANES - 1), (0, 0)))
    c_all = _all_gather("gather_cond", c_pad).reshape(N_DEV, SUBLANES, d)[:, 0, :]
    silu = lambda v: v * _sigmoid(v)
    n_cols = w_ada.shape[2]
    mod_shard = []
    for l in range(depth):
        bias = lax.dynamic_slice_in_dim(b_ada[l], my_chip * n_cols, n_cols)
        mod_shard.append(_mm_plain(f"ada_{l}", c_all, w_ada[l], "nn", F32, add=jnp.broadcast_to(bias, (N_DEV, n_cols)),
                                   a_fn=silu, tm=N_DEV, tn=512, tk=1024))
    mod_block = jnp.concatenate(mod_shard, axis=1)
    mod_all = _all_gather("gather_mod", mod_block).reshape(N_DEV, N_DEV, depth, n_cols)
    mod_rows = lax.dynamic_index_in_dim(mod_all[0::2], my_dev, axis=1, keepdims=False)
    mods = [mod_rows[:, l, :].reshape(n_mod, d) for l in range(depth)]

    loss_blk, dx, d_mods, grads, stacked = _fwd_bwd(xs, loss_target[0], mods, layers, wts)
    loss = lax.psum(loss_blk[0, 0], ("x", "y", "c"))
    grad_x = dx[None]

    core = mc.astype(jnp.int32).reshape(1)
    partials = [stacked[n] for n in BIG]
    from_sibling = _swap_layers("grads_swap_cores", partials)
    chip_parts = [_add_layer(f"grads_add_{n}", g, r, core) for n, g, r in zip(BIG, partials, from_sibling)]
    from_chips = _chip_exchange("grads_exchange_chips", chip_parts)
    reduced = [_sum_blocks(f"grads_sum_{n}", r, F32) for n, r in zip(BIG, from_chips)]
    big_grads = dict(zip(BIG, _share_layers("grads_share_cores", reduced)))

    partial_names = ("g_pre_mix", "g_post_mix", "g_pre_ffn", "g_post_ffn", "d_skip", "b_glu", "b_f", "a_bar",
                     "bbar_re", "bbar_im", "c_re", "c_im")
    n_state, group_ch = b_re.shape[2:]
    contrib = list(d_mods)
    for l in range(depth):
        compact = _compact_partials(grads[l], n_state, group_ch)
        contrib += [compact[n] for n in partial_names]
    contrib_shapes = [a.shape for a in contrib]
    block = _pack(contrib, LANES, BF16_ROWS, F32)
    rows = block.shape[0]
    all_blocks = _all_gather("gather_small_grads", block).reshape(N_DEV, rows, LANES)
    summed = _unpack(_sum_blocks("sum_small_grads", all_blocks, F32), contrib_shapes)
    per_layer = len(partial_names)
    small_grads = {n: [] for n in SMALL}
    d_mod_all = []
    for l in range(depth):
        small_grads["b_ada"].append(summed[l].reshape(-1))
        gl = dict(zip(partial_names, summed[depth + l * per_layer:depth + (l + 1) * per_layer]))
        for n in ("g_pre_mix", "g_post_mix", "g_pre_ffn", "g_post_ffn", "d_skip", "b_glu", "b_f"):
            small_grads[n].append(gl[n])
        for n, gval in _small_grads_from_partials(gl, small, l).items():
            small_grads[n].append(gval)
        d_mod_all.append(all_blocks.reshape(N_DEV, rows * LANES)[:, l * n_mod * d:(l + 1) * n_mod * d])
    small_grads = {n: jnp.stack(v) for n, v in small_grads.items()}

    g_w_ada = []
    for l in range(depth):
        cols = lax.dynamic_slice_in_dim(d_mod_all[l], my_chip * n_cols, n_cols, axis=1)
        g_w_ada.append(_mm_plain(f"dw_ada_{l}", c_all, cols, "tn", F32, a_fn=silu, tm=512, tn=512, tk=N_DEV))
    all_grads = dict(big_grads)
    all_grads.update(small_grads)
    all_grads["w_ada"] = jnp.stack(g_w_ada)

    delta, new_m, new_v = {}, {}, {}
    for n in ("w_ada",) + BIG:
        shape = weights[n].shape
        two_d = lambda a: a.reshape(-1, shape[-1])
        dl, nm, nv = _adamw(f"adamw_{n}", two_d(weights[n]), two_d(all_grads[n]), two_d(moments_m[n]), two_d(moments_v[n]))
        delta[n], new_m[n], new_v[n] = dl.reshape(shape), nm.reshape(shape), nv.reshape(shape)
    small_shapes = [weights[n].shape for n in SMALL]
    packed = [_pack([src[n] for n in SMALL], LANES, SUBLANES, F32) for src in (weights, all_grads, moments_m, moments_v)]
    outs = _adamw("adamw_small", *packed)
    for dst, buf in zip((delta, new_m, new_v), outs):
        dst.update(dict(zip(SMALL, _unpack(buf, small_shapes))))

    return (loss, grad_x, *[all_grads[n] for n in WEIGHTS], *[delta[n] for n in WEIGHTS],
            *[new_m[n] for n in WEIGHTS], *[new_v[n] for n in WEIGHTS])
```

```python
import functools
import math

import jax
import jax.numpy as jnp
from jax import lax
from jax.experimental import pallas as pl
from jax.experimental.pallas import tpu as pltpu

F32 = jnp.float32
BF16 = jnp.bfloat16
MESH = pl.DeviceIdType.MESH

RMS_EPS = 1e-6
EIG_CLIP = 1e-4
ADAM_LR, ADAM_B1, ADAM_B2, ADAM_EPS, ADAM_WD, ADAM_STEP = 0.001, 0.9, 0.999, 1e-08, 0.01, 10

LANES = 128
SUBLANES = 8
VMEM_LIMIT = 56 * 1024 * 1024
S5_ROWS = 256
S5_CHUNK = 256
ATT_BLOCK = 512
F_PAD = 256
POSTNORM_ROWS = 512
N_CHIPS = 4
N_DEV = 8

NN = (((1,), (0,)), ((), ()))
NT = (((1,), (1,)), ((), ()))
TN = (((0,), (0,)), ((), ()))
_DN = {"nn": NN, "nt": NT, "tn": TN}


def _cparams(**kw):
    return pltpu.CompilerParams(vmem_limit_bytes=VMEM_LIMIT, **kw)


def _pick(dim, target):
    best, t = None, LANES
    while t <= min(dim, target):
        if dim % t == 0:
            best = t
        t += LANES
    return best or dim


def _sigmoid(x):
    return 1.0 / (1.0 + jnp.exp(-x))


def _dot(a, b, dn):
    return lax.dot_general(a, b, dn, preferred_element_type=F32)


def _mm_raw(name, a, b, mode, grid, acc_shape, a_spec, b_spec, out_shapes, out_specs, epilogue,
            extra=(), extra_specs=(), a_fn=None):
    nk = grid[2]
    n_extra, n_out = len(extra), len(out_shapes)

    def body(*refs):
        a_ref, b_ref = refs[0], refs[1]
        extra_refs = refs[2:2 + n_extra]
        out_refs = refs[2 + n_extra:2 + n_extra + n_out]
        acc = refs[-1]
        k = pl.program_id(2)

        @pl.when(k == 0)
        def _():
            acc[...] = jnp.zeros_like(acc)

        av = a_ref[...]
        if a_fn is not None:
            av = a_fn(av.astype(F32))
        acc[...] += _dot(av.astype(BF16), b_ref[...].astype(BF16), _DN[mode])

        @pl.when(k == nk - 1)
        def _():
            epilogue(acc[...], extra_refs, out_refs)

    return pl.pallas_call(
        body, name=name, grid=grid,
        in_specs=[a_spec, b_spec, *extra_specs],
        out_specs=list(out_specs), out_shape=list(out_shapes),
        scratch_shapes=[pltpu.VMEM(acc_shape, F32)],
        compiler_params=_cparams(),
    )(a, b, *extra)


def _mm(name, a, b, mode, out_shapes, out_specs, epilogue, extra=(), extra_specs=(),
        tm=512, tn=512, tk=512, a_fn=None):
    if mode == "nn":
        (m, kd), (_, n) = a.shape, b.shape
    elif mode == "nt":
        (m, kd), (n, _) = a.shape, b.shape
    else:
        (kd, m), (_, n) = a.shape, b.shape
    tm, tn, tk = _pick(m, tm), _pick(n, tn), _pick(kd, tk)
    if mode == "tn":
        a_spec = pl.BlockSpec((tk, tm), lambda i, j, k: (k, i))
    else:
        a_spec = pl.BlockSpec((tm, tk), lambda i, j, k: (i, k))
    if mode == "nt":
        b_spec = pl.BlockSpec((tn, tk), lambda i, j, k: (j, k))
    else:
        b_spec = pl.BlockSpec((tk, tn), lambda i, j, k: (k, j))
    res = _mm_raw(name, a, b, mode, (m // tm, n // tn, kd // tk), (tm, tn), a_spec, b_spec, out_shapes, out_specs,
                  epilogue, extra=extra, extra_specs=extra_specs, a_fn=a_fn)
    return res, (tm, tn, tk)


def _store(dtype):
    def epilogue(acc, extra_refs, out_refs):
        out_refs[0][...] = acc.astype(dtype)
    return epilogue


def _mm_sum(name, m, n, tm, tn, pairs, out_dtype):
    offs, total = [], 0
    for pr in pairs:
        offs.append(total)
        total += pr[6]
    n_p = len(pairs)

    def body(*refs):
        o_ref, acc = refs[2 * n_p], refs[2 * n_p + 1]
        k = pl.program_id(2)

        @pl.when(k == 0)
        def _():
            acc[...] = jnp.zeros_like(acc)

        for p_ in range(n_p):
            @pl.when((k >= offs[p_]) & (k < offs[p_] + pairs[p_][6]))
            def _(p_=p_):
                acc[...] += _dot(refs[2 * p_][...].astype(BF16), refs[2 * p_ + 1][...].astype(BF16), NT)

        @pl.when(k == total - 1)
        def _():
            o_ref[...] = acc[...].astype(out_dtype)

    in_specs, operands = [], []
    for (a, a_block, a_index, b, b_block, b_index, steps), off in zip(pairs, offs):
        local = lambda k, off=off, steps=steps: jnp.clip(k - off, 0, steps - 1)
        in_specs.append(pl.BlockSpec(a_block, lambda i, j, k, f=a_index, local=local: f(i, local(k))))
        in_specs.append(pl.BlockSpec(b_block, lambda i, j, k, f=b_index, local=local: f(j, local(k))))
        operands += [a, b]
    return pl.pallas_call(
        body, name=name, grid=(m // tm, n // tn, total), in_specs=in_specs,
        out_specs=pl.BlockSpec((tm, tn), lambda i, j, k: (i, j)), out_shape=jax.ShapeDtypeStruct((m, n), out_dtype),
        scratch_shapes=[pltpu.VMEM((tm, tn), F32)], compiler_params=_cparams(),
    )(*operands)


def _ffn_up(name, h, wg, wu, l):
    s, d = h.shape
    nc, fs = wg.shape[0], wg.shape[3]
    tm, tk = _pick(s, 1024), _pick(d, 1024)
    nk = d // tk

    def body(h_ref, wg_ref, wu_ref, a_ref, b_ref, hid_ref, acc_g, acc_u):
        k = pl.program_id(2)

        @pl.when(k == 0)
        def _():
            acc_g[...] = jnp.zeros_like(acc_g)
            acc_u[...] = jnp.zeros_like(acc_u)

        hv = h_ref[...]
        acc_g[...] += _dot(hv, wg_ref[...], NN)
        acc_u[...] += _dot(hv, wu_ref[...], NN)

        @pl.when(k == nk - 1)
        def _():
            av, bv = acc_g[...], acc_u[...]
            a_ref[...] = av.astype(BF16)
            b_ref[...] = bv.astype(BF16)
            hid_ref[...] = (av * _sigmoid(av) * bv).astype(BF16)

    w_spec = pl.BlockSpec((None, None, tk, fs), lambda i, j, k: (j, l, k, 0))
    o_spec = pl.BlockSpec((None, tm, fs), lambda i, j, k: (j, i, 0))
    sh = jax.ShapeDtypeStruct((nc, s, fs), BF16)
    return pl.pallas_call(
        body, name=name, grid=(s // tm, nc, nk),
        in_specs=[pl.BlockSpec((tm, tk), lambda i, j, k: (i, k)), w_spec, w_spec],
        out_specs=[o_spec] * 3, out_shape=[sh] * 3,
        scratch_shapes=[pltpu.VMEM((tm, fs), F32), pltpu.VMEM((tm, fs), F32)], compiler_params=_cparams(),
    )(h, wg, wu)


def _mm_plain(name, a, b, mode, out_dtype, add=None, a_fn=None, tm=512, tn=512, tk=512):
    if mode == "nn":
        m, n = a.shape[0], b.shape[1]
    elif mode == "nt":
        m, n = a.shape[0], b.shape[0]
    else:
        m, n = a.shape[1], b.shape[1]
    tm_, tn_ = _pick(m, tm), _pick(n, tn)
    spec = pl.BlockSpec((tm_, tn_), lambda i, j, k: (i, j))

    def epilogue(acc, extra_refs, out_refs):
        if add is not None:
            acc = acc + extra_refs[0][...]
        out_refs[0][...] = acc.astype(out_dtype)

    extra = () if add is None else (add,)
    (out,), _ = _mm(name, a, b, mode, [jax.ShapeDtypeStruct((m, n), out_dtype)], [spec], epilogue,
                    extra=extra, extra_specs=[spec] * len(extra), tm=tm, tn=tn, tk=tk, a_fn=a_fn)
    return out


def _row_tile(s, d):
    return _pick(s, max(SUBLANES, (1 << 20) // (4 * d)))


def _prenorm_fwd(name, x, g, scale, shift):
    s, d = x.shape
    tr = _row_tile(s, d)

    def body(x_ref, g_ref, sc_ref, sh_ref, h_ref):
        xv = x_ref[...]
        r = lax.rsqrt(jnp.mean(xv * xv, axis=-1, keepdims=True) + RMS_EPS)
        h_ref[...] = ((xv * r * g_ref[...]) * (1.0 + sc_ref[...]) + sh_ref[...]).astype(BF16)

    row = pl.BlockSpec((tr, d), lambda i: (i, 0))
    vec = pl.BlockSpec((1, d), lambda i: (0, 0))
    return pl.pallas_call(body, name=name, grid=(s // tr,), in_specs=[row, vec, vec, vec], out_specs=row,
                          out_shape=jax.ShapeDtypeStruct((s, d), BF16), compiler_params=_cparams())(x, g, scale, shift)


def _prenorm_bwd(name, dh, x, g, scale, dx_res):
    s, d = x.shape
    tr = _row_tile(s, d)

    def body(dh_ref, x_ref, g_ref, sc_ref, dxr_ref, dx_ref, sums_ref):
        @pl.when(pl.program_id(0) == 0)
        def _():
            sums_ref[...] = jnp.zeros_like(sums_ref)

        xv, dhv, gv = x_ref[...], dh_ref[...].astype(F32), g_ref[...]
        r = lax.rsqrt(jnp.mean(xv * xv, axis=-1, keepdims=True) + RMS_EPS)
        xhat = xv * r
        dxn = dhv * (1.0 + sc_ref[...])
        dxhat = dxn * gv
        dx = r * (dxhat - xhat * jnp.mean(dxhat * xhat, axis=-1, keepdims=True))
        dx_ref[...] = dxr_ref[...] + dx
        sums_ref[0:1, :] += jnp.sum(dhv * (xhat * gv), axis=0, keepdims=True)
        sums_ref[1:2, :] += jnp.sum(dhv, axis=0, keepdims=True)
        sums_ref[2:3, :] += jnp.sum(dxn * xhat, axis=0, keepdims=True)

    row = pl.BlockSpec((tr, d), lambda i: (i, 0))
    vec = pl.BlockSpec((1, d), lambda i: (0, 0))
    acc = pl.BlockSpec((SUBLANES, d), lambda i: (0, 0))
    return pl.pallas_call(
        body, name=name, grid=(s // tr,), in_specs=[row, row, vec, vec, row], out_specs=[row, acc],
        out_shape=[jax.ShapeDtypeStruct((s, d), F32), jax.ShapeDtypeStruct((SUBLANES, d), F32)],
        compiler_params=_cparams())(dh, x, g, scale, dx_res)


def _postnorm_bwd(name, dxn, y, g, gate):
    s, d = y.shape
    tr = _row_tile(s, d)

    def body(dx_ref, y_ref, g_ref, gt_ref, dy_ref, sums_ref):
        @pl.when(pl.program_id(0) == 0)
        def _():
            sums_ref[...] = jnp.zeros_like(sums_ref)

        yv, dxv, gv = y_ref[...], dx_ref[...], g_ref[...]
        r = lax.rsqrt(jnp.mean(yv * yv, axis=-1, keepdims=True) + RMS_EPS)
        yhat = yv * r
        dn = dxv * gt_ref[...]
        dyhat = dn * gv
        dy_ref[...] = (r * (dyhat - yhat * jnp.mean(dyhat * yhat, axis=-1, keepdims=True))).astype(BF16)
        sums_ref[0:1, :] += jnp.sum(dxv * (yhat * gv), axis=0, keepdims=True)
        sums_ref[1:2, :] += jnp.sum(dn * yhat, axis=0, keepdims=True)

    row = pl.BlockSpec((tr, d), lambda i: (i, 0))
    vec = pl.BlockSpec((1, d), lambda i: (0, 0))
    acc = pl.BlockSpec((SUBLANES, d), lambda i: (0, 0))
    return pl.pallas_call(
        body, name=name, grid=(s // tr,), in_specs=[row, row, vec, vec], out_specs=[row, acc],
        out_shape=[jax.ShapeDtypeStruct((s, d), BF16), jax.ShapeDtypeStruct((SUBLANES, d), F32)],
        compiler_params=_cparams())(dxn, y, g, gate)


def _loss_grad(name, y, target):
    s, d = y.shape
    tr = _row_tile(s, d)

    def body(y_ref, t_ref, dy_ref, loss_ref):
        @pl.when(pl.program_id(0) == 0)
        def _():
            loss_ref[...] = jnp.zeros_like(loss_ref)

        err = y_ref[...] - t_ref[...]
        dy_ref[...] = err * (1.0 / d)
        part = jnp.sum(jnp.sum(err * err, axis=-1, keepdims=True), axis=0, keepdims=True) * (0.5 / d)
        loss_ref[...] += jnp.broadcast_to(part, loss_ref.shape)

    row = pl.BlockSpec((tr, d), lambda i: (i, 0))
    acc = pl.BlockSpec((SUBLANES, LANES), lambda i: (0, 0))
    return pl.pallas_call(
        body, name=name, grid=(s // tr,), in_specs=[row, row], out_specs=[row, acc],
        out_shape=[jax.ShapeDtypeStruct((s, d), F32), jax.ShapeDtypeStruct((SUBLANES, LANES), F32)],
        compiler_params=_cparams())(y, target)


def _gelu(y):
    c = math.sqrt(2.0 / math.pi)
    return 0.5 * y * (1.0 + jnp.tanh(c * (y + 0.044715 * (y * y * y))))


def _gelu_grad(y):
    c = math.sqrt(2.0 / math.pi)
    th = jnp.tanh(c * (y + 0.044715 * (y * y * y)))
    return 0.5 * (1.0 + th) + 0.5 * y * (1.0 - th * th) * c * (1.0 + 3.0 * 0.044715 * (y * y))


def _scan_rows(x_ref, row0, n_groups, ns2, tab_ref, carry_ref, reverse, after_group=None, extra_init=None):
    wc = min(S5_CHUNK, ns2)
    shifts = (1, 2, 4)
    for c0 in range(0, ns2, wc):
        re = slice(c0, c0 + wc)
        im = slice(ns2 + c0, ns2 + c0 + wc)
        tabs = [tab_ref[k, :, re] for k in range(8)]

        def group(i, carry, re=re, im=im, tabs=tabs, c0=c0):
            cr, ci, extra = carry
            g = (n_groups - 1 - i) if reverse else i
            r0 = pl.multiple_of(row0 + g * SUBLANES, SUBLANES)
            br = x_ref[pl.ds(r0, SUBLANES), re]
            bi = x_ref[pl.ds(r0, SUBLANES), im]
            for lvl, k in enumerate(shifts):
                mr, mi = tabs[2 * lvl], tabs[2 * lvl + 1]
                sh = (SUBLANES - k) if reverse else k
                sr = pltpu.roll(br, sh, 0)
                si = pltpu.roll(bi, sh, 0)
                br, bi = br + mr * sr - mi * si, bi + mr * si + mi * sr
            apr, api = tabs[6], tabs[7]
            xr = br + apr * cr - api * ci
            xi = bi + apr * ci + api * cr
            x_ref[pl.ds(r0, SUBLANES), re] = xr
            x_ref[pl.ds(r0, SUBLANES), im] = xi
            if after_group is not None:
                extra = after_group(c0, r0, xr, xi, extra)
            if reverse:
                return xr[0:1, :], xi[0:1, :], extra
            return xr[SUBLANES - 1:SUBLANES, :], xi[SUBLANES - 1:SUBLANES, :], extra

        init_extra = extra_init(wc) if extra_init is not None else 0
        cr, ci, extra = lax.fori_loop(0, n_groups, group, (carry_ref[0:1, re], carry_ref[0:1, im], init_extra))
        carry_ref[0:1, re] = cr
        carry_ref[0:1, im] = ci
        if after_group is not None:
            after_group(c0, None, None, None, extra)


def _s5_fwd(name, u, b_blk, c_blk, tab_f, dskip, w_glu, b_glu):
    s, w = u.shape[0], w_glu.shape[0]
    nkb = w // LANES
    ns2 = b_blk.shape[2] // 2 * nkb
    half = ns2 // nkb
    t = min(S5_ROWS, s)
    nblk = s // t

    def body(u_ref, b_ref, c_ref, tab_ref, ds_ref, wg_ref, bg_ref, y_ref, ys_ref, cs_ref, xs, carry):
        @pl.when(pl.program_id(0) == 0)
        def _():
            carry[...] = jnp.zeros_like(carry)

        cs_ref[0] = carry[...]
        for kb in range(nkb):
            bu = _dot(u_ref[:, kb * LANES:(kb + 1) * LANES], b_ref[kb], NN)
            xs[:, kb * half:(kb + 1) * half] = bu[:, :half]
            xs[:, ns2 + kb * half:ns2 + (kb + 1) * half] = bu[:, half:]
        _scan_rows(xs, 0, t // SUBLANES, ns2, tab_ref, carry, reverse=False)
        for kb in range(nkb):
            cols = slice(kb * LANES, (kb + 1) * LANES)
            yk = _dot(xs[:, kb * half:(kb + 1) * half].astype(BF16), c_ref[kb, :half, :], NN)
            yk += _dot(xs[:, ns2 + kb * half:ns2 + (kb + 1) * half].astype(BF16), c_ref[kb, half:, :], NN)
            y_ref[:, cols] = yk + ds_ref[:, cols] * u_ref[:, cols].astype(F32)
        z = _gelu(y_ref[...])
        gate = _sigmoid(_dot(z.astype(BF16), wg_ref[...], NN) + bg_ref[...])
        ys_ref[...] = (z * gate).astype(BF16)

    row = pl.BlockSpec((t, w), lambda i: (i, 0))
    full = lambda shape: pl.BlockSpec(shape, lambda i: (0,) * len(shape))
    return pl.pallas_call(
        body, name=name, grid=(nblk,),
        in_specs=[row, full(b_blk.shape), full(c_blk.shape), full(tab_f.shape), full(dskip.shape),
                  full(w_glu.shape), full(b_glu.shape)],
        out_specs=[row, row, pl.BlockSpec((1, 1, 2 * ns2), lambda i: (i, 0, 0))],
        out_shape=[jax.ShapeDtypeStruct((s, w), F32), jax.ShapeDtypeStruct((s, w), BF16),
                   jax.ShapeDtypeStruct((nblk, 1, 2 * ns2), F32)],
        scratch_shapes=[pltpu.VMEM((t, 2 * ns2), F32), pltpu.VMEM((1, 2 * ns2), F32)],
        compiler_params=_cparams(),
    )(u, b_blk, c_blk, tab_f, dskip, w_glu, b_glu)


def _s5_bwd(name, u, dys, y, carries, b_blk, c_blk, tab_f, tab_r, dskip, w_glu, b_glu):
    s, w = u.shape[0], w_glu.shape[0]
    nkb = w // LANES
    ns2 = b_blk.shape[2] // 2 * nkb
    half = ns2 // nkb
    t = min(S5_ROWS, s)
    nblk = s // t
    ng = t // SUBLANES

    def body(u_ref, dys_ref, y_ref, cs_ref, b_ref, c_ref, tabf_ref, tabr_ref, ds_ref, wg_ref, bg_ref,
             du_ref, db_ref, dc_ref, da_ref, dwg_ref, vec_ref, xs, gs, dyv, fcarry, gcarry):
        @pl.when(pl.program_id(0) == 0)
        def _():
            db_ref[...] = jnp.zeros_like(db_ref)
            dc_ref[...] = jnp.zeros_like(dc_ref)
            da_ref[...] = jnp.zeros_like(da_ref)
            dwg_ref[...] = jnp.zeros_like(dwg_ref)
            vec_ref[...] = jnp.zeros_like(vec_ref)
            gcarry[...] = jnp.zeros_like(gcarry)

        yv = y_ref[...]
        z = _gelu(yv)
        zb = z.astype(BF16)
        gate = _sigmoid(_dot(zb, wg_ref[...], NN) + bg_ref[...])
        dout = dys_ref[...].astype(F32)
        dt = dout * z * gate * (1.0 - gate)
        dtb = dt.astype(BF16)
        dz = dout * gate + _dot(dtb, wg_ref[...], NT)
        dy = dz * _gelu_grad(yv)
        dyv[...] = dy
        dwg_ref[...] += _dot(zb, dtb, TN)
        vec_ref[0:1, :] += jnp.sum(dt, axis=0, keepdims=True)
        vec_ref[1:2, :] += jnp.sum(dy * u_ref[...].astype(F32), axis=0, keepdims=True)

        fcarry[...] = cs_ref[0]
        xs[0:SUBLANES, :] = jnp.broadcast_to(cs_ref[0], (SUBLANES, 2 * ns2))
        for kb in range(nkb):
            bu = _dot(u_ref[:, kb * LANES:(kb + 1) * LANES], b_ref[kb], NN)
            xs[SUBLANES:, kb * half:(kb + 1) * half] = bu[:, :half]
            xs[SUBLANES:, ns2 + kb * half:ns2 + (kb + 1) * half] = bu[:, half:]
        _scan_rows(xs, SUBLANES, ng, ns2, tabf_ref, fcarry, reverse=False)

        for kb in range(nkb):
            dyk = dyv[:, kb * LANES:(kb + 1) * LANES].astype(BF16)
            re = slice(kb * half, (kb + 1) * half)
            im = slice(ns2 + kb * half, ns2 + (kb + 1) * half)
            gs[:, re] = _dot(dyk, c_ref[kb, :half, :], NT)
            gs[:, im] = _dot(dyk, c_ref[kb, half:, :], NT)
            dc_ref[kb, :half, :] += _dot(xs[SUBLANES:, re].astype(BF16), dyk, TN)
            dc_ref[kb, half:, :] += _dot(xs[SUBLANES:, im].astype(BF16), dyk, TN)

        row_is_first = lax.broadcasted_iota(jnp.int32, (SUBLANES, min(S5_CHUNK, ns2)), 0) == 0

        def fold(c0, r0, gr, gi, acc):
            wc = min(S5_CHUNK, ns2)
            re = slice(c0, c0 + wc)
            im = slice(ns2 + c0, ns2 + c0 + wc)
            if r0 is None:
                da_ref[:, re] += acc[0]
                da_ref[:, im] += acc[1]
                return acc
            cur_r = xs[pl.ds(r0 + SUBLANES, SUBLANES), re]
            cur_i = xs[pl.ds(r0 + SUBLANES, SUBLANES), im]
            prv_r = xs[pl.ds(r0, SUBLANES), re]
            prv_i = xs[pl.ds(r0, SUBLANES), im]
            xpr = jnp.where(row_is_first, prv_r[SUBLANES - 1:SUBLANES, :], pltpu.roll(cur_r, 1, 0))
            xpi = jnp.where(row_is_first, prv_i[SUBLANES - 1:SUBLANES, :], pltpu.roll(cur_i, 1, 0))
            return acc[0] + gr * xpr + gi * xpi, acc[1] - gr * xpi + gi * xpr

        zero2 = lambda wc: (jnp.zeros((SUBLANES, wc), F32), jnp.zeros((SUBLANES, wc), F32))
        _scan_rows(gs, 0, ng, ns2, tabr_ref, gcarry, reverse=True, after_group=fold, extra_init=zero2)

        for kb in range(nkb):
            cols = slice(kb * LANES, (kb + 1) * LANES)
            re = slice(kb * half, (kb + 1) * half)
            im = slice(ns2 + kb * half, ns2 + (kb + 1) * half)
            uk = u_ref[:, cols]
            gr = gs[:, re].astype(BF16)
            gi = gs[:, im].astype(BF16)
            db_ref[kb, :, :half] += _dot(uk, gr, TN)
            db_ref[kb, :, half:] += _dot(uk, gi, TN)
            duk = _dot(gr, b_ref[kb, :, :half], NT) + _dot(gi, b_ref[kb, :, half:], NT)
            du_ref[:, cols] = (duk + ds_ref[:, cols] * dyv[:, cols]).astype(BF16)

    rev = lambda i: (nblk - 1 - i, 0)
    row = pl.BlockSpec((t, w), rev)
    full = lambda shape: pl.BlockSpec(shape, lambda i: (0,) * len(shape))
    return pl.pallas_call(
        body, name=name, grid=(nblk,),
        in_specs=[row, row, row, pl.BlockSpec((1, 1, 2 * ns2), lambda i: (nblk - 1 - i, 0, 0)),
                  full(b_blk.shape), full(c_blk.shape), full(tab_f.shape), full(tab_r.shape),
                  full(dskip.shape), full(w_glu.shape), full(b_glu.shape)],
        out_specs=[row, full(b_blk.shape), full(c_blk.shape), full((SUBLANES, 2 * ns2)), full((w, w)),
                   full((SUBLANES, w))],
        out_shape=[jax.ShapeDtypeStruct((s, w), BF16), jax.ShapeDtypeStruct(b_blk.shape, F32),
                   jax.ShapeDtypeStruct(c_blk.shape, F32), jax.ShapeDtypeStruct((SUBLANES, 2 * ns2), F32),
                   jax.ShapeDtypeStruct((w, w), F32), jax.ShapeDtypeStruct((SUBLANES, w), F32)],
        scratch_shapes=[pltpu.VMEM((t + SUBLANES, 2 * ns2), F32), pltpu.VMEM((t, 2 * ns2), F32),
                        pltpu.VMEM((t, w), F32), pltpu.VMEM((1, 2 * ns2), F32), pltpu.VMEM((1, 2 * ns2), F32)],
        compiler_params=_cparams(),
    )(u, dys, y, carries, b_blk, c_blk, tab_f, tab_r, dskip, w_glu, b_glu)


def _log_sigmoid(x):
    return jnp.minimum(x, 0.0) - jnp.log(1.0 + jnp.exp(-jnp.abs(x)))


def _cum_fwd(name, f_t, b_f):
    h, s = f_t.shape
    tc = _pick(s, 512)
    nb = s // tc

    def body(f_ref, b_ref, c_ref, carry):
        @pl.when(pl.program_id(0) == 0)
        def _():
            carry[...] = jnp.zeros_like(carry)

        lf = _log_sigmoid(f_ref[...] + b_ref[...])
        upper = (lax.broadcasted_iota(jnp.int32, (tc, tc), 0) <= lax.broadcasted_iota(jnp.int32, (tc, tc), 1))
        cum = lax.dot_general(lf, upper.astype(F32), NN, precision=lax.Precision.HIGHEST,
                              preferred_element_type=F32) + carry[...]
        c_ref[...] = cum
        carry[...] += jnp.sum(lf, axis=1, keepdims=True)

    blk = pl.BlockSpec((h, tc), lambda i: (0, i))
    return pl.pallas_call(body, name=name, grid=(nb,), in_specs=[blk, pl.BlockSpec((h, 1), lambda i: (0, 0))],
                          out_specs=blk, out_shape=jax.ShapeDtypeStruct((h, s), F32),
                          scratch_shapes=[pltpu.VMEM((h, 1), F32)], compiler_params=_cparams())(f_t, b_f)


def _cum_bwd(name, dcq, dck, f_t, b_f):
    h, s = f_t.shape
    tc = _pick(s, 512)
    nb = s // tc

    def body(dcq_ref, dck_ref, f_ref, b_ref, df_ref, db_ref, carry):
        @pl.when(pl.program_id(0) == 0)
        def _():
            carry[...] = jnp.zeros_like(carry)
            db_ref[...] = jnp.zeros_like(db_ref)

        dc = dcq_ref[...] + dck_ref[...]
        lower = (lax.broadcasted_iota(jnp.int32, (tc, tc), 0) >= lax.broadcasted_iota(jnp.int32, (tc, tc), 1))
        dlf = lax.dot_general(dc, lower.astype(F32), NN, precision=lax.Precision.HIGHEST,
                              preferred_element_type=F32) + carry[...]
        carry[...] += jnp.sum(dc, axis=1, keepdims=True)
        df = dlf * _sigmoid(-(f_ref[...] + b_ref[...]))
        df_ref[...] = df
        db_ref[...] += jnp.broadcast_to(jnp.sum(df, axis=1, keepdims=True), db_ref.shape)

    blk = pl.BlockSpec((h, tc), lambda i: (0, nb - 1 - i))
    return pl.pallas_call(
        body, name=name, grid=(nb,), in_specs=[blk, blk, blk, pl.BlockSpec((h, 1), lambda i: (0, 0))],
        out_specs=[blk, pl.BlockSpec((h, LANES), lambda i: (0, 0))],
        out_shape=[jax.ShapeDtypeStruct((h, s), F32), jax.ShapeDtypeStruct((h, LANES), F32)],
        scratch_shapes=[pltpu.VMEM((h, 1), F32)], compiler_params=_cparams())(dcq, dck, f_t, b_f)


def _attn_fwd(name, qkv, q_blk, k_blk, v_blk, n_pairs, ck):
    s = qkv.shape[0]
    dh = LANES // 2
    t = min(ATT_BLOCK, s)
    nq = s // t
    scale = dh ** -0.5

    def body(q_ref, k_ref, v_ref, ck_ref, o_ref, lse_ref, m_s, acc_s):
        i = pl.program_id(1)
        low = lax.broadcasted_iota(jnp.int32, (1, LANES), 1) < dh
        qs = (q_ref[...].astype(F32) * scale).astype(BF16)
        zero = jnp.zeros_like(qs)
        qh = (jnp.where(low, qs, zero), jnp.where(low, zero, qs))
        m_s[...] = jnp.full(m_s.shape, -1e30, F32)
        acc_s[...] = jnp.zeros_like(acc_s)
        causal = (lax.broadcasted_iota(jnp.int32, (t, t), 1) <= lax.broadcasted_iota(jnp.int32, (t, t), 0))

        def step(j, diagonal):
            r0 = pl.multiple_of(j * t, t)
            kj = k_ref[pl.ds(r0, t), :]
            vj = v_ref[pl.ds(r0, t), :]
            one = jnp.ones_like(vj)
            vh = (jnp.where(low, vj, one), jnp.where(low, one, vj))
            for hd in range(2):
                sc = _dot(qh[hd], kj, NT) - ck_ref[hd, j]
                if diagonal:
                    sc = jnp.where(causal, sc, -1e30)
                m_old = m_s[hd]
                m_new = jnp.maximum(m_old, jnp.max(sc, axis=1, keepdims=True))
                p = jnp.exp(sc - m_new)
                acc_s[hd] = jnp.exp(m_old - m_new) * acc_s[hd] + _dot(p.astype(BF16), vh[hd], NN)
                m_s[hd] = m_new

        def full(j, _):
            step(j, False)
            return 0

        lax.fori_loop(0, i, full, 0)
        step(i, True)
        a0, a1 = acc_s[0], acc_s[1]
        o_ref[...] = jnp.where(low, a0 / pltpu.roll(a0, dh, 1), a1 / pltpu.roll(a1, dh, 1)).astype(BF16)
        lse_ref[0] = m_s[0] + jnp.log(a0[:, dh:dh + 1])
        lse_ref[1] = m_s[1] + jnp.log(a1[:, 0:1])

    return pl.pallas_call(
        body, name=name, grid=(n_pairs, nq),
        in_specs=[pl.BlockSpec((t, LANES), lambda hp, i: (i, q_blk + hp)),
                  pl.BlockSpec((s, LANES), lambda hp, i: (0, k_blk + hp)),
                  pl.BlockSpec((s, LANES), lambda hp, i: (0, v_blk + hp)),
                  pl.BlockSpec((2, nq, 1, t), lambda hp, i: (hp, 0, 0, 0))],
        out_specs=[pl.BlockSpec((t, LANES), lambda hp, i: (i, hp)), pl.BlockSpec((2, t, 1), lambda hp, i: (hp, i, 0))],
        out_shape=[jax.ShapeDtypeStruct((s, LANES * n_pairs), BF16), jax.ShapeDtypeStruct((2 * n_pairs, s, 1), F32)],
        scratch_shapes=[pltpu.VMEM((2, t, 1), F32), pltpu.VMEM((2, t, LANES), F32)],
        compiler_params=_cparams(),
    )(qkv, qkv, qkv, ck)


def _attn_bwd(name, qkv, q_blk, k_blk, v_blk, n_pairs, o, do, lse_rows, ck_cols):
    s = qkv.shape[0]
    dh = LANES // 2
    t = min(ATT_BLOCK, s)
    nk = s // t
    scale = dh ** -0.5

    def body(q_ref, k_ref, v_ref, o_ref, do_ref, lse_ref, ck_ref,
             dq_ref, dk_ref, dv_ref, dcq_ref, dck_ref, delta, dqt, dk_acc, dv_acc):
        j = pl.program_id(1)
        low = lax.broadcasted_iota(jnp.int32, (1, LANES), 1) < dh
        low_rows = lax.broadcasted_iota(jnp.int32, (LANES, 1), 0) < dh

        @pl.when(j == 0)
        def _():
            dqt[...] = jnp.zeros_like(dqt)
            sel = (jnp.broadcast_to(low, (SUBLANES, LANES)).astype(F32), jnp.broadcast_to(~low, (SUBLANES, LANES)).astype(F32))

            def fill(i, _):
                r0 = pl.multiple_of(i * t, t)
                prod = do_ref[pl.ds(r0, t), :].astype(F32) * o_ref[pl.ds(r0, t), :].astype(F32)
                for hd in range(2):
                    delta[hd, i] = lax.dot_general(sel[hd], prod, NT, precision=lax.Precision.HIGHEST,
                                                   preferred_element_type=F32)
                return 0

            lax.fori_loop(0, nk, fill, 0)

        kj, vj = k_ref[...], v_ref[...]
        zero, one = jnp.zeros_like(kj), jnp.ones_like(kj)
        kh = (jnp.where(low, kj, zero), jnp.where(low, zero, kj))
        vh = (jnp.where(low, vj, zero), jnp.where(low, zero, vj))
        kjt = kj.astype(F32).T.astype(BF16)
        one_t = jnp.ones_like(kjt)
        kht = (jnp.where(low_rows, kjt, one_t), jnp.where(low_rows, one_t, kjt))
        dk_acc[...] = jnp.zeros_like(dk_acc)
        dv_acc[...] = jnp.zeros_like(dv_acc)
        causal_t = (lax.broadcasted_iota(jnp.int32, (t, t), 0) <= lax.broadcasted_iota(jnp.int32, (t, t), 1))

        def step(i, diagonal):
            r0 = pl.multiple_of(i * t, t)
            qi = (q_ref[pl.ds(r0, t), :].astype(F32) * scale).astype(BF16)
            doi = do_ref[pl.ds(r0, t), :]
            qone, dzero = jnp.ones_like(qi), jnp.zeros_like(doi)
            qsel = (jnp.where(low, qi, qone), jnp.where(low, qone, qi))
            dosel = (jnp.where(low, doi, dzero), jnp.where(low, dzero, doi))
            for hd in range(2):
                st = _dot(kh[hd], qi, NT) - ck_ref[hd] - lse_ref[hd, i]
                pt = jnp.exp(st)
                if diagonal:
                    pt = jnp.where(causal_t, pt, 0.0)
                dst = pt * (_dot(vh[hd], doi, NT) - delta[hd, i, 0:1, :])
                dsb = dst.astype(BF16)
                dv_acc[...] += _dot(pt.astype(BF16), dosel[hd], NN)
                dk_acc[hd] += _dot(dsb, qsel[hd], NN)
                dqt[hd, i] += _dot(kht[hd], dsb, NN)

        step(j, True)

        def rest(i, _):
            step(i, False)
            return 0

        lax.fori_loop(j + 1, nk, rest, 0)
        dk_ref[...] = jnp.where(low, dk_acc[0], dk_acc[1]).astype(BF16)
        dv_ref[...] = dv_acc[...].astype(BF16)
        dck_ref[0] = -dk_acc[0][:, dh:dh + 1]
        dck_ref[1] = -dk_acc[1][:, 0:1]

        @pl.when(j == nk - 1)
        def _():
            def emit(i, _):
                r0 = pl.multiple_of(i * t, t)
                d0, d1 = dqt[0, i], dqt[1, i]
                dq_ref[pl.ds(r0, t), :] = (jnp.where(low_rows, d0, d1) * scale).T.astype(BF16)
                dcq_ref[0, i] = d0[dh:dh + 1, :]
                dcq_ref[1, i] = d1[0:1, :]
                return 0

            lax.fori_loop(0, nk, emit, 0)

    col_blk = lambda base: pl.BlockSpec((t, LANES), lambda hp, j: (j, base + hp))
    col_all = lambda base: pl.BlockSpec((s, LANES), lambda hp, j: (0, base + hp))
    rows_all = pl.BlockSpec((2, nk, 1, t), lambda hp, j: (hp, 0, 0, 0))
    return pl.pallas_call(
        body, name=name, grid=(n_pairs, nk),
        in_specs=[col_all(q_blk), col_blk(k_blk), col_blk(v_blk), col_all(0), col_all(0), rows_all,
                  pl.BlockSpec((2, t, 1), lambda hp, j: (hp, j, 0))],
        out_specs=[col_all(0), col_blk(0), col_blk(0), rows_all, pl.BlockSpec((2, t, 1), lambda hp, j: (hp, j, 0))],
        out_shape=[jax.ShapeDtypeStruct((s, LANES * n_pairs), BF16), jax.ShapeDtypeStruct((s, LANES * n_pairs), BF16),
                   jax.ShapeDtypeStruct((s, LANES * n_pairs), BF16), jax.ShapeDtypeStruct((2 * n_pairs, nk, 1, t), F32),
                   jax.ShapeDtypeStruct((2 * n_pairs, s, 1), F32)],
        scratch_shapes=[pltpu.VMEM((2, nk, SUBLANES, t), F32), pltpu.VMEM((2, nk, LANES, t), F32),
                        pltpu.VMEM((2, t, LANES), F32), pltpu.VMEM((t, LANES), F32)],
        compiler_params=_cparams(),
    )(qkv, qkv, qkv, o, do, lse_rows, ck_cols)


def _adamw(name, w, g, m, v):
    r, c = w.shape
    tr = _pick8(r, max(SUBLANES, (1 << 20) // (4 * c)))

    def body(w_ref, g_ref, m_ref, v_ref, d_ref, mo_ref, vo_ref):
        gv = g_ref[...]
        m2 = ADAM_B1 * m_ref[...] + (1.0 - ADAM_B1) * gv
        v2 = ADAM_B2 * v_ref[...] + (1.0 - ADAM_B2) * (gv * gv)
        m_hat = m2 / (1.0 - ADAM_B1 ** ADAM_STEP)
        v_hat = v2 / (1.0 - ADAM_B2 ** ADAM_STEP)
        d_ref[...] = -ADAM_LR * (m_hat / (jnp.sqrt(v_hat) + ADAM_EPS) + ADAM_WD * w_ref[...])
        mo_ref[...] = m2
        vo_ref[...] = v2

    blk = pl.BlockSpec((tr, c), lambda i: (i, 0))
    sh = jax.ShapeDtypeStruct((r, c), F32)
    return pl.pallas_call(body, name=name, grid=(r // tr,), in_specs=[blk] * 4, out_specs=[blk] * 3,
                          out_shape=[sh, sh, sh], compiler_params=_cparams())(w, g, m, v)


def _pick8(dim, target, mult=SUBLANES):
    best, t = None, mult
    while t <= min(dim, target):
        if dim % t == 0:
            best = t
        t += mult
    return best or dim


BF16_ROWS = 16


def _sum_blocks(name, x, out_dtype):
    n, r, c = x.shape
    tr = _pick8(r, max(BF16_ROWS, (1 << 19) // (4 * c)), BF16_ROWS)

    def body(x_ref, o_ref):
        acc = x_ref[0].astype(F32)
        for i in range(1, n):
            acc = acc + x_ref[i].astype(F32)
        o_ref[...] = acc.astype(out_dtype)

    return pl.pallas_call(body, name=name, grid=(r // tr,),
                          in_specs=[pl.BlockSpec((n, tr, c), lambda i: (0, i, 0))],
                          out_specs=pl.BlockSpec((tr, c), lambda i: (i, 0)),
                          out_shape=jax.ShapeDtypeStruct((r, c), out_dtype), compiler_params=_cparams())(x)


def _add_layer(name, grads, recv, core):
    _, n, r, c = grads.shape
    tr = _pick8(r, max(BF16_ROWS, (1 << 19) // (4 * c)), BF16_ROWS)

    def body(core_ref, g_ref, r_ref, o_ref):
        o_ref[...] = (g_ref[...].astype(F32) + r_ref[...].astype(F32)).astype(BF16)

    grid_spec = pltpu.PrefetchScalarGridSpec(
        num_scalar_prefetch=1, grid=(r // tr,),
        in_specs=[pl.BlockSpec((None, n, tr, c), lambda i, core_ref: (core_ref[0], 0, i, 0)),
                  pl.BlockSpec((n, tr, c), lambda i, core_ref: (0, i, 0))],
        out_specs=pl.BlockSpec((n, tr, c), lambda i, core_ref: (0, i, 0)))
    return pl.pallas_call(body, name=name, grid_spec=grid_spec,
                          out_shape=jax.ShapeDtypeStruct((n, r, c), BF16), compiler_params=_cparams())(core, grads, recv)


def _all_gather(name, x_shard):
    m_per, n = x_shard.shape

    def body(x_ref, out_ref, send_sems, recv_sems):
        x, y, c = lax.axis_index("x"), lax.axis_index("y"), lax.axis_index("c")
        me, sibling = (x, y, c), (x, y, 1 - c)
        chips = [(1 - x, y), (x, 1 - y), (1 - x, 1 - y)]

        def rows(px, py, pc):
            return out_ref.at[pl.ds((4 * px + 2 * py + pc) * m_per, m_per), :]

        def copy(k, block, to, src=None):
            return pltpu.make_async_remote_copy(
                src_ref=rows(*block) if src is None else src, dst_ref=rows(*block),
                send_sem=send_sems.at[k], recv_sem=recv_sems.at[k], device_id=to, device_id_type=MESH)

        first = [copy(0, me, sibling, src=x_ref)]
        first += [copy(1 + j, me, (*chip, c), src=x_ref) for j, chip in enumerate(chips)]
        for cp in first:
            cp.start()
        passed = [copy(4 + j, (*chip, c), sibling) for j, chip in enumerate(chips)]
        for j, chip in enumerate(chips):
            copy(1 + j, (*chip, c), me).wait_recv()
            passed[j].start()
        copy(0, sibling, me).wait_recv()
        for j, chip in enumerate(chips):
            copy(4 + j, (*chip, 1 - c), me).wait_recv()
        for cp in first + passed:
            cp.wait_send()

    out = pl.pallas_call(
        body, name=name, out_shape=jax.ShapeDtypeStruct((N_DEV * m_per, n), x_shard.dtype),
        in_specs=[pl.BlockSpec(memory_space=pl.ANY)], out_specs=pl.BlockSpec(memory_space=pl.ANY),
        scratch_shapes=[pltpu.SemaphoreType.DMA((7,)), pltpu.SemaphoreType.DMA((7,))],
    )(x_shard)
    my_dev = 4 * lax.axis_index("x") + 2 * lax.axis_index("y") + lax.axis_index("c")
    return lax.dynamic_update_slice(out, x_shard, (my_dev * m_per, 0))


def _hbm_call(name, body, operands, out_shapes, n_sems):
    return pl.pallas_call(
        body, name=name, out_shape=list(out_shapes),
        in_specs=[pl.BlockSpec(memory_space=pl.ANY)] * len(operands),
        out_specs=[pl.BlockSpec(memory_space=pl.ANY)] * len(out_shapes),
        scratch_shapes=[pltpu.SemaphoreType.DMA((n_sems,)), pltpu.SemaphoreType.DMA((n_sems,))],
    )(*operands)


def _put_own(out, own, index):
    start = tuple(index) + (0,) * own.ndim
    return lax.dynamic_update_slice(out, own.reshape((1,) * len(index) + own.shape), start)


def _gather_weights(name, shards):
    n_w = len(shards)

    def body(*refs):
        ins, outs = refs[:n_w], refs[n_w:2 * n_w]
        send_sems, recv_sems = refs[2 * n_w], refs[2 * n_w + 1]
        x, y, c = lax.axis_index("x"), lax.axis_index("y"), lax.axis_index("c")
        my_chip = 2 * x + y
        chips = [(1 - x, y), (x, 1 - y), (1 - x, 1 - y)]

        def copy(w, k, src, chip, layer, to):
            return pltpu.make_async_remote_copy(
                src_ref=src, dst_ref=outs[w].at[chip, layer], send_sem=send_sems.at[6 * w + k],
                recv_sem=recv_sems.at[6 * w + k], device_id=to, device_id_type=MESH)

        started = []
        for w in range(n_w):
            for k, (cx, cy) in enumerate(chips):
                started.append(copy(w, k, ins[w].at[c], my_chip, c, (cx, cy, c)))
                started[-1].start()
        for w in range(n_w):
            for k, (cx, cy) in enumerate(chips):
                chip = 2 * cx + cy
                copy(w, k, ins[w].at[c], chip, c, (cx, cy, c)).wait_recv()
                started.append(copy(w, 3 + k, outs[w].at[chip, c], chip, c, (x, y, 1 - c)))
                started[-1].start()
        for w in range(n_w):
            for k, (cx, cy) in enumerate(chips):
                copy(w, 3 + k, ins[w].at[c], 2 * cx + cy, 1 - c, (x, y, 1 - c)).wait_recv()
        for cp in started:
            cp.wait_send()

    outs = _hbm_call(name, body, shards, [jax.ShapeDtypeStruct((N_CHIPS,) + s.shape, s.dtype) for s in shards], 6 * n_w)
    my_chip = 2 * lax.axis_index("x") + lax.axis_index("y")
    return [_put_own(o, s, (my_chip,)) for o, s in zip(outs, shards)]


def _swap_layers(name, grads):
    n_w = len(grads)

    def body(*refs):
        ins, outs = refs[:n_w], refs[n_w:2 * n_w]
        send_sems, recv_sems = refs[2 * n_w], refs[2 * n_w + 1]
        x, y, c = lax.axis_index("x"), lax.axis_index("y"), lax.axis_index("c")
        copies = [pltpu.make_async_remote_copy(src_ref=ins[w].at[1 - c], dst_ref=outs[w], send_sem=send_sems.at[w],
                                               recv_sem=recv_sems.at[w], device_id=(x, y, 1 - c), device_id_type=MESH)
                  for w in range(n_w)]
        for cp in copies:
            cp.start()
        for cp in copies:
            cp.wait()

    return _hbm_call(name, body, grads, [jax.ShapeDtypeStruct(g.shape[1:], g.dtype) for g in grads], n_w)


def _chip_exchange(name, parts):
    n_w = len(parts)

    def body(*refs):
        ins, outs = refs[:n_w], refs[n_w:2 * n_w]
        send_sems, recv_sems = refs[2 * n_w], refs[2 * n_w + 1]
        x, y, c = lax.axis_index("x"), lax.axis_index("y"), lax.axis_index("c")
        my_chip = 2 * x + y
        chips = [(1 - x, y), (x, 1 - y), (1 - x, 1 - y)]
        copies = [pltpu.make_async_remote_copy(
            src_ref=ins[w].at[2 * cx + cy], dst_ref=outs[w].at[my_chip], send_sem=send_sems.at[3 * w + k],
            recv_sem=recv_sems.at[3 * w + k], device_id=(cx, cy, c), device_id_type=MESH)
            for w in range(n_w) for k, (cx, cy) in enumerate(chips)]
        for cp in copies:
            cp.start()
        for cp in copies:
            cp.wait()

    outs = _hbm_call(name, body, parts, [jax.ShapeDtypeStruct(p.shape, p.dtype) for p in parts], 3 * n_w)
    my_chip = 2 * lax.axis_index("x") + lax.axis_index("y")
    return [_put_own(o, lax.dynamic_index_in_dim(p, my_chip, 0, keepdims=False), (my_chip,)) for o, p in zip(outs, parts)]


def _share_layers(name, reduced):
    n_w = len(reduced)

    def body(*refs):
        ins, outs = refs[:n_w], refs[n_w:2 * n_w]
        send_sems, recv_sems = refs[2 * n_w], refs[2 * n_w + 1]
        x, y, c = lax.axis_index("x"), lax.axis_index("y"), lax.axis_index("c")
        copies = [pltpu.make_async_remote_copy(src_ref=ins[w], dst_ref=outs[w].at[c], send_sem=send_sems.at[w],
                                               recv_sem=recv_sems.at[w], device_id=(x, y, 1 - c), device_id_type=MESH)
                  for w in range(n_w)]
        for cp in copies:
            cp.start()
        for cp in copies:
            cp.wait()

    outs = _hbm_call(name, body, reduced, [jax.ShapeDtypeStruct((2,) + r.shape, r.dtype) for r in reduced], n_w)
    return [_put_own(o, r, (lax.axis_index("c"),)) for o, r in zip(outs, reduced)]


def _pack(arrays, cols, row_multiple, dtype):
    flat = jnp.concatenate([a.reshape(-1).astype(dtype) for a in arrays])
    unit = cols * row_multiple
    total = -(-flat.shape[0] // unit) * unit
    return jnp.pad(flat, (0, total - flat.shape[0])).reshape(total // cols, cols)


def _unpack(buf, shapes):
    flat, out, off = buf.reshape(-1), [], 0
    for sh in shapes:
        n = math.prod(sh)
        out.append(flat[off:off + n].reshape(sh))
        off += n
    return out


def _discretize(lam_re, lam_im, log_dt, b_re, b_im):
    lam = lax.complex(jnp.minimum(lam_re, -EIG_CLIP), lam_im)
    dt = jnp.exp(log_dt)[:, None]
    lam_bar = jnp.exp(lam * dt)
    b_bar = ((lam_bar - 1.0) / lam)[..., None] * lax.complex(b_re, b_im)
    return jnp.real(lam_bar), jnp.imag(lam_bar), jnp.real(b_bar), jnp.imag(b_bar)


def _scan_tables(ar, ai):
    a = lax.complex(ar, ai)
    pw = [a]
    for _ in range(7):
        pw.append(pw[-1] * a)
    rows = jnp.arange(SUBLANES)[:, None]

    def build(p, reverse):
        tabs = []
        for k in (1, 2, 4):
            keep = (rows <= SUBLANES - 1 - k) if reverse else (rows >= k)
            tk = jnp.where(keep, p[k - 1][None, :], 0.0)
            tabs += [jnp.real(tk), jnp.imag(tk)]
        stack = jnp.stack(p[::-1] if reverse else p)
        tabs += [jnp.real(stack), jnp.imag(stack)]
        return jnp.stack(tabs).astype(F32)

    return build(pw, False), build([jnp.conj(p) for p in pw], True)


def _block_diag(per_group, groups_per_block):
    g, a, b = per_group.shape
    x = per_group.reshape(g // groups_per_block, groups_per_block, a, b)
    eye = jnp.eye(groups_per_block, dtype=per_group.dtype)
    out = x[:, :, :, None, :] * eye[None, :, None, :, None]
    return out.reshape(g // groups_per_block, groups_per_block * a, groups_per_block * b)


def _block_diag_extract(dense, groups_per_block, a, b):
    nkb = dense.shape[0]
    x = dense.reshape(nkb, groups_per_block, a, groups_per_block, b)
    idx = jnp.arange(groups_per_block)
    return x[:, idx, :, idx, :].transpose(1, 0, 2, 3).reshape(nkb * groups_per_block, a, b)


def _layer_fwd(tag, l, x, mod, p, wts):
    s, d = x.shape
    w_ssm, w_att = p["w_glu"].shape[0], wts["w_pb"].shape[2]
    heads = p["b_f"].shape[0]
    dh = w_att // heads
    cs = d // N_CHIPS
    fs = wts["w_ffn_down"].shape[2]
    tm = _pick(s, 1024)
    row = lambda v: v.reshape(1, -1)
    sv = {}

    h = _prenorm_fwd(f"prenorm_mix_{tag}", x, row(p["g_pre_mix"]), row(mod[1]), row(mod[0]))
    uqkv = _mm_plain(f"proj_main_{tag}", h, p["w_main"], "nn", BF16, tm=1024, tn=1024, tk=1024)
    fg = _mm_plain(f"proj_gate_{tag}", h, p["w_gates"], "nn", F32, tm=1024, tn=1024, tk=1024)
    f_t = fg[:, 2 * d:2 * d + heads].T

    y_s5, ys, carries = _s5_fwd(f"s5_fwd_{tag}", uqkv, p["b_blk"], p["c_blk"], p["tab_f"], row(p["d_skip"]),
                                p["w_glu"], row(p["b_glu"]))

    assert dh * 2 == LANES and w_ssm % LANES == 0 and w_att % LANES == 0
    n_pairs = w_att // LANES
    blocks = (w_ssm // LANES, w_ssm // LANES + n_pairs, w_ssm // LANES + 2 * n_pairs)
    cum = _cum_fwd(f"cum_fwd_{tag}", f_t, p["b_f"].reshape(heads, 1))
    t = min(ATT_BLOCK, s)
    ck_cols, ck_rows = cum.reshape(heads, s, 1), cum.reshape(heads, s // t, 1, t)
    ya, lse = _attn_fwd(f"attn_fwd_{tag}", uqkv, *blocks, n_pairs, ck_rows)

    tile = pl.BlockSpec((tm, cs), lambda i, j, k: (i, j))
    slab = lambda rows: pl.BlockSpec((None, None, rows, cs), lambda i, j, k: (j, l, 0, 0))

    def merge(acc, extra_refs, out_refs):
        ya_ref, wpb_ref, ga_ref, gb_ref = extra_refs
        a_ref, b_ref, m_ref = out_refs
        bv = _dot(ya_ref[...], wpb_ref[...], NN)
        a_ref[...] = acc.astype(BF16)
        b_ref[...] = bv.astype(BF16)
        m_ref[...] = (_sigmoid(ga_ref[...]) * acc + _sigmoid(gb_ref[...]) * bv).astype(BF16)

    sd_bf = jax.ShapeDtypeStruct((s, d), BF16)
    pa, pb, merged = _mm_raw(
        f"merge_{tag}", ys, wts["w_pa"], "nn", (s // tm, N_CHIPS, 1), (tm, cs),
        pl.BlockSpec((tm, w_ssm), lambda i, j, k: (i, 0)), slab(w_ssm), [sd_bf] * 3, [tile] * 3, merge,
        extra=(ya, wts["w_pb"], fg, fg),
        extra_specs=[pl.BlockSpec((tm, w_att), lambda i, j, k: (i, 0)), slab(w_att), tile,
                     pl.BlockSpec((tm, cs), lambda i, j, k: (i, j + N_CHIPS))])

    tm2 = _pick(s, POSTNORM_ROWS)
    x1, y_mix = _mm_postnorm(
        f"out_proj_{tag}", merged, pl.BlockSpec((tm2, cs), lambda i, j, k: (i, k)), wts["w_o"],
        pl.BlockSpec((None, None, cs, d), lambda i, j, k: (k, l, 0, 0)), N_CHIPS, x, row(mod[2]), row(p["g_post_mix"]))

    h2 = _prenorm_fwd(f"prenorm_ffn_{tag}", x1, row(p["g_pre_ffn"]), row(mod[4]), row(mod[3]))
    a4, b4, hid4 = _ffn_up(f"ffn_up_{tag}", h2, wts["w_ffn_gate"], wts["w_ffn_up"], l)
    x2, y_ffn = _mm_postnorm(
        f"ffn_down_{tag}", hid4, pl.BlockSpec((None, tm2, fs), lambda i, j, k: (k, i, 0)), wts["w_ffn_down"],
        pl.BlockSpec((None, None, fs, d), lambda i, j, k: (k, l, 0, 0)), N_CHIPS, x1, row(mod[5]), row(p["g_post_ffn"]))

    sv.update(x=x, h=h, uqkv=uqkv, fg=fg, f_t=f_t, y_s5=y_s5, ys=ys, carries=carries, blocks=blocks,
              ck_cols=ck_cols, lse_rows=lse.reshape(heads, s // t, 1, t), ya=ya, pa=pa, pb=pb, merged=merged, x1=x1,
              y_mix=y_mix, h2=h2, a4=a4, b4=b4, hid4=hid4, y_ffn=y_ffn)
    return x2, sv


def _mm_postnorm(name, a, a_spec, w, w_spec, nk, x, gate, g):
    s, d = x.shape
    tm = _pick(s, POSTNORM_ROWS)
    rowspec = pl.BlockSpec((tm, d), lambda i, j, k: (i, 0))
    vec = pl.BlockSpec((1, d), lambda i, j, k: (0, 0))

    def epilogue(acc, extra_refs, out_refs):
        x_ref, gate_ref, g_ref = extra_refs
        r = lax.rsqrt(jnp.mean(acc * acc, axis=-1, keepdims=True) + RMS_EPS)
        out_refs[0][...] = x_ref[...] + gate_ref[...] * (acc * r * g_ref[...])
        out_refs[1][...] = acc

    sd = jax.ShapeDtypeStruct((s, d), F32)
    return _mm_raw(name, a, w, "nn", (s // tm, 1, nk), (tm, d), a_spec, w_spec, [sd, sd], [rowspec, rowspec], epilogue,
                   extra=(x, gate, g), extra_specs=[rowspec, vec, vec])


def _layer_bwd(tag, l, dx2, mod, p, wts, sv):
    s, d = dx2.shape
    w_ssm, w_att = p["w_glu"].shape[0], wts["w_pb"].shape[2]
    heads = p["b_f"].shape[0]
    cs = d // N_CHIPS
    fs = wts["w_ffn_down"].shape[2]
    tm, tk, td = _pick(s, 1024), _pick(s, 1024), d
    row = lambda v: v.reshape(1, -1)
    gr = {}

    def dw_slabs(name, act, act_spec, rows, dy, dy_spec, cols, grid_mn, out_index):
        return _mm_raw(name, act, dy, "tn", grid_mn + (s // tk,), (rows, cols), act_spec, dy_spec,
                       [jax.ShapeDtypeStruct((N_CHIPS,) + out_index[1], BF16)],
                       [pl.BlockSpec((None, rows, cols), out_index[0])], _store(BF16))[0]

    dy_ffn, sums = _postnorm_bwd(f"postnorm_bwd_ffn_{tag}", dx2, sv["y_ffn"], row(p["g_post_ffn"]), row(mod[5]))
    d_gate_f, gr["g_post_ffn"] = sums[0], sums[1]
    gr["w_ffn_down"] = dw_slabs(f"dw_down_{tag}", sv["hid4"], pl.BlockSpec((None, tk, fs), lambda i, j, k: (i, k, 0)), fs,
                                dy_ffn, pl.BlockSpec((tk, d), lambda i, j, k: (k, 0)), d, (N_CHIPS, 1),
                                (lambda i, j, k: (i, 0, 0), (fs, d)))

    def swiglu_bwd(acc, extra_refs, out_refs):
        av, bv = extra_refs[0][...].astype(F32), extra_refs[1][...].astype(F32)
        sg = _sigmoid(av)
        out_refs[0][...] = (acc * bv * (sg * (1.0 + av * (1.0 - sg)))).astype(BF16)
        out_refs[1][...] = (acc * (av * sg)).astype(BF16)

    blk4 = pl.BlockSpec((None, tm, fs), lambda i, j, k: (j, i, 0))
    sh4 = jax.ShapeDtypeStruct((N_CHIPS, s, fs), BF16)
    da4, db4 = _mm_raw(f"ffn_down_bwd_{tag}", dy_ffn, wts["w_ffn_down"], "nt", (s // tm, N_CHIPS, 1), (tm, fs),
                       pl.BlockSpec((tm, d), lambda i, j, k: (i, 0)),
                       pl.BlockSpec((None, None, fs, d), lambda i, j, k: (j, l, 0, 0)),
                       [sh4, sh4], [blk4, blk4], swiglu_bwd, extra=(sv["a4"], sv["b4"]), extra_specs=[blk4, blk4])
    for n, act4 in (("w_ffn_gate", da4), ("w_ffn_up", db4)):
        gr[n] = dw_slabs(f"d{n}_{tag}", sv["h2"], pl.BlockSpec((tk, td), lambda i, j, k: (k, i)), td,
                         act4, pl.BlockSpec((None, tk, fs), lambda i, j, k: (j, k, 0)), fs, (d // td, N_CHIPS),
                         (lambda i, j, k: (j, i, 0), (d, fs)))
    pairs = [(act4, (None, tm, fs), lambda i, kk: (kk, i, 0), wts[n], (None, None, td, fs), lambda j, kk: (kk, l, j, 0),
              N_CHIPS) for n, act4 in (("w_ffn_gate", da4), ("w_ffn_up", db4))]
    dh2 = _mm_sum(f"dh_ffn_{tag}", s, d, tm, td, pairs, F32)
    dx1, sums = _prenorm_bwd(f"prenorm_bwd_ffn_{tag}", dh2, sv["x1"], row(p["g_pre_ffn"]), row(mod[4]), dx2)
    d_scale_f, d_shift_f, gr["g_pre_ffn"] = sums[0], sums[1], sums[2]

    dy_mix, sums = _postnorm_bwd(f"postnorm_bwd_mix_{tag}", dx1, sv["y_mix"], row(p["g_post_mix"]), row(mod[2]))
    d_gate_m, gr["g_post_mix"] = sums[0], sums[1]
    gr["w_o"] = dw_slabs(f"dw_o_{tag}", sv["merged"], pl.BlockSpec((tk, cs), lambda i, j, k: (k, i)), cs,
                         dy_mix, pl.BlockSpec((tk, d), lambda i, j, k: (k, 0)), d, (N_CHIPS, 1),
                         (lambda i, j, k: (i, 0, 0), (cs, d)))

    tile = pl.BlockSpec((tm, cs), lambda i, j, k: (i, j))

    def merge_bwd(acc, extra_refs, out_refs):
        a_ref, b_ref, ga_ref, gb_ref = extra_refs
        sa, sb = _sigmoid(ga_ref[...]), _sigmoid(gb_ref[...])
        out_refs[0][...] = (acc * sa).astype(BF16)
        out_refs[1][...] = (acc * sb).astype(BF16)
        out_refs[2][...] = (acc * a_ref[...].astype(F32) * sa * (1.0 - sa)).astype(BF16)
        out_refs[3][...] = (acc * b_ref[...].astype(F32) * sb * (1.0 - sb)).astype(BF16)

    sd_bf = jax.ShapeDtypeStruct((s, d), BF16)
    d_pa, d_pb, d_ga, d_gb = _mm_raw(
        f"out_proj_bwd_{tag}", dy_mix, wts["w_o"], "nt", (s // tm, N_CHIPS, 1), (tm, cs),
        pl.BlockSpec((tm, d), lambda i, j, k: (i, 0)), pl.BlockSpec((None, None, cs, d), lambda i, j, k: (j, l, 0, 0)),
        [sd_bf] * 4, [tile] * 4, merge_bwd, extra=(sv["pa"], sv["pb"], sv["fg"], sv["fg"]),
        extra_specs=[tile, tile, tile, pl.BlockSpec((tm, cs), lambda i, j, k: (i, j + N_CHIPS))])
    d_branch = {}
    for n, act, width, d_p in (("w_pa", sv["ys"], w_ssm, d_pa), ("w_pb", sv["ya"], w_att, d_pb)):
        gr[n] = dw_slabs(f"d{n}_{tag}", act, pl.BlockSpec((tk, width), lambda i, j, k: (k, 0)), width,
                         d_p, pl.BlockSpec((tk, cs), lambda i, j, k: (k, j)), cs, (1, N_CHIPS),
                         (lambda i, j, k: (j, 0, 0), (width, cs)))
        d_branch[n] = _mm_raw(
            f"d_in_{n}_{tag}", d_p, wts[n], "nt", (s // tm, 1, N_CHIPS), (tm, width),
            pl.BlockSpec((tm, cs), lambda i, j, k: (i, k)), pl.BlockSpec((None, None, width, cs), lambda i, j, k: (k, l, 0, 0)),
            [jax.ShapeDtypeStruct((s, width), BF16)], [pl.BlockSpec((tm, width), lambda i, j, k: (i, 0))], _store(BF16))[0]
    d_ys, d_ya = d_branch["w_pa"], d_branch["w_pb"]

    dq, dk, dv, dcq, dck = _attn_bwd(f"attn_bwd_{tag}", sv["uqkv"], *sv["blocks"], w_att // LANES, sv["ya"], d_ya,
                                     sv["lse_rows"], sv["ck_cols"])
    d_f_t, d_bf = _cum_bwd(f"cum_bwd_{tag}", dcq.reshape(heads, s), dck.reshape(heads, s), sv["f_t"],
                           p["b_f"].reshape(heads, 1))
    gr["b_f"] = d_bf[:, 0]

    du, d_bblk, d_cblk, d_abar, d_wglu, vec = _s5_bwd(
        f"s5_bwd_{tag}", sv["uqkv"], d_ys, sv["y_s5"], sv["carries"], p["b_blk"], p["c_blk"], p["tab_f"], p["tab_r"],
        row(p["d_skip"]), p["w_glu"], row(p["b_glu"]))
    gr["w_glu"] = d_wglu.astype(BF16).reshape(N_CHIPS, w_ssm // N_CHIPS, w_ssm)
    gr["b_glu"], gr["d_skip"] = vec[0], vec[1]
    gr["b_blk"], gr["c_blk"], gr["a_bar"] = d_bblk, d_cblk, d_abar

    d_f = jnp.pad(d_f_t.T, ((0, 0), (0, F_PAD - heads))).astype(BF16)
    assert w_ssm % w_att == 0 and (2 * d) % F_PAD == 0
    first = w_ssm // w_att
    main_pieces = [(du, w_ssm, 0), (dq, w_att, first), (dk, w_att, first + 1), (dv, w_att, first + 2)]
    dw = [_mm_plain(f"dw_in{n}_{tag}", sv["h"], piece, "tn", BF16, tm=1024, tn=1024, tk=1024)
          for n, piece in enumerate([du, dq, dk, dv, d_f, d_ga, d_gb])]
    w_in_grad = jnp.concatenate(dw[:4] + [dw[4][:, :heads], dw[5], dw[6]], axis=1)
    gr["w_in"] = w_in_grad.reshape(d, N_CHIPS, w_in_grad.shape[1] // N_CHIPS).transpose(1, 0, 2)
    tmx, tkx = _pick(s, 512), _pick(d, 512)
    pairs = [(piece, (tmx, width), lambda i, kk: (i, 0), p["w_main"], (d, width), lambda j, kk, blk=blk: (j, blk), 1)
             for piece, width, blk in main_pieces]
    steps = d // tkx
    pairs += [(piece, (tmx, tkx), lambda i, kk: (i, kk), p["w_gates"], (d, tkx), lambda j, kk, off=off: (j, off + kk), steps)
              for piece, off in ((d_ga, 0), (d_gb, steps))]
    pairs.append((d_f, (tmx, F_PAD), lambda i, kk: (i, 0), p["w_gates"], (d, F_PAD), lambda j, kk: (j, 2 * d // F_PAD), 1))
    dh1 = _mm_sum(f"dh_mix_{tag}", s, d, tmx, d, pairs, F32)
    dx0, sums = _prenorm_bwd(f"prenorm_bwd_mix_{tag}", dh1, sv["x"], row(p["g_pre_mix"]), row(mod[1]), dx1)
    d_scale_m, d_shift_m, gr["g_pre_mix"] = sums[0], sums[1], sums[2]

    d_mod = jnp.stack([d_shift_m, d_scale_m, d_gate_m, d_shift_f, d_scale_f, d_gate_f])
    return dx0, d_mod, gr


BIG = ("w_in", "w_glu", "w_pa", "w_pb", "w_o", "w_ffn_gate", "w_ffn_up", "w_ffn_down")
SMALL = ("b_ada", "g_pre_mix", "g_post_mix", "g_pre_ffn", "g_post_ffn", "lam_re", "lam_im", "log_dt", "b_re", "b_im",
         "c_re", "c_im", "d_skip", "b_glu", "b_f")
WEIGHTS = ("w_ada", "b_ada", "g_pre_mix", "g_post_mix", "g_pre_ffn", "g_post_ffn", "w_in", "lam_re", "lam_im", "log_dt",
           "b_re", "b_im", "c_re", "c_im", "d_skip", "w_glu", "b_glu", "b_f", "w_pa", "w_pb", "w_o", "w_ffn_gate",
           "w_ffn_up", "w_ffn_down")


def _prepare_layer(wts, small, l):
    w_in = jnp.concatenate([wts["w_in"][j, l] for j in range(N_CHIPS)], axis=1)
    d = w_in.shape[0]
    heads = small["b_f"].shape[1]
    n_groups, n_state, group_ch = small["b_re"].shape[1:]
    w_ssm = n_groups * group_ch
    w_att = wts["w_pb"].shape[2]
    n_main = w_ssm + 3 * w_att
    gpb = LANES // group_ch
    p = {}
    p["w_main"] = w_in[:, :n_main]
    p["w_gates"] = jnp.concatenate(
        [w_in[:, n_main + heads:], w_in[:, n_main:n_main + heads], jnp.zeros((d, F_PAD - heads), BF16)], axis=1)
    p["w_glu"] = wts["w_glu"][:, l].reshape(w_ssm, w_ssm)
    for n in ("g_pre_mix", "g_post_mix", "g_pre_ffn", "g_post_ffn", "d_skip", "b_glu", "b_f"):
        p[n] = small[n][l]
    ar, ai, br, bi = _discretize(small["lam_re"][l], small["lam_im"][l], small["log_dt"][l], small["b_re"][l], small["b_im"][l])
    p["tab_f"], p["tab_r"] = _scan_tables(ar.reshape(-1), ai.reshape(-1))
    bre = _block_diag(br.transpose(0, 2, 1), gpb)
    bim = _block_diag(bi.transpose(0, 2, 1), gpb)
    p["b_blk"] = jnp.concatenate([bre, bim], axis=2).astype(BF16)
    cre = _block_diag(small["c_re"][l].transpose(0, 2, 1), gpb)
    cim = _block_diag(small["c_im"][l].transpose(0, 2, 1), gpb)
    p["c_blk"] = jnp.concatenate([cre, -cim], axis=1).astype(BF16)
    return p


def _compact_partials(gr, n_state, group_ch):
    gpb = LANES // group_ch
    half = gpb * n_state
    out = dict(gr)
    out["bbar_re"] = _block_diag_extract(gr["b_blk"][:, :, :half], gpb, group_ch, n_state).transpose(0, 2, 1)
    out["bbar_im"] = _block_diag_extract(gr["b_blk"][:, :, half:], gpb, group_ch, n_state).transpose(0, 2, 1)
    out["c_re"] = _block_diag_extract(gr["c_blk"][:, :half, :], gpb, n_state, group_ch).transpose(0, 2, 1)
    out["c_im"] = -_block_diag_extract(gr["c_blk"][:, half:, :], gpb, n_state, group_ch).transpose(0, 2, 1)
    return out


def _small_grads_from_partials(gr, small, l):
    n_groups, n_state, _ = small["b_re"].shape[1:]
    ns2 = n_groups * n_state
    d_abar = jnp.sum(gr["a_bar"], axis=0)
    dar, dai = d_abar[:ns2].reshape(n_groups, n_state), d_abar[ns2:].reshape(n_groups, n_state)
    args = (small["lam_re"][l], small["lam_im"][l], small["log_dt"][l], small["b_re"][l], small["b_im"][l])
    _, vjp = jax.vjp(_discretize, *args)
    d_lam_re, d_lam_im, d_log_dt, d_b_re, d_b_im = vjp((dar, dai, gr["bbar_re"], gr["bbar_im"]))
    return dict(lam_re=d_lam_re, lam_im=d_lam_im, log_dt=d_log_dt, b_re=d_b_re, b_im=d_b_im,
                c_re=gr["c_re"], c_im=gr["c_im"])


def _fwd_bwd(xs, target, mods, layers, wts):
    depth = len(layers)
    saved = []
    act = xs
    for l in range(depth):
        act, sv = _layer_fwd(str(l), l, act, mods[l], layers[l], wts)
        saved.append(sv)
    dx, loss_blk = _loss_grad("loss", act, target)
    grads, d_mods = [None] * depth, [None] * depth
    for l in reversed(range(depth)):
        dx, d_mods[l], grads[l] = _layer_bwd(str(l), l, dx, mods[l], layers[l], wts, saved[l])
    stacked = {n: jnp.stack([grads[l][n] for l in range(depth)]) for n in BIG}
    return loss_blk, dx, d_mods, grads, stacked


def kernel(x, c, w_ada, b_ada, g_pre_mix, g_post_mix, g_pre_ffn, g_post_ffn, w_in, lam_re, lam_im, log_dt, b_re, b_im, c_re, c_im, d_skip, w_glu, b_glu, b_f, w_pa, w_pb, w_o, w_ffn_gate, w_ffn_up, w_ffn_down, loss_target, m_w_ada, m_b_ada, m_g_pre_mix, m_g_post_mix, m_g_pre_ffn, m_g_post_ffn, m_w_in, m_lam_re, m_lam_im, m_log_dt, m_b_re, m_b_im, m_c_re, m_c_im, m_d_skip, m_w_glu, m_b_glu, m_b_f, m_w_pa, m_w_pb, m_w_o, m_w_ffn_gate, m_w_ffn_up, m_w_ffn_down, v_w_ada, v_b_ada, v_g_pre_mix, v_g_post_mix, v_g_pre_ffn, v_g_post_ffn, v_w_in, v_lam_re, v_lam_im, v_log_dt, v_b_re, v_b_im, v_c_re, v_c_im, v_d_skip, v_w_glu, v_b_glu, v_b_f, v_w_pa, v_w_pb, v_w_o, v_w_ffn_gate, v_w_ffn_up, v_w_ffn_down):
    local = dict(locals())
    weights = {n: local[n] for n in WEIGHTS}
    moments_m = {n: local["m_" + n] for n in WEIGHTS}
    moments_v = {n: local["v_" + n] for n in WEIGHTS}
    depth, d = g_pre_mix.shape
    n_mod = w_ada.shape[2] * N_CHIPS // d
    mx, my, mc = lax.axis_index("x"), lax.axis_index("y"), lax.axis_index("c")
    my_chip = 2 * mx + my
    my_dev = 4 * mx + 2 * my + mc
    xs = x[0]

    assert depth == 2, "each core of a chip moves and reduces one layer"
    wts = dict(zip(BIG, _gather_weights("gather_weights", [weights[n].astype(BF16) for n in BIG])))
    small = {n: weights[n] for n in SMALL}
    layers = [_prepare_layer(wts, small, l) for l in range(depth)]

    c_pad = jnp.pad(c, ((0, SUBLANES - 1), (0, 0)))
    c_all = _all_gather("gather_cond", c_pad).reshape(N_DEV, SUBLANES, d)[:, 0, :]
    silu = lambda v: v * _sigmoid(v)
    n_cols = w_ada.shape[2]
    mod_shard = []
    for l in range(depth):
        bias = lax.dynamic_slice_in_dim(b_ada[l], my_chip * n_cols, n_cols)
        mod_shard.append(_mm_plain(f"ada_{l}", c_all, w_ada[l], "nn", F32, add=jnp.broadcast_to(bias, (N_DEV, n_cols)),
                                   a_fn=silu, tm=N_DEV, tn=512, tk=1024))
    mod_block = jnp.concatenate(mod_shard, axis=1)
    mod_all = _all_gather("gather_mod", mod_block).reshape(N_DEV, N_DEV, depth, n_cols)
    mod_rows = lax.dynamic_index_in_dim(mod_all[0::2], my_dev, axis=1, keepdims=False)
    mods = [mod_rows[:, l, :].reshape(n_mod, d) for l in range(depth)]

    loss_blk, dx, d_mods, grads, stacked = _fwd_bwd(xs, loss_target[0], mods, layers, wts)
    loss = lax.psum(loss_blk[0, 0], ("x", "y", "c"))
    grad_x = dx[None]

    core = mc.astype(jnp.int32).reshape(1)
    partials = [stacked[n] for n in BIG]
    from_sibling = _swap_layers("grads_swap_cores", partials)
    chip_parts = [_add_layer(f"grads_add_{n}", g, r, core) for n, g, r in zip(BIG, partials, from_sibling)]
    from_chips = _chip_exchange("grads_exchange_chips", chip_parts)
    reduced = [_sum_blocks(f"grads_sum_{n}", r, F32) for n, r in zip(BIG, from_chips)]
    big_grads = dict(zip(BIG, _share_layers("grads_share_cores", reduced)))

    partial_names = ("g_pre_mix", "g_post_mix", "g_pre_ffn", "g_post_ffn", "d_skip", "b_glu", "b_f", "a_bar",
                     "bbar_re", "bbar_im", "c_re", "c_im")
    n_state, group_ch = b_re.shape[2:]
    contrib = list(d_mods)
    for l in range(depth):
        compact = _compact_partials(grads[l], n_state, group_ch)
        contrib += [compact[n] for n in partial_names]
    contrib_shapes = [a.shape for a in contrib]
    block = _pack(contrib, LANES, BF16_ROWS, F32)
    rows = block.shape[0]
    all_blocks = _all_gather("gather_small_grads", block).reshape(N_DEV, rows, LANES)
    summed = _unpack(_sum_blocks("sum_small_grads", all_blocks, F32), contrib_shapes)
    per_layer = len(partial_names)
    small_grads = {n: [] for n in SMALL}
    d_mod_all = []
    for l in range(depth):
        small_grads["b_ada"].append(summed[l].reshape(-1))
        gl = dict(zip(partial_names, summed[depth + l * per_layer:depth + (l + 1) * per_layer]))
        for n in ("g_pre_mix", "g_post_mix", "g_pre_ffn", "g_post_ffn", "d_skip", "b_glu", "b_f"):
            small_grads[n].append(gl[n])
        for n, gval in _small_grads_from_partials(gl, small, l).items():
            small_grads[n].append(gval)
        d_mod_all.append(all_blocks.reshape(N_DEV, rows * LANES)[:, l * n_mod * d:(l + 1) * n_mod * d])
    small_grads = {n: jnp.stack(v) for n, v in small_grads.items()}

    g_w_ada = []
    for l in range(depth):
        cols = lax.dynamic_slice_in_dim(d_mod_all[l], my_chip * n_cols, n_cols, axis=1)
        g_w_ada.append(_mm_plain(f"dw_ada_{l}", c_all, cols, "tn", F32, a_fn=silu, tm=512, tn=512, tk=N_DEV))
    all_grads = dict(big_grads)
    all_grads.update(small_grads)
    all_grads["w_ada"] = jnp.stack(g_w_ada)

    delta, new_m, new_v = {}, {}, {}
    for n in ("w_ada",) + BIG:
        shape = weights[n].shape
        two_d = lambda a: a.reshape(-1, shape[-1])
        dl, nm, nv = _adamw(f"adamw_{n}", two_d(weights[n]), two_d(all_grads[n]), two_d(moments_m[n]), two_d(moments_v[n]))
        delta[n], new_m[n], new_v[n] = dl.reshape(shape), nm.reshape(shape), nv.reshape(shape)
    small_shapes = [weights[n].shape for n in SMALL]
    packed = [_pack([src[n] for n in SMALL], LANES, SUBLANES, F32) for src in (weights, all_grads, moments_m, moments_v)]
    outs = _adamw("adamw_small", *packed)
    for dst, buf in zip((delta, new_m, new_v), outs):
        dst.update(dict(zip(SMALL, _unpack(buf, small_shapes))))

    return (loss, grad_x, *[all_grads[n] for n in WEIGHTS], *[delta[n] for n in WEIGHTS],
            *[new_m[n] for n in WEIGHTS], *[new_v[n] for n in WEIGHTS])
```

```python
import functools
import math

import jax
import jax.numpy as jnp
from jax import lax
from jax.experimental import pallas as pl
from jax.experimental.pallas import tpu as pltpu

F32 = jnp.float32
BF16 = jnp.bfloat16
MESH = pl.DeviceIdType.MESH

RMS_EPS = 1e-6
EIG_CLIP = 1e-4
ADAM_LR, ADAM_B1, ADAM_B2, ADAM_EPS, ADAM_WD, ADAM_STEP = 0.001, 0.9, 0.999, 1e-08, 0.01, 10

LANES = 128
SUBLANES = 8
VMEM_LIMIT = 56 * 1024 * 1024
S5_ROWS = 256
S5_CHUNK = 1024
S5_UNROLL = 4
ATT_BLOCK = 512
F_PAD = 256
POSTNORM_ROWS = 512
N_CHIPS = 4
N_DEV = 8

NN = (((1,), (0,)), ((), ()))
NT = (((1,), (1,)), ((), ()))
TN = (((0,), (0,)), ((), ()))
_DN = {"nn": NN, "nt": NT, "tn": TN}


def _cparams(**kw):
    return pltpu.CompilerParams(vmem_limit_bytes=VMEM_LIMIT, **kw)


def _pick(dim, target):
    best, t = None, LANES
    while t <= min(dim, target):
        if dim % t == 0:
            best = t
        t += LANES
    return best or dim


def _sigmoid(x):
    return 1.0 / (1.0 + jnp.exp(-x))


def _dot(a, b, dn):
    return lax.dot_general(a, b, dn, preferred_element_type=F32)


def _mm_raw(name, a, b, mode, grid, acc_shape, a_spec, b_spec, out_shapes, out_specs, epilogue,
            extra=(), extra_specs=(), a_fn=None):
    nk = grid[2]
    n_extra, n_out = len(extra), len(out_shapes)

    def body(*refs):
        a_ref, b_ref = refs[0], refs[1]
        extra_refs = refs[2:2 + n_extra]
        out_refs = refs[2 + n_extra:2 + n_extra + n_out]
        acc = refs[-1]
        k = pl.program_id(2)

        @pl.when(k == 0)
        def _():
            acc[...] = jnp.zeros_like(acc)

        av = a_ref[...]
        if a_fn is not None:
            av = a_fn(av.astype(F32))
        acc[...] += _dot(av.astype(BF16), b_ref[...].astype(BF16), _DN[mode])

        @pl.when(k == nk - 1)
        def _():
            epilogue(acc[...], extra_refs, out_refs)

    return pl.pallas_call(
        body, name=name, grid=grid,
        in_specs=[a_spec, b_spec, *extra_specs],
        out_specs=list(out_specs), out_shape=list(out_shapes),
        scratch_shapes=[pltpu.VMEM(acc_shape, F32)],
        compiler_params=_cparams(),
    )(a, b, *extra)


def _mm(name, a, b, mode, out_shapes, out_specs, epilogue, extra=(), extra_specs=(),
        tm=512, tn=512, tk=512, a_fn=None):
    if mode == "nn":
        (m, kd), (_, n) = a.shape, b.shape
    elif mode == "nt":
        (m, kd), (n, _) = a.shape, b.shape
    else:
        (kd, m), (_, n) = a.shape, b.shape
    tm, tn, tk = _pick(m, tm), _pick(n, tn), _pick(kd, tk)
    if mode == "tn":
        a_spec = pl.BlockSpec((tk, tm), lambda i, j, k: (k, i))
    else:
        a_spec = pl.BlockSpec((tm, tk), lambda i, j, k: (i, k))
    if mode == "nt":
        b_spec = pl.BlockSpec((tn, tk), lambda i, j, k: (j, k))
    else:
        b_spec = pl.BlockSpec((tk, tn), lambda i, j, k: (k, j))
    res = _mm_raw(name, a, b, mode, (m // tm, n // tn, kd // tk), (tm, tn), a_spec, b_spec, out_shapes, out_specs,
                  epilogue, extra=extra, extra_specs=extra_specs, a_fn=a_fn)
    return res, (tm, tn, tk)


def _store(dtype):
    def epilogue(acc, extra_refs, out_refs):
        out_refs[0][...] = acc.astype(dtype)
    return epilogue


def _mm_sum(name, m, n, tm, tn, pairs, out_dtype):
    offs, total = [], 0
    for pr in pairs:
        offs.append(total)
        total += pr[6]
    n_p = len(pairs)

    def body(*refs):
        o_ref, acc = refs[2 * n_p], refs[2 * n_p + 1]
        k = pl.program_id(2)

        @pl.when(k == 0)
        def _():
            acc[...] = jnp.zeros_like(acc)

        for p_ in range(n_p):
            @pl.when((k >= offs[p_]) & (k < offs[p_] + pairs[p_][6]))
            def _(p_=p_):
                acc[...] += _dot(refs[2 * p_][...].astype(BF16), refs[2 * p_ + 1][...].astype(BF16), NT)

        @pl.when(k == total - 1)
        def _():
            o_ref[...] = acc[...].astype(out_dtype)

    in_specs, operands = [], []
    for (a, a_block, a_index, b, b_block, b_index, steps), off in zip(pairs, offs):
        local = lambda k, off=off, steps=steps: jnp.clip(k - off, 0, steps - 1)
        in_specs.append(pl.BlockSpec(a_block, lambda i, j, k, f=a_index, local=local: f(i, local(k))))
        in_specs.append(pl.BlockSpec(b_block, lambda i, j, k, f=b_index, local=local: f(j, local(k))))
        operands += [a, b]
    return pl.pallas_call(
        body, name=name, grid=(m // tm, n // tn, total), in_specs=in_specs,
        out_specs=pl.BlockSpec((tm, tn), lambda i, j, k: (i, j)), out_shape=jax.ShapeDtypeStruct((m, n), out_dtype),
        scratch_shapes=[pltpu.VMEM((tm, tn), F32)], compiler_params=_cparams(),
    )(*operands)


def _ffn_up(name, h, wg, wu, l):
    s, d = h.shape
    nc, fs = wg.shape[0], wg.shape[3]
    tm, tk = _pick(s, 1024), _pick(d, 1024)
    nk = d // tk

    def body(h_ref, wg_ref, wu_ref, a_ref, b_ref, hid_ref, acc_g, acc_u):
        k = pl.program_id(2)

        @pl.when(k == 0)
        def _():
            acc_g[...] = jnp.zeros_like(acc_g)
            acc_u[...] = jnp.zeros_like(acc_u)

        hv = h_ref[...]
        acc_g[...] += _dot(hv, wg_ref[...], NN)
        acc_u[...] += _dot(hv, wu_ref[...], NN)

        @pl.when(k == nk - 1)
        def _():
            av, bv = acc_g[...], acc_u[...]
            a_ref[...] = av.astype(BF16)
            b_ref[...] = bv.astype(BF16)
            hid_ref[...] = (av * _sigmoid(av) * bv).astype(BF16)

    w_spec = pl.BlockSpec((None, None, tk, fs), lambda i, j, k: (j, l, k, 0))
    o_spec = pl.BlockSpec((None, tm, fs), lambda i, j, k: (j, i, 0))
    sh = jax.ShapeDtypeStruct((nc, s, fs), BF16)
    return pl.pallas_call(
        body, name=name, grid=(s // tm, nc, nk),
        in_specs=[pl.BlockSpec((tm, tk), lambda i, j, k: (i, k)), w_spec, w_spec],
        out_specs=[o_spec] * 3, out_shape=[sh] * 3,
        scratch_shapes=[pltpu.VMEM((tm, fs), F32), pltpu.VMEM((tm, fs), F32)], compiler_params=_cparams(),
    )(h, wg, wu)


def _mm_plain(name, a, b, mode, out_dtype, add=None, a_fn=None, tm=512, tn=512, tk=512):
    if mode == "nn":
        m, n = a.shape[0], b.shape[1]
    elif mode == "nt":
        m, n = a.shape[0], b.shape[0]
    else:
        m, n = a.shape[1], b.shape[1]
    tm_, tn_ = _pick(m, tm), _pick(n, tn)
    spec = pl.BlockSpec((tm_, tn_), lambda i, j, k: (i, j))

    def epilogue(acc, extra_refs, out_refs):
        if add is not None:
            acc = acc + extra_refs[0][...]
        out_refs[0][...] = acc.astype(out_dtype)

    extra = () if add is None else (add,)
    (out,), _ = _mm(name, a, b, mode, [jax.ShapeDtypeStruct((m, n), out_dtype)], [spec], epilogue,
                    extra=extra, extra_specs=[spec] * len(extra), tm=tm, tn=tn, tk=tk, a_fn=a_fn)
    return out


def _row_tile(s, d):
    return _pick(s, max(SUBLANES, (1 << 20) // (4 * d)))


def _prenorm_fwd(name, x, g, scale, shift):
    s, d = x.shape
    tr = _row_tile(s, d)

    def body(x_ref, g_ref, sc_ref, sh_ref, h_ref):
        xv = x_ref[...]
        r = lax.rsqrt(jnp.mean(xv * xv, axis=-1, keepdims=True) + RMS_EPS)
        h_ref[...] = ((xv * r * g_ref[...]) * (1.0 + sc_ref[...]) + sh_ref[...]).astype(BF16)

    row = pl.BlockSpec((tr, d), lambda i: (i, 0))
    vec = pl.BlockSpec((1, d), lambda i: (0, 0))
    return pl.pallas_call(body, name=name, grid=(s // tr,), in_specs=[row, vec, vec, vec], out_specs=row,
                          out_shape=jax.ShapeDtypeStruct((s, d), BF16), compiler_params=_cparams())(x, g, scale, shift)


def _prenorm_bwd(name, dh, x, g, scale, dx_res):
    s, d = x.shape
    tr = _row_tile(s, d)

    def body(dh_ref, x_ref, g_ref, sc_ref, dxr_ref, dx_ref, sums_ref):
        @pl.when(pl.program_id(0) == 0)
        def _():
            sums_ref[...] = jnp.zeros_like(sums_ref)

        xv, dhv, gv = x_ref[...], dh_ref[...].astype(F32), g_ref[...]
        r = lax.rsqrt(jnp.mean(xv * xv, axis=-1, keepdims=True) + RMS_EPS)
        xhat = xv * r
        dxn = dhv * (1.0 + sc_ref[...])
        dxhat = dxn * gv
        dx = r * (dxhat - xhat * jnp.mean(dxhat * xhat, axis=-1, keepdims=True))
        dx_ref[...] = dxr_ref[...] + dx
        sums_ref[0:1, :] += jnp.sum(dhv * (xhat * gv), axis=0, keepdims=True)
        sums_ref[1:2, :] += jnp.sum(dhv, axis=0, keepdims=True)
        sums_ref[2:3, :] += jnp.sum(dxn * xhat, axis=0, keepdims=True)

    row = pl.BlockSpec((tr, d), lambda i: (i, 0))
    vec = pl.BlockSpec((1, d), lambda i: (0, 0))
    acc = pl.BlockSpec((SUBLANES, d), lambda i: (0, 0))
    return pl.pallas_call(
        body, name=name, grid=(s // tr,), in_specs=[row, row, vec, vec, row], out_specs=[row, acc],
        out_shape=[jax.ShapeDtypeStruct((s, d), F32), jax.ShapeDtypeStruct((SUBLANES, d), F32)],
        compiler_params=_cparams())(dh, x, g, scale, dx_res)


def _postnorm_bwd(name, dxn, y, g, gate):
    s, d = y.shape
    tr = _row_tile(s, d)

    def body(dx_ref, y_ref, g_ref, gt_ref, dy_ref, sums_ref):
        @pl.when(pl.program_id(0) == 0)
        def _():
            sums_ref[...] = jnp.zeros_like(sums_ref)

        yv, dxv, gv = y_ref[...], dx_ref[...], g_ref[...]
        r = lax.rsqrt(jnp.mean(yv * yv, axis=-1, keepdims=True) + RMS_EPS)
        yhat = yv * r
        dn = dxv * gt_ref[...]
        dyhat = dn * gv
        dy_ref[...] = (r * (dyhat - yhat * jnp.mean(dyhat * yhat, axis=-1, keepdims=True))).astype(BF16)
        sums_ref[0:1, :] += jnp.sum(dxv * (yhat * gv), axis=0, keepdims=True)
        sums_ref[1:2, :] += jnp.sum(dn * yhat, axis=0, keepdims=True)

    row = pl.BlockSpec((tr, d), lambda i: (i, 0))
    vec = pl.BlockSpec((1, d), lambda i: (0, 0))
    acc = pl.BlockSpec((SUBLANES, d), lambda i: (0, 0))
    return pl.pallas_call(
        body, name=name, grid=(s // tr,), in_specs=[row, row, vec, vec], out_specs=[row, acc],
        out_shape=[jax.ShapeDtypeStruct((s, d), BF16), jax.ShapeDtypeStruct((SUBLANES, d), F32)],
        compiler_params=_cparams())(dxn, y, g, gate)


def _loss_grad(name, y, target):
    s, d = y.shape
    tr = _row_tile(s, d)

    def body(y_ref, t_ref, dy_ref, loss_ref):
        @pl.when(pl.program_id(0) == 0)
        def _():
            loss_ref[...] = jnp.zeros_like(loss_ref)

        err = y_ref[...] - t_ref[...]
        dy_ref[...] = err * (1.0 / d)
        part = jnp.sum(jnp.sum(err * err, axis=-1, keepdims=True), axis=0, keepdims=True) * (0.5 / d)
        loss_ref[...] += jnp.broadcast_to(part, loss_ref.shape)

    row = pl.BlockSpec((tr, d), lambda i: (i, 0))
    acc = pl.BlockSpec((SUBLANES, LANES), lambda i: (0, 0))
    return pl.pallas_call(
        body, name=name, grid=(s // tr,), in_specs=[row, row], out_specs=[row, acc],
        out_shape=[jax.ShapeDtypeStruct((s, d), F32), jax.ShapeDtypeStruct((SUBLANES, LANES), F32)],
        compiler_params=_cparams())(y, target)


def _gelu(y):
    c = math.sqrt(2.0 / math.pi)
    return 0.5 * y * (1.0 + jnp.tanh(c * (y + 0.044715 * (y * y * y))))


def _gelu_grad(y):
    c = math.sqrt(2.0 / math.pi)
    th = jnp.tanh(c * (y + 0.044715 * (y * y * y)))
    return 0.5 * (1.0 + th) + 0.5 * y * (1.0 - th * th) * c * (1.0 + 3.0 * 0.044715 * (y * y))


def _cmul_add(br, bi, ar, ai, xr, xi):
    return br + ar * xr - ai * xi, bi + ar * xi + ai * xr


def _scan_rows(x_ref, row0, n_steps, ns2, pow_ref, tab_ref, carry_ref, reverse, fold=None):
    assert n_steps % SUBLANES == 0
    wc = min(S5_CHUNK, ns2)
    sub = lax.broadcasted_iota(jnp.int32, (SUBLANES, wc), 0)
    unroll = S5_UNROLL if n_steps % S5_UNROLL == 0 else 1
    for c0 in range(0, ns2, wc):
        re = slice(c0, c0 + wc)
        im = slice(ns2 + c0, ns2 + c0 + wc)
        first_power = slice(n_steps - 1, n_steps) if reverse else slice(0, 1)
        ar = jnp.broadcast_to(pow_ref[first_power, re], (SUBLANES, wc))
        ai = jnp.broadcast_to(pow_ref[first_power, im], (SUBLANES, wc))
        rows = lambda r: pl.ds(pl.multiple_of(row0 + r * SUBLANES, SUBLANES), SUBLANES)
        step_of = lambda i: (n_steps - 1 - i) if reverse else i

        def local(i, carry, re=re, im=im, ar=ar, ai=ai):
            for u in range(unroll):
                r = step_of(i * unroll + u)
                carry = _cmul_add(x_ref[rows(r), re], x_ref[rows(r), im], ar, ai, *carry)
                x_ref[rows(r), re], x_ref[rows(r), im] = carry
            return carry

        zero = jnp.zeros((SUBLANES, wc), F32)
        lr, li = lax.fori_loop(0, n_steps // unroll, local, (zero, zero))

        tabs = [tab_ref[k, :, re] for k in range(8)]
        for lvl, k in enumerate((1, 2, 4)):
            sh = (SUBLANES - k) if reverse else k
            lr, li = _cmul_add(lr, li, tabs[2 * lvl], tabs[2 * lvl + 1], pltpu.roll(lr, sh, 0), pltpu.roll(li, sh, 0))
        cr, ci = carry_ref[0:1, re], carry_ref[0:1, im]
        lr, li = _cmul_add(lr, li, tabs[6], tabs[7], cr, ci)
        edge, away, last = (SUBLANES - 1, SUBLANES - 1, 0) if reverse else (0, 1, SUBLANES - 1)
        carry_ref[0:1, re] = lr[last:last + 1, :]
        carry_ref[0:1, im] = li[last:last + 1, :]
        er = jnp.where(sub == edge, cr, pltpu.roll(lr, away, 0))
        ei = jnp.where(sub == edge, ci, pltpu.roll(li, away, 0))

        def fix(j, acc, re=re, im=im, er=er, ei=ei, c0=c0):
            base = pl.ds(pl.multiple_of(j * SUBLANES, SUBLANES), SUBLANES)
            pw_r, pw_i = pow_ref[base, re], pow_ref[base, im]
            for i in range(SUBLANES):
                r = j * SUBLANES + i
                xr, xi = _cmul_add(x_ref[rows(r), re], x_ref[rows(r), im], pw_r[i:i + 1, :], pw_i[i:i + 1, :], er, ei)
                x_ref[rows(r), re], x_ref[rows(r), im] = xr, xi
                if fold is not None:
                    acc = fold(c0, r, xr, xi, acc)
            return acc

        acc = lax.fori_loop(0, n_steps // SUBLANES, fix, (zero, zero) if fold is not None else 0)
        if fold is not None:
            fold(c0, None, None, None, acc)


def _s5_fwd(name, u, b_blk, c_blk, a_f, tab_f, dskip, w_glu, b_glu):
    s, w = u.shape[0], w_glu.shape[0]
    nkb = w // LANES
    ns2 = b_blk.shape[2] // 2 * nkb
    half = ns2 // nkb
    t = min(S5_ROWS, s)
    nblk = s // t

    def body(u_ref, b_ref, c_ref, a_ref, tab_ref, ds_ref, wg_ref, bg_ref, y_ref, ys_ref, cs_ref, xs, carry):
        @pl.when(pl.program_id(0) == 0)
        def _():
            carry[...] = jnp.zeros_like(carry)

        cs_ref[0] = carry[...]
        for kb in range(nkb):
            bu = _dot(u_ref[:, kb * LANES:(kb + 1) * LANES], b_ref[kb], NN)
            xs[:, kb * half:(kb + 1) * half] = bu[:, :half]
            xs[:, ns2 + kb * half:ns2 + (kb + 1) * half] = bu[:, half:]
        _scan_rows(xs, 0, t // SUBLANES, ns2, a_ref, tab_ref, carry, reverse=False)
        for kb in range(nkb):
            cols = slice(kb * LANES, (kb + 1) * LANES)
            yk = _dot(xs[:, kb * half:(kb + 1) * half].astype(BF16), c_ref[kb, :half, :], NN)
            yk += _dot(xs[:, ns2 + kb * half:ns2 + (kb + 1) * half].astype(BF16), c_ref[kb, half:, :], NN)
            y_ref[:, cols] = yk + ds_ref[:, cols] * u_ref[:, cols].astype(F32)
        z = _gelu(y_ref[...])
        gate = _sigmoid(_dot(z.astype(BF16), wg_ref[...], NN) + bg_ref[...])
        ys_ref[...] = (z * gate).astype(BF16)

    row = pl.BlockSpec((t, w), lambda i: (i, 0))
    full = lambda shape: pl.BlockSpec(shape, lambda i: (0,) * len(shape))
    return pl.pallas_call(
        body, name=name, grid=(nblk,),
        in_specs=[row, full(b_blk.shape), full(c_blk.shape), full(a_f.shape), full(tab_f.shape), full(dskip.shape),
                  full(w_glu.shape), full(b_glu.shape)],
        out_specs=[row, row, pl.BlockSpec((1, 1, 2 * ns2), lambda i: (i, 0, 0))],
        out_shape=[jax.ShapeDtypeStruct((s, w), F32), jax.ShapeDtypeStruct((s, w), BF16),
                   jax.ShapeDtypeStruct((nblk, 1, 2 * ns2), F32)],
        scratch_shapes=[pltpu.VMEM((t, 2 * ns2), F32), pltpu.VMEM((1, 2 * ns2), F32)],
        compiler_params=_cparams(),
    )(u, b_blk, c_blk, a_f, tab_f, dskip, w_glu, b_glu)


def _s5_bwd(name, u, dys, y, carries, b_blk, c_blk, a_f, a_r, tab_f, tab_r, dskip, w_glu, b_glu):
    s, w = u.shape[0], w_glu.shape[0]
    nkb = w // LANES
    ns2 = b_blk.shape[2] // 2 * nkb
    half = ns2 // nkb
    t = min(S5_ROWS, s)
    nblk = s // t
    ng = t // SUBLANES

    def body(u_ref, dys_ref, y_ref, cs_ref, b_ref, c_ref, af_ref, ar_ref, tabf_ref, tabr_ref, ds_ref, wg_ref, bg_ref,
             du_ref, db_ref, dc_ref, da_ref, dwg_ref, vec_ref, xs, gs, dyv, fcarry, gcarry):
        @pl.when(pl.program_id(0) == 0)
        def _():
            db_ref[...] = jnp.zeros_like(db_ref)
            dc_ref[...] = jnp.zeros_like(dc_ref)
            da_ref[...] = jnp.zeros_like(da_ref)
            dwg_ref[...] = jnp.zeros_like(dwg_ref)
            vec_ref[...] = jnp.zeros_like(vec_ref)
            gcarry[...] = jnp.zeros_like(gcarry)

        yv = y_ref[...]
        z = _gelu(yv)
        zb = z.astype(BF16)
        gate = _sigmoid(_dot(zb, wg_ref[...], NN) + bg_ref[...])
        dout = dys_ref[...].astype(F32)
        dt = dout * z * gate * (1.0 - gate)
        dtb = dt.astype(BF16)
        dz = dout * gate + _dot(dtb, wg_ref[...], NT)
        dy = dz * _gelu_grad(yv)
        dyv[...] = dy
        dwg_ref[...] += _dot(zb, dtb, TN)
        vec_ref[0:1, :] += jnp.sum(dt, axis=0, keepdims=True)
        vec_ref[1:2, :] += jnp.sum(dy * u_ref[...].astype(F32), axis=0, keepdims=True)

        fcarry[...] = cs_ref[0]
        xs[0:SUBLANES, :] = jnp.broadcast_to(cs_ref[0], (SUBLANES, 2 * ns2))
        for kb in range(nkb):
            bu = _dot(u_ref[:, kb * LANES:(kb + 1) * LANES], b_ref[kb], NN)
            xs[SUBLANES:, kb * half:(kb + 1) * half] = bu[:, :half]
            xs[SUBLANES:, ns2 + kb * half:ns2 + (kb + 1) * half] = bu[:, half:]
        _scan_rows(xs, SUBLANES, ng, ns2, af_ref, tabf_ref, fcarry, reverse=False)
        first_segment = lax.broadcasted_iota(jnp.int32, (SUBLANES, 2 * ns2), 0) == 0
        xs[0:SUBLANES, :] = jnp.where(first_segment, xs[0:SUBLANES, :], pltpu.roll(xs[t:t + SUBLANES, :], 1, 0))

        for kb in range(nkb):
            dyk = dyv[:, kb * LANES:(kb + 1) * LANES].astype(BF16)
            re = slice(kb * half, (kb + 1) * half)
            im = slice(ns2 + kb * half, ns2 + (kb + 1) * half)
            gs[:, re] = _dot(dyk, c_ref[kb, :half, :], NT)
            gs[:, im] = _dot(dyk, c_ref[kb, half:, :], NT)
            dc_ref[kb, :half, :] += _dot(xs[SUBLANES:, re].astype(BF16), dyk, TN)
            dc_ref[kb, half:, :] += _dot(xs[SUBLANES:, im].astype(BF16), dyk, TN)

        def fold(c0, r, gr, gi, acc):
            wc = min(S5_CHUNK, ns2)
            re = slice(c0, c0 + wc)
            im = slice(ns2 + c0, ns2 + c0 + wc)
            if r is None:
                da_ref[:, re] += acc[0]
                da_ref[:, im] += acc[1]
                return acc
            before = pl.ds(pl.multiple_of(r * SUBLANES, SUBLANES), SUBLANES)
            xpr, xpi = xs[before, re], xs[before, im]
            return acc[0] + gr * xpr + gi * xpi, acc[1] - gr * xpi + gi * xpr

        _scan_rows(gs, 0, ng, ns2, ar_ref, tabr_ref, gcarry, reverse=True, fold=fold)

        for kb in range(nkb):
            cols = slice(kb * LANES, (kb + 1) * LANES)
            re = slice(kb * half, (kb + 1) * half)
            im = slice(ns2 + kb * half, ns2 + (kb + 1) * half)
            uk = u_ref[:, cols]
            gr = gs[:, re].astype(BF16)
            gi = gs[:, im].astype(BF16)
            db_ref[kb, :, :half] += _dot(uk, gr, TN)
            db_ref[kb, :, half:] += _dot(uk, gi, TN)
            duk = _dot(gr, b_ref[kb, :, :half], NT) + _dot(gi, b_ref[kb, :, half:], NT)
            du_ref[:, cols] = (duk + ds_ref[:, cols] * dyv[:, cols]).astype(BF16)

    rev = lambda i: (nblk - 1 - i, 0)
    row = pl.BlockSpec((t, w), rev)
    full = lambda shape: pl.BlockSpec(shape, lambda i: (0,) * len(shape))
    return pl.pallas_call(
        body, name=name, grid=(nblk,),
        in_specs=[row, row, row, pl.BlockSpec((1, 1, 2 * ns2), lambda i: (nblk - 1 - i, 0, 0)),
                  full(b_blk.shape), full(c_blk.shape), full(a_f.shape), full(a_r.shape), full(tab_f.shape),
                  full(tab_r.shape), full(dskip.shape), full(w_glu.shape), full(b_glu.shape)],
        out_specs=[row, full(b_blk.shape), full(c_blk.shape), full((SUBLANES, 2 * ns2)), full((w, w)),
                   full((SUBLANES, w))],
        out_shape=[jax.ShapeDtypeStruct((s, w), BF16), jax.ShapeDtypeStruct(b_blk.shape, F32),
                   jax.ShapeDtypeStruct(c_blk.shape, F32), jax.ShapeDtypeStruct((SUBLANES, 2 * ns2), F32),
                   jax.ShapeDtypeStruct((w, w), F32), jax.ShapeDtypeStruct((SUBLANES, w), F32)],
        scratch_shapes=[pltpu.VMEM((t + SUBLANES, 2 * ns2), F32), pltpu.VMEM((t, 2 * ns2), F32),
                        pltpu.VMEM((t, w), F32), pltpu.VMEM((1, 2 * ns2), F32), pltpu.VMEM((1, 2 * ns2), F32)],
        compiler_params=_cparams(),
    )(u, dys, y, carries, b_blk, c_blk, a_f, a_r, tab_f, tab_r, dskip, w_glu, b_glu)


def _log_sigmoid(x):
    return jnp.minimum(x, 0.0) - jnp.log(1.0 + jnp.exp(-jnp.abs(x)))


def _cum_fwd(name, f_t, b_f):
    h, s = f_t.shape
    tc = _pick(s, 512)
    nb = s // tc

    def body(f_ref, b_ref, c_ref, carry):
        @pl.when(pl.program_id(0) == 0)
        def _():
            carry[...] = jnp.zeros_like(carry)

        lf = _log_sigmoid(f_ref[...] + b_ref[...])
        upper = (lax.broadcasted_iota(jnp.int32, (tc, tc), 0) <= lax.broadcasted_iota(jnp.int32, (tc, tc), 1))
        cum = lax.dot_general(lf, upper.astype(F32), NN, precision=lax.Precision.HIGHEST,
                              preferred_element_type=F32) + carry[...]
        c_ref[...] = cum
        carry[...] += jnp.sum(lf, axis=1, keepdims=True)

    blk = pl.BlockSpec((h, tc), lambda i: (0, i))
    return pl.pallas_call(body, name=name, grid=(nb,), in_specs=[blk, pl.BlockSpec((h, 1), lambda i: (0, 0))],
                          out_specs=blk, out_shape=jax.ShapeDtypeStruct((h, s), F32),
                          scratch_shapes=[pltpu.VMEM((h, 1), F32)], compiler_params=_cparams())(f_t, b_f)


def _cum_bwd(name, dcq, dck, f_t, b_f):
    h, s = f_t.shape
    tc = _pick(s, 512)
    nb = s // tc

    def body(dcq_ref, dck_ref, f_ref, b_ref, df_ref, db_ref, carry):
        @pl.when(pl.program_id(0) == 0)
        def _():
            carry[...] = jnp.zeros_like(carry)
            db_ref[...] = jnp.zeros_like(db_ref)

        dc = dcq_ref[...] + dck_ref[...]
        lower = (lax.broadcasted_iota(jnp.int32, (tc, tc), 0) >= lax.broadcasted_iota(jnp.int32, (tc, tc), 1))
        dlf = lax.dot_general(dc, lower.astype(F32), NN, precision=lax.Precision.HIGHEST,
                              preferred_element_type=F32) + carry[...]
        carry[...] += jnp.sum(dc, axis=1, keepdims=True)
        df = dlf * _sigmoid(-(f_ref[...] + b_ref[...]))
        df_ref[...] = df
        db_ref[...] += jnp.broadcast_to(jnp.sum(df, axis=1, keepdims=True), db_ref.shape)

    blk = pl.BlockSpec((h, tc), lambda i: (0, nb - 1 - i))
    return pl.pallas_call(
        body, name=name, grid=(nb,), in_specs=[blk, blk, blk, pl.BlockSpec((h, 1), lambda i: (0, 0))],
        out_specs=[blk, pl.BlockSpec((h, LANES), lambda i: (0, 0))],
        out_shape=[jax.ShapeDtypeStruct((h, s), F32), jax.ShapeDtypeStruct((h, LANES), F32)],
        scratch_shapes=[pltpu.VMEM((h, 1), F32)], compiler_params=_cparams())(dcq, dck, f_t, b_f)


def _attn_fwd(name, qkv, q_blk, k_blk, v_blk, n_pairs, ck):
    s = qkv.shape[0]
    dh = LANES // 2
    t = min(ATT_BLOCK, s)
    nq = s // t
    scale = dh ** -0.5

    def body(q_ref, k_ref, v_ref, ck_ref, o_ref, lse_ref, m_s, acc_s):
        i = pl.program_id(1)
        low = lax.broadcasted_iota(jnp.int32, (1, LANES), 1) < dh
        qs = (q_ref[...].astype(F32) * scale).astype(BF16)
        zero = jnp.zeros_like(qs)
        qh = (jnp.where(low, qs, zero), jnp.where(low, zero, qs))
        m_s[...] = jnp.full(m_s.shape, -1e30, F32)
        acc_s[...] = jnp.zeros_like(acc_s)
        causal = (lax.broadcasted_iota(jnp.int32, (t, t), 1) <= lax.broadcasted_iota(jnp.int32, (t, t), 0))

        def step(j, diagonal):
            r0 = pl.multiple_of(j * t, t)
            kj = k_ref[pl.ds(r0, t), :]
            vj = v_ref[pl.ds(r0, t), :]
            one = jnp.ones_like(vj)
            vh = (jnp.where(low, vj, one), jnp.where(low, one, vj))
            for hd in range(2):
                sc = _dot(qh[hd], kj, NT) - ck_ref[hd, j]
                if diagonal:
                    sc = jnp.where(causal, sc, -1e30)
                m_old = m_s[hd]
                m_new = jnp.maximum(m_old, jnp.max(sc, axis=1, keepdims=True))
                p = jnp.exp(sc - m_new)
                acc_s[hd] = jnp.exp(m_old - m_new) * acc_s[hd] + _dot(p.astype(BF16), vh[hd], NN)
                m_s[hd] = m_new

        def full(j, _):
            step(j, False)
            return 0

        lax.fori_loop(0, i, full, 0)
        step(i, True)
        a0, a1 = acc_s[0], acc_s[1]
        o_ref[...] = jnp.where(low, a0 / pltpu.roll(a0, dh, 1), a1 / pltpu.roll(a1, dh, 1)).astype(BF16)
        lse_ref[0] = m_s[0] + jnp.log(a0[:, dh:dh + 1])
        lse_ref[1] = m_s[1] + jnp.log(a1[:, 0:1])

    return pl.pallas_call(
        body, name=name, grid=(n_pairs, nq),
        in_specs=[pl.BlockSpec((t, LANES), lambda hp, i: (i, q_blk + hp)),
                  pl.BlockSpec((s, LANES), lambda hp, i: (0, k_blk + hp)),
                  pl.BlockSpec((s, LANES), lambda hp, i: (0, v_blk + hp)),
                  pl.BlockSpec((2, nq, 1, t), lambda hp, i: (hp, 0, 0, 0))],
        out_specs=[pl.BlockSpec((t, LANES), lambda hp, i: (i, hp)), pl.BlockSpec((2, t, 1), lambda hp, i: (hp, i, 0))],
        out_shape=[jax.ShapeDtypeStruct((s, LANES * n_pairs), BF16), jax.ShapeDtypeStruct((2 * n_pairs, s, 1), F32)],
        scratch_shapes=[pltpu.VMEM((2, t, 1), F32), pltpu.VMEM((2, t, LANES), F32)],
        compiler_params=_cparams(),
    )(qkv, qkv, qkv, ck)


def _attn_bwd(name, qkv, q_blk, k_blk, v_blk, n_pairs, o, do, lse_rows, ck_cols):
    s = qkv.shape[0]
    dh = LANES // 2
    t = min(ATT_BLOCK, s)
    nk = s // t
    scale = dh ** -0.5

    def body(q_ref, k_ref, v_ref, o_ref, do_ref, lse_ref, ck_ref,
             dq_ref, dk_ref, dv_ref, dcq_ref, dck_ref, delta, dqt, dk_acc, dv_acc):
        j = pl.program_id(1)
        low = lax.broadcasted_iota(jnp.int32, (1, LANES), 1) < dh
        low_rows = lax.broadcasted_iota(jnp.int32, (LANES, 1), 0) < dh

        @pl.when(j == 0)
        def _():
            dqt[...] = jnp.zeros_like(dqt)
            sel = (jnp.broadcast_to(low, (SUBLANES, LANES)).astype(F32), jnp.broadcast_to(~low, (SUBLANES, LANES)).astype(F32))

            def fill(i, _):
                r0 = pl.multiple_of(i * t, t)
                prod = do_ref[pl.ds(r0, t), :].astype(F32) * o_ref[pl.ds(r0, t), :].astype(F32)
                for hd in range(2):
                    delta[hd, i] = lax.dot_general(sel[hd], prod, NT, precision=lax.Precision.HIGHEST,
                                                   preferred_element_type=F32)
                return 0

            lax.fori_loop(0, nk, fill, 0)

        kj, vj = k_ref[...], v_ref[...]
        zero, one = jnp.zeros_like(kj), jnp.ones_like(kj)
        kh = (jnp.where(low, kj, zero), jnp.where(low, zero, kj))
        vh = (jnp.where(low, vj, zero), jnp.where(low, zero, vj))
        kjt = kj.astype(F32).T.astype(BF16)
        one_t = jnp.ones_like(kjt)
        kht = (jnp.where(low_rows, kjt, one_t), jnp.where(low_rows, one_t, kjt))
        dk_acc[...] = jnp.zeros_like(dk_acc)
        dv_acc[...] = jnp.zeros_like(dv_acc)
        causal_t = (lax.broadcasted_iota(jnp.int32, (t, t), 0) <= lax.broadcasted_iota(jnp.int32, (t, t), 1))

        def step(i, diagonal):
            r0 = pl.multiple_of(i * t, t)
            qi = (q_ref[pl.ds(r0, t), :].astype(F32) * scale).astype(BF16)
            doi = do_ref[pl.ds(r0, t), :]
            qone, dzero = jnp.ones_like(qi), jnp.zeros_like(doi)
            qsel = (jnp.where(low, qi, qone), jnp.where(low, qone, qi))
            dosel = (jnp.where(low, doi, dzero), jnp.where(low, dzero, doi))
            for hd in range(2):
                st = _dot(kh[hd], qi, NT) - ck_ref[hd] - lse_ref[hd, i]
                pt = jnp.exp(st)
                if diagonal:
                    pt = jnp.where(causal_t, pt, 0.0)
                dst = pt * (_dot(vh[hd], doi, NT) - delta[hd, i, 0:1, :])
                dsb = dst.astype(BF16)
                dv_acc[...] += _dot(pt.astype(BF16), dosel[hd], NN)
                dk_acc[hd] += _dot(dsb, qsel[hd], NN)
                dqt[hd, i] += _dot(kht[hd], dsb, NN)

        step(j, True)

        def rest(i, _):
            step(i, False)
            return 0

        lax.fori_loop(j + 1, nk, rest, 0)
        dk_ref[...] = jnp.where(low, dk_acc[0], dk_acc[1]).astype(BF16)
        dv_ref[...] = dv_acc[...].astype(BF16)
        dck_ref[0] = -dk_acc[0][:, dh:dh + 1]
        dck_ref[1] = -dk_acc[1][:, 0:1]

        @pl.when(j == nk - 1)
        def _():
            def emit(i, _):
                r0 = pl.multiple_of(i * t, t)
                d0, d1 = dqt[0, i], dqt[1, i]
                dq_ref[pl.ds(r0, t), :] = (jnp.where(low_rows, d0, d1) * scale).T.astype(BF16)
                dcq_ref[0, i] = d0[dh:dh + 1, :]
                dcq_ref[1, i] = d1[0:1, :]
                return 0

            lax.fori_loop(0, nk, emit, 0)

    col_blk = lambda base: pl.BlockSpec((t, LANES), lambda hp, j: (j, base + hp))
    col_all = lambda base: pl.BlockSpec((s, LANES), lambda hp, j: (0, base + hp))
    rows_all = pl.BlockSpec((2, nk, 1, t), lambda hp, j: (hp, 0, 0, 0))
    return pl.pallas_call(
        body, name=name, grid=(n_pairs, nk),
        in_specs=[col_all(q_blk), col_blk(k_blk), col_blk(v_blk), col_all(0), col_all(0), rows_all,
                  pl.BlockSpec((2, t, 1), lambda hp, j: (hp, j, 0))],
        out_specs=[col_all(0), col_blk(0), col_blk(0), rows_all, pl.BlockSpec((2, t, 1), lambda hp, j: (hp, j, 0))],
        out_shape=[jax.ShapeDtypeStruct((s, LANES * n_pairs), BF16), jax.ShapeDtypeStruct((s, LANES * n_pairs), BF16),
                   jax.ShapeDtypeStruct((s, LANES * n_pairs), BF16), jax.ShapeDtypeStruct((2 * n_pairs, nk, 1, t), F32),
                   jax.ShapeDtypeStruct((2 * n_pairs, s, 1), F32)],
        scratch_shapes=[pltpu.VMEM((2, nk, SUBLANES, t), F32), pltpu.VMEM((2, nk, LANES, t), F32),
                        pltpu.VMEM((2, t, LANES), F32), pltpu.VMEM((t, LANES), F32)],
        compiler_params=_cparams(),
    )(qkv, qkv, qkv, o, do, lse_rows, ck_cols)


def _adamw(name, w, g, m, v):
    r, c = w.shape
    tr = _pick8(r, max(SUBLANES, (1 << 20) // (4 * c)))

    def body(w_ref, g_ref, m_ref, v_ref, d_ref, mo_ref, vo_ref):
        gv = g_ref[...]
        m2 = ADAM_B1 * m_ref[...] + (1.0 - ADAM_B1) * gv
        v2 = ADAM_B2 * v_ref[...] + (1.0 - ADAM_B2) * (gv * gv)
        m_hat = m2 / (1.0 - ADAM_B1 ** ADAM_STEP)
        v_hat = v2 / (1.0 - ADAM_B2 ** ADAM_STEP)
        d_ref[...] = -ADAM_LR * (m_hat / (jnp.sqrt(v_hat) + ADAM_EPS) + ADAM_WD * w_ref[...])
        mo_ref[...] = m2
        vo_ref[...] = v2

    blk = pl.BlockSpec((tr, c), lambda i: (i, 0))
    sh = jax.ShapeDtypeStruct((r, c), F32)
    return pl.pallas_call(body, name=name, grid=(r // tr,), in_specs=[blk] * 4, out_specs=[blk] * 3,
                          out_shape=[sh, sh, sh], compiler_params=_cparams())(w, g, m, v)


def _pick8(dim, target, mult=SUBLANES):
    best, t = None, mult
    while t <= min(dim, target):
        if dim % t == 0:
            best = t
        t += mult
    return best or dim


BF16_ROWS = 16


def _sum_blocks(name, x, out_dtype):
    n, r, c = x.shape
    tr = _pick8(r, max(BF16_ROWS, (1 << 19) // (4 * c)), BF16_ROWS)

    def body(x_ref, o_ref):
        acc = x_ref[0].astype(F32)
        for i in range(1, n):
            acc = acc + x_ref[i].astype(F32)
        o_ref[...] = acc.astype(out_dtype)

    return pl.pallas_call(body, name=name, grid=(r // tr,),
                          in_specs=[pl.BlockSpec((n, tr, c), lambda i: (0, i, 0))],
                          out_specs=pl.BlockSpec((tr, c), lambda i: (i, 0)),
                          out_shape=jax.ShapeDtypeStruct((r, c), out_dtype), compiler_params=_cparams())(x)


def _add_layer(name, grads, recv, core):
    _, n, r, c = grads.shape
    tr = _pick8(r, max(BF16_ROWS, (1 << 19) // (4 * c)), BF16_ROWS)

    def body(core_ref, g_ref, r_ref, o_ref):
        o_ref[...] = (g_ref[...].astype(F32) + r_ref[...].astype(F32)).astype(BF16)

    grid_spec = pltpu.PrefetchScalarGridSpec(
        num_scalar_prefetch=1, grid=(r // tr,),
        in_specs=[pl.BlockSpec((None, n, tr, c), lambda i, core_ref: (core_ref[0], 0, i, 0)),
                  pl.BlockSpec((n, tr, c), lambda i, core_ref: (0, i, 0))],
        out_specs=pl.BlockSpec((n, tr, c), lambda i, core_ref: (0, i, 0)))
    return pl.pallas_call(body, name=name, grid_spec=grid_spec,
                          out_shape=jax.ShapeDtypeStruct((n, r, c), BF16), compiler_params=_cparams())(core, grads, recv)


def _all_gather(name, x_shard):
    m_per, n = x_shard.shape

    def body(x_ref, out_ref, send_sems, recv_sems):
        x, y, c = lax.axis_index("x"), lax.axis_index("y"), lax.axis_index("c")
        me, sibling = (x, y, c), (x, y, 1 - c)
        chips = [(1 - x, y), (x, 1 - y), (1 - x, 1 - y)]

        def rows(px, py, pc):
            return out_ref.at[pl.ds((4 * px + 2 * py + pc) * m_per, m_per), :]

        def copy(k, block, to, src=None):
            return pltpu.make_async_remote_copy(
                src_ref=rows(*block) if src is None else src, dst_ref=rows(*block),
                send_sem=send_sems.at[k], recv_sem=recv_sems.at[k], device_id=to, device_id_type=MESH)

        first = [copy(0, me, sibling, src=x_ref)]
        first += [copy(1 + j, me, (*chip, c), src=x_ref) for j, chip in enumerate(chips)]
        for cp in first:
            cp.start()
        passed = [copy(4 + j, (*chip, c), sibling) for j, chip in enumerate(chips)]
        for j, chip in enumerate(chips):
            copy(1 + j, (*chip, c), me).wait_recv()
            passed[j].start()
        copy(0, sibling, me).wait_recv()
        for j, chip in enumerate(chips):
            copy(4 + j, (*chip, 1 - c), me).wait_recv()
        for cp in first + passed:
            cp.wait_send()

    out = pl.pallas_call(
        body, name=name, out_shape=jax.ShapeDtypeStruct((N_DEV * m_per, n), x_shard.dtype),
        in_specs=[pl.BlockSpec(memory_space=pl.ANY)], out_specs=pl.BlockSpec(memory_space=pl.ANY),
        scratch_shapes=[pltpu.SemaphoreType.DMA((7,)), pltpu.SemaphoreType.DMA((7,))],
    )(x_shard)
    my_dev = 4 * lax.axis_index("x") + 2 * lax.axis_index("y") + lax.axis_index("c")
    return lax.dynamic_update_slice(out, x_shard, (my_dev * m_per, 0))


def _hbm_call(name, body, operands, out_shapes, n_sems):
    return pl.pallas_call(
        body, name=name, out_shape=list(out_shapes),
        in_specs=[pl.BlockSpec(memory_space=pl.ANY)] * len(operands),
        out_specs=[pl.BlockSpec(memory_space=pl.ANY)] * len(out_shapes),
        scratch_shapes=[pltpu.SemaphoreType.DMA((n_sems,)), pltpu.SemaphoreType.DMA((n_sems,))],
    )(*operands)


def _put_own(out, own, index):
    start = tuple(index) + (0,) * own.ndim
    return lax.dynamic_update_slice(out, own.reshape((1,) * len(index) + own.shape), start)


def _gather_weights(name, shards):
    n_w = len(shards)

    def body(*refs):
        ins, outs = refs[:n_w], refs[n_w:2 * n_w]
        send_sems, recv_sems = refs[2 * n_w], refs[2 * n_w + 1]
        x, y, c = lax.axis_index("x"), lax.axis_index("y"), lax.axis_index("c")
        my_chip = 2 * x + y
        chips = [(1 - x, y), (x, 1 - y), (1 - x, 1 - y)]

        def copy(w, k, src, chip, layer, to):
            return pltpu.make_async_remote_copy(
                src_ref=src, dst_ref=outs[w].at[chip, layer], send_sem=send_sems.at[6 * w + k],
                recv_sem=recv_sems.at[6 * w + k], device_id=to, device_id_type=MESH)

        started = []
        for w in range(n_w):
            for k, (cx, cy) in enumerate(chips):
                started.append(copy(w, k, ins[w].at[c], my_chip, c, (cx, cy, c)))
                started[-1].start()
        for w in range(n_w):
            for k, (cx, cy) in enumerate(chips):
                chip = 2 * cx + cy
                copy(w, k, ins[w].at[c], chip, c, (cx, cy, c)).wait_recv()
                started.append(copy(w, 3 + k, outs[w].at[chip, c], chip, c, (x, y, 1 - c)))
                started[-1].start()
        for w in range(n_w):
            for k, (cx, cy) in enumerate(chips):
                copy(w, 3 + k, ins[w].at[c], 2 * cx + cy, 1 - c, (x, y, 1 - c)).wait_recv()
        for cp in started:
            cp.wait_send()

    outs = _hbm_call(name, body, shards, [jax.ShapeDtypeStruct((N_CHIPS,) + s.shape, s.dtype) for s in shards], 6 * n_w)
    my_chip = 2 * lax.axis_index("x") + lax.axis_index("y")
    return [_put_own(o, s, (my_chip,)) for o, s in zip(outs, shards)]


def _swap_layers(name, grads):
    n_w = len(grads)

    def body(*refs):
        ins, outs = refs[:n_w], refs[n_w:2 * n_w]
        send_sems, recv_sems = refs[2 * n_w], refs[2 * n_w + 1]
        x, y, c = lax.axis_index("x"), lax.axis_index("y"), lax.axis_index("c")
        copies = [pltpu.make_async_remote_copy(src_ref=ins[w].at[1 - c], dst_ref=outs[w], send_sem=send_sems.at[w],
                                               recv_sem=recv_sems.at[w], device_id=(x, y, 1 - c), device_id_type=MESH)
                  for w in range(n_w)]
        for cp in copies:
            cp.start()
        for cp in copies:
            cp.wait()

    return _hbm_call(name, body, grads, [jax.ShapeDtypeStruct(g.shape[1:], g.dtype) for g in grads], n_w)


def _chip_exchange(name, parts):
    n_w = len(parts)

    def body(*refs):
        ins, outs = refs[:n_w], refs[n_w:2 * n_w]
        send_sems, recv_sems = refs[2 * n_w], refs[2 * n_w + 1]
        x, y, c = lax.axis_index("x"), lax.axis_index("y"), lax.axis_index("c")
        my_chip = 2 * x + y
        chips = [(1 - x, y), (x, 1 - y), (1 - x, 1 - y)]
        copies = [pltpu.make_async_remote_copy(
            src_ref=ins[w].at[2 * cx + cy], dst_ref=outs[w].at[my_chip], send_sem=send_sems.at[3 * w + k],
            recv_sem=recv_sems.at[3 * w + k], device_id=(cx, cy, c), device_id_type=MESH)
            for w in range(n_w) for k, (cx, cy) in enumerate(chips)]
        for cp in copies:
            cp.start()
        for cp in copies:
            cp.wait()

    outs = _hbm_call(name, body, parts, [jax.ShapeDtypeStruct(p.shape, p.dtype) for p in parts], 3 * n_w)
    my_chip = 2 * lax.axis_index("x") + lax.axis_index("y")
    return [_put_own(o, lax.dynamic_index_in_dim(p, my_chip, 0, keepdims=False), (my_chip,)) for o, p in zip(outs, parts)]


def _share_layers(name, reduced):
    n_w = len(reduced)

    def body(*refs):
        ins, outs = refs[:n_w], refs[n_w:2 * n_w]
        send_sems, recv_sems = refs[2 * n_w], refs[2 * n_w + 1]
        x, y, c = lax.axis_index("x"), lax.axis_index("y"), lax.axis_index("c")
        copies = [pltpu.make_async_remote_copy(src_ref=ins[w], dst_ref=outs[w].at[c], send_sem=send_sems.at[w],
                                               recv_sem=recv_sems.at[w], device_id=(x, y, 1 - c), device_id_type=MESH)
                  for w in range(n_w)]
        for cp in copies:
            cp.start()
        for cp in copies:
            cp.wait()

    outs = _hbm_call(name, body, reduced, [jax.ShapeDtypeStruct((2,) + r.shape, r.dtype) for r in reduced], n_w)
    return [_put_own(o, r, (lax.axis_index("c"),)) for o, r in zip(outs, reduced)]


def _pack(arrays, cols, row_multiple, dtype):
    flat = jnp.concatenate([a.reshape(-1).astype(dtype) for a in arrays])
    unit = cols * row_multiple
    total = -(-flat.shape[0] // unit) * unit
    return jnp.pad(flat, (0, total - flat.shape[0])).reshape(total // cols, cols)


def _unpack(buf, shapes):
    flat, out, off = buf.reshape(-1), [], 0
    for sh in shapes:
        n = math.prod(sh)
        out.append(flat[off:off + n].reshape(sh))
        off += n
    return out


def _discretize(lam_re, lam_im, log_dt, b_re, b_im):
    lam = lax.complex(jnp.minimum(lam_re, -EIG_CLIP), lam_im)
    dt = jnp.exp(log_dt)[:, None]
    lam_bar = jnp.exp(lam * dt)
    b_bar = ((lam_bar - 1.0) / lam)[..., None] * lax.complex(b_re, b_im)
    return jnp.real(lam_bar), jnp.imag(lam_bar), jnp.real(b_bar), jnp.imag(b_bar)


def _scan_tables(ar, ai):
    a = lax.complex(ar, ai)
    pw = [a]
    for _ in range(7):
        pw.append(pw[-1] * a)
    rows = jnp.arange(SUBLANES)[:, None]

    def build(p, reverse):
        tabs = []
        for k in (1, 2, 4):
            keep = (rows <= SUBLANES - 1 - k) if reverse else (rows >= k)
            tk = jnp.where(keep, p[k - 1][None, :], 0.0)
            tabs += [jnp.real(tk), jnp.imag(tk)]
        stack = jnp.stack(p[::-1] if reverse else p)
        tabs += [jnp.real(stack), jnp.imag(stack)]
        return jnp.stack(tabs).astype(F32)

    return build(pw, False), build([jnp.conj(p) for p in pw], True)


def _interleave_rows(a, t):
    s, w = a.shape
    return a.reshape(s // t, SUBLANES, t // SUBLANES, w).transpose(0, 2, 1, 3).reshape(s, w)


def _deinterleave_rows(a, t):
    s, w = a.shape
    return a.reshape(s // t, t // SUBLANES, SUBLANES, w).transpose(0, 2, 1, 3).reshape(s, w)


def _block_diag(per_group, groups_per_block):
    g, a, b = per_group.shape
    x = per_group.reshape(g // groups_per_block, groups_per_block, a, b)
    eye = jnp.eye(groups_per_block, dtype=per_group.dtype)
    out = x[:, :, :, None, :] * eye[None, :, None, :, None]
    return out.reshape(g // groups_per_block, groups_per_block * a, groups_per_block * b)


def _block_diag_extract(dense, groups_per_block, a, b):
    nkb = dense.shape[0]
    x = dense.reshape(nkb, groups_per_block, a, groups_per_block, b)
    idx = jnp.arange(groups_per_block)
    return x[:, idx, :, idx, :].transpose(1, 0, 2, 3).reshape(nkb * groups_per_block, a, b)


def _layer_fwd(tag, l, x, mod, p, wts):
    s, d = x.shape
    w_ssm, w_att = p["w_glu"].shape[0], wts["w_pb"].shape[2]
    heads = p["b_f"].shape[0]
    dh = w_att // heads
    cs = d // N_CHIPS
    fs = wts["w_ffn_down"].shape[2]
    tm = _pick(s, 1024)
    row = lambda v: v.reshape(1, -1)
    sv = {}

    h = _prenorm_fwd(f"prenorm_mix_{tag}", x, row(p["g_pre_mix"]), row(mod[1]), row(mod[0]))
    uqkv = _mm_plain(f"proj_main_{tag}", h, p["w_main"], "nn", BF16, tm=1024, tn=1024, tk=1024)
    fg = _mm_plain(f"proj_gate_{tag}", h, p["w_gates"], "nn", F32, tm=1024, tn=1024, tk=1024)
    f_t = fg[:, 2 * d:2 * d + heads].T

    t5 = min(S5_ROWS, s)
    u_il = _interleave_rows(uqkv[:, :w_ssm], t5)
    y_s5, ys_il, carries = _s5_fwd(f"s5_fwd_{tag}", u_il, p["b_blk"], p["c_blk"], p["a_f"], p["tab_f"],
                                   row(p["d_skip"]), p["w_glu"], row(p["b_glu"]))
    ys = _deinterleave_rows(ys_il, t5)

    assert dh * 2 == LANES and w_ssm % LANES == 0 and w_att % LANES == 0
    n_pairs = w_att // LANES
    blocks = (w_ssm // LANES, w_ssm // LANES + n_pairs, w_ssm // LANES + 2 * n_pairs)
    cum = _cum_fwd(f"cum_fwd_{tag}", f_t, p["b_f"].reshape(heads, 1))
    t = min(ATT_BLOCK, s)
    ck_cols, ck_rows = cum.reshape(heads, s, 1), cum.reshape(heads, s // t, 1, t)
    ya, lse = _attn_fwd(f"attn_fwd_{tag}", uqkv, *blocks, n_pairs, ck_rows)

    tile = pl.BlockSpec((tm, cs), lambda i, j, k: (i, j))
    slab = lambda rows: pl.BlockSpec((None, None, rows, cs), lambda i, j, k: (j, l, 0, 0))

    def merge(acc, extra_refs, out_refs):
        ya_ref, wpb_ref, ga_ref, gb_ref = extra_refs
        a_ref, b_ref, m_ref = out_refs
        bv = _dot(ya_ref[...], wpb_ref[...], NN)
        a_ref[...] = acc.astype(BF16)
        b_ref[...] = bv.astype(BF16)
        m_ref[...] = (_sigmoid(ga_ref[...]) * acc + _sigmoid(gb_ref[...]) * bv).astype(BF16)

    sd_bf = jax.ShapeDtypeStruct((s, d), BF16)
    pa, pb, merged = _mm_raw(
        f"merge_{tag}", ys, wts["w_pa"], "nn", (s // tm, N_CHIPS, 1), (tm, cs),
        pl.BlockSpec((tm, w_ssm), lambda i, j, k: (i, 0)), slab(w_ssm), [sd_bf] * 3, [tile] * 3, merge,
        extra=(ya, wts["w_pb"], fg, fg),
        extra_specs=[pl.BlockSpec((tm, w_att), lambda i, j, k: (i, 0)), slab(w_att), tile,
                     pl.BlockSpec((tm, cs), lambda i, j, k: (i, j + N_CHIPS))])

    tm2 = _pick(s, POSTNORM_ROWS)
    x1, y_mix = _mm_postnorm(
        f"out_proj_{tag}", merged, pl.BlockSpec((tm2, cs), lambda i, j, k: (i, k)), wts["w_o"],
        pl.BlockSpec((None, None, cs, d), lambda i, j, k: (k, l, 0, 0)), N_CHIPS, x, row(mod[2]), row(p["g_post_mix"]))

    h2 = _prenorm_fwd(f"prenorm_ffn_{tag}", x1, row(p["g_pre_ffn"]), row(mod[4]), row(mod[3]))
    a4, b4, hid4 = _ffn_up(f"ffn_up_{tag}", h2, wts["w_ffn_gate"], wts["w_ffn_up"], l)
    x2, y_ffn = _mm_postnorm(
        f"ffn_down_{tag}", hid4, pl.BlockSpec((None, tm2, fs), lambda i, j, k: (k, i, 0)), wts["w_ffn_down"],
        pl.BlockSpec((None, None, fs, d), lambda i, j, k: (k, l, 0, 0)), N_CHIPS, x1, row(mod[5]), row(p["g_post_ffn"]))

    sv.update(x=x, h=h, uqkv=uqkv, u_il=u_il, fg=fg, f_t=f_t, y_s5=y_s5, ys=ys, carries=carries, blocks=blocks,
              ck_cols=ck_cols, lse_rows=lse.reshape(heads, s // t, 1, t), ya=ya, pa=pa, pb=pb, merged=merged, x1=x1,
              y_mix=y_mix, h2=h2, a4=a4, b4=b4, hid4=hid4, y_ffn=y_ffn)
    return x2, sv


def _mm_postnorm(name, a, a_spec, w, w_spec, nk, x, gate, g):
    s, d = x.shape
    tm = _pick(s, POSTNORM_ROWS)
    rowspec = pl.BlockSpec((tm, d), lambda i, j, k: (i, 0))
    vec = pl.BlockSpec((1, d), lambda i, j, k: (0, 0))

    def epilogue(acc, extra_refs, out_refs):
        x_ref, gate_ref, g_ref = extra_refs
        r = lax.rsqrt(jnp.mean(acc * acc, axis=-1, keepdims=True) + RMS_EPS)
        out_refs[0][...] = x_ref[...] + gate_ref[...] * (acc * r * g_ref[...])
        out_refs[1][...] = acc

    sd = jax.ShapeDtypeStruct((s, d), F32)
    return _mm_raw(name, a, w, "nn", (s // tm, 1, nk), (tm, d), a_spec, w_spec, [sd, sd], [rowspec, rowspec], epilogue,
                   extra=(x, gate, g), extra_specs=[rowspec, vec, vec])


def _layer_bwd(tag, l, dx2, mod, p, wts, sv):
    s, d = dx2.shape
    w_ssm, w_att = p["w_glu"].shape[0], wts["w_pb"].shape[2]
    heads = p["b_f"].shape[0]
    cs = d // N_CHIPS
    fs = wts["w_ffn_down"].shape[2]
    tm, tk, td = _pick(s, 1024), _pick(s, 1024), d
    row = lambda v: v.reshape(1, -1)
    gr = {}

    def dw_slabs(name, act, act_spec, rows, dy, dy_spec, cols, grid_mn, out_index):
        return _mm_raw(name, act, dy, "tn", grid_mn + (s // tk,), (rows, cols), act_spec, dy_spec,
                       [jax.ShapeDtypeStruct((N_CHIPS,) + out_index[1], BF16)],
                       [pl.BlockSpec((None, rows, cols), out_index[0])], _store(BF16))[0]

    dy_ffn, sums = _postnorm_bwd(f"postnorm_bwd_ffn_{tag}", dx2, sv["y_ffn"], row(p["g_post_ffn"]), row(mod[5]))
    d_gate_f, gr["g_post_ffn"] = sums[0], sums[1]
    gr["w_ffn_down"] = dw_slabs(f"dw_down_{tag}", sv["hid4"], pl.BlockSpec((None, tk, fs), lambda i, j, k: (i, k, 0)), fs,
                                dy_ffn, pl.BlockSpec((tk, d), lambda i, j, k: (k, 0)), d, (N_CHIPS, 1),
                                (lambda i, j, k: (i, 0, 0), (fs, d)))

    def swiglu_bwd(acc, extra_refs, out_refs):
        av, bv = extra_refs[0][...].astype(F32), extra_refs[1][...].astype(F32)
        sg = _sigmoid(av)
        out_refs[0][...] = (acc * bv * (sg * (1.0 + av * (1.0 - sg)))).astype(BF16)
        out_refs[1][...] = (acc * (av * sg)).astype(BF16)

    blk4 = pl.BlockSpec((None, tm, fs), lambda i, j, k: (j, i, 0))
    sh4 = jax.ShapeDtypeStruct((N_CHIPS, s, fs), BF16)
    da4, db4 = _mm_raw(f"ffn_down_bwd_{tag}", dy_ffn, wts["w_ffn_down"], "nt", (s // tm, N_CHIPS, 1), (tm, fs),
                       pl.BlockSpec((tm, d), lambda i, j, k: (i, 0)),
                       pl.BlockSpec((None, None, fs, d), lambda i, j, k: (j, l, 0, 0)),
                       [sh4, sh4], [blk4, blk4], swiglu_bwd, extra=(sv["a4"], sv["b4"]), extra_specs=[blk4, blk4])
    for n, act4 in (("w_ffn_gate", da4), ("w_ffn_up", db4)):
        gr[n] = dw_slabs(f"d{n}_{tag}", sv["h2"], pl.BlockSpec((tk, td), lambda i, j, k: (k, i)), td,
                         act4, pl.BlockSpec((None, tk, fs), lambda i, j, k: (j, k, 0)), fs, (d // td, N_CHIPS),
                         (lambda i, j, k: (j, i, 0), (d, fs)))
    pairs = [(act4, (None, tm, fs), lambda i, kk: (kk, i, 0), wts[n], (None, None, td, fs), lambda j, kk: (kk, l, j, 0),
              N_CHIPS) for n, act4 in (("w_ffn_gate", da4), ("w_ffn_up", db4))]
    dh2 = _mm_sum(f"dh_ffn_{tag}", s, d, tm, td, pairs, F32)
    dx1, sums = _prenorm_bwd(f"prenorm_bwd_ffn_{tag}", dh2, sv["x1"], row(p["g_pre_ffn"]), row(mod[4]), dx2)
    d_scale_f, d_shift_f, gr["g_pre_ffn"] = sums[0], sums[1], sums[2]

    dy_mix, sums = _postnorm_bwd(f"postnorm_bwd_mix_{tag}", dx1, sv["y_mix"], row(p["g_post_mix"]), row(mod[2]))
    d_gate_m, gr["g_post_mix"] = sums[0], sums[1]
    gr["w_o"] = dw_slabs(f"dw_o_{tag}", sv["merged"], pl.BlockSpec((tk, cs), lambda i, j, k: (k, i)), cs,
                         dy_mix, pl.BlockSpec((tk, d), lambda i, j, k: (k, 0)), d, (N_CHIPS, 1),
                         (lambda i, j, k: (i, 0, 0), (cs, d)))

    tile = pl.BlockSpec((tm, cs), lambda i, j, k: (i, j))

    def merge_bwd(acc, extra_refs, out_refs):
        a_ref, b_ref, ga_ref, gb_ref = extra_refs
        sa, sb = _sigmoid(ga_ref[...]), _sigmoid(gb_ref[...])
        out_refs[0][...] = (acc * sa).astype(BF16)
        out_refs[1][...] = (acc * sb).astype(BF16)
        out_refs[2][...] = (acc * a_ref[...].astype(F32) * sa * (1.0 - sa)).astype(BF16)
        out_refs[3][...] = (acc * b_ref[...].astype(F32) * sb * (1.0 - sb)).astype(BF16)

    sd_bf = jax.ShapeDtypeStruct((s, d), BF16)
    d_pa, d_pb, d_ga, d_gb = _mm_raw(
        f"out_proj_bwd_{tag}", dy_mix, wts["w_o"], "nt", (s // tm, N_CHIPS, 1), (tm, cs),
        pl.BlockSpec((tm, d), lambda i, j, k: (i, 0)), pl.BlockSpec((None, None, cs, d), lambda i, j, k: (j, l, 0, 0)),
        [sd_bf] * 4, [tile] * 4, merge_bwd, extra=(sv["pa"], sv["pb"], sv["fg"], sv["fg"]),
        extra_specs=[tile, tile, tile, pl.BlockSpec((tm, cs), lambda i, j, k: (i, j + N_CHIPS))])
    d_branch = {}
    for n, act, width, d_p in (("w_pa", sv["ys"], w_ssm, d_pa), ("w_pb", sv["ya"], w_att, d_pb)):
        gr[n] = dw_slabs(f"d{n}_{tag}", act, pl.BlockSpec((tk, width), lambda i, j, k: (k, 0)), width,
                         d_p, pl.BlockSpec((tk, cs), lambda i, j, k: (k, j)), cs, (1, N_CHIPS),
                         (lambda i, j, k: (j, 0, 0), (width, cs)))
        d_branch[n] = _mm_raw(
            f"d_in_{n}_{tag}", d_p, wts[n], "nt", (s // tm, 1, N_CHIPS), (tm, width),
            pl.BlockSpec((tm, cs), lambda i, j, k: (i, k)), pl.BlockSpec((None, None, width, cs), lambda i, j, k: (k, l, 0, 0)),
            [jax.ShapeDtypeStruct((s, width), BF16)], [pl.BlockSpec((tm, width), lambda i, j, k: (i, 0))], _store(BF16))[0]
    d_ys, d_ya = d_branch["w_pa"], d_branch["w_pb"]

    dq, dk, dv, dcq, dck = _attn_bwd(f"attn_bwd_{tag}", sv["uqkv"], *sv["blocks"], w_att // LANES, sv["ya"], d_ya,
                                     sv["lse_rows"], sv["ck_cols"])
    d_f_t, d_bf = _cum_bwd(f"cum_bwd_{tag}", dcq.reshape(heads, s), dck.reshape(heads, s), sv["f_t"],
                           p["b_f"].reshape(heads, 1))
    gr["b_f"] = d_bf[:, 0]

    t5 = min(S5_ROWS, s)
    du_il, d_bblk, d_cblk, d_abar, d_wglu, vec = _s5_bwd(
        f"s5_bwd_{tag}", sv["u_il"], _interleave_rows(d_ys, t5), sv["y_s5"], sv["carries"], p["b_blk"], p["c_blk"],
        p["a_f"], p["a_r"], p["tab_f"], p["tab_r"], row(p["d_skip"]), p["w_glu"], row(p["b_glu"]))
    du = _deinterleave_rows(du_il, t5)
    gr["w_glu"] = d_wglu.astype(BF16).reshape(N_CHIPS, w_ssm // N_CHIPS, w_ssm)
    gr["b_glu"], gr["d_skip"] = vec[0], vec[1]
    gr["b_blk"], gr["c_blk"], gr["a_bar"] = d_bblk, d_cblk, d_abar

    d_f = jnp.pad(d_f_t.T, ((0, 0), (0, F_PAD - heads))).astype(BF16)
    assert w_ssm % w_att == 0 and (2 * d) % F_PAD == 0
    first = w_ssm // w_att
    main_pieces = [(du, w_ssm, 0), (dq, w_att, first), (dk, w_att, first + 1), (dv, w_att, first + 2)]
    dw = [_mm_plain(f"dw_in{n}_{tag}", sv["h"], piece, "tn", BF16, tm=1024, tn=1024, tk=1024)
          for n, piece in enumerate([du, dq, dk, dv, d_f, d_ga, d_gb])]
    w_in_grad = jnp.concatenate(dw[:4] + [dw[4][:, :heads], dw[5], dw[6]], axis=1)
    gr["w_in"] = w_in_grad.reshape(d, N_CHIPS, w_in_grad.shape[1] // N_CHIPS).transpose(1, 0, 2)
    tmx, tkx = _pick(s, 512), _pick(d, 512)
    pairs = [(piece, (tmx, width), lambda i, kk: (i, 0), p["w_main"], (d, width), lambda j, kk, blk=blk: (j, blk), 1)
             for piece, width, blk in main_pieces]
    steps = d // tkx
    pairs += [(piece, (tmx, tkx), lambda i, kk: (i, kk), p["w_gates"], (d, tkx), lambda j, kk, off=off: (j, off + kk), steps)
              for piece, off in ((d_ga, 0), (d_gb, steps))]
    pairs.append((d_f, (tmx, F_PAD), lambda i, kk: (i, 0), p["w_gates"], (d, F_PAD), lambda j, kk: (j, 2 * d // F_PAD), 1))
    dh1 = _mm_sum(f"dh_mix_{tag}", s, d, tmx, d, pairs, F32)
    dx0, sums = _prenorm_bwd(f"prenorm_bwd_mix_{tag}", dh1, sv["x"], row(p["g_pre_mix"]), row(mod[1]), dx1)
    d_scale_m, d_shift_m, gr["g_pre_mix"] = sums[0], sums[1], sums[2]

    d_mod = jnp.stack([d_shift_m, d_scale_m, d_gate_m, d_shift_f, d_scale_f, d_gate_f])
    return dx0, d_mod, gr


BIG = ("w_in", "w_glu", "w_pa", "w_pb", "w_o", "w_ffn_gate", "w_ffn_up", "w_ffn_down")
SMALL = ("b_ada", "g_pre_mix", "g_post_mix", "g_pre_ffn", "g_post_ffn", "lam_re", "lam_im", "log_dt", "b_re", "b_im",
         "c_re", "c_im", "d_skip", "b_glu", "b_f")
WEIGHTS = ("w_ada", "b_ada", "g_pre_mix", "g_post_mix", "g_pre_ffn", "g_post_ffn", "w_in", "lam_re", "lam_im", "log_dt",
           "b_re", "b_im", "c_re", "c_im", "d_skip", "w_glu", "b_glu", "b_f", "w_pa", "w_pb", "w_o", "w_ffn_gate",
           "w_ffn_up", "w_ffn_down")


def _prepare_layer(wts, small, l, seq):
    w_in = jnp.concatenate([wts["w_in"][j, l] for j in range(N_CHIPS)], axis=1)
    d = w_in.shape[0]
    heads = small["b_f"].shape[1]
    n_groups, n_state, group_ch = small["b_re"].shape[1:]
    w_ssm = n_groups * group_ch
    w_att = wts["w_pb"].shape[2]
    n_main = w_ssm + 3 * w_att
    gpb = LANES // group_ch
    p = {}
    p["w_main"] = w_in[:, :n_main]
    p["w_gates"] = jnp.concatenate(
        [w_in[:, n_main + heads:], w_in[:, n_main:n_main + heads], jnp.zeros((d, F_PAD - heads), BF16)], axis=1)
    p["w_glu"] = wts["w_glu"][:, l].reshape(w_ssm, w_ssm)
    for n in ("g_pre_mix", "g_post_mix", "g_pre_ffn", "g_post_ffn", "d_skip", "b_glu", "b_f"):
        p[n] = small[n][l]
    ar, ai, br, bi = _discretize(small["lam_re"][l], small["lam_im"][l], small["log_dt"][l], small["b_re"][l], small["b_im"][l])
    n_steps = min(S5_ROWS, seq) // SUBLANES
    powers = jnp.cumprod(jnp.broadcast_to(lax.complex(ar, ai).reshape(1, -1), (n_steps, ar.size)), axis=0)
    p["a_f"] = jnp.concatenate([jnp.real(powers), jnp.imag(powers)], axis=1)
    p["a_r"] = jnp.concatenate([jnp.real(powers[::-1]), -jnp.imag(powers[::-1])], axis=1)
    p["tab_f"], p["tab_r"] = _scan_tables(jnp.real(powers[-1]), jnp.imag(powers[-1]))
    bre = _block_diag(br.transpose(0, 2, 1), gpb)
    bim = _block_diag(bi.transpose(0, 2, 1), gpb)
    p["b_blk"] = jnp.concatenate([bre, bim], axis=2).astype(BF16)
    cre = _block_diag(small["c_re"][l].transpose(0, 2, 1), gpb)
    cim = _block_diag(small["c_im"][l].transpose(0, 2, 1), gpb)
    p["c_blk"] = jnp.concatenate([cre, -cim], axis=1).astype(BF16)
    return p


def _compact_partials(gr, n_state, group_ch):
    gpb = LANES // group_ch
    half = gpb * n_state
    out = dict(gr)
    out["bbar_re"] = _block_diag_extract(gr["b_blk"][:, :, :half], gpb, group_ch, n_state).transpose(0, 2, 1)
    out["bbar_im"] = _block_diag_extract(gr["b_blk"][:, :, half:], gpb, group_ch, n_state).transpose(0, 2, 1)
    out["c_re"] = _block_diag_extract(gr["c_blk"][:, :half, :], gpb, n_state, group_ch).transpose(0, 2, 1)
    out["c_im"] = -_block_diag_extract(gr["c_blk"][:, half:, :], gpb, n_state, group_ch).transpose(0, 2, 1)
    return out


def _small_grads_from_partials(gr, small, l):
    n_groups, n_state, _ = small["b_re"].shape[1:]
    ns2 = n_groups * n_state
    d_abar = jnp.sum(gr["a_bar"], axis=0)
    dar, dai = d_abar[:ns2].reshape(n_groups, n_state), d_abar[ns2:].reshape(n_groups, n_state)
    args = (small["lam_re"][l], small["lam_im"][l], small["log_dt"][l], small["b_re"][l], small["b_im"][l])
    _, vjp = jax.vjp(_discretize, *args)
    d_lam_re, d_lam_im, d_log_dt, d_b_re, d_b_im = vjp((dar, dai, gr["bbar_re"], gr["bbar_im"]))
    return dict(lam_re=d_lam_re, lam_im=d_lam_im, log_dt=d_log_dt, b_re=d_b_re, b_im=d_b_im,
                c_re=gr["c_re"], c_im=gr["c_im"])


def _fwd_bwd(xs, target, mods, layers, wts):
    depth = len(layers)
    saved = []
    act = xs
    for l in range(depth):
        act, sv = _layer_fwd(str(l), l, act, mods[l], layers[l], wts)
        saved.append(sv)
    dx, loss_blk = _loss_grad("loss", act, target)
    grads, d_mods = [None] * depth, [None] * depth
    for l in reversed(range(depth)):
        dx, d_mods[l], grads[l] = _layer_bwd(str(l), l, dx, mods[l], layers[l], wts, saved[l])
    stacked = {n: jnp.stack([grads[l][n] for l in range(depth)]) for n in BIG}
    return loss_blk, dx, d_mods, grads, stacked


def kernel(x, c, w_ada, b_ada, g_pre_mix, g_post_mix, g_pre_ffn, g_post_ffn, w_in, lam_re, lam_im, log_dt, b_re, b_im, c_re, c_im, d_skip, w_glu, b_glu, b_f, w_pa, w_pb, w_o, w_ffn_gate, w_ffn_up, w_ffn_down, loss_target, m_w_ada, m_b_ada, m_g_pre_mix, m_g_post_mix, m_g_pre_ffn, m_g_post_ffn, m_w_in, m_lam_re, m_lam_im, m_log_dt, m_b_re, m_b_im, m_c_re, m_c_im, m_d_skip, m_w_glu, m_b_glu, m_b_f, m_w_pa, m_w_pb, m_w_o, m_w_ffn_gate, m_w_ffn_up, m_w_ffn_down, v_w_ada, v_b_ada, v_g_pre_mix, v_g_post_mix, v_g_pre_ffn, v_g_post_ffn, v_w_in, v_lam_re, v_lam_im, v_log_dt, v_b_re, v_b_im, v_c_re, v_c_im, v_d_skip, v_w_glu, v_b_glu, v_b_f, v_w_pa, v_w_pb, v_w_o, v_w_ffn_gate, v_w_ffn_up, v_w_ffn_down):
    local = dict(locals())
    weights = {n: local[n] for n in WEIGHTS}
    moments_m = {n: local["m_" + n] for n in WEIGHTS}
    moments_v = {n: local["v_" + n] for n in WEIGHTS}
    depth, d = g_pre_mix.shape
    n_mod = w_ada.shape[2] * N_CHIPS // d
    mx, my, mc = lax.axis_index("x"), lax.axis_index("y"), lax.axis_index("c")
    my_chip = 2 * mx + my
    my_dev = 4 * mx + 2 * my + mc
    xs = x[0]

    assert depth == 2, "each core of a chip moves and reduces one layer"
    wts = dict(zip(BIG, _gather_weights("gather_weights", [weights[n].astype(BF16) for n in BIG])))
    small = {n: weights[n] for n in SMALL}
    layers = [_prepare_layer(wts, small, l, xs.shape[0]) for l in range(depth)]

    c_pad = jnp.pad(c, ((0, SUBLANES - 1), (0, 0)))
    c_all = _all_gather("gather_cond", c_pad).reshape(N_DEV, SUBLANES, d)[:, 0, :]
    silu = lambda v: v * _sigmoid(v)
    n_cols = w_ada.shape[2]
    mod_shard = []
    for l in range(depth):
        bias = lax.dynamic_slice_in_dim(b_ada[l], my_chip * n_cols, n_cols)
        mod_shard.append(_mm_plain(f"ada_{l}", c_all, w_ada[l], "nn", F32, add=jnp.broadcast_to(bias, (N_DEV, n_cols)),
                                   a_fn=silu, tm=N_DEV, tn=512, tk=1024))
    mod_block = jnp.concatenate(mod_shard, axis=1)
    mod_all = _all_gather("gather_mod", mod_block).reshape(N_DEV, N_DEV, depth, n_cols)
    mod_rows = lax.dynamic_index_in_dim(mod_all[0::2], my_dev, axis=1, keepdims=False)
    mods = [mod_rows[:, l, :].reshape(n_mod, d) for l in range(depth)]

    loss_blk, dx, d_mods, grads, stacked = _fwd_bwd(xs, loss_target[0], mods, layers, wts)
    loss = lax.psum(loss_blk[0, 0], ("x", "y", "c"))
    grad_x = dx[None]

    core = mc.astype(jnp.int32).reshape(1)
    partials = [stacked[n] for n in BIG]
    from_sibling = _swap_layers("grads_swap_cores", partials)
    chip_parts = [_add_layer(f"grads_add_{n}", g, r, core) for n, g, r in zip(BIG, partials, from_sibling)]
    from_chips = _chip_exchange("grads_exchange_chips", chip_parts)
    reduced = [_sum_blocks(f"grads_sum_{n}", r, F32) for n, r in zip(BIG, from_chips)]
    big_grads = dict(zip(BIG, _share_layers("grads_share_cores", reduced)))

    partial_names = ("g_pre_mix", "g_post_mix", "g_pre_ffn", "g_post_ffn", "d_skip", "b_glu", "b_f", "a_bar",
                     "bbar_re", "bbar_im", "c_re", "c_im")
    n_state, group_ch = b_re.shape[2:]
    contrib = list(d_mods)
    for l in range(depth):
        compact = _compact_partials(grads[l], n_state, group_ch)
        contrib += [compact[n] for n in partial_names]
    contrib_shapes = [a.shape for a in contrib]
    block = _pack(contrib, LANES, BF16_ROWS, F32)
    rows = block.shape[0]
    all_blocks = _all_gather("gather_small_grads", block).reshape(N_DEV, rows, LANES)
    summed = _unpack(_sum_blocks("sum_small_grads", all_blocks, F32), contrib_shapes)
    per_layer = len(partial_names)
    small_grads = {n: [] for n in SMALL}
    d_mod_all = []
    for l in range(depth):
        small_grads["b_ada"].append(summed[l].reshape(-1))
        gl = dict(zip(partial_names, summed[depth + l * per_layer:depth + (l + 1) * per_layer]))
        for n in ("g_pre_mix", "g_post_mix", "g_pre_ffn", "g_post_ffn", "d_skip", "b_glu", "b_f"):
            small_grads[n].append(gl[n])
        for n, gval in _small_grads_from_partials(gl, small, l).items():
            small_grads[n].append(gval)
        d_mod_all.append(all_blocks.reshape(N_DEV, rows * LANES)[:, l * n_mod * d:(l + 1) * n_mod * d])
    small_grads = {n: jnp.stack(v) for n, v in small_grads.items()}

    g_w_ada = []
    for l in range(depth):
        cols = lax.dynamic_slice_in_dim(d_mod_all[l], my_chip * n_cols, n_cols, axis=1)
        g_w_ada.append(_mm_plain(f"dw_ada_{l}", c_all, cols, "tn", F32, a_fn=silu, tm=512, tn=512, tk=N_DEV))
    all_grads = dict(big_grads)
    all_grads.update(small_grads)
    all_grads["w_ada"] = jnp.stack(g_w_ada)

    delta, new_m, new_v = {}, {}, {}
    for n in ("w_ada",) + BIG:
        shape = weights[n].shape
        two_d = lambda a: a.reshape(-1, shape[-1])
        dl, nm, nv = _adamw(f"adamw_{n}", two_d(weights[n]), two_d(all_grads[n]), two_d(moments_m[n]), two_d(moments_v[n]))
        delta[n], new_m[n], new_v[n] = dl.reshape(shape), nm.reshape(shape), nv.reshape(shape)
    small_shapes = [weights[n].shape for n in SMALL]
    packed = [_pack([src[n] for n in SMALL], LANES, SUBLANES, F32) for src in (weights, all_grads, moments_m, moments_v)]
    outs = _adamw("adamw_small", *packed)
    for dst, buf in zip((delta, new_m, new_v), outs):
        dst.update(dict(zip(SMALL, _unpack(buf, small_shapes))))

    return (loss, grad_x, *[all_grads[n] for n in WEIGHTS], *[delta[n] for n in WEIGHTS],
            *[new_m[n] for n in WEIGHTS], *[new_v[n] for n in WEIGHTS])
```

```python
import functools
import math

import jax
import jax.numpy as jnp
from jax import lax
from jax.experimental import pallas as pl
from jax.experimental.pallas import tpu as pltpu

F32 = jnp.float32
BF16 = jnp.bfloat16
MESH = pl.DeviceIdType.MESH

RMS_EPS = 1e-6
EIG_CLIP = 1e-4
ADAM_LR, ADAM_B1, ADAM_B2, ADAM_EPS, ADAM_WD, ADAM_STEP = 0.001, 0.9, 0.999, 1e-08, 0.01, 10

LANES = 128
SUBLANES = 8
VMEM_LIMIT = 56 * 1024 * 1024
S5_ROWS = 256
S5_CHUNK = 1024
S5_UNROLL = 4
ATT_BLOCK = 512
F_PAD = 256
POSTNORM_ROWS = 512
N_CHIPS = 4
N_DEV = 8

NN = (((1,), (0,)), ((), ()))
NT = (((1,), (1,)), ((), ()))
TN = (((0,), (0,)), ((), ()))
_DN = {"nn": NN, "nt": NT, "tn": TN}


def _cparams(**kw):
    return pltpu.CompilerParams(vmem_limit_bytes=VMEM_LIMIT, **kw)


def _pick(dim, target):
    best, t = None, LANES
    while t <= min(dim, target):
        if dim % t == 0:
            best = t
        t += LANES
    return best or dim


def _sigmoid(x):
    return 1.0 / (1.0 + jnp.exp(-x))


def _dot(a, b, dn):
    return lax.dot_general(a, b, dn, preferred_element_type=F32)


def _mm_raw(name, a, b, mode, grid, acc_shape, a_spec, b_spec, out_shapes, out_specs, epilogue,
            extra=(), extra_specs=(), a_fn=None):
    nk = grid[2]
    n_extra, n_out = len(extra), len(out_shapes)

    def body(*refs):
        a_ref, b_ref = refs[0], refs[1]
        extra_refs = refs[2:2 + n_extra]
        out_refs = refs[2 + n_extra:2 + n_extra + n_out]
        acc = refs[-1]
        k = pl.program_id(2)

        @pl.when(k == 0)
        def _():
            acc[...] = jnp.zeros_like(acc)

        av = a_ref[...]
        if a_fn is not None:
            av = a_fn(av.astype(F32))
        acc[...] += _dot(av.astype(BF16), b_ref[...].astype(BF16), _DN[mode])

        @pl.when(k == nk - 1)
        def _():
            epilogue(acc[...], extra_refs, out_refs)

    return pl.pallas_call(
        body, name=name, grid=grid,
        in_specs=[a_spec, b_spec, *extra_specs],
        out_specs=list(out_specs), out_shape=list(out_shapes),
        scratch_shapes=[pltpu.VMEM(acc_shape, F32)],
        compiler_params=_cparams(),
    )(a, b, *extra)


def _mm(name, a, b, mode, out_shapes, out_specs, epilogue, extra=(), extra_specs=(),
        tm=512, tn=512, tk=512, a_fn=None):
    if mode == "nn":
        (m, kd), (_, n) = a.shape, b.shape
    elif mode == "nt":
        (m, kd), (n, _) = a.shape, b.shape
    else:
        (kd, m), (_, n) = a.shape, b.shape
    tm, tn, tk = _pick(m, tm), _pick(n, tn), _pick(kd, tk)
    if mode == "tn":
        a_spec = pl.BlockSpec((tk, tm), lambda i, j, k: (k, i))
    else:
        a_spec = pl.BlockSpec((tm, tk), lambda i, j, k: (i, k))
    if mode == "nt":
        b_spec = pl.BlockSpec((tn, tk), lambda i, j, k: (j, k))
    else:
        b_spec = pl.BlockSpec((tk, tn), lambda i, j, k: (k, j))
    res = _mm_raw(name, a, b, mode, (m // tm, n // tn, kd // tk), (tm, tn), a_spec, b_spec, out_shapes, out_specs,
                  epilogue, extra=extra, extra_specs=extra_specs, a_fn=a_fn)
    return res, (tm, tn, tk)


def _store(dtype):
    def epilogue(acc, extra_refs, out_refs):
        out_refs[0][...] = acc.astype(dtype)
    return epilogue


def _mm_sum(name, m, n, tm, tn, pairs, out_dtype):
    offs, total = [], 0
    for pr in pairs:
        offs.append(total)
        total += pr[6]
    n_p = len(pairs)

    def body(*refs):
        o_ref, acc = refs[2 * n_p], refs[2 * n_p + 1]
        k = pl.program_id(2)

        @pl.when(k == 0)
        def _():
            acc[...] = jnp.zeros_like(acc)

        for p_ in range(n_p):
            @pl.when((k >= offs[p_]) & (k < offs[p_] + pairs[p_][6]))
            def _(p_=p_):
                acc[...] += _dot(refs[2 * p_][...].astype(BF16), refs[2 * p_ + 1][...].astype(BF16), NT)

        @pl.when(k == total - 1)
        def _():
            o_ref[...] = acc[...].astype(out_dtype)

    in_specs, operands = [], []
    for (a, a_block, a_index, b, b_block, b_index, steps), off in zip(pairs, offs):
        local = lambda k, off=off, steps=steps: jnp.clip(k - off, 0, steps - 1)
        in_specs.append(pl.BlockSpec(a_block, lambda i, j, k, f=a_index, local=local: f(i, local(k))))
        in_specs.append(pl.BlockSpec(b_block, lambda i, j, k, f=b_index, local=local: f(j, local(k))))
        operands += [a, b]
    return pl.pallas_call(
        body, name=name, grid=(m // tm, n // tn, total), in_specs=in_specs,
        out_specs=pl.BlockSpec((tm, tn), lambda i, j, k: (i, j)), out_shape=jax.ShapeDtypeStruct((m, n), out_dtype),
        scratch_shapes=[pltpu.VMEM((tm, tn), F32)], compiler_params=_cparams(),
    )(*operands)


class _SideJob:
    def __init__(self, arrays, out_shapes, aliases, n_sems, copies):
        self.arrays, self.out_shapes, self.aliases, self.n_sems, self.copies = arrays, out_shapes, aliases, n_sems, copies


def _hosted_call(body, side, name, grid, in_specs, out_specs, out_shape, scratch_shapes, operands):
    if side is None:
        outs = pl.pallas_call(body, name=name, grid=grid, in_specs=in_specs, out_specs=out_specs, out_shape=out_shape,
                              scratch_shapes=scratch_shapes, compiler_params=_cparams())(*operands)
        return outs, []
    n_in, n_out, ns_in, ns_out = len(in_specs), len(out_specs), len(side.arrays), len(side.out_shapes)

    def wrapped(*refs):
        main_in, side_in = refs[:n_in], refs[n_in:n_in + ns_in]
        rest = refs[n_in + ns_in:]
        main_out, side_out, rest = rest[:n_out], rest[n_out:n_out + ns_out], rest[n_out + ns_out:]
        scratch, send_sems, recv_sems = rest[:-2], rest[-2], rest[-1]
        first, last = None, None
        for axis, extent in enumerate(grid):
            at_start, at_end = pl.program_id(axis) == 0, pl.program_id(axis) == extent - 1
            first = at_start if first is None else first & at_start
            last = at_end if last is None else last & at_end

        @pl.when(first)
        def _():
            for cp in side.copies(side_in, side_out, send_sems, recv_sems):
                cp.start()

        body(*main_in, *main_out, *scratch)

        @pl.when(last)
        def _():
            for cp in side.copies(side_in, side_out, send_sems, recv_sems):
                cp.wait()

    hbm = pl.BlockSpec(memory_space=pl.ANY)
    outs = pl.pallas_call(
        wrapped, name=name, grid=grid, in_specs=list(in_specs) + [hbm] * ns_in,
        out_specs=list(out_specs) + [hbm] * ns_out, out_shape=list(out_shape) + list(side.out_shapes),
        scratch_shapes=list(scratch_shapes) + [pltpu.SemaphoreType.DMA((side.n_sems,))] * 2,
        input_output_aliases={n_in + i: n_out + o for i, o in side.aliases.items()},
        compiler_params=_cparams(),
    )(*operands, *side.arrays)
    return outs[:n_out], outs[n_out:]


def _ffn_up(name, h, wg, wu, side=None):
    s, d = h.shape
    nc, fs = wg.shape[0], wg.shape[2]
    tm, tk = _pick(s, 1024), _pick(d, 1024)
    nk = d // tk

    def body(h_ref, wg_ref, wu_ref, a_ref, b_ref, hid_ref, acc_g, acc_u):
        k = pl.program_id(2)

        @pl.when(k == 0)
        def _():
            acc_g[...] = jnp.zeros_like(acc_g)
            acc_u[...] = jnp.zeros_like(acc_u)

        hv = h_ref[...]
        acc_g[...] += _dot(hv, wg_ref[...], NN)
        acc_u[...] += _dot(hv, wu_ref[...], NN)

        @pl.when(k == nk - 1)
        def _():
            av, bv = acc_g[...], acc_u[...]
            a_ref[...] = av.astype(BF16)
            b_ref[...] = bv.astype(BF16)
            hid_ref[...] = (av * _sigmoid(av) * bv).astype(BF16)

    w_spec = pl.BlockSpec((None, tk, fs), lambda i, j, k: (j, k, 0))
    o_spec = pl.BlockSpec((None, tm, fs), lambda i, j, k: (j, i, 0))
    sh = jax.ShapeDtypeStruct((nc, s, fs), BF16)
    return _hosted_call(
        body, side, name, (s // tm, nc, nk), [pl.BlockSpec((tm, tk), lambda i, j, k: (i, k)), w_spec, w_spec],
        [o_spec] * 3, [sh] * 3, [pltpu.VMEM((tm, fs), F32), pltpu.VMEM((tm, fs), F32)], (h, wg, wu))


def _mm_plain(name, a, b, mode, out_dtype, add=None, a_fn=None, tm=512, tn=512, tk=512):
    if mode == "nn":
        m, n = a.shape[0], b.shape[1]
    elif mode == "nt":
        m, n = a.shape[0], b.shape[0]
    else:
        m, n = a.shape[1], b.shape[1]
    tm_, tn_ = _pick(m, tm), _pick(n, tn)
    spec = pl.BlockSpec((tm_, tn_), lambda i, j, k: (i, j))

    def epilogue(acc, extra_refs, out_refs):
        if add is not None:
            acc = acc + extra_refs[0][...]
        out_refs[0][...] = acc.astype(out_dtype)

    extra = () if add is None else (add,)
    (out,), _ = _mm(name, a, b, mode, [jax.ShapeDtypeStruct((m, n), out_dtype)], [spec], epilogue,
                    extra=extra, extra_specs=[spec] * len(extra), tm=tm, tn=tn, tk=tk, a_fn=a_fn)
    return out


def _row_tile(s, d):
    return _pick(s, max(SUBLANES, (1 << 20) // (4 * d)))


def _prenorm_fwd(name, x, g, scale, shift):
    s, d = x.shape
    tr = _row_tile(s, d)

    def body(x_ref, g_ref, sc_ref, sh_ref, h_ref):
        xv = x_ref[...]
        r = lax.rsqrt(jnp.mean(xv * xv, axis=-1, keepdims=True) + RMS_EPS)
        h_ref[...] = ((xv * r * g_ref[...]) * (1.0 + sc_ref[...]) + sh_ref[...]).astype(BF16)

    row = pl.BlockSpec((tr, d), lambda i: (i, 0))
    vec = pl.BlockSpec((1, d), lambda i: (0, 0))
    return pl.pallas_call(body, name=name, grid=(s // tr,), in_specs=[row, vec, vec, vec], out_specs=row,
                          out_shape=jax.ShapeDtypeStruct((s, d), BF16), compiler_params=_cparams())(x, g, scale, shift)


def _prenorm_bwd(name, dh, x, g, scale, dx_res):
    s, d = x.shape
    tr = _row_tile(s, d)

    def body(dh_ref, x_ref, g_ref, sc_ref, dxr_ref, dx_ref, sums_ref):
        @pl.when(pl.program_id(0) == 0)
        def _():
            sums_ref[...] = jnp.zeros_like(sums_ref)

        xv, dhv, gv = x_ref[...], dh_ref[...].astype(F32), g_ref[...]
        r = lax.rsqrt(jnp.mean(xv * xv, axis=-1, keepdims=True) + RMS_EPS)
        xhat = xv * r
        dxn = dhv * (1.0 + sc_ref[...])
        dxhat = dxn * gv
        dx = r * (dxhat - xhat * jnp.mean(dxhat * xhat, axis=-1, keepdims=True))
        dx_ref[...] = dxr_ref[...] + dx
        sums_ref[0:1, :] += jnp.sum(dhv * (xhat * gv), axis=0, keepdims=True)
        sums_ref[1:2, :] += jnp.sum(dhv, axis=0, keepdims=True)
        sums_ref[2:3, :] += jnp.sum(dxn * xhat, axis=0, keepdims=True)

    row = pl.BlockSpec((tr, d), lambda i: (i, 0))
    vec = pl.BlockSpec((1, d), lambda i: (0, 0))
    acc = pl.BlockSpec((SUBLANES, d), lambda i: (0, 0))
    return pl.pallas_call(
        body, name=name, grid=(s // tr,), in_specs=[row, row, vec, vec, row], out_specs=[row, acc],
        out_shape=[jax.ShapeDtypeStruct((s, d), F32), jax.ShapeDtypeStruct((SUBLANES, d), F32)],
        compiler_params=_cparams())(dh, x, g, scale, dx_res)


def _postnorm_bwd(name, dxn, y, g, gate):
    s, d = y.shape
    tr = _row_tile(s, d)

    def body(dx_ref, y_ref, g_ref, gt_ref, dy_ref, sums_ref):
        @pl.when(pl.program_id(0) == 0)
        def _():
            sums_ref[...] = jnp.zeros_like(sums_ref)

        yv, dxv, gv = y_ref[...], dx_ref[...], g_ref[...]
        r = lax.rsqrt(jnp.mean(yv * yv, axis=-1, keepdims=True) + RMS_EPS)
        yhat = yv * r
        dn = dxv * gt_ref[...]
        dyhat = dn * gv
        dy_ref[...] = (r * (dyhat - yhat * jnp.mean(dyhat * yhat, axis=-1, keepdims=True))).astype(BF16)
        sums_ref[0:1, :] += jnp.sum(dxv * (yhat * gv), axis=0, keepdims=True)
        sums_ref[1:2, :] += jnp.sum(dn * yhat, axis=0, keepdims=True)

    row = pl.BlockSpec((tr, d), lambda i: (i, 0))
    vec = pl.BlockSpec((1, d), lambda i: (0, 0))
    acc = pl.BlockSpec((SUBLANES, d), lambda i: (0, 0))
    return pl.pallas_call(
        body, name=name, grid=(s // tr,), in_specs=[row, row, vec, vec], out_specs=[row, acc],
        out_shape=[jax.ShapeDtypeStruct((s, d), BF16), jax.ShapeDtypeStruct((SUBLANES, d), F32)],
        compiler_params=_cparams())(dxn, y, g, gate)


def _loss_grad(name, y, target):
    s, d = y.shape
    tr = _row_tile(s, d)

    def body(y_ref, t_ref, dy_ref, loss_ref):
        @pl.when(pl.program_id(0) == 0)
        def _():
            loss_ref[...] = jnp.zeros_like(loss_ref)

        err = y_ref[...] - t_ref[...]
        dy_ref[...] = err * (1.0 / d)
        part = jnp.sum(jnp.sum(err * err, axis=-1, keepdims=True), axis=0, keepdims=True) * (0.5 / d)
        loss_ref[...] += jnp.broadcast_to(part, loss_ref.shape)

    row = pl.BlockSpec((tr, d), lambda i: (i, 0))
    acc = pl.BlockSpec((SUBLANES, LANES), lambda i: (0, 0))
    return pl.pallas_call(
        body, name=name, grid=(s // tr,), in_specs=[row, row], out_specs=[row, acc],
        out_shape=[jax.ShapeDtypeStruct((s, d), F32), jax.ShapeDtypeStruct((SUBLANES, LANES), F32)],
        compiler_params=_cparams())(y, target)


def _gelu(y):
    c = math.sqrt(2.0 / math.pi)
    return 0.5 * y * (1.0 + jnp.tanh(c * (y + 0.044715 * (y * y * y))))


def _gelu_grad(y):
    c = math.sqrt(2.0 / math.pi)
    th = jnp.tanh(c * (y + 0.044715 * (y * y * y)))
    return 0.5 * (1.0 + th) + 0.5 * y * (1.0 - th * th) * c * (1.0 + 3.0 * 0.044715 * (y * y))


def _cmul_add(br, bi, ar, ai, xr, xi):
    return br + ar * xr - ai * xi, bi + ar * xi + ai * xr


def _scan_rows(x_ref, row0, n_steps, ns2, pow_ref, tab_ref, carry_ref, reverse, fold=None):
    assert n_steps % SUBLANES == 0
    wc = min(S5_CHUNK, ns2)
    sub = lax.broadcasted_iota(jnp.int32, (SUBLANES, wc), 0)
    unroll = S5_UNROLL if n_steps % S5_UNROLL == 0 else 1
    for c0 in range(0, ns2, wc):
        re = slice(c0, c0 + wc)
        im = slice(ns2 + c0, ns2 + c0 + wc)
        first_power = slice(n_steps - 1, n_steps) if reverse else slice(0, 1)
        ar = jnp.broadcast_to(pow_ref[first_power, re], (SUBLANES, wc))
        ai = jnp.broadcast_to(pow_ref[first_power, im], (SUBLANES, wc))
        rows = lambda r: pl.ds(pl.multiple_of(row0 + r * SUBLANES, SUBLANES), SUBLANES)
        step_of = lambda i: (n_steps - 1 - i) if reverse else i

        def local(i, carry, re=re, im=im, ar=ar, ai=ai):
            for u in range(unroll):
                r = step_of(i * unroll + u)
                carry = _cmul_add(x_ref[rows(r), re], x_ref[rows(r), im], ar, ai, *carry)
                x_ref[rows(r), re], x_ref[rows(r), im] = carry
            return carry

        zero = jnp.zeros((SUBLANES, wc), F32)
        lr, li = lax.fori_loop(0, n_steps // unroll, local, (zero, zero))

        tabs = [tab_ref[k, :, re] for k in range(8)]
        for lvl, k in enumerate((1, 2, 4)):
            sh = (SUBLANES - k) if reverse else k
            lr, li = _cmul_add(lr, li, tabs[2 * lvl], tabs[2 * lvl + 1], pltpu.roll(lr, sh, 0), pltpu.roll(li, sh, 0))
        cr, ci = carry_ref[0:1, re], carry_ref[0:1, im]
        lr, li = _cmul_add(lr, li, tabs[6], tabs[7], cr, ci)
        edge, away, last = (SUBLANES - 1, SUBLANES - 1, 0) if reverse else (0, 1, SUBLANES - 1)
        carry_ref[0:1, re] = lr[last:last + 1, :]
        carry_ref[0:1, im] = li[last:last + 1, :]
        er = jnp.where(sub == edge, cr, pltpu.roll(lr, away, 0))
        ei = jnp.where(sub == edge, ci, pltpu.roll(li, away, 0))

        def fix(j, acc, re=re, im=im, er=er, ei=ei, c0=c0):
            base = pl.ds(pl.multiple_of(j * SUBLANES, SUBLANES), SUBLANES)
            pw_r, pw_i = pow_ref[base, re], pow_ref[base, im]
            for i in range(SUBLANES):
                r = j * SUBLANES + i
                xr, xi = _cmul_add(x_ref[rows(r), re], x_ref[rows(r), im], pw_r[i:i + 1, :], pw_i[i:i + 1, :], er, ei)
                x_ref[rows(r), re], x_ref[rows(r), im] = xr, xi
                if fold is not None:
                    acc = fold(c0, r, xr, xi, acc)
            return acc

        acc = lax.fori_loop(0, n_steps // SUBLANES, fix, (zero, zero) if fold is not None else 0)
        if fold is not None:
            fold(c0, None, None, None, acc)


def _s5_fwd(name, u, b_blk, c_blk, a_f, tab_f, dskip, w_glu, b_glu):
    s, w = u.shape[0], w_glu.shape[0]
    nkb = w // LANES
    ns2 = b_blk.shape[2] // 2 * nkb
    half = ns2 // nkb
    t = min(S5_ROWS, s)
    nblk = s // t

    def body(u_ref, b_ref, c_ref, a_ref, tab_ref, ds_ref, wg_ref, bg_ref, y_ref, ys_ref, cs_ref, xs, carry):
        @pl.when(pl.program_id(0) == 0)
        def _():
            carry[...] = jnp.zeros_like(carry)

        cs_ref[0] = carry[...]
        for kb in range(nkb):
            bu = _dot(u_ref[:, kb * LANES:(kb + 1) * LANES], b_ref[kb], NN)
            xs[:, kb * half:(kb + 1) * half] = bu[:, :half]
            xs[:, ns2 + kb * half:ns2 + (kb + 1) * half] = bu[:, half:]
        _scan_rows(xs, 0, t // SUBLANES, ns2, a_ref, tab_ref, carry, reverse=False)
        for kb in range(nkb):
            cols = slice(kb * LANES, (kb + 1) * LANES)
            yk = _dot(xs[:, kb * half:(kb + 1) * half].astype(BF16), c_ref[kb, :half, :], NN)
            yk += _dot(xs[:, ns2 + kb * half:ns2 + (kb + 1) * half].astype(BF16), c_ref[kb, half:, :], NN)
            y_ref[:, cols] = yk + ds_ref[:, cols] * u_ref[:, cols].astype(F32)
        z = _gelu(y_ref[...])
        gate = _sigmoid(_dot(z.astype(BF16), wg_ref[...], NN) + bg_ref[...])
        ys_ref[...] = (z * gate).astype(BF16)

    row = pl.BlockSpec((t, w), lambda i: (i, 0))
    full = lambda shape: pl.BlockSpec(shape, lambda i: (0,) * len(shape))
    return pl.pallas_call(
        body, name=name, grid=(nblk,),
        in_specs=[row, full(b_blk.shape), full(c_blk.shape), full(a_f.shape), full(tab_f.shape), full(dskip.shape),
                  full(w_glu.shape), full(b_glu.shape)],
        out_specs=[row, row, pl.BlockSpec((1, 1, 2 * ns2), lambda i: (i, 0, 0))],
        out_shape=[jax.ShapeDtypeStruct((s, w), F32), jax.ShapeDtypeStruct((s, w), BF16),
                   jax.ShapeDtypeStruct((nblk, 1, 2 * ns2), F32)],
        scratch_shapes=[pltpu.VMEM((t, 2 * ns2), F32), pltpu.VMEM((1, 2 * ns2), F32)],
        compiler_params=_cparams(),
    )(u, b_blk, c_blk, a_f, tab_f, dskip, w_glu, b_glu)


def _s5_bwd(name, u, dys, y, carries, b_blk, c_blk, a_f, a_r, tab_f, tab_r, dskip, w_glu, b_glu):
    s, w = u.shape[0], w_glu.shape[0]
    nkb = w // LANES
    ns2 = b_blk.shape[2] // 2 * nkb
    half = ns2 // nkb
    t = min(S5_ROWS, s)
    nblk = s // t
    ng = t // SUBLANES

    def body(u_ref, dys_ref, y_ref, cs_ref, b_ref, c_ref, af_ref, ar_ref, tabf_ref, tabr_ref, ds_ref, wg_ref, bg_ref,
             du_ref, db_ref, dc_ref, da_ref, dwg_ref, vec_ref, xs, gs, dyv, fcarry, gcarry):
        @pl.when(pl.program_id(0) == 0)
        def _():
            db_ref[...] = jnp.zeros_like(db_ref)
            dc_ref[...] = jnp.zeros_like(dc_ref)
            da_ref[...] = jnp.zeros_like(da_ref)
            dwg_ref[...] = jnp.zeros_like(dwg_ref)
            vec_ref[...] = jnp.zeros_like(vec_ref)
            gcarry[...] = jnp.zeros_like(gcarry)

        yv = y_ref[...]
        z = _gelu(yv)
        zb = z.astype(BF16)
        gate = _sigmoid(_dot(zb, wg_ref[...], NN) + bg_ref[...])
        dout = dys_ref[...].astype(F32)
        dt = dout * z * gate * (1.0 - gate)
        dtb = dt.astype(BF16)
        dz = dout * gate + _dot(dtb, wg_ref[...], NT)
        dy = dz * _gelu_grad(yv)
        dyv[...] = dy
        dwg_ref[...] += _dot(zb, dtb, TN)
        vec_ref[0:1, :] += jnp.sum(dt, axis=0, keepdims=True)
        vec_ref[1:2, :] += jnp.sum(dy * u_ref[...].astype(F32), axis=0, keepdims=True)

        fcarry[...] = cs_ref[0]
        xs[0:SUBLANES, :] = jnp.broadcast_to(cs_ref[0], (SUBLANES, 2 * ns2))
        for kb in range(nkb):
            bu = _dot(u_ref[:, kb * LANES:(kb + 1) * LANES], b_ref[kb], NN)
            xs[SUBLANES:, kb * half:(kb + 1) * half] = bu[:, :half]
            xs[SUBLANES:, ns2 + kb * half:ns2 + (kb + 1) * half] = bu[:, half:]
        _scan_rows(xs, SUBLANES, ng, ns2, af_ref, tabf_ref, fcarry, reverse=False)
        first_segment = lax.broadcasted_iota(jnp.int32, (SUBLANES, 2 * ns2), 0) == 0
        xs[0:SUBLANES, :] = jnp.where(first_segment, xs[0:SUBLANES, :], pltpu.roll(xs[t:t + SUBLANES, :], 1, 0))

        for kb in range(nkb):
            dyk = dyv[:, kb * LANES:(kb + 1) * LANES].astype(BF16)
            re = slice(kb * half, (kb + 1) * half)
            im = slice(ns2 + kb * half, ns2 + (kb + 1) * half)
            gs[:, re] = _dot(dyk, c_ref[kb, :half, :], NT)
            gs[:, im] = _dot(dyk, c_ref[kb, half:, :], NT)
            dc_ref[kb, :half, :] += _dot(xs[SUBLANES:, re].astype(BF16), dyk, TN)
            dc_ref[kb, half:, :] += _dot(xs[SUBLANES:, im].astype(BF16), dyk, TN)

        def fold(c0, r, gr, gi, acc):
            wc = min(S5_CHUNK, ns2)
            re = slice(c0, c0 + wc)
            im = slice(ns2 + c0, ns2 + c0 + wc)
            if r is None:
                da_ref[:, re] += acc[0]
                da_ref[:, im] += acc[1]
                return acc
            before = pl.ds(pl.multiple_of(r * SUBLANES, SUBLANES), SUBLANES)
            xpr, xpi = xs[before, re], xs[before, im]
            return acc[0] + gr * xpr + gi * xpi, acc[1] - gr * xpi + gi * xpr

        _scan_rows(gs, 0, ng, ns2, ar_ref, tabr_ref, gcarry, reverse=True, fold=fold)

        for kb in range(nkb):
            cols = slice(kb * LANES, (kb + 1) * LANES)
            re = slice(kb * half, (kb + 1) * half)
            im = slice(ns2 + kb * half, ns2 + (kb + 1) * half)
            uk = u_ref[:, cols]
            gr = gs[:, re].astype(BF16)
            gi = gs[:, im].astype(BF16)
            db_ref[kb, :, :half] += _dot(uk, gr, TN)
            db_ref[kb, :, half:] += _dot(uk, gi, TN)
            duk = _dot(gr, b_ref[kb, :, :half], NT) + _dot(gi, b_ref[kb, :, half:], NT)
            du_ref[:, cols] = (duk + ds_ref[:, cols] * dyv[:, cols]).astype(BF16)

    rev = lambda i: (nblk - 1 - i, 0)
    row = pl.BlockSpec((t, w), rev)
    full = lambda shape: pl.BlockSpec(shape, lambda i: (0,) * len(shape))
    return pl.pallas_call(
        body, name=name, grid=(nblk,),
        in_specs=[row, row, row, pl.BlockSpec((1, 1, 2 * ns2), lambda i: (nblk - 1 - i, 0, 0)),
                  full(b_blk.shape), full(c_blk.shape), full(a_f.shape), full(a_r.shape), full(tab_f.shape),
                  full(tab_r.shape), full(dskip.shape), full(w_glu.shape), full(b_glu.shape)],
        out_specs=[row, full(b_blk.shape), full(c_blk.shape), full((SUBLANES, 2 * ns2)), full((w, w)),
                   full((SUBLANES, w))],
        out_shape=[jax.ShapeDtypeStruct((s, w), BF16), jax.ShapeDtypeStruct(b_blk.shape, F32),
                   jax.ShapeDtypeStruct(c_blk.shape, F32), jax.ShapeDtypeStruct((SUBLANES, 2 * ns2), F32),
                   jax.ShapeDtypeStruct((w, w), F32), jax.ShapeDtypeStruct((SUBLANES, w), F32)],
        scratch_shapes=[pltpu.VMEM((t + SUBLANES, 2 * ns2), F32), pltpu.VMEM((t, 2 * ns2), F32),
                        pltpu.VMEM((t, w), F32), pltpu.VMEM((1, 2 * ns2), F32), pltpu.VMEM((1, 2 * ns2), F32)],
        compiler_params=_cparams(),
    )(u, dys, y, carries, b_blk, c_blk, a_f, a_r, tab_f, tab_r, dskip, w_glu, b_glu)


def _log_sigmoid(x):
    return jnp.minimum(x, 0.0) - jnp.log(1.0 + jnp.exp(-jnp.abs(x)))


def _cum_fwd(name, f_t, b_f):
    h, s = f_t.shape
    tc = _pick(s, 512)
    nb = s // tc

    def body(f_ref, b_ref, c_ref, carry):
        @pl.when(pl.program_id(0) == 0)
        def _():
            carry[...] = jnp.zeros_like(carry)

        lf = _log_sigmoid(f_ref[...] + b_ref[...])
        upper = (lax.broadcasted_iota(jnp.int32, (tc, tc), 0) <= lax.broadcasted_iota(jnp.int32, (tc, tc), 1))
        cum = lax.dot_general(lf, upper.astype(F32), NN, precision=lax.Precision.HIGHEST,
                              preferred_element_type=F32) + carry[...]
        c_ref[...] = cum
        carry[...] += jnp.sum(lf, axis=1, keepdims=True)

    blk = pl.BlockSpec((h, tc), lambda i: (0, i))
    return pl.pallas_call(body, name=name, grid=(nb,), in_specs=[blk, pl.BlockSpec((h, 1), lambda i: (0, 0))],
                          out_specs=blk, out_shape=jax.ShapeDtypeStruct((h, s), F32),
                          scratch_shapes=[pltpu.VMEM((h, 1), F32)], compiler_params=_cparams())(f_t, b_f)


def _cum_bwd(name, dcq, dck, f_t, b_f):
    h, s = f_t.shape
    tc = _pick(s, 512)
    nb = s // tc

    def body(dcq_ref, dck_ref, f_ref, b_ref, df_ref, db_ref, carry):
        @pl.when(pl.program_id(0) == 0)
        def _():
            carry[...] = jnp.zeros_like(carry)
            db_ref[...] = jnp.zeros_like(db_ref)

        dc = dcq_ref[...] + dck_ref[...]
        lower = (lax.broadcasted_iota(jnp.int32, (tc, tc), 0) >= lax.broadcasted_iota(jnp.int32, (tc, tc), 1))
        dlf = lax.dot_general(dc, lower.astype(F32), NN, precision=lax.Precision.HIGHEST,
                              preferred_element_type=F32) + carry[...]
        carry[...] += jnp.sum(dc, axis=1, keepdims=True)
        df = dlf * _sigmoid(-(f_ref[...] + b_ref[...]))
        df_ref[...] = df
        db_ref[...] += jnp.broadcast_to(jnp.sum(df, axis=1, keepdims=True), db_ref.shape)

    blk = pl.BlockSpec((h, tc), lambda i: (0, nb - 1 - i))
    return pl.pallas_call(
        body, name=name, grid=(nb,), in_specs=[blk, blk, blk, pl.BlockSpec((h, 1), lambda i: (0, 0))],
        out_specs=[blk, pl.BlockSpec((h, LANES), lambda i: (0, 0))],
        out_shape=[jax.ShapeDtypeStruct((h, s), F32), jax.ShapeDtypeStruct((h, LANES), F32)],
        scratch_shapes=[pltpu.VMEM((h, 1), F32)], compiler_params=_cparams())(dcq, dck, f_t, b_f)


def _attn_fwd(name, qkv, q_blk, k_blk, v_blk, n_pairs, ck, side=None):
    s = qkv.shape[0]
    dh = LANES // 2
    t = min(ATT_BLOCK, s)
    nq = s // t
    scale = dh ** -0.5

    def body(q_ref, k_ref, v_ref, ck_ref, o_ref, lse_ref, m_s, acc_s):
        i = pl.program_id(1)
        low = lax.broadcasted_iota(jnp.int32, (1, LANES), 1) < dh
        qs = (q_ref[...].astype(F32) * scale).astype(BF16)
        zero = jnp.zeros_like(qs)
        qh = (jnp.where(low, qs, zero), jnp.where(low, zero, qs))
        m_s[...] = jnp.full(m_s.shape, -1e30, F32)
        acc_s[...] = jnp.zeros_like(acc_s)
        causal = (lax.broadcasted_iota(jnp.int32, (t, t), 1) <= lax.broadcasted_iota(jnp.int32, (t, t), 0))

        def step(j, diagonal):
            r0 = pl.multiple_of(j * t, t)
            kj = k_ref[pl.ds(r0, t), :]
            vj = v_ref[pl.ds(r0, t), :]
            one = jnp.ones_like(vj)
            vh = (jnp.where(low, vj, one), jnp.where(low, one, vj))
            for hd in range(2):
                sc = _dot(qh[hd], kj, NT) - ck_ref[hd, j]
                if diagonal:
                    sc = jnp.where(causal, sc, -1e30)
                m_old = m_s[hd]
                m_new = jnp.maximum(m_old, jnp.max(sc, axis=1, keepdims=True))
                p = jnp.exp(sc - m_new)
                acc_s[hd] = jnp.exp(m_old - m_new) * acc_s[hd] + _dot(p.astype(BF16), vh[hd], NN)
                m_s[hd] = m_new

        def full(j, _):
            step(j, False)
            return 0

        lax.fori_loop(0, i, full, 0)
        step(i, True)
        a0, a1 = acc_s[0], acc_s[1]
        o_ref[...] = jnp.where(low, a0 / pltpu.roll(a0, dh, 1), a1 / pltpu.roll(a1, dh, 1)).astype(BF16)
        lse_ref[0] = m_s[0] + jnp.log(a0[:, dh:dh + 1])
        lse_ref[1] = m_s[1] + jnp.log(a1[:, 0:1])

    return _hosted_call(
        body, side, name, (n_pairs, nq),
        [pl.BlockSpec((t, LANES), lambda hp, i: (i, q_blk + hp)),
         pl.BlockSpec((s, LANES), lambda hp, i: (0, k_blk + hp)),
         pl.BlockSpec((s, LANES), lambda hp, i: (0, v_blk + hp)),
         pl.BlockSpec((2, nq, 1, t), lambda hp, i: (hp, 0, 0, 0))],
        [pl.BlockSpec((t, LANES), lambda hp, i: (i, hp)), pl.BlockSpec((2, t, 1), lambda hp, i: (hp, i, 0))],
        [jax.ShapeDtypeStruct((s, LANES * n_pairs), BF16), jax.ShapeDtypeStruct((2 * n_pairs, s, 1), F32)],
        [pltpu.VMEM((2, t, 1), F32), pltpu.VMEM((2, t, LANES), F32)], (qkv, qkv, qkv, ck))


def _attn_bwd(name, qkv, q_blk, k_blk, v_blk, n_pairs, o, do, lse_rows, ck_cols):
    s = qkv.shape[0]
    dh = LANES // 2
    t = min(ATT_BLOCK, s)
    nk = s // t
    scale = dh ** -0.5

    def body(q_ref, k_ref, v_ref, o_ref, do_ref, lse_ref, ck_ref,
             dq_ref, dk_ref, dv_ref, dcq_ref, dck_ref, delta, dqt, dk_acc, dv_acc):
        j = pl.program_id(1)
        low = lax.broadcasted_iota(jnp.int32, (1, LANES), 1) < dh
        low_rows = lax.broadcasted_iota(jnp.int32, (LANES, 1), 0) < dh

        @pl.when(j == 0)
        def _():
            dqt[...] = jnp.zeros_like(dqt)
            sel = (jnp.broadcast_to(low, (SUBLANES, LANES)).astype(F32), jnp.broadcast_to(~low, (SUBLANES, LANES)).astype(F32))

            def fill(i, _):
                r0 = pl.multiple_of(i * t, t)
                prod = do_ref[pl.ds(r0, t), :].astype(F32) * o_ref[pl.ds(r0, t), :].astype(F32)
                for hd in range(2):
                    delta[hd, i] = lax.dot_general(sel[hd], prod, NT, precision=lax.Precision.HIGHEST,
                                                   preferred_element_type=F32)
                return 0

            lax.fori_loop(0, nk, fill, 0)

        kj, vj = k_ref[...], v_ref[...]
        zero, one = jnp.zeros_like(kj), jnp.ones_like(kj)
        kh = (jnp.where(low, kj, zero), jnp.where(low, zero, kj))
        vh = (jnp.where(low, vj, zero), jnp.where(low, zero, vj))
        kjt = kj.astype(F32).T.astype(BF16)
        one_t = jnp.ones_like(kjt)
        kht = (jnp.where(low_rows, kjt, one_t), jnp.where(low_rows, one_t, kjt))
        dk_acc[...] = jnp.zeros_like(dk_acc)
        dv_acc[...] = jnp.zeros_like(dv_acc)
        causal_t = (lax.broadcasted_iota(jnp.int32, (t, t), 0) <= lax.broadcasted_iota(jnp.int32, (t, t), 1))

        def step(i, diagonal):
            r0 = pl.multiple_of(i * t, t)
            qi = (q_ref[pl.ds(r0, t), :].astype(F32) * scale).astype(BF16)
            doi = do_ref[pl.ds(r0, t), :]
            qone, dzero = jnp.ones_like(qi), jnp.zeros_like(doi)
            qsel = (jnp.where(low, qi, qone), jnp.where(low, qone, qi))
            dosel = (jnp.where(low, doi, dzero), jnp.where(low, dzero, doi))
            for hd in range(2):
                st = _dot(kh[hd], qi, NT) - ck_ref[hd] - lse_ref[hd, i]
                pt = jnp.exp(st)
                if diagonal:
                    pt = jnp.where(causal_t, pt, 0.0)
                dst = pt * (_dot(vh[hd], doi, NT) - delta[hd, i, 0:1, :])
                dsb = dst.astype(BF16)
                dv_acc[...] += _dot(pt.astype(BF16), dosel[hd], NN)
                dk_acc[hd] += _dot(dsb, qsel[hd], NN)
                dqt[hd, i] += _dot(kht[hd], dsb, NN)

        step(j, True)

        def rest(i, _):
            step(i, False)
            return 0

        lax.fori_loop(j + 1, nk, rest, 0)
        dk_ref[...] = jnp.where(low, dk_acc[0], dk_acc[1]).astype(BF16)
        dv_ref[...] = dv_acc[...].astype(BF16)
        dck_ref[0] = -dk_acc[0][:, dh:dh + 1]
        dck_ref[1] = -dk_acc[1][:, 0:1]

        @pl.when(j == nk - 1)
        def _():
            def emit(i, _):
                r0 = pl.multiple_of(i * t, t)
                d0, d1 = dqt[0, i], dqt[1, i]
                dq_ref[pl.ds(r0, t), :] = (jnp.where(low_rows, d0, d1) * scale).T.astype(BF16)
                dcq_ref[0, i] = d0[dh:dh + 1, :]
                dcq_ref[1, i] = d1[0:1, :]
                return 0

            lax.fori_loop(0, nk, emit, 0)

    col_blk = lambda base: pl.BlockSpec((t, LANES), lambda hp, j: (j, base + hp))
    col_all = lambda base: pl.BlockSpec((s, LANES), lambda hp, j: (0, base + hp))
    rows_all = pl.BlockSpec((2, nk, 1, t), lambda hp, j: (hp, 0, 0, 0))
    return pl.pallas_call(
        body, name=name, grid=(n_pairs, nk),
        in_specs=[col_all(q_blk), col_blk(k_blk), col_blk(v_blk), col_all(0), col_all(0), rows_all,
                  pl.BlockSpec((2, t, 1), lambda hp, j: (hp, j, 0))],
        out_specs=[col_all(0), col_blk(0), col_blk(0), rows_all, pl.BlockSpec((2, t, 1), lambda hp, j: (hp, j, 0))],
        out_shape=[jax.ShapeDtypeStruct((s, LANES * n_pairs), BF16), jax.ShapeDtypeStruct((s, LANES * n_pairs), BF16),
                   jax.ShapeDtypeStruct((s, LANES * n_pairs), BF16), jax.ShapeDtypeStruct((2 * n_pairs, nk, 1, t), F32),
                   jax.ShapeDtypeStruct((2 * n_pairs, s, 1), F32)],
        scratch_shapes=[pltpu.VMEM((2, nk, SUBLANES, t), F32), pltpu.VMEM((2, nk, LANES, t), F32),
                        pltpu.VMEM((2, t, LANES), F32), pltpu.VMEM((t, LANES), F32)],
        compiler_params=_cparams(),
    )(qkv, qkv, qkv, o, do, lse_rows, ck_cols)


def _adamw(name, w, g, m, v):
    r, c = w.shape
    tr = _pick8(r, max(SUBLANES, (1 << 20) // (4 * c)))

    def body(w_ref, g_ref, m_ref, v_ref, d_ref, mo_ref, vo_ref):
        gv = g_ref[...]
        m2 = ADAM_B1 * m_ref[...] + (1.0 - ADAM_B1) * gv
        v2 = ADAM_B2 * v_ref[...] + (1.0 - ADAM_B2) * (gv * gv)
        m_hat = m2 / (1.0 - ADAM_B1 ** ADAM_STEP)
        v_hat = v2 / (1.0 - ADAM_B2 ** ADAM_STEP)
        d_ref[...] = -ADAM_LR * (m_hat / (jnp.sqrt(v_hat) + ADAM_EPS) + ADAM_WD * w_ref[...])
        mo_ref[...] = m2
        vo_ref[...] = v2

    blk = pl.BlockSpec((tr, c), lambda i: (i, 0))
    sh = jax.ShapeDtypeStruct((r, c), F32)
    return pl.pallas_call(body, name=name, grid=(r // tr,), in_specs=[blk] * 4, out_specs=[blk] * 3,
                          out_shape=[sh, sh, sh], compiler_params=_cparams())(w, g, m, v)


def _pick8(dim, target, mult=SUBLANES):
    best, t = None, mult
    while t <= min(dim, target):
        if dim % t == 0:
            best = t
        t += mult
    return best or dim


BF16_ROWS = 16


def _sum_blocks(name, x, out_dtype):
    n, r, c = x.shape
    tr = _pick8(r, max(BF16_ROWS, (1 << 19) // (4 * c)), BF16_ROWS)

    def body(x_ref, o_ref):
        acc = x_ref[0].astype(F32)
        for i in range(1, n):
            acc = acc + x_ref[i].astype(F32)
        o_ref[...] = acc.astype(out_dtype)

    return pl.pallas_call(body, name=name, grid=(r // tr,),
                          in_specs=[pl.BlockSpec((n, tr, c), lambda i: (0, i, 0))],
                          out_specs=pl.BlockSpec((tr, c), lambda i: (i, 0)),
                          out_shape=jax.ShapeDtypeStruct((r, c), out_dtype), compiler_params=_cparams())(x)


def _add_layer(name, grads, recv, core):
    _, n, r, c = grads.shape
    tr = _pick8(r, max(BF16_ROWS, (1 << 19) // (4 * c)), BF16_ROWS)

    def body(core_ref, g_ref, r_ref, o_ref):
        o_ref[...] = (g_ref[...].astype(F32) + r_ref[...].astype(F32)).astype(BF16)

    grid_spec = pltpu.PrefetchScalarGridSpec(
        num_scalar_prefetch=1, grid=(r // tr,),
        in_specs=[pl.BlockSpec((None, n, tr, c), lambda i, core_ref: (core_ref[0], 0, i, 0)),
                  pl.BlockSpec((n, tr, c), lambda i, core_ref: (0, i, 0))],
        out_specs=pl.BlockSpec((n, tr, c), lambda i, core_ref: (0, i, 0)))
    return pl.pallas_call(body, name=name, grid_spec=grid_spec,
                          out_shape=jax.ShapeDtypeStruct((n, r, c), BF16), compiler_params=_cparams())(core, grads, recv)


def _all_gather(name, x_shard):
    m_per, n = x_shard.shape

    def body(x_ref, out_ref, send_sems, recv_sems):
        x, y, c = lax.axis_index("x"), lax.axis_index("y"), lax.axis_index("c")
        me, sibling = (x, y, c), (x, y, 1 - c)
        chips = [(1 - x, y), (x, 1 - y), (1 - x, 1 - y)]

        def rows(px, py, pc):
            return out_ref.at[pl.ds((4 * px + 2 * py + pc) * m_per, m_per), :]

        def copy(k, block, to, src=None):
            return pltpu.make_async_remote_copy(
                src_ref=rows(*block) if src is None else src, dst_ref=rows(*block),
                send_sem=send_sems.at[k], recv_sem=recv_sems.at[k], device_id=to, device_id_type=MESH)

        first = [copy(0, me, sibling, src=x_ref)]
        first += [copy(1 + j, me, (*chip, c), src=x_ref) for j, chip in enumerate(chips)]
        for cp in first:
            cp.start()
        passed = [copy(4 + j, (*chip, c), sibling) for j, chip in enumerate(chips)]
        for j, chip in enumerate(chips):
            copy(1 + j, (*chip, c), me).wait_recv()
            passed[j].start()
        copy(0, sibling, me).wait_recv()
        for j, chip in enumerate(chips):
            copy(4 + j, (*chip, 1 - c), me).wait_recv()
        for cp in first + passed:
            cp.wait_send()

    out = pl.pallas_call(
        body, name=name, out_shape=jax.ShapeDtypeStruct((N_DEV * m_per, n), x_shard.dtype),
        in_specs=[pl.BlockSpec(memory_space=pl.ANY)], out_specs=pl.BlockSpec(memory_space=pl.ANY),
        scratch_shapes=[pltpu.SemaphoreType.DMA((7,)), pltpu.SemaphoreType.DMA((7,))],
    )(x_shard)
    my_dev = 4 * lax.axis_index("x") + 2 * lax.axis_index("y") + lax.axis_index("c")
    return lax.dynamic_update_slice(out, x_shard, (my_dev * m_per, 0))


def _hbm_call(name, body, operands, out_shapes, n_sems):
    return pl.pallas_call(
        body, name=name, out_shape=list(out_shapes),
        in_specs=[pl.BlockSpec(memory_space=pl.ANY)] * len(operands),
        out_specs=[pl.BlockSpec(memory_space=pl.ANY)] * len(out_shapes),
        scratch_shapes=[pltpu.SemaphoreType.DMA((n_sems,)), pltpu.SemaphoreType.DMA((n_sems,))],
    )(*operands)


def _put_own(out, own, index):
    start = tuple(index) + (0,) * own.ndim
    return lax.dynamic_update_slice(out, own.reshape((1,) * len(index) + own.shape), start)


def _gather_copies(stage, ins, outs, send_sems, recv_sems):
    x, y, c = lax.axis_index("x"), lax.axis_index("y"), lax.axis_index("c")
    my_chip = 2 * x + y
    copies = []
    for w, out in enumerate(outs):
        half = out.shape[1] // 2
        rows = pl.ds(c * half, half)
        for k, (cx, cy) in enumerate([(1 - x, y), (x, 1 - y), (1 - x, 1 - y)]):
            if stage == 0:
                src, dst, to = ins[w].at[rows], out.at[my_chip, rows], (cx, cy, c)
            else:
                src = dst = out.at[2 * cx + cy, rows]
                to = (x, y, 1 - c)
            copies.append(pltpu.make_async_remote_copy(
                src_ref=src, dst_ref=dst, send_sem=send_sems.at[3 * w + k], recv_sem=recv_sems.at[3 * w + k],
                device_id=to, device_id_type=MESH))
    return copies


def _gathered_shapes(shards):
    return [jax.ShapeDtypeStruct((N_CHIPS,) + s.shape, s.dtype) for s in shards]


def _put_own_slabs(gathered, shards):
    my_chip = 2 * lax.axis_index("x") + lax.axis_index("y")
    return [_put_own(o, s, (my_chip,)) for o, s in zip(gathered, shards)]


def _gather_layer(name, shards):
    n_w = len(shards)

    def body(*refs):
        ins, outs = refs[:n_w], refs[n_w:2 * n_w]
        for stage in (0, 1):
            copies = _gather_copies(stage, ins, outs, refs[2 * n_w + 2 * stage], refs[2 * n_w + 2 * stage + 1])
            for cp in copies:
                cp.start()
            for cp in copies:
                cp.wait()

    outs = pl.pallas_call(
        body, name=name, out_shape=_gathered_shapes(shards),
        in_specs=[pl.BlockSpec(memory_space=pl.ANY)] * n_w, out_specs=[pl.BlockSpec(memory_space=pl.ANY)] * n_w,
        scratch_shapes=[pltpu.SemaphoreType.DMA((3 * n_w,))] * 4,
    )(*shards)
    return _put_own_slabs(outs, shards)


def _gather_side_jobs(shards):
    n_w = len(shards)
    between_chips = _SideJob(list(shards), _gathered_shapes(shards), {}, 3 * n_w,
                             lambda ins, outs, send, recv: _gather_copies(0, ins, outs, send, recv))
    between_cores = lambda partial: _SideJob(list(partial), _gathered_shapes(shards), {w: w for w in range(n_w)}, 3 * n_w,
                                             lambda ins, outs, send, recv: _gather_copies(1, ins, outs, send, recv))
    return between_chips, between_cores


def _swap_layers(name, grads):
    n_w = len(grads)

    def body(*refs):
        ins, outs = refs[:n_w], refs[n_w:2 * n_w]
        send_sems, recv_sems = refs[2 * n_w], refs[2 * n_w + 1]
        x, y, c = lax.axis_index("x"), lax.axis_index("y"), lax.axis_index("c")
        copies = [pltpu.make_async_remote_copy(src_ref=ins[w].at[1 - c], dst_ref=outs[w], send_sem=send_sems.at[w],
                                               recv_sem=recv_sems.at[w], device_id=(x, y, 1 - c), device_id_type=MESH)
                  for w in range(n_w)]
        for cp in copies:
            cp.start()
        for cp in copies:
            cp.wait()

    return _hbm_call(name, body, grads, [jax.ShapeDtypeStruct(g.shape[1:], g.dtype) for g in grads], n_w)


def _chip_exchange(name, parts):
    n_w = len(parts)

    def body(*refs):
        ins, outs = refs[:n_w], refs[n_w:2 * n_w]
        send_sems, recv_sems = refs[2 * n_w], refs[2 * n_w + 1]
        x, y, c = lax.axis_index("x"), lax.axis_index("y"), lax.axis_index("c")
        my_chip = 2 * x + y
        chips = [(1 - x, y), (x, 1 - y), (1 - x, 1 - y)]
        copies = [pltpu.make_async_remote_copy(
            src_ref=ins[w].at[2 * cx + cy], dst_ref=outs[w].at[my_chip], send_sem=send_sems.at[3 * w + k],
            recv_sem=recv_sems.at[3 * w + k], device_id=(cx, cy, c), device_id_type=MESH)
            for w in range(n_w) for k, (cx, cy) in enumerate(chips)]
        for cp in copies:
            cp.start()
        for cp in copies:
            cp.wait()

    outs = _hbm_call(name, body, parts, [jax.ShapeDtypeStruct(p.shape, p.dtype) for p in parts], 3 * n_w)
    my_chip = 2 * lax.axis_index("x") + lax.axis_index("y")
    return [_put_own(o, lax.dynamic_index_in_dim(p, my_chip, 0, keepdims=False), (my_chip,)) for o, p in zip(outs, parts)]


def _share_layers(name, reduced):
    n_w = len(reduced)

    def body(*refs):
        ins, outs = refs[:n_w], refs[n_w:2 * n_w]
        send_sems, recv_sems = refs[2 * n_w], refs[2 * n_w + 1]
        x, y, c = lax.axis_index("x"), lax.axis_index("y"), lax.axis_index("c")
        copies = [pltpu.make_async_remote_copy(src_ref=ins[w], dst_ref=outs[w].at[c], send_sem=send_sems.at[w],
                                               recv_sem=recv_sems.at[w], device_id=(x, y, 1 - c), device_id_type=MESH)
                  for w in range(n_w)]
        for cp in copies:
            cp.start()
        for cp in copies:
            cp.wait()

    outs = _hbm_call(name, body, reduced, [jax.ShapeDtypeStruct((2,) + r.shape, r.dtype) for r in reduced], n_w)
    return [_put_own(o, r, (lax.axis_index("c"),)) for o, r in zip(outs, reduced)]


def _pack(arrays, cols, row_multiple, dtype):
    flat = jnp.concatenate([a.reshape(-1).astype(dtype) for a in arrays])
    unit = cols * row_multiple
    total = -(-flat.shape[0] // unit) * unit
    return jnp.pad(flat, (0, total - flat.shape[0])).reshape(total // cols, cols)


def _unpack(buf, shapes):
    flat, out, off = buf.reshape(-1), [], 0
    for sh in shapes:
        n = math.prod(sh)
        out.append(flat[off:off + n].reshape(sh))
        off += n
    return out


def _discretize(lam_re, lam_im, log_dt, b_re, b_im):
    lam = lax.complex(jnp.minimum(lam_re, -EIG_CLIP), lam_im)
    dt = jnp.exp(log_dt)[:, None]
    lam_bar = jnp.exp(lam * dt)
    b_bar = ((lam_bar - 1.0) / lam)[..., None] * lax.complex(b_re, b_im)
    return jnp.real(lam_bar), jnp.imag(lam_bar), jnp.real(b_bar), jnp.imag(b_bar)


def _scan_tables(ar, ai):
    a = lax.complex(ar, ai)
    pw = [a]
    for _ in range(7):
        pw.append(pw[-1] * a)
    rows = jnp.arange(SUBLANES)[:, None]

    def build(p, reverse):
        tabs = []
        for k in (1, 2, 4):
            keep = (rows <= SUBLANES - 1 - k) if reverse else (rows >= k)
            tk = jnp.where(keep, p[k - 1][None, :], 0.0)
            tabs += [jnp.real(tk), jnp.imag(tk)]
        stack = jnp.stack(p[::-1] if reverse else p)
        tabs += [jnp.real(stack), jnp.imag(stack)]
        return jnp.stack(tabs).astype(F32)

    return build(pw, False), build([jnp.conj(p) for p in pw], True)


def _interleave_rows(a, t):
    s, w = a.shape
    return a.reshape(s // t, SUBLANES, t // SUBLANES, w).transpose(0, 2, 1, 3).reshape(s, w)


def _deinterleave_rows(a, t):
    s, w = a.shape
    return a.reshape(s // t, t // SUBLANES, SUBLANES, w).transpose(0, 2, 1, 3).reshape(s, w)


def _block_diag(per_group, groups_per_block):
    g, a, b = per_group.shape
    x = per_group.reshape(g // groups_per_block, groups_per_block, a, b)
    eye = jnp.eye(groups_per_block, dtype=per_group.dtype)
    out = x[:, :, :, None, :] * eye[None, :, None, :, None]
    return out.reshape(g // groups_per_block, groups_per_block * a, groups_per_block * b)


def _block_diag_extract(dense, groups_per_block, a, b):
    nkb = dense.shape[0]
    x = dense.reshape(nkb, groups_per_block, a, groups_per_block, b)
    idx = jnp.arange(groups_per_block)
    return x[:, idx, :, idx, :].transpose(1, 0, 2, 3).reshape(nkb * groups_per_block, a, b)


def _layer_fwd(tag, x, mod, p, wts, gather_next=None):
    s, d = x.shape
    w_ssm, w_att = p["w_glu"].shape[0], wts["w_pb"].shape[1]
    heads = p["b_f"].shape[0]
    dh = w_att // heads
    cs = d // N_CHIPS
    fs = wts["w_ffn_down"].shape[1]
    tm = _pick(s, 1024)
    row = lambda v: v.reshape(1, -1)
    sv = {}

    h = _prenorm_fwd(f"prenorm_mix_{tag}", x, row(p["g_pre_mix"]), row(mod[1]), row(mod[0]))
    uqkv = _mm_plain(f"proj_main_{tag}", h, p["w_main"], "nn", BF16, tm=1024, tn=1024, tk=1024)
    fg = _mm_plain(f"proj_gate_{tag}", h, p["w_gates"], "nn", F32, tm=1024, tn=1024, tk=1024)
    f_t = fg[:, 2 * d:2 * d + heads].T

    t5 = min(S5_ROWS, s)
    u_il = _interleave_rows(uqkv[:, :w_ssm], t5)
    y_s5, ys_il, carries = _s5_fwd(f"s5_fwd_{tag}", u_il, p["b_blk"], p["c_blk"], p["a_f"], p["tab_f"],
                                   row(p["d_skip"]), p["w_glu"], row(p["b_glu"]))
    ys = _deinterleave_rows(ys_il, t5)

    assert dh * 2 == LANES and w_ssm % LANES == 0 and w_att % LANES == 0
    n_pairs = w_att // LANES
    blocks = (w_ssm // LANES, w_ssm // LANES + n_pairs, w_ssm // LANES + 2 * n_pairs)
    cum = _cum_fwd(f"cum_fwd_{tag}", f_t, p["b_f"].reshape(heads, 1))
    t = min(ATT_BLOCK, s)
    ck_cols, ck_rows = cum.reshape(heads, s, 1), cum.reshape(heads, s // t, 1, t)
    (ya, lse), arrived = _attn_fwd(f"attn_fwd_{tag}", uqkv, *blocks, n_pairs, ck_rows,
                                   side=gather_next[0] if gather_next else None)

    tile = pl.BlockSpec((tm, cs), lambda i, j, k: (i, j))
    slab = lambda rows: pl.BlockSpec((None, rows, cs), lambda i, j, k: (j, 0, 0))

    def merge(acc, extra_refs, out_refs):
        ya_ref, wpb_ref, ga_ref, gb_ref = extra_refs
        a_ref, b_ref, m_ref = out_refs
        bv = _dot(ya_ref[...], wpb_ref[...], NN)
        a_ref[...] = acc.astype(BF16)
        b_ref[...] = bv.astype(BF16)
        m_ref[...] = (_sigmoid(ga_ref[...]) * acc + _sigmoid(gb_ref[...]) * bv).astype(BF16)

    sd_bf = jax.ShapeDtypeStruct((s, d), BF16)
    pa, pb, merged = _mm_raw(
        f"merge_{tag}", ys, wts["w_pa"], "nn", (s // tm, N_CHIPS, 1), (tm, cs),
        pl.BlockSpec((tm, w_ssm), lambda i, j, k: (i, 0)), slab(w_ssm), [sd_bf] * 3, [tile] * 3, merge,
        extra=(ya, wts["w_pb"], fg, fg),
        extra_specs=[pl.BlockSpec((tm, w_att), lambda i, j, k: (i, 0)), slab(w_att), tile,
                     pl.BlockSpec((tm, cs), lambda i, j, k: (i, j + N_CHIPS))])

    tm2 = _pick(s, POSTNORM_ROWS)
    x1, y_mix = _mm_postnorm(
        f"out_proj_{tag}", merged, pl.BlockSpec((tm2, cs), lambda i, j, k: (i, k)), wts["w_o"],
        pl.BlockSpec((None, cs, d), lambda i, j, k: (k, 0, 0)), N_CHIPS, x, row(mod[2]), row(p["g_post_mix"]))

    h2 = _prenorm_fwd(f"prenorm_ffn_{tag}", x1, row(p["g_pre_ffn"]), row(mod[4]), row(mod[3]))
    (a4, b4, hid4), next_wts = _ffn_up(f"ffn_up_{tag}", h2, wts["w_ffn_gate"], wts["w_ffn_up"],
                                       side=gather_next[1](arrived) if gather_next else None)
    x2, y_ffn = _mm_postnorm(
        f"ffn_down_{tag}", hid4, pl.BlockSpec((None, tm2, fs), lambda i, j, k: (k, i, 0)), wts["w_ffn_down"],
        pl.BlockSpec((None, fs, d), lambda i, j, k: (k, 0, 0)), N_CHIPS, x1, row(mod[5]), row(p["g_post_ffn"]))

    sv.update(x=x, h=h, uqkv=uqkv, u_il=u_il, fg=fg, f_t=f_t, y_s5=y_s5, ys=ys, carries=carries, blocks=blocks,
              ck_cols=ck_cols, lse_rows=lse.reshape(heads, s // t, 1, t), ya=ya, pa=pa, pb=pb, merged=merged, x1=x1,
              y_mix=y_mix, h2=h2, a4=a4, b4=b4, hid4=hid4, y_ffn=y_ffn)
    return x2, sv, next_wts


def _mm_postnorm(name, a, a_spec, w, w_spec, nk, x, gate, g):
    s, d = x.shape
    tm = _pick(s, POSTNORM_ROWS)
    rowspec = pl.BlockSpec((tm, d), lambda i, j, k: (i, 0))
    vec = pl.BlockSpec((1, d), lambda i, j, k: (0, 0))

    def epilogue(acc, extra_refs, out_refs):
        x_ref, gate_ref, g_ref = extra_refs
        r = lax.rsqrt(jnp.mean(acc * acc, axis=-1, keepdims=True) + RMS_EPS)
        out_refs[0][...] = x_ref[...] + gate_ref[...] * (acc * r * g_ref[...])
        out_refs[1][...] = acc

    sd = jax.ShapeDtypeStruct((s, d), F32)
    return _mm_raw(name, a, w, "nn", (s // tm, 1, nk), (tm, d), a_spec, w_spec, [sd, sd], [rowspec, rowspec], epilogue,
                   extra=(x, gate, g), extra_specs=[rowspec, vec, vec])


def _layer_bwd(tag, dx2, mod, p, wts, sv):
    s, d = dx2.shape
    w_ssm, w_att = p["w_glu"].shape[0], wts["w_pb"].shape[1]
    heads = p["b_f"].shape[0]
    cs = d // N_CHIPS
    fs = wts["w_ffn_down"].shape[1]
    tm, tk, td = _pick(s, 1024), _pick(s, 1024), d
    row = lambda v: v.reshape(1, -1)
    gr = {}

    def dw_slabs(name, act, act_spec, rows, dy, dy_spec, cols, grid_mn, out_index):
        return _mm_raw(name, act, dy, "tn", grid_mn + (s // tk,), (rows, cols), act_spec, dy_spec,
                       [jax.ShapeDtypeStruct((N_CHIPS,) + out_index[1], BF16)],
                       [pl.BlockSpec((None, rows, cols), out_index[0])], _store(BF16))[0]

    dy_ffn, sums = _postnorm_bwd(f"postnorm_bwd_ffn_{tag}", dx2, sv["y_ffn"], row(p["g_post_ffn"]), row(mod[5]))
    d_gate_f, gr["g_post_ffn"] = sums[0], sums[1]
    gr["w_ffn_down"] = dw_slabs(f"dw_down_{tag}", sv["hid4"], pl.BlockSpec((None, tk, fs), lambda i, j, k: (i, k, 0)), fs,
                                dy_ffn, pl.BlockSpec((tk, d), lambda i, j, k: (k, 0)), d, (N_CHIPS, 1),
                                (lambda i, j, k: (i, 0, 0), (fs, d)))

    def swiglu_bwd(acc, extra_refs, out_refs):
        av, bv = extra_refs[0][...].astype(F32), extra_refs[1][...].astype(F32)
        sg = _sigmoid(av)
        out_refs[0][...] = (acc * bv * (sg * (1.0 + av * (1.0 - sg)))).astype(BF16)
        out_refs[1][...] = (acc * (av * sg)).astype(BF16)

    blk4 = pl.BlockSpec((None, tm, fs), lambda i, j, k: (j, i, 0))
    sh4 = jax.ShapeDtypeStruct((N_CHIPS, s, fs), BF16)
    da4, db4 = _mm_raw(f"ffn_down_bwd_{tag}", dy_ffn, wts["w_ffn_down"], "nt", (s // tm, N_CHIPS, 1), (tm, fs),
                       pl.BlockSpec((tm, d), lambda i, j, k: (i, 0)),
                       pl.BlockSpec((None, fs, d), lambda i, j, k: (j, 0, 0)),
                       [sh4, sh4], [blk4, blk4], swiglu_bwd, extra=(sv["a4"], sv["b4"]), extra_specs=[blk4, blk4])
    for n, act4 in (("w_ffn_gate", da4), ("w_ffn_up", db4)):
        gr[n] = dw_slabs(f"d{n}_{tag}", sv["h2"], pl.BlockSpec((tk, td), lambda i, j, k: (k, i)), td,
                         act4, pl.BlockSpec((None, tk, fs), lambda i, j, k: (j, k, 0)), fs, (d // td, N_CHIPS),
                         (lambda i, j, k: (j, i, 0), (d, fs)))
    pairs = [(act4, (None, tm, fs), lambda i, kk: (kk, i, 0), wts[n], (None, td, fs), lambda j, kk: (kk, j, 0),
              N_CHIPS) for n, act4 in (("w_ffn_gate", da4), ("w_ffn_up", db4))]
    dh2 = _mm_sum(f"dh_ffn_{tag}", s, d, tm, td, pairs, F32)
    dx1, sums = _prenorm_bwd(f"prenorm_bwd_ffn_{tag}", dh2, sv["x1"], row(p["g_pre_ffn"]), row(mod[4]), dx2)
    d_scale_f, d_shift_f, gr["g_pre_ffn"] = sums[0], sums[1], sums[2]

    dy_mix, sums = _postnorm_bwd(f"postnorm_bwd_mix_{tag}", dx1, sv["y_mix"], row(p["g_post_mix"]), row(mod[2]))
    d_gate_m, gr["g_post_mix"] = sums[0], sums[1]
    gr["w_o"] = dw_slabs(f"dw_o_{tag}", sv["merged"], pl.BlockSpec((tk, cs), lambda i, j, k: (k, i)), cs,
                         dy_mix, pl.BlockSpec((tk, d), lambda i, j, k: (k, 0)), d, (N_CHIPS, 1),
                         (lambda i, j, k: (i, 0, 0), (cs, d)))

    tile = pl.BlockSpec((tm, cs), lambda i, j, k: (i, j))

    def merge_bwd(acc, extra_refs, out_refs):
        a_ref, b_ref, ga_ref, gb_ref = extra_refs
        sa, sb = _sigmoid(ga_ref[...]), _sigmoid(gb_ref[...])
        out_refs[0][...] = (acc * sa).astype(BF16)
        out_refs[1][...] = (acc * sb).astype(BF16)
        out_refs[2][...] = (acc * a_ref[...].astype(F32) * sa * (1.0 - sa)).astype(BF16)
        out_refs[3][...] = (acc * b_ref[...].astype(F32) * sb * (1.0 - sb)).astype(BF16)

    sd_bf = jax.ShapeDtypeStruct((s, d), BF16)
    d_pa, d_pb, d_ga, d_gb = _mm_raw(
        f"out_proj_bwd_{tag}", dy_mix, wts["w_o"], "nt", (s // tm, N_CHIPS, 1), (tm, cs),
        pl.BlockSpec((tm, d), lambda i, j, k: (i, 0)), pl.BlockSpec((None, cs, d), lambda i, j, k: (j, 0, 0)),
        [sd_bf] * 4, [tile] * 4, merge_bwd, extra=(sv["pa"], sv["pb"], sv["fg"], sv["fg"]),
        extra_specs=[tile, tile, tile, pl.BlockSpec((tm, cs), lambda i, j, k: (i, j + N_CHIPS))])
    d_branch = {}
    for n, act, width, d_p in (("w_pa", sv["ys"], w_ssm, d_pa), ("w_pb", sv["ya"], w_att, d_pb)):
        gr[n] = dw_slabs(f"d{n}_{tag}", act, pl.BlockSpec((tk, width), lambda i, j, k: (k, 0)), width,
                         d_p, pl.BlockSpec((tk, cs), lambda i, j, k: (k, j)), cs, (1, N_CHIPS),
                         (lambda i, j, k: (j, 0, 0), (width, cs)))
        d_branch[n] = _mm_raw(
            f"d_in_{n}_{tag}", d_p, wts[n], "nt", (s // tm, 1, N_CHIPS), (tm, width),
            pl.BlockSpec((tm, cs), lambda i, j, k: (i, k)), pl.BlockSpec((None, width, cs), lambda i, j, k: (k, 0, 0)),
            [jax.ShapeDtypeStruct((s, width), BF16)], [pl.BlockSpec((tm, width), lambda i, j, k: (i, 0))], _store(BF16))[0]
    d_ys, d_ya = d_branch["w_pa"], d_branch["w_pb"]

    dq, dk, dv, dcq, dck = _attn_bwd(f"attn_bwd_{tag}", sv["uqkv"], *sv["blocks"], w_att // LANES, sv["ya"], d_ya,
                                     sv["lse_rows"], sv["ck_cols"])
    d_f_t, d_bf = _cum_bwd(f"cum_bwd_{tag}", dcq.reshape(heads, s), dck.reshape(heads, s), sv["f_t"],
                           p["b_f"].reshape(heads, 1))
    gr["b_f"] = d_bf[:, 0]

    t5 = min(S5_ROWS, s)
    du_il, d_bblk, d_cblk, d_abar, d_wglu, vec = _s5_bwd(
        f"s5_bwd_{tag}", sv["u_il"], _interleave_rows(d_ys, t5), sv["y_s5"], sv["carries"], p["b_blk"], p["c_blk"],
        p["a_f"], p["a_r"], p["tab_f"], p["tab_r"], row(p["d_skip"]), p["w_glu"], row(p["b_glu"]))
    du = _deinterleave_rows(du_il, t5)
    gr["w_glu"] = d_wglu.astype(BF16).reshape(N_CHIPS, w_ssm // N_CHIPS, w_ssm)
    gr["b_glu"], gr["d_skip"] = vec[0], vec[1]
    gr["b_blk"], gr["c_blk"], gr["a_bar"] = d_bblk, d_cblk, d_abar

    d_f = jnp.pad(d_f_t.T, ((0, 0), (0, F_PAD - heads))).astype(BF16)
    assert w_ssm % w_att == 0 and (2 * d) % F_PAD == 0
    first = w_ssm // w_att
    main_pieces = [(du, w_ssm, 0), (dq, w_att, first), (dk, w_att, first + 1), (dv, w_att, first + 2)]
    dw = [_mm_plain(f"dw_in{n}_{tag}", sv["h"], piece, "tn", BF16, tm=1024, tn=1024, tk=1024)
          for n, piece in enumerate([du, dq, dk, dv, d_f, d_ga, d_gb])]
    w_in_grad = jnp.concatenate(dw[:4] + [dw[4][:, :heads], dw[5], dw[6]], axis=1)
    gr["w_in"] = w_in_grad.reshape(d, N_CHIPS, w_in_grad.shape[1] // N_CHIPS).transpose(1, 0, 2)
    tmx, tkx = _pick(s, 512), _pick(d, 512)
    pairs = [(piece, (tmx, width), lambda i, kk: (i, 0), p["w_main"], (d, width), lambda j, kk, blk=blk: (j, blk), 1)
             for piece, width, blk in main_pieces]
    steps = d // tkx
    pairs += [(piece, (tmx, tkx), lambda i, kk: (i, kk), p["w_gates"], (d, tkx), lambda j, kk, off=off: (j, off + kk), steps)
              for piece, off in ((d_ga, 0), (d_gb, steps))]
    pairs.append((d_f, (tmx, F_PAD), lambda i, kk: (i, 0), p["w_gates"], (d, F_PAD), lambda j, kk: (j, 2 * d // F_PAD), 1))
    dh1 = _mm_sum(f"dh_mix_{tag}", s, d, tmx, d, pairs, F32)
    dx0, sums = _prenorm_bwd(f"prenorm_bwd_mix_{tag}", dh1, sv["x"], row(p["g_pre_mix"]), row(mod[1]), dx1)
    d_scale_m, d_shift_m, gr["g_pre_mix"] = sums[0], sums[1], sums[2]

    d_mod = jnp.stack([d_shift_m, d_scale_m, d_gate_m, d_shift_f, d_scale_f, d_gate_f])
    return dx0, d_mod, gr


BIG = ("w_in", "w_glu", "w_pa", "w_pb", "w_o", "w_ffn_gate", "w_ffn_up", "w_ffn_down")
SMALL = ("b_ada", "g_pre_mix", "g_post_mix", "g_pre_ffn", "g_post_ffn", "lam_re", "lam_im", "log_dt", "b_re", "b_im",
         "c_re", "c_im", "d_skip", "b_glu", "b_f")
WEIGHTS = ("w_ada", "b_ada", "g_pre_mix", "g_post_mix", "g_pre_ffn", "g_post_ffn", "w_in", "lam_re", "lam_im", "log_dt",
           "b_re", "b_im", "c_re", "c_im", "d_skip", "w_glu", "b_glu", "b_f", "w_pa", "w_pb", "w_o", "w_ffn_gate",
           "w_ffn_up", "w_ffn_down")


def _prepare_layer(wts, small, l, seq):
    w_in = jnp.concatenate([wts["w_in"][j] for j in range(N_CHIPS)], axis=1)
    d = w_in.shape[0]
    heads = small["b_f"].shape[1]
    n_groups, n_state, group_ch = small["b_re"].shape[1:]
    w_ssm = n_groups * group_ch
    w_att = wts["w_pb"].shape[1]
    n_main = w_ssm + 3 * w_att
    gpb = LANES // group_ch
    p = {}
    p["w_main"] = w_in[:, :n_main]
    p["w_gates"] = jnp.concatenate(
        [w_in[:, n_main + heads:], w_in[:, n_main:n_main + heads], jnp.zeros((d, F_PAD - heads), BF16)], axis=1)
    p["w_glu"] = wts["w_glu"].reshape(w_ssm, w_ssm)
    for n in ("g_pre_mix", "g_post_mix", "g_pre_ffn", "g_post_ffn", "d_skip", "b_glu", "b_f"):
        p[n] = small[n][l]
    ar, ai, br, bi = _discretize(small["lam_re"][l], small["lam_im"][l], small["log_dt"][l], small["b_re"][l], small["b_im"][l])
    n_steps = min(S5_ROWS, seq) // SUBLANES
    powers = jnp.cumprod(jnp.broadcast_to(lax.complex(ar, ai).reshape(1, -1), (n_steps, ar.size)), axis=0)
    p["a_f"] = jnp.concatenate([jnp.real(powers), jnp.imag(powers)], axis=1)
    p["a_r"] = jnp.concatenate([jnp.real(powers[::-1]), -jnp.imag(powers[::-1])], axis=1)
    p["tab_f"], p["tab_r"] = _scan_tables(jnp.real(powers[-1]), jnp.imag(powers[-1]))
    bre = _block_diag(br.transpose(0, 2, 1), gpb)
    bim = _block_diag(bi.transpose(0, 2, 1), gpb)
    p["b_blk"] = jnp.concatenate([bre, bim], axis=2).astype(BF16)
    cre = _block_diag(small["c_re"][l].transpose(0, 2, 1), gpb)
    cim = _block_diag(small["c_im"][l].transpose(0, 2, 1), gpb)
    p["c_blk"] = jnp.concatenate([cre, -cim], axis=1).astype(BF16)
    return p


def _compact_partials(gr, n_state, group_ch):
    gpb = LANES // group_ch
    half = gpb * n_state
    out = dict(gr)
    out["bbar_re"] = _block_diag_extract(gr["b_blk"][:, :, :half], gpb, group_ch, n_state).transpose(0, 2, 1)
    out["bbar_im"] = _block_diag_extract(gr["b_blk"][:, :, half:], gpb, group_ch, n_state).transpose(0, 2, 1)
    out["c_re"] = _block_diag_extract(gr["c_blk"][:, :half, :], gpb, n_state, group_ch).transpose(0, 2, 1)
    out["c_im"] = -_block_diag_extract(gr["c_blk"][:, half:, :], gpb, n_state, group_ch).transpose(0, 2, 1)
    return out


def _small_grads_from_partials(gr, small, l):
    n_groups, n_state, _ = small["b_re"].shape[1:]
    ns2 = n_groups * n_state
    d_abar = jnp.sum(gr["a_bar"], axis=0)
    dar, dai = d_abar[:ns2].reshape(n_groups, n_state), d_abar[ns2:].reshape(n_groups, n_state)
    args = (small["lam_re"][l], small["lam_im"][l], small["log_dt"][l], small["b_re"][l], small["b_im"][l])
    _, vjp = jax.vjp(_discretize, *args)
    d_lam_re, d_lam_im, d_log_dt, d_b_re, d_b_im = vjp((dar, dai, gr["bbar_re"], gr["bbar_im"]))
    return dict(lam_re=d_lam_re, lam_im=d_lam_im, log_dt=d_log_dt, b_re=d_b_re, b_im=d_b_im,
                c_re=gr["c_re"], c_im=gr["c_im"])


def _fwd_bwd(xs, target, mods, small, wts0, later):
    depth = 1 + len(later)
    saved, layers, wts = [], [], [wts0]
    act = xs
    for l in range(depth):
        layers.append(_prepare_layer(wts[l], small, l, xs.shape[0]))
        shards = later[l] if l + 1 < depth and not isinstance(later[l], dict) else None
        act, sv, gathered = _layer_fwd(str(l), act, mods[l], layers[l], wts[l],
                                       gather_next=_gather_side_jobs(shards) if shards is not None else None)
        saved.append(sv)
        if l + 1 < depth:
            wts.append(dict(zip(BIG, _put_own_slabs(gathered, shards))) if shards is not None else later[l])
    dx, loss_blk = _loss_grad("loss", act, target)
    grads, d_mods = [None] * depth, [None] * depth
    for l in reversed(range(depth)):
        dx, d_mods[l], grads[l] = _layer_bwd(str(l), dx, mods[l], layers[l], wts[l], saved[l])
    stacked = {n: jnp.stack([grads[l][n] for l in range(depth)]) for n in BIG}
    return loss_blk, dx, d_mods, grads, stacked


def kernel(x, c, w_ada, b_ada, g_pre_mix, g_post_mix, g_pre_ffn, g_post_ffn, w_in, lam_re, lam_im, log_dt, b_re, b_im, c_re, c_im, d_skip, w_glu, b_glu, b_f, w_pa, w_pb, w_o, w_ffn_gate, w_ffn_up, w_ffn_down, loss_target, m_w_ada, m_b_ada, m_g_pre_mix, m_g_post_mix, m_g_pre_ffn, m_g_post_ffn, m_w_in, m_lam_re, m_lam_im, m_log_dt, m_b_re, m_b_im, m_c_re, m_c_im, m_d_skip, m_w_glu, m_b_glu, m_b_f, m_w_pa, m_w_pb, m_w_o, m_w_ffn_gate, m_w_ffn_up, m_w_ffn_down, v_w_ada, v_b_ada, v_g_pre_mix, v_g_post_mix, v_g_pre_ffn, v_g_post_ffn, v_w_in, v_lam_re, v_lam_im, v_log_dt, v_b_re, v_b_im, v_c_re, v_c_im, v_d_skip, v_w_glu, v_b_glu, v_b_f, v_w_pa, v_w_pb, v_w_o, v_w_ffn_gate, v_w_ffn_up, v_w_ffn_down):
    local = dict(locals())
    weights = {n: local[n] for n in WEIGHTS}
    moments_m = {n: local["m_" + n] for n in WEIGHTS}
    moments_v = {n: local["v_" + n] for n in WEIGHTS}
    depth, d = g_pre_mix.shape
    n_mod = w_ada.shape[2] * N_CHIPS // d
    mx, my, mc = lax.axis_index("x"), lax.axis_index("y"), lax.axis_index("c")
    my_chip = 2 * mx + my
    my_dev = 4 * mx + 2 * my + mc
    xs = x[0]

    assert depth == 2, "each core of a chip moves and reduces one layer"
    shards = [[weights[n][l].astype(BF16) for n in BIG] for l in range(depth)]
    wts0 = dict(zip(BIG, _gather_layer("gather_weights_0", shards[0])))
    small = {n: weights[n] for n in SMALL}

    c_pad = jnp.pad(c, ((0, SUBLANES - 1), (0, 0)))
    c_all = _all_gather("gather_cond", c_pad).reshape(N_DEV, SUBLANES, d)[:, 0, :]
    silu = lambda v: v * _sigmoid(v)
    n_cols = w_ada.shape[2]
    mod_shard = []
    for l in range(depth):
        bias = lax.dynamic_slice_in_dim(b_ada[l], my_chip * n_cols, n_cols)
        mod_shard.append(_mm_plain(f"ada_{l}", c_all, w_ada[l], "nn", F32, add=jnp.broadcast_to(bias, (N_DEV, n_cols)),
                                   a_fn=silu, tm=N_DEV, tn=512, tk=1024))
    mod_block = jnp.concatenate(mod_shard, axis=1)
    mod_all = _all_gather("gather_mod", mod_block).reshape(N_DEV, N_DEV, depth, n_cols)
    mod_rows = lax.dynamic_index_in_dim(mod_all[0::2], my_dev, axis=1, keepdims=False)
    mods = [mod_rows[:, l, :].reshape(n_mod, d) for l in range(depth)]

    loss_blk, dx, d_mods, grads, stacked = _fwd_bwd(xs, loss_target[0], mods, small, wts0, shards[1:])
    loss = lax.psum(loss_blk[0, 0], ("x", "y", "c"))
    grad_x = dx[None]

    core = mc.astype(jnp.int32).reshape(1)
    partials = [stacked[n] for n in BIG]
    from_sibling = _swap_layers("grads_swap_cores", partials)
    chip_parts = [_add_layer(f"grads_add_{n}", g, r, core) for n, g, r in zip(BIG, partials, from_sibling)]
    from_chips = _chip_exchange("grads_exchange_chips", chip_parts)
    reduced = [_sum_blocks(f"grads_sum_{n}", r, F32) for n, r in zip(BIG, from_chips)]
    big_grads = dict(zip(BIG, _share_layers("grads_share_cores", reduced)))

    partial_names = ("g_pre_mix", "g_post_mix", "g_pre_ffn", "g_post_ffn", "d_skip", "b_glu", "b_f", "a_bar",
                     "bbar_re", "bbar_im", "c_re", "c_im")
    n_state, group_ch = b_re.shape[2:]
    contrib = list(d_mods)
    for l in range(depth):
        compact = _compact_partials(grads[l], n_state, group_ch)
        contrib += [compact[n] for n in partial_names]
    contrib_shapes = [a.shape for a in contrib]
    block = _pack(contrib, LANES, BF16_ROWS, F32)
    rows = block.shape[0]
    all_blocks = _all_gather("gather_small_grads", block).reshape(N_DEV, rows, LANES)
    summed = _unpack(_sum_blocks("sum_small_grads", all_blocks, F32), contrib_shapes)
    per_layer = len(partial_names)
    small_grads = {n: [] for n in SMALL}
    d_mod_all = []
    for l in range(depth):
        small_grads["b_ada"].append(summed[l].reshape(-1))
        gl = dict(zip(partial_names, summed[depth + l * per_layer:depth + (l + 1) * per_layer]))
        for n in ("g_pre_mix", "g_post_mix", "g_pre_ffn", "g_post_ffn", "d_skip", "b_glu", "b_f"):
            small_grads[n].append(gl[n])
        for n, gval in _small_grads_from_partials(gl, small, l).items():
            small_grads[n].append(gval)
        d_mod_all.append(all_blocks.reshape(N_DEV, rows * LANES)[:, l * n_mod * d:(l + 1) * n_mod * d])
    small_grads = {n: jnp.stack(v) for n, v in small_grads.items()}

    g_w_ada = []
    for l in range(depth):
        cols = lax.dynamic_slice_in_dim(d_mod_all[l], my_chip * n_cols, n_cols, axis=1)
        g_w_ada.append(_mm_plain(f"dw_ada_{l}", c_all, cols, "tn", F32, a_fn=silu, tm=512, tn=512, tk=N_DEV))
    all_grads = dict(big_grads)
    all_grads.update(small_grads)
    all_grads["w_ada"] = jnp.stack(g_w_ada)

    delta, new_m, new_v = {}, {}, {}
    for n in ("w_ada",) + BIG:
        shape = weights[n].shape
        two_d = lambda a: a.reshape(-1, shape[-1])
        dl, nm, nv = _adamw(f"adamw_{n}", two_d(weights[n]), two_d(all_grads[n]), two_d(moments_m[n]), two_d(moments_v[n]))
        delta[n], new_m[n], new_v[n] = dl.reshape(shape), nm.reshape(shape), nv.reshape(shape)
    small_shapes = [weights[n].shape for n in SMALL]
    packed = [_pack([src[n] for n in SMALL], LANES, SUBLANES, F32) for src in (weights, all_grads, moments_m, moments_v)]
    outs = _adamw("adamw_small", *packed)
    for dst, buf in zip((delta, new_m, new_v), outs):
        dst.update(dict(zip(SMALL, _unpack(buf, small_shapes))))

    return (loss, grad_x, *[all_grads[n] for n in WEIGHTS], *[delta[n] for n in WEIGHTS],
            *[new_m[n] for n in WEIGHTS], *[new_v[n] for n in WEIGHTS])
```

```python
import functools
import math

import jax
import jax.numpy as jnp
from jax import lax
from jax.experimental import pallas as pl
from jax.experimental.pallas import tpu as pltpu

F32 = jnp.float32
BF16 = jnp.bfloat16
MESH = pl.DeviceIdType.MESH

RMS_EPS = 1e-6
EIG_CLIP = 1e-4
ADAM_LR, ADAM_B1, ADAM_B2, ADAM_EPS, ADAM_WD, ADAM_STEP = 0.001, 0.9, 0.999, 1e-08, 0.01, 10

LANES = 128
SUBLANES = 8
VMEM_LIMIT = 56 * 1024 * 1024
S5_ROWS = 256
S5_CHUNK = 1024
S5_UNROLL = 4
ATT_BLOCK = 512
F_PAD = 256
POSTNORM_ROWS = 512
N_CHIPS = 4
N_DEV = 8

NN = (((1,), (0,)), ((), ()))
NT = (((1,), (1,)), ((), ()))
TN = (((0,), (0,)), ((), ()))
_DN = {"nn": NN, "nt": NT, "tn": TN}


def _cparams(**kw):
    return pltpu.CompilerParams(vmem_limit_bytes=VMEM_LIMIT, **kw)


def _pick(dim, target):
    best, t = None, LANES
    while t <= min(dim, target):
        if dim % t == 0:
            best = t
        t += LANES
    return best or dim


def _sigmoid(x):
    return 1.0 / (1.0 + jnp.exp(-x))


def _dot(a, b, dn):
    return lax.dot_general(a, b, dn, preferred_element_type=F32)


def _mm_raw(name, a, b, mode, grid, acc_shape, a_spec, b_spec, out_shapes, out_specs, epilogue,
            extra=(), extra_specs=(), a_fn=None, side=None):
    nk = grid[2]
    n_extra, n_out = len(extra), len(out_shapes)

    def body(*refs):
        a_ref, b_ref = refs[0], refs[1]
        extra_refs = refs[2:2 + n_extra]
        out_refs = refs[2 + n_extra:2 + n_extra + n_out]
        acc = refs[-1]
        k = pl.program_id(2)

        @pl.when(k == 0)
        def _():
            acc[...] = jnp.zeros_like(acc)

        av = a_ref[...]
        if a_fn is not None:
            av = a_fn(av.astype(F32))
        acc[...] += _dot(av.astype(BF16), b_ref[...].astype(BF16), _DN[mode])

        @pl.when(k == nk - 1)
        def _():
            epilogue(acc[...], extra_refs, out_refs)

    outs, side_outs = _hosted_call(body, side, name, grid, [a_spec, b_spec, *extra_specs], list(out_specs),
                                   list(out_shapes), [pltpu.VMEM(acc_shape, F32)], (a, b, *extra))
    return outs if side is None else (outs, side_outs)


def _mm(name, a, b, mode, out_shapes, out_specs, epilogue, extra=(), extra_specs=(),
        tm=512, tn=512, tk=512, a_fn=None):
    if mode == "nn":
        (m, kd), (_, n) = a.shape, b.shape
    elif mode == "nt":
        (m, kd), (n, _) = a.shape, b.shape
    else:
        (kd, m), (_, n) = a.shape, b.shape
    tm, tn, tk = _pick(m, tm), _pick(n, tn), _pick(kd, tk)
    if mode == "tn":
        a_spec = pl.BlockSpec((tk, tm), lambda i, j, k: (k, i))
    else:
        a_spec = pl.BlockSpec((tm, tk), lambda i, j, k: (i, k))
    if mode == "nt":
        b_spec = pl.BlockSpec((tn, tk), lambda i, j, k: (j, k))
    else:
        b_spec = pl.BlockSpec((tk, tn), lambda i, j, k: (k, j))
    res = _mm_raw(name, a, b, mode, (m // tm, n // tn, kd // tk), (tm, tn), a_spec, b_spec, out_shapes, out_specs,
                  epilogue, extra=extra, extra_specs=extra_specs, a_fn=a_fn)
    return res, (tm, tn, tk)


def _store(dtype):
    def epilogue(acc, extra_refs, out_refs):
        out_refs[0][...] = acc.astype(dtype)
    return epilogue


def _mm_sum(name, m, n, tm, tn, pairs, out_dtype, side=None):
    offs, total = [], 0
    for pr in pairs:
        offs.append(total)
        total += pr[6]
    n_p = len(pairs)

    def body(*refs):
        o_ref, acc = refs[2 * n_p], refs[2 * n_p + 1]
        k = pl.program_id(2)

        @pl.when(k == 0)
        def _():
            acc[...] = jnp.zeros_like(acc)

        for p_ in range(n_p):
            @pl.when((k >= offs[p_]) & (k < offs[p_] + pairs[p_][6]))
            def _(p_=p_):
                acc[...] += _dot(refs[2 * p_][...].astype(BF16), refs[2 * p_ + 1][...].astype(BF16), NT)

        @pl.when(k == total - 1)
        def _():
            o_ref[...] = acc[...].astype(out_dtype)

    in_specs, operands = [], []
    for (a, a_block, a_index, b, b_block, b_index, steps), off in zip(pairs, offs):
        local = lambda k, off=off, steps=steps: jnp.clip(k - off, 0, steps - 1)
        in_specs.append(pl.BlockSpec(a_block, lambda i, j, k, f=a_index, local=local: f(i, local(k))))
        in_specs.append(pl.BlockSpec(b_block, lambda i, j, k, f=b_index, local=local: f(j, local(k))))
        operands += [a, b]
    (out,), side_outs = _hosted_call(
        body, side, name, (m // tm, n // tn, total), in_specs, [pl.BlockSpec((tm, tn), lambda i, j, k: (i, j))],
        [jax.ShapeDtypeStruct((m, n), out_dtype)], [pltpu.VMEM((tm, tn), F32)], operands)
    return out if side is None else (out, side_outs)


class _SideJob:
    def __init__(self, arrays, out_shapes, aliases, n_sems, copies):
        self.arrays, self.out_shapes, self.aliases, self.n_sems, self.copies = arrays, out_shapes, aliases, n_sems, copies


def _hosted_call(body, side, name, grid, in_specs, out_specs, out_shape, scratch_shapes, operands):
    if side is None:
        outs = pl.pallas_call(body, name=name, grid=grid, in_specs=in_specs, out_specs=out_specs, out_shape=out_shape,
                              scratch_shapes=scratch_shapes, compiler_params=_cparams())(*operands)
        return outs, []
    n_in, n_out, ns_in, ns_out = len(in_specs), len(out_specs), len(side.arrays), len(side.out_shapes)

    def wrapped(*refs):
        main_in, side_in = refs[:n_in], refs[n_in:n_in + ns_in]
        rest = refs[n_in + ns_in:]
        main_out, side_out, rest = rest[:n_out], rest[n_out:n_out + ns_out], rest[n_out + ns_out:]
        scratch, send_sems, recv_sems = rest[:-2], rest[-2], rest[-1]
        first, last = None, None
        for axis, extent in enumerate(grid):
            at_start, at_end = pl.program_id(axis) == 0, pl.program_id(axis) == extent - 1
            first = at_start if first is None else first & at_start
            last = at_end if last is None else last & at_end

        @pl.when(first)
        def _():
            for cp in side.copies(side_in, side_out, send_sems, recv_sems):
                cp.start()

        body(*main_in, *main_out, *scratch)

        @pl.when(last)
        def _():
            for cp in side.copies(side_in, side_out, send_sems, recv_sems):
                cp.wait()

    hbm = pl.BlockSpec(memory_space=pl.ANY)
    outs = pl.pallas_call(
        wrapped, name=name, grid=grid, in_specs=list(in_specs) + [hbm] * ns_in,
        out_specs=list(out_specs) + [hbm] * ns_out, out_shape=list(out_shape) + list(side.out_shapes),
        scratch_shapes=list(scratch_shapes) + [pltpu.SemaphoreType.DMA((side.n_sems,))] * 2,
        input_output_aliases={n_in + i: n_out + o for i, o in side.aliases.items()},
        compiler_params=_cparams(),
    )(*operands, *side.arrays)
    return outs[:n_out], outs[n_out:]


def _ffn_up(name, h, wg, wu, side=None):
    s, d = h.shape
    nc, fs = wg.shape[0], wg.shape[2]
    tm, tk = _pick(s, 1024), _pick(d, 1024)
    nk = d // tk

    def body(h_ref, wg_ref, wu_ref, a_ref, b_ref, hid_ref, acc_g, acc_u):
        k = pl.program_id(2)

        @pl.when(k == 0)
        def _():
            acc_g[...] = jnp.zeros_like(acc_g)
            acc_u[...] = jnp.zeros_like(acc_u)

        hv = h_ref[...]
        acc_g[...] += _dot(hv, wg_ref[...], NN)
        acc_u[...] += _dot(hv, wu_ref[...], NN)

        @pl.when(k == nk - 1)
        def _():
            av, bv = acc_g[...], acc_u[...]
            a_ref[...] = av.astype(BF16)
            b_ref[...] = bv.astype(BF16)
            hid_ref[...] = (av * _sigmoid(av) * bv).astype(BF16)

    w_spec = pl.BlockSpec((None, tk, fs), lambda i, j, k: (j, k, 0))
    o_spec = pl.BlockSpec((None, tm, fs), lambda i, j, k: (j, i, 0))
    sh = jax.ShapeDtypeStruct((nc, s, fs), BF16)
    return _hosted_call(
        body, side, name, (s // tm, nc, nk), [pl.BlockSpec((tm, tk), lambda i, j, k: (i, k)), w_spec, w_spec],
        [o_spec] * 3, [sh] * 3, [pltpu.VMEM((tm, fs), F32), pltpu.VMEM((tm, fs), F32)], (h, wg, wu))


def _mm_plain(name, a, b, mode, out_dtype, add=None, a_fn=None, tm=512, tn=512, tk=512):
    if mode == "nn":
        m, n = a.shape[0], b.shape[1]
    elif mode == "nt":
        m, n = a.shape[0], b.shape[0]
    else:
        m, n = a.shape[1], b.shape[1]
    tm_, tn_ = _pick(m, tm), _pick(n, tn)
    spec = pl.BlockSpec((tm_, tn_), lambda i, j, k: (i, j))

    def epilogue(acc, extra_refs, out_refs):
        if add is not None:
            acc = acc + extra_refs[0][...]
        out_refs[0][...] = acc.astype(out_dtype)

    extra = () if add is None else (add,)
    (out,), _ = _mm(name, a, b, mode, [jax.ShapeDtypeStruct((m, n), out_dtype)], [spec], epilogue,
                    extra=extra, extra_specs=[spec] * len(extra), tm=tm, tn=tn, tk=tk, a_fn=a_fn)
    return out


def _row_tile(s, d):
    return _pick(s, max(SUBLANES, (1 << 20) // (4 * d)))


def _prenorm_fwd(name, x, g, scale, shift):
    s, d = x.shape
    tr = _row_tile(s, d)

    def body(x_ref, g_ref, sc_ref, sh_ref, h_ref):
        xv = x_ref[...]
        r = lax.rsqrt(jnp.mean(xv * xv, axis=-1, keepdims=True) + RMS_EPS)
        h_ref[...] = ((xv * r * g_ref[...]) * (1.0 + sc_ref[...]) + sh_ref[...]).astype(BF16)

    row = pl.BlockSpec((tr, d), lambda i: (i, 0))
    vec = pl.BlockSpec((1, d), lambda i: (0, 0))
    return pl.pallas_call(body, name=name, grid=(s // tr,), in_specs=[row, vec, vec, vec], out_specs=row,
                          out_shape=jax.ShapeDtypeStruct((s, d), BF16), compiler_params=_cparams())(x, g, scale, shift)


def _prenorm_bwd(name, dh, x, g, scale, dx_res):
    s, d = x.shape
    tr = _row_tile(s, d)

    def body(dh_ref, x_ref, g_ref, sc_ref, dxr_ref, dx_ref, sums_ref):
        @pl.when(pl.program_id(0) == 0)
        def _():
            sums_ref[...] = jnp.zeros_like(sums_ref)

        xv, dhv, gv = x_ref[...], dh_ref[...].astype(F32), g_ref[...]
        r = lax.rsqrt(jnp.mean(xv * xv, axis=-1, keepdims=True) + RMS_EPS)
        xhat = xv * r
        dxn = dhv * (1.0 + sc_ref[...])
        dxhat = dxn * gv
        dx = r * (dxhat - xhat * jnp.mean(dxhat * xhat, axis=-1, keepdims=True))
        dx_ref[...] = dxr_ref[...] + dx
        sums_ref[0:1, :] += jnp.sum(dhv * (xhat * gv), axis=0, keepdims=True)
        sums_ref[1:2, :] += jnp.sum(dhv, axis=0, keepdims=True)
        sums_ref[2:3, :] += jnp.sum(dxn * xhat, axis=0, keepdims=True)

    row = pl.BlockSpec((tr, d), lambda i: (i, 0))
    vec = pl.BlockSpec((1, d), lambda i: (0, 0))
    acc = pl.BlockSpec((SUBLANES, d), lambda i: (0, 0))
    return pl.pallas_call(
        body, name=name, grid=(s // tr,), in_specs=[row, row, vec, vec, row], out_specs=[row, acc],
        out_shape=[jax.ShapeDtypeStruct((s, d), F32), jax.ShapeDtypeStruct((SUBLANES, d), F32)],
        compiler_params=_cparams())(dh, x, g, scale, dx_res)


def _postnorm_bwd(name, dxn, y, g, gate):
    s, d = y.shape
    tr = _row_tile(s, d)

    def body(dx_ref, y_ref, g_ref, gt_ref, dy_ref, sums_ref):
        @pl.when(pl.program_id(0) == 0)
        def _():
            sums_ref[...] = jnp.zeros_like(sums_ref)

        yv, dxv, gv = y_ref[...], dx_ref[...], g_ref[...]
        r = lax.rsqrt(jnp.mean(yv * yv, axis=-1, keepdims=True) + RMS_EPS)
        yhat = yv * r
        dn = dxv * gt_ref[...]
        dyhat = dn * gv
        dy_ref[...] = (r * (dyhat - yhat * jnp.mean(dyhat * yhat, axis=-1, keepdims=True))).astype(BF16)
        sums_ref[0:1, :] += jnp.sum(dxv * (yhat * gv), axis=0, keepdims=True)
        sums_ref[1:2, :] += jnp.sum(dn * yhat, axis=0, keepdims=True)

    row = pl.BlockSpec((tr, d), lambda i: (i, 0))
    vec = pl.BlockSpec((1, d), lambda i: (0, 0))
    acc = pl.BlockSpec((SUBLANES, d), lambda i: (0, 0))
    return pl.pallas_call(
        body, name=name, grid=(s // tr,), in_specs=[row, row, vec, vec], out_specs=[row, acc],
        out_shape=[jax.ShapeDtypeStruct((s, d), BF16), jax.ShapeDtypeStruct((SUBLANES, d), F32)],
        compiler_params=_cparams())(dxn, y, g, gate)


def _loss_grad(name, y, target):
    s, d = y.shape
    tr = _row_tile(s, d)

    def body(y_ref, t_ref, dy_ref, loss_ref):
        @pl.when(pl.program_id(0) == 0)
        def _():
            loss_ref[...] = jnp.zeros_like(loss_ref)

        err = y_ref[...] - t_ref[...]
        dy_ref[...] = err * (1.0 / d)
        part = jnp.sum(jnp.sum(err * err, axis=-1, keepdims=True), axis=0, keepdims=True) * (0.5 / d)
        loss_ref[...] += jnp.broadcast_to(part, loss_ref.shape)

    row = pl.BlockSpec((tr, d), lambda i: (i, 0))
    acc = pl.BlockSpec((SUBLANES, LANES), lambda i: (0, 0))
    return pl.pallas_call(
        body, name=name, grid=(s // tr,), in_specs=[row, row], out_specs=[row, acc],
        out_shape=[jax.ShapeDtypeStruct((s, d), F32), jax.ShapeDtypeStruct((SUBLANES, LANES), F32)],
        compiler_params=_cparams())(y, target)


def _gelu(y):
    c = math.sqrt(2.0 / math.pi)
    return 0.5 * y * (1.0 + jnp.tanh(c * (y + 0.044715 * (y * y * y))))


def _gelu_grad(y):
    c = math.sqrt(2.0 / math.pi)
    th = jnp.tanh(c * (y + 0.044715 * (y * y * y)))
    return 0.5 * (1.0 + th) + 0.5 * y * (1.0 - th * th) * c * (1.0 + 3.0 * 0.044715 * (y * y))


def _cmul_add(br, bi, ar, ai, xr, xi):
    return br + ar * xr - ai * xi, bi + ar * xi + ai * xr


def _scan_rows(x_ref, row0, n_steps, ns2, pow_ref, tab_ref, carry_ref, reverse, fold=None):
    assert n_steps % SUBLANES == 0
    wc = min(S5_CHUNK, ns2)
    sub = lax.broadcasted_iota(jnp.int32, (SUBLANES, wc), 0)
    unroll = S5_UNROLL if n_steps % S5_UNROLL == 0 else 1
    for c0 in range(0, ns2, wc):
        re = slice(c0, c0 + wc)
        im = slice(ns2 + c0, ns2 + c0 + wc)
        first_power = slice(n_steps - 1, n_steps) if reverse else slice(0, 1)
        ar = jnp.broadcast_to(pow_ref[first_power, re], (SUBLANES, wc))
        ai = jnp.broadcast_to(pow_ref[first_power, im], (SUBLANES, wc))
        rows = lambda r: pl.ds(pl.multiple_of(row0 + r * SUBLANES, SUBLANES), SUBLANES)
        step_of = lambda i: (n_steps - 1 - i) if reverse else i

        def local(i, carry, re=re, im=im, ar=ar, ai=ai):
            for u in range(unroll):
                r = step_of(i * unroll + u)
                carry = _cmul_add(x_ref[rows(r), re], x_ref[rows(r), im], ar, ai, *carry)
                x_ref[rows(r), re], x_ref[rows(r), im] = carry
            return carry

        zero = jnp.zeros((SUBLANES, wc), F32)
        lr, li = lax.fori_loop(0, n_steps // unroll, local, (zero, zero))

        tabs = [tab_ref[k, :, re] for k in range(8)]
        for lvl, k in enumerate((1, 2, 4)):
            sh = (SUBLANES - k) if reverse else k
            lr, li = _cmul_add(lr, li, tabs[2 * lvl], tabs[2 * lvl + 1], pltpu.roll(lr, sh, 0), pltpu.roll(li, sh, 0))
        cr, ci = carry_ref[0:1, re], carry_ref[0:1, im]
        lr, li = _cmul_add(lr, li, tabs[6], tabs[7], cr, ci)
        edge, away, last = (SUBLANES - 1, SUBLANES - 1, 0) if reverse else (0, 1, SUBLANES - 1)
        carry_ref[0:1, re] = lr[last:last + 1, :]
        carry_ref[0:1, im] = li[last:last + 1, :]
        er = jnp.where(sub == edge, cr, pltpu.roll(lr, away, 0))
        ei = jnp.where(sub == edge, ci, pltpu.roll(li, away, 0))

        def fix(j, acc, re=re, im=im, er=er, ei=ei, c0=c0):
            base = pl.ds(pl.multiple_of(j * SUBLANES, SUBLANES), SUBLANES)
            pw_r, pw_i = pow_ref[base, re], pow_ref[base, im]
            for i in range(SUBLANES):
                r = j * SUBLANES + i
                xr, xi = _cmul_add(x_ref[rows(r), re], x_ref[rows(r), im], pw_r[i:i + 1, :], pw_i[i:i + 1, :], er, ei)
                x_ref[rows(r), re], x_ref[rows(r), im] = xr, xi
                if fold is not None:
                    acc = fold(c0, r, xr, xi, acc)
            return acc

        acc = lax.fori_loop(0, n_steps // SUBLANES, fix, (zero, zero) if fold is not None else 0)
        if fold is not None:
            fold(c0, None, None, None, acc)


def _s5_fwd(name, u, b_blk, c_blk, a_f, tab_f, dskip, w_glu, b_glu):
    s, w = u.shape[0], w_glu.shape[0]
    nkb = w // LANES
    ns2 = b_blk.shape[2] // 2 * nkb
    half = ns2 // nkb
    t = min(S5_ROWS, s)
    nblk = s // t

    def body(u_ref, b_ref, c_ref, a_ref, tab_ref, ds_ref, wg_ref, bg_ref, y_ref, ys_ref, cs_ref, xs, carry):
        @pl.when(pl.program_id(0) == 0)
        def _():
            carry[...] = jnp.zeros_like(carry)

        cs_ref[0] = carry[...]
        for kb in range(nkb):
            bu = _dot(u_ref[:, kb * LANES:(kb + 1) * LANES], b_ref[kb], NN)
            xs[:, kb * half:(kb + 1) * half] = bu[:, :half]
            xs[:, ns2 + kb * half:ns2 + (kb + 1) * half] = bu[:, half:]
        _scan_rows(xs, 0, t // SUBLANES, ns2, a_ref, tab_ref, carry, reverse=False)
        for kb in range(nkb):
            cols = slice(kb * LANES, (kb + 1) * LANES)
            yk = _dot(xs[:, kb * half:(kb + 1) * half].astype(BF16), c_ref[kb, :half, :], NN)
            yk += _dot(xs[:, ns2 + kb * half:ns2 + (kb + 1) * half].astype(BF16), c_ref[kb, half:, :], NN)
            y_ref[:, cols] = yk + ds_ref[:, cols] * u_ref[:, cols].astype(F32)
        z = _gelu(y_ref[...])
        gate = _sigmoid(_dot(z.astype(BF16), wg_ref[...], NN) + bg_ref[...])
        ys_ref[...] = (z * gate).astype(BF16)

    row = pl.BlockSpec((t, w), lambda i: (i, 0))
    full = lambda shape: pl.BlockSpec(shape, lambda i: (0,) * len(shape))
    return pl.pallas_call(
        body, name=name, grid=(nblk,),
        in_specs=[row, full(b_blk.shape), full(c_blk.shape), full(a_f.shape), full(tab_f.shape), full(dskip.shape),
                  full(w_glu.shape), full(b_glu.shape)],
        out_specs=[row, row, pl.BlockSpec((1, 1, 2 * ns2), lambda i: (i, 0, 0))],
        out_shape=[jax.ShapeDtypeStruct((s, w), F32), jax.ShapeDtypeStruct((s, w), BF16),
                   jax.ShapeDtypeStruct((nblk, 1, 2 * ns2), F32)],
        scratch_shapes=[pltpu.VMEM((t, 2 * ns2), F32), pltpu.VMEM((1, 2 * ns2), F32)],
        compiler_params=_cparams(),
    )(u, b_blk, c_blk, a_f, tab_f, dskip, w_glu, b_glu)


def _s5_bwd(name, u, dys, y, carries, b_blk, c_blk, a_f, a_r, tab_f, tab_r, dskip, w_glu, b_glu):
    s, w = u.shape[0], w_glu.shape[0]
    nkb = w // LANES
    ns2 = b_blk.shape[2] // 2 * nkb
    half = ns2 // nkb
    t = min(S5_ROWS, s)
    nblk = s // t
    ng = t // SUBLANES

    def body(u_ref, dys_ref, y_ref, cs_ref, b_ref, c_ref, af_ref, ar_ref, tabf_ref, tabr_ref, ds_ref, wg_ref, bg_ref,
             du_ref, db_ref, dc_ref, da_ref, dwg_ref, vec_ref, xs, gs, dyv, fcarry, gcarry):
        @pl.when(pl.program_id(0) == 0)
        def _():
            db_ref[...] = jnp.zeros_like(db_ref)
            dc_ref[...] = jnp.zeros_like(dc_ref)
            da_ref[...] = jnp.zeros_like(da_ref)
            dwg_ref[...] = jnp.zeros_like(dwg_ref)
            vec_ref[...] = jnp.zeros_like(vec_ref)
            gcarry[...] = jnp.zeros_like(gcarry)

        yv = y_ref[...]
        z = _gelu(yv)
        zb = z.astype(BF16)
        gate = _sigmoid(_dot(zb, wg_ref[...], NN) + bg_ref[...])
        dout = dys_ref[...].astype(F32)
        dt = dout * z * gate * (1.0 - gate)
        dtb = dt.astype(BF16)
        dz = dout * gate + _dot(dtb, wg_ref[...], NT)
        dy = dz * _gelu_grad(yv)
        dyv[...] = dy
        dwg_ref[...] += _dot(zb, dtb, TN)
        vec_ref[0:1, :] += jnp.sum(dt, axis=0, keepdims=True)
        vec_ref[1:2, :] += jnp.sum(dy * u_ref[...].astype(F32), axis=0, keepdims=True)

        fcarry[...] = cs_ref[0]
        xs[0:SUBLANES, :] = jnp.broadcast_to(cs_ref[0], (SUBLANES, 2 * ns2))
        for kb in range(nkb):
            bu = _dot(u_ref[:, kb * LANES:(kb + 1) * LANES], b_ref[kb], NN)
            xs[SUBLANES:, kb * half:(kb + 1) * half] = bu[:, :half]
            xs[SUBLANES:, ns2 + kb * half:ns2 + (kb + 1) * half] = bu[:, half:]
        _scan_rows(xs, SUBLANES, ng, ns2, af_ref, tabf_ref, fcarry, reverse=False)
        first_segment = lax.broadcasted_iota(jnp.int32, (SUBLANES, 2 * ns2), 0) == 0
        xs[0:SUBLANES, :] = jnp.where(first_segment, xs[0:SUBLANES, :], pltpu.roll(xs[t:t + SUBLANES, :], 1, 0))

        for kb in range(nkb):
            dyk = dyv[:, kb * LANES:(kb + 1) * LANES].astype(BF16)
            re = slice(kb * half, (kb + 1) * half)
            im = slice(ns2 + kb * half, ns2 + (kb + 1) * half)
            gs[:, re] = _dot(dyk, c_ref[kb, :half, :], NT)
            gs[:, im] = _dot(dyk, c_ref[kb, half:, :], NT)
            dc_ref[kb, :half, :] += _dot(xs[SUBLANES:, re].astype(BF16), dyk, TN)
            dc_ref[kb, half:, :] += _dot(xs[SUBLANES:, im].astype(BF16), dyk, TN)

        def fold(c0, r, gr, gi, acc):
            wc = min(S5_CHUNK, ns2)
            re = slice(c0, c0 + wc)
            im = slice(ns2 + c0, ns2 + c0 + wc)
            if r is None:
                da_ref[:, re] += acc[0]
                da_ref[:, im] += acc[1]
                return acc
            before = pl.ds(pl.multiple_of(r * SUBLANES, SUBLANES), SUBLANES)
            xpr, xpi = xs[before, re], xs[before, im]
            return acc[0] + gr * xpr + gi * xpi, acc[1] - gr * xpi + gi * xpr

        _scan_rows(gs, 0, ng, ns2, ar_ref, tabr_ref, gcarry, reverse=True, fold=fold)

        for kb in range(nkb):
            cols = slice(kb * LANES, (kb + 1) * LANES)
            re = slice(kb * half, (kb + 1) * half)
            im = slice(ns2 + kb * half, ns2 + (kb + 1) * half)
            uk = u_ref[:, cols]
            gr = gs[:, re].astype(BF16)
            gi = gs[:, im].astype(BF16)
            db_ref[kb, :, :half] += _dot(uk, gr, TN)
            db_ref[kb, :, half:] += _dot(uk, gi, TN)
            duk = _dot(gr, b_ref[kb, :, :half], NT) + _dot(gi, b_ref[kb, :, half:], NT)
            du_ref[:, cols] = (duk + ds_ref[:, cols] * dyv[:, cols]).astype(BF16)

    rev = lambda i: (nblk - 1 - i, 0)
    row = pl.BlockSpec((t, w), rev)
    full = lambda shape: pl.BlockSpec(shape, lambda i: (0,) * len(shape))
    return pl.pallas_call(
        body, name=name, grid=(nblk,),
        in_specs=[row, row, row, pl.BlockSpec((1, 1, 2 * ns2), lambda i: (nblk - 1 - i, 0, 0)),
                  full(b_blk.shape), full(c_blk.shape), full(a_f.shape), full(a_r.shape), full(tab_f.shape),
                  full(tab_r.shape), full(dskip.shape), full(w_glu.shape), full(b_glu.shape)],
        out_specs=[row, full(b_blk.shape), full(c_blk.shape), full((SUBLANES, 2 * ns2)), full((w, w)),
                   full((SUBLANES, w))],
        out_shape=[jax.ShapeDtypeStruct((s, w), BF16), jax.ShapeDtypeStruct(b_blk.shape, F32),
                   jax.ShapeDtypeStruct(c_blk.shape, F32), jax.ShapeDtypeStruct((SUBLANES, 2 * ns2), F32),
                   jax.ShapeDtypeStruct((w, w), F32), jax.ShapeDtypeStruct((SUBLANES, w), F32)],
        scratch_shapes=[pltpu.VMEM((t + SUBLANES, 2 * ns2), F32), pltpu.VMEM((t, 2 * ns2), F32),
                        pltpu.VMEM((t, w), F32), pltpu.VMEM((1, 2 * ns2), F32), pltpu.VMEM((1, 2 * ns2), F32)],
        compiler_params=_cparams(),
    )(u, dys, y, carries, b_blk, c_blk, a_f, a_r, tab_f, tab_r, dskip, w_glu, b_glu)


def _log_sigmoid(x):
    return jnp.minimum(x, 0.0) - jnp.log(1.0 + jnp.exp(-jnp.abs(x)))


def _cum_fwd(name, f_t, b_f):
    h, s = f_t.shape
    tc = _pick(s, 512)
    nb = s // tc

    def body(f_ref, b_ref, c_ref, carry):
        @pl.when(pl.program_id(0) == 0)
        def _():
            carry[...] = jnp.zeros_like(carry)

        lf = _log_sigmoid(f_ref[...] + b_ref[...])
        upper = (lax.broadcasted_iota(jnp.int32, (tc, tc), 0) <= lax.broadcasted_iota(jnp.int32, (tc, tc), 1))
        cum = lax.dot_general(lf, upper.astype(F32), NN, precision=lax.Precision.HIGHEST,
                              preferred_element_type=F32) + carry[...]
        c_ref[...] = cum
        carry[...] += jnp.sum(lf, axis=1, keepdims=True)

    blk = pl.BlockSpec((h, tc), lambda i: (0, i))
    return pl.pallas_call(body, name=name, grid=(nb,), in_specs=[blk, pl.BlockSpec((h, 1), lambda i: (0, 0))],
                          out_specs=blk, out_shape=jax.ShapeDtypeStruct((h, s), F32),
                          scratch_shapes=[pltpu.VMEM((h, 1), F32)], compiler_params=_cparams())(f_t, b_f)


def _cum_bwd(name, dcq, dck, f_t, b_f):
    h, s = f_t.shape
    tc = _pick(s, 512)
    nb = s // tc

    def body(dcq_ref, dck_ref, f_ref, b_ref, df_ref, db_ref, carry):
        @pl.when(pl.program_id(0) == 0)
        def _():
            carry[...] = jnp.zeros_like(carry)
            db_ref[...] = jnp.zeros_like(db_ref)

        dc = dcq_ref[...] + dck_ref[...]
        lower = (lax.broadcasted_iota(jnp.int32, (tc, tc), 0) >= lax.broadcasted_iota(jnp.int32, (tc, tc), 1))
        dlf = lax.dot_general(dc, lower.astype(F32), NN, precision=lax.Precision.HIGHEST,
                              preferred_element_type=F32) + carry[...]
        carry[...] += jnp.sum(dc, axis=1, keepdims=True)
        df = dlf * _sigmoid(-(f_ref[...] + b_ref[...]))
        df_ref[...] = df
        db_ref[...] += jnp.broadcast_to(jnp.sum(df, axis=1, keepdims=True), db_ref.shape)

    blk = pl.BlockSpec((h, tc), lambda i: (0, nb - 1 - i))
    return pl.pallas_call(
        body, name=name, grid=(nb,), in_specs=[blk, blk, blk, pl.BlockSpec((h, 1), lambda i: (0, 0))],
        out_specs=[blk, pl.BlockSpec((h, LANES), lambda i: (0, 0))],
        out_shape=[jax.ShapeDtypeStruct((h, s), F32), jax.ShapeDtypeStruct((h, LANES), F32)],
        scratch_shapes=[pltpu.VMEM((h, 1), F32)], compiler_params=_cparams())(dcq, dck, f_t, b_f)


def _attn_fwd(name, qkv, q_blk, k_blk, v_blk, n_pairs, ck, side=None):
    s = qkv.shape[0]
    dh = LANES // 2
    t = min(ATT_BLOCK, s)
    nq = s // t
    scale = dh ** -0.5

    def body(q_ref, k_ref, v_ref, ck_ref, o_ref, lse_ref, m_s, acc_s):
        i = pl.program_id(1)
        low = lax.broadcasted_iota(jnp.int32, (1, LANES), 1) < dh
        qs = (q_ref[...].astype(F32) * scale).astype(BF16)
        zero = jnp.zeros_like(qs)
        qh = (jnp.where(low, qs, zero), jnp.where(low, zero, qs))
        m_s[...] = jnp.full(m_s.shape, -1e30, F32)
        acc_s[...] = jnp.zeros_like(acc_s)
        causal = (lax.broadcasted_iota(jnp.int32, (t, t), 1) <= lax.broadcasted_iota(jnp.int32, (t, t), 0))

        def step(j, diagonal):
            r0 = pl.multiple_of(j * t, t)
            kj = k_ref[pl.ds(r0, t), :]
            vj = v_ref[pl.ds(r0, t), :]
            one = jnp.ones_like(vj)
            vh = (jnp.where(low, vj, one), jnp.where(low, one, vj))
            for hd in range(2):
                sc = _dot(qh[hd], kj, NT) - ck_ref[hd, j]
                if diagonal:
                    sc = jnp.where(causal, sc, -1e30)
                m_old = m_s[hd]
                m_new = jnp.maximum(m_old, jnp.max(sc, axis=1, keepdims=True))
                p = jnp.exp(sc - m_new)
                acc_s[hd] = jnp.exp(m_old - m_new) * acc_s[hd] + _dot(p.astype(BF16), vh[hd], NN)
                m_s[hd] = m_new

        def full(j, _):
            step(j, False)
            return 0

        lax.fori_loop(0, i, full, 0)
        step(i, True)
        a0, a1 = acc_s[0], acc_s[1]
        o_ref[...] = jnp.where(low, a0 / pltpu.roll(a0, dh, 1), a1 / pltpu.roll(a1, dh, 1)).astype(BF16)
        lse_ref[0] = m_s[0] + jnp.log(a0[:, dh:dh + 1])
        lse_ref[1] = m_s[1] + jnp.log(a1[:, 0:1])

    return _hosted_call(
        body, side, name, (n_pairs, nq),
        [pl.BlockSpec((t, LANES), lambda hp, i: (i, q_blk + hp)),
         pl.BlockSpec((s, LANES), lambda hp, i: (0, k_blk + hp)),
         pl.BlockSpec((s, LANES), lambda hp, i: (0, v_blk + hp)),
         pl.BlockSpec((2, nq, 1, t), lambda hp, i: (hp, 0, 0, 0))],
        [pl.BlockSpec((t, LANES), lambda hp, i: (i, hp)), pl.BlockSpec((2, t, 1), lambda hp, i: (hp, i, 0))],
        [jax.ShapeDtypeStruct((s, LANES * n_pairs), BF16), jax.ShapeDtypeStruct((2 * n_pairs, s, 1), F32)],
        [pltpu.VMEM((2, t, 1), F32), pltpu.VMEM((2, t, LANES), F32)], (qkv, qkv, qkv, ck))


def _attn_bwd(name, qkv, q_blk, k_blk, v_blk, n_pairs, o, do, lse_rows, ck_cols, side=None):
    s = qkv.shape[0]
    dh = LANES // 2
    t = min(ATT_BLOCK, s)
    nk = s // t
    scale = dh ** -0.5

    def body(q_ref, k_ref, v_ref, o_ref, do_ref, lse_ref, ck_ref,
             dq_ref, dk_ref, dv_ref, dcq_ref, dck_ref, delta, dqt, dk_acc, dv_acc):
        j = pl.program_id(1)
        low = lax.broadcasted_iota(jnp.int32, (1, LANES), 1) < dh
        low_rows = lax.broadcasted_iota(jnp.int32, (LANES, 1), 0) < dh

        @pl.when(j == 0)
        def _():
            dqt[...] = jnp.zeros_like(dqt)
            sel = (jnp.broadcast_to(low, (SUBLANES, LANES)).astype(F32), jnp.broadcast_to(~low, (SUBLANES, LANES)).astype(F32))

            def fill(i, _):
                r0 = pl.multiple_of(i * t, t)
                prod = do_ref[pl.ds(r0, t), :].astype(F32) * o_ref[pl.ds(r0, t), :].astype(F32)
                for hd in range(2):
                    delta[hd, i] = lax.dot_general(sel[hd], prod, NT, precision=lax.Precision.HIGHEST,
                                                   preferred_element_type=F32)
                return 0

            lax.fori_loop(0, nk, fill, 0)

        kj, vj = k_ref[...], v_ref[...]
        zero, one = jnp.zeros_like(kj), jnp.ones_like(kj)
        kh = (jnp.where(low, kj, zero), jnp.where(low, zero, kj))
        vh = (jnp.where(low, vj, zero), jnp.where(low, zero, vj))
        kjt = kj.astype(F32).T.astype(BF16)
        one_t = jnp.ones_like(kjt)
        kht = (jnp.where(low_rows, kjt, one_t), jnp.where(low_rows, one_t, kjt))
        dk_acc[...] = jnp.zeros_like(dk_acc)
        dv_acc[...] = jnp.zeros_like(dv_acc)
        causal_t = (lax.broadcasted_iota(jnp.int32, (t, t), 0) <= lax.broadcasted_iota(jnp.int32, (t, t), 1))

        def step(i, diagonal):
            r0 = pl.multiple_of(i * t, t)
            qi = (q_ref[pl.ds(r0, t), :].astype(F32) * scale).astype(BF16)
            doi = do_ref[pl.ds(r0, t), :]
            qone, dzero = jnp.ones_like(qi), jnp.zeros_like(doi)
            qsel = (jnp.where(low, qi, qone), jnp.where(low, qone, qi))
            dosel = (jnp.where(low, doi, dzero), jnp.where(low, dzero, doi))
            for hd in range(2):
                st = _dot(kh[hd], qi, NT) - ck_ref[hd] - lse_ref[hd, i]
                pt = jnp.exp(st)
                if diagonal:
                    pt = jnp.where(causal_t, pt, 0.0)
                dst = pt * (_dot(vh[hd], doi, NT) - delta[hd, i, 0:1, :])
                dsb = dst.astype(BF16)
                dv_acc[...] += _dot(pt.astype(BF16), dosel[hd], NN)
                dk_acc[hd] += _dot(dsb, qsel[hd], NN)
                dqt[hd, i] += _dot(kht[hd], dsb, NN)

        step(j, True)

        def rest(i, _):
            step(i, False)
            return 0

        lax.fori_loop(j + 1, nk, rest, 0)
        dk_ref[...] = jnp.where(low, dk_acc[0], dk_acc[1]).astype(BF16)
        dv_ref[...] = dv_acc[...].astype(BF16)
        dck_ref[0] = -dk_acc[0][:, dh:dh + 1]
        dck_ref[1] = -dk_acc[1][:, 0:1]

        @pl.when(j == nk - 1)
        def _():
            def emit(i, _):
                r0 = pl.multiple_of(i * t, t)
                d0, d1 = dqt[0, i], dqt[1, i]
                dq_ref[pl.ds(r0, t), :] = (jnp.where(low_rows, d0, d1) * scale).T.astype(BF16)
                dcq_ref[0, i] = d0[dh:dh + 1, :]
                dcq_ref[1, i] = d1[0:1, :]
                return 0

            lax.fori_loop(0, nk, emit, 0)

    col_blk = lambda base: pl.BlockSpec((t, LANES), lambda hp, j: (j, base + hp))
    col_all = lambda base: pl.BlockSpec((s, LANES), lambda hp, j: (0, base + hp))
    rows_all = pl.BlockSpec((2, nk, 1, t), lambda hp, j: (hp, 0, 0, 0))
    return _hosted_call(
        body, side, name, (n_pairs, nk),
        [col_all(q_blk), col_blk(k_blk), col_blk(v_blk), col_all(0), col_all(0), rows_all,
         pl.BlockSpec((2, t, 1), lambda hp, j: (hp, j, 0))],
        [col_all(0), col_blk(0), col_blk(0), rows_all, pl.BlockSpec((2, t, 1), lambda hp, j: (hp, j, 0))],
        [jax.ShapeDtypeStruct((s, LANES * n_pairs), BF16), jax.ShapeDtypeStruct((s, LANES * n_pairs), BF16),
         jax.ShapeDtypeStruct((s, LANES * n_pairs), BF16), jax.ShapeDtypeStruct((2 * n_pairs, nk, 1, t), F32),
         jax.ShapeDtypeStruct((2 * n_pairs, s, 1), F32)],
        [pltpu.VMEM((2, nk, SUBLANES, t), F32), pltpu.VMEM((2, nk, LANES, t), F32),
         pltpu.VMEM((2, t, LANES), F32), pltpu.VMEM((t, LANES), F32)],
        (qkv, qkv, qkv, o, do, lse_rows, ck_cols))


def _adamw(name, w, g, m, v):
    r, c = w.shape
    tr = _pick8(r, max(SUBLANES, (1 << 20) // (4 * c)))

    def body(w_ref, g_ref, m_ref, v_ref, d_ref, mo_ref, vo_ref):
        gv = g_ref[...]
        m2 = ADAM_B1 * m_ref[...] + (1.0 - ADAM_B1) * gv
        v2 = ADAM_B2 * v_ref[...] + (1.0 - ADAM_B2) * (gv * gv)
        m_hat = m2 / (1.0 - ADAM_B1 ** ADAM_STEP)
        v_hat = v2 / (1.0 - ADAM_B2 ** ADAM_STEP)
        d_ref[...] = -ADAM_LR * (m_hat / (jnp.sqrt(v_hat) + ADAM_EPS) + ADAM_WD * w_ref[...])
        mo_ref[...] = m2
        vo_ref[...] = v2

    blk = pl.BlockSpec((tr, c), lambda i: (i, 0))
    sh = jax.ShapeDtypeStruct((r, c), F32)
    return pl.pallas_call(body, name=name, grid=(r // tr,), in_specs=[blk] * 4, out_specs=[blk] * 3,
                          out_shape=[sh, sh, sh], compiler_params=_cparams())(w, g, m, v)


def _pick8(dim, target, mult=SUBLANES):
    best, t = None, mult
    while t <= min(dim, target):
        if dim % t == 0:
            best = t
        t += mult
    return best or dim


BF16_ROWS = 16


def _sum_blocks(name, x, out_dtype):
    n, r, c = x.shape
    tr = _pick8(r, max(BF16_ROWS, (1 << 19) // (4 * c)), BF16_ROWS)

    def body(x_ref, o_ref):
        acc = x_ref[0].astype(F32)
        for i in range(1, n):
            acc = acc + x_ref[i].astype(F32)
        o_ref[...] = acc.astype(out_dtype)

    return pl.pallas_call(body, name=name, grid=(r // tr,),
                          in_specs=[pl.BlockSpec((n, tr, c), lambda i: (0, i, 0))],
                          out_specs=pl.BlockSpec((tr, c), lambda i: (i, 0)),
                          out_shape=jax.ShapeDtypeStruct((r, c), out_dtype), compiler_params=_cparams())(x)


def _all_gather(name, x_shard):
    m_per, n = x_shard.shape

    def body(x_ref, out_ref, send_sems, recv_sems):
        x, y, c = lax.axis_index("x"), lax.axis_index("y"), lax.axis_index("c")
        me, sibling = (x, y, c), (x, y, 1 - c)
        chips = [(1 - x, y), (x, 1 - y), (1 - x, 1 - y)]

        def rows(px, py, pc):
            return out_ref.at[pl.ds((4 * px + 2 * py + pc) * m_per, m_per), :]

        def copy(k, block, to, src=None):
            return pltpu.make_async_remote_copy(
                src_ref=rows(*block) if src is None else src, dst_ref=rows(*block),
                send_sem=send_sems.at[k], recv_sem=recv_sems.at[k], device_id=to, device_id_type=MESH)

        first = [copy(0, me, sibling, src=x_ref)]
        first += [copy(1 + j, me, (*chip, c), src=x_ref) for j, chip in enumerate(chips)]
        for cp in first:
            cp.start()
        passed = [copy(4 + j, (*chip, c), sibling) for j, chip in enumerate(chips)]
        for j, chip in enumerate(chips):
            copy(1 + j, (*chip, c), me).wait_recv()
            passed[j].start()
        copy(0, sibling, me).wait_recv()
        for j, chip in enumerate(chips):
            copy(4 + j, (*chip, 1 - c), me).wait_recv()
        for cp in first + passed:
            cp.wait_send()

    out = pl.pallas_call(
        body, name=name, out_shape=jax.ShapeDtypeStruct((N_DEV * m_per, n), x_shard.dtype),
        in_specs=[pl.BlockSpec(memory_space=pl.ANY)], out_specs=pl.BlockSpec(memory_space=pl.ANY),
        scratch_shapes=[pltpu.SemaphoreType.DMA((7,)), pltpu.SemaphoreType.DMA((7,))],
    )(x_shard)
    my_dev = 4 * lax.axis_index("x") + 2 * lax.axis_index("y") + lax.axis_index("c")
    return lax.dynamic_update_slice(out, x_shard, (my_dev * m_per, 0))


def _put_own(out, own, index):
    start = tuple(index) + (0,) * own.ndim
    return lax.dynamic_update_slice(out, own.reshape((1,) * len(index) + own.shape), start)


def _gather_copies(stage, ins, outs, send_sems, recv_sems):
    x, y, c = lax.axis_index("x"), lax.axis_index("y"), lax.axis_index("c")
    my_chip = 2 * x + y
    copies = []
    for w, out in enumerate(outs):
        half = out.shape[1] // 2
        rows = pl.ds(c * half, half)
        for k, (cx, cy) in enumerate([(1 - x, y), (x, 1 - y), (1 - x, 1 - y)]):
            if stage == 0:
                src, dst, to = ins[w].at[rows], out.at[my_chip, rows], (cx, cy, c)
            else:
                src = dst = out.at[2 * cx + cy, rows]
                to = (x, y, 1 - c)
            copies.append(pltpu.make_async_remote_copy(
                src_ref=src, dst_ref=dst, send_sem=send_sems.at[3 * w + k], recv_sem=recv_sems.at[3 * w + k],
                device_id=to, device_id_type=MESH))
    return copies


def _gathered_shapes(shards):
    return [jax.ShapeDtypeStruct((N_CHIPS,) + s.shape, s.dtype) for s in shards]


def _put_own_slabs(gathered, shards):
    my_chip = 2 * lax.axis_index("x") + lax.axis_index("y")
    return [_put_own(o, s, (my_chip,)) for o, s in zip(gathered, shards)]


def _gather_layer(name, shards):
    n_w = len(shards)

    def body(*refs):
        ins, outs = refs[:n_w], refs[n_w:2 * n_w]
        for stage in (0, 1):
            copies = _gather_copies(stage, ins, outs, refs[2 * n_w + 2 * stage], refs[2 * n_w + 2 * stage + 1])
            for cp in copies:
                cp.start()
            for cp in copies:
                cp.wait()

    outs = pl.pallas_call(
        body, name=name, out_shape=_gathered_shapes(shards),
        in_specs=[pl.BlockSpec(memory_space=pl.ANY)] * n_w, out_specs=[pl.BlockSpec(memory_space=pl.ANY)] * n_w,
        scratch_shapes=[pltpu.SemaphoreType.DMA((3 * n_w,))] * 4,
    )(*shards)
    return _put_own_slabs(outs, shards)


def _gather_side_jobs(shards):
    n_w = len(shards)
    between_chips = _SideJob(list(shards), _gathered_shapes(shards), {}, 3 * n_w,
                             lambda ins, outs, send, recv: _gather_copies(0, ins, outs, send, recv))
    between_cores = lambda partial: _SideJob(list(partial), _gathered_shapes(shards), {w: w for w in range(n_w)}, 3 * n_w,
                                             lambda ins, outs, send, recv: _gather_copies(1, ins, outs, send, recv))
    return between_chips, between_cores


def _run_job(name, job):
    n_in, n_out = len(job.arrays), len(job.out_shapes)

    def body(*refs):
        copies = job.copies(refs[:n_in], refs[n_in:n_in + n_out], refs[n_in + n_out], refs[n_in + n_out + 1])
        for cp in copies:
            cp.start()
        for cp in copies:
            cp.wait()

    hbm = pl.BlockSpec(memory_space=pl.ANY)
    return pl.pallas_call(
        body, name=name, out_shape=list(job.out_shapes), in_specs=[hbm] * n_in, out_specs=[hbm] * n_out,
        scratch_shapes=[pltpu.SemaphoreType.DMA((job.n_sems,))] * 2, input_output_aliases=dict(job.aliases),
    )(*job.arrays)


def _swap_job(grads):
    def copies(ins, outs, send_sems, recv_sems):
        x, y, c = lax.axis_index("x"), lax.axis_index("y"), lax.axis_index("c")
        return [pltpu.make_async_remote_copy(
            src_ref=g.at[:, pl.ds((1 - c) * (g.shape[1] // 2), g.shape[1] // 2)], dst_ref=outs[w],
            send_sem=send_sems.at[w], recv_sem=recv_sems.at[w], device_id=(x, y, 1 - c), device_id_type=MESH)
            for w, g in enumerate(ins)]

    shapes = [jax.ShapeDtypeStruct((g.shape[0], g.shape[1] // 2, g.shape[2]), g.dtype) for g in grads]
    return _SideJob(list(grads), shapes, {}, len(grads), copies)


def _exchange_job(parts):
    def copies(ins, outs, send_sems, recv_sems):
        x, y, c = lax.axis_index("x"), lax.axis_index("y"), lax.axis_index("c")
        return [pltpu.make_async_remote_copy(
            src_ref=ins[w].at[2 * cx + cy], dst_ref=outs[w].at[2 * x + y], send_sem=send_sems.at[3 * w + k],
            recv_sem=recv_sems.at[3 * w + k], device_id=(cx, cy, c), device_id_type=MESH)
            for w in range(len(ins)) for k, (cx, cy) in enumerate([(1 - x, y), (x, 1 - y), (1 - x, 1 - y)])]

    return _SideJob(list(parts), [jax.ShapeDtypeStruct(p.shape, p.dtype) for p in parts], {}, 3 * len(parts), copies)


def _share_job(reduced, layer, depth, into):
    n_w = len(reduced)

    def copies(ins, outs, send_sems, recv_sems):
        x, y, c = lax.axis_index("x"), lax.axis_index("y"), lax.axis_index("c")
        return [pltpu.make_async_remote_copy(
            src_ref=ins[w], dst_ref=outs[w].at[layer, pl.ds(c * ins[w].shape[0], ins[w].shape[0])],
            send_sem=send_sems.at[w], recv_sem=recv_sems.at[w], device_id=(x, y, 1 - c), device_id_type=MESH)
            for w in range(n_w)]

    shapes = [jax.ShapeDtypeStruct((depth, 2 * r.shape[0], r.shape[1]), r.dtype) for r in reduced]
    if into is None:
        return _SideJob(list(reduced), shapes, {}, n_w, copies)
    return _SideJob(list(reduced) + list(into), shapes, {n_w + w: w for w in range(n_w)}, n_w, copies)


def _add_rows(name, grads, recv, core):
    n, r, c = recv.shape
    tr = _pick8(r, max(BF16_ROWS, (1 << 19) // (4 * c)), BF16_ROWS)
    steps = r // tr

    def body(core_ref, g_ref, r_ref, o_ref):
        o_ref[...] = (g_ref[...].astype(F32) + r_ref[...].astype(F32)).astype(BF16)

    grid_spec = pltpu.PrefetchScalarGridSpec(
        num_scalar_prefetch=1, grid=(steps,),
        in_specs=[pl.BlockSpec((n, tr, c), lambda i, core_ref: (0, core_ref[0] * steps + i, 0)),
                  pl.BlockSpec((n, tr, c), lambda i, core_ref: (0, i, 0))],
        out_specs=pl.BlockSpec((n, tr, c), lambda i, core_ref: (0, i, 0)))
    return pl.pallas_call(body, name=name, grid_spec=grid_spec,
                          out_shape=jax.ShapeDtypeStruct((n, r, c), BF16), compiler_params=_cparams())(core, grads, recv)


class _LayerReduce:
    def __init__(self, tag, layer, depth, grads, core, into):
        self.tag, self.layer, self.depth, self.core, self.into = tag, layer, depth, core, into
        self.state = list(grads)

    def _exchange(self, name, job, carry):
        if carry is None:
            return None, _run_job(f"{name}_{self.tag}", job)
        return carry(job)

    def swap_and_add(self, carry=None):
        grads = self.state
        results, recv = self._exchange("grads_swap_cores", _swap_job(grads), carry)
        self.state = [_add_rows(f"grads_add_{n}_{self.tag}", g, r, self.core) for n, g, r in zip(BIG, grads, recv)]
        return results

    def exchange_and_sum(self, carry=None):
        parts = self.state
        results, arrived = self._exchange("grads_exchange_chips", _exchange_job(parts), carry)
        my_chip = 2 * lax.axis_index("x") + lax.axis_index("y")
        arrived = [_put_own(a, lax.dynamic_index_in_dim(p, my_chip, 0, keepdims=False), (my_chip,))
                   for a, p in zip(arrived, parts)]
        self.state = [_sum_blocks(f"grads_sum_{n}_{self.tag}", a, F32) for n, a in zip(BIG, arrived)]
        return results

    def share(self, carry=None):
        reduced = self.state
        results, outs = self._exchange("grads_share_cores", _share_job(reduced, self.layer, self.depth, self.into), carry)
        self.state =[lax.dynamic_update_slice(o, r[None], (self.layer, lax.axis_index("c") * r.shape[0], 0))
                      for o, r in zip(outs, reduced)]
        return results


def _pack(arrays, cols, row_multiple, dtype):
    flat = jnp.concatenate([a.reshape(-1).astype(dtype) for a in arrays])
    unit = cols * row_multiple
    total = -(-flat.shape[0] // unit) * unit
    return jnp.pad(flat, (0, total - flat.shape[0])).reshape(total // cols, cols)


def _unpack(buf, shapes):
    flat, out, off = buf.reshape(-1), [], 0
    for sh in shapes:
        n = math.prod(sh)
        out.append(flat[off:off + n].reshape(sh))
        off += n
    return out


def _discretize(lam_re, lam_im, log_dt, b_re, b_im):
    lam = lax.complex(jnp.minimum(lam_re, -EIG_CLIP), lam_im)
    dt = jnp.exp(log_dt)[:, None]
    lam_bar = jnp.exp(lam * dt)
    b_bar = ((lam_bar - 1.0) / lam)[..., None] * lax.complex(b_re, b_im)
    return jnp.real(lam_bar), jnp.imag(lam_bar), jnp.real(b_bar), jnp.imag(b_bar)


def _scan_tables(ar, ai):
    a = lax.complex(ar, ai)
    pw = [a]
    for _ in range(7):
        pw.append(pw[-1] * a)
    rows = jnp.arange(SUBLANES)[:, None]

    def build(p, reverse):
        tabs = []
        for k in (1, 2, 4):
            keep = (rows <= SUBLANES - 1 - k) if reverse else (rows >= k)
            tk = jnp.where(keep, p[k - 1][None, :], 0.0)
            tabs += [jnp.real(tk), jnp.imag(tk)]
        stack = jnp.stack(p[::-1] if reverse else p)
        tabs += [jnp.real(stack), jnp.imag(stack)]
        return jnp.stack(tabs).astype(F32)

    return build(pw, False), build([jnp.conj(p) for p in pw], True)


def _interleave_rows(a, t):
    s, w = a.shape
    return a.reshape(s // t, SUBLANES, t // SUBLANES, w).transpose(0, 2, 1, 3).reshape(s, w)


def _deinterleave_rows(a, t):
    s, w = a.shape
    return a.reshape(s // t, t // SUBLANES, SUBLANES, w).transpose(0, 2, 1, 3).reshape(s, w)


def _block_diag(per_group, groups_per_block):
    g, a, b = per_group.shape
    x = per_group.reshape(g // groups_per_block, groups_per_block, a, b)
    eye = jnp.eye(groups_per_block, dtype=per_group.dtype)
    out = x[:, :, :, None, :] * eye[None, :, None, :, None]
    return out.reshape(g // groups_per_block, groups_per_block * a, groups_per_block * b)


def _block_diag_extract(dense, groups_per_block, a, b):
    nkb = dense.shape[0]
    x = dense.reshape(nkb, groups_per_block, a, groups_per_block, b)
    idx = jnp.arange(groups_per_block)
    return x[:, idx, :, idx, :].transpose(1, 0, 2, 3).reshape(nkb * groups_per_block, a, b)


def _layer_fwd(tag, x, mod, p, wts, gather_next=None):
    s, d = x.shape
    w_ssm, w_att = p["w_glu"].shape[0], wts["w_pb"].shape[1]
    heads = p["b_f"].shape[0]
    dh = w_att // heads
    cs = d // N_CHIPS
    fs = wts["w_ffn_down"].shape[1]
    tm = _pick(s, 1024)
    row = lambda v: v.reshape(1, -1)
    sv = {}

    h = _prenorm_fwd(f"prenorm_mix_{tag}", x, row(p["g_pre_mix"]), row(mod[1]), row(mod[0]))
    uqkv = _mm_plain(f"proj_main_{tag}", h, p["w_main"], "nn", BF16, tm=1024, tn=1024, tk=1024)
    fg = _mm_plain(f"proj_gate_{tag}", h, p["w_gates"], "nn", F32, tm=1024, tn=1024, tk=1024)
    f_t = fg[:, 2 * d:2 * d + heads].T

    t5 = min(S5_ROWS, s)
    u_il = _interleave_rows(uqkv[:, :w_ssm], t5)
    y_s5, ys_il, carries = _s5_fwd(f"s5_fwd_{tag}", u_il, p["b_blk"], p["c_blk"], p["a_f"], p["tab_f"],
                                   row(p["d_skip"]), p["w_glu"], row(p["b_glu"]))
    ys = _deinterleave_rows(ys_il, t5)

    assert dh * 2 == LANES and w_ssm % LANES == 0 and w_att % LANES == 0
    n_pairs = w_att // LANES
    blocks = (w_ssm // LANES, w_ssm // LANES + n_pairs, w_ssm // LANES + 2 * n_pairs)
    cum = _cum_fwd(f"cum_fwd_{tag}", f_t, p["b_f"].reshape(heads, 1))
    t = min(ATT_BLOCK, s)
    ck_cols, ck_rows = cum.reshape(heads, s, 1), cum.reshape(heads, s // t, 1, t)
    (ya, lse), arrived = _attn_fwd(f"attn_fwd_{tag}", uqkv, *blocks, n_pairs, ck_rows,
                                   side=gather_next[0] if gather_next else None)

    tile = pl.BlockSpec((tm, cs), lambda i, j, k: (i, j))
    slab = lambda rows: pl.BlockSpec((None, rows, cs), lambda i, j, k: (j, 0, 0))

    def merge(acc, extra_refs, out_refs):
        ya_ref, wpb_ref, ga_ref, gb_ref = extra_refs
        a_ref, b_ref, m_ref = out_refs
        bv = _dot(ya_ref[...], wpb_ref[...], NN)
        a_ref[...] = acc.astype(BF16)
        b_ref[...] = bv.astype(BF16)
        m_ref[...] = (_sigmoid(ga_ref[...]) * acc + _sigmoid(gb_ref[...]) * bv).astype(BF16)

    sd_bf = jax.ShapeDtypeStruct((s, d), BF16)
    pa, pb, merged = _mm_raw(
        f"merge_{tag}", ys, wts["w_pa"], "nn", (s // tm, N_CHIPS, 1), (tm, cs),
        pl.BlockSpec((tm, w_ssm), lambda i, j, k: (i, 0)), slab(w_ssm), [sd_bf] * 3, [tile] * 3, merge,
        extra=(ya, wts["w_pb"], fg, fg),
        extra_specs=[pl.BlockSpec((tm, w_att), lambda i, j, k: (i, 0)), slab(w_att), tile,
                     pl.BlockSpec((tm, cs), lambda i, j, k: (i, j + N_CHIPS))])

    tm2 = _pick(s, POSTNORM_ROWS)
    x1, y_mix = _mm_postnorm(
        f"out_proj_{tag}", merged, pl.BlockSpec((tm2, cs), lambda i, j, k: (i, k)), wts["w_o"],
        pl.BlockSpec((None, cs, d), lambda i, j, k: (k, 0, 0)), N_CHIPS, x, row(mod[2]), row(p["g_post_mix"]))

    h2 = _prenorm_fwd(f"prenorm_ffn_{tag}", x1, row(p["g_pre_ffn"]), row(mod[4]), row(mod[3]))
    (a4, b4, hid4), next_wts = _ffn_up(f"ffn_up_{tag}", h2, wts["w_ffn_gate"], wts["w_ffn_up"],
                                       side=gather_next[1](arrived) if gather_next else None)
    x2, y_ffn = _mm_postnorm(
        f"ffn_down_{tag}", hid4, pl.BlockSpec((None, tm2, fs), lambda i, j, k: (k, i, 0)), wts["w_ffn_down"],
        pl.BlockSpec((None, fs, d), lambda i, j, k: (k, 0, 0)), N_CHIPS, x1, row(mod[5]), row(p["g_post_ffn"]))

    sv.update(x=x, h=h, uqkv=uqkv, u_il=u_il, fg=fg, f_t=f_t, y_s5=y_s5, ys=ys, carries=carries, blocks=blocks,
              ck_cols=ck_cols, lse_rows=lse.reshape(heads, s // t, 1, t), ya=ya, pa=pa, pb=pb, merged=merged, x1=x1,
              y_mix=y_mix, h2=h2, a4=a4, b4=b4, hid4=hid4, y_ffn=y_ffn)
    return x2, sv, next_wts


def _mm_postnorm(name, a, a_spec, w, w_spec, nk, x, gate, g):
    s, d = x.shape
    tm = _pick(s, POSTNORM_ROWS)
    rowspec = pl.BlockSpec((tm, d), lambda i, j, k: (i, 0))
    vec = pl.BlockSpec((1, d), lambda i, j, k: (0, 0))

    def epilogue(acc, extra_refs, out_refs):
        x_ref, gate_ref, g_ref = extra_refs
        r = lax.rsqrt(jnp.mean(acc * acc, axis=-1, keepdims=True) + RMS_EPS)
        out_refs[0][...] = x_ref[...] + gate_ref[...] * (acc * r * g_ref[...])
        out_refs[1][...] = acc

    sd = jax.ShapeDtypeStruct((s, d), F32)
    return _mm_raw(name, a, w, "nn", (s // tm, 1, nk), (tm, d), a_spec, w_spec, [sd, sd], [rowspec, rowspec], epilogue,
                   extra=(x, gate, g), extra_specs=[rowspec, vec, vec])


def _layer_bwd(tag, dx2, mod, p, wts, sv, reduce_later=None):
    s, d = dx2.shape
    w_ssm, w_att = p["w_glu"].shape[0], wts["w_pb"].shape[1]
    heads = p["b_f"].shape[0]
    cs = d // N_CHIPS
    fs = wts["w_ffn_down"].shape[1]
    tm, tk, td = _pick(s, 1024), _pick(s, 1024), d
    row = lambda v: v.reshape(1, -1)
    gr = {}

    def dw_slabs(name, act, act_spec, rows, dy, dy_spec, cols, grid_mn, out_index):
        return _mm_raw(name, act, dy, "tn", grid_mn + (s // tk,), (rows, cols), act_spec, dy_spec,
                       [jax.ShapeDtypeStruct((N_CHIPS,) + out_index[1], BF16)],
                       [pl.BlockSpec((None, rows, cols), out_index[0])], _store(BF16))[0]

    dy_ffn, sums = _postnorm_bwd(f"postnorm_bwd_ffn_{tag}", dx2, sv["y_ffn"], row(p["g_post_ffn"]), row(mod[5]))
    d_gate_f, gr["g_post_ffn"] = sums[0], sums[1]
    gr["w_ffn_down"] = dw_slabs(f"dw_down_{tag}", sv["hid4"], pl.BlockSpec((None, tk, fs), lambda i, j, k: (i, k, 0)), fs,
                                dy_ffn, pl.BlockSpec((tk, d), lambda i, j, k: (k, 0)), d, (N_CHIPS, 1),
                                (lambda i, j, k: (i, 0, 0), (fs, d)))

    def swiglu_bwd(acc, extra_refs, out_refs):
        av, bv = extra_refs[0][...].astype(F32), extra_refs[1][...].astype(F32)
        sg = _sigmoid(av)
        out_refs[0][...] = (acc * bv * (sg * (1.0 + av * (1.0 - sg)))).astype(BF16)
        out_refs[1][...] = (acc * (av * sg)).astype(BF16)

    blk4 = pl.BlockSpec((None, tm, fs), lambda i, j, k: (j, i, 0))
    sh4 = jax.ShapeDtypeStruct((N_CHIPS, s, fs), BF16)
    ffn_down_bwd = lambda side: _mm_raw(
        f"ffn_down_bwd_{tag}", dy_ffn, wts["w_ffn_down"], "nt", (s // tm, N_CHIPS, 1), (tm, fs),
        pl.BlockSpec((tm, d), lambda i, j, k: (i, 0)), pl.BlockSpec((None, fs, d), lambda i, j, k: (j, 0, 0)),
        [sh4, sh4], [blk4, blk4], swiglu_bwd, extra=(sv["a4"], sv["b4"]), extra_specs=[blk4, blk4], side=side)
    da4, db4 = reduce_later.swap_and_add(ffn_down_bwd) if reduce_later else ffn_down_bwd(None)
    for n, act4 in (("w_ffn_gate", da4), ("w_ffn_up", db4)):
        gr[n] = dw_slabs(f"d{n}_{tag}", sv["h2"], pl.BlockSpec((tk, td), lambda i, j, k: (k, i)), td,
                         act4, pl.BlockSpec((None, tk, fs), lambda i, j, k: (j, k, 0)), fs, (d // td, N_CHIPS),
                         (lambda i, j, k: (j, i, 0), (d, fs)))
    pairs = [(act4, (None, tm, fs), lambda i, kk: (kk, i, 0), wts[n], (None, td, fs), lambda j, kk: (kk, j, 0),
              N_CHIPS) for n, act4 in (("w_ffn_gate", da4), ("w_ffn_up", db4))]
    dh2 = _mm_sum(f"dh_ffn_{tag}", s, d, tm, td, pairs, F32)
    dx1, sums = _prenorm_bwd(f"prenorm_bwd_ffn_{tag}", dh2, sv["x1"], row(p["g_pre_ffn"]), row(mod[4]), dx2)
    d_scale_f, d_shift_f, gr["g_pre_ffn"] = sums[0], sums[1], sums[2]

    dy_mix, sums = _postnorm_bwd(f"postnorm_bwd_mix_{tag}", dx1, sv["y_mix"], row(p["g_post_mix"]), row(mod[2]))
    d_gate_m, gr["g_post_mix"] = sums[0], sums[1]
    gr["w_o"] = dw_slabs(f"dw_o_{tag}", sv["merged"], pl.BlockSpec((tk, cs), lambda i, j, k: (k, i)), cs,
                         dy_mix, pl.BlockSpec((tk, d), lambda i, j, k: (k, 0)), d, (N_CHIPS, 1),
                         (lambda i, j, k: (i, 0, 0), (cs, d)))

    tile = pl.BlockSpec((tm, cs), lambda i, j, k: (i, j))

    def merge_bwd(acc, extra_refs, out_refs):
        a_ref, b_ref, ga_ref, gb_ref = extra_refs
        sa, sb = _sigmoid(ga_ref[...]), _sigmoid(gb_ref[...])
        out_refs[0][...] = (acc * sa).astype(BF16)
        out_refs[1][...] = (acc * sb).astype(BF16)
        out_refs[2][...] = (acc * a_ref[...].astype(F32) * sa * (1.0 - sa)).astype(BF16)
        out_refs[3][...] = (acc * b_ref[...].astype(F32) * sb * (1.0 - sb)).astype(BF16)

    sd_bf = jax.ShapeDtypeStruct((s, d), BF16)
    d_pa, d_pb, d_ga, d_gb = _mm_raw(
        f"out_proj_bwd_{tag}", dy_mix, wts["w_o"], "nt", (s // tm, N_CHIPS, 1), (tm, cs),
        pl.BlockSpec((tm, d), lambda i, j, k: (i, 0)), pl.BlockSpec((None, cs, d), lambda i, j, k: (j, 0, 0)),
        [sd_bf] * 4, [tile] * 4, merge_bwd, extra=(sv["pa"], sv["pb"], sv["fg"], sv["fg"]),
        extra_specs=[tile, tile, tile, pl.BlockSpec((tm, cs), lambda i, j, k: (i, j + N_CHIPS))])
    d_branch = {}
    for n, act, width, d_p in (("w_pa", sv["ys"], w_ssm, d_pa), ("w_pb", sv["ya"], w_att, d_pb)):
        gr[n] = dw_slabs(f"d{n}_{tag}", act, pl.BlockSpec((tk, width), lambda i, j, k: (k, 0)), width,
                         d_p, pl.BlockSpec((tk, cs), lambda i, j, k: (k, j)), cs, (1, N_CHIPS),
                         (lambda i, j, k: (j, 0, 0), (width, cs)))
        d_branch[n] = _mm_raw(
            f"d_in_{n}_{tag}", d_p, wts[n], "nt", (s // tm, 1, N_CHIPS), (tm, width),
            pl.BlockSpec((tm, cs), lambda i, j, k: (i, k)), pl.BlockSpec((None, width, cs), lambda i, j, k: (k, 0, 0)),
            [jax.ShapeDtypeStruct((s, width), BF16)], [pl.BlockSpec((tm, width), lambda i, j, k: (i, 0))], _store(BF16))[0]
    d_ys, d_ya = d_branch["w_pa"], d_branch["w_pb"]

    attn_bwd = lambda side: _attn_bwd(f"attn_bwd_{tag}", sv["uqkv"], *sv["blocks"], w_att // LANES, sv["ya"], d_ya,
                                      sv["lse_rows"], sv["ck_cols"], side=side)
    dq, dk, dv, dcq, dck = reduce_later.exchange_and_sum(attn_bwd) if reduce_later else attn_bwd(None)[0]
    d_f_t, d_bf = _cum_bwd(f"cum_bwd_{tag}", dcq.reshape(heads, s), dck.reshape(heads, s), sv["f_t"],
                           p["b_f"].reshape(heads, 1))
    gr["b_f"] = d_bf[:, 0]

    t5 = min(S5_ROWS, s)
    du_il, d_bblk, d_cblk, d_abar, d_wglu, vec = _s5_bwd(
        f"s5_bwd_{tag}", sv["u_il"], _interleave_rows(d_ys, t5), sv["y_s5"], sv["carries"], p["b_blk"], p["c_blk"],
        p["a_f"], p["a_r"], p["tab_f"], p["tab_r"], row(p["d_skip"]), p["w_glu"], row(p["b_glu"]))
    du = _deinterleave_rows(du_il, t5)
    gr["w_glu"] = d_wglu.astype(BF16).reshape(N_CHIPS, w_ssm // N_CHIPS, w_ssm)
    gr["b_glu"], gr["d_skip"] = vec[0], vec[1]
    gr["b_blk"], gr["c_blk"], gr["a_bar"] = d_bblk, d_cblk, d_abar

    d_f = jnp.pad(d_f_t.T, ((0, 0), (0, F_PAD - heads))).astype(BF16)
    assert w_ssm % w_att == 0 and (2 * d) % F_PAD == 0
    first = w_ssm // w_att
    main_pieces = [(du, w_ssm, 0), (dq, w_att, first), (dk, w_att, first + 1), (dv, w_att, first + 2)]
    dw = [_mm_plain(f"dw_in{n}_{tag}", sv["h"], piece, "tn", BF16, tm=1024, tn=1024, tk=1024)
          for n, piece in enumerate([du, dq, dk, dv, d_f, d_ga, d_gb])]
    w_in_grad = jnp.concatenate(dw[:4] + [dw[4][:, :heads], dw[5], dw[6]], axis=1)
    gr["w_in"] = w_in_grad.reshape(d, N_CHIPS, w_in_grad.shape[1] // N_CHIPS).transpose(1, 0, 2)
    tmx, tkx = _pick(s, 512), _pick(d, 512)
    pairs = [(piece, (tmx, width), lambda i, kk: (i, 0), p["w_main"], (d, width), lambda j, kk, blk=blk: (j, blk), 1)
             for piece, width, blk in main_pieces]
    steps = d // tkx
    pairs += [(piece, (tmx, tkx), lambda i, kk: (i, kk), p["w_gates"], (d, tkx), lambda j, kk, off=off: (j, off + kk), steps)
              for piece, off in ((d_ga, 0), (d_gb, steps))]
    pairs.append((d_f, (tmx, F_PAD), lambda i, kk: (i, 0), p["w_gates"], (d, F_PAD), lambda j, kk: (j, 2 * d // F_PAD), 1))
    dh_mix = lambda side: _mm_sum(f"dh_mix_{tag}", s, d, tmx, d, pairs, F32, side=side)
    dh1 = reduce_later.share(dh_mix) if reduce_later else dh_mix(None)
    dx0, sums = _prenorm_bwd(f"prenorm_bwd_mix_{tag}", dh1, sv["x"], row(p["g_pre_mix"]), row(mod[1]), dx1)
    d_scale_m, d_shift_m, gr["g_pre_mix"] = sums[0], sums[1], sums[2]

    d_mod = jnp.stack([d_shift_m, d_scale_m, d_gate_m, d_shift_f, d_scale_f, d_gate_f])
    return dx0, d_mod, gr


BIG = ("w_in", "w_glu", "w_pa", "w_pb", "w_o", "w_ffn_gate", "w_ffn_up", "w_ffn_down")
SMALL = ("b_ada", "g_pre_mix", "g_post_mix", "g_pre_ffn", "g_post_ffn", "lam_re", "lam_im", "log_dt", "b_re", "b_im",
         "c_re", "c_im", "d_skip", "b_glu", "b_f")
WEIGHTS = ("w_ada", "b_ada", "g_pre_mix", "g_post_mix", "g_pre_ffn", "g_post_ffn", "w_in", "lam_re", "lam_im", "log_dt",
           "b_re", "b_im", "c_re", "c_im", "d_skip", "w_glu", "b_glu", "b_f", "w_pa", "w_pb", "w_o", "w_ffn_gate",
           "w_ffn_up", "w_ffn_down")


def _prepare_layer(wts, small, l, seq):
    w_in = jnp.concatenate([wts["w_in"][j] for j in range(N_CHIPS)], axis=1)
    d = w_in.shape[0]
    heads = small["b_f"].shape[1]
    n_groups, n_state, group_ch = small["b_re"].shape[1:]
    w_ssm = n_groups * group_ch
    w_att = wts["w_pb"].shape[1]
    n_main = w_ssm + 3 * w_att
    gpb = LANES // group_ch
    p = {}
    p["w_main"] = w_in[:, :n_main]
    p["w_gates"] = jnp.concatenate(
        [w_in[:, n_main + heads:], w_in[:, n_main:n_main + heads], jnp.zeros((d, F_PAD - heads), BF16)], axis=1)
    p["w_glu"] = wts["w_glu"].reshape(w_ssm, w_ssm)
    for n in ("g_pre_mix", "g_post_mix", "g_pre_ffn", "g_post_ffn", "d_skip", "b_glu", "b_f"):
        p[n] = small[n][l]
    ar, ai, br, bi = _discretize(small["lam_re"][l], small["lam_im"][l], small["log_dt"][l], small["b_re"][l], small["b_im"][l])
    n_steps = min(S5_ROWS, seq) // SUBLANES
    powers = jnp.cumprod(jnp.broadcast_to(lax.complex(ar, ai).reshape(1, -1), (n_steps, ar.size)), axis=0)
    p["a_f"] = jnp.concatenate([jnp.real(powers), jnp.imag(powers)], axis=1)
    p["a_r"] = jnp.concatenate([jnp.real(powers[::-1]), -jnp.imag(powers[::-1])], axis=1)
    p["tab_f"], p["tab_r"] = _scan_tables(jnp.real(powers[-1]), jnp.imag(powers[-1]))
    bre = _block_diag(br.transpose(0, 2, 1), gpb)
    bim = _block_diag(bi.transpose(0, 2, 1), gpb)
    p["b_blk"] = jnp.concatenate([bre, bim], axis=2).astype(BF16)
    cre = _block_diag(small["c_re"][l].transpose(0, 2, 1), gpb)
    cim = _block_diag(small["c_im"][l].transpose(0, 2, 1), gpb)
    p["c_blk"] = jnp.concatenate([cre, -cim], axis=1).astype(BF16)
    return p


def _compact_partials(gr, n_state, group_ch):
    gpb = LANES // group_ch
    half = gpb * n_state
    out = dict(gr)
    out["bbar_re"] = _block_diag_extract(gr["b_blk"][:, :, :half], gpb, group_ch, n_state).transpose(0, 2, 1)
    out["bbar_im"] = _block_diag_extract(gr["b_blk"][:, :, half:], gpb, group_ch, n_state).transpose(0, 2, 1)
    out["c_re"] = _block_diag_extract(gr["c_blk"][:, :half, :], gpb, n_state, group_ch).transpose(0, 2, 1)
    out["c_im"] = -_block_diag_extract(gr["c_blk"][:, half:, :], gpb, n_state, group_ch).transpose(0, 2, 1)
    return out


def _small_grads_from_partials(gr, small, l):
    n_groups, n_state, _ = small["b_re"].shape[1:]
    ns2 = n_groups * n_state
    d_abar = jnp.sum(gr["a_bar"], axis=0)
    dar, dai = d_abar[:ns2].reshape(n_groups, n_state), d_abar[ns2:].reshape(n_groups, n_state)
    args = (small["lam_re"][l], small["lam_im"][l], small["log_dt"][l], small["b_re"][l], small["b_im"][l])
    _, vjp = jax.vjp(_discretize, *args)
    d_lam_re, d_lam_im, d_log_dt, d_b_re, d_b_im = vjp((dar, dai, gr["bbar_re"], gr["bbar_im"]))
    return dict(lam_re=d_lam_re, lam_im=d_lam_im, log_dt=d_log_dt, b_re=d_b_re, b_im=d_b_im,
                c_re=gr["c_re"], c_im=gr["c_im"])


def _fwd_bwd(xs, target, mods, small, wts0, later, core=None):
    depth = 1 + len(later)
    saved, layers, wts = [], [], [wts0]
    act = xs
    for l in range(depth):
        layers.append(_prepare_layer(wts[l], small, l, xs.shape[0]))
        shards = later[l] if l + 1 < depth and not isinstance(later[l], dict) else None
        act, sv, gathered = _layer_fwd(str(l), act, mods[l], layers[l], wts[l],
                                       gather_next=_gather_side_jobs(shards) if shards is not None else None)
        saved.append(sv)
        if l + 1 < depth:
            wts.append(dict(zip(BIG, _put_own_slabs(gathered, shards))) if shards is not None else later[l])
    dx, loss_blk = _loss_grad("loss", act, target)
    grads, d_mods = [None] * depth, [None] * depth
    pending = None
    for l in reversed(range(depth)):
        dx, d_mods[l], grads[l] = _layer_bwd(str(l), dx, mods[l], layers[l], wts[l], saved[l], reduce_later=pending)
        if core is not None:
            pending = _LayerReduce(str(l), l, depth, [grads[l][n] for n in BIG], core,
                                   into=pending.state if pending is not None else None)
    if core is None:
        return loss_blk, dx, d_mods, grads, None
    pending.swap_and_add()
    pending.exchange_and_sum()
    pending.share()
    return loss_blk, dx, d_mods, grads, dict(zip(BIG, pending.state))


def kernel(x, c, w_ada, b_ada, g_pre_mix, g_post_mix, g_pre_ffn, g_post_ffn, w_in, lam_re, lam_im, log_dt, b_re, b_im, c_re, c_im, d_skip, w_glu, b_glu, b_f, w_pa, w_pb, w_o, w_ffn_gate, w_ffn_up, w_ffn_down, loss_target, m_w_ada, m_b_ada, m_g_pre_mix, m_g_post_mix, m_g_pre_ffn, m_g_post_ffn, m_w_in, m_lam_re, m_lam_im, m_log_dt, m_b_re, m_b_im, m_c_re, m_c_im, m_d_skip, m_w_glu, m_b_glu, m_b_f, m_w_pa, m_w_pb, m_w_o, m_w_ffn_gate, m_w_ffn_up, m_w_ffn_down, v_w_ada, v_b_ada, v_g_pre_mix, v_g_post_mix, v_g_pre_ffn, v_g_post_ffn, v_w_in, v_lam_re, v_lam_im, v_log_dt, v_b_re, v_b_im, v_c_re, v_c_im, v_d_skip, v_w_glu, v_b_glu, v_b_f, v_w_pa, v_w_pb, v_w_o, v_w_ffn_gate, v_w_ffn_up, v_w_ffn_down):
    local = dict(locals())
    weights = {n: local[n] for n in WEIGHTS}
    moments_m = {n: local["m_" + n] for n in WEIGHTS}
    moments_v = {n: local["v_" + n] for n in WEIGHTS}
    depth, d = g_pre_mix.shape
    n_mod = w_ada.shape[2] * N_CHIPS // d
    mx, my, mc = lax.axis_index("x"), lax.axis_index("y"), lax.axis_index("c")
    my_chip = 2 * mx + my
    my_dev = 4 * mx + 2 * my + mc
    xs = x[0]

    shards = [[weights[n][l].astype(BF16) for n in BIG] for l in range(depth)]
    wts0 = dict(zip(BIG, _gather_layer("gather_weights_0", shards[0])))
    small = {n: weights[n] for n in SMALL}

    c_pad = jnp.pad(c, ((0, SUBLANES - 1), (0, 0)))
    c_all = _all_gather("gather_cond", c_pad).reshape(N_DEV, SUBLANES, d)[:, 0, :]
    silu = lambda v: v * _sigmoid(v)
    n_cols = w_ada.shape[2]
    mod_shard = []
    for l in range(depth):
        bias = lax.dynamic_slice_in_dim(b_ada[l], my_chip * n_cols, n_cols)
        mod_shard.append(_mm_plain(f"ada_{l}", c_all, w_ada[l], "nn", F32, add=jnp.broadcast_to(bias, (N_DEV, n_cols)),
                                   a_fn=silu, tm=N_DEV, tn=512, tk=1024))
    mod_block = jnp.concatenate(mod_shard, axis=1)
    mod_all = _all_gather("gather_mod", mod_block).reshape(N_DEV, N_DEV, depth, n_cols)
    mod_rows = lax.dynamic_index_in_dim(mod_all[0::2], my_dev, axis=1, keepdims=False)
    mods = [mod_rows[:, l, :].reshape(n_mod, d) for l in range(depth)]

    loss_blk, dx, d_mods, grads, big_grads = _fwd_bwd(xs, loss_target[0], mods, small, wts0, shards[1:],
                                                      core=mc.astype(jnp.int32).reshape(1))
    loss = lax.psum(loss_blk[0, 0], ("x", "y", "c"))
    grad_x = dx[None]

    partial_names = ("g_pre_mix", "g_post_mix", "g_pre_ffn", "g_post_ffn", "d_skip", "b_glu", "b_f", "a_bar",
                     "bbar_re", "bbar_im", "c_re", "c_im")
    n_state, group_ch = b_re.shape[2:]
    contrib = list(d_mods)
    for l in range(depth):
        compact = _compact_partials(grads[l], n_state, group_ch)
        contrib += [compact[n] for n in partial_names]
    contrib_shapes = [a.shape for a in contrib]
    block = _pack(contrib, LANES, BF16_ROWS, F32)
    rows = block.shape[0]
    all_blocks = _all_gather("gather_small_grads", block).reshape(N_DEV, rows, LANES)
    summed = _unpack(_sum_blocks("sum_small_grads", all_blocks, F32), contrib_shapes)
    per_layer = len(partial_names)
    small_grads = {n: [] for n in SMALL}
    d_mod_all = []
    for l in range(depth):
        small_grads["b_ada"].append(summed[l].reshape(-1))
        gl = dict(zip(partial_names, summed[depth + l * per_layer:depth + (l + 1) * per_layer]))
        for n in ("g_pre_mix", "g_post_mix", "g_pre_ffn", "g_post_ffn", "d_skip", "b_glu", "b_f"):
            small_grads[n].append(gl[n])
        for n, gval in _small_grads_from_partials(gl, small, l).items():
            small_grads[n].append(gval)
        d_mod_all.append(all_blocks.reshape(N_DEV, rows * LANES)[:, l * n_mod * d:(l + 1) * n_mod * d])
    small_grads = {n: jnp.stack(v) for n, v in small_grads.items()}

    g_w_ada = []
    for l in range(depth):
        cols = lax.dynamic_slice_in_dim(d_mod_all[l], my_chip * n_cols, n_cols, axis=1)
        g_w_ada.append(_mm_plain(f"dw_ada_{l}", c_all, cols, "tn", F32, a_fn=silu, tm=512, tn=512, tk=N_DEV))
    all_grads = dict(big_grads)
    all_grads.update(small_grads)
    all_grads["w_ada"] = jnp.stack(g_w_ada)

    delta, new_m, new_v = {}, {}, {}
    for n in ("w_ada",) + BIG:
        shape = weights[n].shape
        two_d = lambda a: a.reshape(-1, shape[-1])
        dl, nm, nv = _adamw(f"adamw_{n}", two_d(weights[n]), two_d(all_grads[n]), two_d(moments_m[n]), two_d(moments_v[n]))
        delta[n], new_m[n], new_v[n] = dl.reshape(shape), nm.reshape(shape), nv.reshape(shape)
    small_shapes = [weights[n].shape for n in SMALL]
    packed = [_pack([src[n] for n in SMALL], LANES, SUBLANES, F32) for src in (weights, all_grads, moments_m, moments_v)]
    outs = _adamw("adamw_small", *packed)
    for dst, buf in zip((delta, new_m, new_v), outs):
        dst.update(dict(zip(SMALL, _unpack(buf, small_shapes))))

    return (loss, grad_x, *[all_grads[n] for n in WEIGHTS], *[delta[n] for n in WEIGHTS],
            *[new_m[n] for n in WEIGHTS], *[new_v[n] for n in WEIGHTS])
```

```python
import functools
import math

import jax
import jax.numpy as jnp
from jax import lax
from jax.experimental import pallas as pl
from jax.experimental.pallas import tpu as pltpu

F32 = jnp.float32
BF16 = jnp.bfloat16
MESH = pl.DeviceIdType.MESH

RMS_EPS = 1e-6
EIG_CLIP = 1e-4
ADAM_LR, ADAM_B1, ADAM_B2, ADAM_EPS, ADAM_WD, ADAM_STEP = 0.001, 0.9, 0.999, 1e-08, 0.01, 10

LANES = 128
SUBLANES = 8
VMEM_LIMIT = 56 * 1024 * 1024
S5_ROWS = 256
S5_CHUNK = 1024
S5_UNROLL = 4
ATT_BLOCK = 512
F_PAD = 256
POSTNORM_ROWS = 512
N_CHIPS = 4
N_DEV = 8

NN = (((1,), (0,)), ((), ()))
NT = (((1,), (1,)), ((), ()))
TN = (((0,), (0,)), ((), ()))
_DN = {"nn": NN, "nt": NT, "tn": TN}


def _cparams(**kw):
    return pltpu.CompilerParams(vmem_limit_bytes=VMEM_LIMIT, **kw)


def _pick(dim, target):
    best, t = None, LANES
    while t <= min(dim, target):
        if dim % t == 0:
            best = t
        t += LANES
    return best or dim


def _sigmoid(x):
    return 1.0 / (1.0 + jnp.exp(-x))


def _dot(a, b, dn):
    return lax.dot_general(a, b, dn, preferred_element_type=F32)


def _mm_raw(name, a, b, mode, grid, acc_shape, a_spec, b_spec, out_shapes, out_specs, epilogue,
            extra=(), extra_specs=(), a_fn=None, side=None):
    nk = grid[2]
    n_extra, n_out = len(extra), len(out_shapes)

    def body(*refs):
        a_ref, b_ref = refs[0], refs[1]
        extra_refs = refs[2:2 + n_extra]
        out_refs = refs[2 + n_extra:2 + n_extra + n_out]
        acc = refs[-1]
        k = pl.program_id(2)

        @pl.when(k == 0)
        def _():
            acc[...] = jnp.zeros_like(acc)

        av = a_ref[...]
        if a_fn is not None:
            av = a_fn(av.astype(F32))
        acc[...] += _dot(av.astype(BF16), b_ref[...].astype(BF16), _DN[mode])

        @pl.when(k == nk - 1)
        def _():
            epilogue(acc[...], extra_refs, out_refs)

    outs, side_outs = _hosted_call(body, side, name, grid, [a_spec, b_spec, *extra_specs], list(out_specs),
                                   list(out_shapes), [pltpu.VMEM(acc_shape, F32)], (a, b, *extra))
    return outs if side is None else (outs, side_outs)


def _mm(name, a, b, mode, out_shapes, out_specs, epilogue, extra=(), extra_specs=(),
        tm=512, tn=512, tk=512, a_fn=None):
    if mode == "nn":
        (m, kd), (_, n) = a.shape, b.shape
    elif mode == "nt":
        (m, kd), (n, _) = a.shape, b.shape
    else:
        (kd, m), (_, n) = a.shape, b.shape
    tm, tn, tk = _pick(m, tm), _pick(n, tn), _pick(kd, tk)
    if mode == "tn":
        a_spec = pl.BlockSpec((tk, tm), lambda i, j, k: (k, i))
    else:
        a_spec = pl.BlockSpec((tm, tk), lambda i, j, k: (i, k))
    if mode == "nt":
        b_spec = pl.BlockSpec((tn, tk), lambda i, j, k: (j, k))
    else:
        b_spec = pl.BlockSpec((tk, tn), lambda i, j, k: (k, j))
    res = _mm_raw(name, a, b, mode, (m // tm, n // tn, kd // tk), (tm, tn), a_spec, b_spec, out_shapes, out_specs,
                  epilogue, extra=extra, extra_specs=extra_specs, a_fn=a_fn)
    return res, (tm, tn, tk)


def _store(dtype):
    def epilogue(acc, extra_refs, out_refs):
        out_refs[0][...] = acc.astype(dtype)
    return epilogue


def _mm_sum(name, m, n, tm, tn, pairs, out_dtype, side=None):
    offs, total = [], 0
    for pr in pairs:
        offs.append(total)
        total += pr[6]
    n_p = len(pairs)

    def body(*refs):
        o_ref, acc = refs[2 * n_p], refs[2 * n_p + 1]
        k = pl.program_id(2)

        @pl.when(k == 0)
        def _():
            acc[...] = jnp.zeros_like(acc)

        for p_ in range(n_p):
            @pl.when((k >= offs[p_]) & (k < offs[p_] + pairs[p_][6]))
            def _(p_=p_):
                acc[...] += _dot(refs[2 * p_][...].astype(BF16), refs[2 * p_ + 1][...].astype(BF16), NT)

        @pl.when(k == total - 1)
        def _():
            o_ref[...] = acc[...].astype(out_dtype)

    in_specs, operands = [], []
    for (a, a_block, a_index, b, b_block, b_index, steps), off in zip(pairs, offs):
        local = lambda k, off=off, steps=steps: jnp.clip(k - off, 0, steps - 1)
        in_specs.append(pl.BlockSpec(a_block, lambda i, j, k, f=a_index, local=local: f(i, local(k))))
        in_specs.append(pl.BlockSpec(b_block, lambda i, j, k, f=b_index, local=local: f(j, local(k))))
        operands += [a, b]
    (out,), side_outs = _hosted_call(
        body, side, name, (m // tm, n // tn, total), in_specs, [pl.BlockSpec((tm, tn), lambda i, j, k: (i, j))],
        [jax.ShapeDtypeStruct((m, n), out_dtype)], [pltpu.VMEM((tm, tn), F32)], operands)
    return out if side is None else (out, side_outs)


class _SideJob:
    def __init__(self, arrays, out_shapes, aliases, n_sems, copies):
        self.arrays, self.out_shapes, self.aliases, self.n_sems, self.copies = arrays, out_shapes, aliases, n_sems, copies


def _hosted_call(body, side, name, grid, in_specs, out_specs, out_shape, scratch_shapes, operands):
    if side is None:
        outs = pl.pallas_call(body, name=name, grid=grid, in_specs=in_specs, out_specs=out_specs, out_shape=out_shape,
                              scratch_shapes=scratch_shapes, compiler_params=_cparams())(*operands)
        return outs, []
    n_in, n_out, ns_in, ns_out = len(in_specs), len(out_specs), len(side.arrays), len(side.out_shapes)

    def wrapped(*refs):
        main_in, side_in = refs[:n_in], refs[n_in:n_in + ns_in]
        rest = refs[n_in + ns_in:]
        main_out, side_out, rest = rest[:n_out], rest[n_out:n_out + ns_out], rest[n_out + ns_out:]
        scratch, send_sems, recv_sems = rest[:-2], rest[-2], rest[-1]
        first, last = None, None
        for axis, extent in enumerate(grid):
            at_start, at_end = pl.program_id(axis) == 0, pl.program_id(axis) == extent - 1
            first = at_start if first is None else first & at_start
            last = at_end if last is None else last & at_end

        @pl.when(first)
        def _():
            for cp in side.copies(side_in, side_out, send_sems, recv_sems):
                cp.start()

        body(*main_in, *main_out, *scratch)

        @pl.when(last)
        def _():
            for cp in side.copies(side_in, side_out, send_sems, recv_sems):
                cp.wait()

    hbm = pl.BlockSpec(memory_space=pl.ANY)
    outs = pl.pallas_call(
        wrapped, name=name, grid=grid, in_specs=list(in_specs) + [hbm] * ns_in,
        out_specs=list(out_specs) + [hbm] * ns_out, out_shape=list(out_shape) + list(side.out_shapes),
        scratch_shapes=list(scratch_shapes) + [pltpu.SemaphoreType.DMA((side.n_sems,))] * 2,
        input_output_aliases={n_in + i: n_out + o for i, o in side.aliases.items()},
        compiler_params=_cparams(),
    )(*operands, *side.arrays)
    return outs[:n_out], outs[n_out:]


def _ffn_up(name, h, wg, wu, side=None):
    s, d = h.shape
    nc, fs = wg.shape[0], wg.shape[2]
    tm, tk = _pick(s, 1024), _pick(d, 1024)
    nk = d // tk

    def body(h_ref, wg_ref, wu_ref, a_ref, b_ref, hid_ref, acc_g, acc_u):
        k = pl.program_id(2)

        @pl.when(k == 0)
        def _():
            acc_g[...] = jnp.zeros_like(acc_g)
            acc_u[...] = jnp.zeros_like(acc_u)

        hv = h_ref[...]
        acc_g[...] += _dot(hv, wg_ref[...], NN)
        acc_u[...] += _dot(hv, wu_ref[...], NN)

        @pl.when(k == nk - 1)
        def _():
            av, bv = acc_g[...], acc_u[...]
            a_ref[...] = av.astype(BF16)
            b_ref[...] = bv.astype(BF16)
            hid_ref[...] = (av * _sigmoid(av) * bv).astype(BF16)

    w_spec = pl.BlockSpec((None, tk, fs), lambda i, j, k: (j, k, 0))
    o_spec = pl.BlockSpec((None, tm, fs), lambda i, j, k: (j, i, 0))
    sh = jax.ShapeDtypeStruct((nc, s, fs), BF16)
    return _hosted_call(
        body, side, name, (s // tm, nc, nk), [pl.BlockSpec((tm, tk), lambda i, j, k: (i, k)), w_spec, w_spec],
        [o_spec] * 3, [sh] * 3, [pltpu.VMEM((tm, fs), F32), pltpu.VMEM((tm, fs), F32)], (h, wg, wu))


def _mm_plain(name, a, b, mode, out_dtype, add=None, a_fn=None, tm=512, tn=512, tk=512):
    if mode == "nn":
        m, n = a.shape[0], b.shape[1]
    elif mode == "nt":
        m, n = a.shape[0], b.shape[0]
    else:
        m, n = a.shape[1], b.shape[1]
    tm_, tn_ = _pick(m, tm), _pick(n, tn)
    spec = pl.BlockSpec((tm_, tn_), lambda i, j, k: (i, j))

    def epilogue(acc, extra_refs, out_refs):
        if add is not None:
            acc = acc + extra_refs[0][...]
        out_refs[0][...] = acc.astype(out_dtype)

    extra = () if add is None else (add,)
    (out,), _ = _mm(name, a, b, mode, [jax.ShapeDtypeStruct((m, n), out_dtype)], [spec], epilogue,
                    extra=extra, extra_specs=[spec] * len(extra), tm=tm, tn=tn, tk=tk, a_fn=a_fn)
    return out


def _row_tile(s, d):
    return _pick(s, max(SUBLANES, (1 << 20) // (4 * d)))


def _prenorm_fwd(name, x, g, scale, shift):
    s, d = x.shape
    tr = _row_tile(s, d)

    def body(x_ref, g_ref, sc_ref, sh_ref, h_ref):
        xv = x_ref[...]
        r = lax.rsqrt(jnp.mean(xv * xv, axis=-1, keepdims=True) + RMS_EPS)
        h_ref[...] = ((xv * r * g_ref[...]) * (1.0 + sc_ref[...]) + sh_ref[...]).astype(BF16)

    row = pl.BlockSpec((tr, d), lambda i: (i, 0))
    vec = pl.BlockSpec((1, d), lambda i: (0, 0))
    return pl.pallas_call(body, name=name, grid=(s // tr,), in_specs=[row, vec, vec, vec], out_specs=row,
                          out_shape=jax.ShapeDtypeStruct((s, d), BF16), compiler_params=_cparams())(x, g, scale, shift)


def _prenorm_bwd(name, dh, x, g, scale, dx_res):
    s, d = x.shape
    tr = _row_tile(s, d)

    def body(dh_ref, x_ref, g_ref, sc_ref, dxr_ref, dx_ref, sums_ref):
        @pl.when(pl.program_id(0) == 0)
        def _():
            sums_ref[...] = jnp.zeros_like(sums_ref)

        xv, dhv, gv = x_ref[...], dh_ref[...].astype(F32), g_ref[...]
        r = lax.rsqrt(jnp.mean(xv * xv, axis=-1, keepdims=True) + RMS_EPS)
        xhat = xv * r
        dxn = dhv * (1.0 + sc_ref[...])
        dxhat = dxn * gv
        dx = r * (dxhat - xhat * jnp.mean(dxhat * xhat, axis=-1, keepdims=True))
        dx_ref[...] = dxr_ref[...] + dx
        sums_ref[0:1, :] += jnp.sum(dhv * (xhat * gv), axis=0, keepdims=True)
        sums_ref[1:2, :] += jnp.sum(dhv, axis=0, keepdims=True)
        sums_ref[2:3, :] += jnp.sum(dxn * xhat, axis=0, keepdims=True)

    row = pl.BlockSpec((tr, d), lambda i: (i, 0))
    vec = pl.BlockSpec((1, d), lambda i: (0, 0))
    acc = pl.BlockSpec((SUBLANES, d), lambda i: (0, 0))
    return pl.pallas_call(
        body, name=name, grid=(s // tr,), in_specs=[row, row, vec, vec, row], out_specs=[row, acc],
        out_shape=[jax.ShapeDtypeStruct((s, d), F32), jax.ShapeDtypeStruct((SUBLANES, d), F32)],
        compiler_params=_cparams())(dh, x, g, scale, dx_res)


def _postnorm_bwd(name, dxn, y, g, gate):
    s, d = y.shape
    tr = _row_tile(s, d)

    def body(dx_ref, y_ref, g_ref, gt_ref, dy_ref, sums_ref):
        @pl.when(pl.program_id(0) == 0)
        def _():
            sums_ref[...] = jnp.zeros_like(sums_ref)

        yv, dxv, gv = y_ref[...], dx_ref[...], g_ref[...]
        r = lax.rsqrt(jnp.mean(yv * yv, axis=-1, keepdims=True) + RMS_EPS)
        yhat = yv * r
        dn = dxv * gt_ref[...]
        dyhat = dn * gv
        dy_ref[...] = (r * (dyhat - yhat * jnp.mean(dyhat * yhat, axis=-1, keepdims=True))).astype(BF16)
        sums_ref[0:1, :] += jnp.sum(dxv * (yhat * gv), axis=0, keepdims=True)
        sums_ref[1:2, :] += jnp.sum(dn * yhat, axis=0, keepdims=True)

    row = pl.BlockSpec((tr, d), lambda i: (i, 0))
    vec = pl.BlockSpec((1, d), lambda i: (0, 0))
    acc = pl.BlockSpec((SUBLANES, d), lambda i: (0, 0))
    return pl.pallas_call(
        body, name=name, grid=(s // tr,), in_specs=[row, row, vec, vec], out_specs=[row, acc],
        out_shape=[jax.ShapeDtypeStruct((s, d), BF16), jax.ShapeDtypeStruct((SUBLANES, d), F32)],
        compiler_params=_cparams())(dxn, y, g, gate)


def _loss_grad(name, y, target):
    s, d = y.shape
    tr = _row_tile(s, d)

    def body(y_ref, t_ref, dy_ref, loss_ref):
        @pl.when(pl.program_id(0) == 0)
        def _():
            loss_ref[...] = jnp.zeros_like(loss_ref)

        err = y_ref[...] - t_ref[...]
        dy_ref[...] = err * (1.0 / d)
        part = jnp.sum(jnp.sum(err * err, axis=-1, keepdims=True), axis=0, keepdims=True) * (0.5 / d)
        loss_ref[...] += jnp.broadcast_to(part, loss_ref.shape)

    row = pl.BlockSpec((tr, d), lambda i: (i, 0))
    acc = pl.BlockSpec((SUBLANES, LANES), lambda i: (0, 0))
    return pl.pallas_call(
        body, name=name, grid=(s // tr,), in_specs=[row, row], out_specs=[row, acc],
        out_shape=[jax.ShapeDtypeStruct((s, d), F32), jax.ShapeDtypeStruct((SUBLANES, LANES), F32)],
        compiler_params=_cparams())(y, target)


def _gelu(y):
    c = math.sqrt(2.0 / math.pi)
    return 0.5 * y * (1.0 + jnp.tanh(c * (y + 0.044715 * (y * y * y))))


def _gelu_grad(y):
    c = math.sqrt(2.0 / math.pi)
    th = jnp.tanh(c * (y + 0.044715 * (y * y * y)))
    return 0.5 * (1.0 + th) + 0.5 * y * (1.0 - th * th) * c * (1.0 + 3.0 * 0.044715 * (y * y))


def _cmul_add(br, bi, ar, ai, xr, xi):
    return br + ar * xr - ai * xi, bi + ar * xi + ai * xr


def _scan_rows(x_ref, row0, n_steps, ns2, pow_ref, tab_ref, carry_ref, reverse, fold=None):
    assert n_steps % SUBLANES == 0
    wc = min(S5_CHUNK, ns2)
    sub = lax.broadcasted_iota(jnp.int32, (SUBLANES, wc), 0)
    unroll = S5_UNROLL if n_steps % S5_UNROLL == 0 else 1
    for c0 in range(0, ns2, wc):
        re = slice(c0, c0 + wc)
        im = slice(ns2 + c0, ns2 + c0 + wc)
        first_power = slice(n_steps - 1, n_steps) if reverse else slice(0, 1)
        ar = jnp.broadcast_to(pow_ref[first_power, re], (SUBLANES, wc))
        ai = jnp.broadcast_to(pow_ref[first_power, im], (SUBLANES, wc))
        rows = lambda r: pl.ds(pl.multiple_of(row0 + r * SUBLANES, SUBLANES), SUBLANES)
        step_of = lambda i: (n_steps - 1 - i) if reverse else i

        def local(i, carry, re=re, im=im, ar=ar, ai=ai):
            for u in range(unroll):
                r = step_of(i * unroll + u)
                carry = _cmul_add(x_ref[rows(r), re], x_ref[rows(r), im], ar, ai, *carry)
                x_ref[rows(r), re], x_ref[rows(r), im] = carry
            return carry

        zero = jnp.zeros((SUBLANES, wc), F32)
        lr, li = lax.fori_loop(0, n_steps // unroll, local, (zero, zero))

        tabs = [tab_ref[k, :, re] for k in range(8)]
        for lvl, k in enumerate((1, 2, 4)):
            sh = (SUBLANES - k) if reverse else k
            lr, li = _cmul_add(lr, li, tabs[2 * lvl], tabs[2 * lvl + 1], pltpu.roll(lr, sh, 0), pltpu.roll(li, sh, 0))
        cr, ci = carry_ref[0:1, re], carry_ref[0:1, im]
        lr, li = _cmul_add(lr, li, tabs[6], tabs[7], cr, ci)
        edge, away, last = (SUBLANES - 1, SUBLANES - 1, 0) if reverse else (0, 1, SUBLANES - 1)
        carry_ref[0:1, re] = lr[last:last + 1, :]
        carry_ref[0:1, im] = li[last:last + 1, :]
        er = jnp.where(sub == edge, cr, pltpu.roll(lr, away, 0))
        ei = jnp.where(sub == edge, ci, pltpu.roll(li, away, 0))

        def fix(j, acc, re=re, im=im, er=er, ei=ei, c0=c0):
            base = pl.ds(pl.multiple_of(j * SUBLANES, SUBLANES), SUBLANES)
            pw_r, pw_i = pow_ref[base, re], pow_ref[base, im]
            for i in range(SUBLANES):
                r = j * SUBLANES + i
                xr, xi = _cmul_add(x_ref[rows(r), re], x_ref[rows(r), im], pw_r[i:i + 1, :], pw_i[i:i + 1, :], er, ei)
                x_ref[rows(r), re], x_ref[rows(r), im] = xr, xi
                if fold is not None:
                    acc = fold(c0, r, xr, xi, acc)
            return acc

        acc = lax.fori_loop(0, n_steps // SUBLANES, fix, (zero, zero) if fold is not None else 0)
        if fold is not None:
            fold(c0, None, None, None, acc)


def _s5_fwd(name, u, b_blk, c_blk, a_f, tab_f, dskip, w_glu, b_glu):
    s, w = u.shape[0], w_glu.shape[0]
    nkb = w // LANES
    ns2 = b_blk.shape[2] // 2 * nkb
    half = ns2 // nkb
    t = min(S5_ROWS, s)
    nblk = s // t

    def body(u_ref, b_ref, c_ref, a_ref, tab_ref, ds_ref, wg_ref, bg_ref, y_ref, ys_ref, cs_ref, xs, carry):
        @pl.when(pl.program_id(0) == 0)
        def _():
            carry[...] = jnp.zeros_like(carry)

        cs_ref[0] = carry[...]
        for kb in range(nkb):
            bu = _dot(u_ref[:, kb * LANES:(kb + 1) * LANES], b_ref[kb], NN)
            xs[:, kb * half:(kb + 1) * half] = bu[:, :half]
            xs[:, ns2 + kb * half:ns2 + (kb + 1) * half] = bu[:, half:]
        _scan_rows(xs, 0, t // SUBLANES, ns2, a_ref, tab_ref, carry, reverse=False)
        for kb in range(nkb):
            cols = slice(kb * LANES, (kb + 1) * LANES)
            yk = _dot(xs[:, kb * half:(kb + 1) * half].astype(BF16), c_ref[kb, :half, :], NN)
            yk += _dot(xs[:, ns2 + kb * half:ns2 + (kb + 1) * half].astype(BF16), c_ref[kb, half:, :], NN)
            y_ref[:, cols] = yk + ds_ref[:, cols] * u_ref[:, cols].astype(F32)
        z = _gelu(y_ref[...])
        gate = _sigmoid(_dot(z.astype(BF16), wg_ref[...], NN) + bg_ref[...])
        ys_ref[...] = (z * gate).astype(BF16)

    row = pl.BlockSpec((t, w), lambda i: (i, 0))
    full = lambda shape: pl.BlockSpec(shape, lambda i: (0,) * len(shape))
    return pl.pallas_call(
        body, name=name, grid=(nblk,),
        in_specs=[row, full(b_blk.shape), full(c_blk.shape), full(a_f.shape), full(tab_f.shape), full(dskip.shape),
                  full(w_glu.shape), full(b_glu.shape)],
        out_specs=[row, row, pl.BlockSpec((1, 1, 2 * ns2), lambda i: (i, 0, 0))],
        out_shape=[jax.ShapeDtypeStruct((s, w), F32), jax.ShapeDtypeStruct((s, w), BF16),
                   jax.ShapeDtypeStruct((nblk, 1, 2 * ns2), F32)],
        scratch_shapes=[pltpu.VMEM((t, 2 * ns2), F32), pltpu.VMEM((1, 2 * ns2), F32)],
        compiler_params=_cparams(),
    )(u, b_blk, c_blk, a_f, tab_f, dskip, w_glu, b_glu)


def _s5_bwd(name, u, dys, y, carries, b_blk, c_blk, a_f, a_r, tab_f, tab_r, dskip, w_glu, b_glu):
    s, w = u.shape[0], w_glu.shape[0]
    nkb = w // LANES
    ns2 = b_blk.shape[2] // 2 * nkb
    half = ns2 // nkb
    t = min(S5_ROWS, s)
    nblk = s // t
    ng = t // SUBLANES

    def body(u_ref, dys_ref, y_ref, cs_ref, b_ref, c_ref, af_ref, ar_ref, tabf_ref, tabr_ref, ds_ref, wg_ref, bg_ref,
             du_ref, db_ref, dc_ref, da_ref, dwg_ref, vec_ref, xs, gs, dyv, fcarry, gcarry):
        @pl.when(pl.program_id(0) == 0)
        def _():
            db_ref[...] = jnp.zeros_like(db_ref)
            dc_ref[...] = jnp.zeros_like(dc_ref)
            da_ref[...] = jnp.zeros_like(da_ref)
            dwg_ref[...] = jnp.zeros_like(dwg_ref)
            vec_ref[...] = jnp.zeros_like(vec_ref)
            gcarry[...] = jnp.zeros_like(gcarry)

        yv = y_ref[...]
        z = _gelu(yv)
        zb = z.astype(BF16)
        gate = _sigmoid(_dot(zb, wg_ref[...], NN) + bg_ref[...])
        dout = dys_ref[...].astype(F32)
        dt = dout * z * gate * (1.0 - gate)
        dtb = dt.astype(BF16)
        dz = dout * gate + _dot(dtb, wg_ref[...], NT)
        dy = dz * _gelu_grad(yv)
        dyv[...] = dy
        dwg_ref[...] += _dot(zb, dtb, TN)
        vec_ref[0:1, :] += jnp.sum(dt, axis=0, keepdims=True)
        vec_ref[1:2, :] += jnp.sum(dy * u_ref[...].astype(F32), axis=0, keepdims=True)

        fcarry[...] = cs_ref[0]
        xs[0:SUBLANES, :] = jnp.broadcast_to(cs_ref[0], (SUBLANES, 2 * ns2))
        for kb in range(nkb):
            bu = _dot(u_ref[:, kb * LANES:(kb + 1) * LANES], b_ref[kb], NN)
            xs[SUBLANES:, kb * half:(kb + 1) * half] = bu[:, :half]
            xs[SUBLANES:, ns2 + kb * half:ns2 + (kb + 1) * half] = bu[:, half:]
        _scan_rows(xs, SUBLANES, ng, ns2, af_ref, tabf_ref, fcarry, reverse=False)
        first_segment = lax.broadcasted_iota(jnp.int32, (SUBLANES, 2 * ns2), 0) == 0
        xs[0:SUBLANES, :] = jnp.where(first_segment, xs[0:SUBLANES, :], pltpu.roll(xs[t:t + SUBLANES, :], 1, 0))

        for kb in range(nkb):
            dyk = dyv[:, kb * LANES:(kb + 1) * LANES].astype(BF16)
            re = slice(kb * half, (kb + 1) * half)
            im = slice(ns2 + kb * half, ns2 + (kb + 1) * half)
            gs[:, re] = _dot(dyk, c_ref[kb, :half, :], NT)
            gs[:, im] = _dot(dyk, c_ref[kb, half:, :], NT)
            dc_ref[kb, :half, :] += _dot(xs[SUBLANES:, re].astype(BF16), dyk, TN)
            dc_ref[kb, half:, :] += _dot(xs[SUBLANES:, im].astype(BF16), dyk, TN)

        def fold(c0, r, gr, gi, acc):
            wc = min(S5_CHUNK, ns2)
            re = slice(c0, c0 + wc)
            im = slice(ns2 + c0, ns2 + c0 + wc)
            if r is None:
                da_ref[:, re] += acc[0]
                da_ref[:, im] += acc[1]
                return acc
            before = pl.ds(pl.multiple_of(r * SUBLANES, SUBLANES), SUBLANES)
            xpr, xpi = xs[before, re], xs[before, im]
            return acc[0] + gr * xpr + gi * xpi, acc[1] - gr * xpi + gi * xpr

        _scan_rows(gs, 0, ng, ns2, ar_ref, tabr_ref, gcarry, reverse=True, fold=fold)

        for kb in range(nkb):
            cols = slice(kb * LANES, (kb + 1) * LANES)
            re = slice(kb * half, (kb + 1) * half)
            im = slice(ns2 + kb * half, ns2 + (kb + 1) * half)
            uk = u_ref[:, cols]
            gr = gs[:, re].astype(BF16)
            gi = gs[:, im].astype(BF16)
            db_ref[kb, :, :half] += _dot(uk, gr, TN)
            db_ref[kb, :, half:] += _dot(uk, gi, TN)
            duk = _dot(gr, b_ref[kb, :, :half], NT) + _dot(gi, b_ref[kb, :, half:], NT)
            du_ref[:, cols] = (duk + ds_ref[:, cols] * dyv[:, cols]).astype(BF16)

    rev = lambda i: (nblk - 1 - i, 0)
    row = pl.BlockSpec((t, w), rev)
    full = lambda shape: pl.BlockSpec(shape, lambda i: (0,) * len(shape))
    return pl.pallas_call(
        body, name=name, grid=(nblk,),
        in_specs=[row, row, row, pl.BlockSpec((1, 1, 2 * ns2), lambda i: (nblk - 1 - i, 0, 0)),
                  full(b_blk.shape), full(c_blk.shape), full(a_f.shape), full(a_r.shape), full(tab_f.shape),
                  full(tab_r.shape), full(dskip.shape), full(w_glu.shape), full(b_glu.shape)],
        out_specs=[row, full(b_blk.shape), full(c_blk.shape), full((SUBLANES, 2 * ns2)), full((w, w)),
                   full((SUBLANES, w))],
        out_shape=[jax.ShapeDtypeStruct((s, w), BF16), jax.ShapeDtypeStruct(b_blk.shape, F32),
                   jax.ShapeDtypeStruct(c_blk.shape, F32), jax.ShapeDtypeStruct((SUBLANES, 2 * ns2), F32),
                   jax.ShapeDtypeStruct((w, w), F32), jax.ShapeDtypeStruct((SUBLANES, w), F32)],
        scratch_shapes=[pltpu.VMEM((t + SUBLANES, 2 * ns2), F32), pltpu.VMEM((t, 2 * ns2), F32),
                        pltpu.VMEM((t, w), F32), pltpu.VMEM((1, 2 * ns2), F32), pltpu.VMEM((1, 2 * ns2), F32)],
        compiler_params=_cparams(),
    )(u, dys, y, carries, b_blk, c_blk, a_f, a_r, tab_f, tab_r, dskip, w_glu, b_glu)


def _log_sigmoid(x):
    return jnp.minimum(x, 0.0) - jnp.log(1.0 + jnp.exp(-jnp.abs(x)))


def _cum_fwd(name, f_t, b_f):
    h, s = f_t.shape
    tc = _pick(s, 512)
    nb = s // tc

    def body(f_ref, b_ref, c_ref, carry):
        @pl.when(pl.program_id(0) == 0)
        def _():
            carry[...] = jnp.zeros_like(carry)

        lf = _log_sigmoid(f_ref[...] + b_ref[...])
        upper = (lax.broadcasted_iota(jnp.int32, (tc, tc), 0) <= lax.broadcasted_iota(jnp.int32, (tc, tc), 1))
        cum = lax.dot_general(lf, upper.astype(F32), NN, precision=lax.Precision.HIGHEST,
                              preferred_element_type=F32) + carry[...]
        c_ref[...] = cum
        carry[...] += jnp.sum(lf, axis=1, keepdims=True)

    blk = pl.BlockSpec((h, tc), lambda i: (0, i))
    return pl.pallas_call(body, name=name, grid=(nb,), in_specs=[blk, pl.BlockSpec((h, 1), lambda i: (0, 0))],
                          out_specs=blk, out_shape=jax.ShapeDtypeStruct((h, s), F32),
                          scratch_shapes=[pltpu.VMEM((h, 1), F32)], compiler_params=_cparams())(f_t, b_f)


def _cum_bwd(name, dcq, dck, f_t, b_f):
    h, s = f_t.shape
    tc = _pick(s, 512)
    nb = s // tc

    def body(dcq_ref, dck_ref, f_ref, b_ref, df_ref, db_ref, carry):
        @pl.when(pl.program_id(0) == 0)
        def _():
            carry[...] = jnp.zeros_like(carry)
            db_ref[...] = jnp.zeros_like(db_ref)

        dc = dcq_ref[...] + dck_ref[...]
        lower = (lax.broadcasted_iota(jnp.int32, (tc, tc), 0) >= lax.broadcasted_iota(jnp.int32, (tc, tc), 1))
        dlf = lax.dot_general(dc, lower.astype(F32), NN, precision=lax.Precision.HIGHEST,
                              preferred_element_type=F32) + carry[...]
        carry[...] += jnp.sum(dc, axis=1, keepdims=True)
        df = dlf * _sigmoid(-(f_ref[...] + b_ref[...]))
        df_ref[...] = df
        db_ref[...] += jnp.broadcast_to(jnp.sum(df, axis=1, keepdims=True), db_ref.shape)

    blk = pl.BlockSpec((h, tc), lambda i: (0, nb - 1 - i))
    return pl.pallas_call(
        body, name=name, grid=(nb,), in_specs=[blk, blk, blk, pl.BlockSpec((h, 1), lambda i: (0, 0))],
        out_specs=[blk, pl.BlockSpec((h, LANES), lambda i: (0, 0))],
        out_shape=[jax.ShapeDtypeStruct((h, s), F32), jax.ShapeDtypeStruct((h, LANES), F32)],
        scratch_shapes=[pltpu.VMEM((h, 1), F32)], compiler_params=_cparams())(dcq, dck, f_t, b_f)


def _attn_fwd(name, qkv, q_blk, k_blk, v_blk, n_pairs, ck, side=None):
    s = qkv.shape[0]
    dh = LANES // 2
    t = min(ATT_BLOCK, s)
    nq = s // t
    scale = dh ** -0.5

    def body(q_ref, k_ref, v_ref, ck_ref, o_ref, lse_ref, m_s, acc_s):
        i = pl.program_id(1)
        low = lax.broadcasted_iota(jnp.int32, (1, LANES), 1) < dh
        qs = (q_ref[...].astype(F32) * scale).astype(BF16)
        zero = jnp.zeros_like(qs)
        qh = (jnp.where(low, qs, zero), jnp.where(low, zero, qs))
        m_s[...] = jnp.full(m_s.shape, -1e30, F32)
        acc_s[...] = jnp.zeros_like(acc_s)
        causal = (lax.broadcasted_iota(jnp.int32, (t, t), 1) <= lax.broadcasted_iota(jnp.int32, (t, t), 0))

        def step(j, diagonal):
            r0 = pl.multiple_of(j * t, t)
            kj = k_ref[pl.ds(r0, t), :]
            vj = v_ref[pl.ds(r0, t), :]
            one = jnp.ones_like(vj)
            vh = (jnp.where(low, vj, one), jnp.where(low, one, vj))
            for hd in range(2):
                sc = _dot(qh[hd], kj, NT) - ck_ref[hd, j]
                if diagonal:
                    sc = jnp.where(causal, sc, -1e30)
                m_old = m_s[hd]
                m_new = jnp.maximum(m_old, jnp.max(sc, axis=1, keepdims=True))
                p = jnp.exp(sc - m_new)
                acc_s[hd] = jnp.exp(m_old - m_new) * acc_s[hd] + _dot(p.astype(BF16), vh[hd], NN)
                m_s[hd] = m_new

        def full(j, _):
            step(j, False)
            return 0

        lax.fori_loop(0, i, full, 0)
        step(i, True)
        a0, a1 = acc_s[0], acc_s[1]
        o_ref[...] = jnp.where(low, a0 / pltpu.roll(a0, dh, 1), a1 / pltpu.roll(a1, dh, 1)).astype(BF16)
        lse_ref[0] = m_s[0] + jnp.log(a0[:, dh:dh + 1])
        lse_ref[1] = m_s[1] + jnp.log(a1[:, 0:1])

    return _hosted_call(
        body, side, name, (n_pairs, nq),
        [pl.BlockSpec((t, LANES), lambda hp, i: (i, q_blk + hp)),
         pl.BlockSpec((s, LANES), lambda hp, i: (0, k_blk + hp)),
         pl.BlockSpec((s, LANES), lambda hp, i: (0, v_blk + hp)),
         pl.BlockSpec((2, nq, 1, t), lambda hp, i: (hp, 0, 0, 0))],
        [pl.BlockSpec((t, LANES), lambda hp, i: (i, hp)), pl.BlockSpec((2, t, 1), lambda hp, i: (hp, i, 0))],
        [jax.ShapeDtypeStruct((s, LANES * n_pairs), BF16), jax.ShapeDtypeStruct((2 * n_pairs, s, 1), F32)],
        [pltpu.VMEM((2, t, 1), F32), pltpu.VMEM((2, t, LANES), F32)], (qkv, qkv, qkv, ck))


def _attn_bwd(name, qkv, q_blk, k_blk, v_blk, n_pairs, o, do, lse_rows, ck_cols, side=None):
    s = qkv.shape[0]
    dh = LANES // 2
    t = min(ATT_BLOCK, s)
    nk = s // t
    scale = dh ** -0.5

    def body(q_ref, k_ref, v_ref, o_ref, do_ref, lse_ref, ck_ref,
             dq_ref, dk_ref, dv_ref, dcq_ref, dck_ref, delta, dqt, dk_acc, dv_acc):
        j = pl.program_id(1)
        low = lax.broadcasted_iota(jnp.int32, (1, LANES), 1) < dh
        low_rows = lax.broadcasted_iota(jnp.int32, (LANES, 1), 0) < dh

        @pl.when(j == 0)
        def _():
            dqt[...] = jnp.zeros_like(dqt)
            sel = (jnp.broadcast_to(low, (SUBLANES, LANES)).astype(F32), jnp.broadcast_to(~low, (SUBLANES, LANES)).astype(F32))

            def fill(i, _):
                r0 = pl.multiple_of(i * t, t)
                prod = do_ref[pl.ds(r0, t), :].astype(F32) * o_ref[pl.ds(r0, t), :].astype(F32)
                for hd in range(2):
                    delta[hd, i] = lax.dot_general(sel[hd], prod, NT, precision=lax.Precision.HIGHEST,
                                                   preferred_element_type=F32)
                return 0

            lax.fori_loop(0, nk, fill, 0)

        kj, vj = k_ref[...], v_ref[...]
        zero, one = jnp.zeros_like(kj), jnp.ones_like(kj)
        kh = (jnp.where(low, kj, zero), jnp.where(low, zero, kj))
        vh = (jnp.where(low, vj, zero), jnp.where(low, zero, vj))
        kjt = kj.astype(F32).T.astype(BF16)
        one_t = jnp.ones_like(kjt)
        kht = (jnp.where(low_rows, kjt, one_t), jnp.where(low_rows, one_t, kjt))
        dk_acc[...] = jnp.zeros_like(dk_acc)
        dv_acc[...] = jnp.zeros_like(dv_acc)
        causal_t = (lax.broadcasted_iota(jnp.int32, (t, t), 0) <= lax.broadcasted_iota(jnp.int32, (t, t), 1))

        def step(i, diagonal):
            r0 = pl.multiple_of(i * t, t)
            qi = (q_ref[pl.ds(r0, t), :].astype(F32) * scale).astype(BF16)
            doi = do_ref[pl.ds(r0, t), :]
            qone, dzero = jnp.ones_like(qi), jnp.zeros_like(doi)
            qsel = (jnp.where(low, qi, qone), jnp.where(low, qone, qi))
            dosel = (jnp.where(low, doi, dzero), jnp.where(low, dzero, doi))
            for hd in range(2):
                st = _dot(kh[hd], qi, NT) - ck_ref[hd] - lse_ref[hd, i]
                pt = jnp.exp(st)
                if diagonal:
                    pt = jnp.where(causal_t, pt, 0.0)
                dst = pt * (_dot(vh[hd], doi, NT) - delta[hd, i, 0:1, :])
                dsb = dst.astype(BF16)
                dv_acc[...] += _dot(pt.astype(BF16), dosel[hd], NN)
                dk_acc[hd] += _dot(dsb, qsel[hd], NN)
                dqt[hd, i] += _dot(kht[hd], dsb, NN)

        step(j, True)

        def rest(i, _):
            step(i, False)
            return 0

        lax.fori_loop(j + 1, nk, rest, 0)
        dk_ref[...] = jnp.where(low, dk_acc[0], dk_acc[1]).astype(BF16)
        dv_ref[...] = dv_acc[...].astype(BF16)
        dck_ref[0] = -dk_acc[0][:, dh:dh + 1]
        dck_ref[1] = -dk_acc[1][:, 0:1]

        @pl.when(j == nk - 1)
        def _():
            def emit(i, _):
                r0 = pl.multiple_of(i * t, t)
                d0, d1 = dqt[0, i], dqt[1, i]
                dq_ref[pl.ds(r0, t), :] = (jnp.where(low_rows, d0, d1) * scale).T.astype(BF16)
                dcq_ref[0, i] = d0[dh:dh + 1, :]
                dcq_ref[1, i] = d1[0:1, :]
                return 0

            lax.fori_loop(0, nk, emit, 0)

    col_blk = lambda base: pl.BlockSpec((t, LANES), lambda hp, j: (j, base + hp))
    col_all = lambda base: pl.BlockSpec((s, LANES), lambda hp, j: (0, base + hp))
    rows_all = pl.BlockSpec((2, nk, 1, t), lambda hp, j: (hp, 0, 0, 0))
    return _hosted_call(
        body, side, name, (n_pairs, nk),
        [col_all(q_blk), col_blk(k_blk), col_blk(v_blk), col_all(0), col_all(0), rows_all,
         pl.BlockSpec((2, t, 1), lambda hp, j: (hp, j, 0))],
        [col_all(0), col_blk(0), col_blk(0), rows_all, pl.BlockSpec((2, t, 1), lambda hp, j: (hp, j, 0))],
        [jax.ShapeDtypeStruct((s, LANES * n_pairs), BF16), jax.ShapeDtypeStruct((s, LANES * n_pairs), BF16),
         jax.ShapeDtypeStruct((s, LANES * n_pairs), BF16), jax.ShapeDtypeStruct((2 * n_pairs, nk, 1, t), F32),
         jax.ShapeDtypeStruct((2 * n_pairs, s, 1), F32)],
        [pltpu.VMEM((2, nk, SUBLANES, t), F32), pltpu.VMEM((2, nk, LANES, t), F32),
         pltpu.VMEM((2, t, LANES), F32), pltpu.VMEM((t, LANES), F32)],
        (qkv, qkv, qkv, o, do, lse_rows, ck_cols))


def _adamw(name, w, g, m, v):
    n_l, r, c = w.shape
    tr = _pick8(r, max(SUBLANES, (1 << 20) // (4 * c)))

    def body(w_ref, g_ref, m_ref, v_ref, d_ref, mo_ref, vo_ref):
        gv = g_ref[...]
        m2 = ADAM_B1 * m_ref[...] + (1.0 - ADAM_B1) * gv
        v2 = ADAM_B2 * v_ref[...] + (1.0 - ADAM_B2) * (gv * gv)
        m_hat = m2 / (1.0 - ADAM_B1 ** ADAM_STEP)
        v_hat = v2 / (1.0 - ADAM_B2 ** ADAM_STEP)
        d_ref[...] = -ADAM_LR * (m_hat / (jnp.sqrt(v_hat) + ADAM_EPS) + ADAM_WD * w_ref[...])
        mo_ref[...] = m2
        vo_ref[...] = v2

    blk = pl.BlockSpec((None, tr, c), lambda l, i: (l, i, 0))
    sh = jax.ShapeDtypeStruct((n_l, r, c), F32)
    return pl.pallas_call(body, name=name, grid=(n_l, r // tr), in_specs=[blk] * 4, out_specs=[blk] * 3,
                          out_shape=[sh, sh, sh], compiler_params=_cparams())(w, g, m, v)


def _pick8(dim, target, mult=SUBLANES):
    best, t = None, mult
    while t <= min(dim, target):
        if dim % t == 0:
            best = t
        t += mult
    return best or dim


BF16_ROWS = 16


def _sum_blocks(name, x, out_dtype):
    n, r, c = x.shape
    tr = _pick8(r, max(BF16_ROWS, (1 << 19) // (4 * c)), BF16_ROWS)

    def body(x_ref, o_ref):
        acc = x_ref[0].astype(F32)
        for i in range(1, n):
            acc = acc + x_ref[i].astype(F32)
        o_ref[...] = acc.astype(out_dtype)

    return pl.pallas_call(body, name=name, grid=(r // tr,),
                          in_specs=[pl.BlockSpec((n, tr, c), lambda i: (0, i, 0))],
                          out_specs=pl.BlockSpec((tr, c), lambda i: (i, 0)),
                          out_shape=jax.ShapeDtypeStruct((r, c), out_dtype), compiler_params=_cparams())(x)


def _all_gather(name, x_shard):
    m_per, n = x_shard.shape

    def body(x_ref, out_ref, send_sems, recv_sems):
        x, y, c = lax.axis_index("x"), lax.axis_index("y"), lax.axis_index("c")
        me, sibling = (x, y, c), (x, y, 1 - c)
        chips = [(1 - x, y), (x, 1 - y), (1 - x, 1 - y)]

        def rows(px, py, pc):
            return out_ref.at[pl.ds((4 * px + 2 * py + pc) * m_per, m_per), :]

        def copy(k, block, to, src=None):
            return pltpu.make_async_remote_copy(
                src_ref=rows(*block) if src is None else src, dst_ref=rows(*block),
                send_sem=send_sems.at[k], recv_sem=recv_sems.at[k], device_id=to, device_id_type=MESH)

        first = [copy(0, me, sibling, src=x_ref)]
        first += [copy(1 + j, me, (*chip, c), src=x_ref) for j, chip in enumerate(chips)]
        for cp in first:
            cp.start()
        passed = [copy(4 + j, (*chip, c), sibling) for j, chip in enumerate(chips)]
        for j, chip in enumerate(chips):
            copy(1 + j, (*chip, c), me).wait_recv()
            passed[j].start()
        copy(0, sibling, me).wait_recv()
        for j, chip in enumerate(chips):
            copy(4 + j, (*chip, 1 - c), me).wait_recv()
        for cp in first + passed:
            cp.wait_send()

    out = pl.pallas_call(
        body, name=name, out_shape=jax.ShapeDtypeStruct((N_DEV * m_per, n), x_shard.dtype),
        in_specs=[pl.BlockSpec(memory_space=pl.ANY)], out_specs=pl.BlockSpec(memory_space=pl.ANY),
        scratch_shapes=[pltpu.SemaphoreType.DMA((7,)), pltpu.SemaphoreType.DMA((7,))],
    )(x_shard)
    my_dev = 4 * lax.axis_index("x") + 2 * lax.axis_index("y") + lax.axis_index("c")
    return lax.dynamic_update_slice(out, x_shard, (my_dev * m_per, 0))


def _put_own(out, own, index):
    start = tuple(index) + (0,) * own.ndim
    return lax.dynamic_update_slice(out, own.reshape((1,) * len(index) + own.shape), start)


def _gather_copies(stage, ins, outs, send_sems, recv_sems):
    x, y, c = lax.axis_index("x"), lax.axis_index("y"), lax.axis_index("c")
    my_chip = 2 * x + y
    copies = []
    for w, out in enumerate(outs):
        half = out.shape[1] // 2
        rows = pl.ds(c * half, half)
        for k, (cx, cy) in enumerate([(1 - x, y), (x, 1 - y), (1 - x, 1 - y)]):
            if stage == 0:
                src, dst, to = ins[w].at[rows], out.at[my_chip, rows], (cx, cy, c)
            else:
                src = dst = out.at[2 * cx + cy, rows]
                to = (x, y, 1 - c)
            copies.append(pltpu.make_async_remote_copy(
                src_ref=src, dst_ref=dst, send_sem=send_sems.at[3 * w + k], recv_sem=recv_sems.at[3 * w + k],
                device_id=to, device_id_type=MESH))
    return copies


def _gathered_shapes(shards):
    return [jax.ShapeDtypeStruct((N_CHIPS,) + s.shape, s.dtype) for s in shards]


def _put_own_slabs(gathered, shards):
    my_chip = 2 * lax.axis_index("x") + lax.axis_index("y")
    return [_put_own(o, s, (my_chip,)) for o, s in zip(gathered, shards)]


def _gather_layer(name, shards):
    n_w = len(shards)

    def body(*refs):
        ins, outs = refs[:n_w], refs[n_w:2 * n_w]
        for stage in (0, 1):
            copies = _gather_copies(stage, ins, outs, refs[2 * n_w + 2 * stage], refs[2 * n_w + 2 * stage + 1])
            for cp in copies:
                cp.start()
            for cp in copies:
                cp.wait()

    outs = pl.pallas_call(
        body, name=name, out_shape=_gathered_shapes(shards),
        in_specs=[pl.BlockSpec(memory_space=pl.ANY)] * n_w, out_specs=[pl.BlockSpec(memory_space=pl.ANY)] * n_w,
        scratch_shapes=[pltpu.SemaphoreType.DMA((3 * n_w,))] * 4,
    )(*shards)
    return _put_own_slabs(outs, shards)


def _gather_side_jobs(shards):
    n_w = len(shards)
    between_chips = _SideJob(list(shards), _gathered_shapes(shards), {}, 3 * n_w,
                             lambda ins, outs, send, recv: _gather_copies(0, ins, outs, send, recv))
    between_cores = lambda partial: _SideJob(list(partial), _gathered_shapes(shards), {w: w for w in range(n_w)}, 3 * n_w,
                                             lambda ins, outs, send, recv: _gather_copies(1, ins, outs, send, recv))
    return between_chips, between_cores


def _run_job(name, job):
    n_in, n_out = len(job.arrays), len(job.out_shapes)

    def body(*refs):
        copies = job.copies(refs[:n_in], refs[n_in:n_in + n_out], refs[n_in + n_out], refs[n_in + n_out + 1])
        for cp in copies:
            cp.start()
        for cp in copies:
            cp.wait()

    hbm = pl.BlockSpec(memory_space=pl.ANY)
    return pl.pallas_call(
        body, name=name, out_shape=list(job.out_shapes), in_specs=[hbm] * n_in, out_specs=[hbm] * n_out,
        scratch_shapes=[pltpu.SemaphoreType.DMA((job.n_sems,))] * 2, input_output_aliases=dict(job.aliases),
    )(*job.arrays)


def _swap_job(grads):
    def copies(ins, outs, send_sems, recv_sems):
        x, y, c = lax.axis_index("x"), lax.axis_index("y"), lax.axis_index("c")
        return [pltpu.make_async_remote_copy(
            src_ref=g.at[:, pl.ds((1 - c) * (g.shape[1] // 2), g.shape[1] // 2)], dst_ref=outs[w],
            send_sem=send_sems.at[w], recv_sem=recv_sems.at[w], device_id=(x, y, 1 - c), device_id_type=MESH)
            for w, g in enumerate(ins)]

    shapes = [jax.ShapeDtypeStruct((g.shape[0], g.shape[1] // 2, g.shape[2]), g.dtype) for g in grads]
    return _SideJob(list(grads), shapes, {}, len(grads), copies)


def _exchange_job(parts):
    def copies(ins, outs, send_sems, recv_sems):
        x, y, c = lax.axis_index("x"), lax.axis_index("y"), lax.axis_index("c")
        return [pltpu.make_async_remote_copy(
            src_ref=ins[w].at[2 * cx + cy], dst_ref=outs[w].at[2 * x + y], send_sem=send_sems.at[3 * w + k],
            recv_sem=recv_sems.at[3 * w + k], device_id=(cx, cy, c), device_id_type=MESH)
            for w in range(len(ins)) for k, (cx, cy) in enumerate([(1 - x, y), (x, 1 - y), (1 - x, 1 - y)])]

    return _SideJob(list(parts), [jax.ShapeDtypeStruct(p.shape, p.dtype) for p in parts], {}, 3 * len(parts), copies)


def _share_job(reduced, layer, depth, into):
    n_w = len(reduced)

    def copies(ins, outs, send_sems, recv_sems):
        x, y, c = lax.axis_index("x"), lax.axis_index("y"), lax.axis_index("c")
        return [pltpu.make_async_remote_copy(
            src_ref=ins[w], dst_ref=outs[w].at[layer, pl.ds(c * ins[w].shape[0], ins[w].shape[0])],
            send_sem=send_sems.at[w], recv_sem=recv_sems.at[w], device_id=(x, y, 1 - c), device_id_type=MESH)
            for w in range(n_w)]

    shapes = [jax.ShapeDtypeStruct((depth, 2 * r.shape[0], r.shape[1]), r.dtype) for r in reduced]
    if into is None:
        return _SideJob(list(reduced), shapes, {}, n_w, copies)
    return _SideJob(list(reduced) + list(into), shapes, {n_w + w: w for w in range(n_w)}, n_w, copies)


def _add_rows(name, grads, recv, core):
    n, r, c = recv.shape
    tr = _pick8(r, max(BF16_ROWS, (1 << 19) // (4 * c)), BF16_ROWS)
    steps = r // tr

    def body(core_ref, g_ref, r_ref, o_ref):
        o_ref[...] = (g_ref[...].astype(F32) + r_ref[...].astype(F32)).astype(BF16)

    grid_spec = pltpu.PrefetchScalarGridSpec(
        num_scalar_prefetch=1, grid=(steps,),
        in_specs=[pl.BlockSpec((n, tr, c), lambda i, core_ref: (0, core_ref[0] * steps + i, 0)),
                  pl.BlockSpec((n, tr, c), lambda i, core_ref: (0, i, 0))],
        out_specs=pl.BlockSpec((n, tr, c), lambda i, core_ref: (0, i, 0)))
    return pl.pallas_call(body, name=name, grid_spec=grid_spec,
                          out_shape=jax.ShapeDtypeStruct((n, r, c), BF16), compiler_params=_cparams())(core, grads, recv)


class _LayerReduce:
    def __init__(self, tag, layer, depth, grads, core, into):
        self.tag, self.layer, self.depth, self.core, self.into = tag, layer, depth, core, into
        self.state = list(grads)

    def _exchange(self, name, job, carry):
        if carry is None:
            return None, _run_job(f"{name}_{self.tag}", job)
        return carry(job)

    def swap_and_add(self, carry=None):
        grads = self.state
        results, recv = self._exchange("grads_swap_cores", _swap_job(grads), carry)
        self.state = [_add_rows(f"grads_add_{n}_{self.tag}", g, r, self.core) for n, g, r in zip(BIG, grads, recv)]
        return results

    def exchange_and_sum(self, carry=None):
        parts = self.state
        results, arrived = self._exchange("grads_exchange_chips", _exchange_job(parts), carry)
        my_chip = 2 * lax.axis_index("x") + lax.axis_index("y")
        arrived = [_put_own(a, lax.dynamic_index_in_dim(p, my_chip, 0, keepdims=False), (my_chip,))
                   for a, p in zip(arrived, parts)]
        self.state = [_sum_blocks(f"grads_sum_{n}_{self.tag}", a, F32) for n, a in zip(BIG, arrived)]
        return results

    def share(self, carry=None):
        reduced = self.state
        results, outs = self._exchange("grads_share_cores", _share_job(reduced, self.layer, self.depth, self.into), carry)
        self.state =[lax.dynamic_update_slice(o, r[None], (self.layer, lax.axis_index("c") * r.shape[0], 0))
                      for o, r in zip(outs, reduced)]
        return results


def _rows_of(shape, cols):
    return -(-math.prod(shape) // cols)


def _pack(arrays, cols, row_multiple, dtype):
    blocks = []
    for a in arrays:
        flat = a.reshape(-1).astype(dtype)
        rows = _rows_of(a.shape, cols)
        blocks.append(jnp.pad(flat, (0, rows * cols - flat.shape[0])).reshape(rows, cols))
    total = sum(b.shape[0] for b in blocks)
    blocks.append(jnp.zeros((-total % row_multiple, cols), dtype))
    return jnp.concatenate(blocks, axis=0)


def _unpack(buf, shapes):
    out, off = [], 0
    for sh in shapes:
        rows = _rows_of(sh, buf.shape[1])
        out.append(buf[off:off + rows].reshape(-1)[:math.prod(sh)].reshape(sh))
        off += rows
    return out


def _discretize(lam_re, lam_im, log_dt, b_re, b_im):
    lam = lax.complex(jnp.minimum(lam_re, -EIG_CLIP), lam_im)
    dt = jnp.exp(log_dt)[:, None]
    lam_bar = jnp.exp(lam * dt)
    b_bar = ((lam_bar - 1.0) / lam)[..., None] * lax.complex(b_re, b_im)
    return jnp.real(lam_bar), jnp.imag(lam_bar), jnp.real(b_bar), jnp.imag(b_bar)


def _scan_tables(ar, ai):
    a = lax.complex(ar, ai)
    pw = [a]
    for _ in range(7):
        pw.append(pw[-1] * a)
    rows = jnp.arange(SUBLANES)[:, None]

    def build(p, reverse):
        tabs = []
        for k in (1, 2, 4):
            keep = (rows <= SUBLANES - 1 - k) if reverse else (rows >= k)
            tk = jnp.where(keep, p[k - 1][None, :], 0.0)
            tabs += [jnp.real(tk), jnp.imag(tk)]
        stack = jnp.stack(p[::-1] if reverse else p)
        tabs += [jnp.real(stack), jnp.imag(stack)]
        return jnp.stack(tabs).astype(F32)

    return build(pw, False), build([jnp.conj(p) for p in pw], True)


def _interleave_rows(a, t):
    s, w = a.shape
    return a.reshape(s // t, SUBLANES, t // SUBLANES, w).transpose(0, 2, 1, 3).reshape(s, w)


def _deinterleave_rows(a, t):
    s, w = a.shape
    return a.reshape(s // t, t // SUBLANES, SUBLANES, w).transpose(0, 2, 1, 3).reshape(s, w)


def _block_diag(per_group, groups_per_block):
    g, a, b = per_group.shape
    x = per_group.reshape(g // groups_per_block, groups_per_block, a, b)
    eye = jnp.eye(groups_per_block, dtype=per_group.dtype)
    out = x[:, :, :, None, :] * eye[None, :, None, :, None]
    return out.reshape(g // groups_per_block, groups_per_block * a, groups_per_block * b)


def _block_diag_extract(dense, groups_per_block, a, b):
    nkb = dense.shape[0]
    x = dense.reshape(nkb, groups_per_block, a, groups_per_block, b)
    idx = jnp.arange(groups_per_block)
    return x[:, idx, :, idx, :].transpose(1, 0, 2, 3).reshape(nkb * groups_per_block, a, b)


def _layer_fwd(tag, x, mod, p, wts, gather_next=None):
    s, d = x.shape
    w_ssm, w_att = p["w_glu"].shape[0], wts["w_pb"].shape[1]
    heads = p["b_f"].shape[0]
    dh = w_att // heads
    cs = d // N_CHIPS
    fs = wts["w_ffn_down"].shape[1]
    tm = _pick(s, 1024)
    row = lambda v: v.reshape(1, -1)
    sv = {}

    h = _prenorm_fwd(f"prenorm_mix_{tag}", x, row(p["g_pre_mix"]), row(mod[1]), row(mod[0]))
    uqkv = _mm_plain(f"proj_main_{tag}", h, p["w_main"], "nn", BF16, tm=1024, tn=1024, tk=1024)
    fg = _mm_plain(f"proj_gate_{tag}", h, p["w_gates"], "nn", F32, tm=1024, tn=1024, tk=1024)
    f_t = fg[:, 2 * d:2 * d + heads].T

    t5 = min(S5_ROWS, s)
    u_il = _interleave_rows(uqkv[:, :w_ssm], t5)
    y_s5, ys_il, carries = _s5_fwd(f"s5_fwd_{tag}", u_il, p["b_blk"], p["c_blk"], p["a_f"], p["tab_f"],
                                   row(p["d_skip"]), p["w_glu"], row(p["b_glu"]))
    ys = _deinterleave_rows(ys_il, t5)

    assert dh * 2 == LANES and w_ssm % LANES == 0 and w_att % LANES == 0
    n_pairs = w_att // LANES
    blocks = (w_ssm // LANES, w_ssm // LANES + n_pairs, w_ssm // LANES + 2 * n_pairs)
    cum = _cum_fwd(f"cum_fwd_{tag}", f_t, p["b_f"].reshape(heads, 1))
    t = min(ATT_BLOCK, s)
    ck_cols, ck_rows = cum.reshape(heads, s, 1), cum.reshape(heads, s // t, 1, t)
    (ya, lse), arrived = _attn_fwd(f"attn_fwd_{tag}", uqkv, *blocks, n_pairs, ck_rows,
                                   side=gather_next[0] if gather_next else None)

    tile = pl.BlockSpec((tm, cs), lambda i, j, k: (i, j))
    slab = lambda rows: pl.BlockSpec((None, rows, cs), lambda i, j, k: (j, 0, 0))

    def merge(acc, extra_refs, out_refs):
        ya_ref, wpb_ref, ga_ref, gb_ref = extra_refs
        a_ref, b_ref, m_ref = out_refs
        bv = _dot(ya_ref[...], wpb_ref[...], NN)
        a_ref[...] = acc.astype(BF16)
        b_ref[...] = bv.astype(BF16)
        m_ref[...] = (_sigmoid(ga_ref[...]) * acc + _sigmoid(gb_ref[...]) * bv).astype(BF16)

    sd_bf = jax.ShapeDtypeStruct((s, d), BF16)
    pa, pb, merged = _mm_raw(
        f"merge_{tag}", ys, wts["w_pa"], "nn", (s // tm, N_CHIPS, 1), (tm, cs),
        pl.BlockSpec((tm, w_ssm), lambda i, j, k: (i, 0)), slab(w_ssm), [sd_bf] * 3, [tile] * 3, merge,
        extra=(ya, wts["w_pb"], fg, fg),
        extra_specs=[pl.BlockSpec((tm, w_att), lambda i, j, k: (i, 0)), slab(w_att), tile,
                     pl.BlockSpec((tm, cs), lambda i, j, k: (i, j + N_CHIPS))])

    tm2 = _pick(s, POSTNORM_ROWS)
    x1, y_mix = _mm_postnorm(
        f"out_proj_{tag}", merged, pl.BlockSpec((tm2, cs), lambda i, j, k: (i, k)), wts["w_o"],
        pl.BlockSpec((None, cs, d), lambda i, j, k: (k, 0, 0)), N_CHIPS, x, row(mod[2]), row(p["g_post_mix"]))

    h2 = _prenorm_fwd(f"prenorm_ffn_{tag}", x1, row(p["g_pre_ffn"]), row(mod[4]), row(mod[3]))
    (a4, b4, hid4), next_wts = _ffn_up(f"ffn_up_{tag}", h2, wts["w_ffn_gate"], wts["w_ffn_up"],
                                       side=gather_next[1](arrived) if gather_next else None)
    x2, y_ffn = _mm_postnorm(
        f"ffn_down_{tag}", hid4, pl.BlockSpec((None, tm2, fs), lambda i, j, k: (k, i, 0)), wts["w_ffn_down"],
        pl.BlockSpec((None, fs, d), lambda i, j, k: (k, 0, 0)), N_CHIPS, x1, row(mod[5]), row(p["g_post_ffn"]))

    sv.update(x=x, h=h, uqkv=uqkv, u_il=u_il, fg=fg, f_t=f_t, y_s5=y_s5, ys=ys, carries=carries, blocks=blocks,
              ck_cols=ck_cols, lse_rows=lse.reshape(heads, s // t, 1, t), ya=ya, pa=pa, pb=pb, merged=merged, x1=x1,
              y_mix=y_mix, h2=h2, a4=a4, b4=b4, hid4=hid4, y_ffn=y_ffn)
    return x2, sv, next_wts


def _mm_postnorm(name, a, a_spec, w, w_spec, nk, x, gate, g):
    s, d = x.shape
    tm = _pick(s, POSTNORM_ROWS)
    rowspec = pl.BlockSpec((tm, d), lambda i, j, k: (i, 0))
    vec = pl.BlockSpec((1, d), lambda i, j, k: (0, 0))

    def epilogue(acc, extra_refs, out_refs):
        x_ref, gate_ref, g_ref = extra_refs
        r = lax.rsqrt(jnp.mean(acc * acc, axis=-1, keepdims=True) + RMS_EPS)
        out_refs[0][...] = x_ref[...] + gate_ref[...] * (acc * r * g_ref[...])
        out_refs[1][...] = acc

    sd = jax.ShapeDtypeStruct((s, d), F32)
    return _mm_raw(name, a, w, "nn", (s // tm, 1, nk), (tm, d), a_spec, w_spec, [sd, sd], [rowspec, rowspec], epilogue,
                   extra=(x, gate, g), extra_specs=[rowspec, vec, vec])


def _layer_bwd(tag, dx2, mod, p, wts, sv, reduce_later=None):
    s, d = dx2.shape
    w_ssm, w_att = p["w_glu"].shape[0], wts["w_pb"].shape[1]
    heads = p["b_f"].shape[0]
    cs = d // N_CHIPS
    fs = wts["w_ffn_down"].shape[1]
    tm, tk, td = _pick(s, 1024), _pick(s, 1024), d
    row = lambda v: v.reshape(1, -1)
    gr = {}

    def dw_slabs(name, act, act_spec, rows, dy, dy_spec, cols, grid_mn, out_index):
        return _mm_raw(name, act, dy, "tn", grid_mn + (s // tk,), (rows, cols), act_spec, dy_spec,
                       [jax.ShapeDtypeStruct((N_CHIPS,) + out_index[1], BF16)],
                       [pl.BlockSpec((None, rows, cols), out_index[0])], _store(BF16))[0]

    dy_ffn, sums = _postnorm_bwd(f"postnorm_bwd_ffn_{tag}", dx2, sv["y_ffn"], row(p["g_post_ffn"]), row(mod[5]))
    d_gate_f, gr["g_post_ffn"] = sums[0], sums[1]
    gr["w_ffn_down"] = dw_slabs(f"dw_down_{tag}", sv["hid4"], pl.BlockSpec((None, tk, fs), lambda i, j, k: (i, k, 0)), fs,
                                dy_ffn, pl.BlockSpec((tk, d), lambda i, j, k: (k, 0)), d, (N_CHIPS, 1),
                                (lambda i, j, k: (i, 0, 0), (fs, d)))

    def swiglu_bwd(acc, extra_refs, out_refs):
        av, bv = extra_refs[0][...].astype(F32), extra_refs[1][...].astype(F32)
        sg = _sigmoid(av)
        out_refs[0][...] = (acc * bv * (sg * (1.0 + av * (1.0 - sg)))).astype(BF16)
        out_refs[1][...] = (acc * (av * sg)).astype(BF16)

    blk4 = pl.BlockSpec((None, tm, fs), lambda i, j, k: (j, i, 0))
    sh4 = jax.ShapeDtypeStruct((N_CHIPS, s, fs), BF16)
    ffn_down_bwd = lambda side: _mm_raw(
        f"ffn_down_bwd_{tag}", dy_ffn, wts["w_ffn_down"], "nt", (s // tm, N_CHIPS, 1), (tm, fs),
        pl.BlockSpec((tm, d), lambda i, j, k: (i, 0)), pl.BlockSpec((None, fs, d), lambda i, j, k: (j, 0, 0)),
        [sh4, sh4], [blk4, blk4], swiglu_bwd, extra=(sv["a4"], sv["b4"]), extra_specs=[blk4, blk4], side=side)
    da4, db4 = reduce_later.swap_and_add(ffn_down_bwd) if reduce_later else ffn_down_bwd(None)
    for n, act4 in (("w_ffn_gate", da4), ("w_ffn_up", db4)):
        gr[n] = dw_slabs(f"d{n}_{tag}", sv["h2"], pl.BlockSpec((tk, td), lambda i, j, k: (k, i)), td,
                         act4, pl.BlockSpec((None, tk, fs), lambda i, j, k: (j, k, 0)), fs, (d // td, N_CHIPS),
                         (lambda i, j, k: (j, i, 0), (d, fs)))
    pairs = [(act4, (None, tm, fs), lambda i, kk: (kk, i, 0), wts[n], (None, td, fs), lambda j, kk: (kk, j, 0),
              N_CHIPS) for n, act4 in (("w_ffn_gate", da4), ("w_ffn_up", db4))]
    dh2 = _mm_sum(f"dh_ffn_{tag}", s, d, tm, td, pairs, F32)
    dx1, sums = _prenorm_bwd(f"prenorm_bwd_ffn_{tag}", dh2, sv["x1"], row(p["g_pre_ffn"]), row(mod[4]), dx2)
    d_scale_f, d_shift_f, gr["g_pre_ffn"] = sums[0], sums[1], sums[2]

    dy_mix, sums = _postnorm_bwd(f"postnorm_bwd_mix_{tag}", dx1, sv["y_mix"], row(p["g_post_mix"]), row(mod[2]))
    d_gate_m, gr["g_post_mix"] = sums[0], sums[1]
    gr["w_o"] = dw_slabs(f"dw_o_{tag}", sv["merged"], pl.BlockSpec((tk, cs), lambda i, j, k: (k, i)), cs,
                         dy_mix, pl.BlockSpec((tk, d), lambda i, j, k: (k, 0)), d, (N_CHIPS, 1),
                         (lambda i, j, k: (i, 0, 0), (cs, d)))

    tile = pl.BlockSpec((tm, cs), lambda i, j, k: (i, j))

    def merge_bwd(acc, extra_refs, out_refs):
        a_ref, b_ref, ga_ref, gb_ref = extra_refs
        sa, sb = _sigmoid(ga_ref[...]), _sigmoid(gb_ref[...])
        out_refs[0][...] = (acc * sa).astype(BF16)
        out_refs[1][...] = (acc * sb).astype(BF16)
        out_refs[2][...] = (acc * a_ref[...].astype(F32) * sa * (1.0 - sa)).astype(BF16)
        out_refs[3][...] = (acc * b_ref[...].astype(F32) * sb * (1.0 - sb)).astype(BF16)

    sd_bf = jax.ShapeDtypeStruct((s, d), BF16)
    d_pa, d_pb, d_ga, d_gb = _mm_raw(
        f"out_proj_bwd_{tag}", dy_mix, wts["w_o"], "nt", (s // tm, N_CHIPS, 1), (tm, cs),
        pl.BlockSpec((tm, d), lambda i, j, k: (i, 0)), pl.BlockSpec((None, cs, d), lambda i, j, k: (j, 0, 0)),
        [sd_bf] * 4, [tile] * 4, merge_bwd, extra=(sv["pa"], sv["pb"], sv["fg"], sv["fg"]),
        extra_specs=[tile, tile, tile, pl.BlockSpec((tm, cs), lambda i, j, k: (i, j + N_CHIPS))])
    d_branch = {}
    for n, act, width, d_p in (("w_pa", sv["ys"], w_ssm, d_pa), ("w_pb", sv["ya"], w_att, d_pb)):
        gr[n] = dw_slabs(f"d{n}_{tag}", act, pl.BlockSpec((tk, width), lambda i, j, k: (k, 0)), width,
                         d_p, pl.BlockSpec((tk, cs), lambda i, j, k: (k, j)), cs, (1, N_CHIPS),
                         (lambda i, j, k: (j, 0, 0), (width, cs)))
        d_branch[n] = _mm_raw(
            f"d_in_{n}_{tag}", d_p, wts[n], "nt", (s // tm, 1, N_CHIPS), (tm, width),
            pl.BlockSpec((tm, cs), lambda i, j, k: (i, k)), pl.BlockSpec((None, width, cs), lambda i, j, k: (k, 0, 0)),
            [jax.ShapeDtypeStruct((s, width), BF16)], [pl.BlockSpec((tm, width), lambda i, j, k: (i, 0))], _store(BF16))[0]
    d_ys, d_ya = d_branch["w_pa"], d_branch["w_pb"]

    attn_bwd = lambda side: _attn_bwd(f"attn_bwd_{tag}", sv["uqkv"], *sv["blocks"], w_att // LANES, sv["ya"], d_ya,
                                      sv["lse_rows"], sv["ck_cols"], side=side)
    dq, dk, dv, dcq, dck = reduce_later.exchange_and_sum(attn_bwd) if reduce_later else attn_bwd(None)[0]
    d_f_t, d_bf = _cum_bwd(f"cum_bwd_{tag}", dcq.reshape(heads, s), dck.reshape(heads, s), sv["f_t"],
                           p["b_f"].reshape(heads, 1))
    gr["b_f"] = d_bf[:, 0]

    t5 = min(S5_ROWS, s)
    du_il, d_bblk, d_cblk, d_abar, d_wglu, vec = _s5_bwd(
        f"s5_bwd_{tag}", sv["u_il"], _interleave_rows(d_ys, t5), sv["y_s5"], sv["carries"], p["b_blk"], p["c_blk"],
        p["a_f"], p["a_r"], p["tab_f"], p["tab_r"], row(p["d_skip"]), p["w_glu"], row(p["b_glu"]))
    du = _deinterleave_rows(du_il, t5)
    gr["w_glu"] = d_wglu.astype(BF16).reshape(N_CHIPS, w_ssm // N_CHIPS, w_ssm)
    gr["b_glu"], gr["d_skip"] = vec[0], vec[1]
    gr["b_blk"], gr["c_blk"], gr["a_bar"] = d_bblk, d_cblk, d_abar

    d_f = jnp.pad(d_f_t.T, ((0, 0), (0, F_PAD - heads))).astype(BF16)
    assert w_ssm % w_att == 0 and (2 * d) % F_PAD == 0
    first = w_ssm // w_att
    main_pieces = [(du, w_ssm, 0), (dq, w_att, first), (dk, w_att, first + 1), (dv, w_att, first + 2)]
    dw = [_mm_plain(f"dw_in{n}_{tag}", sv["h"], piece, "tn", BF16, tm=1024, tn=1024, tk=1024)
          for n, piece in enumerate([du, dq, dk, dv, d_f, d_ga, d_gb])]
    w_in_grad = jnp.concatenate(dw[:4] + [dw[4][:, :heads], dw[5], dw[6]], axis=1)
    gr["w_in"] = w_in_grad.reshape(d, N_CHIPS, w_in_grad.shape[1] // N_CHIPS).transpose(1, 0, 2)
    tmx, tkx = _pick(s, 512), _pick(d, 512)
    pairs = [(piece, (tmx, width), lambda i, kk: (i, 0), p["w_main"], (d, width), lambda j, kk, blk=blk: (j, blk), 1)
             for piece, width, blk in main_pieces]
    steps = d // tkx
    pairs += [(piece, (tmx, tkx), lambda i, kk: (i, kk), p["w_gates"], (d, tkx), lambda j, kk, off=off: (j, off + kk), steps)
              for piece, off in ((d_ga, 0), (d_gb, steps))]
    pairs.append((d_f, (tmx, F_PAD), lambda i, kk: (i, 0), p["w_gates"], (d, F_PAD), lambda j, kk: (j, 2 * d // F_PAD), 1))
    dh_mix = lambda side: _mm_sum(f"dh_mix_{tag}", s, d, tmx, d, pairs, F32, side=side)
    dh1 = reduce_later.share(dh_mix) if reduce_later else dh_mix(None)
    dx0, sums = _prenorm_bwd(f"prenorm_bwd_mix_{tag}", dh1, sv["x"], row(p["g_pre_mix"]), row(mod[1]), dx1)
    d_scale_m, d_shift_m, gr["g_pre_mix"] = sums[0], sums[1], sums[2]

    d_mod = jnp.stack([d_shift_m, d_scale_m, d_gate_m, d_shift_f, d_scale_f, d_gate_f])
    return dx0, d_mod, gr


BIG = ("w_in", "w_glu", "w_pa", "w_pb", "w_o", "w_ffn_gate", "w_ffn_up", "w_ffn_down")
SMALL = ("b_ada", "g_pre_mix", "g_post_mix", "g_pre_ffn", "g_post_ffn", "lam_re", "lam_im", "log_dt", "b_re", "b_im",
         "c_re", "c_im", "d_skip", "b_glu", "b_f")
WEIGHTS = ("w_ada", "b_ada", "g_pre_mix", "g_post_mix", "g_pre_ffn", "g_post_ffn", "w_in", "lam_re", "lam_im", "log_dt",
           "b_re", "b_im", "c_re", "c_im", "d_skip", "w_glu", "b_glu", "b_f", "w_pa", "w_pb", "w_o", "w_ffn_gate",
           "w_ffn_up", "w_ffn_down")


def _prepare_layer(wts, small, l, seq):
    w_in = jnp.concatenate([wts["w_in"][j] for j in range(N_CHIPS)], axis=1)
    d = w_in.shape[0]
    heads = small["b_f"].shape[1]
    n_groups, n_state, group_ch = small["b_re"].shape[1:]
    w_ssm = n_groups * group_ch
    w_att = wts["w_pb"].shape[1]
    n_main = w_ssm + 3 * w_att
    gpb = LANES // group_ch
    p = {}
    p["w_main"] = w_in[:, :n_main]
    p["w_gates"] = jnp.concatenate(
        [w_in[:, n_main + heads:], w_in[:, n_main:n_main + heads], jnp.zeros((d, F_PAD - heads), BF16)], axis=1)
    p["w_glu"] = wts["w_glu"].reshape(w_ssm, w_ssm)
    for n in ("g_pre_mix", "g_post_mix", "g_pre_ffn", "g_post_ffn", "d_skip", "b_glu", "b_f"):
        p[n] = small[n][l]
    ar, ai, br, bi = _discretize(small["lam_re"][l], small["lam_im"][l], small["log_dt"][l], small["b_re"][l], small["b_im"][l])
    n_steps = min(S5_ROWS, seq) // SUBLANES
    powers = jnp.cumprod(jnp.broadcast_to(lax.complex(ar, ai).reshape(1, -1), (n_steps, ar.size)), axis=0)
    p["a_f"] = jnp.concatenate([jnp.real(powers), jnp.imag(powers)], axis=1)
    p["a_r"] = jnp.concatenate([jnp.real(powers[::-1]), -jnp.imag(powers[::-1])], axis=1)
    p["tab_f"], p["tab_r"] = _scan_tables(jnp.real(powers[-1]), jnp.imag(powers[-1]))
    bre = _block_diag(br.transpose(0, 2, 1), gpb)
    bim = _block_diag(bi.transpose(0, 2, 1), gpb)
    p["b_blk"] = jnp.concatenate([bre, bim], axis=2).astype(BF16)
    cre = _block_diag(small["c_re"][l].transpose(0, 2, 1), gpb)
    cim = _block_diag(small["c_im"][l].transpose(0, 2, 1), gpb)
    p["c_blk"] = jnp.concatenate([cre, -cim], axis=1).astype(BF16)
    return p


def _compact_partials(gr, n_state, group_ch):
    gpb = LANES // group_ch
    half = gpb * n_state
    out = dict(gr)
    out["bbar_re"] = _block_diag_extract(gr["b_blk"][:, :, :half], gpb, group_ch, n_state).transpose(0, 2, 1)
    out["bbar_im"] = _block_diag_extract(gr["b_blk"][:, :, half:], gpb, group_ch, n_state).transpose(0, 2, 1)
    out["c_re"] = _block_diag_extract(gr["c_blk"][:, :half, :], gpb, n_state, group_ch).transpose(0, 2, 1)
    out["c_im"] = -_block_diag_extract(gr["c_blk"][:, half:, :], gpb, n_state, group_ch).transpose(0, 2, 1)
    return out


def _small_grads_from_partials(gr, small, l):
    n_groups, n_state, _ = small["b_re"].shape[1:]
    ns2 = n_groups * n_state
    d_abar = jnp.sum(gr["a_bar"], axis=0)
    dar, dai = d_abar[:ns2].reshape(n_groups, n_state), d_abar[ns2:].reshape(n_groups, n_state)
    args = (small["lam_re"][l], small["lam_im"][l], small["log_dt"][l], small["b_re"][l], small["b_im"][l])
    _, vjp = jax.vjp(_discretize, *args)
    d_lam_re, d_lam_im, d_log_dt, d_b_re, d_b_im = vjp((dar, dai, gr["bbar_re"], gr["bbar_im"]))
    return dict(lam_re=d_lam_re, lam_im=d_lam_im, log_dt=d_log_dt, b_re=d_b_re, b_im=d_b_im,
                c_re=gr["c_re"], c_im=gr["c_im"])


def _fwd_bwd(xs, target, mods, small, wts0, later, core=None):
    depth = 1 + len(later)
    saved, layers, wts = [], [], [wts0]
    act = xs
    for l in range(depth):
        layers.append(_prepare_layer(wts[l], small, l, xs.shape[0]))
        shards = later[l] if l + 1 < depth and not isinstance(later[l], dict) else None
        act, sv, gathered = _layer_fwd(str(l), act, mods[l], layers[l], wts[l],
                                       gather_next=_gather_side_jobs(shards) if shards is not None else None)
        saved.append(sv)
        if l + 1 < depth:
            wts.append(dict(zip(BIG, _put_own_slabs(gathered, shards))) if shards is not None else later[l])
    dx, loss_blk = _loss_grad("loss", act, target)
    grads, d_mods = [None] * depth, [None] * depth
    pending = None
    for l in reversed(range(depth)):
        dx, d_mods[l], grads[l] = _layer_bwd(str(l), dx, mods[l], layers[l], wts[l], saved[l], reduce_later=pending)
        if core is not None:
            pending = _LayerReduce(str(l), l, depth, [grads[l][n] for n in BIG], core,
                                   into=pending.state if pending is not None else None)
    if core is None:
        return loss_blk, dx, d_mods, grads, None
    pending.swap_and_add()
    pending.exchange_and_sum()
    pending.share()
    return loss_blk, dx, d_mods, grads, dict(zip(BIG, pending.state))


def kernel(x, c, w_ada, b_ada, g_pre_mix, g_post_mix, g_pre_ffn, g_post_ffn, w_in, lam_re, lam_im, log_dt, b_re, b_im, c_re, c_im, d_skip, w_glu, b_glu, b_f, w_pa, w_pb, w_o, w_ffn_gate, w_ffn_up, w_ffn_down, loss_target, m_w_ada, m_b_ada, m_g_pre_mix, m_g_post_mix, m_g_pre_ffn, m_g_post_ffn, m_w_in, m_lam_re, m_lam_im, m_log_dt, m_b_re, m_b_im, m_c_re, m_c_im, m_d_skip, m_w_glu, m_b_glu, m_b_f, m_w_pa, m_w_pb, m_w_o, m_w_ffn_gate, m_w_ffn_up, m_w_ffn_down, v_w_ada, v_b_ada, v_g_pre_mix, v_g_post_mix, v_g_pre_ffn, v_g_post_ffn, v_w_in, v_lam_re, v_lam_im, v_log_dt, v_b_re, v_b_im, v_c_re, v_c_im, v_d_skip, v_w_glu, v_b_glu, v_b_f, v_w_pa, v_w_pb, v_w_o, v_w_ffn_gate, v_w_ffn_up, v_w_ffn_down):
    local = dict(locals())
    weights = {n: local[n] for n in WEIGHTS}
    moments_m = {n: local["m_" + n] for n in WEIGHTS}
    moments_v = {n: local["v_" + n] for n in WEIGHTS}
    depth, d = g_pre_mix.shape
    n_mod = w_ada.shape[2] * N_CHIPS // d
    mx, my, mc = lax.axis_index("x"), lax.axis_index("y"), lax.axis_index("c")
    my_chip = 2 * mx + my
    my_dev = 4 * mx + 2 * my + mc
    xs = x[0]

    shards = [[weights[n][l].astype(BF16) for n in BIG] for l in range(depth)]
    wts0 = dict(zip(BIG, _gather_layer("gather_weights_0", shards[0])))
    small = {n: weights[n] for n in SMALL}

    c_pad = jnp.pad(c, ((0, SUBLANES - 1), (0, 0)))
    c_all = _all_gather("gather_cond", c_pad).reshape(N_DEV, SUBLANES, d)[:, 0, :]
    silu = lambda v: v * _sigmoid(v)
    n_cols = w_ada.shape[2]
    mod_shard = []
    for l in range(depth):
        bias = lax.dynamic_slice_in_dim(b_ada[l], my_chip * n_cols, n_cols)
        mod_shard.append(_mm_plain(f"ada_{l}", c_all, w_ada[l], "nn", F32, add=jnp.broadcast_to(bias, (N_DEV, n_cols)),
                                   a_fn=silu, tm=N_DEV, tn=512, tk=1024))
    mod_block = jnp.concatenate(mod_shard, axis=1)
    mod_all = _all_gather("gather_mod", mod_block).reshape(N_DEV, N_DEV, depth, n_cols)
    mod_rows = lax.dynamic_index_in_dim(mod_all[0::2], my_dev, axis=1, keepdims=False)
    mods = [mod_rows[:, l, :].reshape(n_mod, d) for l in range(depth)]

    loss_blk, dx, d_mods, grads, big_grads = _fwd_bwd(xs, loss_target[0], mods, small, wts0, shards[1:],
                                                      core=mc.astype(jnp.int32).reshape(1))
    loss = lax.psum(loss_blk[0, 0], ("x", "y", "c"))
    grad_x = dx[None]

    partial_names = ("g_pre_mix", "g_post_mix", "g_pre_ffn", "g_post_ffn", "d_skip", "b_glu", "b_f", "a_bar",
                     "bbar_re", "bbar_im", "c_re", "c_im")
    n_state, group_ch = b_re.shape[2:]
    contrib = list(d_mods)
    for l in range(depth):
        compact = _compact_partials(grads[l], n_state, group_ch)
        contrib += [compact[n] for n in partial_names]
    contrib_shapes = [a.shape for a in contrib]
    block = _pack(contrib, LANES, BF16_ROWS, F32)
    rows = block.shape[0]
    all_blocks = _all_gather("gather_small_grads", block).reshape(N_DEV, rows, LANES)
    summed = _unpack(_sum_blocks("sum_small_grads", all_blocks, F32), contrib_shapes)
    per_layer = len(partial_names)
    small_grads = {n: [] for n in SMALL}
    d_mod_all = []
    for l in range(depth):
        small_grads["b_ada"].append(summed[l].reshape(-1))
        gl = dict(zip(partial_names, summed[depth + l * per_layer:depth + (l + 1) * per_layer]))
        for n in ("g_pre_mix", "g_post_mix", "g_pre_ffn", "g_post_ffn", "d_skip", "b_glu", "b_f"):
            small_grads[n].append(gl[n])
        for n, gval in _small_grads_from_partials(gl, small, l).items():
            small_grads[n].append(gval)
        mod_rows_ = n_mod * d // LANES
        d_mod_all.append(all_blocks[:, l * mod_rows_:(l + 1) * mod_rows_, :].reshape(N_DEV, n_mod * d))
    small_grads = {n: jnp.stack(v) for n, v in small_grads.items()}

    g_w_ada = []
    for l in range(depth):
        cols = lax.dynamic_slice_in_dim(d_mod_all[l], my_chip * n_cols, n_cols, axis=1)
        g_w_ada.append(_mm_plain(f"dw_ada_{l}", c_all, cols, "tn", F32, a_fn=silu, tm=512, tn=512, tk=N_DEV))
    all_grads = dict(big_grads)
    all_grads.update(small_grads)
    all_grads["w_ada"] = jnp.stack(g_w_ada)

    delta, new_m, new_v = {}, {}, {}
    for n in ("w_ada",) + BIG:
        delta[n], new_m[n], new_v[n] = _adamw(f"adamw_{n}", weights[n], all_grads[n], moments_m[n], moments_v[n])
    small_shapes = [weights[n].shape for n in SMALL]
    packed = [_pack([src[n] for n in SMALL], LANES, SUBLANES, F32)[None] for src in (weights, all_grads, moments_m, moments_v)]
    outs = _adamw("adamw_small", *packed)
    for dst, buf in zip((delta, new_m, new_v), outs):
        dst.update(dict(zip(SMALL, _unpack(buf[0], small_shapes))))

    return (loss, grad_x, *[all_grads[n] for n in WEIGHTS], *[delta[n] for n in WEIGHTS],
            *[new_m[n] for n in WEIGHTS], *[new_v[n] for n in WEIGHTS])
```

```python
import functools
import math

import jax
import jax.numpy as jnp
from jax import lax
from jax.experimental import pallas as pl
from jax.experimental.pallas import tpu as pltpu

F32 = jnp.float32
BF16 = jnp.bfloat16
MESH = pl.DeviceIdType.MESH

RMS_EPS = 1e-6
EIG_CLIP = 1e-4
ADAM_LR, ADAM_B1, ADAM_B2, ADAM_EPS, ADAM_WD, ADAM_STEP = 0.001, 0.9, 0.999, 1e-08, 0.01, 10

LANES = 128
SUBLANES = 8
VMEM_LIMIT = 56 * 1024 * 1024
S5_ROWS = 256
S5_CHUNK = 1024
S5_UNROLL = 4
ATT_BLOCK = 512
F_PAD = 256
POSTNORM_ROWS = 1024
N_CHIPS = 4
N_DEV = 8

NN = (((1,), (0,)), ((), ()))
NT = (((1,), (1,)), ((), ()))
TN = (((0,), (0,)), ((), ()))
_DN = {"nn": NN, "nt": NT, "tn": TN}


def _cparams(**kw):
    return pltpu.CompilerParams(vmem_limit_bytes=VMEM_LIMIT, **kw)


def _pick(dim, target):
    best, t = None, LANES
    while t <= min(dim, target):
        if dim % t == 0:
            best = t
        t += LANES
    return best or dim


def _sigmoid(x):
    return 1.0 / (1.0 + jnp.exp(-x))


def _dot(a, b, dn):
    return lax.dot_general(a, b, dn, preferred_element_type=F32)


def _mm_raw(name, a, b, mode, grid, acc_shape, a_spec, b_spec, out_shapes, out_specs, epilogue,
            extra=(), extra_specs=(), a_fn=None, side=None):
    nk = grid[2]
    n_extra, n_out = len(extra), len(out_shapes)

    def body(*refs):
        a_ref, b_ref = refs[0], refs[1]
        extra_refs = refs[2:2 + n_extra]
        out_refs = refs[2 + n_extra:2 + n_extra + n_out]
        acc = refs[-1]
        k = pl.program_id(2)

        @pl.when(k == 0)
        def _():
            acc[...] = jnp.zeros_like(acc)

        av = a_ref[...]
        if a_fn is not None:
            av = a_fn(av.astype(F32))
        acc[...] += _dot(av.astype(BF16), b_ref[...].astype(BF16), _DN[mode])

        @pl.when(k == nk - 1)
        def _():
            epilogue(acc[...], extra_refs, out_refs)

    outs, side_outs = _hosted_call(body, side, name, grid, [a_spec, b_spec, *extra_specs], list(out_specs),
                                   list(out_shapes), [pltpu.VMEM(acc_shape, F32)], (a, b, *extra))
    return outs if side is None else (outs, side_outs)


def _mm(name, a, b, mode, out_shapes, out_specs, epilogue, extra=(), extra_specs=(),
        tm=512, tn=512, tk=512, a_fn=None):
    if mode == "nn":
        (m, kd), (_, n) = a.shape, b.shape
    elif mode == "nt":
        (m, kd), (n, _) = a.shape, b.shape
    else:
        (kd, m), (_, n) = a.shape, b.shape
    tm, tn, tk = _pick(m, tm), _pick(n, tn), _pick(kd, tk)
    if mode == "tn":
        a_spec = pl.BlockSpec((tk, tm), lambda i, j, k: (k, i))
    else:
        a_spec = pl.BlockSpec((tm, tk), lambda i, j, k: (i, k))
    if mode == "nt":
        b_spec = pl.BlockSpec((tn, tk), lambda i, j, k: (j, k))
    else:
        b_spec = pl.BlockSpec((tk, tn), lambda i, j, k: (k, j))
    res = _mm_raw(name, a, b, mode, (m // tm, n // tn, kd // tk), (tm, tn), a_spec, b_spec, out_shapes, out_specs,
                  epilogue, extra=extra, extra_specs=extra_specs, a_fn=a_fn)
    return res, (tm, tn, tk)


def _store(dtype):
    def epilogue(acc, extra_refs, out_refs):
        out_refs[0][...] = acc.astype(dtype)
    return epilogue


def _mm_sum(name, m, n, tm, tn, pairs, out_dtype, side=None):
    offs, total = [], 0
    for pr in pairs:
        offs.append(total)
        total += pr[6]
    n_p = len(pairs)

    def body(*refs):
        o_ref, acc = refs[2 * n_p], refs[2 * n_p + 1]
        k = pl.program_id(2)

        @pl.when(k == 0)
        def _():
            acc[...] = jnp.zeros_like(acc)

        for p_ in range(n_p):
            @pl.when((k >= offs[p_]) & (k < offs[p_] + pairs[p_][6]))
            def _(p_=p_):
                acc[...] += _dot(refs[2 * p_][...].astype(BF16), refs[2 * p_ + 1][...].astype(BF16), NT)

        @pl.when(k == total - 1)
        def _():
            o_ref[...] = acc[...].astype(out_dtype)

    in_specs, operands = [], []
    for (a, a_block, a_index, b, b_block, b_index, steps), off in zip(pairs, offs):
        local = lambda k, off=off, steps=steps: jnp.clip(k - off, 0, steps - 1)
        in_specs.append(pl.BlockSpec(a_block, lambda i, j, k, f=a_index, local=local: f(i, local(k))))
        in_specs.append(pl.BlockSpec(b_block, lambda i, j, k, f=b_index, local=local: f(j, local(k))))
        operands += [a, b]
    (out,), side_outs = _hosted_call(
        body, side, name, (m // tm, n // tn, total), in_specs, [pl.BlockSpec((tm, tn), lambda i, j, k: (i, j))],
        [jax.ShapeDtypeStruct((m, n), out_dtype)], [pltpu.VMEM((tm, tn), F32)], operands)
    return out if side is None else (out, side_outs)


class _SideJob:
    def __init__(self, arrays, out_shapes, aliases, n_sems, copies):
        self.arrays, self.out_shapes, self.aliases, self.n_sems, self.copies = arrays, out_shapes, aliases, n_sems, copies


def _hosted_call(body, side, name, grid, in_specs, out_specs, out_shape, scratch_shapes, operands):
    if side is None:
        outs = pl.pallas_call(body, name=name, grid=grid, in_specs=in_specs, out_specs=out_specs, out_shape=out_shape,
                              scratch_shapes=scratch_shapes, compiler_params=_cparams())(*operands)
        return outs, []
    n_in, n_out, ns_in, ns_out = len(in_specs), len(out_specs), len(side.arrays), len(side.out_shapes)

    def wrapped(*refs):
        main_in, side_in = refs[:n_in], refs[n_in:n_in + ns_in]
        rest = refs[n_in + ns_in:]
        main_out, side_out, rest = rest[:n_out], rest[n_out:n_out + ns_out], rest[n_out + ns_out:]
        scratch, send_sems, recv_sems = rest[:-2], rest[-2], rest[-1]
        first, last = None, None
        for axis, extent in enumerate(grid):
            at_start, at_end = pl.program_id(axis) == 0, pl.program_id(axis) == extent - 1
            first = at_start if first is None else first & at_start
            last = at_end if last is None else last & at_end

        @pl.when(first)
        def _():
            for cp in side.copies(side_in, side_out, send_sems, recv_sems):
                cp.start()

        body(*main_in, *main_out, *scratch)

        @pl.when(last)
        def _():
            for cp in side.copies(side_in, side_out, send_sems, recv_sems):
                cp.wait()

    hbm = pl.BlockSpec(memory_space=pl.ANY)
    outs = pl.pallas_call(
        wrapped, name=name, grid=grid, in_specs=list(in_specs) + [hbm] * ns_in,
        out_specs=list(out_specs) + [hbm] * ns_out, out_shape=list(out_shape) + list(side.out_shapes),
        scratch_shapes=list(scratch_shapes) + [pltpu.SemaphoreType.DMA((side.n_sems,))] * 2,
        input_output_aliases={n_in + i: n_out + o for i, o in side.aliases.items()},
        compiler_params=_cparams(),
    )(*operands, *side.arrays)
    return outs[:n_out], outs[n_out:]


def _ffn_up(name, h, wg, wu, side=None):
    s, d = h.shape
    nc, fs = wg.shape[0], wg.shape[2]
    tm, tk = _pick(s, 1024), _pick(d, 1024)
    nk = d // tk

    def body(h_ref, wg_ref, wu_ref, a_ref, b_ref, hid_ref, acc_g, acc_u):
        k = pl.program_id(2)

        @pl.when(k == 0)
        def _():
            acc_g[...] = jnp.zeros_like(acc_g)
            acc_u[...] = jnp.zeros_like(acc_u)

        hv = h_ref[...]
        acc_g[...] += _dot(hv, wg_ref[...], NN)
        acc_u[...] += _dot(hv, wu_ref[...], NN)

        @pl.when(k == nk - 1)
        def _():
            av, bv = acc_g[...], acc_u[...]
            a_ref[...] = av.astype(BF16)
            b_ref[...] = bv.astype(BF16)
            hid_ref[...] = (av * _sigmoid(av) * bv).astype(BF16)

    w_spec = pl.BlockSpec((None, tk, fs), lambda i, j, k: (j, k, 0))
    o_spec = pl.BlockSpec((None, tm, fs), lambda i, j, k: (j, i, 0))
    sh = jax.ShapeDtypeStruct((nc, s, fs), BF16)
    return _hosted_call(
        body, side, name, (s // tm, nc, nk), [pl.BlockSpec((tm, tk), lambda i, j, k: (i, k)), w_spec, w_spec],
        [o_spec] * 3, [sh] * 3, [pltpu.VMEM((tm, fs), F32), pltpu.VMEM((tm, fs), F32)], (h, wg, wu))


def _mm_plain(name, a, b, mode, out_dtype, add=None, a_fn=None, tm=512, tn=512, tk=512):
    if mode == "nn":
        m, n = a.shape[0], b.shape[1]
    elif mode == "nt":
        m, n = a.shape[0], b.shape[0]
    else:
        m, n = a.shape[1], b.shape[1]
    tm_, tn_ = _pick(m, tm), _pick(n, tn)
    spec = pl.BlockSpec((tm_, tn_), lambda i, j, k: (i, j))

    def epilogue(acc, extra_refs, out_refs):
        if add is not None:
            acc = acc + extra_refs[0][...]
        out_refs[0][...] = acc.astype(out_dtype)

    extra = () if add is None else (add,)
    (out,), _ = _mm(name, a, b, mode, [jax.ShapeDtypeStruct((m, n), out_dtype)], [spec], epilogue,
                    extra=extra, extra_specs=[spec] * len(extra), tm=tm, tn=tn, tk=tk, a_fn=a_fn)
    return out


def _row_tile(s, d):
    return _pick(s, max(SUBLANES, (1 << 20) // (4 * d)))


def _prenorm_fwd(name, x, g, scale, shift):
    s, d = x.shape
    tr = _row_tile(s, d)

    def body(x_ref, g_ref, sc_ref, sh_ref, h_ref):
        xv = x_ref[...]
        r = lax.rsqrt(jnp.mean(xv * xv, axis=-1, keepdims=True) + RMS_EPS)
        h_ref[...] = ((xv * r * g_ref[...]) * (1.0 + sc_ref[...]) + sh_ref[...]).astype(BF16)

    row = pl.BlockSpec((tr, d), lambda i: (i, 0))
    vec = pl.BlockSpec((1, d), lambda i: (0, 0))
    return pl.pallas_call(body, name=name, grid=(s // tr,), in_specs=[row, vec, vec, vec], out_specs=row,
                          out_shape=jax.ShapeDtypeStruct((s, d), BF16), compiler_params=_cparams())(x, g, scale, shift)


def _prenorm_bwd(name, dh, x, g, scale, dx_res):
    s, d = x.shape
    tr = _row_tile(s, d)

    def body(dh_ref, x_ref, g_ref, sc_ref, dxr_ref, dx_ref, sums_ref):
        @pl.when(pl.program_id(0) == 0)
        def _():
            sums_ref[...] = jnp.zeros_like(sums_ref)

        xv, dhv, gv = x_ref[...], dh_ref[...].astype(F32), g_ref[...]
        r = lax.rsqrt(jnp.mean(xv * xv, axis=-1, keepdims=True) + RMS_EPS)
        xhat = xv * r
        dxn = dhv * (1.0 + sc_ref[...])
        dxhat = dxn * gv
        dx = r * (dxhat - xhat * jnp.mean(dxhat * xhat, axis=-1, keepdims=True))
        dx_ref[...] = dxr_ref[...] + dx
        sums_ref[0:1, :] += jnp.sum(dhv * (xhat * gv), axis=0, keepdims=True)
        sums_ref[1:2, :] += jnp.sum(dhv, axis=0, keepdims=True)
        sums_ref[2:3, :] += jnp.sum(dxn * xhat, axis=0, keepdims=True)

    row = pl.BlockSpec((tr, d), lambda i: (i, 0))
    vec = pl.BlockSpec((1, d), lambda i: (0, 0))
    acc = pl.BlockSpec((SUBLANES, d), lambda i: (0, 0))
    return pl.pallas_call(
        body, name=name, grid=(s // tr,), in_specs=[row, row, vec, vec, row], out_specs=[row, acc],
        out_shape=[jax.ShapeDtypeStruct((s, d), F32), jax.ShapeDtypeStruct((SUBLANES, d), F32)],
        compiler_params=_cparams())(dh, x, g, scale, dx_res)


def _postnorm_bwd(name, dxn, y, g, gate):
    s, d = y.shape
    tr = _row_tile(s, d)

    def body(dx_ref, y_ref, g_ref, gt_ref, dy_ref, sums_ref):
        @pl.when(pl.program_id(0) == 0)
        def _():
            sums_ref[...] = jnp.zeros_like(sums_ref)

        yv, dxv, gv = y_ref[...], dx_ref[...], g_ref[...]
        r = lax.rsqrt(jnp.mean(yv * yv, axis=-1, keepdims=True) + RMS_EPS)
        yhat = yv * r
        dn = dxv * gt_ref[...]
        dyhat = dn * gv
        dy_ref[...] = (r * (dyhat - yhat * jnp.mean(dyhat * yhat, axis=-1, keepdims=True))).astype(BF16)
        sums_ref[0:1, :] += jnp.sum(dxv * (yhat * gv), axis=0, keepdims=True)
        sums_ref[1:2, :] += jnp.sum(dn * yhat, axis=0, keepdims=True)

    row = pl.BlockSpec((tr, d), lambda i: (i, 0))
    vec = pl.BlockSpec((1, d), lambda i: (0, 0))
    acc = pl.BlockSpec((SUBLANES, d), lambda i: (0, 0))
    return pl.pallas_call(
        body, name=name, grid=(s // tr,), in_specs=[row, row, vec, vec], out_specs=[row, acc],
        out_shape=[jax.ShapeDtypeStruct((s, d), BF16), jax.ShapeDtypeStruct((SUBLANES, d), F32)],
        compiler_params=_cparams())(dxn, y, g, gate)


def _loss_grad(name, y, target):
    s, d = y.shape
    tr = _row_tile(s, d)

    def body(y_ref, t_ref, dy_ref, loss_ref):
        @pl.when(pl.program_id(0) == 0)
        def _():
            loss_ref[...] = jnp.zeros_like(loss_ref)

        err = y_ref[...] - t_ref[...]
        dy_ref[...] = err * (1.0 / d)
        part = jnp.sum(jnp.sum(err * err, axis=-1, keepdims=True), axis=0, keepdims=True) * (0.5 / d)
        loss_ref[...] += jnp.broadcast_to(part, loss_ref.shape)

    row = pl.BlockSpec((tr, d), lambda i: (i, 0))
    acc = pl.BlockSpec((SUBLANES, LANES), lambda i: (0, 0))
    return pl.pallas_call(
        body, name=name, grid=(s // tr,), in_specs=[row, row], out_specs=[row, acc],
        out_shape=[jax.ShapeDtypeStruct((s, d), F32), jax.ShapeDtypeStruct((SUBLANES, LANES), F32)],
        compiler_params=_cparams())(y, target)


def _gelu(y):
    c = math.sqrt(2.0 / math.pi)
    return 0.5 * y * (1.0 + jnp.tanh(c * (y + 0.044715 * (y * y * y))))


def _gelu_grad(y):
    c = math.sqrt(2.0 / math.pi)
    th = jnp.tanh(c * (y + 0.044715 * (y * y * y)))
    return 0.5 * (1.0 + th) + 0.5 * y * (1.0 - th * th) * c * (1.0 + 3.0 * 0.044715 * (y * y))


def _cmul_add(br, bi, ar, ai, xr, xi):
    return br + ar * xr - ai * xi, bi + ar * xi + ai * xr


def _scan_rows(x_ref, row0, n_steps, ns2, pow_ref, tab_ref, carry_ref, reverse, fold=None):
    assert n_steps % SUBLANES == 0
    wc = min(S5_CHUNK, ns2)
    sub = lax.broadcasted_iota(jnp.int32, (SUBLANES, wc), 0)
    unroll = S5_UNROLL if n_steps % S5_UNROLL == 0 else 1
    for c0 in range(0, ns2, wc):
        re = slice(c0, c0 + wc)
        im = slice(ns2 + c0, ns2 + c0 + wc)
        first_power = slice(n_steps - 1, n_steps) if reverse else slice(0, 1)
        ar = jnp.broadcast_to(pow_ref[first_power, re], (SUBLANES, wc))
        ai = jnp.broadcast_to(pow_ref[first_power, im], (SUBLANES, wc))
        rows = lambda r: pl.ds(pl.multiple_of(row0 + r * SUBLANES, SUBLANES), SUBLANES)
        step_of = lambda i: (n_steps - 1 - i) if reverse else i

        def local(i, carry, re=re, im=im, ar=ar, ai=ai):
            for u in range(unroll):
                r = step_of(i * unroll + u)
                carry = _cmul_add(x_ref[rows(r), re], x_ref[rows(r), im], ar, ai, *carry)
                x_ref[rows(r), re], x_ref[rows(r), im] = carry
            return carry

        zero = jnp.zeros((SUBLANES, wc), F32)
        lr, li = lax.fori_loop(0, n_steps // unroll, local, (zero, zero))

        tabs = [tab_ref[k, :, re] for k in range(8)]
        for lvl, k in enumerate((1, 2, 4)):
            sh = (SUBLANES - k) if reverse else k
            lr, li = _cmul_add(lr, li, tabs[2 * lvl], tabs[2 * lvl + 1], pltpu.roll(lr, sh, 0), pltpu.roll(li, sh, 0))
        cr, ci = carry_ref[0:1, re], carry_ref[0:1, im]
        lr, li = _cmul_add(lr, li, tabs[6], tabs[7], cr, ci)
        edge, away, last = (SUBLANES - 1, SUBLANES - 1, 0) if reverse else (0, 1, SUBLANES - 1)
        carry_ref[0:1, re] = lr[last:last + 1, :]
        carry_ref[0:1, im] = li[last:last + 1, :]
        er = jnp.where(sub == edge, cr, pltpu.roll(lr, away, 0))
        ei = jnp.where(sub == edge, ci, pltpu.roll(li, away, 0))

        def fix(j, acc, re=re, im=im, er=er, ei=ei, c0=c0):
            base = pl.ds(pl.multiple_of(j * SUBLANES, SUBLANES), SUBLANES)
            pw_r, pw_i = pow_ref[base, re], pow_ref[base, im]
            for i in range(SUBLANES):
                r = j * SUBLANES + i
                xr, xi = _cmul_add(x_ref[rows(r), re], x_ref[rows(r), im], pw_r[i:i + 1, :], pw_i[i:i + 1, :], er, ei)
                x_ref[rows(r), re], x_ref[rows(r), im] = xr, xi
                if fold is not None:
                    acc = fold(c0, r, xr, xi, acc)
            return acc

        acc = lax.fori_loop(0, n_steps // SUBLANES, fix, (zero, zero) if fold is not None else 0)
        if fold is not None:
            fold(c0, None, None, None, acc)


def _s5_fwd(name, u, b_blk, c_blk, a_f, tab_f, dskip, w_glu, b_glu):
    s, w = u.shape[0], w_glu.shape[0]
    nkb = w // LANES
    ns2 = b_blk.shape[2] // 2 * nkb
    half = ns2 // nkb
    t = min(S5_ROWS, s)
    nblk = s // t

    def body(u_ref, b_ref, c_ref, a_ref, tab_ref, ds_ref, wg_ref, bg_ref, y_ref, ys_ref, cs_ref, xs, carry):
        @pl.when(pl.program_id(0) == 0)
        def _():
            carry[...] = jnp.zeros_like(carry)

        cs_ref[0] = carry[...]
        for kb in range(nkb):
            bu = _dot(u_ref[:, kb * LANES:(kb + 1) * LANES], b_ref[kb], NN)
            xs[:, kb * half:(kb + 1) * half] = bu[:, :half]
            xs[:, ns2 + kb * half:ns2 + (kb + 1) * half] = bu[:, half:]
        _scan_rows(xs, 0, t // SUBLANES, ns2, a_ref, tab_ref, carry, reverse=False)
        for kb in range(nkb):
            cols = slice(kb * LANES, (kb + 1) * LANES)
            yk = _dot(xs[:, kb * half:(kb + 1) * half].astype(BF16), c_ref[kb, :half, :], NN)
            yk += _dot(xs[:, ns2 + kb * half:ns2 + (kb + 1) * half].astype(BF16), c_ref[kb, half:, :], NN)
            y_ref[:, cols] = yk + ds_ref[:, cols] * u_ref[:, cols].astype(F32)
        z = _gelu(y_ref[...])
        gate = _sigmoid(_dot(z.astype(BF16), wg_ref[...], NN) + bg_ref[...])
        ys_ref[...] = (z * gate).astype(BF16)

    row = pl.BlockSpec((t, w), lambda i: (i, 0))
    full = lambda shape: pl.BlockSpec(shape, lambda i: (0,) * len(shape))
    return pl.pallas_call(
        body, name=name, grid=(nblk,),
        in_specs=[row, full(b_blk.shape), full(c_blk.shape), full(a_f.shape), full(tab_f.shape), full(dskip.shape),
                  full(w_glu.shape), full(b_glu.shape)],
        out_specs=[row, row, pl.BlockSpec((1, 1, 2 * ns2), lambda i: (i, 0, 0))],
        out_shape=[jax.ShapeDtypeStruct((s, w), F32), jax.ShapeDtypeStruct((s, w), BF16),
                   jax.ShapeDtypeStruct((nblk, 1, 2 * ns2), F32)],
        scratch_shapes=[pltpu.VMEM((t, 2 * ns2), F32), pltpu.VMEM((1, 2 * ns2), F32)],
        compiler_params=_cparams(),
    )(u, b_blk, c_blk, a_f, tab_f, dskip, w_glu, b_glu)


def _s5_bwd(name, u, dys, y, carries, b_blk, c_blk, a_f, a_r, tab_f, tab_r, dskip, w_glu, b_glu):
    s, w = u.shape[0], w_glu.shape[0]
    nkb = w // LANES
    ns2 = b_blk.shape[2] // 2 * nkb
    half = ns2 // nkb
    t = min(S5_ROWS, s)
    nblk = s // t
    ng = t // SUBLANES

    def body(u_ref, dys_ref, y_ref, cs_ref, b_ref, c_ref, af_ref, ar_ref, tabf_ref, tabr_ref, ds_ref, wg_ref, bg_ref,
             du_ref, db_ref, dc_ref, da_ref, dwg_ref, vec_ref, xs, gs, dyv, fcarry, gcarry):
        @pl.when(pl.program_id(0) == 0)
        def _():
            db_ref[...] = jnp.zeros_like(db_ref)
            dc_ref[...] = jnp.zeros_like(dc_ref)
            da_ref[...] = jnp.zeros_like(da_ref)
            dwg_ref[...] = jnp.zeros_like(dwg_ref)
            vec_ref[...] = jnp.zeros_like(vec_ref)
            gcarry[...] = jnp.zeros_like(gcarry)

        yv = y_ref[...]
        z = _gelu(yv)
        zb = z.astype(BF16)
        gate = _sigmoid(_dot(zb, wg_ref[...], NN) + bg_ref[...])
        dout = dys_ref[...].astype(F32)
        dt = dout * z * gate * (1.0 - gate)
        dtb = dt.astype(BF16)
        dz = dout * gate + _dot(dtb, wg_ref[...], NT)
        dy = dz * _gelu_grad(yv)
        dyv[...] = dy
        dwg_ref[...] += _dot(zb, dtb, TN)
        vec_ref[0:1, :] += jnp.sum(dt, axis=0, keepdims=True)
        vec_ref[1:2, :] += jnp.sum(dy * u_ref[...].astype(F32), axis=0, keepdims=True)

        fcarry[...] = cs_ref[0]
        xs[0:SUBLANES, :] = jnp.broadcast_to(cs_ref[0], (SUBLANES, 2 * ns2))
        for kb in range(nkb):
            bu = _dot(u_ref[:, kb * LANES:(kb + 1) * LANES], b_ref[kb], NN)
            xs[SUBLANES:, kb * half:(kb + 1) * half] = bu[:, :half]
            xs[SUBLANES:, ns2 + kb * half:ns2 + (kb + 1) * half] = bu[:, half:]
        _scan_rows(xs, SUBLANES, ng, ns2, af_ref, tabf_ref, fcarry, reverse=False)
        first_segment = lax.broadcasted_iota(jnp.int32, (SUBLANES, 2 * ns2), 0) == 0
        xs[0:SUBLANES, :] = jnp.where(first_segment, xs[0:SUBLANES, :], pltpu.roll(xs[t:t + SUBLANES, :], 1, 0))

        for kb in range(nkb):
            dyk = dyv[:, kb * LANES:(kb + 1) * LANES].astype(BF16)
            re = slice(kb * half, (kb + 1) * half)
            im = slice(ns2 + kb * half, ns2 + (kb + 1) * half)
            gs[:, re] = _dot(dyk, c_ref[kb, :half, :], NT)
            gs[:, im] = _dot(dyk, c_ref[kb, half:, :], NT)
            dc_ref[kb, :half, :] += _dot(xs[SUBLANES:, re].astype(BF16), dyk, TN)
            dc_ref[kb, half:, :] += _dot(xs[SUBLANES:, im].astype(BF16), dyk, TN)

        def fold(c0, r, gr, gi, acc):
            wc = min(S5_CHUNK, ns2)
            re = slice(c0, c0 + wc)
            im = slice(ns2 + c0, ns2 + c0 + wc)
            if r is None:
                da_ref[:, re] += acc[0]
                da_ref[:, im] += acc[1]
                return acc
            before = pl.ds(pl.multiple_of(r * SUBLANES, SUBLANES), SUBLANES)
            xpr, xpi = xs[before, re], xs[before, im]
            return acc[0] + gr * xpr + gi * xpi, acc[1] - gr * xpi + gi * xpr

        _scan_rows(gs, 0, ng, ns2, ar_ref, tabr_ref, gcarry, reverse=True, fold=fold)

        for kb in range(nkb):
            cols = slice(kb * LANES, (kb + 1) * LANES)
            re = slice(kb * half, (kb + 1) * half)
            im = slice(ns2 + kb * half, ns2 + (kb + 1) * half)
            uk = u_ref[:, cols]
            gr = gs[:, re].astype(BF16)
            gi = gs[:, im].astype(BF16)
            db_ref[kb, :, :half] += _dot(uk, gr, TN)
            db_ref[kb, :, half:] += _dot(uk, gi, TN)
            duk = _dot(gr, b_ref[kb, :, :half], NT) + _dot(gi, b_ref[kb, :, half:], NT)
            du_ref[:, cols] = (duk + ds_ref[:, cols] * dyv[:, cols]).astype(BF16)

    rev = lambda i: (nblk - 1 - i, 0)
    row = pl.BlockSpec((t, w), rev)
    full = lambda shape: pl.BlockSpec(shape, lambda i: (0,) * len(shape))
    return pl.pallas_call(
        body, name=name, grid=(nblk,),
        in_specs=[row, row, row, pl.BlockSpec((1, 1, 2 * ns2), lambda i: (nblk - 1 - i, 0, 0)),
                  full(b_blk.shape), full(c_blk.shape), full(a_f.shape), full(a_r.shape), full(tab_f.shape),
                  full(tab_r.shape), full(dskip.shape), full(w_glu.shape), full(b_glu.shape)],
        out_specs=[row, full(b_blk.shape), full(c_blk.shape), full((SUBLANES, 2 * ns2)), full((w, w)),
                   full((SUBLANES, w))],
        out_shape=[jax.ShapeDtypeStruct((s, w), BF16), jax.ShapeDtypeStruct(b_blk.shape, F32),
                   jax.ShapeDtypeStruct(c_blk.shape, F32), jax.ShapeDtypeStruct((SUBLANES, 2 * ns2), F32),
                   jax.ShapeDtypeStruct((w, w), F32), jax.ShapeDtypeStruct((SUBLANES, w), F32)],
        scratch_shapes=[pltpu.VMEM((t + SUBLANES, 2 * ns2), F32), pltpu.VMEM((t, 2 * ns2), F32),
                        pltpu.VMEM((t, w), F32), pltpu.VMEM((1, 2 * ns2), F32), pltpu.VMEM((1, 2 * ns2), F32)],
        compiler_params=_cparams(),
    )(u, dys, y, carries, b_blk, c_blk, a_f, a_r, tab_f, tab_r, dskip, w_glu, b_glu)


def _log_sigmoid(x):
    return jnp.minimum(x, 0.0) - jnp.log(1.0 + jnp.exp(-jnp.abs(x)))


def _cum_fwd(name, f_t, b_f):
    h, s = f_t.shape
    tc = _pick(s, 512)
    nb = s // tc

    def body(f_ref, b_ref, c_ref, carry):
        @pl.when(pl.program_id(0) == 0)
        def _():
            carry[...] = jnp.zeros_like(carry)

        lf = _log_sigmoid(f_ref[...] + b_ref[...])
        upper = (lax.broadcasted_iota(jnp.int32, (tc, tc), 0) <= lax.broadcasted_iota(jnp.int32, (tc, tc), 1))
        cum = lax.dot_general(lf, upper.astype(F32), NN, precision=lax.Precision.HIGHEST,
                              preferred_element_type=F32) + carry[...]
        c_ref[...] = cum
        carry[...] += jnp.sum(lf, axis=1, keepdims=True)

    blk = pl.BlockSpec((h, tc), lambda i: (0, i))
    return pl.pallas_call(body, name=name, grid=(nb,), in_specs=[blk, pl.BlockSpec((h, 1), lambda i: (0, 0))],
                          out_specs=blk, out_shape=jax.ShapeDtypeStruct((h, s), F32),
                          scratch_shapes=[pltpu.VMEM((h, 1), F32)], compiler_params=_cparams())(f_t, b_f)


def _cum_bwd(name, dcq, dck, f_t, b_f):
    h, s = f_t.shape
    tc = _pick(s, 512)
    nb = s // tc

    def body(dcq_ref, dck_ref, f_ref, b_ref, df_ref, db_ref, carry):
        @pl.when(pl.program_id(0) == 0)
        def _():
            carry[...] = jnp.zeros_like(carry)
            db_ref[...] = jnp.zeros_like(db_ref)

        dc = dcq_ref[...] + dck_ref[...]
        lower = (lax.broadcasted_iota(jnp.int32, (tc, tc), 0) >= lax.broadcasted_iota(jnp.int32, (tc, tc), 1))
        dlf = lax.dot_general(dc, lower.astype(F32), NN, precision=lax.Precision.HIGHEST,
                              preferred_element_type=F32) + carry[...]
        carry[...] += jnp.sum(dc, axis=1, keepdims=True)
        df = dlf * _sigmoid(-(f_ref[...] + b_ref[...]))
        df_ref[...] = df
        db_ref[...] += jnp.broadcast_to(jnp.sum(df, axis=1, keepdims=True), db_ref.shape)

    blk = pl.BlockSpec((h, tc), lambda i: (0, nb - 1 - i))
    return pl.pallas_call(
        body, name=name, grid=(nb,), in_specs=[blk, blk, blk, pl.BlockSpec((h, 1), lambda i: (0, 0))],
        out_specs=[blk, pl.BlockSpec((h, LANES), lambda i: (0, 0))],
        out_shape=[jax.ShapeDtypeStruct((h, s), F32), jax.ShapeDtypeStruct((h, LANES), F32)],
        scratch_shapes=[pltpu.VMEM((h, 1), F32)], compiler_params=_cparams())(dcq, dck, f_t, b_f)


def _attn_fwd(name, qkv, q_blk, k_blk, v_blk, n_pairs, ck, side=None):
    s = qkv.shape[0]
    dh = LANES // 2
    t = min(ATT_BLOCK, s)
    nq = s // t
    scale = dh ** -0.5

    def body(q_ref, k_ref, v_ref, ck_ref, o_ref, lse_ref, m_s, acc_s):
        i = pl.program_id(1)
        low = lax.broadcasted_iota(jnp.int32, (1, LANES), 1) < dh
        qs = (q_ref[...].astype(F32) * scale).astype(BF16)
        zero = jnp.zeros_like(qs)
        qh = (jnp.where(low, qs, zero), jnp.where(low, zero, qs))
        m_s[...] = jnp.full(m_s.shape, -1e30, F32)
        acc_s[...] = jnp.zeros_like(acc_s)
        causal = (lax.broadcasted_iota(jnp.int32, (t, t), 1) <= lax.broadcasted_iota(jnp.int32, (t, t), 0))

        def step(j, diagonal):
            r0 = pl.multiple_of(j * t, t)
            kj = k_ref[pl.ds(r0, t), :]
            vj = v_ref[pl.ds(r0, t), :]
            one = jnp.ones_like(vj)
            vh = (jnp.where(low, vj, one), jnp.where(low, one, vj))
            for hd in range(2):
                sc = _dot(qh[hd], kj, NT) - ck_ref[hd, j]
                if diagonal:
                    sc = jnp.where(causal, sc, -1e30)
                m_old = m_s[hd]
                m_new = jnp.maximum(m_old, jnp.max(sc, axis=1, keepdims=True))
                p = jnp.exp(sc - m_new)
                acc_s[hd] = jnp.exp(m_old - m_new) * acc_s[hd] + _dot(p.astype(BF16), vh[hd], NN)
                m_s[hd] = m_new

        def full(j, _):
            step(j, False)
            return 0

        lax.fori_loop(0, i, full, 0)
        step(i, True)
        a0, a1 = acc_s[0], acc_s[1]
        o_ref[...] = jnp.where(low, a0 / pltpu.roll(a0, dh, 1), a1 / pltpu.roll(a1, dh, 1)).astype(BF16)
        lse_ref[0] = m_s[0] + jnp.log(a0[:, dh:dh + 1])
        lse_ref[1] = m_s[1] + jnp.log(a1[:, 0:1])

    return _hosted_call(
        body, side, name, (n_pairs, nq),
        [pl.BlockSpec((t, LANES), lambda hp, i: (i, q_blk + hp)),
         pl.BlockSpec((s, LANES), lambda hp, i: (0, k_blk + hp)),
         pl.BlockSpec((s, LANES), lambda hp, i: (0, v_blk + hp)),
         pl.BlockSpec((2, nq, 1, t), lambda hp, i: (hp, 0, 0, 0))],
        [pl.BlockSpec((t, LANES), lambda hp, i: (i, hp)), pl.BlockSpec((2, t, 1), lambda hp, i: (hp, i, 0))],
        [jax.ShapeDtypeStruct((s, LANES * n_pairs), BF16), jax.ShapeDtypeStruct((2 * n_pairs, s, 1), F32)],
        [pltpu.VMEM((2, t, 1), F32), pltpu.VMEM((2, t, LANES), F32)], (qkv, qkv, qkv, ck))


def _attn_bwd(name, qkv, q_blk, k_blk, v_blk, n_pairs, o, do, lse_rows, ck_cols, side=None):
    s = qkv.shape[0]
    dh = LANES // 2
    t = min(ATT_BLOCK, s)
    nk = s // t
    scale = dh ** -0.5

    def body(q_ref, k_ref, v_ref, o_ref, do_ref, lse_ref, ck_ref,
             dq_ref, dk_ref, dv_ref, dcq_ref, dck_ref, delta, dqt, dk_acc, dv_acc):
        j = pl.program_id(1)
        low = lax.broadcasted_iota(jnp.int32, (1, LANES), 1) < dh
        low_rows = lax.broadcasted_iota(jnp.int32, (LANES, 1), 0) < dh

        @pl.when(j == 0)
        def _():
            dqt[...] = jnp.zeros_like(dqt)
            sel = (jnp.broadcast_to(low, (SUBLANES, LANES)).astype(F32), jnp.broadcast_to(~low, (SUBLANES, LANES)).astype(F32))

            def fill(i, _):
                r0 = pl.multiple_of(i * t, t)
                prod = do_ref[pl.ds(r0, t), :].astype(F32) * o_ref[pl.ds(r0, t), :].astype(F32)
                for hd in range(2):
                    delta[hd, i] = lax.dot_general(sel[hd], prod, NT, precision=lax.Precision.HIGHEST,
                                                   preferred_element_type=F32)
                return 0

            lax.fori_loop(0, nk, fill, 0)

        kj, vj = k_ref[...], v_ref[...]
        zero, one = jnp.zeros_like(kj), jnp.ones_like(kj)
        kh = (jnp.where(low, kj, zero), jnp.where(low, zero, kj))
        vh = (jnp.where(low, vj, zero), jnp.where(low, zero, vj))
        kjt = kj.astype(F32).T.astype(BF16)
        one_t = jnp.ones_like(kjt)
        kht = (jnp.where(low_rows, kjt, one_t), jnp.where(low_rows, one_t, kjt))
        dk_acc[...] = jnp.zeros_like(dk_acc)
        dv_acc[...] = jnp.zeros_like(dv_acc)
        causal_t = (lax.broadcasted_iota(jnp.int32, (t, t), 0) <= lax.broadcasted_iota(jnp.int32, (t, t), 1))

        def step(i, diagonal):
            r0 = pl.multiple_of(i * t, t)
            qi = (q_ref[pl.ds(r0, t), :].astype(F32) * scale).astype(BF16)
            doi = do_ref[pl.ds(r0, t), :]
            qone, dzero = jnp.ones_like(qi), jnp.zeros_like(doi)
            qsel = (jnp.where(low, qi, qone), jnp.where(low, qone, qi))
            dosel = (jnp.where(low, doi, dzero), jnp.where(low, dzero, doi))
            for hd in range(2):
                st = _dot(kh[hd], qi, NT) - ck_ref[hd] - lse_ref[hd, i]
                pt = jnp.exp(st)
                if diagonal:
                    pt = jnp.where(causal_t, pt, 0.0)
                dst = pt * (_dot(vh[hd], doi, NT) - delta[hd, i, 0:1, :])
                dsb = dst.astype(BF16)
                dv_acc[...] += _dot(pt.astype(BF16), dosel[hd], NN)
                dk_acc[hd] += _dot(dsb, qsel[hd], NN)
                dqt[hd, i] += _dot(kht[hd], dsb, NN)

        step(j, True)

        def rest(i, _):
            step(i, False)
            return 0

        lax.fori_loop(j + 1, nk, rest, 0)
        dk_ref[...] = jnp.where(low, dk_acc[0], dk_acc[1]).astype(BF16)
        dv_ref[...] = dv_acc[...].astype(BF16)
        dck_ref[0] = -dk_acc[0][:, dh:dh + 1]
        dck_ref[1] = -dk_acc[1][:, 0:1]

        @pl.when(j == nk - 1)
        def _():
            def emit(i, _):
                r0 = pl.multiple_of(i * t, t)
                d0, d1 = dqt[0, i], dqt[1, i]
                dq_ref[pl.ds(r0, t), :] = (jnp.where(low_rows, d0, d1) * scale).T.astype(BF16)
                dcq_ref[0, i] = d0[dh:dh + 1, :]
                dcq_ref[1, i] = d1[0:1, :]
                return 0

            lax.fori_loop(0, nk, emit, 0)

    col_blk = lambda base: pl.BlockSpec((t, LANES), lambda hp, j: (j, base + hp))
    col_all = lambda base: pl.BlockSpec((s, LANES), lambda hp, j: (0, base + hp))
    rows_all = pl.BlockSpec((2, nk, 1, t), lambda hp, j: (hp, 0, 0, 0))
    return _hosted_call(
        body, side, name, (n_pairs, nk),
        [col_all(q_blk), col_blk(k_blk), col_blk(v_blk), col_all(0), col_all(0), rows_all,
         pl.BlockSpec((2, t, 1), lambda hp, j: (hp, j, 0))],
        [col_all(0), col_blk(0), col_blk(0), rows_all, pl.BlockSpec((2, t, 1), lambda hp, j: (hp, j, 0))],
        [jax.ShapeDtypeStruct((s, LANES * n_pairs), BF16), jax.ShapeDtypeStruct((s, LANES * n_pairs), BF16),
         jax.ShapeDtypeStruct((s, LANES * n_pairs), BF16), jax.ShapeDtypeStruct((2 * n_pairs, nk, 1, t), F32),
         jax.ShapeDtypeStruct((2 * n_pairs, s, 1), F32)],
        [pltpu.VMEM((2, nk, SUBLANES, t), F32), pltpu.VMEM((2, nk, LANES, t), F32),
         pltpu.VMEM((2, t, LANES), F32), pltpu.VMEM((t, LANES), F32)],
        (qkv, qkv, qkv, o, do, lse_rows, ck_cols))


def _adamw(name, w, g, m, v):
    n_l, r, c = w.shape
    tr = _pick8(r, max(SUBLANES, (1 << 20) // (4 * c)))

    def body(w_ref, g_ref, m_ref, v_ref, d_ref, mo_ref, vo_ref):
        gv = g_ref[...]
        m2 = ADAM_B1 * m_ref[...] + (1.0 - ADAM_B1) * gv
        v2 = ADAM_B2 * v_ref[...] + (1.0 - ADAM_B2) * (gv * gv)
        m_hat = m2 / (1.0 - ADAM_B1 ** ADAM_STEP)
        v_hat = v2 / (1.0 - ADAM_B2 ** ADAM_STEP)
        d_ref[...] = -ADAM_LR * (m_hat / (jnp.sqrt(v_hat) + ADAM_EPS) + ADAM_WD * w_ref[...])
        mo_ref[...] = m2
        vo_ref[...] = v2

    blk = pl.BlockSpec((None, tr, c), lambda l, i: (l, i, 0))
    sh = jax.ShapeDtypeStruct((n_l, r, c), F32)
    return pl.pallas_call(body, name=name, grid=(n_l, r // tr), in_specs=[blk] * 4, out_specs=[blk] * 3,
                          out_shape=[sh, sh, sh], compiler_params=_cparams())(w, g, m, v)


def _pick8(dim, target, mult=SUBLANES):
    best, t = None, mult
    while t <= min(dim, target):
        if dim % t == 0:
            best = t
        t += mult
    return best or dim


BF16_ROWS = 16


def _sum_blocks(name, x, out_dtype):
    n, r, c = x.shape
    tr = _pick8(r, max(BF16_ROWS, (1 << 19) // (4 * c)), BF16_ROWS)

    def body(x_ref, o_ref):
        acc = x_ref[0].astype(F32)
        for i in range(1, n):
            acc = acc + x_ref[i].astype(F32)
        o_ref[...] = acc.astype(out_dtype)

    return pl.pallas_call(body, name=name, grid=(r // tr,),
                          in_specs=[pl.BlockSpec((n, tr, c), lambda i: (0, i, 0))],
                          out_specs=pl.BlockSpec((tr, c), lambda i: (i, 0)),
                          out_shape=jax.ShapeDtypeStruct((r, c), out_dtype), compiler_params=_cparams())(x)


def _all_gather(name, x_shard):
    m_per, n = x_shard.shape

    def body(x_ref, out_ref, send_sems, recv_sems):
        x, y, c = lax.axis_index("x"), lax.axis_index("y"), lax.axis_index("c")
        me, sibling = (x, y, c), (x, y, 1 - c)
        chips = [(1 - x, y), (x, 1 - y), (1 - x, 1 - y)]

        def rows(px, py, pc):
            return out_ref.at[pl.ds((4 * px + 2 * py + pc) * m_per, m_per), :]

        def copy(k, block, to, src=None):
            return pltpu.make_async_remote_copy(
                src_ref=rows(*block) if src is None else src, dst_ref=rows(*block),
                send_sem=send_sems.at[k], recv_sem=recv_sems.at[k], device_id=to, device_id_type=MESH)

        first = [copy(0, me, sibling, src=x_ref)]
        first += [copy(1 + j, me, (*chip, c), src=x_ref) for j, chip in enumerate(chips)]
        for cp in first:
            cp.start()
        passed = [copy(4 + j, (*chip, c), sibling) for j, chip in enumerate(chips)]
        for j, chip in enumerate(chips):
            copy(1 + j, (*chip, c), me).wait_recv()
            passed[j].start()
        copy(0, sibling, me).wait_recv()
        for j, chip in enumerate(chips):
            copy(4 + j, (*chip, 1 - c), me).wait_recv()
        for cp in first + passed:
            cp.wait_send()

    out = pl.pallas_call(
        body, name=name, out_shape=jax.ShapeDtypeStruct((N_DEV * m_per, n), x_shard.dtype),
        in_specs=[pl.BlockSpec(memory_space=pl.ANY)], out_specs=pl.BlockSpec(memory_space=pl.ANY),
        scratch_shapes=[pltpu.SemaphoreType.DMA((7,)), pltpu.SemaphoreType.DMA((7,))],
    )(x_shard)
    my_dev = 4 * lax.axis_index("x") + 2 * lax.axis_index("y") + lax.axis_index("c")
    return lax.dynamic_update_slice(out, x_shard, (my_dev * m_per, 0))


def _put_own(out, own, index):
    start = tuple(index) + (0,) * own.ndim
    return lax.dynamic_update_slice(out, own.reshape((1,) * len(index) + own.shape), start)


def _gather_copies(stage, ins, outs, send_sems, recv_sems):
    x, y, c = lax.axis_index("x"), lax.axis_index("y"), lax.axis_index("c")
    my_chip = 2 * x + y
    copies = []
    for w, out in enumerate(outs):
        half = out.shape[1] // 2
        rows = pl.ds(c * half, half)
        for k, (cx, cy) in enumerate([(1 - x, y), (x, 1 - y), (1 - x, 1 - y)]):
            if stage == 0:
                src, dst, to = ins[w].at[rows], out.at[my_chip, rows], (cx, cy, c)
            else:
                src = dst = out.at[2 * cx + cy, rows]
                to = (x, y, 1 - c)
            copies.append(pltpu.make_async_remote_copy(
                src_ref=src, dst_ref=dst, send_sem=send_sems.at[3 * w + k], recv_sem=recv_sems.at[3 * w + k],
                device_id=to, device_id_type=MESH))
    return copies


def _gathered_shapes(shards):
    return [jax.ShapeDtypeStruct((N_CHIPS,) + s.shape, s.dtype) for s in shards]


def _put_own_slabs(gathered, shards):
    my_chip = 2 * lax.axis_index("x") + lax.axis_index("y")
    return [_put_own(o, s, (my_chip,)) for o, s in zip(gathered, shards)]


def _gather_layer(name, shards):
    n_w = len(shards)

    def body(*refs):
        ins, outs = refs[:n_w], refs[n_w:2 * n_w]
        for stage in (0, 1):
            copies = _gather_copies(stage, ins, outs, refs[2 * n_w + 2 * stage], refs[2 * n_w + 2 * stage + 1])
            for cp in copies:
                cp.start()
            for cp in copies:
                cp.wait()

    outs = pl.pallas_call(
        body, name=name, out_shape=_gathered_shapes(shards),
        in_specs=[pl.BlockSpec(memory_space=pl.ANY)] * n_w, out_specs=[pl.BlockSpec(memory_space=pl.ANY)] * n_w,
        scratch_shapes=[pltpu.SemaphoreType.DMA((3 * n_w,))] * 4,
    )(*shards)
    return _put_own_slabs(outs, shards)


def _gather_side_jobs(shards):
    n_w = len(shards)
    between_chips = _SideJob(list(shards), _gathered_shapes(shards), {}, 3 * n_w,
                             lambda ins, outs, send, recv: _gather_copies(0, ins, outs, send, recv))
    between_cores = lambda partial: _SideJob(list(partial), _gathered_shapes(shards), {w: w for w in range(n_w)}, 3 * n_w,
                                             lambda ins, outs, send, recv: _gather_copies(1, ins, outs, send, recv))
    return between_chips, between_cores


def _run_job(name, job):
    n_in, n_out = len(job.arrays), len(job.out_shapes)

    def body(*refs):
        copies = job.copies(refs[:n_in], refs[n_in:n_in + n_out], refs[n_in + n_out], refs[n_in + n_out + 1])
        for cp in copies:
            cp.start()
        for cp in copies:
            cp.wait()

    hbm = pl.BlockSpec(memory_space=pl.ANY)
    return pl.pallas_call(
        body, name=name, out_shape=list(job.out_shapes), in_specs=[hbm] * n_in, out_specs=[hbm] * n_out,
        scratch_shapes=[pltpu.SemaphoreType.DMA((job.n_sems,))] * 2, input_output_aliases=dict(job.aliases),
    )(*job.arrays)


def _swap_job(grads):
    def copies(ins, outs, send_sems, recv_sems):
        x, y, c = lax.axis_index("x"), lax.axis_index("y"), lax.axis_index("c")
        return [pltpu.make_async_remote_copy(
            src_ref=g.at[:, pl.ds((1 - c) * (g.shape[1] // 2), g.shape[1] // 2)], dst_ref=outs[w],
            send_sem=send_sems.at[w], recv_sem=recv_sems.at[w], device_id=(x, y, 1 - c), device_id_type=MESH)
            for w, g in enumerate(ins)]

    shapes = [jax.ShapeDtypeStruct((g.shape[0], g.shape[1] // 2, g.shape[2]), g.dtype) for g in grads]
    return _SideJob(list(grads), shapes, {}, len(grads), copies)


def _exchange_job(parts):
    def copies(ins, outs, send_sems, recv_sems):
        x, y, c = lax.axis_index("x"), lax.axis_index("y"), lax.axis_index("c")
        return [pltpu.make_async_remote_copy(
            src_ref=ins[w].at[2 * cx + cy], dst_ref=outs[w].at[2 * x + y], send_sem=send_sems.at[3 * w + k],
            recv_sem=recv_sems.at[3 * w + k], device_id=(cx, cy, c), device_id_type=MESH)
            for w in range(len(ins)) for k, (cx, cy) in enumerate([(1 - x, y), (x, 1 - y), (1 - x, 1 - y)])]

    return _SideJob(list(parts), [jax.ShapeDtypeStruct(p.shape, p.dtype) for p in parts], {}, 3 * len(parts), copies)


def _share_job(reduced, layer, depth, into):
    n_w = len(reduced)

    def copies(ins, outs, send_sems, recv_sems):
        x, y, c = lax.axis_index("x"), lax.axis_index("y"), lax.axis_index("c")
        return [pltpu.make_async_remote_copy(
            src_ref=ins[w], dst_ref=outs[w].at[layer, pl.ds(c * ins[w].shape[0], ins[w].shape[0])],
            send_sem=send_sems.at[w], recv_sem=recv_sems.at[w], device_id=(x, y, 1 - c), device_id_type=MESH)
            for w in range(n_w)]

    shapes = [jax.ShapeDtypeStruct((depth, 2 * r.shape[0], r.shape[1]), r.dtype) for r in reduced]
    if into is None:
        return _SideJob(list(reduced), shapes, {}, n_w, copies)
    return _SideJob(list(reduced) + list(into), shapes, {n_w + w: w for w in range(n_w)}, n_w, copies)


def _add_rows(name, grads, recv, core):
    n, r, c = recv.shape
    tr = _pick8(r, max(BF16_ROWS, (1 << 19) // (4 * c)), BF16_ROWS)
    steps = r // tr

    def body(core_ref, g_ref, r_ref, o_ref):
        o_ref[...] = (g_ref[...].astype(F32) + r_ref[...].astype(F32)).astype(BF16)

    grid_spec = pltpu.PrefetchScalarGridSpec(
        num_scalar_prefetch=1, grid=(steps,),
        in_specs=[pl.BlockSpec((n, tr, c), lambda i, core_ref: (0, core_ref[0] * steps + i, 0)),
                  pl.BlockSpec((n, tr, c), lambda i, core_ref: (0, i, 0))],
        out_specs=pl.BlockSpec((n, tr, c), lambda i, core_ref: (0, i, 0)))
    return pl.pallas_call(body, name=name, grid_spec=grid_spec,
                          out_shape=jax.ShapeDtypeStruct((n, r, c), BF16), compiler_params=_cparams())(core, grads, recv)


class _LayerReduce:
    def __init__(self, tag, layer, depth, grads, core, into):
        self.tag, self.layer, self.depth, self.core, self.into = tag, layer, depth, core, into
        self.state = list(grads)

    def _exchange(self, name, job, carry):
        if carry is None:
            return None, _run_job(f"{name}_{self.tag}", job)
        return carry(job)

    def swap_and_add(self, carry=None):
        grads = self.state
        results, recv = self._exchange("grads_swap_cores", _swap_job(grads), carry)
        self.state = [_add_rows(f"grads_add_{n}_{self.tag}", g, r, self.core) for n, g, r in zip(BIG, grads, recv)]
        return results

    def exchange_and_sum(self, carry=None):
        parts = self.state
        results, arrived = self._exchange("grads_exchange_chips", _exchange_job(parts), carry)
        my_chip = 2 * lax.axis_index("x") + lax.axis_index("y")
        arrived = [_put_own(a, lax.dynamic_index_in_dim(p, my_chip, 0, keepdims=False), (my_chip,))
                   for a, p in zip(arrived, parts)]
        self.state = [_sum_blocks(f"grads_sum_{n}_{self.tag}", a, F32) for n, a in zip(BIG, arrived)]
        return results

    def share(self, carry=None):
        reduced = self.state
        results, outs = self._exchange("grads_share_cores", _share_job(reduced, self.layer, self.depth, self.into), carry)
        self.state =[lax.dynamic_update_slice(o, r[None], (self.layer, lax.axis_index("c") * r.shape[0], 0))
                      for o, r in zip(outs, reduced)]
        return results


def _pack(arrays, cols, row_multiple, dtype):
    flat = jnp.concatenate([a.reshape(-1).astype(dtype) for a in arrays])
    unit = cols * row_multiple
    total = -(-flat.shape[0] // unit) * unit
    return jnp.pad(flat, (0, total - flat.shape[0])).reshape(total // cols, cols)


def _unpack(buf, shapes):
    flat, out, off = buf.reshape(-1), [], 0
    for sh in shapes:
        n = math.prod(sh)
        out.append(flat[off:off + n].reshape(sh))
        off += n
    return out


def _discretize(lam_re, lam_im, log_dt, b_re, b_im):
    lam = lax.complex(jnp.minimum(lam_re, -EIG_CLIP), lam_im)
    dt = jnp.exp(log_dt)[:, None]
    lam_bar = jnp.exp(lam * dt)
    b_bar = ((lam_bar - 1.0) / lam)[..., None] * lax.complex(b_re, b_im)
    return jnp.real(lam_bar), jnp.imag(lam_bar), jnp.real(b_bar), jnp.imag(b_bar)


def _scan_tables(ar, ai):
    a = lax.complex(ar, ai)
    pw = [a]
    for _ in range(7):
        pw.append(pw[-1] * a)
    rows = jnp.arange(SUBLANES)[:, None]

    def build(p, reverse):
        tabs = []
        for k in (1, 2, 4):
            keep = (rows <= SUBLANES - 1 - k) if reverse else (rows >= k)
            tk = jnp.where(keep, p[k - 1][None, :], 0.0)
            tabs += [jnp.real(tk), jnp.imag(tk)]
        stack = jnp.stack(p[::-1] if reverse else p)
        tabs += [jnp.real(stack), jnp.imag(stack)]
        return jnp.stack(tabs).astype(F32)

    return build(pw, False), build([jnp.conj(p) for p in pw], True)


def _interleave_rows(a, t):
    s, w = a.shape
    return a.reshape(s // t, SUBLANES, t // SUBLANES, w).transpose(0, 2, 1, 3).reshape(s, w)


def _deinterleave_rows(a, t):
    s, w = a.shape
    return a.reshape(s // t, t // SUBLANES, SUBLANES, w).transpose(0, 2, 1, 3).reshape(s, w)


def _block_diag(per_group, groups_per_block):
    g, a, b = per_group.shape
    x = per_group.reshape(g // groups_per_block, groups_per_block, a, b)
    eye = jnp.eye(groups_per_block, dtype=per_group.dtype)
    out = x[:, :, :, None, :] * eye[None, :, None, :, None]
    return out.reshape(g // groups_per_block, groups_per_block * a, groups_per_block * b)


def _block_diag_extract(dense, groups_per_block, a, b):
    nkb = dense.shape[0]
    x = dense.reshape(nkb, groups_per_block, a, groups_per_block, b)
    idx = jnp.arange(groups_per_block)
    return x[:, idx, :, idx, :].transpose(1, 0, 2, 3).reshape(nkb * groups_per_block, a, b)


def _layer_fwd(tag, x, mod, p, wts, gather_next=None):
    s, d = x.shape
    w_ssm, w_att = p["w_glu"].shape[0], wts["w_pb"].shape[1]
    heads = p["b_f"].shape[0]
    dh = w_att // heads
    cs = d // N_CHIPS
    fs = wts["w_ffn_down"].shape[1]
    tm = _pick(s, 1024)
    row = lambda v: v.reshape(1, -1)
    sv = {}

    h = _prenorm_fwd(f"prenorm_mix_{tag}", x, row(p["g_pre_mix"]), row(mod[1]), row(mod[0]))
    uqkv = _mm_plain(f"proj_main_{tag}", h, p["w_main"], "nn", BF16, tm=1024, tn=1024, tk=1024)
    fg = _mm_plain(f"proj_gate_{tag}", h, p["w_gates"], "nn", F32, tm=1024, tn=1024, tk=1024)
    f_t = fg[:, 2 * d:2 * d + heads].T

    t5 = min(S5_ROWS, s)
    u_il = _interleave_rows(uqkv[:, :w_ssm], t5)
    y_s5, ys_il, carries = _s5_fwd(f"s5_fwd_{tag}", u_il, p["b_blk"], p["c_blk"], p["a_f"], p["tab_f"],
                                   row(p["d_skip"]), p["w_glu"], row(p["b_glu"]))
    ys = _deinterleave_rows(ys_il, t5)

    assert dh * 2 == LANES and w_ssm % LANES == 0 and w_att % LANES == 0
    n_pairs = w_att // LANES
    blocks = (w_ssm // LANES, w_ssm // LANES + n_pairs, w_ssm // LANES + 2 * n_pairs)
    cum = _cum_fwd(f"cum_fwd_{tag}", f_t, p["b_f"].reshape(heads, 1))
    t = min(ATT_BLOCK, s)
    ck_cols, ck_rows = cum.reshape(heads, s, 1), cum.reshape(heads, s // t, 1, t)
    (ya, lse), arrived = _attn_fwd(f"attn_fwd_{tag}", uqkv, *blocks, n_pairs, ck_rows,
                                   side=gather_next[0] if gather_next else None)

    tile = pl.BlockSpec((tm, cs), lambda i, j, k: (i, j))
    slab = lambda rows: pl.BlockSpec((None, rows, cs), lambda i, j, k: (j, 0, 0))

    def merge(acc, extra_refs, out_refs):
        ya_ref, wpb_ref, ga_ref, gb_ref = extra_refs
        a_ref, b_ref, m_ref = out_refs
        bv = _dot(ya_ref[...], wpb_ref[...], NN)
        a_ref[...] = acc.astype(BF16)
        b_ref[...] = bv.astype(BF16)
        m_ref[...] = (_sigmoid(ga_ref[...]) * acc + _sigmoid(gb_ref[...]) * bv).astype(BF16)

    sd_bf = jax.ShapeDtypeStruct((s, d), BF16)
    pa, pb, merged = _mm_raw(
        f"merge_{tag}", ys, wts["w_pa"], "nn", (s // tm, N_CHIPS, 1), (tm, cs),
        pl.BlockSpec((tm, w_ssm), lambda i, j, k: (i, 0)), slab(w_ssm), [sd_bf] * 3, [tile] * 3, merge,
        extra=(ya, wts["w_pb"], fg, fg),
        extra_specs=[pl.BlockSpec((tm, w_att), lambda i, j, k: (i, 0)), slab(w_att), tile,
                     pl.BlockSpec((tm, cs), lambda i, j, k: (i, j + N_CHIPS))])

    tm2 = _pick(s, POSTNORM_ROWS)
    x1, y_mix = _mm_postnorm(
        f"out_proj_{tag}", merged, pl.BlockSpec((tm2, cs), lambda i, j, k: (i, k)), wts["w_o"],
        pl.BlockSpec((None, cs, d), lambda i, j, k: (k, 0, 0)), N_CHIPS, x, row(mod[2]), row(p["g_post_mix"]))

    h2 = _prenorm_fwd(f"prenorm_ffn_{tag}", x1, row(p["g_pre_ffn"]), row(mod[4]), row(mod[3]))
    (a4, b4, hid4), next_wts = _ffn_up(f"ffn_up_{tag}", h2, wts["w_ffn_gate"], wts["w_ffn_up"],
                                       side=gather_next[1](arrived) if gather_next else None)
    x2, y_ffn = _mm_postnorm(
        f"ffn_down_{tag}", hid4, pl.BlockSpec((None, tm2, fs), lambda i, j, k: (k, i, 0)), wts["w_ffn_down"],
        pl.BlockSpec((None, fs, d), lambda i, j, k: (k, 0, 0)), N_CHIPS, x1, row(mod[5]), row(p["g_post_ffn"]))

    sv.update(x=x, h=h, uqkv=uqkv, u_il=u_il, fg=fg, f_t=f_t, y_s5=y_s5, ys=ys, carries=carries, blocks=blocks,
              ck_cols=ck_cols, lse_rows=lse.reshape(heads, s // t, 1, t), ya=ya, pa=pa, pb=pb, merged=merged, x1=x1,
              y_mix=y_mix, h2=h2, a4=a4, b4=b4, hid4=hid4, y_ffn=y_ffn)
    return x2, sv, next_wts


def _mm_postnorm(name, a, a_spec, w, w_spec, nk, x, gate, g):
    s, d = x.shape
    tm = _pick(s, POSTNORM_ROWS)
    rowspec = pl.BlockSpec((tm, d), lambda i, j, k: (i, 0))
    vec = pl.BlockSpec((1, d), lambda i, j, k: (0, 0))

    def epilogue(acc, extra_refs, out_refs):
        x_ref, gate_ref, g_ref = extra_refs
        r = lax.rsqrt(jnp.mean(acc * acc, axis=-1, keepdims=True) + RMS_EPS)
        out_refs[0][...] = x_ref[...] + gate_ref[...] * (acc * r * g_ref[...])
        out_refs[1][...] = acc

    sd = jax.ShapeDtypeStruct((s, d), F32)
    return _mm_raw(name, a, w, "nn", (s // tm, 1, nk), (tm, d), a_spec, w_spec, [sd, sd], [rowspec, rowspec], epilogue,
                   extra=(x, gate, g), extra_specs=[rowspec, vec, vec])


def _layer_bwd(tag, dx2, mod, p, wts, sv, reduce_later=None):
    s, d = dx2.shape
    w_ssm, w_att = p["w_glu"].shape[0], wts["w_pb"].shape[1]
    heads = p["b_f"].shape[0]
    cs = d // N_CHIPS
    fs = wts["w_ffn_down"].shape[1]
    tm, tk, td = _pick(s, 1024), _pick(s, 1024), d
    row = lambda v: v.reshape(1, -1)
    gr = {}

    def dw_slabs(name, act, act_spec, rows, dy, dy_spec, cols, grid_mn, out_index):
        return _mm_raw(name, act, dy, "tn", grid_mn + (s // tk,), (rows, cols), act_spec, dy_spec,
                       [jax.ShapeDtypeStruct((N_CHIPS,) + out_index[1], BF16)],
                       [pl.BlockSpec((None, rows, cols), out_index[0])], _store(BF16))[0]

    dy_ffn, sums = _postnorm_bwd(f"postnorm_bwd_ffn_{tag}", dx2, sv["y_ffn"], row(p["g_post_ffn"]), row(mod[5]))
    d_gate_f, gr["g_post_ffn"] = sums[0], sums[1]
    gr["w_ffn_down"] = dw_slabs(f"dw_down_{tag}", sv["hid4"], pl.BlockSpec((None, tk, fs), lambda i, j, k: (i, k, 0)), fs,
                                dy_ffn, pl.BlockSpec((tk, d), lambda i, j, k: (k, 0)), d, (N_CHIPS, 1),
                                (lambda i, j, k: (i, 0, 0), (fs, d)))

    def swiglu_bwd(acc, extra_refs, out_refs):
        av, bv = extra_refs[0][...].astype(F32), extra_refs[1][...].astype(F32)
        sg = _sigmoid(av)
        out_refs[0][...] = (acc * bv * (sg * (1.0 + av * (1.0 - sg)))).astype(BF16)
        out_refs[1][...] = (acc * (av * sg)).astype(BF16)

    blk4 = pl.BlockSpec((None, tm, fs), lambda i, j, k: (j, i, 0))
    sh4 = jax.ShapeDtypeStruct((N_CHIPS, s, fs), BF16)
    ffn_down_bwd = lambda side: _mm_raw(
        f"ffn_down_bwd_{tag}", dy_ffn, wts["w_ffn_down"], "nt", (s // tm, N_CHIPS, 1), (tm, fs),
        pl.BlockSpec((tm, d), lambda i, j, k: (i, 0)), pl.BlockSpec((None, fs, d), lambda i, j, k: (j, 0, 0)),
        [sh4, sh4], [blk4, blk4], swiglu_bwd, extra=(sv["a4"], sv["b4"]), extra_specs=[blk4, blk4], side=side)
    da4, db4 = reduce_later.swap_and_add(ffn_down_bwd) if reduce_later else ffn_down_bwd(None)
    for n, act4 in (("w_ffn_gate", da4), ("w_ffn_up", db4)):
        gr[n] = dw_slabs(f"d{n}_{tag}", sv["h2"], pl.BlockSpec((tk, td), lambda i, j, k: (k, i)), td,
                         act4, pl.BlockSpec((None, tk, fs), lambda i, j, k: (j, k, 0)), fs, (d // td, N_CHIPS),
                         (lambda i, j, k: (j, i, 0), (d, fs)))
    pairs = [(act4, (None, tm, fs), lambda i, kk: (kk, i, 0), wts[n], (None, td, fs), lambda j, kk: (kk, j, 0),
              N_CHIPS) for n, act4 in (("w_ffn_gate", da4), ("w_ffn_up", db4))]
    dh2 = _mm_sum(f"dh_ffn_{tag}", s, d, tm, td, pairs, F32)
    dx1, sums = _prenorm_bwd(f"prenorm_bwd_ffn_{tag}", dh2, sv["x1"], row(p["g_pre_ffn"]), row(mod[4]), dx2)
    d_scale_f, d_shift_f, gr["g_pre_ffn"] = sums[0], sums[1], sums[2]

    dy_mix, sums = _postnorm_bwd(f"postnorm_bwd_mix_{tag}", dx1, sv["y_mix"], row(p["g_post_mix"]), row(mod[2]))
    d_gate_m, gr["g_post_mix"] = sums[0], sums[1]
    gr["w_o"] = dw_slabs(f"dw_o_{tag}", sv["merged"], pl.BlockSpec((tk, cs), lambda i, j, k: (k, i)), cs,
                         dy_mix, pl.BlockSpec((tk, d), lambda i, j, k: (k, 0)), d, (N_CHIPS, 1),
                         (lambda i, j, k: (i, 0, 0), (cs, d)))

    tile = pl.BlockSpec((tm, cs), lambda i, j, k: (i, j))

    def merge_bwd(acc, extra_refs, out_refs):
        a_ref, b_ref, ga_ref, gb_ref = extra_refs
        sa, sb = _sigmoid(ga_ref[...]), _sigmoid(gb_ref[...])
        out_refs[0][...] = (acc * sa).astype(BF16)
        out_refs[1][...] = (acc * sb).astype(BF16)
        out_refs[2][...] = (acc * a_ref[...].astype(F32) * sa * (1.0 - sa)).astype(BF16)
        out_refs[3][...] = (acc * b_ref[...].astype(F32) * sb * (1.0 - sb)).astype(BF16)

    sd_bf = jax.ShapeDtypeStruct((s, d), BF16)
    d_pa, d_pb, d_ga, d_gb = _mm_raw(
        f"out_proj_bwd_{tag}", dy_mix, wts["w_o"], "nt", (s // tm, N_CHIPS, 1), (tm, cs),
        pl.BlockSpec((tm, d), lambda i, j, k: (i, 0)), pl.BlockSpec((None, cs, d), lambda i, j, k: (j, 0, 0)),
        [sd_bf] * 4, [tile] * 4, merge_bwd, extra=(sv["pa"], sv["pb"], sv["fg"], sv["fg"]),
        extra_specs=[tile, tile, tile, pl.BlockSpec((tm, cs), lambda i, j, k: (i, j + N_CHIPS))])
    d_branch = {}
    for n, act, width, d_p in (("w_pa", sv["ys"], w_ssm, d_pa), ("w_pb", sv["ya"], w_att, d_pb)):
        gr[n] = dw_slabs(f"d{n}_{tag}", act, pl.BlockSpec((tk, width), lambda i, j, k: (k, 0)), width,
                         d_p, pl.BlockSpec((tk, cs), lambda i, j, k: (k, j)), cs, (1, N_CHIPS),
                         (lambda i, j, k: (j, 0, 0), (width, cs)))
        d_branch[n] = _mm_raw(
            f"d_in_{n}_{tag}", d_p, wts[n], "nt", (s // tm, 1, N_CHIPS), (tm, width),
            pl.BlockSpec((tm, cs), lambda i, j, k: (i, k)), pl.BlockSpec((None, width, cs), lambda i, j, k: (k, 0, 0)),
            [jax.ShapeDtypeStruct((s, width), BF16)], [pl.BlockSpec((tm, width), lambda i, j, k: (i, 0))], _store(BF16))[0]
    d_ys, d_ya = d_branch["w_pa"], d_branch["w_pb"]

    attn_bwd = lambda side: _attn_bwd(f"attn_bwd_{tag}", sv["uqkv"], *sv["blocks"], w_att // LANES, sv["ya"], d_ya,
                                      sv["lse_rows"], sv["ck_cols"], side=side)
    dq, dk, dv, dcq, dck = reduce_later.exchange_and_sum(attn_bwd) if reduce_later else attn_bwd(None)[0]
    d_f_t, d_bf = _cum_bwd(f"cum_bwd_{tag}", dcq.reshape(heads, s), dck.reshape(heads, s), sv["f_t"],
                           p["b_f"].reshape(heads, 1))
    gr["b_f"] = d_bf[:, 0]

    t5 = min(S5_ROWS, s)
    du_il, d_bblk, d_cblk, d_abar, d_wglu, vec = _s5_bwd(
        f"s5_bwd_{tag}", sv["u_il"], _interleave_rows(d_ys, t5), sv["y_s5"], sv["carries"], p["b_blk"], p["c_blk"],
        p["a_f"], p["a_r"], p["tab_f"], p["tab_r"], row(p["d_skip"]), p["w_glu"], row(p["b_glu"]))
    du = _deinterleave_rows(du_il, t5)
    gr["w_glu"] = d_wglu.astype(BF16).reshape(N_CHIPS, w_ssm // N_CHIPS, w_ssm)
    gr["b_glu"], gr["d_skip"] = vec[0], vec[1]
    gr["b_blk"], gr["c_blk"], gr["a_bar"] = d_bblk, d_cblk, d_abar

    d_f = jnp.pad(d_f_t.T, ((0, 0), (0, F_PAD - heads))).astype(BF16)
    assert w_ssm % w_att == 0 and (2 * d) % F_PAD == 0
    first = w_ssm // w_att
    main_pieces = [(du, w_ssm, 0), (dq, w_att, first), (dk, w_att, first + 1), (dv, w_att, first + 2)]
    dw = [_mm_plain(f"dw_in{n}_{tag}", sv["h"], piece, "tn", BF16, tm=1024, tn=1024, tk=1024)
          for n, piece in enumerate([du, dq, dk, dv, d_f, d_ga, d_gb])]
    w_in_grad = jnp.concatenate(dw[:4] + [dw[4][:, :heads], dw[5], dw[6]], axis=1)
    gr["w_in"] = w_in_grad.reshape(d, N_CHIPS, w_in_grad.shape[1] // N_CHIPS).transpose(1, 0, 2)
    tmx, tkx = _pick(s, 1024), _pick(d, 512)
    pairs = [(piece, (tmx, width), lambda i, kk: (i, 0), p["w_main"], (d, width), lambda j, kk, blk=blk: (j, blk), 1)
             for piece, width, blk in main_pieces]
    steps = d // tkx
    pairs += [(piece, (tmx, tkx), lambda i, kk: (i, kk), p["w_gates"], (d, tkx), lambda j, kk, off=off: (j, off + kk), steps)
              for piece, off in ((d_ga, 0), (d_gb, steps))]
    pairs.append((d_f, (tmx, F_PAD), lambda i, kk: (i, 0), p["w_gates"], (d, F_PAD), lambda j, kk: (j, 2 * d // F_PAD), 1))
    dh_mix = lambda side: _mm_sum(f"dh_mix_{tag}", s, d, tmx, d, pairs, F32, side=side)
    dh1 = reduce_later.share(dh_mix) if reduce_later else dh_mix(None)
    dx0, sums = _prenorm_bwd(f"prenorm_bwd_mix_{tag}", dh1, sv["x"], row(p["g_pre_mix"]), row(mod[1]), dx1)
    d_scale_m, d_shift_m, gr["g_pre_mix"] = sums[0], sums[1], sums[2]

    d_mod = jnp.stack([d_shift_m, d_scale_m, d_gate_m, d_shift_f, d_scale_f, d_gate_f])
    return dx0, d_mod, gr


BIG = ("w_in", "w_glu", "w_pa", "w_pb", "w_o", "w_ffn_gate", "w_ffn_up", "w_ffn_down")
SMALL = ("b_ada", "g_pre_mix", "g_post_mix", "g_pre_ffn", "g_post_ffn", "lam_re", "lam_im", "log_dt", "b_re", "b_im",
         "c_re", "c_im", "d_skip", "b_glu", "b_f")
WEIGHTS = ("w_ada", "b_ada", "g_pre_mix", "g_post_mix", "g_pre_ffn", "g_post_ffn", "w_in", "lam_re", "lam_im", "log_dt",
           "b_re", "b_im", "c_re", "c_im", "d_skip", "w_glu", "b_glu", "b_f", "w_pa", "w_pb", "w_o", "w_ffn_gate",
           "w_ffn_up", "w_ffn_down")


def _prepare_layer(wts, small, l, seq):
    w_in = jnp.concatenate([wts["w_in"][j] for j in range(N_CHIPS)], axis=1)
    d = w_in.shape[0]
    heads = small["b_f"].shape[1]
    n_groups, n_state, group_ch = small["b_re"].shape[1:]
    w_ssm = n_groups * group_ch
    w_att = wts["w_pb"].shape[1]
    n_main = w_ssm + 3 * w_att
    gpb = LANES // group_ch
    p = {}
    p["w_main"] = w_in[:, :n_main]
    p["w_gates"] = jnp.concatenate(
        [w_in[:, n_main + heads:], w_in[:, n_main:n_main + heads], jnp.zeros((d, F_PAD - heads), BF16)], axis=1)
    p["w_glu"] = wts["w_glu"].reshape(w_ssm, w_ssm)
    for n in ("g_pre_mix", "g_post_mix", "g_pre_ffn", "g_post_ffn", "d_skip", "b_glu", "b_f"):
        p[n] = small[n][l]
    ar, ai, br, bi = _discretize(small["lam_re"][l], small["lam_im"][l], small["log_dt"][l], small["b_re"][l], small["b_im"][l])
    n_steps = min(S5_ROWS, seq) // SUBLANES
    powers = jnp.cumprod(jnp.broadcast_to(lax.complex(ar, ai).reshape(1, -1), (n_steps, ar.size)), axis=0)
    p["a_f"] = jnp.concatenate([jnp.real(powers), jnp.imag(powers)], axis=1)
    p["a_r"] = jnp.concatenate([jnp.real(powers[::-1]), -jnp.imag(powers[::-1])], axis=1)
    p["tab_f"], p["tab_r"] = _scan_tables(jnp.real(powers[-1]), jnp.imag(powers[-1]))
    bre = _block_diag(br.transpose(0, 2, 1), gpb)
    bim = _block_diag(bi.transpose(0, 2, 1), gpb)
    p["b_blk"] = jnp.concatenate([bre, bim], axis=2).astype(BF16)
    cre = _block_diag(small["c_re"][l].transpose(0, 2, 1), gpb)
    cim = _block_diag(small["c_im"][l].transpose(0, 2, 1), gpb)
    p["c_blk"] = jnp.concatenate([cre, -cim], axis=1).astype(BF16)
    return p


def _compact_partials(gr, n_state, group_ch):
    gpb = LANES // group_ch
    half = gpb * n_state
    out = dict(gr)
    out["bbar_re"] = _block_diag_extract(gr["b_blk"][:, :, :half], gpb, group_ch, n_state).transpose(0, 2, 1)
    out["bbar_im"] = _block_diag_extract(gr["b_blk"][:, :, half:], gpb, group_ch, n_state).transpose(0, 2, 1)
    out["c_re"] = _block_diag_extract(gr["c_blk"][:, :half, :], gpb, n_state, group_ch).transpose(0, 2, 1)
    out["c_im"] = -_block_diag_extract(gr["c_blk"][:, half:, :], gpb, n_state, group_ch).transpose(0, 2, 1)
    return out


def _small_grads_from_partials(gr, small, l):
    n_groups, n_state, _ = small["b_re"].shape[1:]
    ns2 = n_groups * n_state
    d_abar = jnp.sum(gr["a_bar"], axis=0)
    dar, dai = d_abar[:ns2].reshape(n_groups, n_state), d_abar[ns2:].reshape(n_groups, n_state)
    args = (small["lam_re"][l], small["lam_im"][l], small["log_dt"][l], small["b_re"][l], small["b_im"][l])
    _, vjp = jax.vjp(_discretize, *args)
    d_lam_re, d_lam_im, d_log_dt, d_b_re, d_b_im = vjp((dar, dai, gr["bbar_re"], gr["bbar_im"]))
    return dict(lam_re=d_lam_re, lam_im=d_lam_im, log_dt=d_log_dt, b_re=d_b_re, b_im=d_b_im,
                c_re=gr["c_re"], c_im=gr["c_im"])


def _fwd_bwd(xs, target, mods, small, wts0, later, core=None):
    depth = 1 + len(later)
    saved, layers, wts = [], [], [wts0]
    act = xs
    for l in range(depth):
        layers.append(_prepare_layer(wts[l], small, l, xs.shape[0]))
        shards = later[l] if l + 1 < depth and not isinstance(later[l], dict) else None
        act, sv, gathered = _layer_fwd(str(l), act, mods[l], layers[l], wts[l],
                                       gather_next=_gather_side_jobs(shards) if shards is not None else None)
        saved.append(sv)
        if l + 1 < depth:
            wts.append(dict(zip(BIG, _put_own_slabs(gathered, shards))) if shards is not None else later[l])
    dx, loss_blk = _loss_grad("loss", act, target)
    grads, d_mods = [None] * depth, [None] * depth
    pending = None
    for l in reversed(range(depth)):
        dx, d_mods[l], grads[l] = _layer_bwd(str(l), dx, mods[l], layers[l], wts[l], saved[l], reduce_later=pending)
        if core is not None:
            pending = _LayerReduce(str(l), l, depth, [grads[l][n] for n in BIG], core,
                                   into=pending.state if pending is not None else None)
    if core is None:
        return loss_blk, dx, d_mods, grads, None
    pending.swap_and_add()
    pending.exchange_and_sum()
    pending.share()
    return loss_blk, dx, d_mods, grads, dict(zip(BIG, pending.state))


def kernel(x, c, w_ada, b_ada, g_pre_mix, g_post_mix, g_pre_ffn, g_post_ffn, w_in, lam_re, lam_im, log_dt, b_re, b_im, c_re, c_im, d_skip, w_glu, b_glu, b_f, w_pa, w_pb, w_o, w_ffn_gate, w_ffn_up, w_ffn_down, loss_target, m_w_ada, m_b_ada, m_g_pre_mix, m_g_post_mix, m_g_pre_ffn, m_g_post_ffn, m_w_in, m_lam_re, m_lam_im, m_log_dt, m_b_re, m_b_im, m_c_re, m_c_im, m_d_skip, m_w_glu, m_b_glu, m_b_f, m_w_pa, m_w_pb, m_w_o, m_w_ffn_gate, m_w_ffn_up, m_w_ffn_down, v_w_ada, v_b_ada, v_g_pre_mix, v_g_post_mix, v_g_pre_ffn, v_g_post_ffn, v_w_in, v_lam_re, v_lam_im, v_log_dt, v_b_re, v_b_im, v_c_re, v_c_im, v_d_skip, v_w_glu, v_b_glu, v_b_f, v_w_pa, v_w_pb, v_w_o, v_w_ffn_gate, v_w_ffn_up, v_w_ffn_down):
    local = dict(locals())
    weights = {n: local[n] for n in WEIGHTS}
    moments_m = {n: local["m_" + n] for n in WEIGHTS}
    moments_v = {n: local["v_" + n] for n in WEIGHTS}
    depth, d = g_pre_mix.shape
    n_mod = w_ada.shape[2] * N_CHIPS // d
    mx, my, mc = lax.axis_index("x"), lax.axis_index("y"), lax.axis_index("c")
    my_chip = 2 * mx + my
    my_dev = 4 * mx + 2 * my + mc
    xs = x[0]

    shards = [[weights[n][l].astype(BF16) for n in BIG] for l in range(depth)]
    wts0 = dict(zip(BIG, _gather_layer("gather_weights_0", shards[0])))
    small = {n: weights[n] for n in SMALL}

    c_pad = jnp.pad(c, ((0, SUBLANES - 1), (0, 0)))
    c_all = _all_gather("gather_cond", c_pad).reshape(N_DEV, SUBLANES, d)[:, 0, :]
    silu = lambda v: v * _sigmoid(v)
    n_cols = w_ada.shape[2]
    mod_shard = []
    for l in range(depth):
        bias = lax.dynamic_slice_in_dim(b_ada[l], my_chip * n_cols, n_cols)
        mod_shard.append(_mm_plain(f"ada_{l}", c_all, w_ada[l], "nn", F32, add=jnp.broadcast_to(bias, (N_DEV, n_cols)),
                                   a_fn=silu, tm=N_DEV, tn=512, tk=1024))
    mod_block = jnp.concatenate(mod_shard, axis=1)
    mod_all = _all_gather("gather_mod", mod_block).reshape(N_DEV, N_DEV, depth, n_cols)
    mod_rows = lax.dynamic_index_in_dim(mod_all[0::2], my_dev, axis=1, keepdims=False)
    mods = [mod_rows[:, l, :].reshape(n_mod, d) for l in range(depth)]

    loss_blk, dx, d_mods, grads, big_grads = _fwd_bwd(xs, loss_target[0], mods, small, wts0, shards[1:],
                                                      core=mc.astype(jnp.int32).reshape(1))
    loss = lax.psum(loss_blk[0, 0], ("x", "y", "c"))
    grad_x = dx[None]

    partial_names = ("g_pre_mix", "g_post_mix", "g_pre_ffn", "g_post_ffn", "d_skip", "b_glu", "b_f", "a_bar",
                     "bbar_re", "bbar_im", "c_re", "c_im")
    n_state, group_ch = b_re.shape[2:]
    contrib = list(d_mods)
    for l in range(depth):
        compact = _compact_partials(grads[l], n_state, group_ch)
        contrib += [compact[n] for n in partial_names]
    contrib_shapes = [a.shape for a in contrib]
    block = _pack(contrib, LANES, BF16_ROWS, F32)
    rows = block.shape[0]
    all_blocks = _all_gather("gather_small_grads", block).reshape(N_DEV, rows, LANES)
    summed = _unpack(_sum_blocks("sum_small_grads", all_blocks, F32), contrib_shapes)
    per_layer = len(partial_names)
    small_grads = {n: [] for n in SMALL}
    d_mod_all = []
    for l in range(depth):
        small_grads["b_ada"].append(summed[l].reshape(-1))
        gl = dict(zip(partial_names, summed[depth + l * per_layer:depth + (l + 1) * per_layer]))
        for n in ("g_pre_mix", "g_post_mix", "g_pre_ffn", "g_post_ffn", "d_skip", "b_glu", "b_f"):
            small_grads[n].append(gl[n])
        for n, gval in _small_grads_from_partials(gl, small, l).items():
            small_grads[n].append(gval)
        mod_rows_ = n_mod * d // LANES
        d_mod_all.append(all_blocks[:, l * mod_rows_:(l + 1) * mod_rows_, :].reshape(N_DEV, n_mod * d))
    small_grads = {n: jnp.stack(v) for n, v in small_grads.items()}

    g_w_ada = []
    for l in range(depth):
        cols = lax.dynamic_slice_in_dim(d_mod_all[l], my_chip * n_cols, n_cols, axis=1)
        g_w_ada.append(_mm_plain(f"dw_ada_{l}", c_all, cols, "tn", F32, a_fn=silu, tm=512, tn=512, tk=N_DEV))
    all_grads = dict(big_grads)
    all_grads.update(small_grads)
    all_grads["w_ada"] = jnp.stack(g_w_ada)

    delta, new_m, new_v = {}, {}, {}
    for n in ("w_ada",) + BIG:
        delta[n], new_m[n], new_v[n] = _adamw(f"adamw_{n}", weights[n], all_grads[n], moments_m[n], moments_v[n])
    small_shapes = [weights[n].shape for n in SMALL]
    packed = [_pack([src[n] for n in SMALL], LANES, SUBLANES, F32)[None] for src in (weights, all_grads, moments_m, moments_v)]
    outs = _adamw("adamw_small", *packed)
    for dst, buf in zip((delta, new_m, new_v), outs):
        dst.update(dict(zip(SMALL, _unpack(buf[0], small_shapes))))

    return (loss, grad_x, *[all_grads[n] for n in WEIGHTS], *[delta[n] for n in WEIGHTS],
            *[new_m[n] for n in WEIGHTS], *[new_v[n] for n in WEIGHTS])
```

```python
import functools
import math

import jax
import jax.numpy as jnp
from jax import lax
from jax.experimental import pallas as pl
from jax.experimental.pallas import tpu as pltpu

F32 = jnp.float32
BF16 = jnp.bfloat16
MESH = pl.DeviceIdType.MESH

RMS_EPS = 1e-6
EIG_CLIP = 1e-4
ADAM_LR, ADAM_B1, ADAM_B2, ADAM_EPS, ADAM_WD, ADAM_STEP = 0.001, 0.9, 0.999, 1e-08, 0.01, 10

LANES = 128
SUBLANES = 8
VMEM_LIMIT = 56 * 1024 * 1024
S5_ROWS = 256
S5_CHUNK = 1024
S5_UNROLL = 4
ATT_BLOCK = 512
F_PAD = 256
POSTNORM_ROWS = 1024
N_CHIPS = 4
N_DEV = 8

NN = (((1,), (0,)), ((), ()))
NT = (((1,), (1,)), ((), ()))
TN = (((0,), (0,)), ((), ()))
_DN = {"nn": NN, "nt": NT, "tn": TN}


def _cparams(**kw):
    return pltpu.CompilerParams(vmem_limit_bytes=VMEM_LIMIT, **kw)


def _pick(dim, target):
    best, t = None, LANES
    while t <= min(dim, target):
        if dim % t == 0:
            best = t
        t += LANES
    return best or dim


def _sigmoid(x):
    return 1.0 / (1.0 + jnp.exp(-x))


def _dot(a, b, dn):
    return lax.dot_general(a, b, dn, preferred_element_type=F32)


def _mm_raw(name, a, b, mode, grid, acc_shape, a_spec, b_spec, out_shapes, out_specs, epilogue,
            extra=(), extra_specs=(), a_fn=None, side=None):
    nk = grid[2]
    n_extra, n_out = len(extra), len(out_shapes)

    def body(*refs):
        a_ref, b_ref = refs[0], refs[1]
        extra_refs = refs[2:2 + n_extra]
        out_refs = refs[2 + n_extra:2 + n_extra + n_out]
        acc = refs[-1]
        k = pl.program_id(2)

        @pl.when(k == 0)
        def _():
            acc[...] = jnp.zeros_like(acc)

        av = a_ref[...]
        if a_fn is not None:
            av = a_fn(av.astype(F32))
        acc[...] += _dot(av.astype(BF16), b_ref[...].astype(BF16), _DN[mode])

        @pl.when(k == nk - 1)
        def _():
            epilogue(acc[...], extra_refs, out_refs)

    outs, side_outs = _hosted_call(body, side, name, grid, [a_spec, b_spec, *extra_specs], list(out_specs),
                                   list(out_shapes), [pltpu.VMEM(acc_shape, F32)], (a, b, *extra))
    return outs if side is None else (outs, side_outs)


def _mm(name, a, b, mode, out_shapes, out_specs, epilogue, extra=(), extra_specs=(),
        tm=512, tn=512, tk=512, a_fn=None):
    if mode == "nn":
        (m, kd), (_, n) = a.shape, b.shape
    elif mode == "nt":
        (m, kd), (n, _) = a.shape, b.shape
    else:
        (kd, m), (_, n) = a.shape, b.shape
    tm, tn, tk = _pick(m, tm), _pick(n, tn), _pick(kd, tk)
    if mode == "tn":
        a_spec = pl.BlockSpec((tk, tm), lambda i, j, k: (k, i))
    else:
        a_spec = pl.BlockSpec((tm, tk), lambda i, j, k: (i, k))
    if mode == "nt":
        b_spec = pl.BlockSpec((tn, tk), lambda i, j, k: (j, k))
    else:
        b_spec = pl.BlockSpec((tk, tn), lambda i, j, k: (k, j))
    res = _mm_raw(name, a, b, mode, (m // tm, n // tn, kd // tk), (tm, tn), a_spec, b_spec, out_shapes, out_specs,
                  epilogue, extra=extra, extra_specs=extra_specs, a_fn=a_fn)
    return res, (tm, tn, tk)


def _store(dtype):
    def epilogue(acc, extra_refs, out_refs):
        out_refs[0][...] = acc.astype(dtype)
    return epilogue


def _mm_sum(name, m, n, tm, tn, pairs, out_dtype, side=None):
    offs, total = [], 0
    for pr in pairs:
        offs.append(total)
        total += pr[6]
    n_p = len(pairs)

    def body(*refs):
        o_ref, acc = refs[2 * n_p], refs[2 * n_p + 1]
        k = pl.program_id(2)

        @pl.when(k == 0)
        def _():
            acc[...] = jnp.zeros_like(acc)

        for p_ in range(n_p):
            @pl.when((k >= offs[p_]) & (k < offs[p_] + pairs[p_][6]))
            def _(p_=p_):
                acc[...] += _dot(refs[2 * p_][...].astype(BF16), refs[2 * p_ + 1][...].astype(BF16), NT)

        @pl.when(k == total - 1)
        def _():
            o_ref[...] = acc[...].astype(out_dtype)

    in_specs, operands = [], []
    for (a, a_block, a_index, b, b_block, b_index, steps), off in zip(pairs, offs):
        local = lambda k, off=off, steps=steps: jnp.clip(k - off, 0, steps - 1)
        in_specs.append(pl.BlockSpec(a_block, lambda i, j, k, f=a_index, local=local: f(i, local(k))))
        in_specs.append(pl.BlockSpec(b_block, lambda i, j, k, f=b_index, local=local: f(j, local(k))))
        operands += [a, b]
    (out,), side_outs = _hosted_call(
        body, side, name, (m // tm, n // tn, total), in_specs, [pl.BlockSpec((tm, tn), lambda i, j, k: (i, j))],
        [jax.ShapeDtypeStruct((m, n), out_dtype)], [pltpu.VMEM((tm, tn), F32)], operands)
    return out if side is None else (out, side_outs)


class _SideJob:
    def __init__(self, arrays, out_shapes, aliases, n_sems, copies):
        self.arrays, self.out_shapes, self.aliases, self.n_sems, self.copies = arrays, out_shapes, aliases, n_sems, copies


def _hosted_call(body, side, name, grid, in_specs, out_specs, out_shape, scratch_shapes, operands):
    if side is None:
        outs = pl.pallas_call(body, name=name, grid=grid, in_specs=in_specs, out_specs=out_specs, out_shape=out_shape,
                              scratch_shapes=scratch_shapes, compiler_params=_cparams())(*operands)
        return outs, []
    n_in, n_out, ns_in, ns_out = len(in_specs), len(out_specs), len(side.arrays), len(side.out_shapes)

    def wrapped(*refs):
        main_in, side_in = refs[:n_in], refs[n_in:n_in + ns_in]
        rest = refs[n_in + ns_in:]
        main_out, side_out, rest = rest[:n_out], rest[n_out:n_out + ns_out], rest[n_out + ns_out:]
        scratch, send_sems, recv_sems = rest[:-2], rest[-2], rest[-1]
        first, last = None, None
        for axis, extent in enumerate(grid):
            at_start, at_end = pl.program_id(axis) == 0, pl.program_id(axis) == extent - 1
            first = at_start if first is None else first & at_start
            last = at_end if last is None else last & at_end

        @pl.when(first)
        def _():
            for cp in side.copies(side_in, side_out, send_sems, recv_sems):
                cp.start()

        body(*main_in, *main_out, *scratch)

        @pl.when(last)
        def _():
            for cp in side.copies(side_in, side_out, send_sems, recv_sems):
                cp.wait()

    hbm = pl.BlockSpec(memory_space=pl.ANY)
    outs = pl.pallas_call(
        wrapped, name=name, grid=grid, in_specs=list(in_specs) + [hbm] * ns_in,
        out_specs=list(out_specs) + [hbm] * ns_out, out_shape=list(out_shape) + list(side.out_shapes),
        scratch_shapes=list(scratch_shapes) + [pltpu.SemaphoreType.DMA((side.n_sems,))] * 2,
        input_output_aliases={n_in + i: n_out + o for i, o in side.aliases.items()},
        compiler_params=_cparams(),
    )(*operands, *side.arrays)
    return outs[:n_out], outs[n_out:]


def _ffn_up(name, h, wg, wu, side=None):
    s, d = h.shape
    nc, fs = wg.shape[0], wg.shape[2]
    tm, tk = _pick(s, 1024), _pick(d, 1024)
    nk = d // tk

    def body(h_ref, wg_ref, wu_ref, a_ref, b_ref, hid_ref, acc_g, acc_u):
        k = pl.program_id(2)

        @pl.when(k == 0)
        def _():
            acc_g[...] = jnp.zeros_like(acc_g)
            acc_u[...] = jnp.zeros_like(acc_u)

        hv = h_ref[...]
        acc_g[...] += _dot(hv, wg_ref[...], NN)
        acc_u[...] += _dot(hv, wu_ref[...], NN)

        @pl.when(k == nk - 1)
        def _():
            av, bv = acc_g[...], acc_u[...]
            a_ref[...] = av.astype(BF16)
            b_ref[...] = bv.astype(BF16)
            hid_ref[...] = (av * _sigmoid(av) * bv).astype(BF16)

    w_spec = pl.BlockSpec((None, tk, fs), lambda i, j, k: (j, k, 0))
    o_spec = pl.BlockSpec((None, tm, fs), lambda i, j, k: (j, i, 0))
    sh = jax.ShapeDtypeStruct((nc, s, fs), BF16)
    return _hosted_call(
        body, side, name, (s // tm, nc, nk), [pl.BlockSpec((tm, tk), lambda i, j, k: (i, k)), w_spec, w_spec],
        [o_spec] * 3, [sh] * 3, [pltpu.VMEM((tm, fs), F32), pltpu.VMEM((tm, fs), F32)], (h, wg, wu))


def _mm_plain(name, a, b, mode, out_dtype, add=None, a_fn=None, tm=512, tn=512, tk=512):
    if mode == "nn":
        m, n = a.shape[0], b.shape[1]
    elif mode == "nt":
        m, n = a.shape[0], b.shape[0]
    else:
        m, n = a.shape[1], b.shape[1]
    tm_, tn_ = _pick(m, tm), _pick(n, tn)
    spec = pl.BlockSpec((tm_, tn_), lambda i, j, k: (i, j))

    def epilogue(acc, extra_refs, out_refs):
        if add is not None:
            acc = acc + extra_refs[0][...]
        out_refs[0][...] = acc.astype(out_dtype)

    extra = () if add is None else (add,)
    (out,), _ = _mm(name, a, b, mode, [jax.ShapeDtypeStruct((m, n), out_dtype)], [spec], epilogue,
                    extra=extra, extra_specs=[spec] * len(extra), tm=tm, tn=tn, tk=tk, a_fn=a_fn)
    return out


def _row_tile(s, d):
    return _pick(s, max(SUBLANES, (1 << 20) // (4 * d)))


def _prenorm_fwd(name, x, g, scale, shift):
    s, d = x.shape
    tr = _row_tile(s, d)

    def body(x_ref, g_ref, sc_ref, sh_ref, h_ref):
        xv = x_ref[...]
        r = lax.rsqrt(jnp.mean(xv * xv, axis=-1, keepdims=True) + RMS_EPS)
        h_ref[...] = ((xv * r * g_ref[...]) * (1.0 + sc_ref[...]) + sh_ref[...]).astype(BF16)

    row = pl.BlockSpec((tr, d), lambda i: (i, 0))
    vec = pl.BlockSpec((1, d), lambda i: (0, 0))
    return pl.pallas_call(body, name=name, grid=(s // tr,), in_specs=[row, vec, vec, vec], out_specs=row,
                          out_shape=jax.ShapeDtypeStruct((s, d), BF16), compiler_params=_cparams())(x, g, scale, shift)


def _prenorm_bwd(name, dh, x, g, scale, dx_res):
    s, d = x.shape
    tr = _row_tile(s, d)

    def body(dh_ref, x_ref, g_ref, sc_ref, dxr_ref, dx_ref, sums_ref):
        @pl.when(pl.program_id(0) == 0)
        def _():
            sums_ref[...] = jnp.zeros_like(sums_ref)

        xv, dhv, gv = x_ref[...], dh_ref[...].astype(F32), g_ref[...]
        r = lax.rsqrt(jnp.mean(xv * xv, axis=-1, keepdims=True) + RMS_EPS)
        xhat = xv * r
        dxn = dhv * (1.0 + sc_ref[...])
        dxhat = dxn * gv
        dx = r * (dxhat - xhat * jnp.mean(dxhat * xhat, axis=-1, keepdims=True))
        dx_ref[...] = dxr_ref[...] + dx
        sums_ref[0:1, :] += jnp.sum(dhv * (xhat * gv), axis=0, keepdims=True)
        sums_ref[1:2, :] += jnp.sum(dhv, axis=0, keepdims=True)
        sums_ref[2:3, :] += jnp.sum(dxn * xhat, axis=0, keepdims=True)

    row = pl.BlockSpec((tr, d), lambda i: (i, 0))
    vec = pl.BlockSpec((1, d), lambda i: (0, 0))
    acc = pl.BlockSpec((SUBLANES, d), lambda i: (0, 0))
    return pl.pallas_call(
        body, name=name, grid=(s // tr,), in_specs=[row, row, vec, vec, row], out_specs=[row, acc],
        out_shape=[jax.ShapeDtypeStruct((s, d), F32), jax.ShapeDtypeStruct((SUBLANES, d), F32)],
        compiler_params=_cparams())(dh, x, g, scale, dx_res)


def _postnorm_bwd(name, dxn, y, g, gate):
    s, d = y.shape
    tr = _row_tile(s, d)

    def body(dx_ref, y_ref, g_ref, gt_ref, dy_ref, sums_ref):
        @pl.when(pl.program_id(0) == 0)
        def _():
            sums_ref[...] = jnp.zeros_like(sums_ref)

        yv, dxv, gv = y_ref[...], dx_ref[...], g_ref[...]
        r = lax.rsqrt(jnp.mean(yv * yv, axis=-1, keepdims=True) + RMS_EPS)
        yhat = yv * r
        dn = dxv * gt_ref[...]
        dyhat = dn * gv
        dy_ref[...] = (r * (dyhat - yhat * jnp.mean(dyhat * yhat, axis=-1, keepdims=True))).astype(BF16)
        sums_ref[0:1, :] += jnp.sum(dxv * (yhat * gv), axis=0, keepdims=True)
        sums_ref[1:2, :] += jnp.sum(dn * yhat, axis=0, keepdims=True)

    row = pl.BlockSpec((tr, d), lambda i: (i, 0))
    vec = pl.BlockSpec((1, d), lambda i: (0, 0))
    acc = pl.BlockSpec((SUBLANES, d), lambda i: (0, 0))
    return pl.pallas_call(
        body, name=name, grid=(s // tr,), in_specs=[row, row, vec, vec], out_specs=[row, acc],
        out_shape=[jax.ShapeDtypeStruct((s, d), BF16), jax.ShapeDtypeStruct((SUBLANES, d), F32)],
        compiler_params=_cparams())(dxn, y, g, gate)


def _loss_grad(name, y, target):
    s, d = y.shape
    tr = _row_tile(s, d)

    def body(y_ref, t_ref, dy_ref, loss_ref):
        @pl.when(pl.program_id(0) == 0)
        def _():
            loss_ref[...] = jnp.zeros_like(loss_ref)

        err = y_ref[...] - t_ref[...]
        dy_ref[...] = err * (1.0 / d)
        part = jnp.sum(jnp.sum(err * err, axis=-1, keepdims=True), axis=0, keepdims=True) * (0.5 / d)
        loss_ref[...] += jnp.broadcast_to(part, loss_ref.shape)

    row = pl.BlockSpec((tr, d), lambda i: (i, 0))
    acc = pl.BlockSpec((SUBLANES, LANES), lambda i: (0, 0))
    return pl.pallas_call(
        body, name=name, grid=(s // tr,), in_specs=[row, row], out_specs=[row, acc],
        out_shape=[jax.ShapeDtypeStruct((s, d), F32), jax.ShapeDtypeStruct((SUBLANES, LANES), F32)],
        compiler_params=_cparams())(y, target)


def _gelu(y):
    c = math.sqrt(2.0 / math.pi)
    return 0.5 * y * (1.0 + jnp.tanh(c * (y + 0.044715 * (y * y * y))))


def _gelu_grad(y):
    c = math.sqrt(2.0 / math.pi)
    th = jnp.tanh(c * (y + 0.044715 * (y * y * y)))
    return 0.5 * (1.0 + th) + 0.5 * y * (1.0 - th * th) * c * (1.0 + 3.0 * 0.044715 * (y * y))


def _cmul_add(br, bi, ar, ai, xr, xi):
    return br + ar * xr - ai * xi, bi + ar * xi + ai * xr


def _scan_rows(x_ref, row0, n_steps, ns2, pow_ref, tab_ref, carry_ref, reverse, fold=None):
    assert n_steps % SUBLANES == 0
    wc = min(S5_CHUNK, ns2)
    sub = lax.broadcasted_iota(jnp.int32, (SUBLANES, wc), 0)
    unroll = S5_UNROLL if n_steps % S5_UNROLL == 0 else 1
    for c0 in range(0, ns2, wc):
        re = slice(c0, c0 + wc)
        im = slice(ns2 + c0, ns2 + c0 + wc)
        first_power = slice(n_steps - 1, n_steps) if reverse else slice(0, 1)
        ar = jnp.broadcast_to(pow_ref[first_power, re], (SUBLANES, wc))
        ai = jnp.broadcast_to(pow_ref[first_power, im], (SUBLANES, wc))
        rows = lambda r: pl.ds(pl.multiple_of(row0 + r * SUBLANES, SUBLANES), SUBLANES)
        step_of = lambda i: (n_steps - 1 - i) if reverse else i

        def local(i, carry, re=re, im=im, ar=ar, ai=ai):
            for u in range(unroll):
                r = step_of(i * unroll + u)
                carry = _cmul_add(x_ref[rows(r), re], x_ref[rows(r), im], ar, ai, *carry)
                x_ref[rows(r), re], x_ref[rows(r), im] = carry
            return carry

        zero = jnp.zeros((SUBLANES, wc), F32)
        lr, li = lax.fori_loop(0, n_steps // unroll, local, (zero, zero))

        tabs = [tab_ref[k, :, re] for k in range(8)]
        for lvl, k in enumerate((1, 2, 4)):
            sh = (SUBLANES - k) if reverse else k
            lr, li = _cmul_add(lr, li, tabs[2 * lvl], tabs[2 * lvl + 1], pltpu.roll(lr, sh, 0), pltpu.roll(li, sh, 0))
        cr, ci = carry_ref[0:1, re], carry_ref[0:1, im]
        lr, li = _cmul_add(lr, li, tabs[6], tabs[7], cr, ci)
        edge, away, last = (SUBLANES - 1, SUBLANES - 1, 0) if reverse else (0, 1, SUBLANES - 1)
        carry_ref[0:1, re] = lr[last:last + 1, :]
        carry_ref[0:1, im] = li[last:last + 1, :]
        er = jnp.where(sub == edge, cr, pltpu.roll(lr, away, 0))
        ei = jnp.where(sub == edge, ci, pltpu.roll(li, away, 0))

        def fix(j, acc, re=re, im=im, er=er, ei=ei, c0=c0):
            base = pl.ds(pl.multiple_of(j * SUBLANES, SUBLANES), SUBLANES)
            pw_r, pw_i = pow_ref[base, re], pow_ref[base, im]
            for i in range(SUBLANES):
                r = j * SUBLANES + i
                xr, xi = _cmul_add(x_ref[rows(r), re], x_ref[rows(r), im], pw_r[i:i + 1, :], pw_i[i:i + 1, :], er, ei)
                x_ref[rows(r), re], x_ref[rows(r), im] = xr, xi
                if fold is not None:
                    acc = fold(c0, r, xr, xi, acc)
            return acc

        acc = lax.fori_loop(0, n_steps // SUBLANES, fix, (zero, zero) if fold is not None else 0)
        if fold is not None:
            fold(c0, None, None, None, acc)


def _s5_fwd(name, u, b_blk, c_blk, a_f, tab_f, dskip, w_glu, b_glu):
    s, w = u.shape[0], w_glu.shape[0]
    nkb = w // LANES
    ns2 = b_blk.shape[2] // 2 * nkb
    half = ns2 // nkb
    t = min(S5_ROWS, s)
    nblk = s // t

    def body(u_ref, b_ref, c_ref, a_ref, tab_ref, ds_ref, wg_ref, bg_ref, y_ref, ys_ref, cs_ref, xs, carry):
        @pl.when(pl.program_id(0) == 0)
        def _():
            carry[...] = jnp.zeros_like(carry)

        cs_ref[0] = carry[...]
        for kb in range(nkb):
            bu = _dot(u_ref[:, kb * LANES:(kb + 1) * LANES], b_ref[kb], NN)
            xs[:, kb * half:(kb + 1) * half] = bu[:, :half]
            xs[:, ns2 + kb * half:ns2 + (kb + 1) * half] = bu[:, half:]
        _scan_rows(xs, 0, t // SUBLANES, ns2, a_ref, tab_ref, carry, reverse=False)
        for kb in range(nkb):
            cols = slice(kb * LANES, (kb + 1) * LANES)
            yk = _dot(xs[:, kb * half:(kb + 1) * half].astype(BF16), c_ref[kb, :half, :], NN)
            yk += _dot(xs[:, ns2 + kb * half:ns2 + (kb + 1) * half].astype(BF16), c_ref[kb, half:, :], NN)
            y_ref[:, cols] = yk + ds_ref[:, cols] * u_ref[:, cols].astype(F32)
        z = _gelu(y_ref[...])
        gate = _sigmoid(_dot(z.astype(BF16), wg_ref[...], NN) + bg_ref[...])
        ys_ref[...] = (z * gate).astype(BF16)

    row = pl.BlockSpec((t, w), lambda i: (i, 0))
    full = lambda shape: pl.BlockSpec(shape, lambda i: (0,) * len(shape))
    return pl.pallas_call(
        body, name=name, grid=(nblk,),
        in_specs=[row, full(b_blk.shape), full(c_blk.shape), full(a_f.shape), full(tab_f.shape), full(dskip.shape),
                  full(w_glu.shape), full(b_glu.shape)],
        out_specs=[row, row, pl.BlockSpec((1, 1, 2 * ns2), lambda i: (i, 0, 0))],
        out_shape=[jax.ShapeDtypeStruct((s, w), F32), jax.ShapeDtypeStruct((s, w), BF16),
                   jax.ShapeDtypeStruct((nblk, 1, 2 * ns2), F32)],
        scratch_shapes=[pltpu.VMEM((t, 2 * ns2), F32), pltpu.VMEM((1, 2 * ns2), F32)],
        compiler_params=_cparams(),
    )(u, b_blk, c_blk, a_f, tab_f, dskip, w_glu, b_glu)


def _s5_bwd(name, u, dys, y, carries, b_blk, c_blk, a_f, a_r, tab_f, tab_r, dskip, w_glu, b_glu):
    s, w = u.shape[0], w_glu.shape[0]
    nkb = w // LANES
    ns2 = b_blk.shape[2] // 2 * nkb
    half = ns2 // nkb
    t = min(S5_ROWS, s)
    nblk = s // t
    ng = t // SUBLANES

    def body(u_ref, dys_ref, y_ref, cs_ref, b_ref, c_ref, af_ref, ar_ref, tabf_ref, tabr_ref, ds_ref, wg_ref, bg_ref,
             du_ref, db_ref, dc_ref, da_ref, dwg_ref, vec_ref, xs, gs, dyv, fcarry, gcarry):
        @pl.when(pl.program_id(0) == 0)
        def _():
            db_ref[...] = jnp.zeros_like(db_ref)
            dc_ref[...] = jnp.zeros_like(dc_ref)
            da_ref[...] = jnp.zeros_like(da_ref)
            dwg_ref[...] = jnp.zeros_like(dwg_ref)
            vec_ref[...] = jnp.zeros_like(vec_ref)
            gcarry[...] = jnp.zeros_like(gcarry)

        yv = y_ref[...]
        z = _gelu(yv)
        zb = z.astype(BF16)
        gate = _sigmoid(_dot(zb, wg_ref[...], NN) + bg_ref[...])
        dout = dys_ref[...].astype(F32)
        dt = dout * z * gate * (1.0 - gate)
        dtb = dt.astype(BF16)
        dz = dout * gate + _dot(dtb, wg_ref[...], NT)
        dy = dz * _gelu_grad(yv)
        dyv[...] = dy
        dwg_ref[...] += _dot(zb, dtb, TN)
        vec_ref[0:1, :] += jnp.sum(dt, axis=0, keepdims=True)
        vec_ref[1:2, :] += jnp.sum(dy * u_ref[...].astype(F32), axis=0, keepdims=True)

        fcarry[...] = cs_ref[0]
        xs[0:SUBLANES, :] = jnp.broadcast_to(cs_ref[0], (SUBLANES, 2 * ns2))
        for kb in range(nkb):
            bu = _dot(u_ref[:, kb * LANES:(kb + 1) * LANES], b_ref[kb], NN)
            xs[SUBLANES:, kb * half:(kb + 1) * half] = bu[:, :half]
            xs[SUBLANES:, ns2 + kb * half:ns2 + (kb + 1) * half] = bu[:, half:]
        _scan_rows(xs, SUBLANES, ng, ns2, af_ref, tabf_ref, fcarry, reverse=False)
        first_segment = lax.broadcasted_iota(jnp.int32, (SUBLANES, 2 * ns2), 0) == 0
        xs[0:SUBLANES, :] = jnp.where(first_segment, xs[0:SUBLANES, :], pltpu.roll(xs[t:t + SUBLANES, :], 1, 0))

        for kb in range(nkb):
            dyk = dyv[:, kb * LANES:(kb + 1) * LANES].astype(BF16)
            re = slice(kb * half, (kb + 1) * half)
            im = slice(ns2 + kb * half, ns2 + (kb + 1) * half)
            gs[:, re] = _dot(dyk, c_ref[kb, :half, :], NT)
            gs[:, im] = _dot(dyk, c_ref[kb, half:, :], NT)
            dc_ref[kb, :half, :] += _dot(xs[SUBLANES:, re].astype(BF16), dyk, TN)
            dc_ref[kb, half:, :] += _dot(xs[SUBLANES:, im].astype(BF16), dyk, TN)

        def fold(c0, r, gr, gi, acc):
            wc = min(S5_CHUNK, ns2)
            re = slice(c0, c0 + wc)
            im = slice(ns2 + c0, ns2 + c0 + wc)
            if r is None:
                da_ref[:, re] += acc[0]
                da_ref[:, im] += acc[1]
                return acc
            before = pl.ds(pl.multiple_of(r * SUBLANES, SUBLANES), SUBLANES)
            xpr, xpi = xs[before, re], xs[before, im]
            return acc[0] + gr * xpr + gi * xpi, acc[1] - gr * xpi + gi * xpr

        _scan_rows(gs, 0, ng, ns2, ar_ref, tabr_ref, gcarry, reverse=True, fold=fold)

        for kb in range(nkb):
            cols = slice(kb * LANES, (kb + 1) * LANES)
            re = slice(kb * half, (kb + 1) * half)
            im = slice(ns2 + kb * half, ns2 + (kb + 1) * half)
            uk = u_ref[:, cols]
            gr = gs[:, re].astype(BF16)
            gi = gs[:, im].astype(BF16)
            db_ref[kb, :, :half] += _dot(uk, gr, TN)
            db_ref[kb, :, half:] += _dot(uk, gi, TN)
            duk = _dot(gr, b_ref[kb, :, :half], NT) + _dot(gi, b_ref[kb, :, half:], NT)
            du_ref[:, cols] = (duk + ds_ref[:, cols] * dyv[:, cols]).astype(BF16)

    rev = lambda i: (nblk - 1 - i, 0)
    row = pl.BlockSpec((t, w), rev)
    full = lambda shape: pl.BlockSpec(shape, lambda i: (0,) * len(shape))
    return pl.pallas_call(
        body, name=name, grid=(nblk,),
        in_specs=[row, row, row, pl.BlockSpec((1, 1, 2 * ns2), lambda i: (nblk - 1 - i, 0, 0)),
                  full(b_blk.shape), full(c_blk.shape), full(a_f.shape), full(a_r.shape), full(tab_f.shape),
                  full(tab_r.shape), full(dskip.shape), full(w_glu.shape), full(b_glu.shape)],
        out_specs=[row, full(b_blk.shape), full(c_blk.shape), full((SUBLANES, 2 * ns2)), full((w, w)),
                   full((SUBLANES, w))],
        out_shape=[jax.ShapeDtypeStruct((s, w), BF16), jax.ShapeDtypeStruct(b_blk.shape, F32),
                   jax.ShapeDtypeStruct(c_blk.shape, F32), jax.ShapeDtypeStruct((SUBLANES, 2 * ns2), F32),
                   jax.ShapeDtypeStruct((w, w), F32), jax.ShapeDtypeStruct((SUBLANES, w), F32)],
        scratch_shapes=[pltpu.VMEM((t + SUBLANES, 2 * ns2), F32), pltpu.VMEM((t, 2 * ns2), F32),
                        pltpu.VMEM((t, w), F32), pltpu.VMEM((1, 2 * ns2), F32), pltpu.VMEM((1, 2 * ns2), F32)],
        compiler_params=_cparams(),
    )(u, dys, y, carries, b_blk, c_blk, a_f, a_r, tab_f, tab_r, dskip, w_glu, b_glu)


def _log_sigmoid(x):
    return jnp.minimum(x, 0.0) - jnp.log(1.0 + jnp.exp(-jnp.abs(x)))


def _cum_fwd(name, f_t, b_f):
    h, s = f_t.shape
    tc = _pick(s, 512)
    nb = s // tc

    def body(f_ref, b_ref, c_ref, carry):
        @pl.when(pl.program_id(0) == 0)
        def _():
            carry[...] = jnp.zeros_like(carry)

        lf = _log_sigmoid(f_ref[...] + b_ref[...])
        upper = (lax.broadcasted_iota(jnp.int32, (tc, tc), 0) <= lax.broadcasted_iota(jnp.int32, (tc, tc), 1))
        cum = lax.dot_general(lf, upper.astype(F32), NN, precision=lax.Precision.HIGHEST,
                              preferred_element_type=F32) + carry[...]
        c_ref[...] = cum
        carry[...] += jnp.sum(lf, axis=1, keepdims=True)

    blk = pl.BlockSpec((h, tc), lambda i: (0, i))
    return pl.pallas_call(body, name=name, grid=(nb,), in_specs=[blk, pl.BlockSpec((h, 1), lambda i: (0, 0))],
                          out_specs=blk, out_shape=jax.ShapeDtypeStruct((h, s), F32),
                          scratch_shapes=[pltpu.VMEM((h, 1), F32)], compiler_params=_cparams())(f_t, b_f)


def _cum_bwd(name, dcq, dck, f_t, b_f):
    h, s = f_t.shape
    tc = _pick(s, 512)
    nb = s // tc

    def body(dcq_ref, dck_ref, f_ref, b_ref, df_ref, db_ref, carry):
        @pl.when(pl.program_id(0) == 0)
        def _():
            carry[...] = jnp.zeros_like(carry)
            db_ref[...] = jnp.zeros_like(db_ref)

        dc = dcq_ref[...] + dck_ref[...]
        lower = (lax.broadcasted_iota(jnp.int32, (tc, tc), 0) >= lax.broadcasted_iota(jnp.int32, (tc, tc), 1))
        dlf = lax.dot_general(dc, lower.astype(F32), NN, precision=lax.Precision.HIGHEST,
                              preferred_element_type=F32) + carry[...]
        carry[...] += jnp.sum(dc, axis=1, keepdims=True)
        df = dlf * _sigmoid(-(f_ref[...] + b_ref[...]))
        df_ref[...] = df
        db_ref[...] += jnp.broadcast_to(jnp.sum(df, axis=1, keepdims=True), db_ref.shape)

    blk = pl.BlockSpec((h, tc), lambda i: (0, nb - 1 - i))
    return pl.pallas_call(
        body, name=name, grid=(nb,), in_specs=[blk, blk, blk, pl.BlockSpec((h, 1), lambda i: (0, 0))],
        out_specs=[blk, pl.BlockSpec((h, LANES), lambda i: (0, 0))],
        out_shape=[jax.ShapeDtypeStruct((h, s), F32), jax.ShapeDtypeStruct((h, LANES), F32)],
        scratch_shapes=[pltpu.VMEM((h, 1), F32)], compiler_params=_cparams())(dcq, dck, f_t, b_f)


def _attn_fwd(name, qkv, q_blk, k_blk, v_blk, n_pairs, ck, side=None):
    s = qkv.shape[0]
    dh = LANES // 2
    t = min(ATT_BLOCK, s)
    nq = s // t
    scale = dh ** -0.5

    def body(q_ref, k_ref, v_ref, ck_ref, o_ref, lse_ref, m_s, acc_s):
        i = pl.program_id(1)
        low = lax.broadcasted_iota(jnp.int32, (1, LANES), 1) < dh
        qs = (q_ref[...].astype(F32) * scale).astype(BF16)
        zero = jnp.zeros_like(qs)
        qh = (jnp.where(low, qs, zero), jnp.where(low, zero, qs))
        m_s[...] = jnp.full(m_s.shape, -1e30, F32)
        acc_s[...] = jnp.zeros_like(acc_s)
        causal = (lax.broadcasted_iota(jnp.int32, (t, t), 1) <= lax.broadcasted_iota(jnp.int32, (t, t), 0))

        def step(j, diagonal):
            r0 = pl.multiple_of(j * t, t)
            kj = k_ref[pl.ds(r0, t), :]
            vj = v_ref[pl.ds(r0, t), :]
            one = jnp.ones_like(vj)
            vh = (jnp.where(low, vj, one), jnp.where(low, one, vj))
            for hd in range(2):
                sc = _dot(qh[hd], kj, NT) - ck_ref[hd, j]
                if diagonal:
                    sc = jnp.where(causal, sc, -1e30)
                m_old = m_s[hd]
                m_new = jnp.maximum(m_old, jnp.max(sc, axis=1, keepdims=True))
                p = jnp.exp(sc - m_new)
                acc_s[hd] = jnp.exp(m_old - m_new) * acc_s[hd] + _dot(p.astype(BF16), vh[hd], NN)
                m_s[hd] = m_new

        def full(j, _):
            step(j, False)
            return 0

        lax.fori_loop(0, i, full, 0)
        step(i, True)
        a0, a1 = acc_s[0], acc_s[1]
        o_ref[...] = jnp.where(low, a0 / pltpu.roll(a0, dh, 1), a1 / pltpu.roll(a1, dh, 1)).astype(BF16)
        lse_ref[0] = m_s[0] + jnp.log(a0[:, dh:dh + 1])
        lse_ref[1] = m_s[1] + jnp.log(a1[:, 0:1])

    return _hosted_call(
        body, side, name, (n_pairs, nq),
        [pl.BlockSpec((t, LANES), lambda hp, i: (i, q_blk + hp)),
         pl.BlockSpec((s, LANES), lambda hp, i: (0, k_blk + hp)),
         pl.BlockSpec((s, LANES), lambda hp, i: (0, v_blk + hp)),
         pl.BlockSpec((2, nq, 1, t), lambda hp, i: (hp, 0, 0, 0))],
        [pl.BlockSpec((t, LANES), lambda hp, i: (i, hp)), pl.BlockSpec((2, t, 1), lambda hp, i: (hp, i, 0))],
        [jax.ShapeDtypeStruct((s, LANES * n_pairs), BF16), jax.ShapeDtypeStruct((2 * n_pairs, s, 1), F32)],
        [pltpu.VMEM((2, t, 1), F32), pltpu.VMEM((2, t, LANES), F32)], (qkv, qkv, qkv, ck))


def _attn_bwd(name, qkv, q_blk, k_blk, v_blk, n_pairs, o, do, lse_rows, ck_cols, side=None):
    s = qkv.shape[0]
    dh = LANES // 2
    t = min(ATT_BLOCK, s)
    nk = s // t
    scale = dh ** -0.5

    def body(q_ref, k_ref, v_ref, o_ref, do_ref, lse_ref, ck_ref,
             dq_ref, dk_ref, dv_ref, dcq_ref, dck_ref, delta, dqt, dk_acc, dv_acc):
        j = pl.program_id(1)
        low = lax.broadcasted_iota(jnp.int32, (1, LANES), 1) < dh
        low_rows = lax.broadcasted_iota(jnp.int32, (LANES, 1), 0) < dh

        @pl.when(j == 0)
        def _():
            dqt[...] = jnp.zeros_like(dqt)
            sel = (jnp.broadcast_to(low, (SUBLANES, LANES)).astype(F32), jnp.broadcast_to(~low, (SUBLANES, LANES)).astype(F32))

            def fill(i, _):
                r0 = pl.multiple_of(i * t, t)
                prod = do_ref[pl.ds(r0, t), :].astype(F32) * o_ref[pl.ds(r0, t), :].astype(F32)
                for hd in range(2):
                    delta[hd, i] = lax.dot_general(sel[hd], prod, NT, precision=lax.Precision.HIGHEST,
                                                   preferred_element_type=F32)
                return 0

            lax.fori_loop(0, nk, fill, 0)

        kj, vj = k_ref[...], v_ref[...]
        zero, one = jnp.zeros_like(kj), jnp.ones_like(kj)
        kh = (jnp.where(low, kj, zero), jnp.where(low, zero, kj))
        vh = (jnp.where(low, vj, zero), jnp.where(low, zero, vj))
        kjt = kj.astype(F32).T.astype(BF16)
        one_t = jnp.ones_like(kjt)
        kht = (jnp.where(low_rows, kjt, one_t), jnp.where(low_rows, one_t, kjt))
        dk_acc[...] = jnp.zeros_like(dk_acc)
        dv_acc[...] = jnp.zeros_like(dv_acc)
        causal_t = (lax.broadcasted_iota(jnp.int32, (t, t), 0) <= lax.broadcasted_iota(jnp.int32, (t, t), 1))

        def step(i, diagonal):
            r0 = pl.multiple_of(i * t, t)
            qi = (q_ref[pl.ds(r0, t), :].astype(F32) * scale).astype(BF16)
            doi = do_ref[pl.ds(r0, t), :]
            qone, dzero = jnp.ones_like(qi), jnp.zeros_like(doi)
            qsel = (jnp.where(low, qi, qone), jnp.where(low, qone, qi))
            dosel = (jnp.where(low, doi, dzero), jnp.where(low, dzero, doi))
            for hd in range(2):
                st = _dot(kh[hd], qi, NT) - ck_ref[hd] - lse_ref[hd, i]
                pt = jnp.exp(st)
                if diagonal:
                    pt = jnp.where(causal_t, pt, 0.0)
                dst = pt * (_dot(vh[hd], doi, NT) - delta[hd, i, 0:1, :])
                dsb = dst.astype(BF16)
                dv_acc[...] += _dot(pt.astype(BF16), dosel[hd], NN)
                dk_acc[hd] += _dot(dsb, qsel[hd], NN)
                dqt[hd, i] += _dot(kht[hd], dsb, NN)

        step(j, True)

        def rest(i, _):
            step(i, False)
            return 0

        lax.fori_loop(j + 1, nk, rest, 0)
        dk_ref[...] = jnp.where(low, dk_acc[0], dk_acc[1]).astype(BF16)
        dv_ref[...] = dv_acc[...].astype(BF16)
        dck_ref[0] = -dk_acc[0][:, dh:dh + 1]
        dck_ref[1] = -dk_acc[1][:, 0:1]

        @pl.when(j == nk - 1)
        def _():
            def emit(i, _):
                r0 = pl.multiple_of(i * t, t)
                d0, d1 = dqt[0, i], dqt[1, i]
                dq_ref[pl.ds(r0, t), :] = (jnp.where(low_rows, d0, d1) * scale).T.astype(BF16)
                dcq_ref[0, i] = d0[dh:dh + 1, :]
                dcq_ref[1, i] = d1[0:1, :]
                return 0

            lax.fori_loop(0, nk, emit, 0)

    col_blk = lambda base: pl.BlockSpec((t, LANES), lambda hp, j: (j, base + hp))
    col_all = lambda base: pl.BlockSpec((s, LANES), lambda hp, j: (0, base + hp))
    rows_all = pl.BlockSpec((2, nk, 1, t), lambda hp, j: (hp, 0, 0, 0))
    return _hosted_call(
        body, side, name, (n_pairs, nk),
        [col_all(q_blk), col_blk(k_blk), col_blk(v_blk), col_all(0), col_all(0), rows_all,
         pl.BlockSpec((2, t, 1), lambda hp, j: (hp, j, 0))],
        [col_all(0), col_blk(0), col_blk(0), rows_all, pl.BlockSpec((2, t, 1), lambda hp, j: (hp, j, 0))],
        [jax.ShapeDtypeStruct((s, LANES * n_pairs), BF16), jax.ShapeDtypeStruct((s, LANES * n_pairs), BF16),
         jax.ShapeDtypeStruct((s, LANES * n_pairs), BF16), jax.ShapeDtypeStruct((2 * n_pairs, nk, 1, t), F32),
         jax.ShapeDtypeStruct((2 * n_pairs, s, 1), F32)],
        [pltpu.VMEM((2, nk, SUBLANES, t), F32), pltpu.VMEM((2, nk, LANES, t), F32),
         pltpu.VMEM((2, t, LANES), F32), pltpu.VMEM((t, LANES), F32)],
        (qkv, qkv, qkv, o, do, lse_rows, ck_cols))


def _adamw(name, w, g, m, v):
    n_l, r, c = w.shape
    by_rows = r % SUBLANES == 0
    tr = _pick8(r, max(SUBLANES, (1 << 20) // (4 * c))) if by_rows else r
    tl = 1 if by_rows else max(t for t in range(1, n_l + 1) if n_l % t == 0 and t * r * c * 4 <= (1 << 20))

    def body(w_ref, g_ref, m_ref, v_ref, d_ref, mo_ref, vo_ref):
        gv = g_ref[...]
        m2 = ADAM_B1 * m_ref[...] + (1.0 - ADAM_B1) * gv
        v2 = ADAM_B2 * v_ref[...] + (1.0 - ADAM_B2) * (gv * gv)
        m_hat = m2 / (1.0 - ADAM_B1 ** ADAM_STEP)
        v_hat = v2 / (1.0 - ADAM_B2 ** ADAM_STEP)
        d_ref[...] = -ADAM_LR * (m_hat / (jnp.sqrt(v_hat) + ADAM_EPS) + ADAM_WD * w_ref[...])
        mo_ref[...] = m2
        vo_ref[...] = v2

    blk = pl.BlockSpec((None, tr, c) if by_rows else (tl, r, c), lambda l, i: (l, i, 0))
    sh = jax.ShapeDtypeStruct((n_l, r, c), F32)
    return pl.pallas_call(body, name=name, grid=(n_l // tl, r // tr), in_specs=[blk] * 4,
                          out_specs=[blk] * 3, out_shape=[sh, sh, sh], compiler_params=_cparams())(w, g, m, v)


def _pick8(dim, target, mult=SUBLANES):
    best, t = None, mult
    while t <= min(dim, target):
        if dim % t == 0:
            best = t
        t += mult
    return best or dim


BF16_ROWS = 16


def _sum_blocks(name, x, out_dtype):
    n, r, c = x.shape
    tr = _pick8(r, max(BF16_ROWS, (1 << 19) // (4 * c)), BF16_ROWS)

    def body(x_ref, o_ref):
        acc = x_ref[0].astype(F32)
        for i in range(1, n):
            acc = acc + x_ref[i].astype(F32)
        o_ref[...] = acc.astype(out_dtype)

    return pl.pallas_call(body, name=name, grid=(r // tr,),
                          in_specs=[pl.BlockSpec((n, tr, c), lambda i: (0, i, 0))],
                          out_specs=pl.BlockSpec((tr, c), lambda i: (i, 0)),
                          out_shape=jax.ShapeDtypeStruct((r, c), out_dtype), compiler_params=_cparams())(x)


def _all_gather(name, x_shard):
    m_per, n = x_shard.shape

    def body(x_ref, out_ref, send_sems, recv_sems):
        x, y, c = lax.axis_index("x"), lax.axis_index("y"), lax.axis_index("c")
        me, sibling = (x, y, c), (x, y, 1 - c)
        chips = [(1 - x, y), (x, 1 - y), (1 - x, 1 - y)]

        def rows(px, py, pc):
            return out_ref.at[pl.ds((4 * px + 2 * py + pc) * m_per, m_per), :]

        def copy(k, block, to, src=None):
            return pltpu.make_async_remote_copy(
                src_ref=rows(*block) if src is None else src, dst_ref=rows(*block),
                send_sem=send_sems.at[k], recv_sem=recv_sems.at[k], device_id=to, device_id_type=MESH)

        first = [copy(0, me, sibling, src=x_ref)]
        first += [copy(1 + j, me, (*chip, c), src=x_ref) for j, chip in enumerate(chips)]
        for cp in first:
            cp.start()
        passed = [copy(4 + j, (*chip, c), sibling) for j, chip in enumerate(chips)]
        for j, chip in enumerate(chips):
            copy(1 + j, (*chip, c), me).wait_recv()
            passed[j].start()
        copy(0, sibling, me).wait_recv()
        for j, chip in enumerate(chips):
            copy(4 + j, (*chip, 1 - c), me).wait_recv()
        for cp in first + passed:
            cp.wait_send()

    out = pl.pallas_call(
        body, name=name, out_shape=jax.ShapeDtypeStruct((N_DEV * m_per, n), x_shard.dtype),
        in_specs=[pl.BlockSpec(memory_space=pl.ANY)], out_specs=pl.BlockSpec(memory_space=pl.ANY),
        scratch_shapes=[pltpu.SemaphoreType.DMA((7,)), pltpu.SemaphoreType.DMA((7,))],
    )(x_shard)
    my_dev = 4 * lax.axis_index("x") + 2 * lax.axis_index("y") + lax.axis_index("c")
    return lax.dynamic_update_slice(out, x_shard, (my_dev * m_per, 0))


def _put_own(out, own, index):
    start = tuple(index) + (0,) * own.ndim
    return lax.dynamic_update_slice(out, own.reshape((1,) * len(index) + own.shape), start)


def _gather_copies(stage, ins, outs, send_sems, recv_sems):
    x, y, c = lax.axis_index("x"), lax.axis_index("y"), lax.axis_index("c")
    my_chip = 2 * x + y
    copies = []
    for w, out in enumerate(outs):
        half = out.shape[1] // 2
        rows = pl.ds(c * half, half)
        for k, (cx, cy) in enumerate([(1 - x, y), (x, 1 - y), (1 - x, 1 - y)]):
            if stage == 0:
                src, dst, to = ins[w].at[rows], out.at[my_chip, rows], (cx, cy, c)
            else:
                src = dst = out.at[2 * cx + cy, rows]
                to = (x, y, 1 - c)
            copies.append(pltpu.make_async_remote_copy(
                src_ref=src, dst_ref=dst, send_sem=send_sems.at[3 * w + k], recv_sem=recv_sems.at[3 * w + k],
                device_id=to, device_id_type=MESH))
    return copies


def _gathered_shapes(shards):
    return [jax.ShapeDtypeStruct((N_CHIPS,) + s.shape, s.dtype) for s in shards]


def _put_own_slabs(gathered, shards):
    my_chip = 2 * lax.axis_index("x") + lax.axis_index("y")
    return [_put_own(o, s, (my_chip,)) for o, s in zip(gathered, shards)]


def _gather_layer(name, shards):
    n_w = len(shards)

    def body(*refs):
        ins, outs = refs[:n_w], refs[n_w:2 * n_w]
        for stage in (0, 1):
            copies = _gather_copies(stage, ins, outs, refs[2 * n_w + 2 * stage], refs[2 * n_w + 2 * stage + 1])
            for cp in copies:
                cp.start()
            for cp in copies:
                cp.wait()

    outs = pl.pallas_call(
        body, name=name, out_shape=_gathered_shapes(shards),
        in_specs=[pl.BlockSpec(memory_space=pl.ANY)] * n_w, out_specs=[pl.BlockSpec(memory_space=pl.ANY)] * n_w,
        scratch_shapes=[pltpu.SemaphoreType.DMA((3 * n_w,))] * 4,
    )(*shards)
    return _put_own_slabs(outs, shards)


def _gather_side_jobs(shards):
    n_w = len(shards)
    between_chips = _SideJob(list(shards), _gathered_shapes(shards), {}, 3 * n_w,
                             lambda ins, outs, send, recv: _gather_copies(0, ins, outs, send, recv))
    between_cores = lambda partial: _SideJob(list(partial), _gathered_shapes(shards), {w: w for w in range(n_w)}, 3 * n_w,
                                             lambda ins, outs, send, recv: _gather_copies(1, ins, outs, send, recv))
    return between_chips, between_cores


def _run_job(name, job):
    n_in, n_out = len(job.arrays), len(job.out_shapes)

    def body(*refs):
        copies = job.copies(refs[:n_in], refs[n_in:n_in + n_out], refs[n_in + n_out], refs[n_in + n_out + 1])
        for cp in copies:
            cp.start()
        for cp in copies:
            cp.wait()

    hbm = pl.BlockSpec(memory_space=pl.ANY)
    return pl.pallas_call(
        body, name=name, out_shape=list(job.out_shapes), in_specs=[hbm] * n_in, out_specs=[hbm] * n_out,
        scratch_shapes=[pltpu.SemaphoreType.DMA((job.n_sems,))] * 2, input_output_aliases=dict(job.aliases),
    )(*job.arrays)


def _swap_job(grads):
    def copies(ins, outs, send_sems, recv_sems):
        x, y, c = lax.axis_index("x"), lax.axis_index("y"), lax.axis_index("c")
        return [pltpu.make_async_remote_copy(
            src_ref=g.at[:, pl.ds((1 - c) * (g.shape[1] // 2), g.shape[1] // 2)], dst_ref=outs[w],
            send_sem=send_sems.at[w], recv_sem=recv_sems.at[w], device_id=(x, y, 1 - c), device_id_type=MESH)
            for w, g in enumerate(ins)]

    shapes = [jax.ShapeDtypeStruct((g.shape[0], g.shape[1] // 2, g.shape[2]), g.dtype) for g in grads]
    return _SideJob(list(grads), shapes, {}, len(grads), copies)


def _exchange_job(parts):
    def copies(ins, outs, send_sems, recv_sems):
        x, y, c = lax.axis_index("x"), lax.axis_index("y"), lax.axis_index("c")
        return [pltpu.make_async_remote_copy(
            src_ref=ins[w].at[2 * cx + cy], dst_ref=outs[w].at[2 * x + y], send_sem=send_sems.at[3 * w + k],
            recv_sem=recv_sems.at[3 * w + k], device_id=(cx, cy, c), device_id_type=MESH)
            for w in range(len(ins)) for k, (cx, cy) in enumerate([(1 - x, y), (x, 1 - y), (1 - x, 1 - y)])]

    return _SideJob(list(parts), [jax.ShapeDtypeStruct(p.shape, p.dtype) for p in parts], {}, 3 * len(parts), copies)


def _share_job(reduced, layer, depth, into):
    n_w = len(reduced)

    def copies(ins, outs, send_sems, recv_sems):
        x, y, c = lax.axis_index("x"), lax.axis_index("y"), lax.axis_index("c")
        return [pltpu.make_async_remote_copy(
            src_ref=ins[w], dst_ref=outs[w].at[layer, pl.ds(c * ins[w].shape[0], ins[w].shape[0])],
            send_sem=send_sems.at[w], recv_sem=recv_sems.at[w], device_id=(x, y, 1 - c), device_id_type=MESH)
            for w in range(n_w)]

    shapes = [jax.ShapeDtypeStruct((depth, 2 * r.shape[0], r.shape[1]), r.dtype) for r in reduced]
    if into is None:
        return _SideJob(list(reduced), shapes, {}, n_w, copies)
    return _SideJob(list(reduced) + list(into), shapes, {n_w + w: w for w in range(n_w)}, n_w, copies)


def _add_rows(name, grads, recv, core):
    n, r, c = recv.shape
    tr = _pick8(r, max(BF16_ROWS, (1 << 19) // (4 * c)), BF16_ROWS)
    steps = r // tr

    def body(core_ref, g_ref, r_ref, o_ref):
        o_ref[...] = (g_ref[...].astype(F32) + r_ref[...].astype(F32)).astype(BF16)

    grid_spec = pltpu.PrefetchScalarGridSpec(
        num_scalar_prefetch=1, grid=(steps,),
        in_specs=[pl.BlockSpec((n, tr, c), lambda i, core_ref: (0, core_ref[0] * steps + i, 0)),
                  pl.BlockSpec((n, tr, c), lambda i, core_ref: (0, i, 0))],
        out_specs=pl.BlockSpec((n, tr, c), lambda i, core_ref: (0, i, 0)))
    return pl.pallas_call(body, name=name, grid_spec=grid_spec,
                          out_shape=jax.ShapeDtypeStruct((n, r, c), BF16), compiler_params=_cparams())(core, grads, recv)


class _LayerReduce:
    def __init__(self, tag, layer, depth, grads, core, into):
        self.tag, self.layer, self.depth, self.core, self.into = tag, layer, depth, core, into
        self.state = list(grads)

    def _exchange(self, name, job, carry):
        if carry is None:
            return None, _run_job(f"{name}_{self.tag}", job)
        return carry(job)

    def swap_and_add(self, carry=None):
        grads = self.state
        results, recv = self._exchange("grads_swap_cores", _swap_job(grads), carry)
        self.state = [_add_rows(f"grads_add_{n}_{self.tag}", g, r, self.core) for n, g, r in zip(BIG, grads, recv)]
        return results

    def exchange_and_sum(self, carry=None):
        parts = self.state
        results, arrived = self._exchange("grads_exchange_chips", _exchange_job(parts), carry)
        my_chip = 2 * lax.axis_index("x") + lax.axis_index("y")
        arrived = [_put_own(a, lax.dynamic_index_in_dim(p, my_chip, 0, keepdims=False), (my_chip,))
                   for a, p in zip(arrived, parts)]
        self.state = [_sum_blocks(f"grads_sum_{n}_{self.tag}", a, F32) for n, a in zip(BIG, arrived)]
        return results

    def share(self, carry=None):
        reduced = self.state
        results, outs = self._exchange("grads_share_cores", _share_job(reduced, self.layer, self.depth, self.into), carry)
        self.state =[lax.dynamic_update_slice(o, r[None], (self.layer, lax.axis_index("c") * r.shape[0], 0))
                      for o, r in zip(outs, reduced)]
        return results


def _pack(arrays, cols, row_multiple, dtype):
    flat = jnp.concatenate([a.reshape(-1).astype(dtype) for a in arrays])
    unit = cols * row_multiple
    total = -(-flat.shape[0] // unit) * unit
    return jnp.pad(flat, (0, total - flat.shape[0])).reshape(total // cols, cols)


def _unpack(buf, shapes):
    flat, out, off = buf.reshape(-1), [], 0
    for sh in shapes:
        n = math.prod(sh)
        out.append(flat[off:off + n].reshape(sh))
        off += n
    return out


def _discretize(lam_re, lam_im, log_dt, b_re, b_im):
    lam = lax.complex(jnp.minimum(lam_re, -EIG_CLIP), lam_im)
    dt = jnp.exp(log_dt)[:, None]
    lam_bar = jnp.exp(lam * dt)
    b_bar = ((lam_bar - 1.0) / lam)[..., None] * lax.complex(b_re, b_im)
    return jnp.real(lam_bar), jnp.imag(lam_bar), jnp.real(b_bar), jnp.imag(b_bar)


def _scan_tables(ar, ai):
    a = lax.complex(ar, ai)
    pw = [a]
    for _ in range(7):
        pw.append(pw[-1] * a)
    rows = jnp.arange(SUBLANES)[:, None]

    def build(p, reverse):
        tabs = []
        for k in (1, 2, 4):
            keep = (rows <= SUBLANES - 1 - k) if reverse else (rows >= k)
            tk = jnp.where(keep, p[k - 1][None, :], 0.0)
            tabs += [jnp.real(tk), jnp.imag(tk)]
        stack = jnp.stack(p[::-1] if reverse else p)
        tabs += [jnp.real(stack), jnp.imag(stack)]
        return jnp.stack(tabs).astype(F32)

    return build(pw, False), build([jnp.conj(p) for p in pw], True)


def _interleave_rows(a, t):
    s, w = a.shape
    return a.reshape(s // t, SUBLANES, t // SUBLANES, w).transpose(0, 2, 1, 3).reshape(s, w)


def _deinterleave_rows(a, t):
    s, w = a.shape
    return a.reshape(s // t, t // SUBLANES, SUBLANES, w).transpose(0, 2, 1, 3).reshape(s, w)


def _block_diag(per_group, groups_per_block):
    g, a, b = per_group.shape
    x = per_group.reshape(g // groups_per_block, groups_per_block, a, b)
    eye = jnp.eye(groups_per_block, dtype=per_group.dtype)
    out = x[:, :, :, None, :] * eye[None, :, None, :, None]
    return out.reshape(g // groups_per_block, groups_per_block * a, groups_per_block * b)


def _block_diag_extract(dense, groups_per_block, a, b):
    nkb = dense.shape[0]
    x = dense.reshape(nkb, groups_per_block, a, groups_per_block, b)
    idx = jnp.arange(groups_per_block)
    return x[:, idx, :, idx, :].transpose(1, 0, 2, 3).reshape(nkb * groups_per_block, a, b)


def _layer_fwd(tag, x, mod, p, wts, gather_next=None):
    s, d = x.shape
    w_ssm, w_att = p["w_glu"].shape[0], wts["w_pb"].shape[1]
    heads = p["b_f"].shape[0]
    dh = w_att // heads
    cs = d // N_CHIPS
    fs = wts["w_ffn_down"].shape[1]
    tm = _pick(s, 1024)
    row = lambda v: v.reshape(1, -1)
    sv = {}

    h = _prenorm_fwd(f"prenorm_mix_{tag}", x, row(p["g_pre_mix"]), row(mod[1]), row(mod[0]))
    uqkv = _mm_plain(f"proj_main_{tag}", h, p["w_main"], "nn", BF16, tm=1024, tn=1024, tk=1024)
    fg = _mm_plain(f"proj_gate_{tag}", h, p["w_gates"], "nn", F32, tm=1024, tn=1024, tk=1024)
    f_t = fg[:, 2 * d:2 * d + heads].T

    t5 = min(S5_ROWS, s)
    u_il = _interleave_rows(uqkv[:, :w_ssm], t5)
    y_s5, ys_il, carries = _s5_fwd(f"s5_fwd_{tag}", u_il, p["b_blk"], p["c_blk"], p["a_f"], p["tab_f"],
                                   row(p["d_skip"]), p["w_glu"], row(p["b_glu"]))
    ys = _deinterleave_rows(ys_il, t5)

    assert dh * 2 == LANES and w_ssm % LANES == 0 and w_att % LANES == 0
    n_pairs = w_att // LANES
    blocks = (w_ssm // LANES, w_ssm // LANES + n_pairs, w_ssm // LANES + 2 * n_pairs)
    cum = _cum_fwd(f"cum_fwd_{tag}", f_t, p["b_f"].reshape(heads, 1))
    t = min(ATT_BLOCK, s)
    ck_cols, ck_rows = cum.reshape(heads, s, 1), cum.reshape(heads, s // t, 1, t)
    (ya, lse), arrived = _attn_fwd(f"attn_fwd_{tag}", uqkv, *blocks, n_pairs, ck_rows,
                                   side=gather_next[0] if gather_next else None)

    tile = pl.BlockSpec((tm, cs), lambda i, j, k: (i, j))
    slab = lambda rows: pl.BlockSpec((None, rows, cs), lambda i, j, k: (j, 0, 0))

    def merge(acc, extra_refs, out_refs):
        ya_ref, wpb_ref, ga_ref, gb_ref = extra_refs
        a_ref, b_ref, m_ref = out_refs
        bv = _dot(ya_ref[...], wpb_ref[...], NN)
        a_ref[...] = acc.astype(BF16)
        b_ref[...] = bv.astype(BF16)
        m_ref[...] = (_sigmoid(ga_ref[...]) * acc + _sigmoid(gb_ref[...]) * bv).astype(BF16)

    sd_bf = jax.ShapeDtypeStruct((s, d), BF16)
    pa, pb, merged = _mm_raw(
        f"merge_{tag}", ys, wts["w_pa"], "nn", (s // tm, N_CHIPS, 1), (tm, cs),
        pl.BlockSpec((tm, w_ssm), lambda i, j, k: (i, 0)), slab(w_ssm), [sd_bf] * 3, [tile] * 3, merge,
        extra=(ya, wts["w_pb"], fg, fg),
        extra_specs=[pl.BlockSpec((tm, w_att), lambda i, j, k: (i, 0)), slab(w_att), tile,
                     pl.BlockSpec((tm, cs), lambda i, j, k: (i, j + N_CHIPS))])

    tm2 = _pick(s, POSTNORM_ROWS)
    x1, y_mix = _mm_postnorm(
        f"out_proj_{tag}", merged, pl.BlockSpec((tm2, cs), lambda i, j, k: (i, k)), wts["w_o"],
        pl.BlockSpec((None, cs, d), lambda i, j, k: (k, 0, 0)), N_CHIPS, x, row(mod[2]), row(p["g_post_mix"]))

    h2 = _prenorm_fwd(f"prenorm_ffn_{tag}", x1, row(p["g_pre_ffn"]), row(mod[4]), row(mod[3]))
    (a4, b4, hid4), next_wts = _ffn_up(f"ffn_up_{tag}", h2, wts["w_ffn_gate"], wts["w_ffn_up"],
                                       side=gather_next[1](arrived) if gather_next else None)
    x2, y_ffn = _mm_postnorm(
        f"ffn_down_{tag}", hid4, pl.BlockSpec((None, tm2, fs), lambda i, j, k: (k, i, 0)), wts["w_ffn_down"],
        pl.BlockSpec((None, fs, d), lambda i, j, k: (k, 0, 0)), N_CHIPS, x1, row(mod[5]), row(p["g_post_ffn"]))

    sv.update(x=x, h=h, uqkv=uqkv, u_il=u_il, fg=fg, f_t=f_t, y_s5=y_s5, ys=ys, carries=carries, blocks=blocks,
              ck_cols=ck_cols, lse_rows=lse.reshape(heads, s // t, 1, t), ya=ya, pa=pa, pb=pb, merged=merged, x1=x1,
              y_mix=y_mix, h2=h2, a4=a4, b4=b4, hid4=hid4, y_ffn=y_ffn)
    return x2, sv, next_wts


def _mm_postnorm(name, a, a_spec, w, w_spec, nk, x, gate, g):
    s, d = x.shape
    tm = _pick(s, POSTNORM_ROWS)
    rowspec = pl.BlockSpec((tm, d), lambda i, j, k: (i, 0))
    vec = pl.BlockSpec((1, d), lambda i, j, k: (0, 0))

    def epilogue(acc, extra_refs, out_refs):
        x_ref, gate_ref, g_ref = extra_refs
        r = lax.rsqrt(jnp.mean(acc * acc, axis=-1, keepdims=True) + RMS_EPS)
        out_refs[0][...] = x_ref[...] + gate_ref[...] * (acc * r * g_ref[...])
        out_refs[1][...] = acc

    sd = jax.ShapeDtypeStruct((s, d), F32)
    return _mm_raw(name, a, w, "nn", (s // tm, 1, nk), (tm, d), a_spec, w_spec, [sd, sd], [rowspec, rowspec], epilogue,
                   extra=(x, gate, g), extra_specs=[rowspec, vec, vec])


def _layer_bwd(tag, dx2, mod, p, wts, sv, reduce_later=None):
    s, d = dx2.shape
    w_ssm, w_att = p["w_glu"].shape[0], wts["w_pb"].shape[1]
    heads = p["b_f"].shape[0]
    cs = d // N_CHIPS
    fs = wts["w_ffn_down"].shape[1]
    tm, tk, td = _pick(s, 1024), _pick(s, 1024), d
    row = lambda v: v.reshape(1, -1)
    gr = {}

    def dw_slabs(name, act, act_spec, rows, dy, dy_spec, cols, grid_mn, out_index):
        return _mm_raw(name, act, dy, "tn", grid_mn + (s // tk,), (rows, cols), act_spec, dy_spec,
                       [jax.ShapeDtypeStruct((N_CHIPS,) + out_index[1], BF16)],
                       [pl.BlockSpec((None, rows, cols), out_index[0])], _store(BF16))[0]

    dy_ffn, sums = _postnorm_bwd(f"postnorm_bwd_ffn_{tag}", dx2, sv["y_ffn"], row(p["g_post_ffn"]), row(mod[5]))
    d_gate_f, gr["g_post_ffn"] = sums[0], sums[1]
    gr["w_ffn_down"] = dw_slabs(f"dw_down_{tag}", sv["hid4"], pl.BlockSpec((None, tk, fs), lambda i, j, k: (i, k, 0)), fs,
                                dy_ffn, pl.BlockSpec((tk, d), lambda i, j, k: (k, 0)), d, (N_CHIPS, 1),
                                (lambda i, j, k: (i, 0, 0), (fs, d)))

    def swiglu_bwd(acc, extra_refs, out_refs):
        av, bv = extra_refs[0][...].astype(F32), extra_refs[1][...].astype(F32)
        sg = _sigmoid(av)
        out_refs[0][...] = (acc * bv * (sg * (1.0 + av * (1.0 - sg)))).astype(BF16)
        out_refs[1][...] = (acc * (av * sg)).astype(BF16)

    blk4 = pl.BlockSpec((None, tm, fs), lambda i, j, k: (j, i, 0))
    sh4 = jax.ShapeDtypeStruct((N_CHIPS, s, fs), BF16)
    ffn_down_bwd = lambda side: _mm_raw(
        f"ffn_down_bwd_{tag}", dy_ffn, wts["w_ffn_down"], "nt", (s // tm, N_CHIPS, 1), (tm, fs),
        pl.BlockSpec((tm, d), lambda i, j, k: (i, 0)), pl.BlockSpec((None, fs, d), lambda i, j, k: (j, 0, 0)),
        [sh4, sh4], [blk4, blk4], swiglu_bwd, extra=(sv["a4"], sv["b4"]), extra_specs=[blk4, blk4], side=side)
    da4, db4 = reduce_later.swap_and_add(ffn_down_bwd) if reduce_later else ffn_down_bwd(None)
    for n, act4 in (("w_ffn_gate", da4), ("w_ffn_up", db4)):
        gr[n] = dw_slabs(f"d{n}_{tag}", sv["h2"], pl.BlockSpec((tk, td), lambda i, j, k: (k, i)), td,
                         act4, pl.BlockSpec((None, tk, fs), lambda i, j, k: (j, k, 0)), fs, (d // td, N_CHIPS),
                         (lambda i, j, k: (j, i, 0), (d, fs)))
    pairs = [(act4, (None, tm, fs), lambda i, kk: (kk, i, 0), wts[n], (None, td, fs), lambda j, kk: (kk, j, 0),
              N_CHIPS) for n, act4 in (("w_ffn_gate", da4), ("w_ffn_up", db4))]
    dh2 = _mm_sum(f"dh_ffn_{tag}", s, d, tm, td, pairs, F32)
    dx1, sums = _prenorm_bwd(f"prenorm_bwd_ffn_{tag}", dh2, sv["x1"], row(p["g_pre_ffn"]), row(mod[4]), dx2)
    d_scale_f, d_shift_f, gr["g_pre_ffn"] = sums[0], sums[1], sums[2]

    dy_mix, sums = _postnorm_bwd(f"postnorm_bwd_mix_{tag}", dx1, sv["y_mix"], row(p["g_post_mix"]), row(mod[2]))
    d_gate_m, gr["g_post_mix"] = sums[0], sums[1]
    gr["w_o"] = dw_slabs(f"dw_o_{tag}", sv["merged"], pl.BlockSpec((tk, cs), lambda i, j, k: (k, i)), cs,
                         dy_mix, pl.BlockSpec((tk, d), lambda i, j, k: (k, 0)), d, (N_CHIPS, 1),
                         (lambda i, j, k: (i, 0, 0), (cs, d)))

    tile = pl.BlockSpec((tm, cs), lambda i, j, k: (i, j))

    def merge_bwd(acc, extra_refs, out_refs):
        a_ref, b_ref, ga_ref, gb_ref = extra_refs
        sa, sb = _sigmoid(ga_ref[...]), _sigmoid(gb_ref[...])
        out_refs[0][...] = (acc * sa).astype(BF16)
        out_refs[1][...] = (acc * sb).astype(BF16)
        out_refs[2][...] = (acc * a_ref[...].astype(F32) * sa * (1.0 - sa)).astype(BF16)
        out_refs[3][...] = (acc * b_ref[...].astype(F32) * sb * (1.0 - sb)).astype(BF16)

    sd_bf = jax.ShapeDtypeStruct((s, d), BF16)
    d_pa, d_pb, d_ga, d_gb = _mm_raw(
        f"out_proj_bwd_{tag}", dy_mix, wts["w_o"], "nt", (s // tm, N_CHIPS, 1), (tm, cs),
        pl.BlockSpec((tm, d), lambda i, j, k: (i, 0)), pl.BlockSpec((None, cs, d), lambda i, j, k: (j, 0, 0)),
        [sd_bf] * 4, [tile] * 4, merge_bwd, extra=(sv["pa"], sv["pb"], sv["fg"], sv["fg"]),
        extra_specs=[tile, tile, tile, pl.BlockSpec((tm, cs), lambda i, j, k: (i, j + N_CHIPS))])
    d_branch = {}
    for n, act, width, d_p in (("w_pa", sv["ys"], w_ssm, d_pa), ("w_pb", sv["ya"], w_att, d_pb)):
        gr[n] = dw_slabs(f"d{n}_{tag}", act, pl.BlockSpec((tk, width), lambda i, j, k: (k, 0)), width,
                         d_p, pl.BlockSpec((tk, cs), lambda i, j, k: (k, j)), cs, (1, N_CHIPS),
                         (lambda i, j, k: (j, 0, 0), (width, cs)))
        d_branch[n] = _mm_raw(
            f"d_in_{n}_{tag}", d_p, wts[n], "nt", (s // tm, 1, N_CHIPS), (tm, width),
            pl.BlockSpec((tm, cs), lambda i, j, k: (i, k)), pl.BlockSpec((None, width, cs), lambda i, j, k: (k, 0, 0)),
            [jax.ShapeDtypeStruct((s, width), BF16)], [pl.BlockSpec((tm, width), lambda i, j, k: (i, 0))], _store(BF16))[0]
    d_ys, d_ya = d_branch["w_pa"], d_branch["w_pb"]

    attn_bwd = lambda side: _attn_bwd(f"attn_bwd_{tag}", sv["uqkv"], *sv["blocks"], w_att // LANES, sv["ya"], d_ya,
                                      sv["lse_rows"], sv["ck_cols"], side=side)
    dq, dk, dv, dcq, dck = reduce_later.exchange_and_sum(attn_bwd) if reduce_later else attn_bwd(None)[0]
    d_f_t, d_bf = _cum_bwd(f"cum_bwd_{tag}", dcq.reshape(heads, s), dck.reshape(heads, s), sv["f_t"],
                           p["b_f"].reshape(heads, 1))
    gr["b_f"] = d_bf[:, 0]

    t5 = min(S5_ROWS, s)
    du_il, d_bblk, d_cblk, d_abar, d_wglu, vec = _s5_bwd(
        f"s5_bwd_{tag}", sv["u_il"], _interleave_rows(d_ys, t5), sv["y_s5"], sv["carries"], p["b_blk"], p["c_blk"],
        p["a_f"], p["a_r"], p["tab_f"], p["tab_r"], row(p["d_skip"]), p["w_glu"], row(p["b_glu"]))
    du = _deinterleave_rows(du_il, t5)
    gr["w_glu"] = d_wglu.astype(BF16).reshape(N_CHIPS, w_ssm // N_CHIPS, w_ssm)
    gr["b_glu"], gr["d_skip"] = vec[0], vec[1]
    gr["b_blk"], gr["c_blk"], gr["a_bar"] = d_bblk, d_cblk, d_abar

    d_f = jnp.pad(d_f_t.T, ((0, 0), (0, F_PAD - heads))).astype(BF16)
    assert w_ssm % w_att == 0 and (2 * d) % F_PAD == 0
    first = w_ssm // w_att
    main_pieces = [(du, w_ssm, 0), (dq, w_att, first), (dk, w_att, first + 1), (dv, w_att, first + 2)]
    dw = [_mm_plain(f"dw_in{n}_{tag}", sv["h"], piece, "tn", BF16, tm=1024, tn=1024, tk=1024)
          for n, piece in enumerate([du, dq, dk, dv, d_f, d_ga, d_gb])]
    w_in_grad = jnp.concatenate(dw[:4] + [dw[4][:, :heads], dw[5], dw[6]], axis=1)
    gr["w_in"] = w_in_grad.reshape(d, N_CHIPS, w_in_grad.shape[1] // N_CHIPS).transpose(1, 0, 2)
    tmx, tkx = _pick(s, 1024), _pick(d, 512)
    pairs = [(piece, (tmx, width), lambda i, kk: (i, 0), p["w_main"], (d, width), lambda j, kk, blk=blk: (j, blk), 1)
             for piece, width, blk in main_pieces]
    steps = d // tkx
    pairs += [(piece, (tmx, tkx), lambda i, kk: (i, kk), p["w_gates"], (d, tkx), lambda j, kk, off=off: (j, off + kk), steps)
              for piece, off in ((d_ga, 0), (d_gb, steps))]
    pairs.append((d_f, (tmx, F_PAD), lambda i, kk: (i, 0), p["w_gates"], (d, F_PAD), lambda j, kk: (j, 2 * d // F_PAD), 1))
    dh_mix = lambda side: _mm_sum(f"dh_mix_{tag}", s, d, tmx, d, pairs, F32, side=side)
    dh1 = reduce_later.share(dh_mix) if reduce_later else dh_mix(None)
    dx0, sums = _prenorm_bwd(f"prenorm_bwd_mix_{tag}", dh1, sv["x"], row(p["g_pre_mix"]), row(mod[1]), dx1)
    d_scale_m, d_shift_m, gr["g_pre_mix"] = sums[0], sums[1], sums[2]

    d_mod = jnp.stack([d_shift_m, d_scale_m, d_gate_m, d_shift_f, d_scale_f, d_gate_f])
    return dx0, d_mod, gr


BIG = ("w_in", "w_glu", "w_pa", "w_pb", "w_o", "w_ffn_gate", "w_ffn_up", "w_ffn_down")
SMALL = ("b_ada", "g_pre_mix", "g_post_mix", "g_pre_ffn", "g_post_ffn", "lam_re", "lam_im", "log_dt", "b_re", "b_im",
         "c_re", "c_im", "d_skip", "b_glu", "b_f")
WEIGHTS = ("w_ada", "b_ada", "g_pre_mix", "g_post_mix", "g_pre_ffn", "g_post_ffn", "w_in", "lam_re", "lam_im", "log_dt",
           "b_re", "b_im", "c_re", "c_im", "d_skip", "w_glu", "b_glu", "b_f", "w_pa", "w_pb", "w_o", "w_ffn_gate",
           "w_ffn_up", "w_ffn_down")


def _prepare_layer(wts, small, l, seq):
    w_in = jnp.concatenate([wts["w_in"][j] for j in range(N_CHIPS)], axis=1)
    d = w_in.shape[0]
    heads = small["b_f"].shape[1]
    n_groups, n_state, group_ch = small["b_re"].shape[1:]
    w_ssm = n_groups * group_ch
    w_att = wts["w_pb"].shape[1]
    n_main = w_ssm + 3 * w_att
    gpb = LANES // group_ch
    p = {}
    p["w_main"] = w_in[:, :n_main]
    p["w_gates"] = jnp.concatenate(
        [w_in[:, n_main + heads:], w_in[:, n_main:n_main + heads], jnp.zeros((d, F_PAD - heads), BF16)], axis=1)
    p["w_glu"] = wts["w_glu"].reshape(w_ssm, w_ssm)
    for n in ("g_pre_mix", "g_post_mix", "g_pre_ffn", "g_post_ffn", "d_skip", "b_glu", "b_f"):
        p[n] = small[n][l]
    ar, ai, br, bi = _discretize(small["lam_re"][l], small["lam_im"][l], small["log_dt"][l], small["b_re"][l], small["b_im"][l])
    n_steps = min(S5_ROWS, seq) // SUBLANES
    powers = jnp.cumprod(jnp.broadcast_to(lax.complex(ar, ai).reshape(1, -1), (n_steps, ar.size)), axis=0)
    p["a_f"] = jnp.concatenate([jnp.real(powers), jnp.imag(powers)], axis=1)
    p["a_r"] = jnp.concatenate([jnp.real(powers[::-1]), -jnp.imag(powers[::-1])], axis=1)
    p["tab_f"], p["tab_r"] = _scan_tables(jnp.real(powers[-1]), jnp.imag(powers[-1]))
    bre = _block_diag(br.transpose(0, 2, 1), gpb)
    bim = _block_diag(bi.transpose(0, 2, 1), gpb)
    p["b_blk"] = jnp.concatenate([bre, bim], axis=2).astype(BF16)
    cre = _block_diag(small["c_re"][l].transpose(0, 2, 1), gpb)
    cim = _block_diag(small["c_im"][l].transpose(0, 2, 1), gpb)
    p["c_blk"] = jnp.concatenate([cre, -cim], axis=1).astype(BF16)
    return p


def _compact_partials(gr, n_state, group_ch):
    gpb = LANES // group_ch
    half = gpb * n_state
    out = dict(gr)
    out["bbar_re"] = _block_diag_extract(gr["b_blk"][:, :, :half], gpb, group_ch, n_state).transpose(0, 2, 1)
    out["bbar_im"] = _block_diag_extract(gr["b_blk"][:, :, half:], gpb, group_ch, n_state).transpose(0, 2, 1)
    out["c_re"] = _block_diag_extract(gr["c_blk"][:, :half, :], gpb, n_state, group_ch).transpose(0, 2, 1)
    out["c_im"] = -_block_diag_extract(gr["c_blk"][:, half:, :], gpb, n_state, group_ch).transpose(0, 2, 1)
    return out


def _small_grads_from_partials(gr, small, l):
    n_groups, n_state, _ = small["b_re"].shape[1:]
    ns2 = n_groups * n_state
    d_abar = jnp.sum(gr["a_bar"], axis=0)
    dar, dai = d_abar[:ns2].reshape(n_groups, n_state), d_abar[ns2:].reshape(n_groups, n_state)
    args = (small["lam_re"][l], small["lam_im"][l], small["log_dt"][l], small["b_re"][l], small["b_im"][l])
    _, vjp = jax.vjp(_discretize, *args)
    d_lam_re, d_lam_im, d_log_dt, d_b_re, d_b_im = vjp((dar, dai, gr["bbar_re"], gr["bbar_im"]))
    return dict(lam_re=d_lam_re, lam_im=d_lam_im, log_dt=d_log_dt, b_re=d_b_re, b_im=d_b_im,
                c_re=gr["c_re"], c_im=gr["c_im"])


def _fwd_bwd(xs, target, mods, small, wts0, later, core=None):
    depth = 1 + len(later)
    saved, layers, wts = [], [], [wts0]
    act = xs
    for l in range(depth):
        layers.append(_prepare_layer(wts[l], small, l, xs.shape[0]))
        shards = later[l] if l + 1 < depth and not isinstance(later[l], dict) else None
        act, sv, gathered = _layer_fwd(str(l), act, mods[l], layers[l], wts[l],
                                       gather_next=_gather_side_jobs(shards) if shards is not None else None)
        saved.append(sv)
        if l + 1 < depth:
            wts.append(dict(zip(BIG, _put_own_slabs(gathered, shards))) if shards is not None else later[l])
    dx, loss_blk = _loss_grad("loss", act, target)
    grads, d_mods = [None] * depth, [None] * depth
    pending = None
    for l in reversed(range(depth)):
        dx, d_mods[l], grads[l] = _layer_bwd(str(l), dx, mods[l], layers[l], wts[l], saved[l], reduce_later=pending)
        if core is not None:
            pending = _LayerReduce(str(l), l, depth, [grads[l][n] for n in BIG], core,
                                   into=pending.state if pending is not None else None)
    if core is None:
        return loss_blk, dx, d_mods, grads, None
    pending.swap_and_add()
    pending.exchange_and_sum()
    pending.share()
    return loss_blk, dx, d_mods, grads, dict(zip(BIG, pending.state))


def kernel(x, c, w_ada, b_ada, g_pre_mix, g_post_mix, g_pre_ffn, g_post_ffn, w_in, lam_re, lam_im, log_dt, b_re, b_im, c_re, c_im, d_skip, w_glu, b_glu, b_f, w_pa, w_pb, w_o, w_ffn_gate, w_ffn_up, w_ffn_down, loss_target, m_w_ada, m_b_ada, m_g_pre_mix, m_g_post_mix, m_g_pre_ffn, m_g_post_ffn, m_w_in, m_lam_re, m_lam_im, m_log_dt, m_b_re, m_b_im, m_c_re, m_c_im, m_d_skip, m_w_glu, m_b_glu, m_b_f, m_w_pa, m_w_pb, m_w_o, m_w_ffn_gate, m_w_ffn_up, m_w_ffn_down, v_w_ada, v_b_ada, v_g_pre_mix, v_g_post_mix, v_g_pre_ffn, v_g_post_ffn, v_w_in, v_lam_re, v_lam_im, v_log_dt, v_b_re, v_b_im, v_c_re, v_c_im, v_d_skip, v_w_glu, v_b_glu, v_b_f, v_w_pa, v_w_pb, v_w_o, v_w_ffn_gate, v_w_ffn_up, v_w_ffn_down):
    local = dict(locals())
    weights = {n: local[n] for n in WEIGHTS}
    moments_m = {n: local["m_" + n] for n in WEIGHTS}
    moments_v = {n: local["v_" + n] for n in WEIGHTS}
    depth, d = g_pre_mix.shape
    n_mod = w_ada.shape[2] * N_CHIPS // d
    mx, my, mc = lax.axis_index("x"), lax.axis_index("y"), lax.axis_index("c")
    my_chip = 2 * mx + my
    my_dev = 4 * mx + 2 * my + mc
    xs = x[0]

    shards = [[weights[n][l].astype(BF16) for n in BIG] for l in range(depth)]
    wts0 = dict(zip(BIG, _gather_layer("gather_weights_0", shards[0])))
    small = {n: weights[n] for n in SMALL}

    c_pad = jnp.pad(c, ((0, SUBLANES - 1), (0, 0)))
    c_all = _all_gather("gather_cond", c_pad).reshape(N_DEV, SUBLANES, d)[:, 0, :]
    silu = lambda v: v * _sigmoid(v)
    n_cols = w_ada.shape[2]
    mod_shard = []
    for l in range(depth):
        bias = lax.dynamic_slice_in_dim(b_ada[l], my_chip * n_cols, n_cols)
        mod_shard.append(_mm_plain(f"ada_{l}", c_all, w_ada[l], "nn", F32, add=jnp.broadcast_to(bias, (N_DEV, n_cols)),
                                   a_fn=silu, tm=N_DEV, tn=512, tk=1024))
    mod_block = jnp.concatenate(mod_shard, axis=1)
    mod_all = _all_gather("gather_mod", mod_block).reshape(N_DEV, N_DEV, depth, n_cols)
    mod_rows = lax.dynamic_index_in_dim(mod_all[0::2], my_dev, axis=1, keepdims=False)
    mods = [mod_rows[:, l, :].reshape(n_mod, d) for l in range(depth)]

    loss_blk, dx, d_mods, grads, big_grads = _fwd_bwd(xs, loss_target[0], mods, small, wts0, shards[1:],
                                                      core=mc.astype(jnp.int32).reshape(1))
    loss = lax.psum(loss_blk[0, 0], ("x", "y", "c"))
    grad_x = dx[None]

    partial_names = ("g_pre_mix", "g_post_mix", "g_pre_ffn", "g_post_ffn", "d_skip", "b_glu", "b_f", "a_bar",
                     "bbar_re", "bbar_im", "c_re", "c_im")
    n_state, group_ch = b_re.shape[2:]
    contrib = list(d_mods)
    for l in range(depth):
        compact = _compact_partials(grads[l], n_state, group_ch)
        contrib += [compact[n] for n in partial_names]
    contrib_shapes = [a.shape for a in contrib]
    block = _pack(contrib, LANES, BF16_ROWS, F32)
    rows = block.shape[0]
    all_blocks = _all_gather("gather_small_grads", block).reshape(N_DEV, rows, LANES)
    summed = _unpack(_sum_blocks("sum_small_grads", all_blocks, F32), contrib_shapes)
    per_layer = len(partial_names)
    small_grads = {n: [] for n in SMALL}
    d_mod_all = []
    for l in range(depth):
        small_grads["b_ada"].append(summed[l].reshape(-1))
        gl = dict(zip(partial_names, summed[depth + l * per_layer:depth + (l + 1) * per_layer]))
        for n in ("g_pre_mix", "g_post_mix", "g_pre_ffn", "g_post_ffn", "d_skip", "b_glu", "b_f"):
            small_grads[n].append(gl[n])
        for n, gval in _small_grads_from_partials(gl, small, l).items():
            small_grads[n].append(gval)
        mod_rows_ = n_mod * d // LANES
        d_mod_all.append(all_blocks[:, l * mod_rows_:(l + 1) * mod_rows_, :].reshape(N_DEV, n_mod * d))
    small_grads = {n: jnp.stack(v) for n, v in small_grads.items()}

    g_w_ada = []
    for l in range(depth):
        cols = lax.dynamic_slice_in_dim(d_mod_all[l], my_chip * n_cols, n_cols, axis=1)
        g_w_ada.append(_mm_plain(f"dw_ada_{l}", c_all, cols, "tn", F32, a_fn=silu, tm=512, tn=512, tk=N_DEV))
    all_grads = dict(big_grads)
    all_grads.update(small_grads)
    all_grads["w_ada"] = jnp.stack(g_w_ada)

    delta, new_m, new_v = {}, {}, {}
    for n in ("w_ada",) + BIG:
        last = weights[n].shape[2]
        to_stored, from_stored = ((0, 1, 2),) * 2 if last % LANES == 0 else ((0, 2, 1),) * 2 if last % SUBLANES == 0 \
            else ((2, 0, 1), (1, 2, 0))
        view, back = (lambda a: a.transpose(to_stored)), (lambda a: a.transpose(from_stored))
        outs = _adamw(f"adamw_{n}", view(weights[n]), view(all_grads[n]), view(moments_m[n]), view(moments_v[n]))
        delta[n], new_m[n], new_v[n] = (back(o) for o in outs)
    small_shapes = [weights[n].shape for n in SMALL]
    packed = [_pack([src[n] for n in SMALL], LANES, SUBLANES, F32)[None] for src in (weights, all_grads, moments_m, moments_v)]
    outs = _adamw("adamw_small", *packed)
    for dst, buf in zip((delta, new_m, new_v), outs):
        dst.update(dict(zip(SMALL, _unpack(buf[0], small_shapes))))

    return (loss, grad_x, *[all_grads[n] for n in WEIGHTS], *[delta[n] for n in WEIGHTS],
            *[new_m[n] for n in WEIGHTS], *[new_v[n] for n in WEIGHTS])
```

```python
import functools
import math

import jax
import jax.numpy as jnp
from jax import lax
from jax.experimental import pallas as pl
from jax.experimental.pallas import tpu as pltpu

F32 = jnp.float32
BF16 = jnp.bfloat16
MESH = pl.DeviceIdType.MESH

RMS_EPS = 1e-6
EIG_CLIP = 1e-4
ADAM_LR, ADAM_B1, ADAM_B2, ADAM_EPS, ADAM_WD, ADAM_STEP = 0.001, 0.9, 0.999, 1e-08, 0.01, 10

LANES = 128
SUBLANES = 8
VMEM_LIMIT = 56 * 1024 * 1024
S5_ROWS = 256
S5_CHUNK = 1024
S5_UNROLL = 4
ATT_BLOCK = 512
F_PAD = 256
POSTNORM_ROWS = 1024
N_CHIPS = 4
N_DEV = 8

NN = (((1,), (0,)), ((), ()))
NT = (((1,), (1,)), ((), ()))
TN = (((0,), (0,)), ((), ()))
_DN = {"nn": NN, "nt": NT, "tn": TN}


def _cparams(**kw):
    return pltpu.CompilerParams(vmem_limit_bytes=VMEM_LIMIT, **kw)


def _pick(dim, target):
    best, t = None, LANES
    while t <= min(dim, target):
        if dim % t == 0:
            best = t
        t += LANES
    return best or dim


def _sigmoid(x):
    return 1.0 / (1.0 + jnp.exp(-x))


def _dot(a, b, dn):
    return lax.dot_general(a, b, dn, preferred_element_type=F32)


def _mm_raw(name, a, b, mode, grid, acc_shape, a_spec, b_spec, out_shapes, out_specs, epilogue,
            extra=(), extra_specs=(), a_fn=None, side=None):
    nk = grid[2]
    n_extra, n_out = len(extra), len(out_shapes)

    def body(*refs):
        a_ref, b_ref = refs[0], refs[1]
        extra_refs = refs[2:2 + n_extra]
        out_refs = refs[2 + n_extra:2 + n_extra + n_out]
        acc = refs[-1]
        k = pl.program_id(2)

        @pl.when(k == 0)
        def _():
            acc[...] = jnp.zeros_like(acc)

        av = a_ref[...]
        if a_fn is not None:
            av = a_fn(av.astype(F32))
        acc[...] += _dot(av.astype(BF16), b_ref[...].astype(BF16), _DN[mode])

        @pl.when(k == nk - 1)
        def _():
            epilogue(acc[...], extra_refs, out_refs)

    outs, side_outs = _hosted_call(body, side, name, grid, [a_spec, b_spec, *extra_specs], list(out_specs),
                                   list(out_shapes), [pltpu.VMEM(acc_shape, F32)], (a, b, *extra))
    return outs if side is None else (outs, side_outs)


def _mm(name, a, b, mode, out_shapes, out_specs, epilogue, extra=(), extra_specs=(),
        tm=512, tn=512, tk=512, a_fn=None):
    if mode == "nn":
        (m, kd), (_, n) = a.shape, b.shape
    elif mode == "nt":
        (m, kd), (n, _) = a.shape, b.shape
    else:
        (kd, m), (_, n) = a.shape, b.shape
    tm, tn, tk = _pick(m, tm), _pick(n, tn), _pick(kd, tk)
    if mode == "tn":
        a_spec = pl.BlockSpec((tk, tm), lambda i, j, k: (k, i))
    else:
        a_spec = pl.BlockSpec((tm, tk), lambda i, j, k: (i, k))
    if mode == "nt":
        b_spec = pl.BlockSpec((tn, tk), lambda i, j, k: (j, k))
    else:
        b_spec = pl.BlockSpec((tk, tn), lambda i, j, k: (k, j))
    res = _mm_raw(name, a, b, mode, (m // tm, n // tn, kd // tk), (tm, tn), a_spec, b_spec, out_shapes, out_specs,
                  epilogue, extra=extra, extra_specs=extra_specs, a_fn=a_fn)
    return res, (tm, tn, tk)


def _store(dtype):
    def epilogue(acc, extra_refs, out_refs):
        out_refs[0][...] = acc.astype(dtype)
    return epilogue


def _mm_sum(name, m, n, tm, tn, pairs, out_dtype, side=None):
    offs, total = [], 0
    for pr in pairs:
        offs.append(total)
        total += pr[6]
    n_p = len(pairs)

    def body(*refs):
        o_ref, acc = refs[2 * n_p], refs[2 * n_p + 1]
        k = pl.program_id(2)

        @pl.when(k == 0)
        def _():
            acc[...] = jnp.zeros_like(acc)

        for p_ in range(n_p):
            @pl.when((k >= offs[p_]) & (k < offs[p_] + pairs[p_][6]))
            def _(p_=p_):
                acc[...] += _dot(refs[2 * p_][...].astype(BF16), refs[2 * p_ + 1][...].astype(BF16), NT)

        @pl.when(k == total - 1)
        def _():
            o_ref[...] = acc[...].astype(out_dtype)

    in_specs, operands = [], []
    for (a, a_block, a_index, b, b_block, b_index, steps), off in zip(pairs, offs):
        local = lambda k, off=off, steps=steps: jnp.clip(k - off, 0, steps - 1)
        in_specs.append(pl.BlockSpec(a_block, lambda i, j, k, f=a_index, local=local: f(i, local(k))))
        in_specs.append(pl.BlockSpec(b_block, lambda i, j, k, f=b_index, local=local: f(j, local(k))))
        operands += [a, b]
    (out,), side_outs = _hosted_call(
        body, side, name, (m // tm, n // tn, total), in_specs, [pl.BlockSpec((tm, tn), lambda i, j, k: (i, j))],
        [jax.ShapeDtypeStruct((m, n), out_dtype)], [pltpu.VMEM((tm, tn), F32)], operands)
    return out if side is None else (out, side_outs)


class _SideJob:
    def __init__(self, arrays, out_shapes, aliases, n_sems, copies):
        self.arrays, self.out_shapes, self.aliases, self.n_sems, self.copies = arrays, out_shapes, aliases, n_sems, copies


def _hosted_call(body, side, name, grid, in_specs, out_specs, out_shape, scratch_shapes, operands):
    if side is None:
        outs = pl.pallas_call(body, name=name, grid=grid, in_specs=in_specs, out_specs=out_specs, out_shape=out_shape,
                              scratch_shapes=scratch_shapes, compiler_params=_cparams())(*operands)
        return outs, []
    n_in, n_out, ns_in, ns_out = len(in_specs), len(out_specs), len(side.arrays), len(side.out_shapes)

    def wrapped(*refs):
        main_in, side_in = refs[:n_in], refs[n_in:n_in + ns_in]
        rest = refs[n_in + ns_in:]
        main_out, side_out, rest = rest[:n_out], rest[n_out:n_out + ns_out], rest[n_out + ns_out:]
        scratch, send_sems, recv_sems = rest[:-2], rest[-2], rest[-1]
        first, last = None, None
        for axis, extent in enumerate(grid):
            at_start, at_end = pl.program_id(axis) == 0, pl.program_id(axis) == extent - 1
            first = at_start if first is None else first & at_start
            last = at_end if last is None else last & at_end

        @pl.when(first)
        def _():
            for cp in side.copies(side_in, side_out, send_sems, recv_sems):
                cp.start()

        body(*main_in, *main_out, *scratch)

        @pl.when(last)
        def _():
            for cp in side.copies(side_in, side_out, send_sems, recv_sems):
                cp.wait()

    hbm = pl.BlockSpec(memory_space=pl.ANY)
    outs = pl.pallas_call(
        wrapped, name=name, grid=grid, in_specs=list(in_specs) + [hbm] * ns_in,
        out_specs=list(out_specs) + [hbm] * ns_out, out_shape=list(out_shape) + list(side.out_shapes),
        scratch_shapes=list(scratch_shapes) + [pltpu.SemaphoreType.DMA((side.n_sems,))] * 2,
        input_output_aliases={n_in + i: n_out + o for i, o in side.aliases.items()},
        compiler_params=_cparams(),
    )(*operands, *side.arrays)
    return outs[:n_out], outs[n_out:]


def _ffn_up(name, h, wg, wu, side=None):
    s, d = h.shape
    nc, fs = wg.shape[0], wg.shape[2]
    tm, tk = _pick(s, 1024), _pick(d, 1024)
    nk = d // tk

    def body(h_ref, wg_ref, wu_ref, a_ref, b_ref, hid_ref, acc_g, acc_u):
        k = pl.program_id(2)

        @pl.when(k == 0)
        def _():
            acc_g[...] = jnp.zeros_like(acc_g)
            acc_u[...] = jnp.zeros_like(acc_u)

        hv = h_ref[...]
        acc_g[...] += _dot(hv, wg_ref[...], NN)
        acc_u[...] += _dot(hv, wu_ref[...], NN)

        @pl.when(k == nk - 1)
        def _():
            av, bv = acc_g[...], acc_u[...]
            a_ref[...] = av.astype(BF16)
            b_ref[...] = bv.astype(BF16)
            hid_ref[...] = (av * _sigmoid(av) * bv).astype(BF16)

    w_spec = pl.BlockSpec((None, tk, fs), lambda i, j, k: (j, k, 0))
    o_spec = pl.BlockSpec((None, tm, fs), lambda i, j, k: (j, i, 0))
    sh = jax.ShapeDtypeStruct((nc, s, fs), BF16)
    return _hosted_call(
        body, side, name, (s // tm, nc, nk), [pl.BlockSpec((tm, tk), lambda i, j, k: (i, k)), w_spec, w_spec],
        [o_spec] * 3, [sh] * 3, [pltpu.VMEM((tm, fs), F32), pltpu.VMEM((tm, fs), F32)], (h, wg, wu))


def _mm_plain(name, a, b, mode, out_dtype, add=None, a_fn=None, tm=512, tn=512, tk=512):
    if mode == "nn":
        m, n = a.shape[0], b.shape[1]
    elif mode == "nt":
        m, n = a.shape[0], b.shape[0]
    else:
        m, n = a.shape[1], b.shape[1]
    tm_, tn_ = _pick(m, tm), _pick(n, tn)
    spec = pl.BlockSpec((tm_, tn_), lambda i, j, k: (i, j))

    def epilogue(acc, extra_refs, out_refs):
        if add is not None:
            acc = acc + extra_refs[0][...]
        out_refs[0][...] = acc.astype(out_dtype)

    extra = () if add is None else (add,)
    (out,), _ = _mm(name, a, b, mode, [jax.ShapeDtypeStruct((m, n), out_dtype)], [spec], epilogue,
                    extra=extra, extra_specs=[spec] * len(extra), tm=tm, tn=tn, tk=tk, a_fn=a_fn)
    return out


def _row_tile(s, d):
    return _pick(s, max(SUBLANES, (1 << 20) // (4 * d)))


def _prenorm_fwd(name, x, g, scale, shift):
    s, d = x.shape
    tr = _row_tile(s, d)

    def body(x_ref, g_ref, sc_ref, sh_ref, h_ref):
        xv = x_ref[...]
        r = lax.rsqrt(jnp.mean(xv * xv, axis=-1, keepdims=True) + RMS_EPS)
        h_ref[...] = ((xv * r * g_ref[...]) * (1.0 + sc_ref[...]) + sh_ref[...]).astype(BF16)

    row = pl.BlockSpec((tr, d), lambda i: (i, 0))
    vec = pl.BlockSpec((1, d), lambda i: (0, 0))
    return pl.pallas_call(body, name=name, grid=(s // tr,), in_specs=[row, vec, vec, vec], out_specs=row,
                          out_shape=jax.ShapeDtypeStruct((s, d), BF16), compiler_params=_cparams())(x, g, scale, shift)


def _prenorm_bwd(name, dh, x, g, scale, dx_res):
    s, d = x.shape
    tr = _row_tile(s, d)

    def body(dh_ref, x_ref, g_ref, sc_ref, dxr_ref, dx_ref, sums_ref):
        @pl.when(pl.program_id(0) == 0)
        def _():
            sums_ref[...] = jnp.zeros_like(sums_ref)

        xv, dhv, gv = x_ref[...], dh_ref[...].astype(F32), g_ref[...]
        r = lax.rsqrt(jnp.mean(xv * xv, axis=-1, keepdims=True) + RMS_EPS)
        xhat = xv * r
        dxn = dhv * (1.0 + sc_ref[...])
        dxhat = dxn * gv
        dx = r * (dxhat - xhat * jnp.mean(dxhat * xhat, axis=-1, keepdims=True))
        dx_ref[...] = dxr_ref[...] + dx
        sums_ref[0:1, :] += jnp.sum(dhv * (xhat * gv), axis=0, keepdims=True)
        sums_ref[1:2, :] += jnp.sum(dhv, axis=0, keepdims=True)
        sums_ref[2:3, :] += jnp.sum(dxn * xhat, axis=0, keepdims=True)

    row = pl.BlockSpec((tr, d), lambda i: (i, 0))
    vec = pl.BlockSpec((1, d), lambda i: (0, 0))
    acc = pl.BlockSpec((SUBLANES, d), lambda i: (0, 0))
    return pl.pallas_call(
        body, name=name, grid=(s // tr,), in_specs=[row, row, vec, vec, row], out_specs=[row, acc],
        out_shape=[jax.ShapeDtypeStruct((s, d), F32), jax.ShapeDtypeStruct((SUBLANES, d), F32)],
        compiler_params=_cparams())(dh, x, g, scale, dx_res)


def _postnorm_bwd(name, dxn, y, g, gate):
    s, d = y.shape
    tr = _row_tile(s, d)

    def body(dx_ref, y_ref, g_ref, gt_ref, dy_ref, sums_ref):
        @pl.when(pl.program_id(0) == 0)
        def _():
            sums_ref[...] = jnp.zeros_like(sums_ref)

        yv, dxv, gv = y_ref[...], dx_ref[...], g_ref[...]
        r = lax.rsqrt(jnp.mean(yv * yv, axis=-1, keepdims=True) + RMS_EPS)
        yhat = yv * r
        dn = dxv * gt_ref[...]
        dyhat = dn * gv
        dy_ref[...] = (r * (dyhat - yhat * jnp.mean(dyhat * yhat, axis=-1, keepdims=True))).astype(BF16)
        sums_ref[0:1, :] += jnp.sum(dxv * (yhat * gv), axis=0, keepdims=True)
        sums_ref[1:2, :] += jnp.sum(dn * yhat, axis=0, keepdims=True)

    row = pl.BlockSpec((tr, d), lambda i: (i, 0))
    vec = pl.BlockSpec((1, d), lambda i: (0, 0))
    acc = pl.BlockSpec((SUBLANES, d), lambda i: (0, 0))
    return pl.pallas_call(
        body, name=name, grid=(s // tr,), in_specs=[row, row, vec, vec], out_specs=[row, acc],
        out_shape=[jax.ShapeDtypeStruct((s, d), BF16), jax.ShapeDtypeStruct((SUBLANES, d), F32)],
        compiler_params=_cparams())(dxn, y, g, gate)


def _loss_grad(name, y, target):
    s, d = y.shape
    tr = _row_tile(s, d)

    def body(y_ref, t_ref, dy_ref, loss_ref):
        @pl.when(pl.program_id(0) == 0)
        def _():
            loss_ref[...] = jnp.zeros_like(loss_ref)

        err = y_ref[...] - t_ref[...]
        dy_ref[...] = err * (1.0 / d)
        part = jnp.sum(jnp.sum(err * err, axis=-1, keepdims=True), axis=0, keepdims=True) * (0.5 / d)
        loss_ref[...] += jnp.broadcast_to(part, loss_ref.shape)

    row = pl.BlockSpec((tr, d), lambda i: (i, 0))
    acc = pl.BlockSpec((SUBLANES, LANES), lambda i: (0, 0))
    return pl.pallas_call(
        body, name=name, grid=(s // tr,), in_specs=[row, row], out_specs=[row, acc],
        out_shape=[jax.ShapeDtypeStruct((s, d), F32), jax.ShapeDtypeStruct((SUBLANES, LANES), F32)],
        compiler_params=_cparams())(y, target)


def _gelu(y):
    c = math.sqrt(2.0 / math.pi)
    return 0.5 * y * (1.0 + jnp.tanh(c * (y + 0.044715 * (y * y * y))))


def _gelu_grad(y):
    c = math.sqrt(2.0 / math.pi)
    th = jnp.tanh(c * (y + 0.044715 * (y * y * y)))
    return 0.5 * (1.0 + th) + 0.5 * y * (1.0 - th * th) * c * (1.0 + 3.0 * 0.044715 * (y * y))


def _cmul_add(br, bi, ar, ai, xr, xi):
    return br + ar * xr - ai * xi, bi + ar * xi + ai * xr


def _scan_rows(x_ref, row0, n_steps, ns2, pow_ref, tab_ref, carry_ref, reverse, fold=None):
    assert n_steps % SUBLANES == 0
    wc = min(S5_CHUNK, ns2)
    sub = lax.broadcasted_iota(jnp.int32, (SUBLANES, wc), 0)
    unroll = S5_UNROLL if n_steps % S5_UNROLL == 0 else 1
    for c0 in range(0, ns2, wc):
        re = slice(c0, c0 + wc)
        im = slice(ns2 + c0, ns2 + c0 + wc)
        first_power = slice(n_steps - 1, n_steps) if reverse else slice(0, 1)
        ar = jnp.broadcast_to(pow_ref[first_power, re], (SUBLANES, wc))
        ai = jnp.broadcast_to(pow_ref[first_power, im], (SUBLANES, wc))
        rows = lambda r: pl.ds(pl.multiple_of(row0 + r * SUBLANES, SUBLANES), SUBLANES)
        step_of = lambda i: (n_steps - 1 - i) if reverse else i

        def local(i, carry, re=re, im=im, ar=ar, ai=ai):
            for u in range(unroll):
                r = step_of(i * unroll + u)
                carry = _cmul_add(x_ref[rows(r), re], x_ref[rows(r), im], ar, ai, *carry)
                x_ref[rows(r), re], x_ref[rows(r), im] = carry
            return carry

        zero = jnp.zeros((SUBLANES, wc), F32)
        lr, li = lax.fori_loop(0, n_steps // unroll, local, (zero, zero))

        tabs = [tab_ref[k, :, re] for k in range(8)]
        for lvl, k in enumerate((1, 2, 4)):
            sh = (SUBLANES - k) if reverse else k
            lr, li = _cmul_add(lr, li, tabs[2 * lvl], tabs[2 * lvl + 1], pltpu.roll(lr, sh, 0), pltpu.roll(li, sh, 0))
        cr, ci = carry_ref[0:1, re], carry_ref[0:1, im]
        lr, li = _cmul_add(lr, li, tabs[6], tabs[7], cr, ci)
        edge, away, last = (SUBLANES - 1, SUBLANES - 1, 0) if reverse else (0, 1, SUBLANES - 1)
        carry_ref[0:1, re] = lr[last:last + 1, :]
        carry_ref[0:1, im] = li[last:last + 1, :]
        er = jnp.where(sub == edge, cr, pltpu.roll(lr, away, 0))
        ei = jnp.where(sub == edge, ci, pltpu.roll(li, away, 0))

        def fix(j, acc, re=re, im=im, er=er, ei=ei, c0=c0):
            base = pl.ds(pl.multiple_of(j * SUBLANES, SUBLANES), SUBLANES)
            pw_r, pw_i = pow_ref[base, re], pow_ref[base, im]
            for i in range(SUBLANES):
                r = j * SUBLANES + i
                xr, xi = _cmul_add(x_ref[rows(r), re], x_ref[rows(r), im], pw_r[i:i + 1, :], pw_i[i:i + 1, :], er, ei)
                x_ref[rows(r), re], x_ref[rows(r), im] = xr, xi
                if fold is not None:
                    acc = fold(c0, r, xr, xi, acc)
            return acc

        acc = lax.fori_loop(0, n_steps // SUBLANES, fix, (zero, zero) if fold is not None else 0)
        if fold is not None:
            fold(c0, None, None, None, acc)


def _s5_fwd(name, u, b_blk, c_blk, a_f, tab_f, dskip, w_glu, b_glu):
    s, w = u.shape[0], w_glu.shape[0]
    nkb = w // LANES
    ns2 = b_blk.shape[2] // 2 * nkb
    half = ns2 // nkb
    t = min(S5_ROWS, s)
    nblk = s // t

    def body(u_ref, b_ref, c_ref, a_ref, tab_ref, ds_ref, wg_ref, bg_ref, y_ref, ys_ref, cs_ref, xs, carry):
        @pl.when(pl.program_id(0) == 0)
        def _():
            carry[...] = jnp.zeros_like(carry)

        cs_ref[0] = carry[...]
        for kb in range(nkb):
            bu = _dot(u_ref[:, kb * LANES:(kb + 1) * LANES], b_ref[kb], NN)
            xs[:, kb * half:(kb + 1) * half] = bu[:, :half]
            xs[:, ns2 + kb * half:ns2 + (kb + 1) * half] = bu[:, half:]
        _scan_rows(xs, 0, t // SUBLANES, ns2, a_ref, tab_ref, carry, reverse=False)
        for kb in range(nkb):
            cols = slice(kb * LANES, (kb + 1) * LANES)
            yk = _dot(xs[:, kb * half:(kb + 1) * half].astype(BF16), c_ref[kb, :half, :], NN)
            yk += _dot(xs[:, ns2 + kb * half:ns2 + (kb + 1) * half].astype(BF16), c_ref[kb, half:, :], NN)
            y_ref[:, cols] = yk + ds_ref[:, cols] * u_ref[:, cols].astype(F32)
        z = _gelu(y_ref[...])
        gate = _sigmoid(_dot(z.astype(BF16), wg_ref[...], NN) + bg_ref[...])
        ys_ref[...] = (z * gate).astype(BF16)

    row = pl.BlockSpec((t, w), lambda i: (i, 0))
    full = lambda shape: pl.BlockSpec(shape, lambda i: (0,) * len(shape))
    return pl.pallas_call(
        body, name=name, grid=(nblk,),
        in_specs=[row, full(b_blk.shape), full(c_blk.shape), full(a_f.shape), full(tab_f.shape), full(dskip.shape),
                  full(w_glu.shape), full(b_glu.shape)],
        out_specs=[row, row, pl.BlockSpec((1, 1, 2 * ns2), lambda i: (i, 0, 0))],
        out_shape=[jax.ShapeDtypeStruct((s, w), F32), jax.ShapeDtypeStruct((s, w), BF16),
                   jax.ShapeDtypeStruct((nblk, 1, 2 * ns2), F32)],
        scratch_shapes=[pltpu.VMEM((t, 2 * ns2), F32), pltpu.VMEM((1, 2 * ns2), F32)],
        compiler_params=_cparams(),
    )(u, b_blk, c_blk, a_f, tab_f, dskip, w_glu, b_glu)


def _s5_bwd(name, u, dys, y, carries, b_blk, c_blk, a_f, a_r, tab_f, tab_r, dskip, w_glu, b_glu):
    s, w = u.shape[0], w_glu.shape[0]
    nkb = w // LANES
    ns2 = b_blk.shape[2] // 2 * nkb
    half = ns2 // nkb
    t = min(S5_ROWS, s)
    nblk = s // t
    ng = t // SUBLANES

    def body(u_ref, dys_ref, y_ref, cs_ref, b_ref, c_ref, af_ref, ar_ref, tabf_ref, tabr_ref, ds_ref, wg_ref, bg_ref,
             du_ref, db_ref, dc_ref, da_ref, dwg_ref, vec_ref, xs, gs, dyv, fcarry, gcarry):
        @pl.when(pl.program_id(0) == 0)
        def _():
            db_ref[...] = jnp.zeros_like(db_ref)
            dc_ref[...] = jnp.zeros_like(dc_ref)
            da_ref[...] = jnp.zeros_like(da_ref)
            dwg_ref[...] = jnp.zeros_like(dwg_ref)
            vec_ref[...] = jnp.zeros_like(vec_ref)
            gcarry[...] = jnp.zeros_like(gcarry)

        yv = y_ref[...]
        z = _gelu(yv)
        zb = z.astype(BF16)
        gate = _sigmoid(_dot(zb, wg_ref[...], NN) + bg_ref[...])
        dout = dys_ref[...].astype(F32)
        dt = dout * z * gate * (1.0 - gate)
        dtb = dt.astype(BF16)
        dz = dout * gate + _dot(dtb, wg_ref[...], NT)
        dy = dz * _gelu_grad(yv)
        dyv[...] = dy
        dwg_ref[...] += _dot(zb, dtb, TN)
        vec_ref[0:1, :] += jnp.sum(dt, axis=0, keepdims=True)
        vec_ref[1:2, :] += jnp.sum(dy * u_ref[...].astype(F32), axis=0, keepdims=True)

        fcarry[...] = cs_ref[0]
        xs[0:SUBLANES, :] = jnp.broadcast_to(cs_ref[0], (SUBLANES, 2 * ns2))
        for kb in range(nkb):
            bu = _dot(u_ref[:, kb * LANES:(kb + 1) * LANES], b_ref[kb], NN)
            xs[SUBLANES:, kb * half:(kb + 1) * half] = bu[:, :half]
            xs[SUBLANES:, ns2 + kb * half:ns2 + (kb + 1) * half] = bu[:, half:]
        _scan_rows(xs, SUBLANES, ng, ns2, af_ref, tabf_ref, fcarry, reverse=False)
        first_segment = lax.broadcasted_iota(jnp.int32, (SUBLANES, 2 * ns2), 0) == 0
        xs[0:SUBLANES, :] = jnp.where(first_segment, xs[0:SUBLANES, :], pltpu.roll(xs[t:t + SUBLANES, :], 1, 0))

        for kb in range(nkb):
            dyk = dyv[:, kb * LANES:(kb + 1) * LANES].astype(BF16)
            re = slice(kb * half, (kb + 1) * half)
            im = slice(ns2 + kb * half, ns2 + (kb + 1) * half)
            gs[:, re] = _dot(dyk, c_ref[kb, :half, :], NT)
            gs[:, im] = _dot(dyk, c_ref[kb, half:, :], NT)
            dc_ref[kb, :half, :] += _dot(xs[SUBLANES:, re].astype(BF16), dyk, TN)
            dc_ref[kb, half:, :] += _dot(xs[SUBLANES:, im].astype(BF16), dyk, TN)

        def fold(c0, r, gr, gi, acc):
            wc = min(S5_CHUNK, ns2)
            re = slice(c0, c0 + wc)
            im = slice(ns2 + c0, ns2 + c0 + wc)
            if r is None:
                da_ref[:, re] += acc[0]
                da_ref[:, im] += acc[1]
                return acc
            before = pl.ds(pl.multiple_of(r * SUBLANES, SUBLANES), SUBLANES)
            xpr, xpi = xs[before, re], xs[before, im]
            return acc[0] + gr * xpr + gi * xpi, acc[1] - gr * xpi + gi * xpr

        _scan_rows(gs, 0, ng, ns2, ar_ref, tabr_ref, gcarry, reverse=True, fold=fold)

        for kb in range(nkb):
            cols = slice(kb * LANES, (kb + 1) * LANES)
            re = slice(kb * half, (kb + 1) * half)
            im = slice(ns2 + kb * half, ns2 + (kb + 1) * half)
            uk = u_ref[:, cols]
            gr = gs[:, re].astype(BF16)
            gi = gs[:, im].astype(BF16)
            db_ref[kb, :, :half] += _dot(uk, gr, TN)
            db_ref[kb, :, half:] += _dot(uk, gi, TN)
            duk = _dot(gr, b_ref[kb, :, :half], NT) + _dot(gi, b_ref[kb, :, half:], NT)
            du_ref[:, cols] = (duk + ds_ref[:, cols] * dyv[:, cols]).astype(BF16)

    rev = lambda i: (nblk - 1 - i, 0)
    row = pl.BlockSpec((t, w), rev)
    full = lambda shape: pl.BlockSpec(shape, lambda i: (0,) * len(shape))
    return pl.pallas_call(
        body, name=name, grid=(nblk,),
        in_specs=[row, row, row, pl.BlockSpec((1, 1, 2 * ns2), lambda i: (nblk - 1 - i, 0, 0)),
                  full(b_blk.shape), full(c_blk.shape), full(a_f.shape), full(a_r.shape), full(tab_f.shape),
                  full(tab_r.shape), full(dskip.shape), full(w_glu.shape), full(b_glu.shape)],
        out_specs=[row, full(b_blk.shape), full(c_blk.shape), full((SUBLANES, 2 * ns2)), full((w, w)),
                   full((SUBLANES, w))],
        out_shape=[jax.ShapeDtypeStruct((s, w), BF16), jax.ShapeDtypeStruct(b_blk.shape, F32),
                   jax.ShapeDtypeStruct(c_blk.shape, F32), jax.ShapeDtypeStruct((SUBLANES, 2 * ns2), F32),
                   jax.ShapeDtypeStruct((w, w), F32), jax.ShapeDtypeStruct((SUBLANES, w), F32)],
        scratch_shapes=[pltpu.VMEM((t + SUBLANES, 2 * ns2), F32), pltpu.VMEM((t, 2 * ns2), F32),
                        pltpu.VMEM((t, w), F32), pltpu.VMEM((1, 2 * ns2), F32), pltpu.VMEM((1, 2 * ns2), F32)],
        compiler_params=_cparams(),
    )(u, dys, y, carries, b_blk, c_blk, a_f, a_r, tab_f, tab_r, dskip, w_glu, b_glu)


def _log_sigmoid(x):
    return jnp.minimum(x, 0.0) - jnp.log(1.0 + jnp.exp(-jnp.abs(x)))


def _cum_fwd(name, f_t, b_f):
    h, s = f_t.shape
    tc = _pick(s, 512)
    nb = s // tc

    def body(f_ref, b_ref, c_ref, carry):
        @pl.when(pl.program_id(0) == 0)
        def _():
            carry[...] = jnp.zeros_like(carry)

        lf = _log_sigmoid(f_ref[...] + b_ref[...])
        upper = (lax.broadcasted_iota(jnp.int32, (tc, tc), 0) <= lax.broadcasted_iota(jnp.int32, (tc, tc), 1))
        cum = lax.dot_general(lf, upper.astype(F32), NN, precision=lax.Precision.HIGHEST,
                              preferred_element_type=F32) + carry[...]
        c_ref[...] = cum
        carry[...] += jnp.sum(lf, axis=1, keepdims=True)

    blk = pl.BlockSpec((h, tc), lambda i: (0, i))
    return pl.pallas_call(body, name=name, grid=(nb,), in_specs=[blk, pl.BlockSpec((h, 1), lambda i: (0, 0))],
                          out_specs=blk, out_shape=jax.ShapeDtypeStruct((h, s), F32),
                          scratch_shapes=[pltpu.VMEM((h, 1), F32)], compiler_params=_cparams())(f_t, b_f)


def _cum_bwd(name, dcq, dck, f_t, b_f):
    h, s = f_t.shape
    tc = _pick(s, 512)
    nb = s // tc

    def body(dcq_ref, dck_ref, f_ref, b_ref, df_ref, db_ref, carry):
        @pl.when(pl.program_id(0) == 0)
        def _():
            carry[...] = jnp.zeros_like(carry)
            db_ref[...] = jnp.zeros_like(db_ref)

        dc = dcq_ref[...] + dck_ref[...]
        lower = (lax.broadcasted_iota(jnp.int32, (tc, tc), 0) >= lax.broadcasted_iota(jnp.int32, (tc, tc), 1))
        dlf = lax.dot_general(dc, lower.astype(F32), NN, precision=lax.Precision.HIGHEST,
                              preferred_element_type=F32) + carry[...]
        carry[...] += jnp.sum(dc, axis=1, keepdims=True)
        df = dlf * _sigmoid(-(f_ref[...] + b_ref[...]))
        df_ref[...] = df
        db_ref[...] += jnp.broadcast_to(jnp.sum(df, axis=1, keepdims=True), db_ref.shape)

    blk = pl.BlockSpec((h, tc), lambda i: (0, nb - 1 - i))
    return pl.pallas_call(
        body, name=name, grid=(nb,), in_specs=[blk, blk, blk, pl.BlockSpec((h, 1), lambda i: (0, 0))],
        out_specs=[blk, pl.BlockSpec((h, LANES), lambda i: (0, 0))],
        out_shape=[jax.ShapeDtypeStruct((h, s), F32), jax.ShapeDtypeStruct((h, LANES), F32)],
        scratch_shapes=[pltpu.VMEM((h, 1), F32)], compiler_params=_cparams())(dcq, dck, f_t, b_f)


def _attn_fwd(name, qkv, q_blk, k_blk, v_blk, n_pairs, ck, side=None):
    s = qkv.shape[0]
    dh = LANES // 2
    t = min(ATT_BLOCK, s)
    nq = s // t
    scale = dh ** -0.5

    def body(q_ref, k_ref, v_ref, ck_ref, o_ref, lse_ref, m_s, acc_s):
        i = pl.program_id(1)
        low = lax.broadcasted_iota(jnp.int32, (1, LANES), 1) < dh
        qs = (q_ref[...].astype(F32) * scale).astype(BF16)
        zero = jnp.zeros_like(qs)
        qh = (jnp.where(low, qs, zero), jnp.where(low, zero, qs))
        m_s[...] = jnp.full(m_s.shape, -1e30, F32)
        acc_s[...] = jnp.zeros_like(acc_s)
        causal = (lax.broadcasted_iota(jnp.int32, (t, t), 1) <= lax.broadcasted_iota(jnp.int32, (t, t), 0))

        def step(j, diagonal):
            r0 = pl.multiple_of(j * t, t)
            kj = k_ref[pl.ds(r0, t), :]
            vj = v_ref[pl.ds(r0, t), :]
            one = jnp.ones_like(vj)
            vh = (jnp.where(low, vj, one), jnp.where(low, one, vj))
            for hd in range(2):
                sc = _dot(qh[hd], kj, NT) - ck_ref[hd, j]
                if diagonal:
                    sc = jnp.where(causal, sc, -1e30)
                m_old = m_s[hd]
                m_new = jnp.maximum(m_old, jnp.max(sc, axis=1, keepdims=True))
                p = jnp.exp(sc - m_new)
                acc_s[hd] = jnp.exp(m_old - m_new) * acc_s[hd] + _dot(p.astype(BF16), vh[hd], NN)
                m_s[hd] = m_new

        def full(j, _):
            step(j, False)
            return 0

        lax.fori_loop(0, i, full, 0)
        step(i, True)
        a0, a1 = acc_s[0], acc_s[1]
        o_ref[...] = jnp.where(low, a0 / pltpu.roll(a0, dh, 1), a1 / pltpu.roll(a1, dh, 1)).astype(BF16)
        lse_ref[0] = m_s[0] + jnp.log(a0[:, dh:dh + 1])
        lse_ref[1] = m_s[1] + jnp.log(a1[:, 0:1])

    return _hosted_call(
        body, side, name, (n_pairs, nq),
        [pl.BlockSpec((t, LANES), lambda hp, i: (i, q_blk + hp)),
         pl.BlockSpec((s, LANES), lambda hp, i: (0, k_blk + hp)),
         pl.BlockSpec((s, LANES), lambda hp, i: (0, v_blk + hp)),
         pl.BlockSpec((2, nq, 1, t), lambda hp, i: (hp, 0, 0, 0))],
        [pl.BlockSpec((t, LANES), lambda hp, i: (i, hp)), pl.BlockSpec((2, t, 1), lambda hp, i: (hp, i, 0))],
        [jax.ShapeDtypeStruct((s, LANES * n_pairs), BF16), jax.ShapeDtypeStruct((2 * n_pairs, s, 1), F32)],
        [pltpu.VMEM((2, t, 1), F32), pltpu.VMEM((2, t, LANES), F32)], (qkv, qkv, qkv, ck))


def _attn_bwd(name, qkv, q_blk, k_blk, v_blk, n_pairs, o, do, lse_rows, ck_cols, side=None):
    s = qkv.shape[0]
    dh = LANES // 2
    t = min(ATT_BLOCK, s)
    nk = s // t
    scale = dh ** -0.5

    def body(q_ref, k_ref, v_ref, o_ref, do_ref, lse_ref, ck_ref,
             dq_ref, dk_ref, dv_ref, dcq_ref, dck_ref, delta, dqt, dk_acc, dv_acc):
        j = pl.program_id(1)
        low = lax.broadcasted_iota(jnp.int32, (1, LANES), 1) < dh
        low_rows = lax.broadcasted_iota(jnp.int32, (LANES, 1), 0) < dh

        @pl.when(j == 0)
        def _():
            dqt[...] = jnp.zeros_like(dqt)
            sel = (jnp.broadcast_to(low, (SUBLANES, LANES)).astype(F32), jnp.broadcast_to(~low, (SUBLANES, LANES)).astype(F32))

            def fill(i, _):
                r0 = pl.multiple_of(i * t, t)
                prod = do_ref[pl.ds(r0, t), :].astype(F32) * o_ref[pl.ds(r0, t), :].astype(F32)
                for hd in range(2):
                    delta[hd, i] = lax.dot_general(sel[hd], prod, NT, precision=lax.Precision.HIGHEST,
                                                   preferred_element_type=F32)
                return 0

            lax.fori_loop(0, nk, fill, 0)

        kj, vj = k_ref[...], v_ref[...]
        zero, one = jnp.zeros_like(kj), jnp.ones_like(kj)
        kh = (jnp.where(low, kj, zero), jnp.where(low, zero, kj))
        vh = (jnp.where(low, vj, zero), jnp.where(low, zero, vj))
        kjt = kj.astype(F32).T.astype(BF16)
        one_t = jnp.ones_like(kjt)
        kht = (jnp.where(low_rows, kjt, one_t), jnp.where(low_rows, one_t, kjt))
        dk_acc[...] = jnp.zeros_like(dk_acc)
        dv_acc[...] = jnp.zeros_like(dv_acc)
        causal_t = (lax.broadcasted_iota(jnp.int32, (t, t), 0) <= lax.broadcasted_iota(jnp.int32, (t, t), 1))

        def step(i, diagonal):
            r0 = pl.multiple_of(i * t, t)
            qi = (q_ref[pl.ds(r0, t), :].astype(F32) * scale).astype(BF16)
            doi = do_ref[pl.ds(r0, t), :]
            qone, dzero = jnp.ones_like(qi), jnp.zeros_like(doi)
            qsel = (jnp.where(low, qi, qone), jnp.where(low, qone, qi))
            dosel = (jnp.where(low, doi, dzero), jnp.where(low, dzero, doi))
            for hd in range(2):
                st = _dot(kh[hd], qi, NT) - ck_ref[hd] - lse_ref[hd, i]
                pt = jnp.exp(st)
                if diagonal:
                    pt = jnp.where(causal_t, pt, 0.0)
                dst = pt * (_dot(vh[hd], doi, NT) - delta[hd, i, 0:1, :])
                dsb = dst.astype(BF16)
                dv_acc[...] += _dot(pt.astype(BF16), dosel[hd], NN)
                dk_acc[hd] += _dot(dsb, qsel[hd], NN)
                dqt[hd, i] += _dot(kht[hd], dsb, NN)

        step(j, True)

        def rest(i, _):
            step(i, False)
            return 0

        lax.fori_loop(j + 1, nk, rest, 0)
        dk_ref[...] = jnp.where(low, dk_acc[0], dk_acc[1]).astype(BF16)
        dv_ref[...] = dv_acc[...].astype(BF16)
        dck_ref[0] = -dk_acc[0][:, dh:dh + 1]
        dck_ref[1] = -dk_acc[1][:, 0:1]

        @pl.when(j == nk - 1)
        def _():
            def emit(i, _):
                r0 = pl.multiple_of(i * t, t)
                d0, d1 = dqt[0, i], dqt[1, i]
                dq_ref[pl.ds(r0, t), :] = (jnp.where(low_rows, d0, d1) * scale).T.astype(BF16)
                dcq_ref[0, i] = d0[dh:dh + 1, :]
                dcq_ref[1, i] = d1[0:1, :]
                return 0

            lax.fori_loop(0, nk, emit, 0)

    col_blk = lambda base: pl.BlockSpec((t, LANES), lambda hp, j: (j, base + hp))
    col_all = lambda base: pl.BlockSpec((s, LANES), lambda hp, j: (0, base + hp))
    rows_all = pl.BlockSpec((2, nk, 1, t), lambda hp, j: (hp, 0, 0, 0))
    return _hosted_call(
        body, side, name, (n_pairs, nk),
        [col_all(q_blk), col_blk(k_blk), col_blk(v_blk), col_all(0), col_all(0), rows_all,
         pl.BlockSpec((2, t, 1), lambda hp, j: (hp, j, 0))],
        [col_all(0), col_blk(0), col_blk(0), rows_all, pl.BlockSpec((2, t, 1), lambda hp, j: (hp, j, 0))],
        [jax.ShapeDtypeStruct((s, LANES * n_pairs), BF16), jax.ShapeDtypeStruct((s, LANES * n_pairs), BF16),
         jax.ShapeDtypeStruct((s, LANES * n_pairs), BF16), jax.ShapeDtypeStruct((2 * n_pairs, nk, 1, t), F32),
         jax.ShapeDtypeStruct((2 * n_pairs, s, 1), F32)],
        [pltpu.VMEM((2, nk, SUBLANES, t), F32), pltpu.VMEM((2, nk, LANES, t), F32),
         pltpu.VMEM((2, t, LANES), F32), pltpu.VMEM((t, LANES), F32)],
        (qkv, qkv, qkv, o, do, lse_rows, ck_cols))


def _adamw(name, w, g, m, v):
    n_l, r, c = w.shape
    by_rows = r % SUBLANES == 0
    tr = _pick8(r, max(SUBLANES, (1 << 20) // (4 * c))) if by_rows else r
    tl = 1 if by_rows else max(t for t in range(1, n_l + 1) if n_l % t == 0 and t * r * c * 4 <= (1 << 20))

    def body(w_ref, g_ref, m_ref, v_ref, d_ref, mo_ref, vo_ref):
        gv = g_ref[...]
        m2 = ADAM_B1 * m_ref[...] + (1.0 - ADAM_B1) * gv
        v2 = ADAM_B2 * v_ref[...] + (1.0 - ADAM_B2) * (gv * gv)
        m_hat = m2 / (1.0 - ADAM_B1 ** ADAM_STEP)
        v_hat = v2 / (1.0 - ADAM_B2 ** ADAM_STEP)
        d_ref[...] = -ADAM_LR * (m_hat / (jnp.sqrt(v_hat) + ADAM_EPS) + ADAM_WD * w_ref[...])
        mo_ref[...] = m2
        vo_ref[...] = v2

    blk = pl.BlockSpec((None, tr, c) if by_rows else (tl, r, c), lambda l, i: (l, i, 0))
    sh = jax.ShapeDtypeStruct((n_l, r, c), F32)
    return pl.pallas_call(body, name=name, grid=(n_l // tl, r // tr), in_specs=[blk] * 4,
                          out_specs=[blk] * 3, out_shape=[sh, sh, sh], compiler_params=_cparams())(w, g, m, v)


def _pick8(dim, target, mult=SUBLANES):
    best, t = None, mult
    while t <= min(dim, target):
        if dim % t == 0:
            best = t
        t += mult
    return best or dim


BF16_ROWS = 16


def _sum_blocks(name, x, out_dtype):
    n, r, c = x.shape
    tr = _pick8(r, max(BF16_ROWS, (1 << 19) // (4 * c)), BF16_ROWS)

    def body(x_ref, o_ref):
        acc = x_ref[0].astype(F32)
        for i in range(1, n):
            acc = acc + x_ref[i].astype(F32)
        o_ref[...] = acc.astype(out_dtype)

    return pl.pallas_call(body, name=name, grid=(r // tr,),
                          in_specs=[pl.BlockSpec((n, tr, c), lambda i: (0, i, 0))],
                          out_specs=pl.BlockSpec((tr, c), lambda i: (i, 0)),
                          out_shape=jax.ShapeDtypeStruct((r, c), out_dtype), compiler_params=_cparams())(x)


def _all_gather(name, x_shard):
    m_per, n = x_shard.shape

    def body(x_ref, out_ref, send_sems, recv_sems):
        x, y, c = lax.axis_index("x"), lax.axis_index("y"), lax.axis_index("c")
        me, sibling = (x, y, c), (x, y, 1 - c)
        chips = [(1 - x, y), (x, 1 - y), (1 - x, 1 - y)]

        def rows(px, py, pc):
            return out_ref.at[pl.ds((4 * px + 2 * py + pc) * m_per, m_per), :]

        def copy(k, block, to, src=None):
            return pltpu.make_async_remote_copy(
                src_ref=rows(*block) if src is None else src, dst_ref=rows(*block),
                send_sem=send_sems.at[k], recv_sem=recv_sems.at[k], device_id=to, device_id_type=MESH)

        first = [copy(0, me, sibling, src=x_ref)]
        first += [copy(1 + j, me, (*chip, c), src=x_ref) for j, chip in enumerate(chips)]
        for cp in first:
            cp.start()
        passed = [copy(4 + j, (*chip, c), sibling) for j, chip in enumerate(chips)]
        for j, chip in enumerate(chips):
            copy(1 + j, (*chip, c), me).wait_recv()
            passed[j].start()
        copy(0, sibling, me).wait_recv()
        for j, chip in enumerate(chips):
            copy(4 + j, (*chip, 1 - c), me).wait_recv()
        for cp in first + passed:
            cp.wait_send()

    out = pl.pallas_call(
        body, name=name, out_shape=jax.ShapeDtypeStruct((N_DEV * m_per, n), x_shard.dtype),
        in_specs=[pl.BlockSpec(memory_space=pl.ANY)], out_specs=pl.BlockSpec(memory_space=pl.ANY),
        scratch_shapes=[pltpu.SemaphoreType.DMA((7,)), pltpu.SemaphoreType.DMA((7,))],
    )(x_shard)
    my_dev = 4 * lax.axis_index("x") + 2 * lax.axis_index("y") + lax.axis_index("c")
    return lax.dynamic_update_slice(out, x_shard, (my_dev * m_per, 0))


def _put_own(out, own, index):
    start = tuple(index) + (0,) * own.ndim
    return lax.dynamic_update_slice(out, own.reshape((1,) * len(index) + own.shape), start)


def _gather_copies(stage, ins, outs, send_sems, recv_sems):
    x, y, c = lax.axis_index("x"), lax.axis_index("y"), lax.axis_index("c")
    my_chip = 2 * x + y
    copies = []
    for w, out in enumerate(outs):
        half = out.shape[1] // 2
        rows = pl.ds(c * half, half)
        for k, (cx, cy) in enumerate([(1 - x, y), (x, 1 - y), (1 - x, 1 - y)]):
            if stage == 0:
                src, dst, to = ins[w].at[rows], out.at[my_chip, rows], (cx, cy, c)
            else:
                src = dst = out.at[2 * cx + cy, rows]
                to = (x, y, 1 - c)
            copies.append(pltpu.make_async_remote_copy(
                src_ref=src, dst_ref=dst, send_sem=send_sems.at[3 * w + k], recv_sem=recv_sems.at[3 * w + k],
                device_id=to, device_id_type=MESH))
    return copies


def _gathered_shapes(shards):
    return [jax.ShapeDtypeStruct((N_CHIPS,) + s.shape, s.dtype) for s in shards]


def _put_own_slabs(gathered, shards):
    my_chip = 2 * lax.axis_index("x") + lax.axis_index("y")
    return [_put_own(o, s, (my_chip,)) for o, s in zip(gathered, shards)]


def _gather_layer(name, shards):
    n_w = len(shards)

    def body(*refs):
        ins, outs = refs[:n_w], refs[n_w:2 * n_w]
        for stage in (0, 1):
            copies = _gather_copies(stage, ins, outs, refs[2 * n_w + 2 * stage], refs[2 * n_w + 2 * stage + 1])
            for cp in copies:
                cp.start()
            for cp in copies:
                cp.wait()

    outs = pl.pallas_call(
        body, name=name, out_shape=_gathered_shapes(shards),
        in_specs=[pl.BlockSpec(memory_space=pl.ANY)] * n_w, out_specs=[pl.BlockSpec(memory_space=pl.ANY)] * n_w,
        scratch_shapes=[pltpu.SemaphoreType.DMA((3 * n_w,))] * 4,
    )(*shards)
    return _put_own_slabs(outs, shards)


def _gather_side_jobs(shards):
    n_w = len(shards)
    between_chips = _SideJob(list(shards), _gathered_shapes(shards), {}, 3 * n_w,
                             lambda ins, outs, send, recv: _gather_copies(0, ins, outs, send, recv))
    between_cores = lambda partial: _SideJob(list(partial), _gathered_shapes(shards), {w: w for w in range(n_w)}, 3 * n_w,
                                             lambda ins, outs, send, recv: _gather_copies(1, ins, outs, send, recv))
    return between_chips, between_cores


def _run_job(name, job):
    n_in, n_out = len(job.arrays), len(job.out_shapes)

    def body(*refs):
        copies = job.copies(refs[:n_in], refs[n_in:n_in + n_out], refs[n_in + n_out], refs[n_in + n_out + 1])
        for cp in copies:
            cp.start()
        for cp in copies:
            cp.wait()

    hbm = pl.BlockSpec(memory_space=pl.ANY)
    return pl.pallas_call(
        body, name=name, out_shape=list(job.out_shapes), in_specs=[hbm] * n_in, out_specs=[hbm] * n_out,
        scratch_shapes=[pltpu.SemaphoreType.DMA((job.n_sems,))] * 2, input_output_aliases=dict(job.aliases),
    )(*job.arrays)


def _swap_job(grads):
    def copies(ins, outs, send_sems, recv_sems):
        x, y, c = lax.axis_index("x"), lax.axis_index("y"), lax.axis_index("c")
        return [pltpu.make_async_remote_copy(
            src_ref=g.at[:, pl.ds((1 - c) * (g.shape[1] // 2), g.shape[1] // 2)], dst_ref=outs[w],
            send_sem=send_sems.at[w], recv_sem=recv_sems.at[w], device_id=(x, y, 1 - c), device_id_type=MESH)
            for w, g in enumerate(ins)]

    shapes = [jax.ShapeDtypeStruct((g.shape[0], g.shape[1] // 2, g.shape[2]), g.dtype) for g in grads]
    return _SideJob(list(grads), shapes, {}, len(grads), copies)


def _exchange_job(parts):
    def copies(ins, outs, send_sems, recv_sems):
        x, y, c = lax.axis_index("x"), lax.axis_index("y"), lax.axis_index("c")
        return [pltpu.make_async_remote_copy(
            src_ref=ins[w].at[2 * cx + cy], dst_ref=outs[w].at[2 * x + y], send_sem=send_sems.at[3 * w + k],
            recv_sem=recv_sems.at[3 * w + k], device_id=(cx, cy, c), device_id_type=MESH)
            for w in range(len(ins)) for k, (cx, cy) in enumerate([(1 - x, y), (x, 1 - y), (1 - x, 1 - y)])]

    return _SideJob(list(parts), [jax.ShapeDtypeStruct(p.shape, p.dtype) for p in parts], {}, 3 * len(parts), copies)


def _share_job(bufs, layer):
    def copies(ins, outs, send_sems, recv_sems):
        x, y, c = lax.axis_index("x"), lax.axis_index("y"), lax.axis_index("c")
        mine = [o.at[layer, pl.ds(c * (o.shape[1] // 2), o.shape[1] // 2)] for o in outs]
        return [pltpu.make_async_remote_copy(src_ref=rows, dst_ref=rows, send_sem=send_sems.at[w], recv_sem=recv_sems.at[w],
                                             device_id=(x, y, 1 - c), device_id_type=MESH) for w, rows in enumerate(mine)]

    return _SideJob(list(bufs), [jax.ShapeDtypeStruct(b.shape, b.dtype) for b in bufs], {w: w for w in range(len(bufs))},
                    len(bufs), copies)


def _sum_into(name, blocks, core, layer, depth, into):
    n, r, c = blocks.shape
    tr = _pick8(r, max(BF16_ROWS, (1 << 19) // (4 * c)), BF16_ROWS)
    steps = r // tr

    def body(core_ref, x_ref, *rest):
        acc = x_ref[0].astype(F32)
        for i in range(1, n):
            acc = acc + x_ref[i].astype(F32)
        rest[-1][...] = acc

    grid_spec = pltpu.PrefetchScalarGridSpec(
        num_scalar_prefetch=1, grid=(steps,),
        in_specs=[pl.BlockSpec((n, tr, c), lambda i, core_ref: (0, i, 0))]
        + ([pl.BlockSpec(memory_space=pl.ANY)] if into is not None else []),
        out_specs=pl.BlockSpec((None, tr, c), lambda i, core_ref: (layer, core_ref[0] * steps + i, 0)))
    return pl.pallas_call(
        body, name=name, grid_spec=grid_spec, out_shape=jax.ShapeDtypeStruct((depth, 2 * r, c), F32),
        input_output_aliases={2: 0} if into is not None else {}, compiler_params=_cparams(),
    )(core, blocks, *([into] if into is not None else []))


def _add_rows(name, grads, recv, core):
    n, r, c = recv.shape
    tr = _pick8(r, max(BF16_ROWS, (1 << 19) // (4 * c)), BF16_ROWS)
    steps = r // tr

    def body(core_ref, g_ref, r_ref, o_ref):
        o_ref[...] = (g_ref[...].astype(F32) + r_ref[...].astype(F32)).astype(BF16)

    grid_spec = pltpu.PrefetchScalarGridSpec(
        num_scalar_prefetch=1, grid=(steps,),
        in_specs=[pl.BlockSpec((n, tr, c), lambda i, core_ref: (0, core_ref[0] * steps + i, 0)),
                  pl.BlockSpec((n, tr, c), lambda i, core_ref: (0, i, 0))],
        out_specs=pl.BlockSpec((n, tr, c), lambda i, core_ref: (0, i, 0)))
    return pl.pallas_call(body, name=name, grid_spec=grid_spec,
                          out_shape=jax.ShapeDtypeStruct((n, r, c), BF16), compiler_params=_cparams())(core, grads, recv)


class _LayerReduce:
    def __init__(self, tag, layer, depth, grads, core, into):
        self.tag, self.layer, self.depth, self.core, self.into = tag, layer, depth, core, into
        self.state = list(grads)

    def _exchange(self, name, job, carry):
        if carry is None:
            return None, _run_job(f"{name}_{self.tag}", job)
        return carry(job)

    def swap_and_add(self, carry=None):
        grads = self.state
        results, recv = self._exchange("grads_swap_cores", _swap_job(grads), carry)
        self.state = [_add_rows(f"grads_add_{n}_{self.tag}", g, r, self.core) for n, g, r in zip(BIG, grads, recv)]
        return results

    def exchange_and_sum(self, carry=None):
        parts = self.state
        results, arrived = self._exchange("grads_exchange_chips", _exchange_job(parts), carry)
        my_chip = 2 * lax.axis_index("x") + lax.axis_index("y")
        arrived = [_put_own(a, lax.dynamic_index_in_dim(p, my_chip, 0, keepdims=False), (my_chip,))
                   for a, p in zip(arrived, parts)]
        into = self.into if self.into is not None else [None] * len(arrived)
        self.state = [_sum_into(f"grads_sum_{n}_{self.tag}", a, self.core, self.layer, self.depth, b)
                      for n, a, b in zip(BIG, arrived, into)]
        return results

    def share(self, carry=None):
        results, self.state = self._exchange("grads_share_cores", _share_job(self.state, self.layer), carry)
        return results


def _pack(arrays, cols, row_multiple, dtype):
    flat = jnp.concatenate([a.reshape(-1).astype(dtype) for a in arrays])
    unit = cols * row_multiple
    total = -(-flat.shape[0] // unit) * unit
    return jnp.pad(flat, (0, total - flat.shape[0])).reshape(total // cols, cols)


def _unpack(buf, shapes):
    flat, out, off = buf.reshape(-1), [], 0
    for sh in shapes:
        n = math.prod(sh)
        out.append(flat[off:off + n].reshape(sh))
        off += n
    return out


def _discretize(lam_re, lam_im, log_dt, b_re, b_im):
    lam = lax.complex(jnp.minimum(lam_re, -EIG_CLIP), lam_im)
    dt = jnp.exp(log_dt)[:, None]
    lam_bar = jnp.exp(lam * dt)
    b_bar = ((lam_bar - 1.0) / lam)[..., None] * lax.complex(b_re, b_im)
    return jnp.real(lam_bar), jnp.imag(lam_bar), jnp.real(b_bar), jnp.imag(b_bar)


def _scan_tables(ar, ai):
    a = lax.complex(ar, ai)
    pw = [a]
    for _ in range(7):
        pw.append(pw[-1] * a)
    rows = jnp.arange(SUBLANES)[:, None]

    def build(p, reverse):
        tabs = []
        for k in (1, 2, 4):
            keep = (rows <= SUBLANES - 1 - k) if reverse else (rows >= k)
            tk = jnp.where(keep, p[k - 1][None, :], 0.0)
            tabs += [jnp.real(tk), jnp.imag(tk)]
        stack = jnp.stack(p[::-1] if reverse else p)
        tabs += [jnp.real(stack), jnp.imag(stack)]
        return jnp.stack(tabs).astype(F32)

    return build(pw, False), build([jnp.conj(p) for p in pw], True)


def _interleave_rows(a, t):
    s, w = a.shape
    return a.reshape(s // t, SUBLANES, t // SUBLANES, w).transpose(0, 2, 1, 3).reshape(s, w)


def _deinterleave_rows(a, t):
    s, w = a.shape
    return a.reshape(s // t, t // SUBLANES, SUBLANES, w).transpose(0, 2, 1, 3).reshape(s, w)


def _block_diag(per_group, groups_per_block):
    g, a, b = per_group.shape
    x = per_group.reshape(g // groups_per_block, groups_per_block, a, b)
    eye = jnp.eye(groups_per_block, dtype=per_group.dtype)
    out = x[:, :, :, None, :] * eye[None, :, None, :, None]
    return out.reshape(g // groups_per_block, groups_per_block * a, groups_per_block * b)


def _block_diag_extract(dense, groups_per_block, a, b):
    nkb = dense.shape[0]
    x = dense.reshape(nkb, groups_per_block, a, groups_per_block, b)
    idx = jnp.arange(groups_per_block)
    return x[:, idx, :, idx, :].transpose(1, 0, 2, 3).reshape(nkb * groups_per_block, a, b)


def _layer_fwd(tag, x, mod, p, wts, gather_next=None):
    s, d = x.shape
    w_ssm, w_att = p["w_glu"].shape[0], wts["w_pb"].shape[1]
    heads = p["b_f"].shape[0]
    dh = w_att // heads
    cs = d // N_CHIPS
    fs = wts["w_ffn_down"].shape[1]
    tm = _pick(s, 1024)
    row = lambda v: v.reshape(1, -1)
    sv = {}

    h = _prenorm_fwd(f"prenorm_mix_{tag}", x, row(p["g_pre_mix"]), row(mod[1]), row(mod[0]))
    uqkv = _mm_plain(f"proj_main_{tag}", h, p["w_main"], "nn", BF16, tm=1024, tn=1024, tk=1024)
    fg = _mm_plain(f"proj_gate_{tag}", h, p["w_gates"], "nn", F32, tm=1024, tn=1024, tk=1024)
    f_t = fg[:, 2 * d:2 * d + heads].T

    t5 = min(S5_ROWS, s)
    u_il = _interleave_rows(uqkv[:, :w_ssm], t5)
    y_s5, ys_il, carries = _s5_fwd(f"s5_fwd_{tag}", u_il, p["b_blk"], p["c_blk"], p["a_f"], p["tab_f"],
                                   row(p["d_skip"]), p["w_glu"], row(p["b_glu"]))
    ys = _deinterleave_rows(ys_il, t5)

    assert dh * 2 == LANES and w_ssm % LANES == 0 and w_att % LANES == 0
    n_pairs = w_att // LANES
    blocks = (w_ssm // LANES, w_ssm // LANES + n_pairs, w_ssm // LANES + 2 * n_pairs)
    cum = _cum_fwd(f"cum_fwd_{tag}", f_t, p["b_f"].reshape(heads, 1))
    t = min(ATT_BLOCK, s)
    ck_cols, ck_rows = cum.reshape(heads, s, 1), cum.reshape(heads, s // t, 1, t)
    (ya, lse), arrived = _attn_fwd(f"attn_fwd_{tag}", uqkv, *blocks, n_pairs, ck_rows,
                                   side=gather_next[0] if gather_next else None)

    tile = pl.BlockSpec((tm, cs), lambda i, j, k: (i, j))
    slab = lambda rows: pl.BlockSpec((None, rows, cs), lambda i, j, k: (j, 0, 0))

    def merge(acc, extra_refs, out_refs):
        ya_ref, wpb_ref, ga_ref, gb_ref = extra_refs
        a_ref, b_ref, m_ref = out_refs
        bv = _dot(ya_ref[...], wpb_ref[...], NN)
        a_ref[...] = acc.astype(BF16)
        b_ref[...] = bv.astype(BF16)
        m_ref[...] = (_sigmoid(ga_ref[...]) * acc + _sigmoid(gb_ref[...]) * bv).astype(BF16)

    sd_bf = jax.ShapeDtypeStruct((s, d), BF16)
    pa, pb, merged = _mm_raw(
        f"merge_{tag}", ys, wts["w_pa"], "nn", (s // tm, N_CHIPS, 1), (tm, cs),
        pl.BlockSpec((tm, w_ssm), lambda i, j, k: (i, 0)), slab(w_ssm), [sd_bf] * 3, [tile] * 3, merge,
        extra=(ya, wts["w_pb"], fg, fg),
        extra_specs=[pl.BlockSpec((tm, w_att), lambda i, j, k: (i, 0)), slab(w_att), tile,
                     pl.BlockSpec((tm, cs), lambda i, j, k: (i, j + N_CHIPS))])

    tm2 = _pick(s, POSTNORM_ROWS)
    x1, y_mix = _mm_postnorm(
        f"out_proj_{tag}", merged, pl.BlockSpec((tm2, cs), lambda i, j, k: (i, k)), wts["w_o"],
        pl.BlockSpec((None, cs, d), lambda i, j, k: (k, 0, 0)), N_CHIPS, x, row(mod[2]), row(p["g_post_mix"]))

    h2 = _prenorm_fwd(f"prenorm_ffn_{tag}", x1, row(p["g_pre_ffn"]), row(mod[4]), row(mod[3]))
    (a4, b4, hid4), next_wts = _ffn_up(f"ffn_up_{tag}", h2, wts["w_ffn_gate"], wts["w_ffn_up"],
                                       side=gather_next[1](arrived) if gather_next else None)
    x2, y_ffn = _mm_postnorm(
        f"ffn_down_{tag}", hid4, pl.BlockSpec((None, tm2, fs), lambda i, j, k: (k, i, 0)), wts["w_ffn_down"],
        pl.BlockSpec((None, fs, d), lambda i, j, k: (k, 0, 0)), N_CHIPS, x1, row(mod[5]), row(p["g_post_ffn"]))

    sv.update(x=x, h=h, uqkv=uqkv, u_il=u_il, fg=fg, f_t=f_t, y_s5=y_s5, ys=ys, carries=carries, blocks=blocks,
              ck_cols=ck_cols, lse_rows=lse.reshape(heads, s // t, 1, t), ya=ya, pa=pa, pb=pb, merged=merged, x1=x1,
              y_mix=y_mix, h2=h2, a4=a4, b4=b4, hid4=hid4, y_ffn=y_ffn)
    return x2, sv, next_wts


def _mm_postnorm(name, a, a_spec, w, w_spec, nk, x, gate, g):
    s, d = x.shape
    tm = _pick(s, POSTNORM_ROWS)
    rowspec = pl.BlockSpec((tm, d), lambda i, j, k: (i, 0))
    vec = pl.BlockSpec((1, d), lambda i, j, k: (0, 0))

    def epilogue(acc, extra_refs, out_refs):
        x_ref, gate_ref, g_ref = extra_refs
        r = lax.rsqrt(jnp.mean(acc * acc, axis=-1, keepdims=True) + RMS_EPS)
        out_refs[0][...] = x_ref[...] + gate_ref[...] * (acc * r * g_ref[...])
        out_refs[1][...] = acc

    sd = jax.ShapeDtypeStruct((s, d), F32)
    return _mm_raw(name, a, w, "nn", (s // tm, 1, nk), (tm, d), a_spec, w_spec, [sd, sd], [rowspec, rowspec], epilogue,
                   extra=(x, gate, g), extra_specs=[rowspec, vec, vec])


def _layer_bwd(tag, dx2, mod, p, wts, sv, reduce_later=None):
    s, d = dx2.shape
    w_ssm, w_att = p["w_glu"].shape[0], wts["w_pb"].shape[1]
    heads = p["b_f"].shape[0]
    cs = d // N_CHIPS
    fs = wts["w_ffn_down"].shape[1]
    tm, tk, td = _pick(s, 1024), _pick(s, 1024), d
    row = lambda v: v.reshape(1, -1)
    gr = {}

    def dw_slabs(name, act, act_spec, rows, dy, dy_spec, cols, grid_mn, out_index):
        return _mm_raw(name, act, dy, "tn", grid_mn + (s // tk,), (rows, cols), act_spec, dy_spec,
                       [jax.ShapeDtypeStruct((N_CHIPS,) + out_index[1], BF16)],
                       [pl.BlockSpec((None, rows, cols), out_index[0])], _store(BF16))[0]

    dy_ffn, sums = _postnorm_bwd(f"postnorm_bwd_ffn_{tag}", dx2, sv["y_ffn"], row(p["g_post_ffn"]), row(mod[5]))
    d_gate_f, gr["g_post_ffn"] = sums[0], sums[1]
    gr["w_ffn_down"] = dw_slabs(f"dw_down_{tag}", sv["hid4"], pl.BlockSpec((None, tk, fs), lambda i, j, k: (i, k, 0)), fs,
                                dy_ffn, pl.BlockSpec((tk, d), lambda i, j, k: (k, 0)), d, (N_CHIPS, 1),
                                (lambda i, j, k: (i, 0, 0), (fs, d)))

    def swiglu_bwd(acc, extra_refs, out_refs):
        av, bv = extra_refs[0][...].astype(F32), extra_refs[1][...].astype(F32)
        sg = _sigmoid(av)
        out_refs[0][...] = (acc * bv * (sg * (1.0 + av * (1.0 - sg)))).astype(BF16)
        out_refs[1][...] = (acc * (av * sg)).astype(BF16)

    blk4 = pl.BlockSpec((None, tm, fs), lambda i, j, k: (j, i, 0))
    sh4 = jax.ShapeDtypeStruct((N_CHIPS, s, fs), BF16)
    ffn_down_bwd = lambda side: _mm_raw(
        f"ffn_down_bwd_{tag}", dy_ffn, wts["w_ffn_down"], "nt", (s // tm, N_CHIPS, 1), (tm, fs),
        pl.BlockSpec((tm, d), lambda i, j, k: (i, 0)), pl.BlockSpec((None, fs, d), lambda i, j, k: (j, 0, 0)),
        [sh4, sh4], [blk4, blk4], swiglu_bwd, extra=(sv["a4"], sv["b4"]), extra_specs=[blk4, blk4], side=side)
    da4, db4 = reduce_later.swap_and_add(ffn_down_bwd) if reduce_later else ffn_down_bwd(None)
    for n, act4 in (("w_ffn_gate", da4), ("w_ffn_up", db4)):
        gr[n] = dw_slabs(f"d{n}_{tag}", sv["h2"], pl.BlockSpec((tk, td), lambda i, j, k: (k, i)), td,
                         act4, pl.BlockSpec((None, tk, fs), lambda i, j, k: (j, k, 0)), fs, (d // td, N_CHIPS),
                         (lambda i, j, k: (j, i, 0), (d, fs)))
    pairs = [(act4, (None, tm, fs), lambda i, kk: (kk, i, 0), wts[n], (None, td, fs), lambda j, kk: (kk, j, 0),
              N_CHIPS) for n, act4 in (("w_ffn_gate", da4), ("w_ffn_up", db4))]
    dh2 = _mm_sum(f"dh_ffn_{tag}", s, d, tm, td, pairs, F32)
    dx1, sums = _prenorm_bwd(f"prenorm_bwd_ffn_{tag}", dh2, sv["x1"], row(p["g_pre_ffn"]), row(mod[4]), dx2)
    d_scale_f, d_shift_f, gr["g_pre_ffn"] = sums[0], sums[1], sums[2]

    dy_mix, sums = _postnorm_bwd(f"postnorm_bwd_mix_{tag}", dx1, sv["y_mix"], row(p["g_post_mix"]), row(mod[2]))
    d_gate_m, gr["g_post_mix"] = sums[0], sums[1]
    gr["w_o"] = dw_slabs(f"dw_o_{tag}", sv["merged"], pl.BlockSpec((tk, cs), lambda i, j, k: (k, i)), cs,
                         dy_mix, pl.BlockSpec((tk, d), lambda i, j, k: (k, 0)), d, (N_CHIPS, 1),
                         (lambda i, j, k: (i, 0, 0), (cs, d)))

    tile = pl.BlockSpec((tm, cs), lambda i, j, k: (i, j))

    def merge_bwd(acc, extra_refs, out_refs):
        a_ref, b_ref, ga_ref, gb_ref = extra_refs
        sa, sb = _sigmoid(ga_ref[...]), _sigmoid(gb_ref[...])
        out_refs[0][...] = (acc * sa).astype(BF16)
        out_refs[1][...] = (acc * sb).astype(BF16)
        out_refs[2][...] = (acc * a_ref[...].astype(F32) * sa * (1.0 - sa)).astype(BF16)
        out_refs[3][...] = (acc * b_ref[...].astype(F32) * sb * (1.0 - sb)).astype(BF16)

    sd_bf = jax.ShapeDtypeStruct((s, d), BF16)
    d_pa, d_pb, d_ga, d_gb = _mm_raw(
        f"out_proj_bwd_{tag}", dy_mix, wts["w_o"], "nt", (s // tm, N_CHIPS, 1), (tm, cs),
        pl.BlockSpec((tm, d), lambda i, j, k: (i, 0)), pl.BlockSpec((None, cs, d), lambda i, j, k: (j, 0, 0)),
        [sd_bf] * 4, [tile] * 4, merge_bwd, extra=(sv["pa"], sv["pb"], sv["fg"], sv["fg"]),
        extra_specs=[tile, tile, tile, pl.BlockSpec((tm, cs), lambda i, j, k: (i, j + N_CHIPS))])
    d_branch = {}
    for n, act, width, d_p in (("w_pa", sv["ys"], w_ssm, d_pa), ("w_pb", sv["ya"], w_att, d_pb)):
        gr[n] = dw_slabs(f"d{n}_{tag}", act, pl.BlockSpec((tk, width), lambda i, j, k: (k, 0)), width,
                         d_p, pl.BlockSpec((tk, cs), lambda i, j, k: (k, j)), cs, (1, N_CHIPS),
                         (lambda i, j, k: (j, 0, 0), (width, cs)))
        d_branch[n] = _mm_raw(
            f"d_in_{n}_{tag}", d_p, wts[n], "nt", (s // tm, 1, N_CHIPS), (tm, width),
            pl.BlockSpec((tm, cs), lambda i, j, k: (i, k)), pl.BlockSpec((None, width, cs), lambda i, j, k: (k, 0, 0)),
            [jax.ShapeDtypeStruct((s, width), BF16)], [pl.BlockSpec((tm, width), lambda i, j, k: (i, 0))], _store(BF16))[0]
    d_ys, d_ya = d_branch["w_pa"], d_branch["w_pb"]

    attn_bwd = lambda side: _attn_bwd(f"attn_bwd_{tag}", sv["uqkv"], *sv["blocks"], w_att // LANES, sv["ya"], d_ya,
                                      sv["lse_rows"], sv["ck_cols"], side=side)
    dq, dk, dv, dcq, dck = reduce_later.exchange_and_sum(attn_bwd) if reduce_later else attn_bwd(None)[0]
    d_f_t, d_bf = _cum_bwd(f"cum_bwd_{tag}", dcq.reshape(heads, s), dck.reshape(heads, s), sv["f_t"],
                           p["b_f"].reshape(heads, 1))
    gr["b_f"] = d_bf[:, 0]

    t5 = min(S5_ROWS, s)
    du_il, d_bblk, d_cblk, d_abar, d_wglu, vec = _s5_bwd(
        f"s5_bwd_{tag}", sv["u_il"], _interleave_rows(d_ys, t5), sv["y_s5"], sv["carries"], p["b_blk"], p["c_blk"],
        p["a_f"], p["a_r"], p["tab_f"], p["tab_r"], row(p["d_skip"]), p["w_glu"], row(p["b_glu"]))
    du = _deinterleave_rows(du_il, t5)
    gr["w_glu"] = d_wglu.astype(BF16).reshape(N_CHIPS, w_ssm // N_CHIPS, w_ssm)
    gr["b_glu"], gr["d_skip"] = vec[0], vec[1]
    gr["b_blk"], gr["c_blk"], gr["a_bar"] = d_bblk, d_cblk, d_abar

    d_f = jnp.pad(d_f_t.T, ((0, 0), (0, F_PAD - heads))).astype(BF16)
    assert w_ssm % w_att == 0 and (2 * d) % F_PAD == 0
    first = w_ssm // w_att
    main_pieces = [(du, w_ssm, 0), (dq, w_att, first), (dk, w_att, first + 1), (dv, w_att, first + 2)]
    dw = [_mm_plain(f"dw_in{n}_{tag}", sv["h"], piece, "tn", BF16, tm=1024, tn=1024, tk=1024)
          for n, piece in enumerate([du, dq, dk, dv, d_f, d_ga, d_gb])]
    w_in_grad = jnp.concatenate(dw[:4] + [dw[4][:, :heads], dw[5], dw[6]], axis=1)
    gr["w_in"] = w_in_grad.reshape(d, N_CHIPS, w_in_grad.shape[1] // N_CHIPS).transpose(1, 0, 2)
    tmx, tkx = _pick(s, 1024), _pick(d, 512)
    pairs = [(piece, (tmx, width), lambda i, kk: (i, 0), p["w_main"], (d, width), lambda j, kk, blk=blk: (j, blk), 1)
             for piece, width, blk in main_pieces]
    steps = d // tkx
    pairs += [(piece, (tmx, tkx), lambda i, kk: (i, kk), p["w_gates"], (d, tkx), lambda j, kk, off=off: (j, off + kk), steps)
              for piece, off in ((d_ga, 0), (d_gb, steps))]
    pairs.append((d_f, (tmx, F_PAD), lambda i, kk: (i, 0), p["w_gates"], (d, F_PAD), lambda j, kk: (j, 2 * d // F_PAD), 1))
    dh_mix = lambda side: _mm_sum(f"dh_mix_{tag}", s, d, tmx, d, pairs, F32, side=side)
    dh1 = reduce_later.share(dh_mix) if reduce_later else dh_mix(None)
    dx0, sums = _prenorm_bwd(f"prenorm_bwd_mix_{tag}", dh1, sv["x"], row(p["g_pre_mix"]), row(mod[1]), dx1)
    d_scale_m, d_shift_m, gr["g_pre_mix"] = sums[0], sums[1], sums[2]

    d_mod = jnp.stack([d_shift_m, d_scale_m, d_gate_m, d_shift_f, d_scale_f, d_gate_f])
    return dx0, d_mod, gr


BIG = ("w_in", "w_glu", "w_pa", "w_pb", "w_o", "w_ffn_gate", "w_ffn_up", "w_ffn_down")
SMALL = ("b_ada", "g_pre_mix", "g_post_mix", "g_pre_ffn", "g_post_ffn", "lam_re", "lam_im", "log_dt", "b_re", "b_im",
         "c_re", "c_im", "d_skip", "b_glu", "b_f")
WEIGHTS = ("w_ada", "b_ada", "g_pre_mix", "g_post_mix", "g_pre_ffn", "g_post_ffn", "w_in", "lam_re", "lam_im", "log_dt",
           "b_re", "b_im", "c_re", "c_im", "d_skip", "w_glu", "b_glu", "b_f", "w_pa", "w_pb", "w_o", "w_ffn_gate",
           "w_ffn_up", "w_ffn_down")


def _prepare_layer(wts, small, l, seq):
    w_in = jnp.concatenate([wts["w_in"][j] for j in range(N_CHIPS)], axis=1)
    d = w_in.shape[0]
    heads = small["b_f"].shape[1]
    n_groups, n_state, group_ch = small["b_re"].shape[1:]
    w_ssm = n_groups * group_ch
    w_att = wts["w_pb"].shape[1]
    n_main = w_ssm + 3 * w_att
    gpb = LANES // group_ch
    p = {}
    p["w_main"] = w_in[:, :n_main]
    p["w_gates"] = jnp.concatenate(
        [w_in[:, n_main + heads:], w_in[:, n_main:n_main + heads], jnp.zeros((d, F_PAD - heads), BF16)], axis=1)
    p["w_glu"] = wts["w_glu"].reshape(w_ssm, w_ssm)
    for n in ("g_pre_mix", "g_post_mix", "g_pre_ffn", "g_post_ffn", "d_skip", "b_glu", "b_f"):
        p[n] = small[n][l]
    ar, ai, br, bi = _discretize(small["lam_re"][l], small["lam_im"][l], small["log_dt"][l], small["b_re"][l], small["b_im"][l])
    n_steps = min(S5_ROWS, seq) // SUBLANES
    powers = jnp.cumprod(jnp.broadcast_to(lax.complex(ar, ai).reshape(1, -1), (n_steps, ar.size)), axis=0)
    p["a_f"] = jnp.concatenate([jnp.real(powers), jnp.imag(powers)], axis=1)
    p["a_r"] = jnp.concatenate([jnp.real(powers[::-1]), -jnp.imag(powers[::-1])], axis=1)
    p["tab_f"], p["tab_r"] = _scan_tables(jnp.real(powers[-1]), jnp.imag(powers[-1]))
    bre = _block_diag(br.transpose(0, 2, 1), gpb)
    bim = _block_diag(bi.transpose(0, 2, 1), gpb)
    p["b_blk"] = jnp.concatenate([bre, bim], axis=2).astype(BF16)
    cre = _block_diag(small["c_re"][l].transpose(0, 2, 1), gpb)
    cim = _block_diag(small["c_im"][l].transpose(0, 2, 1), gpb)
    p["c_blk"] = jnp.concatenate([cre, -cim], axis=1).astype(BF16)
    return p


def _compact_partials(gr, n_state, group_ch):
    gpb = LANES // group_ch
    half = gpb * n_state
    out = dict(gr)
    out["bbar_re"] = _block_diag_extract(gr["b_blk"][:, :, :half], gpb, group_ch, n_state).transpose(0, 2, 1)
    out["bbar_im"] = _block_diag_extract(gr["b_blk"][:, :, half:], gpb, group_ch, n_state).transpose(0, 2, 1)
    out["c_re"] = _block_diag_extract(gr["c_blk"][:, :half, :], gpb, n_state, group_ch).transpose(0, 2, 1)
    out["c_im"] = -_block_diag_extract(gr["c_blk"][:, half:, :], gpb, n_state, group_ch).transpose(0, 2, 1)
    return out


def _small_grads_from_partials(gr, small, l):
    n_groups, n_state, _ = small["b_re"].shape[1:]
    ns2 = n_groups * n_state
    d_abar = jnp.sum(gr["a_bar"], axis=0)
    dar, dai = d_abar[:ns2].reshape(n_groups, n_state), d_abar[ns2:].reshape(n_groups, n_state)
    args = (small["lam_re"][l], small["lam_im"][l], small["log_dt"][l], small["b_re"][l], small["b_im"][l])
    _, vjp = jax.vjp(_discretize, *args)
    d_lam_re, d_lam_im, d_log_dt, d_b_re, d_b_im = vjp((dar, dai, gr["bbar_re"], gr["bbar_im"]))
    return dict(lam_re=d_lam_re, lam_im=d_lam_im, log_dt=d_log_dt, b_re=d_b_re, b_im=d_b_im,
                c_re=gr["c_re"], c_im=gr["c_im"])


def _fwd_bwd(xs, target, mods, small, wts0, later, core=None):
    depth = 1 + len(later)
    saved, layers, wts = [], [], [wts0]
    act = xs
    for l in range(depth):
        layers.append(_prepare_layer(wts[l], small, l, xs.shape[0]))
        shards = later[l] if l + 1 < depth and not isinstance(later[l], dict) else None
        act, sv, gathered = _layer_fwd(str(l), act, mods[l], layers[l], wts[l],
                                       gather_next=_gather_side_jobs(shards) if shards is not None else None)
        saved.append(sv)
        if l + 1 < depth:
            wts.append(dict(zip(BIG, _put_own_slabs(gathered, shards))) if shards is not None else later[l])
    dx, loss_blk = _loss_grad("loss", act, target)
    grads, d_mods = [None] * depth, [None] * depth
    pending = None
    for l in reversed(range(depth)):
        dx, d_mods[l], grads[l] = _layer_bwd(str(l), dx, mods[l], layers[l], wts[l], saved[l], reduce_later=pending)
        if core is not None:
            pending = _LayerReduce(str(l), l, depth, [grads[l][n] for n in BIG], core,
                                   into=pending.state if pending is not None else None)
    if core is None:
        return loss_blk, dx, d_mods, grads, None
    pending.swap_and_add()
    pending.exchange_and_sum()
    pending.share()
    return loss_blk, dx, d_mods, grads, dict(zip(BIG, pending.state))


def kernel(x, c, w_ada, b_ada, g_pre_mix, g_post_mix, g_pre_ffn, g_post_ffn, w_in, lam_re, lam_im, log_dt, b_re, b_im, c_re, c_im, d_skip, w_glu, b_glu, b_f, w_pa, w_pb, w_o, w_ffn_gate, w_ffn_up, w_ffn_down, loss_target, m_w_ada, m_b_ada, m_g_pre_mix, m_g_post_mix, m_g_pre_ffn, m_g_post_ffn, m_w_in, m_lam_re, m_lam_im, m_log_dt, m_b_re, m_b_im, m_c_re, m_c_im, m_d_skip, m_w_glu, m_b_glu, m_b_f, m_w_pa, m_w_pb, m_w_o, m_w_ffn_gate, m_w_ffn_up, m_w_ffn_down, v_w_ada, v_b_ada, v_g_pre_mix, v_g_post_mix, v_g_pre_ffn, v_g_post_ffn, v_w_in, v_lam_re, v_lam_im, v_log_dt, v_b_re, v_b_im, v_c_re, v_c_im, v_d_skip, v_w_glu, v_b_glu, v_b_f, v_w_pa, v_w_pb, v_w_o, v_w_ffn_gate, v_w_ffn_up, v_w_ffn_down):
    local = dict(locals())
    weights = {n: local[n] for n in WEIGHTS}
    moments_m = {n: local["m_" + n] for n in WEIGHTS}
    moments_v = {n: local["v_" + n] for n in WEIGHTS}
    depth, d = g_pre_mix.shape
    n_mod = w_ada.shape[2] * N_CHIPS // d
    mx, my, mc = lax.axis_index("x"), lax.axis_index("y"), lax.axis_index("c")
    my_chip = 2 * mx + my
    my_dev = 4 * mx + 2 * my + mc
    xs = x[0]

    shards = [[weights[n][l].astype(BF16) for n in BIG] for l in range(depth)]
    wts0 = dict(zip(BIG, _gather_layer("gather_weights_0", shards[0])))
    small = {n: weights[n] for n in SMALL}

    c_pad = jnp.pad(c, ((0, SUBLANES - 1), (0, 0)))
    c_all = _all_gather("gather_cond", c_pad).reshape(N_DEV, SUBLANES, d)[:, 0, :]
    silu = lambda v: v * _sigmoid(v)
    n_cols = w_ada.shape[2]
    mod_shard = []
    for l in range(depth):
        bias = lax.dynamic_slice_in_dim(b_ada[l], my_chip * n_cols, n_cols)
        mod_shard.append(_mm_plain(f"ada_{l}", c_all, w_ada[l], "nn", F32, add=jnp.broadcast_to(bias, (N_DEV, n_cols)),
                                   a_fn=silu, tm=N_DEV, tn=512, tk=1024))
    mod_block = jnp.concatenate(mod_shard, axis=1)
    mod_all = _all_gather("gather_mod", mod_block).reshape(N_DEV, N_DEV, depth, n_cols)
    mod_rows = lax.dynamic_index_in_dim(mod_all[0::2], my_dev, axis=1, keepdims=False)
    mods = [mod_rows[:, l, :].reshape(n_mod, d) for l in range(depth)]

    loss_blk, dx, d_mods, grads, big_grads = _fwd_bwd(xs, loss_target[0], mods, small, wts0, shards[1:],
                                                      core=mc.astype(jnp.int32).reshape(1))
    loss = lax.psum(loss_blk[0, 0], ("x", "y", "c"))
    grad_x = dx[None]

    partial_names = ("g_pre_mix", "g_post_mix", "g_pre_ffn", "g_post_ffn", "d_skip", "b_glu", "b_f", "a_bar",
                     "bbar_re", "bbar_im", "c_re", "c_im")
    n_state, group_ch = b_re.shape[2:]
    contrib = list(d_mods)
    for l in range(depth):
        compact = _compact_partials(grads[l], n_state, group_ch)
        contrib += [compact[n] for n in partial_names]
    contrib_shapes = [a.shape for a in contrib]
    block = _pack(contrib, LANES, BF16_ROWS, F32)
    rows = block.shape[0]
    all_blocks = _all_gather("gather_small_grads", block).reshape(N_DEV, rows, LANES)
    summed = _unpack(_sum_blocks("sum_small_grads", all_blocks, F32), contrib_shapes)
    per_layer = len(partial_names)
    small_grads = {n: [] for n in SMALL}
    d_mod_all = []
    for l in range(depth):
        small_grads["b_ada"].append(summed[l].reshape(-1))
        gl = dict(zip(partial_names, summed[depth + l * per_layer:depth + (l + 1) * per_layer]))
        for n in ("g_pre_mix", "g_post_mix", "g_pre_ffn", "g_post_ffn", "d_skip", "b_glu", "b_f"):
            small_grads[n].append(gl[n])
        for n, gval in _small_grads_from_partials(gl, small, l).items():
            small_grads[n].append(gval)
        mod_rows_ = n_mod * d // LANES
        d_mod_all.append(all_blocks[:, l * mod_rows_:(l + 1) * mod_rows_, :].reshape(N_DEV, n_mod * d))
    small_grads = {n: jnp.stack(v) for n, v in small_grads.items()}

    g_w_ada = []
    for l in range(depth):
        cols = lax.dynamic_slice_in_dim(d_mod_all[l], my_chip * n_cols, n_cols, axis=1)
        g_w_ada.append(_mm_plain(f"dw_ada_{l}", c_all, cols, "tn", F32, a_fn=silu, tm=512, tn=512, tk=N_DEV))
    all_grads = dict(big_grads)
    all_grads.update(small_grads)
    all_grads["w_ada"] = jnp.stack(g_w_ada)

    delta, new_m, new_v = {}, {}, {}
    for n in ("w_ada",) + BIG:
        last = weights[n].shape[2]
        to_stored, from_stored = ((0, 1, 2),) * 2 if last % LANES == 0 else ((0, 2, 1),) * 2 if last % SUBLANES == 0 \
            else ((2, 0, 1), (1, 2, 0))
        view, back = (lambda a: a.transpose(to_stored)), (lambda a: a.transpose(from_stored))
        outs = _adamw(f"adamw_{n}", view(weights[n]), view(all_grads[n]), view(moments_m[n]), view(moments_v[n]))
        delta[n], new_m[n], new_v[n] = (back(o) for o in outs)
    small_shapes = [weights[n].shape for n in SMALL]
    packed = [_pack([src[n] for n in SMALL], LANES, SUBLANES, F32)[None] for src in (weights, all_grads, moments_m, moments_v)]
    outs = _adamw("adamw_small", *packed)
    for dst, buf in zip((delta, new_m, new_v), outs):
        dst.update(dict(zip(SMALL, _unpack(buf[0], small_shapes))))

    return (loss, grad_x, *[all_grads[n] for n in WEIGHTS], *[delta[n] for n in WEIGHTS],
            *[new_m[n] for n in WEIGHTS], *[new_v[n] for n in WEIGHTS])
```

```python
import math

import jax
import jax.numpy as jnp
from jax import lax
from jax.experimental import pallas as pl
from jax.experimental.pallas import tpu as pltpu

F32 = jnp.float32
BF16 = jnp.bfloat16
MESH = pl.DeviceIdType.MESH

RMS_EPS = 1e-6
EIG_CLIP = 1e-4
ADAM_LR, ADAM_B1, ADAM_B2, ADAM_EPS, ADAM_WD, ADAM_STEP = 0.001, 0.9, 0.999, 1e-08, 0.01, 10

LANES = 128
SUBLANES = 8
VMEM_LIMIT = 56 * 1024 * 1024
ROW_TILE_BYTES = 1 << 20
SUM_TILE_BYTES = 1 << 19
S5_ROWS = 256
S5_CHUNK = 1024
S5_UNROLL = 4
ATT_BLOCK = 512
F_PAD = 256
POSTNORM_ROWS = 1024
N_CHIPS = 4
N_DEV = 8

NN = (((1,), (0,)), ((), ()))
NT = (((1,), (1,)), ((), ()))
TN = (((0,), (0,)), ((), ()))
_DN = {"nn": NN, "nt": NT, "tn": TN}


def _cparams(**kw):
    return pltpu.CompilerParams(vmem_limit_bytes=VMEM_LIMIT, **kw)


def _pick(dim, target):
    best, t = None, LANES
    while t <= min(dim, target):
        if dim % t == 0:
            best = t
        t += LANES
    return best or dim


def _sigmoid(x):
    return 1.0 / (1.0 + jnp.exp(-x))


def _dot(a, b, dn):
    return lax.dot_general(a, b, dn, preferred_element_type=F32)


def _mm_raw(name, a, b, mode, grid, acc_shape, a_spec, b_spec, out_shapes, out_specs, epilogue,
            extra=(), extra_specs=(), a_fn=None, side=None):
    nk = grid[2]
    n_extra, n_out = len(extra), len(out_shapes)

    def body(*refs):
        a_ref, b_ref = refs[0], refs[1]
        extra_refs = refs[2:2 + n_extra]
        out_refs = refs[2 + n_extra:2 + n_extra + n_out]
        acc = refs[-1]
        k = pl.program_id(2)

        @pl.when(k == 0)
        def _():
            acc[...] = jnp.zeros_like(acc)

        av = a_ref[...]
        if a_fn is not None:
            av = a_fn(av.astype(F32))
        acc[...] += _dot(av.astype(BF16), b_ref[...].astype(BF16), _DN[mode])

        @pl.when(k == nk - 1)
        def _():
            epilogue(acc[...], extra_refs, out_refs)

    outs, side_outs = _hosted_call(body, side, name, grid, [a_spec, b_spec, *extra_specs], list(out_specs),
                                   list(out_shapes), [pltpu.VMEM(acc_shape, F32)], (a, b, *extra))
    return outs if side is None else (outs, side_outs)


def _mm(name, a, b, mode, out_shapes, out_specs, epilogue, extra=(), extra_specs=(),
        tm=512, tn=512, tk=512, a_fn=None):
    if mode == "nn":
        (m, kd), (_, n) = a.shape, b.shape
    elif mode == "nt":
        (m, kd), (n, _) = a.shape, b.shape
    else:
        (kd, m), (_, n) = a.shape, b.shape
    tm, tn, tk = _pick(m, tm), _pick(n, tn), _pick(kd, tk)
    if mode == "tn":
        a_spec = pl.BlockSpec((tk, tm), lambda i, j, k: (k, i))
    else:
        a_spec = pl.BlockSpec((tm, tk), lambda i, j, k: (i, k))
    if mode == "nt":
        b_spec = pl.BlockSpec((tn, tk), lambda i, j, k: (j, k))
    else:
        b_spec = pl.BlockSpec((tk, tn), lambda i, j, k: (k, j))
    res = _mm_raw(name, a, b, mode, (m // tm, n // tn, kd // tk), (tm, tn), a_spec, b_spec, out_shapes, out_specs,
                  epilogue, extra=extra, extra_specs=extra_specs, a_fn=a_fn)
    return res, (tm, tn, tk)


def _store(dtype):
    def epilogue(acc, extra_refs, out_refs):
        out_refs[0][...] = acc.astype(dtype)
    return epilogue


def _mm_sum(name, m, n, tm, tn, pairs, out_dtype, side=None):
    offs, total = [], 0
    for pr in pairs:
        offs.append(total)
        total += pr[6]
    n_p = len(pairs)

    def body(*refs):
        o_ref, acc = refs[2 * n_p], refs[2 * n_p + 1]
        k = pl.program_id(2)

        @pl.when(k == 0)
        def _():
            acc[...] = jnp.zeros_like(acc)

        for p_ in range(n_p):
            @pl.when((k >= offs[p_]) & (k < offs[p_] + pairs[p_][6]))
            def _(p_=p_):
                acc[...] += _dot(refs[2 * p_][...].astype(BF16), refs[2 * p_ + 1][...].astype(BF16), NT)

        @pl.when(k == total - 1)
        def _():
            o_ref[...] = acc[...].astype(out_dtype)

    in_specs, operands = [], []
    for (a, a_block, a_index, b, b_block, b_index, steps), off in zip(pairs, offs):
        local = lambda k, off=off, steps=steps: jnp.clip(k - off, 0, steps - 1)
        in_specs.append(pl.BlockSpec(a_block, lambda i, j, k, f=a_index, local=local: f(i, local(k))))
        in_specs.append(pl.BlockSpec(b_block, lambda i, j, k, f=b_index, local=local: f(j, local(k))))
        operands += [a, b]
    (out,), side_outs = _hosted_call(
        body, side, name, (m // tm, n // tn, total), in_specs, [pl.BlockSpec((tm, tn), lambda i, j, k: (i, j))],
        [jax.ShapeDtypeStruct((m, n), out_dtype)], [pltpu.VMEM((tm, tn), F32)], operands)
    return out if side is None else (out, side_outs)


class _SideJob:
    def __init__(self, arrays, out_shapes, aliases, n_sems, copies):
        self.arrays, self.out_shapes, self.aliases, self.n_sems, self.copies = arrays, out_shapes, aliases, n_sems, copies


def _hosted_call(body, side, name, grid, in_specs, out_specs, out_shape, scratch_shapes, operands):
    if side is None:
        outs = pl.pallas_call(body, name=name, grid=grid, in_specs=in_specs, out_specs=out_specs, out_shape=out_shape,
                              scratch_shapes=scratch_shapes, compiler_params=_cparams())(*operands)
        return outs, []
    n_in, n_out, ns_in, ns_out = len(in_specs), len(out_specs), len(side.arrays), len(side.out_shapes)

    def wrapped(*refs):
        main_in, side_in = refs[:n_in], refs[n_in:n_in + ns_in]
        rest = refs[n_in + ns_in:]
        main_out, side_out, rest = rest[:n_out], rest[n_out:n_out + ns_out], rest[n_out + ns_out:]
        scratch, send_sems, recv_sems = rest[:-2], rest[-2], rest[-1]
        first, last = None, None
        for axis, extent in enumerate(grid):
            at_start, at_end = pl.program_id(axis) == 0, pl.program_id(axis) == extent - 1
            first = at_start if first is None else first & at_start
            last = at_end if last is None else last & at_end

        @pl.when(first)
        def _():
            for cp in side.copies(side_in, side_out, send_sems, recv_sems):
                cp.start()

        body(*main_in, *main_out, *scratch)

        @pl.when(last)
        def _():
            for cp in side.copies(side_in, side_out, send_sems, recv_sems):
                cp.wait()

    hbm = pl.BlockSpec(memory_space=pl.ANY)
    outs = pl.pallas_call(
        wrapped, name=name, grid=grid, in_specs=list(in_specs) + [hbm] * ns_in,
        out_specs=list(out_specs) + [hbm] * ns_out, out_shape=list(out_shape) + list(side.out_shapes),
        scratch_shapes=list(scratch_shapes) + [pltpu.SemaphoreType.DMA((side.n_sems,))] * 2,
        input_output_aliases={n_in + i: n_out + o for i, o in side.aliases.items()},
        compiler_params=_cparams(),
    )(*operands, *side.arrays)
    return outs[:n_out], outs[n_out:]


def _ffn_up(name, h, wg, wu, side=None):
    s, d = h.shape
    nc, fs = wg.shape[0], wg.shape[2]
    tm, tk = _pick(s, 1024), _pick(d, 1024)
    nk = d // tk

    def body(h_ref, wg_ref, wu_ref, a_ref, b_ref, hid_ref, acc_g, acc_u):
        k = pl.program_id(2)

        @pl.when(k == 0)
        def _():
            acc_g[...] = jnp.zeros_like(acc_g)
            acc_u[...] = jnp.zeros_like(acc_u)

        hv = h_ref[...]
        acc_g[...] += _dot(hv, wg_ref[...], NN)
        acc_u[...] += _dot(hv, wu_ref[...], NN)

        @pl.when(k == nk - 1)
        def _():
            av, bv = acc_g[...], acc_u[...]
            a_ref[...] = av.astype(BF16)
            b_ref[...] = bv.astype(BF16)
            hid_ref[...] = (av * _sigmoid(av) * bv).astype(BF16)

    w_spec = pl.BlockSpec((None, tk, fs), lambda i, j, k: (j, k, 0))
    o_spec = pl.BlockSpec((None, tm, fs), lambda i, j, k: (j, i, 0))
    sh = jax.ShapeDtypeStruct((nc, s, fs), BF16)
    return _hosted_call(
        body, side, name, (s // tm, nc, nk), [pl.BlockSpec((tm, tk), lambda i, j, k: (i, k)), w_spec, w_spec],
        [o_spec] * 3, [sh] * 3, [pltpu.VMEM((tm, fs), F32), pltpu.VMEM((tm, fs), F32)], (h, wg, wu))


def _mm_plain(name, a, b, mode, out_dtype, add=None, a_fn=None, tm=512, tn=512, tk=512):
    if mode == "nn":
        m, n = a.shape[0], b.shape[1]
    elif mode == "nt":
        m, n = a.shape[0], b.shape[0]
    else:
        m, n = a.shape[1], b.shape[1]
    tm_, tn_ = _pick(m, tm), _pick(n, tn)
    spec = pl.BlockSpec((tm_, tn_), lambda i, j, k: (i, j))

    def epilogue(acc, extra_refs, out_refs):
        if add is not None:
            acc = acc + extra_refs[0][...]
        out_refs[0][...] = acc.astype(out_dtype)

    extra = () if add is None else (add,)
    (out,), _ = _mm(name, a, b, mode, [jax.ShapeDtypeStruct((m, n), out_dtype)], [spec], epilogue,
                    extra=extra, extra_specs=[spec] * len(extra), tm=tm, tn=tn, tk=tk, a_fn=a_fn)
    return out


def _row_tile(s, d):
    return _pick(s, max(SUBLANES, ROW_TILE_BYTES // (4 * d)))


def _prenorm_fwd(name, x, g, scale, shift):
    s, d = x.shape
    tr = _row_tile(s, d)

    def body(x_ref, g_ref, sc_ref, sh_ref, h_ref):
        xv = x_ref[...]
        r = lax.rsqrt(jnp.mean(xv * xv, axis=-1, keepdims=True) + RMS_EPS)
        h_ref[...] = ((xv * r * g_ref[...]) * (1.0 + sc_ref[...]) + sh_ref[...]).astype(BF16)

    row = pl.BlockSpec((tr, d), lambda i: (i, 0))
    vec = pl.BlockSpec((1, d), lambda i: (0, 0))
    return pl.pallas_call(body, name=name, grid=(s // tr,), in_specs=[row, vec, vec, vec], out_specs=row,
                          out_shape=jax.ShapeDtypeStruct((s, d), BF16), compiler_params=_cparams())(x, g, scale, shift)


def _prenorm_bwd(name, dh, x, g, scale, dx_res):
    s, d = x.shape
    tr = _row_tile(s, d)

    def body(dh_ref, x_ref, g_ref, sc_ref, dxr_ref, dx_ref, sums_ref):
        @pl.when(pl.program_id(0) == 0)
        def _():
            sums_ref[...] = jnp.zeros_like(sums_ref)

        xv, dhv, gv = x_ref[...], dh_ref[...].astype(F32), g_ref[...]
        r = lax.rsqrt(jnp.mean(xv * xv, axis=-1, keepdims=True) + RMS_EPS)
        xhat = xv * r
        dxn = dhv * (1.0 + sc_ref[...])
        dxhat = dxn * gv
        dx = r * (dxhat - xhat * jnp.mean(dxhat * xhat, axis=-1, keepdims=True))
        dx_ref[...] = dxr_ref[...] + dx
        sums_ref[0:1, :] += jnp.sum(dhv * (xhat * gv), axis=0, keepdims=True)
        sums_ref[1:2, :] += jnp.sum(dhv, axis=0, keepdims=True)
        sums_ref[2:3, :] += jnp.sum(dxn * xhat, axis=0, keepdims=True)

    row = pl.BlockSpec((tr, d), lambda i: (i, 0))
    vec = pl.BlockSpec((1, d), lambda i: (0, 0))
    acc = pl.BlockSpec((SUBLANES, d), lambda i: (0, 0))
    return pl.pallas_call(
        body, name=name, grid=(s // tr,), in_specs=[row, row, vec, vec, row], out_specs=[row, acc],
        out_shape=[jax.ShapeDtypeStruct((s, d), F32), jax.ShapeDtypeStruct((SUBLANES, d), F32)],
        compiler_params=_cparams())(dh, x, g, scale, dx_res)


def _postnorm_bwd(name, dxn, y, g, gate):
    s, d = y.shape
    tr = _row_tile(s, d)

    def body(dx_ref, y_ref, g_ref, gt_ref, dy_ref, sums_ref):
        @pl.when(pl.program_id(0) == 0)
        def _():
            sums_ref[...] = jnp.zeros_like(sums_ref)

        yv, dxv, gv = y_ref[...], dx_ref[...], g_ref[...]
        r = lax.rsqrt(jnp.mean(yv * yv, axis=-1, keepdims=True) + RMS_EPS)
        yhat = yv * r
        dn = dxv * gt_ref[...]
        dyhat = dn * gv
        dy_ref[...] = (r * (dyhat - yhat * jnp.mean(dyhat * yhat, axis=-1, keepdims=True))).astype(BF16)
        sums_ref[0:1, :] += jnp.sum(dxv * (yhat * gv), axis=0, keepdims=True)
        sums_ref[1:2, :] += jnp.sum(dn * yhat, axis=0, keepdims=True)

    row = pl.BlockSpec((tr, d), lambda i: (i, 0))
    vec = pl.BlockSpec((1, d), lambda i: (0, 0))
    acc = pl.BlockSpec((SUBLANES, d), lambda i: (0, 0))
    return pl.pallas_call(
        body, name=name, grid=(s // tr,), in_specs=[row, row, vec, vec], out_specs=[row, acc],
        out_shape=[jax.ShapeDtypeStruct((s, d), BF16), jax.ShapeDtypeStruct((SUBLANES, d), F32)],
        compiler_params=_cparams())(dxn, y, g, gate)


def _loss_grad(name, y, target):
    s, d = y.shape
    tr = _row_tile(s, d)

    def body(y_ref, t_ref, dy_ref, loss_ref):
        @pl.when(pl.program_id(0) == 0)
        def _():
            loss_ref[...] = jnp.zeros_like(loss_ref)

        err = y_ref[...] - t_ref[...]
        dy_ref[...] = err * (1.0 / d)
        part = jnp.sum(jnp.sum(err * err, axis=-1, keepdims=True), axis=0, keepdims=True) * (0.5 / d)
        loss_ref[...] += jnp.broadcast_to(part, loss_ref.shape)

    row = pl.BlockSpec((tr, d), lambda i: (i, 0))
    acc = pl.BlockSpec((SUBLANES, LANES), lambda i: (0, 0))
    return pl.pallas_call(
        body, name=name, grid=(s // tr,), in_specs=[row, row], out_specs=[row, acc],
        out_shape=[jax.ShapeDtypeStruct((s, d), F32), jax.ShapeDtypeStruct((SUBLANES, LANES), F32)],
        compiler_params=_cparams())(y, target)


def _gelu(y):
    c = math.sqrt(2.0 / math.pi)
    return 0.5 * y * (1.0 + jnp.tanh(c * (y + 0.044715 * (y * y * y))))


def _gelu_grad(y):
    c = math.sqrt(2.0 / math.pi)
    th = jnp.tanh(c * (y + 0.044715 * (y * y * y)))
    return 0.5 * (1.0 + th) + 0.5 * y * (1.0 - th * th) * c * (1.0 + 3.0 * 0.044715 * (y * y))


def _cmul_add(br, bi, ar, ai, xr, xi):
    return br + ar * xr - ai * xi, bi + ar * xi + ai * xr


def _scan_rows(x_ref, row0, n_steps, ns2, pow_ref, tab_ref, carry_ref, reverse, fold=None):
    assert n_steps % SUBLANES == 0
    wc = min(S5_CHUNK, ns2)
    sub = lax.broadcasted_iota(jnp.int32, (SUBLANES, wc), 0)
    unroll = S5_UNROLL if n_steps % S5_UNROLL == 0 else 1
    for c0 in range(0, ns2, wc):
        re = slice(c0, c0 + wc)
        im = slice(ns2 + c0, ns2 + c0 + wc)
        first_power = slice(n_steps - 1, n_steps) if reverse else slice(0, 1)
        ar = jnp.broadcast_to(pow_ref[first_power, re], (SUBLANES, wc))
        ai = jnp.broadcast_to(pow_ref[first_power, im], (SUBLANES, wc))
        rows = lambda r: pl.ds(pl.multiple_of(row0 + r * SUBLANES, SUBLANES), SUBLANES)
        step_of = lambda i: (n_steps - 1 - i) if reverse else i

        def local(i, carry, re=re, im=im, ar=ar, ai=ai):
            for u in range(unroll):
                r = step_of(i * unroll + u)
                carry = _cmul_add(x_ref[rows(r), re], x_ref[rows(r), im], ar, ai, *carry)
                x_ref[rows(r), re], x_ref[rows(r), im] = carry
            return carry

        zero = jnp.zeros((SUBLANES, wc), F32)
        lr, li = lax.fori_loop(0, n_steps // unroll, local, (zero, zero))

        tabs = [tab_ref[k, :, re] for k in range(8)]
        for lvl, k in enumerate((1, 2, 4)):
            sh = (SUBLANES - k) if reverse else k
            lr, li = _cmul_add(lr, li, tabs[2 * lvl], tabs[2 * lvl + 1], pltpu.roll(lr, sh, 0), pltpu.roll(li, sh, 0))
        cr, ci = carry_ref[0:1, re], carry_ref[0:1, im]
        lr, li = _cmul_add(lr, li, tabs[6], tabs[7], cr, ci)
        edge, away, last = (SUBLANES - 1, SUBLANES - 1, 0) if reverse else (0, 1, SUBLANES - 1)
        carry_ref[0:1, re] = lr[last:last + 1, :]
        carry_ref[0:1, im] = li[last:last + 1, :]
        er = jnp.where(sub == edge, cr, pltpu.roll(lr, away, 0))
        ei = jnp.where(sub == edge, ci, pltpu.roll(li, away, 0))

        def fix(j, acc, re=re, im=im, er=er, ei=ei, c0=c0):
            base = pl.ds(pl.multiple_of(j * SUBLANES, SUBLANES), SUBLANES)
            pw_r, pw_i = pow_ref[base, re], pow_ref[base, im]
            for i in range(SUBLANES):
                r = j * SUBLANES + i
                xr, xi = _cmul_add(x_ref[rows(r), re], x_ref[rows(r), im], pw_r[i:i + 1, :], pw_i[i:i + 1, :], er, ei)
                x_ref[rows(r), re], x_ref[rows(r), im] = xr, xi
                if fold is not None:
                    acc = fold(c0, r, xr, xi, acc)
            return acc

        acc = lax.fori_loop(0, n_steps // SUBLANES, fix, (zero, zero) if fold is not None else 0)
        if fold is not None:
            fold(c0, None, None, None, acc)


def _s5_fwd(name, u, b_blk, c_blk, a_f, tab_f, dskip, w_glu, b_glu):
    s, w = u.shape[0], w_glu.shape[0]
    nkb = w // LANES
    ns2 = b_blk.shape[2] // 2 * nkb
    half = ns2 // nkb
    t = min(S5_ROWS, s)
    nblk = s // t

    def body(u_ref, b_ref, c_ref, a_ref, tab_ref, ds_ref, wg_ref, bg_ref, y_ref, ys_ref, cs_ref, xs, carry):
        @pl.when(pl.program_id(0) == 0)
        def _():
            carry[...] = jnp.zeros_like(carry)

        cs_ref[0] = carry[...]
        for kb in range(nkb):
            bu = _dot(u_ref[:, kb * LANES:(kb + 1) * LANES], b_ref[kb], NN)
            xs[:, kb * half:(kb + 1) * half] = bu[:, :half]
            xs[:, ns2 + kb * half:ns2 + (kb + 1) * half] = bu[:, half:]
        _scan_rows(xs, 0, t // SUBLANES, ns2, a_ref, tab_ref, carry, reverse=False)
        for kb in range(nkb):
            cols = slice(kb * LANES, (kb + 1) * LANES)
            yk = _dot(xs[:, kb * half:(kb + 1) * half].astype(BF16), c_ref[kb, :half, :], NN)
            yk += _dot(xs[:, ns2 + kb * half:ns2 + (kb + 1) * half].astype(BF16), c_ref[kb, half:, :], NN)
            y_ref[:, cols] = yk + ds_ref[:, cols] * u_ref[:, cols].astype(F32)
        z = _gelu(y_ref[...])
        gate = _sigmoid(_dot(z.astype(BF16), wg_ref[...], NN) + bg_ref[...])
        ys_ref[...] = (z * gate).astype(BF16)

    row = pl.BlockSpec((t, w), lambda i: (i, 0))
    full = lambda shape: pl.BlockSpec(shape, lambda i: (0,) * len(shape))
    return pl.pallas_call(
        body, name=name, grid=(nblk,),
        in_specs=[row, full(b_blk.shape), full(c_blk.shape), full(a_f.shape), full(tab_f.shape), full(dskip.shape),
                  full(w_glu.shape), full(b_glu.shape)],
        out_specs=[row, row, pl.BlockSpec((1, 1, 2 * ns2), lambda i: (i, 0, 0))],
        out_shape=[jax.ShapeDtypeStruct((s, w), F32), jax.ShapeDtypeStruct((s, w), BF16),
                   jax.ShapeDtypeStruct((nblk, 1, 2 * ns2), F32)],
        scratch_shapes=[pltpu.VMEM((t, 2 * ns2), F32), pltpu.VMEM((1, 2 * ns2), F32)],
        compiler_params=_cparams(),
    )(u, b_blk, c_blk, a_f, tab_f, dskip, w_glu, b_glu)


def _s5_bwd(name, u, dys, y, carries, b_blk, c_blk, a_f, a_r, tab_f, tab_r, dskip, w_glu, b_glu):
    s, w = u.shape[0], w_glu.shape[0]
    nkb = w // LANES
    ns2 = b_blk.shape[2] // 2 * nkb
    half = ns2 // nkb
    t = min(S5_ROWS, s)
    nblk = s // t
    ng = t // SUBLANES

    def body(u_ref, dys_ref, y_ref, cs_ref, b_ref, c_ref, af_ref, ar_ref, tabf_ref, tabr_ref, ds_ref, wg_ref, bg_ref,
             du_ref, db_ref, dc_ref, da_ref, dwg_ref, vec_ref, xs, gs, dyv, fcarry, gcarry):
        @pl.when(pl.program_id(0) == 0)
        def _():
            db_ref[...] = jnp.zeros_like(db_ref)
            dc_ref[...] = jnp.zeros_like(dc_ref)
            da_ref[...] = jnp.zeros_like(da_ref)
            dwg_ref[...] = jnp.zeros_like(dwg_ref)
            vec_ref[...] = jnp.zeros_like(vec_ref)
            gcarry[...] = jnp.zeros_like(gcarry)

        yv = y_ref[...]
        z = _gelu(yv)
        zb = z.astype(BF16)
        gate = _sigmoid(_dot(zb, wg_ref[...], NN) + bg_ref[...])
        dout = dys_ref[...].astype(F32)
        dt = dout * z * gate * (1.0 - gate)
        dtb = dt.astype(BF16)
        dz = dout * gate + _dot(dtb, wg_ref[...], NT)
        dy = dz * _gelu_grad(yv)
        dyv[...] = dy
        dwg_ref[...] += _dot(zb, dtb, TN)
        vec_ref[0:1, :] += jnp.sum(dt, axis=0, keepdims=True)
        vec_ref[1:2, :] += jnp.sum(dy * u_ref[...].astype(F32), axis=0, keepdims=True)

        fcarry[...] = cs_ref[0]
        xs[0:SUBLANES, :] = jnp.broadcast_to(cs_ref[0], (SUBLANES, 2 * ns2))
        for kb in range(nkb):
            bu = _dot(u_ref[:, kb * LANES:(kb + 1) * LANES], b_ref[kb], NN)
            xs[SUBLANES:, kb * half:(kb + 1) * half] = bu[:, :half]
            xs[SUBLANES:, ns2 + kb * half:ns2 + (kb + 1) * half] = bu[:, half:]
        _scan_rows(xs, SUBLANES, ng, ns2, af_ref, tabf_ref, fcarry, reverse=False)
        first_segment = lax.broadcasted_iota(jnp.int32, (SUBLANES, 2 * ns2), 0) == 0
        xs[0:SUBLANES, :] = jnp.where(first_segment, xs[0:SUBLANES, :], pltpu.roll(xs[t:t + SUBLANES, :], 1, 0))

        for kb in range(nkb):
            dyk = dyv[:, kb * LANES:(kb + 1) * LANES].astype(BF16)
            re = slice(kb * half, (kb + 1) * half)
            im = slice(ns2 + kb * half, ns2 + (kb + 1) * half)
            gs[:, re] = _dot(dyk, c_ref[kb, :half, :], NT)
            gs[:, im] = _dot(dyk, c_ref[kb, half:, :], NT)
            dc_ref[kb, :half, :] += _dot(xs[SUBLANES:, re].astype(BF16), dyk, TN)
            dc_ref[kb, half:, :] += _dot(xs[SUBLANES:, im].astype(BF16), dyk, TN)

        def fold(c0, r, gr, gi, acc):
            wc = min(S5_CHUNK, ns2)
            re = slice(c0, c0 + wc)
            im = slice(ns2 + c0, ns2 + c0 + wc)
            if r is None:
                da_ref[:, re] += acc[0]
                da_ref[:, im] += acc[1]
                return acc
            before = pl.ds(pl.multiple_of(r * SUBLANES, SUBLANES), SUBLANES)
            xpr, xpi = xs[before, re], xs[before, im]
            return acc[0] + gr * xpr + gi * xpi, acc[1] - gr * xpi + gi * xpr

        _scan_rows(gs, 0, ng, ns2, ar_ref, tabr_ref, gcarry, reverse=True, fold=fold)

        for kb in range(nkb):
            cols = slice(kb * LANES, (kb + 1) * LANES)
            re = slice(kb * half, (kb + 1) * half)
            im = slice(ns2 + kb * half, ns2 + (kb + 1) * half)
            uk = u_ref[:, cols]
            gr = gs[:, re].astype(BF16)
            gi = gs[:, im].astype(BF16)
            db_ref[kb, :, :half] += _dot(uk, gr, TN)
            db_ref[kb, :, half:] += _dot(uk, gi, TN)
            duk = _dot(gr, b_ref[kb, :, :half], NT) + _dot(gi, b_ref[kb, :, half:], NT)
            du_ref[:, cols] = (duk + ds_ref[:, cols] * dyv[:, cols]).astype(BF16)

    rev = lambda i: (nblk - 1 - i, 0)
    row = pl.BlockSpec((t, w), rev)
    full = lambda shape: pl.BlockSpec(shape, lambda i: (0,) * len(shape))
    return pl.pallas_call(
        body, name=name, grid=(nblk,),
        in_specs=[row, row, row, pl.BlockSpec((1, 1, 2 * ns2), lambda i: (nblk - 1 - i, 0, 0)),
                  full(b_blk.shape), full(c_blk.shape), full(a_f.shape), full(a_r.shape), full(tab_f.shape),
                  full(tab_r.shape), full(dskip.shape), full(w_glu.shape), full(b_glu.shape)],
        out_specs=[row, full(b_blk.shape), full(c_blk.shape), full((SUBLANES, 2 * ns2)), full((w, w)),
                   full((SUBLANES, w))],
        out_shape=[jax.ShapeDtypeStruct((s, w), BF16), jax.ShapeDtypeStruct(b_blk.shape, F32),
                   jax.ShapeDtypeStruct(c_blk.shape, F32), jax.ShapeDtypeStruct((SUBLANES, 2 * ns2), F32),
                   jax.ShapeDtypeStruct((w, w), F32), jax.ShapeDtypeStruct((SUBLANES, w), F32)],
        scratch_shapes=[pltpu.VMEM((t + SUBLANES, 2 * ns2), F32), pltpu.VMEM((t, 2 * ns2), F32),
                        pltpu.VMEM((t, w), F32), pltpu.VMEM((1, 2 * ns2), F32), pltpu.VMEM((1, 2 * ns2), F32)],
        compiler_params=_cparams(),
    )(u, dys, y, carries, b_blk, c_blk, a_f, a_r, tab_f, tab_r, dskip, w_glu, b_glu)


def _log_sigmoid(x):
    return jnp.minimum(x, 0.0) - jnp.log(1.0 + jnp.exp(-jnp.abs(x)))


def _cum_fwd(name, f_t, b_f):
    h, s = f_t.shape
    tc = _pick(s, 512)
    nb = s // tc

    def body(f_ref, b_ref, c_ref, carry):
        @pl.when(pl.program_id(0) == 0)
        def _():
            carry[...] = jnp.zeros_like(carry)

        lf = _log_sigmoid(f_ref[...] + b_ref[...])
        upper = (lax.broadcasted_iota(jnp.int32, (tc, tc), 0) <= lax.broadcasted_iota(jnp.int32, (tc, tc), 1))
        cum = lax.dot_general(lf, upper.astype(F32), NN, precision=lax.Precision.HIGHEST,
                              preferred_element_type=F32) + carry[...]
        c_ref[...] = cum
        carry[...] += jnp.sum(lf, axis=1, keepdims=True)

    blk = pl.BlockSpec((h, tc), lambda i: (0, i))
    return pl.pallas_call(body, name=name, grid=(nb,), in_specs=[blk, pl.BlockSpec((h, 1), lambda i: (0, 0))],
                          out_specs=blk, out_shape=jax.ShapeDtypeStruct((h, s), F32),
                          scratch_shapes=[pltpu.VMEM((h, 1), F32)], compiler_params=_cparams())(f_t, b_f)


def _cum_bwd(name, dcq, dck, f_t, b_f):
    h, s = f_t.shape
    tc = _pick(s, 512)
    nb = s // tc

    def body(dcq_ref, dck_ref, f_ref, b_ref, df_ref, db_ref, carry):
        @pl.when(pl.program_id(0) == 0)
        def _():
            carry[...] = jnp.zeros_like(carry)
            db_ref[...] = jnp.zeros_like(db_ref)

        dc = dcq_ref[...] + dck_ref[...]
        lower = (lax.broadcasted_iota(jnp.int32, (tc, tc), 0) >= lax.broadcasted_iota(jnp.int32, (tc, tc), 1))
        dlf = lax.dot_general(dc, lower.astype(F32), NN, precision=lax.Precision.HIGHEST,
                              preferred_element_type=F32) + carry[...]
        carry[...] += jnp.sum(dc, axis=1, keepdims=True)
        df = dlf * _sigmoid(-(f_ref[...] + b_ref[...]))
        df_ref[...] = df
        db_ref[...] += jnp.broadcast_to(jnp.sum(df, axis=1, keepdims=True), db_ref.shape)

    blk = pl.BlockSpec((h, tc), lambda i: (0, nb - 1 - i))
    return pl.pallas_call(
        body, name=name, grid=(nb,), in_specs=[blk, blk, blk, pl.BlockSpec((h, 1), lambda i: (0, 0))],
        out_specs=[blk, pl.BlockSpec((h, LANES), lambda i: (0, 0))],
        out_shape=[jax.ShapeDtypeStruct((h, s), F32), jax.ShapeDtypeStruct((h, LANES), F32)],
        scratch_shapes=[pltpu.VMEM((h, 1), F32)], compiler_params=_cparams())(dcq, dck, f_t, b_f)


def _attn_fwd(name, qkv, q_blk, k_blk, v_blk, n_pairs, ck, side=None):
    s = qkv.shape[0]
    dh = LANES // 2
    t = min(ATT_BLOCK, s)
    nq = s // t
    scale = dh ** -0.5

    def body(q_ref, k_ref, v_ref, ck_ref, o_ref, lse_ref, m_s, acc_s):
        i = pl.program_id(1)
        low = lax.broadcasted_iota(jnp.int32, (1, LANES), 1) < dh
        qs = (q_ref[...].astype(F32) * scale).astype(BF16)
        zero = jnp.zeros_like(qs)
        qh = (jnp.where(low, qs, zero), jnp.where(low, zero, qs))
        m_s[...] = jnp.full(m_s.shape, -1e30, F32)
        acc_s[...] = jnp.zeros_like(acc_s)
        causal = (lax.broadcasted_iota(jnp.int32, (t, t), 1) <= lax.broadcasted_iota(jnp.int32, (t, t), 0))

        def step(j, diagonal):
            r0 = pl.multiple_of(j * t, t)
            kj = k_ref[pl.ds(r0, t), :]
            vj = v_ref[pl.ds(r0, t), :]
            one = jnp.ones_like(vj)
            vh = (jnp.where(low, vj, one), jnp.where(low, one, vj))
            for hd in range(2):
                sc = _dot(qh[hd], kj, NT) - ck_ref[hd, j]
                if diagonal:
                    sc = jnp.where(causal, sc, -1e30)
                m_old = m_s[hd]
                m_new = jnp.maximum(m_old, jnp.max(sc, axis=1, keepdims=True))
                p = jnp.exp(sc - m_new)
                acc_s[hd] = jnp.exp(m_old - m_new) * acc_s[hd] + _dot(p.astype(BF16), vh[hd], NN)
                m_s[hd] = m_new

        def full(j, _):
            step(j, False)
            return 0

        lax.fori_loop(0, i, full, 0)
        step(i, True)
        a0, a1 = acc_s[0], acc_s[1]
        o_ref[...] = jnp.where(low, a0 / pltpu.roll(a0, dh, 1), a1 / pltpu.roll(a1, dh, 1)).astype(BF16)
        lse_ref[0] = m_s[0] + jnp.log(a0[:, dh:dh + 1])
        lse_ref[1] = m_s[1] + jnp.log(a1[:, 0:1])

    return _hosted_call(
        body, side, name, (n_pairs, nq),
        [pl.BlockSpec((t, LANES), lambda hp, i: (i, q_blk + hp)),
         pl.BlockSpec((s, LANES), lambda hp, i: (0, k_blk + hp)),
         pl.BlockSpec((s, LANES), lambda hp, i: (0, v_blk + hp)),
         pl.BlockSpec((2, nq, 1, t), lambda hp, i: (hp, 0, 0, 0))],
        [pl.BlockSpec((t, LANES), lambda hp, i: (i, hp)), pl.BlockSpec((2, t, 1), lambda hp, i: (hp, i, 0))],
        [jax.ShapeDtypeStruct((s, LANES * n_pairs), BF16), jax.ShapeDtypeStruct((2 * n_pairs, s, 1), F32)],
        [pltpu.VMEM((2, t, 1), F32), pltpu.VMEM((2, t, LANES), F32)], (qkv, qkv, qkv, ck))


def _attn_bwd(name, qkv, q_blk, k_blk, v_blk, n_pairs, o, do, lse_rows, ck_cols, side=None):
    s = qkv.shape[0]
    dh = LANES // 2
    t = min(ATT_BLOCK, s)
    nk = s // t
    scale = dh ** -0.5

    def body(q_ref, k_ref, v_ref, o_ref, do_ref, lse_ref, ck_ref,
             dq_ref, dk_ref, dv_ref, dcq_ref, dck_ref, delta, dqt, dk_acc, dv_acc):
        j = pl.program_id(1)
        low = lax.broadcasted_iota(jnp.int32, (1, LANES), 1) < dh
        low_rows = lax.broadcasted_iota(jnp.int32, (LANES, 1), 0) < dh

        @pl.when(j == 0)
        def _():
            dqt[...] = jnp.zeros_like(dqt)
            sel = (jnp.broadcast_to(low, (SUBLANES, LANES)).astype(F32), jnp.broadcast_to(~low, (SUBLANES, LANES)).astype(F32))

            def fill(i, _):
                r0 = pl.multiple_of(i * t, t)
                prod = do_ref[pl.ds(r0, t), :].astype(F32) * o_ref[pl.ds(r0, t), :].astype(F32)
                for hd in range(2):
                    delta[hd, i] = lax.dot_general(sel[hd], prod, NT, precision=lax.Precision.HIGHEST,
                                                   preferred_element_type=F32)
                return 0

            lax.fori_loop(0, nk, fill, 0)

        kj, vj = k_ref[...], v_ref[...]
        zero, one = jnp.zeros_like(kj), jnp.ones_like(kj)
        kh = (jnp.where(low, kj, zero), jnp.where(low, zero, kj))
        vh = (jnp.where(low, vj, zero), jnp.where(low, zero, vj))
        kjt = kj.astype(F32).T.astype(BF16)
        one_t = jnp.ones_like(kjt)
        kht = (jnp.where(low_rows, kjt, one_t), jnp.where(low_rows, one_t, kjt))
        dk_acc[...] = jnp.zeros_like(dk_acc)
        dv_acc[...] = jnp.zeros_like(dv_acc)
        causal_t = (lax.broadcasted_iota(jnp.int32, (t, t), 0) <= lax.broadcasted_iota(jnp.int32, (t, t), 1))

        def step(i, diagonal):
            r0 = pl.multiple_of(i * t, t)
            qi = (q_ref[pl.ds(r0, t), :].astype(F32) * scale).astype(BF16)
            doi = do_ref[pl.ds(r0, t), :]
            qone, dzero = jnp.ones_like(qi), jnp.zeros_like(doi)
            qsel = (jnp.where(low, qi, qone), jnp.where(low, qone, qi))
            dosel = (jnp.where(low, doi, dzero), jnp.where(low, dzero, doi))
            for hd in range(2):
                st = _dot(kh[hd], qi, NT) - ck_ref[hd] - lse_ref[hd, i]
                pt = jnp.exp(st)
                if diagonal:
                    pt = jnp.where(causal_t, pt, 0.0)
                dst = pt * (_dot(vh[hd], doi, NT) - delta[hd, i, 0:1, :])
                dsb = dst.astype(BF16)
                dv_acc[...] += _dot(pt.astype(BF16), dosel[hd], NN)
                dk_acc[hd] += _dot(dsb, qsel[hd], NN)
                dqt[hd, i] += _dot(kht[hd], dsb, NN)

        step(j, True)

        def rest(i, _):
            step(i, False)
            return 0

        lax.fori_loop(j + 1, nk, rest, 0)
        dk_ref[...] = jnp.where(low, dk_acc[0], dk_acc[1]).astype(BF16)
        dv_ref[...] = dv_acc[...].astype(BF16)
        dck_ref[0] = -dk_acc[0][:, dh:dh + 1]
        dck_ref[1] = -dk_acc[1][:, 0:1]

        @pl.when(j == nk - 1)
        def _():
            def emit(i, _):
                r0 = pl.multiple_of(i * t, t)
                d0, d1 = dqt[0, i], dqt[1, i]
                dq_ref[pl.ds(r0, t), :] = (jnp.where(low_rows, d0, d1) * scale).T.astype(BF16)
                dcq_ref[0, i] = d0[dh:dh + 1, :]
                dcq_ref[1, i] = d1[0:1, :]
                return 0

            lax.fori_loop(0, nk, emit, 0)

    col_blk = lambda base: pl.BlockSpec((t, LANES), lambda hp, j: (j, base + hp))
    col_all = lambda base: pl.BlockSpec((s, LANES), lambda hp, j: (0, base + hp))
    rows_all = pl.BlockSpec((2, nk, 1, t), lambda hp, j: (hp, 0, 0, 0))
    return _hosted_call(
        body, side, name, (n_pairs, nk),
        [col_all(q_blk), col_blk(k_blk), col_blk(v_blk), col_all(0), col_all(0), rows_all,
         pl.BlockSpec((2, t, 1), lambda hp, j: (hp, j, 0))],
        [col_all(0), col_blk(0), col_blk(0), rows_all, pl.BlockSpec((2, t, 1), lambda hp, j: (hp, j, 0))],
        [jax.ShapeDtypeStruct((s, LANES * n_pairs), BF16), jax.ShapeDtypeStruct((s, LANES * n_pairs), BF16),
         jax.ShapeDtypeStruct((s, LANES * n_pairs), BF16), jax.ShapeDtypeStruct((2 * n_pairs, nk, 1, t), F32),
         jax.ShapeDtypeStruct((2 * n_pairs, s, 1), F32)],
        [pltpu.VMEM((2, nk, SUBLANES, t), F32), pltpu.VMEM((2, nk, LANES, t), F32),
         pltpu.VMEM((2, t, LANES), F32), pltpu.VMEM((t, LANES), F32)],
        (qkv, qkv, qkv, o, do, lse_rows, ck_cols))


def _adamw(name, w, g, m, v):
    n_l, r, c = w.shape
    by_rows = r % SUBLANES == 0
    tr = _pick8(r, max(SUBLANES, ROW_TILE_BYTES // (4 * c))) if by_rows else r
    tl = 1 if by_rows else max(t for t in range(1, n_l + 1) if n_l % t == 0 and t * r * c * 4 <= ROW_TILE_BYTES)

    def body(w_ref, g_ref, m_ref, v_ref, d_ref, mo_ref, vo_ref):
        gv = g_ref[...]
        m2 = ADAM_B1 * m_ref[...] + (1.0 - ADAM_B1) * gv
        v2 = ADAM_B2 * v_ref[...] + (1.0 - ADAM_B2) * (gv * gv)
        m_hat = m2 / (1.0 - ADAM_B1 ** ADAM_STEP)
        v_hat = v2 / (1.0 - ADAM_B2 ** ADAM_STEP)
        d_ref[...] = -ADAM_LR * (m_hat / (jnp.sqrt(v_hat) + ADAM_EPS) + ADAM_WD * w_ref[...])
        mo_ref[...] = m2
        vo_ref[...] = v2

    blk = pl.BlockSpec((None, tr, c) if by_rows else (tl, r, c), lambda l, i: (l, i, 0))
    sh = jax.ShapeDtypeStruct((n_l, r, c), F32)
    return pl.pallas_call(body, name=name, grid=(n_l // tl, r // tr), in_specs=[blk] * 4,
                          out_specs=[blk] * 3, out_shape=[sh, sh, sh], compiler_params=_cparams())(w, g, m, v)


def _pick8(dim, target, mult=SUBLANES):
    best, t = None, mult
    while t <= min(dim, target):
        if dim % t == 0:
            best = t
        t += mult
    return best or dim


BF16_ROWS = 16


def _sum_blocks(name, x, out_dtype):
    n, r, c = x.shape
    tr = _pick8(r, max(BF16_ROWS, SUM_TILE_BYTES // (4 * c)), BF16_ROWS)

    def body(x_ref, o_ref):
        acc = x_ref[0].astype(F32)
        for i in range(1, n):
            acc = acc + x_ref[i].astype(F32)
        o_ref[...] = acc.astype(out_dtype)

    return pl.pallas_call(body, name=name, grid=(r // tr,),
                          in_specs=[pl.BlockSpec((n, tr, c), lambda i: (0, i, 0))],
                          out_specs=pl.BlockSpec((tr, c), lambda i: (i, 0)),
                          out_shape=jax.ShapeDtypeStruct((r, c), out_dtype), compiler_params=_cparams())(x)


def _all_gather(name, x_shard):
    m_per, n = x_shard.shape

    def body(x_ref, out_ref, send_sems, recv_sems):
        x, y, c = lax.axis_index("x"), lax.axis_index("y"), lax.axis_index("c")
        me, sibling = (x, y, c), (x, y, 1 - c)
        chips = [(1 - x, y), (x, 1 - y), (1 - x, 1 - y)]

        def rows(px, py, pc):
            return out_ref.at[pl.ds((4 * px + 2 * py + pc) * m_per, m_per), :]

        def copy(k, block, to, src=None):
            return pltpu.make_async_remote_copy(
                src_ref=rows(*block) if src is None else src, dst_ref=rows(*block),
                send_sem=send_sems.at[k], recv_sem=recv_sems.at[k], device_id=to, device_id_type=MESH)

        first = [copy(0, me, sibling, src=x_ref)]
        first += [copy(1 + j, me, (*chip, c), src=x_ref) for j, chip in enumerate(chips)]
        for cp in first:
            cp.start()
        passed = [copy(4 + j, (*chip, c), sibling) for j, chip in enumerate(chips)]
        for j, chip in enumerate(chips):
            copy(1 + j, (*chip, c), me).wait_recv()
            passed[j].start()
        copy(0, sibling, me).wait_recv()
        for j, chip in enumerate(chips):
            copy(4 + j, (*chip, 1 - c), me).wait_recv()
        for cp in first + passed:
            cp.wait_send()

    out = pl.pallas_call(
        body, name=name, out_shape=jax.ShapeDtypeStruct((N_DEV * m_per, n), x_shard.dtype),
        in_specs=[pl.BlockSpec(memory_space=pl.ANY)], out_specs=pl.BlockSpec(memory_space=pl.ANY),
        scratch_shapes=[pltpu.SemaphoreType.DMA((7,)), pltpu.SemaphoreType.DMA((7,))],
    )(x_shard)
    my_dev = 4 * lax.axis_index("x") + 2 * lax.axis_index("y") + lax.axis_index("c")
    return lax.dynamic_update_slice(out, x_shard, (my_dev * m_per, 0))


def _put_own(out, own, index):
    start = tuple(index) + (0,) * own.ndim
    return lax.dynamic_update_slice(out, own.reshape((1,) * len(index) + own.shape), start)


def _gather_copies(stage, ins, outs, send_sems, recv_sems):
    x, y, c = lax.axis_index("x"), lax.axis_index("y"), lax.axis_index("c")
    my_chip = 2 * x + y
    copies = []
    for w, out in enumerate(outs):
        half = out.shape[1] // 2
        rows = pl.ds(c * half, half)
        for k, (cx, cy) in enumerate([(1 - x, y), (x, 1 - y), (1 - x, 1 - y)]):
            if stage == 0:
                src, dst, to = ins[w].at[rows], out.at[my_chip, rows], (cx, cy, c)
            else:
                src = dst = out.at[2 * cx + cy, rows]
                to = (x, y, 1 - c)
            copies.append(pltpu.make_async_remote_copy(
                src_ref=src, dst_ref=dst, send_sem=send_sems.at[3 * w + k], recv_sem=recv_sems.at[3 * w + k],
                device_id=to, device_id_type=MESH))
    return copies


def _gathered_shapes(shards):
    return [jax.ShapeDtypeStruct((N_CHIPS,) + s.shape, s.dtype) for s in shards]


def _put_own_slabs(gathered, shards):
    my_chip = 2 * lax.axis_index("x") + lax.axis_index("y")
    return [_put_own(o, s, (my_chip,)) for o, s in zip(gathered, shards)]


def _gather_layer(name, shards):
    n_w = len(shards)

    def body(*refs):
        ins, outs = refs[:n_w], refs[n_w:2 * n_w]
        for stage in (0, 1):
            copies = _gather_copies(stage, ins, outs, refs[2 * n_w + 2 * stage], refs[2 * n_w + 2 * stage + 1])
            for cp in copies:
                cp.start()
            for cp in copies:
                cp.wait()

    outs = pl.pallas_call(
        body, name=name, out_shape=_gathered_shapes(shards),
        in_specs=[pl.BlockSpec(memory_space=pl.ANY)] * n_w, out_specs=[pl.BlockSpec(memory_space=pl.ANY)] * n_w,
        scratch_shapes=[pltpu.SemaphoreType.DMA((3 * n_w,))] * 4,
    )(*shards)
    return _put_own_slabs(outs, shards)


def _gather_side_jobs(shards):
    between_chips = _SideJob(list(shards), _gathered_shapes(shards), {}, 3 * len(shards),
                             lambda ins, outs, send, recv: _gather_copies(0, ins, outs, send, recv))
    between_cores = lambda partial: _SideJob(
        list(partial), [jax.ShapeDtypeStruct(p.shape, p.dtype) for p in partial], {w: w for w in range(len(partial))},
        3 * len(partial), lambda ins, outs, send, recv: _gather_copies(1, ins, outs, send, recv))
    return between_chips, between_cores


def _run_job(name, job):
    n_in, n_out = len(job.arrays), len(job.out_shapes)

    def body(*refs):
        copies = job.copies(refs[:n_in], refs[n_in:n_in + n_out], refs[n_in + n_out], refs[n_in + n_out + 1])
        for cp in copies:
            cp.start()
        for cp in copies:
            cp.wait()

    hbm = pl.BlockSpec(memory_space=pl.ANY)
    return pl.pallas_call(
        body, name=name, out_shape=list(job.out_shapes), in_specs=[hbm] * n_in, out_specs=[hbm] * n_out,
        scratch_shapes=[pltpu.SemaphoreType.DMA((job.n_sems,))] * 2, input_output_aliases=dict(job.aliases),
    )(*job.arrays)


def _swap_job(grads):
    def copies(ins, outs, send_sems, recv_sems):
        x, y, c = lax.axis_index("x"), lax.axis_index("y"), lax.axis_index("c")
        return [pltpu.make_async_remote_copy(
            src_ref=g.at[:, pl.ds((1 - c) * (g.shape[1] // 2), g.shape[1] // 2)], dst_ref=outs[w],
            send_sem=send_sems.at[w], recv_sem=recv_sems.at[w], device_id=(x, y, 1 - c), device_id_type=MESH)
            for w, g in enumerate(ins)]

    shapes = [jax.ShapeDtypeStruct((g.shape[0], g.shape[1] // 2, g.shape[2]), g.dtype) for g in grads]
    return _SideJob(list(grads), shapes, {}, len(grads), copies)


def _exchange_job(parts):
    def copies(ins, outs, send_sems, recv_sems):
        x, y, c = lax.axis_index("x"), lax.axis_index("y"), lax.axis_index("c")
        return [pltpu.make_async_remote_copy(
            src_ref=ins[w].at[2 * cx + cy], dst_ref=outs[w].at[2 * x + y], send_sem=send_sems.at[3 * w + k],
            recv_sem=recv_sems.at[3 * w + k], device_id=(cx, cy, c), device_id_type=MESH)
            for w in range(len(ins)) for k, (cx, cy) in enumerate([(1 - x, y), (x, 1 - y), (1 - x, 1 - y)])]

    return _SideJob(list(parts), [jax.ShapeDtypeStruct(p.shape, p.dtype) for p in parts], {}, 3 * len(parts), copies)


def _share_job(bufs, layer):
    def copies(ins, outs, send_sems, recv_sems):
        x, y, c = lax.axis_index("x"), lax.axis_index("y"), lax.axis_index("c")
        mine = [o.at[layer, pl.ds(c * (o.shape[1] // 2), o.shape[1] // 2)] for o in outs]
        return [pltpu.make_async_remote_copy(src_ref=rows, dst_ref=rows, send_sem=send_sems.at[w], recv_sem=recv_sems.at[w],
                                             device_id=(x, y, 1 - c), device_id_type=MESH) for w, rows in enumerate(mine)]

    return _SideJob(list(bufs), [jax.ShapeDtypeStruct(b.shape, b.dtype) for b in bufs], {w: w for w in range(len(bufs))},
                    len(bufs), copies)


def _sum_into(name, blocks, core, layer, depth, into):
    n, r, c = blocks.shape
    tr = _pick8(r, max(BF16_ROWS, SUM_TILE_BYTES // (4 * c)), BF16_ROWS)
    steps = r // tr

    def body(core_ref, x_ref, *rest):
        acc = x_ref[0].astype(F32)
        for i in range(1, n):
            acc = acc + x_ref[i].astype(F32)
        rest[-1][...] = acc

    grid_spec = pltpu.PrefetchScalarGridSpec(
        num_scalar_prefetch=1, grid=(steps,),
        in_specs=[pl.BlockSpec((n, tr, c), lambda i, core_ref: (0, i, 0))]
        + ([pl.BlockSpec(memory_space=pl.ANY)] if into is not None else []),
        out_specs=pl.BlockSpec((None, tr, c), lambda i, core_ref: (layer, core_ref[0] * steps + i, 0)))
    return pl.pallas_call(
        body, name=name, grid_spec=grid_spec, out_shape=jax.ShapeDtypeStruct((depth, 2 * r, c), F32),
        input_output_aliases={2: 0} if into is not None else {}, compiler_params=_cparams(),
    )(core, blocks, *([into] if into is not None else []))


def _add_rows(name, grads, recv, core):
    n, r, c = recv.shape
    tr = _pick8(r, max(BF16_ROWS, SUM_TILE_BYTES // (4 * c)), BF16_ROWS)
    steps = r // tr

    def body(core_ref, g_ref, r_ref, o_ref):
        o_ref[...] = (g_ref[...].astype(F32) + r_ref[...].astype(F32)).astype(BF16)

    grid_spec = pltpu.PrefetchScalarGridSpec(
        num_scalar_prefetch=1, grid=(steps,),
        in_specs=[pl.BlockSpec((n, tr, c), lambda i, core_ref: (0, core_ref[0] * steps + i, 0)),
                  pl.BlockSpec((n, tr, c), lambda i, core_ref: (0, i, 0))],
        out_specs=pl.BlockSpec((n, tr, c), lambda i, core_ref: (0, i, 0)))
    return pl.pallas_call(body, name=name, grid_spec=grid_spec,
                          out_shape=jax.ShapeDtypeStruct((n, r, c), BF16), compiler_params=_cparams())(core, grads, recv)


class _LayerReduce:
    def __init__(self, tag, layer, depth, grads, core, into):
        self.tag, self.layer, self.depth, self.core, self.into = tag, layer, depth, core, into
        self.state = list(grads)

    def _exchange(self, name, job, carry):
        if carry is None:
            return None, _run_job(f"{name}_{self.tag}", job)
        return carry(job)

    def swap_and_add(self, carry=None):
        grads = self.state
        results, recv = self._exchange("grads_swap_cores", _swap_job(grads), carry)
        self.state = [_add_rows(f"grads_add_{n}_{self.tag}", g, r, self.core) for n, g, r in zip(BIG, grads, recv)]
        return results

    def exchange_and_sum(self, carry=None):
        parts = self.state
        results, arrived = self._exchange("grads_exchange_chips", _exchange_job(parts), carry)
        my_chip = 2 * lax.axis_index("x") + lax.axis_index("y")
        arrived = [_put_own(a, lax.dynamic_index_in_dim(p, my_chip, 0, keepdims=False), (my_chip,))
                   for a, p in zip(arrived, parts)]
        into = self.into if self.into is not None else [None] * len(arrived)
        self.state = [_sum_into(f"grads_sum_{n}_{self.tag}", a, self.core, self.layer, self.depth, b)
                      for n, a, b in zip(BIG, arrived, into)]
        return results

    def share(self, carry=None):
        results, self.state = self._exchange("grads_share_cores", _share_job(self.state, self.layer), carry)
        return results


def _pack(arrays, cols, row_multiple, dtype):
    flat = jnp.concatenate([a.reshape(-1).astype(dtype) for a in arrays])
    unit = cols * row_multiple
    total = -(-flat.shape[0] // unit) * unit
    return jnp.pad(flat, (0, total - flat.shape[0])).reshape(total // cols, cols)


def _unpack(buf, shapes):
    flat, out, off = buf.reshape(-1), [], 0
    for sh in shapes:
        n = math.prod(sh)
        out.append(flat[off:off + n].reshape(sh))
        off += n
    return out


def _discretize(lam_re, lam_im, log_dt, b_re, b_im):
    lam = lax.complex(jnp.minimum(lam_re, -EIG_CLIP), lam_im)
    dt = jnp.exp(log_dt)[:, None]
    lam_bar = jnp.exp(lam * dt)
    b_bar = ((lam_bar - 1.0) / lam)[..., None] * lax.complex(b_re, b_im)
    return jnp.real(lam_bar), jnp.imag(lam_bar), jnp.real(b_bar), jnp.imag(b_bar)


def _scan_tables(ar, ai):
    a = lax.complex(ar, ai)
    pw = [a]
    for _ in range(7):
        pw.append(pw[-1] * a)
    rows = jnp.arange(SUBLANES)[:, None]

    def build(p, reverse):
        tabs = []
        for k in (1, 2, 4):
            keep = (rows <= SUBLANES - 1 - k) if reverse else (rows >= k)
            tk = jnp.where(keep, p[k - 1][None, :], 0.0)
            tabs += [jnp.real(tk), jnp.imag(tk)]
        stack = jnp.stack(p[::-1] if reverse else p)
        tabs += [jnp.real(stack), jnp.imag(stack)]
        return jnp.stack(tabs).astype(F32)

    return build(pw, False), build([jnp.conj(p) for p in pw], True)


def _interleave_rows(a, t):
    s, w = a.shape
    return a.reshape(s // t, SUBLANES, t // SUBLANES, w).transpose(0, 2, 1, 3).reshape(s, w)


def _deinterleave_rows(a, t):
    s, w = a.shape
    return a.reshape(s // t, t // SUBLANES, SUBLANES, w).transpose(0, 2, 1, 3).reshape(s, w)


def _block_diag(per_group, groups_per_block):
    g, a, b = per_group.shape
    x = per_group.reshape(g // groups_per_block, groups_per_block, a, b)
    eye = jnp.eye(groups_per_block, dtype=per_group.dtype)
    out = x[:, :, :, None, :] * eye[None, :, None, :, None]
    return out.reshape(g // groups_per_block, groups_per_block * a, groups_per_block * b)


def _block_diag_extract(dense, groups_per_block, a, b):
    nkb = dense.shape[0]
    x = dense.reshape(nkb, groups_per_block, a, groups_per_block, b)
    idx = jnp.arange(groups_per_block)
    return x[:, idx, :, idx, :].transpose(1, 0, 2, 3).reshape(nkb * groups_per_block, a, b)


def _layer_fwd(tag, x, mod, p, wts, carried=None):
    s, d = x.shape
    w_ssm, w_att = p["w_glu"].shape[0], p["w_att"]
    heads = p["b_f"].shape[0]
    dh = w_att // heads
    cs = d // N_CHIPS
    tm = _pick(s, 1024)
    row = lambda v: v.reshape(1, -1)
    sv = {}

    h = _prenorm_fwd(f"prenorm_mix_{tag}", x, row(p["g_pre_mix"]), row(mod[1]), row(mod[0]))
    uqkv = _mm_plain(f"proj_main_{tag}", h, p["w_main"], "nn", BF16, tm=1024, tn=1024, tk=1024)
    fg = _mm_plain(f"proj_gate_{tag}", h, p["w_gates"], "nn", F32, tm=1024, tn=1024, tk=1024)
    f_t = fg[:, 2 * d:2 * d + heads].T

    t5 = min(S5_ROWS, s)
    u_il = _interleave_rows(uqkv[:, :w_ssm], t5)
    y_s5, ys_il, carries = _s5_fwd(f"s5_fwd_{tag}", u_il, p["b_blk"], p["c_blk"], p["a_f"], p["tab_f"],
                                   row(p["d_skip"]), p["w_glu"], row(p["b_glu"]))
    ys = _deinterleave_rows(ys_il, t5)

    assert dh * 2 == LANES and w_ssm % LANES == 0 and w_att % LANES == 0
    n_pairs = w_att // LANES
    blocks = (w_ssm // LANES, w_ssm // LANES + n_pairs, w_ssm // LANES + 2 * n_pairs)
    cum = _cum_fwd(f"cum_fwd_{tag}", f_t, p["b_f"].reshape(heads, 1))
    t = min(ATT_BLOCK, s)
    ck_cols, ck_rows = cum.reshape(heads, s, 1), cum.reshape(heads, s // t, 1, t)
    late_names, late_shards, next_shards = carried if carried else ((), [], [])
    chips_job, cores_job = _gather_side_jobs(list(late_shards) + list(next_shards)) if carried else (None, None)
    (ya, lse), arrived = _attn_fwd(f"attn_fwd_{tag}", uqkv, *blocks, n_pairs, ck_rows, side=chips_job)
    if late_names:
        late = _run_job(f"gather_weights_late_{tag}", cores_job(arrived[:len(late_names)]))
        wts = {**wts, **dict(zip(late_names, _put_own_slabs(late, late_shards)))}
        arrived = arrived[len(late_names):]
    fs = wts["w_ffn_down"].shape[1]

    tile = pl.BlockSpec((tm, cs), lambda i, j, k: (i, j))
    slab = lambda rows: pl.BlockSpec((None, rows, cs), lambda i, j, k: (j, 0, 0))

    def merge(acc, extra_refs, out_refs):
        ya_ref, wpb_ref, ga_ref, gb_ref = extra_refs
        a_ref, b_ref, m_ref = out_refs
        bv = _dot(ya_ref[...], wpb_ref[...], NN)
        a_ref[...] = acc.astype(BF16)
        b_ref[...] = bv.astype(BF16)
        m_ref[...] = (_sigmoid(ga_ref[...]) * acc + _sigmoid(gb_ref[...]) * bv).astype(BF16)

    sd_bf = jax.ShapeDtypeStruct((s, d), BF16)
    pa, pb, merged = _mm_raw(
        f"merge_{tag}", ys, wts["w_pa"], "nn", (s // tm, N_CHIPS, 1), (tm, cs),
        pl.BlockSpec((tm, w_ssm), lambda i, j, k: (i, 0)), slab(w_ssm), [sd_bf] * 3, [tile] * 3, merge,
        extra=(ya, wts["w_pb"], fg, fg),
        extra_specs=[pl.BlockSpec((tm, w_att), lambda i, j, k: (i, 0)), slab(w_att), tile,
                     pl.BlockSpec((tm, cs), lambda i, j, k: (i, j + N_CHIPS))])

    tm2 = _pick(s, POSTNORM_ROWS)
    x1, y_mix = _mm_postnorm(
        f"out_proj_{tag}", merged, pl.BlockSpec((tm2, cs), lambda i, j, k: (i, k)), wts["w_o"],
        pl.BlockSpec((None, cs, d), lambda i, j, k: (k, 0, 0)), N_CHIPS, x, row(mod[2]), row(p["g_post_mix"]))

    h2 = _prenorm_fwd(f"prenorm_ffn_{tag}", x1, row(p["g_pre_ffn"]), row(mod[4]), row(mod[3]))
    (a4, b4, hid4), next_wts = _ffn_up(f"ffn_up_{tag}", h2, wts["w_ffn_gate"], wts["w_ffn_up"],
                                       side=cores_job(arrived) if carried and arrived else None)
    x2, y_ffn = _mm_postnorm(
        f"ffn_down_{tag}", hid4, pl.BlockSpec((None, tm2, fs), lambda i, j, k: (k, i, 0)), wts["w_ffn_down"],
        pl.BlockSpec((None, fs, d), lambda i, j, k: (k, 0, 0)), N_CHIPS, x1, row(mod[5]), row(p["g_post_ffn"]))

    sv.update(x=x, h=h, uqkv=uqkv, u_il=u_il, fg=fg, f_t=f_t, y_s5=y_s5, ys=ys, carries=carries, blocks=blocks,
              ck_cols=ck_cols, lse_rows=lse.reshape(heads, s // t, 1, t), ya=ya, pa=pa, pb=pb, merged=merged, x1=x1,
              y_mix=y_mix, h2=h2, a4=a4, b4=b4, hid4=hid4, y_ffn=y_ffn)
    return x2, sv, wts, next_wts


def _mm_postnorm(name, a, a_spec, w, w_spec, nk, x, gate, g):
    s, d = x.shape
    tm = _pick(s, POSTNORM_ROWS)
    rowspec = pl.BlockSpec((tm, d), lambda i, j, k: (i, 0))
    vec = pl.BlockSpec((1, d), lambda i, j, k: (0, 0))

    def epilogue(acc, extra_refs, out_refs):
        x_ref, gate_ref, g_ref = extra_refs
        r = lax.rsqrt(jnp.mean(acc * acc, axis=-1, keepdims=True) + RMS_EPS)
        out_refs[0][...] = x_ref[...] + gate_ref[...] * (acc * r * g_ref[...])
        out_refs[1][...] = acc

    sd = jax.ShapeDtypeStruct((s, d), F32)
    return _mm_raw(name, a, w, "nn", (s // tm, 1, nk), (tm, d), a_spec, w_spec, [sd, sd], [rowspec, rowspec], epilogue,
                   extra=(x, gate, g), extra_specs=[rowspec, vec, vec])


def _layer_bwd(tag, dx2, mod, p, wts, sv, reduce_later=None):
    s, d = dx2.shape
    w_ssm, w_att = p["w_glu"].shape[0], wts["w_pb"].shape[1]
    heads = p["b_f"].shape[0]
    cs = d // N_CHIPS
    fs = wts["w_ffn_down"].shape[1]
    tm, tk, td = _pick(s, 1024), _pick(s, 1024), d
    row = lambda v: v.reshape(1, -1)
    gr = {}

    def dw_slabs(name, act, act_spec, rows, dy, dy_spec, cols, grid_mn, out_index):
        return _mm_raw(name, act, dy, "tn", grid_mn + (s // tk,), (rows, cols), act_spec, dy_spec,
                       [jax.ShapeDtypeStruct((N_CHIPS,) + out_index[1], BF16)],
                       [pl.BlockSpec((None, rows, cols), out_index[0])], _store(BF16))[0]

    dy_ffn, sums = _postnorm_bwd(f"postnorm_bwd_ffn_{tag}", dx2, sv["y_ffn"], row(p["g_post_ffn"]), row(mod[5]))
    d_gate_f, gr["g_post_ffn"] = sums[0], sums[1]
    gr["w_ffn_down"] = dw_slabs(f"dw_down_{tag}", sv["hid4"], pl.BlockSpec((None, tk, fs), lambda i, j, k: (i, k, 0)), fs,
                                dy_ffn, pl.BlockSpec((tk, d), lambda i, j, k: (k, 0)), d, (N_CHIPS, 1),
                                (lambda i, j, k: (i, 0, 0), (fs, d)))

    def swiglu_bwd(acc, extra_refs, out_refs):
        av, bv = extra_refs[0][...].astype(F32), extra_refs[1][...].astype(F32)
        sg = _sigmoid(av)
        out_refs[0][...] = (acc * bv * (sg * (1.0 + av * (1.0 - sg)))).astype(BF16)
        out_refs[1][...] = (acc * (av * sg)).astype(BF16)

    blk4 = pl.BlockSpec((None, tm, fs), lambda i, j, k: (j, i, 0))
    sh4 = jax.ShapeDtypeStruct((N_CHIPS, s, fs), BF16)
    ffn_down_bwd = lambda side: _mm_raw(
        f"ffn_down_bwd_{tag}", dy_ffn, wts["w_ffn_down"], "nt", (s // tm, N_CHIPS, 1), (tm, fs),
        pl.BlockSpec((tm, d), lambda i, j, k: (i, 0)), pl.BlockSpec((None, fs, d), lambda i, j, k: (j, 0, 0)),
        [sh4, sh4], [blk4, blk4], swiglu_bwd, extra=(sv["a4"], sv["b4"]), extra_specs=[blk4, blk4], side=side)
    da4, db4 = reduce_later.swap_and_add(ffn_down_bwd) if reduce_later else ffn_down_bwd(None)
    for n, act4 in (("w_ffn_gate", da4), ("w_ffn_up", db4)):
        gr[n] = dw_slabs(f"d{n}_{tag}", sv["h2"], pl.BlockSpec((tk, td), lambda i, j, k: (k, i)), td,
                         act4, pl.BlockSpec((None, tk, fs), lambda i, j, k: (j, k, 0)), fs, (d // td, N_CHIPS),
                         (lambda i, j, k: (j, i, 0), (d, fs)))
    pairs = [(act4, (None, tm, fs), lambda i, kk: (kk, i, 0), wts[n], (None, td, fs), lambda j, kk: (kk, j, 0),
              N_CHIPS) for n, act4 in (("w_ffn_gate", da4), ("w_ffn_up", db4))]
    dh2 = _mm_sum(f"dh_ffn_{tag}", s, d, tm, td, pairs, F32)
    dx1, sums = _prenorm_bwd(f"prenorm_bwd_ffn_{tag}", dh2, sv["x1"], row(p["g_pre_ffn"]), row(mod[4]), dx2)
    d_scale_f, d_shift_f, gr["g_pre_ffn"] = sums[0], sums[1], sums[2]

    dy_mix, sums = _postnorm_bwd(f"postnorm_bwd_mix_{tag}", dx1, sv["y_mix"], row(p["g_post_mix"]), row(mod[2]))
    d_gate_m, gr["g_post_mix"] = sums[0], sums[1]
    gr["w_o"] = dw_slabs(f"dw_o_{tag}", sv["merged"], pl.BlockSpec((tk, cs), lambda i, j, k: (k, i)), cs,
                         dy_mix, pl.BlockSpec((tk, d), lambda i, j, k: (k, 0)), d, (N_CHIPS, 1),
                         (lambda i, j, k: (i, 0, 0), (cs, d)))

    tile = pl.BlockSpec((tm, cs), lambda i, j, k: (i, j))

    def merge_bwd(acc, extra_refs, out_refs):
        a_ref, b_ref, ga_ref, gb_ref = extra_refs
        sa, sb = _sigmoid(ga_ref[...]), _sigmoid(gb_ref[...])
        out_refs[0][...] = (acc * sa).astype(BF16)
        out_refs[1][...] = (acc * sb).astype(BF16)
        out_refs[2][...] = (acc * a_ref[...].astype(F32) * sa * (1.0 - sa)).astype(BF16)
        out_refs[3][...] = (acc * b_ref[...].astype(F32) * sb * (1.0 - sb)).astype(BF16)

    sd_bf = jax.ShapeDtypeStruct((s, d), BF16)
    d_pa, d_pb, d_ga, d_gb = _mm_raw(
        f"out_proj_bwd_{tag}", dy_mix, wts["w_o"], "nt", (s // tm, N_CHIPS, 1), (tm, cs),
        pl.BlockSpec((tm, d), lambda i, j, k: (i, 0)), pl.BlockSpec((None, cs, d), lambda i, j, k: (j, 0, 0)),
        [sd_bf] * 4, [tile] * 4, merge_bwd, extra=(sv["pa"], sv["pb"], sv["fg"], sv["fg"]),
        extra_specs=[tile, tile, tile, pl.BlockSpec((tm, cs), lambda i, j, k: (i, j + N_CHIPS))])
    d_branch = {}
    for n, act, width, d_p in (("w_pa", sv["ys"], w_ssm, d_pa), ("w_pb", sv["ya"], w_att, d_pb)):
        gr[n] = dw_slabs(f"d{n}_{tag}", act, pl.BlockSpec((tk, width), lambda i, j, k: (k, 0)), width,
                         d_p, pl.BlockSpec((tk, cs), lambda i, j, k: (k, j)), cs, (1, N_CHIPS),
                         (lambda i, j, k: (j, 0, 0), (width, cs)))
        d_branch[n] = _mm_raw(
            f"d_in_{n}_{tag}", d_p, wts[n], "nt", (s // tm, 1, N_CHIPS), (tm, width),
            pl.BlockSpec((tm, cs), lambda i, j, k: (i, k)), pl.BlockSpec((None, width, cs), lambda i, j, k: (k, 0, 0)),
            [jax.ShapeDtypeStruct((s, width), BF16)], [pl.BlockSpec((tm, width), lambda i, j, k: (i, 0))], _store(BF16))[0]
    d_ys, d_ya = d_branch["w_pa"], d_branch["w_pb"]

    attn_bwd = lambda side: _attn_bwd(f"attn_bwd_{tag}", sv["uqkv"], *sv["blocks"], w_att // LANES, sv["ya"], d_ya,
                                      sv["lse_rows"], sv["ck_cols"], side=side)
    dq, dk, dv, dcq, dck = reduce_later.exchange_and_sum(attn_bwd) if reduce_later else attn_bwd(None)[0]
    d_f_t, d_bf = _cum_bwd(f"cum_bwd_{tag}", dcq.reshape(heads, s), dck.reshape(heads, s), sv["f_t"],
                           p["b_f"].reshape(heads, 1))
    gr["b_f"] = d_bf[:, 0]

    t5 = min(S5_ROWS, s)
    du_il, d_bblk, d_cblk, d_abar, d_wglu, vec = _s5_bwd(
        f"s5_bwd_{tag}", sv["u_il"], _interleave_rows(d_ys, t5), sv["y_s5"], sv["carries"], p["b_blk"], p["c_blk"],
        p["a_f"], p["a_r"], p["tab_f"], p["tab_r"], row(p["d_skip"]), p["w_glu"], row(p["b_glu"]))
    du = _deinterleave_rows(du_il, t5)
    gr["w_glu"] = d_wglu.astype(BF16).reshape(N_CHIPS, w_ssm // N_CHIPS, w_ssm)
    gr["b_glu"], gr["d_skip"] = vec[0], vec[1]
    gr["b_blk"], gr["c_blk"], gr["a_bar"] = d_bblk, d_cblk, d_abar

    d_f = jnp.pad(d_f_t.T, ((0, 0), (0, F_PAD - heads))).astype(BF16)
    assert w_ssm % w_att == 0 and (2 * d) % F_PAD == 0
    first = w_ssm // w_att
    main_pieces = [(du, w_ssm, 0), (dq, w_att, first), (dk, w_att, first + 1), (dv, w_att, first + 2)]
    dw = [_mm_plain(f"dw_in{n}_{tag}", sv["h"], piece, "tn", BF16, tm=1024, tn=1024, tk=1024)
          for n, piece in enumerate([du, dq, dk, dv, d_f, d_ga, d_gb])]
    w_in_grad = jnp.concatenate(dw[:4] + [dw[4][:, :heads], dw[5], dw[6]], axis=1)
    gr["w_in"] = w_in_grad.reshape(d, N_CHIPS, w_in_grad.shape[1] // N_CHIPS).transpose(1, 0, 2)
    tmx, tkx = _pick(s, 1024), _pick(d, 512)
    pairs = [(piece, (tmx, width), lambda i, kk: (i, 0), p["w_main"], (d, width), lambda j, kk, blk=blk: (j, blk), 1)
             for piece, width, blk in main_pieces]
    steps = d // tkx
    pairs += [(piece, (tmx, tkx), lambda i, kk: (i, kk), p["w_gates"], (d, tkx), lambda j, kk, off=off: (j, off + kk), steps)
              for piece, off in ((d_ga, 0), (d_gb, steps))]
    pairs.append((d_f, (tmx, F_PAD), lambda i, kk: (i, 0), p["w_gates"], (d, F_PAD), lambda j, kk: (j, 2 * d // F_PAD), 1))
    dh_mix = lambda side: _mm_sum(f"dh_mix_{tag}", s, d, tmx, d, pairs, F32, side=side)
    dh1 = reduce_later.share(dh_mix) if reduce_later else dh_mix(None)
    dx0, sums = _prenorm_bwd(f"prenorm_bwd_mix_{tag}", dh1, sv["x"], row(p["g_pre_mix"]), row(mod[1]), dx1)
    d_scale_m, d_shift_m, gr["g_pre_mix"] = sums[0], sums[1], sums[2]

    d_mod = jnp.stack([d_shift_m, d_scale_m, d_gate_m, d_shift_f, d_scale_f, d_gate_f])
    return dx0, d_mod, gr


BIG = ("w_in", "w_glu", "w_pa", "w_pb", "w_o", "w_ffn_gate", "w_ffn_up", "w_ffn_down")
FIRST_USED = ("w_in", "w_glu")
SMALL = ("b_ada", "g_pre_mix", "g_post_mix", "g_pre_ffn", "g_post_ffn", "lam_re", "lam_im", "log_dt", "b_re", "b_im",
         "c_re", "c_im", "d_skip", "b_glu", "b_f")
WEIGHTS = ("w_ada", "b_ada", "g_pre_mix", "g_post_mix", "g_pre_ffn", "g_post_ffn", "w_in", "lam_re", "lam_im", "log_dt",
           "b_re", "b_im", "c_re", "c_im", "d_skip", "w_glu", "b_glu", "b_f", "w_pa", "w_pb", "w_o", "w_ffn_gate",
           "w_ffn_up", "w_ffn_down")


def _prepare_layer(wts, small, l, seq):
    w_in = jnp.concatenate([wts["w_in"][j] for j in range(N_CHIPS)], axis=1)
    d = w_in.shape[0]
    heads = small["b_f"].shape[1]
    n_groups, n_state, group_ch = small["b_re"].shape[1:]
    w_ssm = n_groups * group_ch
    w_att = (w_in.shape[1] - w_ssm - heads - 2 * d) // 3
    n_main = w_ssm + 3 * w_att
    gpb = LANES // group_ch
    p = {"w_att": w_att}
    p["w_main"] = w_in[:, :n_main]
    p["w_gates"] = jnp.concatenate(
        [w_in[:, n_main + heads:], w_in[:, n_main:n_main + heads], jnp.zeros((d, F_PAD - heads), BF16)], axis=1)
    p["w_glu"] = wts["w_glu"].reshape(w_ssm, w_ssm)
    for n in ("g_pre_mix", "g_post_mix", "g_pre_ffn", "g_post_ffn", "d_skip", "b_glu", "b_f"):
        p[n] = small[n][l]
    ar, ai, br, bi = _discretize(small["lam_re"][l], small["lam_im"][l], small["log_dt"][l], small["b_re"][l], small["b_im"][l])
    n_steps = min(S5_ROWS, seq) // SUBLANES
    powers = jnp.cumprod(jnp.broadcast_to(lax.complex(ar, ai).reshape(1, -1), (n_steps, ar.size)), axis=0)
    p["a_f"] = jnp.concatenate([jnp.real(powers), jnp.imag(powers)], axis=1)
    p["a_r"] = jnp.concatenate([jnp.real(powers[::-1]), -jnp.imag(powers[::-1])], axis=1)
    p["tab_f"], p["tab_r"] = _scan_tables(jnp.real(powers[-1]), jnp.imag(powers[-1]))
    bre = _block_diag(br.transpose(0, 2, 1), gpb)
    bim = _block_diag(bi.transpose(0, 2, 1), gpb)
    p["b_blk"] = jnp.concatenate([bre, bim], axis=2).astype(BF16)
    cre = _block_diag(small["c_re"][l].transpose(0, 2, 1), gpb)
    cim = _block_diag(small["c_im"][l].transpose(0, 2, 1), gpb)
    p["c_blk"] = jnp.concatenate([cre, -cim], axis=1).astype(BF16)
    return p


def _compact_partials(gr, n_state, group_ch):
    gpb = LANES // group_ch
    half = gpb * n_state
    out = dict(gr)
    out["bbar_re"] = _block_diag_extract(gr["b_blk"][:, :, :half], gpb, group_ch, n_state).transpose(0, 2, 1)
    out["bbar_im"] = _block_diag_extract(gr["b_blk"][:, :, half:], gpb, group_ch, n_state).transpose(0, 2, 1)
    out["c_re"] = _block_diag_extract(gr["c_blk"][:, :half, :], gpb, n_state, group_ch).transpose(0, 2, 1)
    out["c_im"] = -_block_diag_extract(gr["c_blk"][:, half:, :], gpb, n_state, group_ch).transpose(0, 2, 1)
    return out


def _small_grads_from_partials(gr, small, l):
    n_groups, n_state, _ = small["b_re"].shape[1:]
    ns2 = n_groups * n_state
    d_abar = jnp.sum(gr["a_bar"], axis=0)
    dar, dai = d_abar[:ns2].reshape(n_groups, n_state), d_abar[ns2:].reshape(n_groups, n_state)
    args = (small["lam_re"][l], small["lam_im"][l], small["log_dt"][l], small["b_re"][l], small["b_im"][l])
    _, vjp = jax.vjp(_discretize, *args)
    d_lam_re, d_lam_im, d_log_dt, d_b_re, d_b_im = vjp((dar, dai, gr["bbar_re"], gr["bbar_im"]))
    return dict(lam_re=d_lam_re, lam_im=d_lam_im, log_dt=d_log_dt, b_re=d_b_re, b_im=d_b_im,
                c_re=gr["c_re"], c_im=gr["c_im"])


def _fwd_bwd(xs, target, mods, small, wts0, later, core=None, late0=None):
    depth = 1 + len(later)
    saved, layers, wts = [], [], [wts0]
    act = xs
    for l in range(depth):
        layers.append(_prepare_layer(wts[l], small, l, xs.shape[0]))
        shards = later[l] if l + 1 < depth and not isinstance(later[l], dict) else None
        late = late0 if l == 0 and late0 else ((), [])
        carried = (late[0], late[1], shards or []) if (late[0] or shards) else None
        act, sv, wts[l], gathered = _layer_fwd(str(l), act, mods[l], layers[l], wts[l], carried=carried)
        saved.append(sv)
        if l + 1 < depth:
            wts.append(dict(zip(BIG, _put_own_slabs(gathered, shards))) if shards is not None else later[l])
    dx, loss_blk = _loss_grad("loss", act, target)
    grads, d_mods = [None] * depth, [None] * depth
    pending = None
    for l in reversed(range(depth)):
        dx, d_mods[l], grads[l] = _layer_bwd(str(l), dx, mods[l], layers[l], wts[l], saved[l], reduce_later=pending)
        if core is not None:
            pending = _LayerReduce(str(l), l, depth, [grads[l][n] for n in BIG], core,
                                   into=pending.state if pending is not None else None)
    if core is None:
        return loss_blk, dx, d_mods, grads, None
    pending.swap_and_add()
    pending.exchange_and_sum()
    pending.share()
    return loss_blk, dx, d_mods, grads, dict(zip(BIG, pending.state))


def kernel(x, c, w_ada, b_ada, g_pre_mix, g_post_mix, g_pre_ffn, g_post_ffn, w_in, lam_re, lam_im, log_dt, b_re, b_im, c_re, c_im, d_skip, w_glu, b_glu, b_f, w_pa, w_pb, w_o, w_ffn_gate, w_ffn_up, w_ffn_down, loss_target, m_w_ada, m_b_ada, m_g_pre_mix, m_g_post_mix, m_g_pre_ffn, m_g_post_ffn, m_w_in, m_lam_re, m_lam_im, m_log_dt, m_b_re, m_b_im, m_c_re, m_c_im, m_d_skip, m_w_glu, m_b_glu, m_b_f, m_w_pa, m_w_pb, m_w_o, m_w_ffn_gate, m_w_ffn_up, m_w_ffn_down, v_w_ada, v_b_ada, v_g_pre_mix, v_g_post_mix, v_g_pre_ffn, v_g_post_ffn, v_w_in, v_lam_re, v_lam_im, v_log_dt, v_b_re, v_b_im, v_c_re, v_c_im, v_d_skip, v_w_glu, v_b_glu, v_b_f, v_w_pa, v_w_pb, v_w_o, v_w_ffn_gate, v_w_ffn_up, v_w_ffn_down):
    local = dict(locals())
    weights = {n: local[n] for n in WEIGHTS}
    moments_m = {n: local["m_" + n] for n in WEIGHTS}
    moments_v = {n: local["v_" + n] for n in WEIGHTS}
    depth, d = g_pre_mix.shape
    n_mod = w_ada.shape[2] * N_CHIPS // d
    mx, my, mc = lax.axis_index("x"), lax.axis_index("y"), lax.axis_index("c")
    my_chip = 2 * mx + my
    my_dev = 4 * mx + 2 * my + mc
    xs = x[0]

    shards = [[weights[n][l].astype(BF16) for n in BIG] for l in range(depth)]
    early = [i for i, n in enumerate(BIG) if n in FIRST_USED]
    late = [i for i, n in enumerate(BIG) if n not in FIRST_USED]
    wts0 = dict(zip([BIG[i] for i in early], _gather_layer("gather_weights_0", [shards[0][i] for i in early])))
    late0 = ([BIG[i] for i in late], [shards[0][i] for i in late])
    small = {n: weights[n] for n in SMALL}

    c_pad = jnp.pad(c, ((0, SUBLANES - 1), (0, 0)))
    c_all = _all_gather("gather_cond", c_pad).reshape(N_DEV, SUBLANES, d)[:, 0, :]
    silu = lambda v: v * _sigmoid(v)
    n_cols = w_ada.shape[2]
    mod_shard = []
    for l in range(depth):
        bias = lax.dynamic_slice_in_dim(b_ada[l], my_chip * n_cols, n_cols)
        mod_shard.append(_mm_plain(f"ada_{l}", c_all, w_ada[l], "nn", F32, add=jnp.broadcast_to(bias, (N_DEV, n_cols)),
                                   a_fn=silu, tm=N_DEV, tn=512, tk=1024))
    mod_block = jnp.concatenate(mod_shard, axis=1)
    mod_all = _all_gather("gather_mod", mod_block).reshape(N_DEV, N_DEV, depth, n_cols)
    mod_rows = lax.dynamic_index_in_dim(mod_all[0::2], my_dev, axis=1, keepdims=False)
    mods = [mod_rows[:, l, :].reshape(n_mod, d) for l in range(depth)]

    loss_blk, dx, d_mods, grads, big_grads = _fwd_bwd(xs, loss_target[0], mods, small, wts0, shards[1:],
                                                      core=mc.astype(jnp.int32).reshape(1), late0=late0)
    loss = lax.psum(loss_blk[0, 0], ("x", "y", "c"))
    grad_x = dx[None]

    partial_names = ("g_pre_mix", "g_post_mix", "g_pre_ffn", "g_post_ffn", "d_skip", "b_glu", "b_f", "a_bar",
                     "bbar_re", "bbar_im", "c_re", "c_im")
    n_state, group_ch = b_re.shape[2:]
    contrib = list(d_mods)
    for l in range(depth):
        compact = _compact_partials(grads[l], n_state, group_ch)
        contrib += [compact[n] for n in partial_names]
    contrib_shapes = [a.shape for a in contrib]
    block = _pack(contrib, LANES, BF16_ROWS, F32)
    rows = block.shape[0]
    all_blocks = _all_gather("gather_small_grads", block).reshape(N_DEV, rows, LANES)
    summed = _unpack(_sum_blocks("sum_small_grads", all_blocks, F32), contrib_shapes)
    per_layer = len(partial_names)
    small_grads = {n: [] for n in SMALL}
    d_mod_all = []
    for l in range(depth):
        small_grads["b_ada"].append(summed[l].reshape(-1))
        gl = dict(zip(partial_names, summed[depth + l * per_layer:depth + (l + 1) * per_layer]))
        for n in ("g_pre_mix", "g_post_mix", "g_pre_ffn", "g_post_ffn", "d_skip", "b_glu", "b_f"):
            small_grads[n].append(gl[n])
        for n, gval in _small_grads_from_partials(gl, small, l).items():
            small_grads[n].append(gval)
        mod_rows_ = n_mod * d // LANES
        d_mod_all.append(all_blocks[:, l * mod_rows_:(l + 1) * mod_rows_, :].reshape(N_DEV, n_mod * d))
    small_grads = {n: jnp.stack(v) for n, v in small_grads.items()}

    g_w_ada = []
    for l in range(depth):
        cols = lax.dynamic_slice_in_dim(d_mod_all[l], my_chip * n_cols, n_cols, axis=1)
        g_w_ada.append(_mm_plain(f"dw_ada_{l}", c_all, cols, "tn", F32, a_fn=silu, tm=512, tn=512, tk=N_DEV))
    all_grads = dict(big_grads)
    all_grads.update(small_grads)
    all_grads["w_ada"] = jnp.stack(g_w_ada)

    delta, new_m, new_v = {}, {}, {}
    for n in ("w_ada",) + BIG:
        last = weights[n].shape[2]
        to_stored, from_stored = ((0, 1, 2),) * 2 if last % LANES == 0 else ((0, 2, 1),) * 2 if last % SUBLANES == 0 \
            else ((2, 0, 1), (1, 2, 0))
        view, back = (lambda a: a.transpose(to_stored)), (lambda a: a.transpose(from_stored))
        outs = _adamw(f"adamw_{n}", view(weights[n]), view(all_grads[n]), view(moments_m[n]), view(moments_v[n]))
        delta[n], new_m[n], new_v[n] = (back(o) for o in outs)
    small_shapes = [weights[n].shape for n in SMALL]
    packed = [_pack([src[n] for n in SMALL], LANES, SUBLANES, F32)[None] for src in (weights, all_grads, moments_m, moments_v)]
    outs = _adamw("adamw_small", *packed)
    for dst, buf in zip((delta, new_m, new_v), outs):
        dst.update(dict(zip(SMALL, _unpack(buf[0], small_shapes))))

    return (loss, grad_x, *[all_grads[n] for n in WEIGHTS], *[delta[n] for n in WEIGHTS],
            *[new_m[n] for n in WEIGHTS], *[new_v[n] for n in WEIGHTS])
```

```python
import math

import jax
import jax.numpy as jnp
from jax import lax
from jax.experimental import pallas as pl
from jax.experimental.pallas import tpu as pltpu

F32 = jnp.float32
BF16 = jnp.bfloat16
MESH = pl.DeviceIdType.MESH

RMS_EPS = 1e-6
EIG_CLIP = 1e-4
ADAM_LR, ADAM_B1, ADAM_B2, ADAM_EPS, ADAM_WD, ADAM_STEP = 0.001, 0.9, 0.999, 1e-08, 0.01, 10

LANES = 128
SUBLANES = 8
VMEM_LIMIT = 56 * 1024 * 1024
ROW_TILE_BYTES = 1 << 20
SUM_TILE_BYTES = 1 << 19
S5_ROWS = 256
S5_CHUNK = 1024
S5_UNROLL = 4
ATT_BLOCK = 512
F_PAD = 256
POSTNORM_ROWS = 1024
N_CHIPS = 4
N_DEV = 8

NN = (((1,), (0,)), ((), ()))
NT = (((1,), (1,)), ((), ()))
TN = (((0,), (0,)), ((), ()))
_DN = {"nn": NN, "nt": NT, "tn": TN}


def _cparams(**kw):
    return pltpu.CompilerParams(vmem_limit_bytes=VMEM_LIMIT, **kw)


def _pick(dim, target):
    best, t = None, LANES
    while t <= min(dim, target):
        if dim % t == 0:
            best = t
        t += LANES
    return best or dim


def _sigmoid(x):
    return 1.0 / (1.0 + jnp.exp(-x))


def _dot(a, b, dn):
    return lax.dot_general(a, b, dn, preferred_element_type=F32)


def _mm_raw(name, a, b, mode, grid, acc_shape, a_spec, b_spec, out_shapes, out_specs, epilogue,
            extra=(), extra_specs=(), a_fn=None, side=None):
    nk = grid[2]
    n_extra, n_out = len(extra), len(out_shapes)

    def body(*refs):
        a_ref, b_ref = refs[0], refs[1]
        extra_refs = refs[2:2 + n_extra]
        out_refs = refs[2 + n_extra:2 + n_extra + n_out]
        acc = refs[-1]
        k = pl.program_id(2)

        @pl.when(k == 0)
        def _():
            acc[...] = jnp.zeros_like(acc)

        av = a_ref[...]
        if a_fn is not None:
            av = a_fn(av.astype(F32))
        acc[...] += _dot(av.astype(BF16), b_ref[...].astype(BF16), _DN[mode])

        @pl.when(k == nk - 1)
        def _():
            epilogue(acc[...], extra_refs, out_refs)

    outs, side_outs = _hosted_call(body, side, name, grid, [a_spec, b_spec, *extra_specs], list(out_specs),
                                   list(out_shapes), [pltpu.VMEM(acc_shape, F32)], (a, b, *extra))
    return outs if side is None else (outs, side_outs)


def _mm(name, a, b, mode, out_shapes, out_specs, epilogue, extra=(), extra_specs=(),
        tm=512, tn=512, tk=512, a_fn=None):
    if mode == "nn":
        (m, kd), (_, n) = a.shape, b.shape
    elif mode == "nt":
        (m, kd), (n, _) = a.shape, b.shape
    else:
        (kd, m), (_, n) = a.shape, b.shape
    tm, tn, tk = _pick(m, tm), _pick(n, tn), _pick(kd, tk)
    if mode == "tn":
        a_spec = pl.BlockSpec((tk, tm), lambda i, j, k: (k, i))
    else:
        a_spec = pl.BlockSpec((tm, tk), lambda i, j, k: (i, k))
    if mode == "nt":
        b_spec = pl.BlockSpec((tn, tk), lambda i, j, k: (j, k))
    else:
        b_spec = pl.BlockSpec((tk, tn), lambda i, j, k: (k, j))
    res = _mm_raw(name, a, b, mode, (m // tm, n // tn, kd // tk), (tm, tn), a_spec, b_spec, out_shapes, out_specs,
                  epilogue, extra=extra, extra_specs=extra_specs, a_fn=a_fn)
    return res, (tm, tn, tk)


def _store(dtype):
    def epilogue(acc, extra_refs, out_refs):
        out_refs[0][...] = acc.astype(dtype)
    return epilogue


def _mm_sum(name, m, n, tm, tn, pairs, out_dtype, side=None):
    offs, total = [], 0
    for pr in pairs:
        offs.append(total)
        total += pr[6]
    n_p = len(pairs)

    def body(*refs):
        o_ref, acc = refs[2 * n_p], refs[2 * n_p + 1]
        k = pl.program_id(2)

        @pl.when(k == 0)
        def _():
            acc[...] = jnp.zeros_like(acc)

        for p_ in range(n_p):
            @pl.when((k >= offs[p_]) & (k < offs[p_] + pairs[p_][6]))
            def _(p_=p_):
                acc[...] += _dot(refs[2 * p_][...].astype(BF16), refs[2 * p_ + 1][...].astype(BF16), NT)

        @pl.when(k == total - 1)
        def _():
            o_ref[...] = acc[...].astype(out_dtype)

    in_specs, operands = [], []
    for (a, a_block, a_index, b, b_block, b_index, steps), off in zip(pairs, offs):
        local = lambda k, off=off, steps=steps: jnp.clip(k - off, 0, steps - 1)
        in_specs.append(pl.BlockSpec(a_block, lambda i, j, k, f=a_index, local=local: f(i, local(k))))
        in_specs.append(pl.BlockSpec(b_block, lambda i, j, k, f=b_index, local=local: f(j, local(k))))
        operands += [a, b]
    (out,), side_outs = _hosted_call(
        body, side, name, (m // tm, n // tn, total), in_specs, [pl.BlockSpec((tm, tn), lambda i, j, k: (i, j))],
        [jax.ShapeDtypeStruct((m, n), out_dtype)], [pltpu.VMEM((tm, tn), F32)], operands)
    return out if side is None else (out, side_outs)


class _SideJob:
    def __init__(self, arrays, out_shapes, aliases, n_sems, copies):
        self.arrays, self.out_shapes, self.aliases, self.n_sems, self.copies = arrays, out_shapes, aliases, n_sems, copies


def _hosted_call(body, side, name, grid, in_specs, out_specs, out_shape, scratch_shapes, operands):
    if side is None:
        outs = pl.pallas_call(body, name=name, grid=grid, in_specs=in_specs, out_specs=out_specs, out_shape=out_shape,
                              scratch_shapes=scratch_shapes, compiler_params=_cparams())(*operands)
        return outs, []
    n_in, n_out, ns_in, ns_out = len(in_specs), len(out_specs), len(side.arrays), len(side.out_shapes)

    def wrapped(*refs):
        main_in, side_in = refs[:n_in], refs[n_in:n_in + ns_in]
        rest = refs[n_in + ns_in:]
        main_out, side_out, rest = rest[:n_out], rest[n_out:n_out + ns_out], rest[n_out + ns_out:]
        scratch, send_sems, recv_sems = rest[:-2], rest[-2], rest[-1]
        first, last = None, None
        for axis, extent in enumerate(grid):
            at_start, at_end = pl.program_id(axis) == 0, pl.program_id(axis) == extent - 1
            first = at_start if first is None else first & at_start
            last = at_end if last is None else last & at_end

        @pl.when(first)
        def _():
            for cp in side.copies(side_in, side_out, send_sems, recv_sems):
                cp.start()

        body(*main_in, *main_out, *scratch)

        @pl.when(last)
        def _():
            for cp in side.copies(side_in, side_out, send_sems, recv_sems):
                cp.wait()

    hbm = pl.BlockSpec(memory_space=pl.ANY)
    outs = pl.pallas_call(
        wrapped, name=name, grid=grid, in_specs=list(in_specs) + [hbm] * ns_in,
        out_specs=list(out_specs) + [hbm] * ns_out, out_shape=list(out_shape) + list(side.out_shapes),
        scratch_shapes=list(scratch_shapes) + [pltpu.SemaphoreType.DMA((side.n_sems,))] * 2,
        input_output_aliases={n_in + i: n_out + o for i, o in side.aliases.items()},
        compiler_params=_cparams(),
    )(*operands, *side.arrays)
    return outs[:n_out], outs[n_out:]


def _ffn_up(name, h, wg, wu, side=None):
    s, d = h.shape
    nc, fs = wg.shape[0], wg.shape[2]
    tm, tk = _pick(s, 1024), _pick(d, 1024)
    nk = d // tk

    def body(h_ref, wg_ref, wu_ref, a_ref, b_ref, hid_ref, acc_g, acc_u):
        k = pl.program_id(2)

        @pl.when(k == 0)
        def _():
            acc_g[...] = jnp.zeros_like(acc_g)
            acc_u[...] = jnp.zeros_like(acc_u)

        hv = h_ref[...]
        acc_g[...] += _dot(hv, wg_ref[...], NN)
        acc_u[...] += _dot(hv, wu_ref[...], NN)

        @pl.when(k == nk - 1)
        def _():
            av, bv = acc_g[...], acc_u[...]
            a_ref[...] = av.astype(BF16)
            b_ref[...] = bv.astype(BF16)
            hid_ref[...] = (av * _sigmoid(av) * bv).astype(BF16)

    w_spec = pl.BlockSpec((None, tk, fs), lambda i, j, k: (j, k, 0))
    o_spec = pl.BlockSpec((None, tm, fs), lambda i, j, k: (j, i, 0))
    sh = jax.ShapeDtypeStruct((nc, s, fs), BF16)
    return _hosted_call(
        body, side, name, (s // tm, nc, nk), [pl.BlockSpec((tm, tk), lambda i, j, k: (i, k)), w_spec, w_spec],
        [o_spec] * 3, [sh] * 3, [pltpu.VMEM((tm, fs), F32), pltpu.VMEM((tm, fs), F32)], (h, wg, wu))


def _mm_plain(name, a, b, mode, out_dtype, add=None, a_fn=None, tm=512, tn=512, tk=512):
    if mode == "nn":
        m, n = a.shape[0], b.shape[1]
    elif mode == "nt":
        m, n = a.shape[0], b.shape[0]
    else:
        m, n = a.shape[1], b.shape[1]
    tm_, tn_ = _pick(m, tm), _pick(n, tn)
    spec = pl.BlockSpec((tm_, tn_), lambda i, j, k: (i, j))

    def epilogue(acc, extra_refs, out_refs):
        if add is not None:
            acc = acc + extra_refs[0][...]
        out_refs[0][...] = acc.astype(out_dtype)

    extra = () if add is None else (add,)
    (out,), _ = _mm(name, a, b, mode, [jax.ShapeDtypeStruct((m, n), out_dtype)], [spec], epilogue,
                    extra=extra, extra_specs=[spec] * len(extra), tm=tm, tn=tn, tk=tk, a_fn=a_fn)
    return out


def _row_tile(s, d):
    return _pick(s, max(SUBLANES, ROW_TILE_BYTES // (4 * d)))


def _prenorm_fwd(name, x, g, scale, shift):
    s, d = x.shape
    tr = _row_tile(s, d)

    def body(x_ref, g_ref, sc_ref, sh_ref, h_ref):
        xv = x_ref[...]
        r = lax.rsqrt(jnp.mean(xv * xv, axis=-1, keepdims=True) + RMS_EPS)
        h_ref[...] = ((xv * r * g_ref[...]) * (1.0 + sc_ref[...]) + sh_ref[...]).astype(BF16)

    row = pl.BlockSpec((tr, d), lambda i: (i, 0))
    vec = pl.BlockSpec((1, d), lambda i: (0, 0))
    return pl.pallas_call(body, name=name, grid=(s // tr,), in_specs=[row, vec, vec, vec], out_specs=row,
                          out_shape=jax.ShapeDtypeStruct((s, d), BF16), compiler_params=_cparams())(x, g, scale, shift)


def _prenorm_bwd(name, dh, x, g, scale, dx_res):
    s, d = x.shape
    tr = _row_tile(s, d)

    def body(dh_ref, x_ref, g_ref, sc_ref, dxr_ref, dx_ref, sums_ref):
        @pl.when(pl.program_id(0) == 0)
        def _():
            sums_ref[...] = jnp.zeros_like(sums_ref)

        xv, dhv, gv = x_ref[...], dh_ref[...].astype(F32), g_ref[...]
        r = lax.rsqrt(jnp.mean(xv * xv, axis=-1, keepdims=True) + RMS_EPS)
        xhat = xv * r
        dxn = dhv * (1.0 + sc_ref[...])
        dxhat = dxn * gv
        dx = r * (dxhat - xhat * jnp.mean(dxhat * xhat, axis=-1, keepdims=True))
        dx_ref[...] = dxr_ref[...] + dx
        sums_ref[0:1, :] += jnp.sum(dhv * (xhat * gv), axis=0, keepdims=True)
        sums_ref[1:2, :] += jnp.sum(dhv, axis=0, keepdims=True)
        sums_ref[2:3, :] += jnp.sum(dxn * xhat, axis=0, keepdims=True)

    row = pl.BlockSpec((tr, d), lambda i: (i, 0))
    vec = pl.BlockSpec((1, d), lambda i: (0, 0))
    acc = pl.BlockSpec((SUBLANES, d), lambda i: (0, 0))
    return pl.pallas_call(
        body, name=name, grid=(s // tr,), in_specs=[row, row, vec, vec, row], out_specs=[row, acc],
        out_shape=[jax.ShapeDtypeStruct((s, d), F32), jax.ShapeDtypeStruct((SUBLANES, d), F32)],
        compiler_params=_cparams())(dh, x, g, scale, dx_res)


def _postnorm_bwd(name, dxn, y, g, gate):
    s, d = y.shape
    tr = _row_tile(s, d)

    def body(dx_ref, y_ref, g_ref, gt_ref, dy_ref, sums_ref):
        @pl.when(pl.program_id(0) == 0)
        def _():
            sums_ref[...] = jnp.zeros_like(sums_ref)

        yv, dxv, gv = y_ref[...], dx_ref[...], g_ref[...]
        r = lax.rsqrt(jnp.mean(yv * yv, axis=-1, keepdims=True) + RMS_EPS)
        yhat = yv * r
        dn = dxv * gt_ref[...]
        dyhat = dn * gv
        dy_ref[...] = (r * (dyhat - yhat * jnp.mean(dyhat * yhat, axis=-1, keepdims=True))).astype(BF16)
        sums_ref[0:1, :] += jnp.sum(dxv * (yhat * gv), axis=0, keepdims=True)
        sums_ref[1:2, :] += jnp.sum(dn * yhat, axis=0, keepdims=True)

    row = pl.BlockSpec((tr, d), lambda i: (i, 0))
    vec = pl.BlockSpec((1, d), lambda i: (0, 0))
    acc = pl.BlockSpec((SUBLANES, d), lambda i: (0, 0))
    return pl.pallas_call(
        body, name=name, grid=(s // tr,), in_specs=[row, row, vec, vec], out_specs=[row, acc],
        out_shape=[jax.ShapeDtypeStruct((s, d), BF16), jax.ShapeDtypeStruct((SUBLANES, d), F32)],
        compiler_params=_cparams())(dxn, y, g, gate)


def _loss_grad(name, y, target):
    s, d = y.shape
    tr = _row_tile(s, d)

    def body(y_ref, t_ref, dy_ref, loss_ref):
        @pl.when(pl.program_id(0) == 0)
        def _():
            loss_ref[...] = jnp.zeros_like(loss_ref)

        err = y_ref[...] - t_ref[...]
        dy_ref[...] = err * (1.0 / d)
        part = jnp.sum(jnp.sum(err * err, axis=-1, keepdims=True), axis=0, keepdims=True) * (0.5 / d)
        loss_ref[...] += jnp.broadcast_to(part, loss_ref.shape)

    row = pl.BlockSpec((tr, d), lambda i: (i, 0))
    acc = pl.BlockSpec((SUBLANES, LANES), lambda i: (0, 0))
    return pl.pallas_call(
        body, name=name, grid=(s // tr,), in_specs=[row, row], out_specs=[row, acc],
        out_shape=[jax.ShapeDtypeStruct((s, d), F32), jax.ShapeDtypeStruct((SUBLANES, LANES), F32)],
        compiler_params=_cparams())(y, target)


def _gelu(y):
    c = math.sqrt(2.0 / math.pi)
    return 0.5 * y * (1.0 + jnp.tanh(c * (y + 0.044715 * (y * y * y))))


def _gelu_grad(y):
    c = math.sqrt(2.0 / math.pi)
    th = jnp.tanh(c * (y + 0.044715 * (y * y * y)))
    return 0.5 * (1.0 + th) + 0.5 * y * (1.0 - th * th) * c * (1.0 + 3.0 * 0.044715 * (y * y))


def _cmul_add(br, bi, ar, ai, xr, xi):
    return br + ar * xr - ai * xi, bi + ar * xi + ai * xr


def _scan_rows(x_ref, row0, n_steps, ns2, pow_ref, tab_ref, carry_ref, reverse, fold=None):
    assert n_steps % SUBLANES == 0
    wc = min(S5_CHUNK, ns2)
    sub = lax.broadcasted_iota(jnp.int32, (SUBLANES, wc), 0)
    unroll = S5_UNROLL if n_steps % S5_UNROLL == 0 else 1
    for c0 in range(0, ns2, wc):
        re = slice(c0, c0 + wc)
        im = slice(ns2 + c0, ns2 + c0 + wc)
        first_power = slice(n_steps - 1, n_steps) if reverse else slice(0, 1)
        ar = jnp.broadcast_to(pow_ref[first_power, re], (SUBLANES, wc))
        ai = jnp.broadcast_to(pow_ref[first_power, im], (SUBLANES, wc))
        rows = lambda r: pl.ds(pl.multiple_of(row0 + r * SUBLANES, SUBLANES), SUBLANES)
        step_of = lambda i: (n_steps - 1 - i) if reverse else i

        def local(i, carry, re=re, im=im, ar=ar, ai=ai):
            for u in range(unroll):
                r = step_of(i * unroll + u)
                carry = _cmul_add(x_ref[rows(r), re], x_ref[rows(r), im], ar, ai, *carry)
                x_ref[rows(r), re], x_ref[rows(r), im] = carry
            return carry

        zero = jnp.zeros((SUBLANES, wc), F32)
        lr, li = lax.fori_loop(0, n_steps // unroll, local, (zero, zero))

        tabs = [tab_ref[k, :, re] for k in range(8)]
        for lvl, k in enumerate((1, 2, 4)):
            sh = (SUBLANES - k) if reverse else k
            lr, li = _cmul_add(lr, li, tabs[2 * lvl], tabs[2 * lvl + 1], pltpu.roll(lr, sh, 0), pltpu.roll(li, sh, 0))
        cr, ci = carry_ref[0:1, re], carry_ref[0:1, im]
        lr, li = _cmul_add(lr, li, tabs[6], tabs[7], cr, ci)
        edge, away, last = (SUBLANES - 1, SUBLANES - 1, 0) if reverse else (0, 1, SUBLANES - 1)
        carry_ref[0:1, re] = lr[last:last + 1, :]
        carry_ref[0:1, im] = li[last:last + 1, :]
        er = jnp.where(sub == edge, cr, pltpu.roll(lr, away, 0))
        ei = jnp.where(sub == edge, ci, pltpu.roll(li, away, 0))

        def fix(j, acc, re=re, im=im, er=er, ei=ei, c0=c0):
            base = pl.ds(pl.multiple_of(j * SUBLANES, SUBLANES), SUBLANES)
            pw_r, pw_i = pow_ref[base, re], pow_ref[base, im]
            for i in range(SUBLANES):
                r = j * SUBLANES + i
                xr, xi = _cmul_add(x_ref[rows(r), re], x_ref[rows(r), im], pw_r[i:i + 1, :], pw_i[i:i + 1, :], er, ei)
                x_ref[rows(r), re], x_ref[rows(r), im] = xr, xi
                if fold is not None:
                    acc = fold(c0, r, xr, xi, acc)
            return acc

        acc = lax.fori_loop(0, n_steps // SUBLANES, fix, (zero, zero) if fold is not None else 0)
        if fold is not None:
            fold(c0, None, None, None, acc)


def _s5_fwd(name, u, b_blk, c_blk, a_f, tab_f, dskip, w_glu, b_glu):
    s, w = u.shape[0], w_glu.shape[0]
    nkb = w // LANES
    ns2 = b_blk.shape[2] // 2 * nkb
    half = ns2 // nkb
    t = min(S5_ROWS, s)
    nblk = s // t

    def body(u_ref, b_ref, c_ref, a_ref, tab_ref, ds_ref, wg_ref, bg_ref, y_ref, ys_ref, cs_ref, xs, carry):
        @pl.when(pl.program_id(0) == 0)
        def _():
            carry[...] = jnp.zeros_like(carry)

        cs_ref[0] = carry[...]
        for kb in range(nkb):
            bu = _dot(u_ref[:, kb * LANES:(kb + 1) * LANES], b_ref[kb], NN)
            xs[:, kb * half:(kb + 1) * half] = bu[:, :half]
            xs[:, ns2 + kb * half:ns2 + (kb + 1) * half] = bu[:, half:]
        _scan_rows(xs, 0, t // SUBLANES, ns2, a_ref, tab_ref, carry, reverse=False)
        for kb in range(nkb):
            cols = slice(kb * LANES, (kb + 1) * LANES)
            yk = _dot(xs[:, kb * half:(kb + 1) * half].astype(BF16), c_ref[kb, :half, :], NN)
            yk += _dot(xs[:, ns2 + kb * half:ns2 + (kb + 1) * half].astype(BF16), c_ref[kb, half:, :], NN)
            y_ref[:, cols] = yk + ds_ref[:, cols] * u_ref[:, cols].astype(F32)
        z = _gelu(y_ref[...])
        gate = _sigmoid(_dot(z.astype(BF16), wg_ref[...], NN) + bg_ref[...])
        ys_ref[...] = (z * gate).astype(BF16)

    row = pl.BlockSpec((t, w), lambda i: (i, 0))
    full = lambda shape: pl.BlockSpec(shape, lambda i: (0,) * len(shape))
    return pl.pallas_call(
        body, name=name, grid=(nblk,),
        in_specs=[row, full(b_blk.shape), full(c_blk.shape), full(a_f.shape), full(tab_f.shape), full(dskip.shape),
                  full(w_glu.shape), full(b_glu.shape)],
        out_specs=[row, row, pl.BlockSpec((1, 1, 2 * ns2), lambda i: (i, 0, 0))],
        out_shape=[jax.ShapeDtypeStruct((s, w), F32), jax.ShapeDtypeStruct((s, w), BF16),
                   jax.ShapeDtypeStruct((nblk, 1, 2 * ns2), F32)],
        scratch_shapes=[pltpu.VMEM((t, 2 * ns2), F32), pltpu.VMEM((1, 2 * ns2), F32)],
        compiler_params=_cparams(),
    )(u, b_blk, c_blk, a_f, tab_f, dskip, w_glu, b_glu)


def _s5_bwd(name, u, dys, y, carries, b_blk, c_blk, a_f, a_r, tab_f, tab_r, dskip, w_glu, b_glu):
    s, w = u.shape[0], w_glu.shape[0]
    nkb = w // LANES
    ns2 = b_blk.shape[2] // 2 * nkb
    half = ns2 // nkb
    t = min(S5_ROWS, s)
    nblk = s // t
    ng = t // SUBLANES

    def body(u_ref, dys_ref, y_ref, cs_ref, b_ref, c_ref, af_ref, ar_ref, tabf_ref, tabr_ref, ds_ref, wg_ref, bg_ref,
             du_ref, db_ref, dc_ref, da_ref, dwg_ref, vec_ref, xs, gs, dyv, fcarry, gcarry):
        @pl.when(pl.program_id(0) == 0)
        def _():
            db_ref[...] = jnp.zeros_like(db_ref)
            dc_ref[...] = jnp.zeros_like(dc_ref)
            da_ref[...] = jnp.zeros_like(da_ref)
            dwg_ref[...] = jnp.zeros_like(dwg_ref)
            vec_ref[...] = jnp.zeros_like(vec_ref)
            gcarry[...] = jnp.zeros_like(gcarry)

        yv = y_ref[...]
        z = _gelu(yv)
        zb = z.astype(BF16)
        gate = _sigmoid(_dot(zb, wg_ref[...], NN) + bg_ref[...])
        dout = dys_ref[...].astype(F32)
        dt = dout * z * gate * (1.0 - gate)
        dtb = dt.astype(BF16)
        dz = dout * gate + _dot(dtb, wg_ref[...], NT)
        dy = dz * _gelu_grad(yv)
        dyv[...] = dy
        dwg_ref[...] += _dot(zb, dtb, TN)
        vec_ref[0:1, :] += jnp.sum(dt, axis=0, keepdims=True)
        vec_ref[1:2, :] += jnp.sum(dy * u_ref[...].astype(F32), axis=0, keepdims=True)

        fcarry[...] = cs_ref[0]
        xs[0:SUBLANES, :] = jnp.broadcast_to(cs_ref[0], (SUBLANES, 2 * ns2))
        for kb in range(nkb):
            bu = _dot(u_ref[:, kb * LANES:(kb + 1) * LANES], b_ref[kb], NN)
            xs[SUBLANES:, kb * half:(kb + 1) * half] = bu[:, :half]
            xs[SUBLANES:, ns2 + kb * half:ns2 + (kb + 1) * half] = bu[:, half:]
        _scan_rows(xs, SUBLANES, ng, ns2, af_ref, tabf_ref, fcarry, reverse=False)
        first_segment = lax.broadcasted_iota(jnp.int32, (SUBLANES, 2 * ns2), 0) == 0
        xs[0:SUBLANES, :] = jnp.where(first_segment, xs[0:SUBLANES, :], pltpu.roll(xs[t:t + SUBLANES, :], 1, 0))

        for kb in range(nkb):
            dyk = dyv[:, kb * LANES:(kb + 1) * LANES].astype(BF16)
            re = slice(kb * half, (kb + 1) * half)
            im = slice(ns2 + kb * half, ns2 + (kb + 1) * half)
            gs[:, re] = _dot(dyk, c_ref[kb, :half, :], NT)
            gs[:, im] = _dot(dyk, c_ref[kb, half:, :], NT)
            dc_ref[kb, :half, :] += _dot(xs[SUBLANES:, re].astype(BF16), dyk, TN)
            dc_ref[kb, half:, :] += _dot(xs[SUBLANES:, im].astype(BF16), dyk, TN)

        def fold(c0, r, gr, gi, acc):
            wc = min(S5_CHUNK, ns2)
            re = slice(c0, c0 + wc)
            im = slice(ns2 + c0, ns2 + c0 + wc)
            if r is None:
                da_ref[:, re] += acc[0]
                da_ref[:, im] += acc[1]
                return acc
            before = pl.ds(pl.multiple_of(r * SUBLANES, SUBLANES), SUBLANES)
            xpr, xpi = xs[before, re], xs[before, im]
            return acc[0] + gr * xpr + gi * xpi, acc[1] - gr * xpi + gi * xpr

        _scan_rows(gs, 0, ng, ns2, ar_ref, tabr_ref, gcarry, reverse=True, fold=fold)

        for kb in range(nkb):
            cols = slice(kb * LANES, (kb + 1) * LANES)
            re = slice(kb * half, (kb + 1) * half)
            im = slice(ns2 + kb * half, ns2 + (kb + 1) * half)
            uk = u_ref[:, cols]
            gr = gs[:, re].astype(BF16)
            gi = gs[:, im].astype(BF16)
            db_ref[kb, :, :half] += _dot(uk, gr, TN)
            db_ref[kb, :, half:] += _dot(uk, gi, TN)
            duk = _dot(gr, b_ref[kb, :, :half], NT) + _dot(gi, b_ref[kb, :, half:], NT)
            du_ref[:, cols] = (duk + ds_ref[:, cols] * dyv[:, cols]).astype(BF16)

    rev = lambda i: (nblk - 1 - i, 0)
    row = pl.BlockSpec((t, w), rev)
    full = lambda shape: pl.BlockSpec(shape, lambda i: (0,) * len(shape))
    return pl.pallas_call(
        body, name=name, grid=(nblk,),
        in_specs=[row, row, row, pl.BlockSpec((1, 1, 2 * ns2), lambda i: (nblk - 1 - i, 0, 0)),
                  full(b_blk.shape), full(c_blk.shape), full(a_f.shape), full(a_r.shape), full(tab_f.shape),
                  full(tab_r.shape), full(dskip.shape), full(w_glu.shape), full(b_glu.shape)],
        out_specs=[row, full(b_blk.shape), full(c_blk.shape), full((SUBLANES, 2 * ns2)), full((w, w)),
                   full((SUBLANES, w))],
        out_shape=[jax.ShapeDtypeStruct((s, w), BF16), jax.ShapeDtypeStruct(b_blk.shape, F32),
                   jax.ShapeDtypeStruct(c_blk.shape, F32), jax.ShapeDtypeStruct((SUBLANES, 2 * ns2), F32),
                   jax.ShapeDtypeStruct((w, w), F32), jax.ShapeDtypeStruct((SUBLANES, w), F32)],
        scratch_shapes=[pltpu.VMEM((t + SUBLANES, 2 * ns2), F32), pltpu.VMEM((t, 2 * ns2), F32),
                        pltpu.VMEM((t, w), F32), pltpu.VMEM((1, 2 * ns2), F32), pltpu.VMEM((1, 2 * ns2), F32)],
        compiler_params=_cparams(),
    )(u, dys, y, carries, b_blk, c_blk, a_f, a_r, tab_f, tab_r, dskip, w_glu, b_glu)


def _log_sigmoid(x):
    return jnp.minimum(x, 0.0) - jnp.log(1.0 + jnp.exp(-jnp.abs(x)))


def _cum_fwd(name, f_t, b_f):
    h, s = f_t.shape
    tc = _pick(s, 512)
    nb = s // tc

    def body(f_ref, b_ref, c_ref, carry):
        @pl.when(pl.program_id(0) == 0)
        def _():
            carry[...] = jnp.zeros_like(carry)

        lf = _log_sigmoid(f_ref[...] + b_ref[...])
        upper = (lax.broadcasted_iota(jnp.int32, (tc, tc), 0) <= lax.broadcasted_iota(jnp.int32, (tc, tc), 1))
        cum = lax.dot_general(lf, upper.astype(F32), NN, precision=lax.Precision.HIGHEST,
                              preferred_element_type=F32) + carry[...]
        c_ref[...] = cum
        carry[...] += jnp.sum(lf, axis=1, keepdims=True)

    blk = pl.BlockSpec((h, tc), lambda i: (0, i))
    return pl.pallas_call(body, name=name, grid=(nb,), in_specs=[blk, pl.BlockSpec((h, 1), lambda i: (0, 0))],
                          out_specs=blk, out_shape=jax.ShapeDtypeStruct((h, s), F32),
                          scratch_shapes=[pltpu.VMEM((h, 1), F32)], compiler_params=_cparams())(f_t, b_f)


def _cum_bwd(name, dcq, dck, f_t, b_f):
    h, s = f_t.shape
    tc = _pick(s, 512)
    nb = s // tc

    def body(dcq_ref, dck_ref, f_ref, b_ref, df_ref, db_ref, carry):
        @pl.when(pl.program_id(0) == 0)
        def _():
            carry[...] = jnp.zeros_like(carry)
            db_ref[...] = jnp.zeros_like(db_ref)

        dc = dcq_ref[...] + dck_ref[...]
        lower = (lax.broadcasted_iota(jnp.int32, (tc, tc), 0) >= lax.broadcasted_iota(jnp.int32, (tc, tc), 1))
        dlf = lax.dot_general(dc, lower.astype(F32), NN, precision=lax.Precision.HIGHEST,
                              preferred_element_type=F32) + carry[...]
        carry[...] += jnp.sum(dc, axis=1, keepdims=True)
        df = dlf * _sigmoid(-(f_ref[...] + b_ref[...]))
        df_ref[...] = df
        db_ref[...] += jnp.broadcast_to(jnp.sum(df, axis=1, keepdims=True), db_ref.shape)

    blk = pl.BlockSpec((h, tc), lambda i: (0, nb - 1 - i))
    return pl.pallas_call(
        body, name=name, grid=(nb,), in_specs=[blk, blk, blk, pl.BlockSpec((h, 1), lambda i: (0, 0))],
        out_specs=[blk, pl.BlockSpec((h, LANES), lambda i: (0, 0))],
        out_shape=[jax.ShapeDtypeStruct((h, s), F32), jax.ShapeDtypeStruct((h, LANES), F32)],
        scratch_shapes=[pltpu.VMEM((h, 1), F32)], compiler_params=_cparams())(dcq, dck, f_t, b_f)


def _attn_fwd(name, qkv, q_blk, k_blk, v_blk, n_pairs, ck, side=None):
    s = qkv.shape[0]
    dh = LANES // 2
    t = min(ATT_BLOCK, s)
    nq = s // t
    scale = dh ** -0.5

    def body(q_ref, k_ref, v_ref, ck_ref, o_ref, lse_ref, m_s, acc_s):
        i = pl.program_id(1)
        low = lax.broadcasted_iota(jnp.int32, (1, LANES), 1) < dh
        qs = (q_ref[...].astype(F32) * scale).astype(BF16)
        zero = jnp.zeros_like(qs)
        qh = (jnp.where(low, qs, zero), jnp.where(low, zero, qs))
        m_s[...] = jnp.full(m_s.shape, -1e30, F32)
        acc_s[...] = jnp.zeros_like(acc_s)
        causal = (lax.broadcasted_iota(jnp.int32, (t, t), 1) <= lax.broadcasted_iota(jnp.int32, (t, t), 0))

        def step(j, diagonal):
            r0 = pl.multiple_of(j * t, t)
            kj = k_ref[pl.ds(r0, t), :]
            vj = v_ref[pl.ds(r0, t), :]
            one = jnp.ones_like(vj)
            vh = (jnp.where(low, vj, one), jnp.where(low, one, vj))
            for hd in range(2):
                sc = _dot(qh[hd], kj, NT) - ck_ref[hd, j]
                if diagonal:
                    sc = jnp.where(causal, sc, -1e30)
                m_old = m_s[hd]
                m_new = jnp.maximum(m_old, jnp.max(sc, axis=1, keepdims=True))
                p = jnp.exp(sc - m_new)
                acc_s[hd] = jnp.exp(m_old - m_new) * acc_s[hd] + _dot(p.astype(BF16), vh[hd], NN)
                m_s[hd] = m_new

        def full(j, _):
            step(j, False)
            return 0

        lax.fori_loop(0, i, full, 0)
        step(i, True)
        a0, a1 = acc_s[0], acc_s[1]
        o_ref[...] = jnp.where(low, a0 / pltpu.roll(a0, dh, 1), a1 / pltpu.roll(a1, dh, 1)).astype(BF16)
        lse_ref[0] = m_s[0] + jnp.log(a0[:, dh:dh + 1])
        lse_ref[1] = m_s[1] + jnp.log(a1[:, 0:1])

    return _hosted_call(
        body, side, name, (n_pairs, nq),
        [pl.BlockSpec((t, LANES), lambda hp, i: (i, q_blk + hp)),
         pl.BlockSpec((s, LANES), lambda hp, i: (0, k_blk + hp)),
         pl.BlockSpec((s, LANES), lambda hp, i: (0, v_blk + hp)),
         pl.BlockSpec((2, nq, 1, t), lambda hp, i: (hp, 0, 0, 0))],
        [pl.BlockSpec((t, LANES), lambda hp, i: (i, hp)), pl.BlockSpec((2, t, 1), lambda hp, i: (hp, i, 0))],
        [jax.ShapeDtypeStruct((s, LANES * n_pairs), BF16), jax.ShapeDtypeStruct((2 * n_pairs, s, 1), F32)],
        [pltpu.VMEM((2, t, 1), F32), pltpu.VMEM((2, t, LANES), F32)], (qkv, qkv, qkv, ck))


def _attn_bwd(name, qkv, q_blk, k_blk, v_blk, n_pairs, o, do, lse_rows, ck_cols, side=None):
    s = qkv.shape[0]
    dh = LANES // 2
    t = min(ATT_BLOCK, s)
    nk = s // t
    scale = dh ** -0.5

    def body(q_ref, k_ref, v_ref, o_ref, do_ref, lse_ref, ck_ref,
             dq_ref, dk_ref, dv_ref, dcq_ref, dck_ref, delta, dqt, dk_acc, dv_acc):
        j = pl.program_id(1)
        low = lax.broadcasted_iota(jnp.int32, (1, LANES), 1) < dh
        low_rows = lax.broadcasted_iota(jnp.int32, (LANES, 1), 0) < dh

        @pl.when(j == 0)
        def _():
            dqt[...] = jnp.zeros_like(dqt)
            sel = (jnp.broadcast_to(low, (SUBLANES, LANES)).astype(F32), jnp.broadcast_to(~low, (SUBLANES, LANES)).astype(F32))

            def fill(i, _):
                r0 = pl.multiple_of(i * t, t)
                prod = do_ref[pl.ds(r0, t), :].astype(F32) * o_ref[pl.ds(r0, t), :].astype(F32)
                for hd in range(2):
                    delta[hd, i] = lax.dot_general(sel[hd], prod, NT, precision=lax.Precision.HIGHEST,
                                                   preferred_element_type=F32)
                return 0

            lax.fori_loop(0, nk, fill, 0)

        kj, vj = k_ref[...], v_ref[...]
        zero, one = jnp.zeros_like(kj), jnp.ones_like(kj)
        kh = (jnp.where(low, kj, zero), jnp.where(low, zero, kj))
        vh = (jnp.where(low, vj, zero), jnp.where(low, zero, vj))
        kjt = kj.astype(F32).T.astype(BF16)
        one_t = jnp.ones_like(kjt)
        kht = (jnp.where(low_rows, kjt, one_t), jnp.where(low_rows, one_t, kjt))
        dk_acc[...] = jnp.zeros_like(dk_acc)
        dv_acc[...] = jnp.zeros_like(dv_acc)
        causal_t = (lax.broadcasted_iota(jnp.int32, (t, t), 0) <= lax.broadcasted_iota(jnp.int32, (t, t), 1))

        def step(i, diagonal):
            r0 = pl.multiple_of(i * t, t)
            qi = (q_ref[pl.ds(r0, t), :].astype(F32) * scale).astype(BF16)
            doi = do_ref[pl.ds(r0, t), :]
            qone, dzero = jnp.ones_like(qi), jnp.zeros_like(doi)
            qsel = (jnp.where(low, qi, qone), jnp.where(low, qone, qi))
            dosel = (jnp.where(low, doi, dzero), jnp.where(low, dzero, doi))
            for hd in range(2):
                st = _dot(kh[hd], qi, NT) - ck_ref[hd] - lse_ref[hd, i]
                pt = jnp.exp(st)
                if diagonal:
                    pt = jnp.where(causal_t, pt, 0.0)
                dst = pt * (_dot(vh[hd], doi, NT) - delta[hd, i, 0:1, :])
                dsb = dst.astype(BF16)
                dv_acc[...] += _dot(pt.astype(BF16), dosel[hd], NN)
                dk_acc[hd] += _dot(dsb, qsel[hd], NN)
                dqt[hd, i] += _dot(kht[hd], dsb, NN)

        step(j, True)

        def rest(i, _):
            step(i, False)
            return 0

        lax.fori_loop(j + 1, nk, rest, 0)
        dk_ref[...] = jnp.where(low, dk_acc[0], dk_acc[1]).astype(BF16)
        dv_ref[...] = dv_acc[...].astype(BF16)
        dck_ref[0] = -dk_acc[0][:, dh:dh + 1]
        dck_ref[1] = -dk_acc[1][:, 0:1]

        @pl.when(j == nk - 1)
        def _():
            def emit(i, _):
                r0 = pl.multiple_of(i * t, t)
                d0, d1 = dqt[0, i], dqt[1, i]
                dq_ref[pl.ds(r0, t), :] = (jnp.where(low_rows, d0, d1) * scale).T.astype(BF16)
                dcq_ref[0, i] = d0[dh:dh + 1, :]
                dcq_ref[1, i] = d1[0:1, :]
                return 0

            lax.fori_loop(0, nk, emit, 0)

    col_blk = lambda base: pl.BlockSpec((t, LANES), lambda hp, j: (j, base + hp))
    col_all = lambda base: pl.BlockSpec((s, LANES), lambda hp, j: (0, base + hp))
    rows_all = pl.BlockSpec((2, nk, 1, t), lambda hp, j: (hp, 0, 0, 0))
    return _hosted_call(
        body, side, name, (n_pairs, nk),
        [col_all(q_blk), col_blk(k_blk), col_blk(v_blk), col_all(0), col_all(0), rows_all,
         pl.BlockSpec((2, t, 1), lambda hp, j: (hp, j, 0))],
        [col_all(0), col_blk(0), col_blk(0), rows_all, pl.BlockSpec((2, t, 1), lambda hp, j: (hp, j, 0))],
        [jax.ShapeDtypeStruct((s, LANES * n_pairs), BF16), jax.ShapeDtypeStruct((s, LANES * n_pairs), BF16),
         jax.ShapeDtypeStruct((s, LANES * n_pairs), BF16), jax.ShapeDtypeStruct((2 * n_pairs, nk, 1, t), F32),
         jax.ShapeDtypeStruct((2 * n_pairs, s, 1), F32)],
        [pltpu.VMEM((2, nk, SUBLANES, t), F32), pltpu.VMEM((2, nk, LANES, t), F32),
         pltpu.VMEM((2, t, LANES), F32), pltpu.VMEM((t, LANES), F32)],
        (qkv, qkv, qkv, o, do, lse_rows, ck_cols))


def _adamw(name, w, g, m, v):
    n_l, r, c = w.shape
    by_rows = r % SUBLANES == 0
    tr = _pick8(r, max(SUBLANES, ROW_TILE_BYTES // (4 * c))) if by_rows else r
    tl = 1 if by_rows else max(t for t in range(1, n_l + 1) if n_l % t == 0 and t * r * c * 4 <= ROW_TILE_BYTES)

    def body(w_ref, g_ref, m_ref, v_ref, d_ref, mo_ref, vo_ref):
        gv = g_ref[...]
        m2 = ADAM_B1 * m_ref[...] + (1.0 - ADAM_B1) * gv
        v2 = ADAM_B2 * v_ref[...] + (1.0 - ADAM_B2) * (gv * gv)
        m_hat = m2 / (1.0 - ADAM_B1 ** ADAM_STEP)
        v_hat = v2 / (1.0 - ADAM_B2 ** ADAM_STEP)
        d_ref[...] = -ADAM_LR * (m_hat / (jnp.sqrt(v_hat) + ADAM_EPS) + ADAM_WD * w_ref[...])
        mo_ref[...] = m2
        vo_ref[...] = v2

    blk = pl.BlockSpec((None, tr, c) if by_rows else (tl, r, c), lambda l, i: (l, i, 0))
    sh = jax.ShapeDtypeStruct((n_l, r, c), F32)
    return pl.pallas_call(body, name=name, grid=(n_l // tl, r // tr), in_specs=[blk] * 4,
                          out_specs=[blk] * 3, out_shape=[sh, sh, sh], compiler_params=_cparams())(w, g, m, v)


def _pick8(dim, target, mult=SUBLANES):
    best, t = None, mult
    while t <= min(dim, target):
        if dim % t == 0:
            best = t
        t += mult
    return best or dim


BF16_ROWS = 16


def _sum_blocks(name, x, out_dtype):
    n, r, c = x.shape
    tr = _pick8(r, max(BF16_ROWS, SUM_TILE_BYTES // (4 * c)), BF16_ROWS)

    def body(x_ref, o_ref):
        acc = x_ref[0].astype(F32)
        for i in range(1, n):
            acc = acc + x_ref[i].astype(F32)
        o_ref[...] = acc.astype(out_dtype)

    return pl.pallas_call(body, name=name, grid=(r // tr,),
                          in_specs=[pl.BlockSpec((n, tr, c), lambda i: (0, i, 0))],
                          out_specs=pl.BlockSpec((tr, c), lambda i: (i, 0)),
                          out_shape=jax.ShapeDtypeStruct((r, c), out_dtype), compiler_params=_cparams())(x)


def _all_gather(name, x_shard):
    m_per, n = x_shard.shape

    def body(x_ref, out_ref, send_sems, recv_sems):
        x, y, c = lax.axis_index("x"), lax.axis_index("y"), lax.axis_index("c")
        me, sibling = (x, y, c), (x, y, 1 - c)
        chips = [(1 - x, y), (x, 1 - y), (1 - x, 1 - y)]

        def rows(px, py, pc):
            return out_ref.at[pl.ds((4 * px + 2 * py + pc) * m_per, m_per), :]

        def copy(k, block, to, src=None):
            return pltpu.make_async_remote_copy(
                src_ref=rows(*block) if src is None else src, dst_ref=rows(*block),
                send_sem=send_sems.at[k], recv_sem=recv_sems.at[k], device_id=to, device_id_type=MESH)

        first = [copy(0, me, sibling, src=x_ref)]
        first += [copy(1 + j, me, (*chip, c), src=x_ref) for j, chip in enumerate(chips)]
        for cp in first:
            cp.start()
        passed = [copy(4 + j, (*chip, c), sibling) for j, chip in enumerate(chips)]
        for j, chip in enumerate(chips):
            copy(1 + j, (*chip, c), me).wait_recv()
            passed[j].start()
        copy(0, sibling, me).wait_recv()
        for j, chip in enumerate(chips):
            copy(4 + j, (*chip, 1 - c), me).wait_recv()
        for cp in first + passed:
            cp.wait_send()

    out = pl.pallas_call(
        body, name=name, out_shape=jax.ShapeDtypeStruct((N_DEV * m_per, n), x_shard.dtype),
        in_specs=[pl.BlockSpec(memory_space=pl.ANY)], out_specs=pl.BlockSpec(memory_space=pl.ANY),
        scratch_shapes=[pltpu.SemaphoreType.DMA((7,)), pltpu.SemaphoreType.DMA((7,))],
    )(x_shard)
    my_dev = 4 * lax.axis_index("x") + 2 * lax.axis_index("y") + lax.axis_index("c")
    return lax.dynamic_update_slice(out, x_shard, (my_dev * m_per, 0))


def _put_own(out, own, index):
    start = tuple(index) + (0,) * own.ndim
    return lax.dynamic_update_slice(out, own.reshape((1,) * len(index) + own.shape), start)


def _gather_copies(stage, ins, outs, send_sems, recv_sems):
    x, y, c = lax.axis_index("x"), lax.axis_index("y"), lax.axis_index("c")
    my_chip = 2 * x + y
    copies = []
    for w, out in enumerate(outs):
        half = out.shape[1] // 2
        rows = pl.ds(c * half, half)
        for k, (cx, cy) in enumerate([(1 - x, y), (x, 1 - y), (1 - x, 1 - y)]):
            if stage == 0:
                src, dst, to = ins[w].at[rows], out.at[my_chip, rows], (cx, cy, c)
            else:
                src = dst = out.at[2 * cx + cy, rows]
                to = (x, y, 1 - c)
            copies.append(pltpu.make_async_remote_copy(
                src_ref=src, dst_ref=dst, send_sem=send_sems.at[3 * w + k], recv_sem=recv_sems.at[3 * w + k],
                device_id=to, device_id_type=MESH))
    return copies


def _gathered_shapes(shards):
    return [jax.ShapeDtypeStruct((N_CHIPS,) + s.shape, s.dtype) for s in shards]


def _put_own_slabs(gathered, shards):
    my_chip = 2 * lax.axis_index("x") + lax.axis_index("y")
    return [_put_own(o, s, (my_chip,)) for o, s in zip(gathered, shards)]


def _gather_layer(name, shards):
    n_w = len(shards)

    def body(*refs):
        ins, outs = refs[:n_w], refs[n_w:2 * n_w]
        for stage in (0, 1):
            copies = _gather_copies(stage, ins, outs, refs[2 * n_w + 2 * stage], refs[2 * n_w + 2 * stage + 1])
            for cp in copies:
                cp.start()
            for cp in copies:
                cp.wait()

    outs = pl.pallas_call(
        body, name=name, out_shape=_gathered_shapes(shards),
        in_specs=[pl.BlockSpec(memory_space=pl.ANY)] * n_w, out_specs=[pl.BlockSpec(memory_space=pl.ANY)] * n_w,
        scratch_shapes=[pltpu.SemaphoreType.DMA((3 * n_w,))] * 4,
    )(*shards)
    return _put_own_slabs(outs, shards)


def _gather_side_jobs(shards):
    between_chips = _SideJob(list(shards), _gathered_shapes(shards), {}, 3 * len(shards),
                             lambda ins, outs, send, recv: _gather_copies(0, ins, outs, send, recv))
    between_cores = lambda partial: _SideJob(
        list(partial), [jax.ShapeDtypeStruct(p.shape, p.dtype) for p in partial], {w: w for w in range(len(partial))},
        3 * len(partial), lambda ins, outs, send, recv: _gather_copies(1, ins, outs, send, recv))
    return between_chips, between_cores


def _run_job(name, job):
    n_in, n_out = len(job.arrays), len(job.out_shapes)

    def body(*refs):
        copies = job.copies(refs[:n_in], refs[n_in:n_in + n_out], refs[n_in + n_out], refs[n_in + n_out + 1])
        for cp in copies:
            cp.start()
        for cp in copies:
            cp.wait()

    hbm = pl.BlockSpec(memory_space=pl.ANY)
    return pl.pallas_call(
        body, name=name, out_shape=list(job.out_shapes), in_specs=[hbm] * n_in, out_specs=[hbm] * n_out,
        scratch_shapes=[pltpu.SemaphoreType.DMA((job.n_sems,))] * 2, input_output_aliases=dict(job.aliases),
    )(*job.arrays)


def _swap_job(grads):
    def copies(ins, outs, send_sems, recv_sems):
        x, y, c = lax.axis_index("x"), lax.axis_index("y"), lax.axis_index("c")
        return [pltpu.make_async_remote_copy(
            src_ref=g.at[:, pl.ds((1 - c) * (g.shape[1] // 2), g.shape[1] // 2)], dst_ref=outs[w],
            send_sem=send_sems.at[w], recv_sem=recv_sems.at[w], device_id=(x, y, 1 - c), device_id_type=MESH)
            for w, g in enumerate(ins)]

    shapes = [jax.ShapeDtypeStruct((g.shape[0], g.shape[1] // 2, g.shape[2]), g.dtype) for g in grads]
    return _SideJob(list(grads), shapes, {}, len(grads), copies)


def _exchange_job(parts):
    def copies(ins, outs, send_sems, recv_sems):
        x, y, c = lax.axis_index("x"), lax.axis_index("y"), lax.axis_index("c")
        return [pltpu.make_async_remote_copy(
            src_ref=ins[w].at[2 * cx + cy], dst_ref=outs[w].at[2 * x + y], send_sem=send_sems.at[3 * w + k],
            recv_sem=recv_sems.at[3 * w + k], device_id=(cx, cy, c), device_id_type=MESH)
            for w in range(len(ins)) for k, (cx, cy) in enumerate([(1 - x, y), (x, 1 - y), (1 - x, 1 - y)])]

    return _SideJob(list(parts), [jax.ShapeDtypeStruct(p.shape, p.dtype) for p in parts], {}, 3 * len(parts), copies)


def _share_job(bufs, layers):
    def copies(ins, outs, send_sems, recv_sems):
        x, y, c = lax.axis_index("x"), lax.axis_index("y"), lax.axis_index("c")
        mine = [o.at[layer, pl.ds(c * (o.shape[1] // 2), o.shape[1] // 2)] for o, ls in zip(outs, layers) for layer in ls]
        return [pltpu.make_async_remote_copy(src_ref=rows, dst_ref=rows, send_sem=send_sems.at[k], recv_sem=recv_sems.at[k],
                                             device_id=(x, y, 1 - c), device_id_type=MESH) for k, rows in enumerate(mine)]

    return _SideJob(list(bufs), [jax.ShapeDtypeStruct(b.shape, b.dtype) for b in bufs], {w: w for w in range(len(bufs))},
                    sum(len(ls) for ls in layers), copies)


def _sum_into(name, blocks, core, layer, depth, into):
    n, r, c = blocks.shape
    tr = _pick8(r, max(BF16_ROWS, SUM_TILE_BYTES // (4 * c)), BF16_ROWS)
    steps = r // tr

    def body(core_ref, x_ref, *rest):
        acc = x_ref[0].astype(F32)
        for i in range(1, n):
            acc = acc + x_ref[i].astype(F32)
        rest[-1][...] = acc

    grid_spec = pltpu.PrefetchScalarGridSpec(
        num_scalar_prefetch=1, grid=(steps,),
        in_specs=[pl.BlockSpec((n, tr, c), lambda i, core_ref: (0, i, 0))]
        + ([pl.BlockSpec(memory_space=pl.ANY)] if into is not None else []),
        out_specs=pl.BlockSpec((None, tr, c), lambda i, core_ref: (layer, core_ref[0] * steps + i, 0)))
    return pl.pallas_call(
        body, name=name, grid_spec=grid_spec, out_shape=jax.ShapeDtypeStruct((depth, 2 * r, c), F32),
        input_output_aliases={2: 0} if into is not None else {}, compiler_params=_cparams(),
    )(core, blocks, *([into] if into is not None else []))


def _add_rows(name, grads, recv, core):
    n, r, c = recv.shape
    tr = _pick8(r, max(BF16_ROWS, SUM_TILE_BYTES // (4 * c)), BF16_ROWS)
    steps = r // tr

    def body(core_ref, g_ref, r_ref, o_ref):
        o_ref[...] = (g_ref[...].astype(F32) + r_ref[...].astype(F32)).astype(BF16)

    grid_spec = pltpu.PrefetchScalarGridSpec(
        num_scalar_prefetch=1, grid=(steps,),
        in_specs=[pl.BlockSpec((n, tr, c), lambda i, core_ref: (0, core_ref[0] * steps + i, 0)),
                  pl.BlockSpec((n, tr, c), lambda i, core_ref: (0, i, 0))],
        out_specs=pl.BlockSpec((n, tr, c), lambda i, core_ref: (0, i, 0)))
    return pl.pallas_call(body, name=name, grid_spec=grid_spec,
                          out_shape=jax.ShapeDtypeStruct((n, r, c), BF16), compiler_params=_cparams())(core, grads, recv)


class _LayerReduce:
    def __init__(self, tag, layer, depth, names, grads, core, bufs):
        self.tag, self.layer, self.depth, self.names, self.core, self.bufs = tag, layer, depth, list(names), core, bufs
        self.state = list(grads)

    def _exchange(self, name, job, carry):
        if carry is None:
            return None, _run_job(f"{name}_{self.tag}", job)
        return carry(job)

    def swap_and_add(self, carry=None):
        grads = self.state
        results, recv = self._exchange("grads_swap_cores", _swap_job(grads), carry)
        self.state = [_add_rows(f"grads_add_{n}_{self.tag}", g, r, self.core) for n, g, r in zip(self.names, grads, recv)]
        return results

    def exchange_and_sum(self, carry=None, also=None):
        group = [self] + ([also] if also is not None else [])
        results, arrived = self._exchange("grads_exchange_chips", _exchange_job([p for r in group for p in r.state]), carry)
        my_chip = 2 * lax.axis_index("x") + lax.axis_index("y")
        for r in group:
            mine, arrived = arrived[:len(r.state)], arrived[len(r.state):]
            for n, a, p in zip(r.names, mine, r.state):
                a = _put_own(a, lax.dynamic_index_in_dim(p, my_chip, 0, keepdims=False), (my_chip,))
                r.bufs[n] = _sum_into(f"grads_sum_{n}_{r.tag}", a, r.core, r.layer, r.depth, r.bufs.get(n))
        return results

    def share(self, carry=None, also=None):
        layers = {}
        for r in [self] + ([also] if also is not None else []):
            for n in r.names:
                layers.setdefault(n, []).append(r.layer)
        names = list(layers)
        job = _share_job([self.bufs[n] for n in names], [layers[n] for n in names])
        results, outs = self._exchange("grads_share_cores", job, carry)
        self.bufs.update(zip(names, outs))
        return results


def _pack(arrays, cols, row_multiple, dtype):
    flat = jnp.concatenate([a.reshape(-1).astype(dtype) for a in arrays])
    unit = cols * row_multiple
    total = -(-flat.shape[0] // unit) * unit
    return jnp.pad(flat, (0, total - flat.shape[0])).reshape(total // cols, cols)


def _unpack(buf, shapes):
    flat, out, off = buf.reshape(-1), [], 0
    for sh in shapes:
        n = math.prod(sh)
        out.append(flat[off:off + n].reshape(sh))
        off += n
    return out


def _discretize(lam_re, lam_im, log_dt, b_re, b_im):
    lam = lax.complex(jnp.minimum(lam_re, -EIG_CLIP), lam_im)
    dt = jnp.exp(log_dt)[:, None]
    lam_bar = jnp.exp(lam * dt)
    b_bar = ((lam_bar - 1.0) / lam)[..., None] * lax.complex(b_re, b_im)
    return jnp.real(lam_bar), jnp.imag(lam_bar), jnp.real(b_bar), jnp.imag(b_bar)


def _scan_tables(ar, ai):
    a = lax.complex(ar, ai)
    pw = [a]
    for _ in range(7):
        pw.append(pw[-1] * a)
    rows = jnp.arange(SUBLANES)[:, None]

    def build(p, reverse):
        tabs = []
        for k in (1, 2, 4):
            keep = (rows <= SUBLANES - 1 - k) if reverse else (rows >= k)
            tk = jnp.where(keep, p[k - 1][None, :], 0.0)
            tabs += [jnp.real(tk), jnp.imag(tk)]
        stack = jnp.stack(p[::-1] if reverse else p)
        tabs += [jnp.real(stack), jnp.imag(stack)]
        return jnp.stack(tabs).astype(F32)

    return build(pw, False), build([jnp.conj(p) for p in pw], True)


def _interleave_rows(a, t):
    s, w = a.shape
    return a.reshape(s // t, SUBLANES, t // SUBLANES, w).transpose(0, 2, 1, 3).reshape(s, w)


def _deinterleave_rows(a, t):
    s, w = a.shape
    return a.reshape(s // t, t // SUBLANES, SUBLANES, w).transpose(0, 2, 1, 3).reshape(s, w)


def _block_diag(per_group, groups_per_block):
    g, a, b = per_group.shape
    x = per_group.reshape(g // groups_per_block, groups_per_block, a, b)
    eye = jnp.eye(groups_per_block, dtype=per_group.dtype)
    out = x[:, :, :, None, :] * eye[None, :, None, :, None]
    return out.reshape(g // groups_per_block, groups_per_block * a, groups_per_block * b)


def _block_diag_extract(dense, groups_per_block, a, b):
    nkb = dense.shape[0]
    x = dense.reshape(nkb, groups_per_block, a, groups_per_block, b)
    idx = jnp.arange(groups_per_block)
    return x[:, idx, :, idx, :].transpose(1, 0, 2, 3).reshape(nkb * groups_per_block, a, b)


def _layer_fwd(tag, x, mod, p, wts, carried=None):
    s, d = x.shape
    w_ssm, w_att = p["w_glu"].shape[0], p["w_att"]
    heads = p["b_f"].shape[0]
    dh = w_att // heads
    cs = d // N_CHIPS
    tm = _pick(s, 1024)
    row = lambda v: v.reshape(1, -1)
    sv = {}

    h = _prenorm_fwd(f"prenorm_mix_{tag}", x, row(p["g_pre_mix"]), row(mod[1]), row(mod[0]))
    uqkv = _mm_plain(f"proj_main_{tag}", h, p["w_main"], "nn", BF16, tm=1024, tn=1024, tk=1024)
    fg = _mm_plain(f"proj_gate_{tag}", h, p["w_gates"], "nn", F32, tm=1024, tn=1024, tk=1024)
    f_t = fg[:, 2 * d:2 * d + heads].T

    t5 = min(S5_ROWS, s)
    u_il = _interleave_rows(uqkv[:, :w_ssm], t5)
    y_s5, ys_il, carries = _s5_fwd(f"s5_fwd_{tag}", u_il, p["b_blk"], p["c_blk"], p["a_f"], p["tab_f"],
                                   row(p["d_skip"]), p["w_glu"], row(p["b_glu"]))
    ys = _deinterleave_rows(ys_il, t5)

    assert dh * 2 == LANES and w_ssm % LANES == 0 and w_att % LANES == 0
    n_pairs = w_att // LANES
    blocks = (w_ssm // LANES, w_ssm // LANES + n_pairs, w_ssm // LANES + 2 * n_pairs)
    cum = _cum_fwd(f"cum_fwd_{tag}", f_t, p["b_f"].reshape(heads, 1))
    t = min(ATT_BLOCK, s)
    ck_cols, ck_rows = cum.reshape(heads, s, 1), cum.reshape(heads, s // t, 1, t)
    late_names, late_shards, next_shards = carried if carried else ((), [], [])
    chips_job, cores_job = _gather_side_jobs(list(late_shards) + list(next_shards)) if carried else (None, None)
    (ya, lse), arrived = _attn_fwd(f"attn_fwd_{tag}", uqkv, *blocks, n_pairs, ck_rows, side=chips_job)
    if late_names:
        late = _run_job(f"gather_weights_late_{tag}", cores_job(arrived[:len(late_names)]))
        wts = {**wts, **dict(zip(late_names, _put_own_slabs(late, late_shards)))}
        arrived = arrived[len(late_names):]
    fs = wts["w_ffn_down"].shape[1]

    tile = pl.BlockSpec((tm, cs), lambda i, j, k: (i, j))
    slab = lambda rows: pl.BlockSpec((None, rows, cs), lambda i, j, k: (j, 0, 0))

    def merge(acc, extra_refs, out_refs):
        ya_ref, wpb_ref, ga_ref, gb_ref = extra_refs
        a_ref, b_ref, m_ref = out_refs
        bv = _dot(ya_ref[...], wpb_ref[...], NN)
        a_ref[...] = acc.astype(BF16)
        b_ref[...] = bv.astype(BF16)
        m_ref[...] = (_sigmoid(ga_ref[...]) * acc + _sigmoid(gb_ref[...]) * bv).astype(BF16)

    sd_bf = jax.ShapeDtypeStruct((s, d), BF16)
    pa, pb, merged = _mm_raw(
        f"merge_{tag}", ys, wts["w_pa"], "nn", (s // tm, N_CHIPS, 1), (tm, cs),
        pl.BlockSpec((tm, w_ssm), lambda i, j, k: (i, 0)), slab(w_ssm), [sd_bf] * 3, [tile] * 3, merge,
        extra=(ya, wts["w_pb"], fg, fg),
        extra_specs=[pl.BlockSpec((tm, w_att), lambda i, j, k: (i, 0)), slab(w_att), tile,
                     pl.BlockSpec((tm, cs), lambda i, j, k: (i, j + N_CHIPS))])

    tm2 = _pick(s, POSTNORM_ROWS)
    x1, y_mix = _mm_postnorm(
        f"out_proj_{tag}", merged, pl.BlockSpec((tm2, cs), lambda i, j, k: (i, k)), wts["w_o"],
        pl.BlockSpec((None, cs, d), lambda i, j, k: (k, 0, 0)), N_CHIPS, x, row(mod[2]), row(p["g_post_mix"]))

    h2 = _prenorm_fwd(f"prenorm_ffn_{tag}", x1, row(p["g_pre_ffn"]), row(mod[4]), row(mod[3]))
    (a4, b4, hid4), next_wts = _ffn_up(f"ffn_up_{tag}", h2, wts["w_ffn_gate"], wts["w_ffn_up"],
                                       side=cores_job(arrived) if carried and arrived else None)
    x2, y_ffn = _mm_postnorm(
        f"ffn_down_{tag}", hid4, pl.BlockSpec((None, tm2, fs), lambda i, j, k: (k, i, 0)), wts["w_ffn_down"],
        pl.BlockSpec((None, fs, d), lambda i, j, k: (k, 0, 0)), N_CHIPS, x1, row(mod[5]), row(p["g_post_ffn"]))

    sv.update(x=x, h=h, uqkv=uqkv, u_il=u_il, fg=fg, f_t=f_t, y_s5=y_s5, ys=ys, carries=carries, blocks=blocks,
              ck_cols=ck_cols, lse_rows=lse.reshape(heads, s // t, 1, t), ya=ya, pa=pa, pb=pb, merged=merged, x1=x1,
              y_mix=y_mix, h2=h2, a4=a4, b4=b4, hid4=hid4, y_ffn=y_ffn)
    return x2, sv, wts, next_wts


def _mm_postnorm(name, a, a_spec, w, w_spec, nk, x, gate, g):
    s, d = x.shape
    tm = _pick(s, POSTNORM_ROWS)
    rowspec = pl.BlockSpec((tm, d), lambda i, j, k: (i, 0))
    vec = pl.BlockSpec((1, d), lambda i, j, k: (0, 0))

    def epilogue(acc, extra_refs, out_refs):
        x_ref, gate_ref, g_ref = extra_refs
        r = lax.rsqrt(jnp.mean(acc * acc, axis=-1, keepdims=True) + RMS_EPS)
        out_refs[0][...] = x_ref[...] + gate_ref[...] * (acc * r * g_ref[...])
        out_refs[1][...] = acc

    sd = jax.ShapeDtypeStruct((s, d), F32)
    return _mm_raw(name, a, w, "nn", (s // tm, 1, nk), (tm, d), a_spec, w_spec, [sd, sd], [rowspec, rowspec], epilogue,
                   extra=(x, gate, g), extra_specs=[rowspec, vec, vec])


def _layer_bwd(tag, dx2, mod, p, wts, sv, reduce_later=None, early=None):
    s, d = dx2.shape
    w_ssm, w_att = p["w_glu"].shape[0], wts["w_pb"].shape[1]
    heads = p["b_f"].shape[0]
    cs = d // N_CHIPS
    fs = wts["w_ffn_down"].shape[1]
    tm, tk, td = _pick(s, 1024), _pick(s, 1024), d
    row = lambda v: v.reshape(1, -1)
    gr = {}

    def dw_slabs(name, act, act_spec, rows, dy, dy_spec, cols, grid_mn, out_index):
        return _mm_raw(name, act, dy, "tn", grid_mn + (s // tk,), (rows, cols), act_spec, dy_spec,
                       [jax.ShapeDtypeStruct((N_CHIPS,) + out_index[1], BF16)],
                       [pl.BlockSpec((None, rows, cols), out_index[0])], _store(BF16))[0]

    dy_ffn, sums = _postnorm_bwd(f"postnorm_bwd_ffn_{tag}", dx2, sv["y_ffn"], row(p["g_post_ffn"]), row(mod[5]))
    d_gate_f, gr["g_post_ffn"] = sums[0], sums[1]
    gr["w_ffn_down"] = dw_slabs(f"dw_down_{tag}", sv["hid4"], pl.BlockSpec((None, tk, fs), lambda i, j, k: (i, k, 0)), fs,
                                dy_ffn, pl.BlockSpec((tk, d), lambda i, j, k: (k, 0)), d, (N_CHIPS, 1),
                                (lambda i, j, k: (i, 0, 0), (fs, d)))

    def swiglu_bwd(acc, extra_refs, out_refs):
        av, bv = extra_refs[0][...].astype(F32), extra_refs[1][...].astype(F32)
        sg = _sigmoid(av)
        out_refs[0][...] = (acc * bv * (sg * (1.0 + av * (1.0 - sg)))).astype(BF16)
        out_refs[1][...] = (acc * (av * sg)).astype(BF16)

    blk4 = pl.BlockSpec((None, tm, fs), lambda i, j, k: (j, i, 0))
    sh4 = jax.ShapeDtypeStruct((N_CHIPS, s, fs), BF16)
    ffn_down_bwd = lambda side: _mm_raw(
        f"ffn_down_bwd_{tag}", dy_ffn, wts["w_ffn_down"], "nt", (s // tm, N_CHIPS, 1), (tm, fs),
        pl.BlockSpec((tm, d), lambda i, j, k: (i, 0)), pl.BlockSpec((None, fs, d), lambda i, j, k: (j, 0, 0)),
        [sh4, sh4], [blk4, blk4], swiglu_bwd, extra=(sv["a4"], sv["b4"]), extra_specs=[blk4, blk4], side=side)
    da4, db4 = reduce_later.swap_and_add(ffn_down_bwd) if reduce_later else ffn_down_bwd(None)
    for n, act4 in (("w_ffn_gate", da4), ("w_ffn_up", db4)):
        gr[n] = dw_slabs(f"d{n}_{tag}", sv["h2"], pl.BlockSpec((tk, td), lambda i, j, k: (k, i)), td,
                         act4, pl.BlockSpec((None, tk, fs), lambda i, j, k: (j, k, 0)), fs, (d // td, N_CHIPS),
                         (lambda i, j, k: (j, i, 0), (d, fs)))
    pairs = [(act4, (None, tm, fs), lambda i, kk: (kk, i, 0), wts[n], (None, td, fs), lambda j, kk: (kk, j, 0),
              N_CHIPS) for n, act4 in (("w_ffn_gate", da4), ("w_ffn_up", db4))]
    own_early = None
    if early is not None:
        own_early = _LayerReduce(f"{tag}e", early[0], early[1], EARLY_REDUCED, [gr[n] for n in EARLY_REDUCED], *early[2:])
    dh_ffn = lambda side: _mm_sum(f"dh_ffn_{tag}", s, d, tm, td, pairs, F32, side=side)
    dh2 = own_early.swap_and_add(dh_ffn) if own_early else dh_ffn(None)
    dx1, sums = _prenorm_bwd(f"prenorm_bwd_ffn_{tag}", dh2, sv["x1"], row(p["g_pre_ffn"]), row(mod[4]), dx2)
    d_scale_f, d_shift_f, gr["g_pre_ffn"] = sums[0], sums[1], sums[2]

    dy_mix, sums = _postnorm_bwd(f"postnorm_bwd_mix_{tag}", dx1, sv["y_mix"], row(p["g_post_mix"]), row(mod[2]))
    d_gate_m, gr["g_post_mix"] = sums[0], sums[1]
    gr["w_o"] = dw_slabs(f"dw_o_{tag}", sv["merged"], pl.BlockSpec((tk, cs), lambda i, j, k: (k, i)), cs,
                         dy_mix, pl.BlockSpec((tk, d), lambda i, j, k: (k, 0)), d, (N_CHIPS, 1),
                         (lambda i, j, k: (i, 0, 0), (cs, d)))

    tile = pl.BlockSpec((tm, cs), lambda i, j, k: (i, j))

    def merge_bwd(acc, extra_refs, out_refs):
        a_ref, b_ref, ga_ref, gb_ref = extra_refs
        sa, sb = _sigmoid(ga_ref[...]), _sigmoid(gb_ref[...])
        out_refs[0][...] = (acc * sa).astype(BF16)
        out_refs[1][...] = (acc * sb).astype(BF16)
        out_refs[2][...] = (acc * a_ref[...].astype(F32) * sa * (1.0 - sa)).astype(BF16)
        out_refs[3][...] = (acc * b_ref[...].astype(F32) * sb * (1.0 - sb)).astype(BF16)

    sd_bf = jax.ShapeDtypeStruct((s, d), BF16)
    d_pa, d_pb, d_ga, d_gb = _mm_raw(
        f"out_proj_bwd_{tag}", dy_mix, wts["w_o"], "nt", (s // tm, N_CHIPS, 1), (tm, cs),
        pl.BlockSpec((tm, d), lambda i, j, k: (i, 0)), pl.BlockSpec((None, cs, d), lambda i, j, k: (j, 0, 0)),
        [sd_bf] * 4, [tile] * 4, merge_bwd, extra=(sv["pa"], sv["pb"], sv["fg"], sv["fg"]),
        extra_specs=[tile, tile, tile, pl.BlockSpec((tm, cs), lambda i, j, k: (i, j + N_CHIPS))])
    d_branch = {}
    for n, act, width, d_p in (("w_pa", sv["ys"], w_ssm, d_pa), ("w_pb", sv["ya"], w_att, d_pb)):
        gr[n] = dw_slabs(f"d{n}_{tag}", act, pl.BlockSpec((tk, width), lambda i, j, k: (k, 0)), width,
                         d_p, pl.BlockSpec((tk, cs), lambda i, j, k: (k, j)), cs, (1, N_CHIPS),
                         (lambda i, j, k: (j, 0, 0), (width, cs)))
        d_branch[n] = _mm_raw(
            f"d_in_{n}_{tag}", d_p, wts[n], "nt", (s // tm, 1, N_CHIPS), (tm, width),
            pl.BlockSpec((tm, cs), lambda i, j, k: (i, k)), pl.BlockSpec((None, width, cs), lambda i, j, k: (k, 0, 0)),
            [jax.ShapeDtypeStruct((s, width), BF16)], [pl.BlockSpec((tm, width), lambda i, j, k: (i, 0))], _store(BF16))[0]
    d_ys, d_ya = d_branch["w_pa"], d_branch["w_pb"]

    attn_bwd = lambda side: _attn_bwd(f"attn_bwd_{tag}", sv["uqkv"], *sv["blocks"], w_att // LANES, sv["ya"], d_ya,
                                      sv["lse_rows"], sv["ck_cols"], side=side)
    dq, dk, dv, dcq, dck = (reduce_later.exchange_and_sum(attn_bwd, also=own_early) if reduce_later
                            else attn_bwd(None)[0])
    d_f_t, d_bf = _cum_bwd(f"cum_bwd_{tag}", dcq.reshape(heads, s), dck.reshape(heads, s), sv["f_t"],
                           p["b_f"].reshape(heads, 1))
    gr["b_f"] = d_bf[:, 0]

    t5 = min(S5_ROWS, s)
    du_il, d_bblk, d_cblk, d_abar, d_wglu, vec = _s5_bwd(
        f"s5_bwd_{tag}", sv["u_il"], _interleave_rows(d_ys, t5), sv["y_s5"], sv["carries"], p["b_blk"], p["c_blk"],
        p["a_f"], p["a_r"], p["tab_f"], p["tab_r"], row(p["d_skip"]), p["w_glu"], row(p["b_glu"]))
    du = _deinterleave_rows(du_il, t5)
    gr["w_glu"] = d_wglu.astype(BF16).reshape(N_CHIPS, w_ssm // N_CHIPS, w_ssm)
    gr["b_glu"], gr["d_skip"] = vec[0], vec[1]
    gr["b_blk"], gr["c_blk"], gr["a_bar"] = d_bblk, d_cblk, d_abar

    d_f = jnp.pad(d_f_t.T, ((0, 0), (0, F_PAD - heads))).astype(BF16)
    assert w_ssm % w_att == 0 and (2 * d) % F_PAD == 0
    first = w_ssm // w_att
    main_pieces = [(du, w_ssm, 0), (dq, w_att, first), (dk, w_att, first + 1), (dv, w_att, first + 2)]
    dw = [_mm_plain(f"dw_in{n}_{tag}", sv["h"], piece, "tn", BF16, tm=1024, tn=1024, tk=1024)
          for n, piece in enumerate([du, dq, dk, dv, d_f, d_ga, d_gb])]
    w_in_grad = jnp.concatenate(dw[:4] + [dw[4][:, :heads], dw[5], dw[6]], axis=1)
    gr["w_in"] = w_in_grad.reshape(d, N_CHIPS, w_in_grad.shape[1] // N_CHIPS).transpose(1, 0, 2)
    tmx, tkx = _pick(s, 1024), _pick(d, 512)
    pairs = [(piece, (tmx, width), lambda i, kk: (i, 0), p["w_main"], (d, width), lambda j, kk, blk=blk: (j, blk), 1)
             for piece, width, blk in main_pieces]
    steps = d // tkx
    pairs += [(piece, (tmx, tkx), lambda i, kk: (i, kk), p["w_gates"], (d, tkx), lambda j, kk, off=off: (j, off + kk), steps)
              for piece, off in ((d_ga, 0), (d_gb, steps))]
    pairs.append((d_f, (tmx, F_PAD), lambda i, kk: (i, 0), p["w_gates"], (d, F_PAD), lambda j, kk: (j, 2 * d // F_PAD), 1))
    dh_mix = lambda side: _mm_sum(f"dh_mix_{tag}", s, d, tmx, d, pairs, F32, side=side)
    dh1 = reduce_later.share(dh_mix, also=own_early) if reduce_later else dh_mix(None)
    dx0, sums = _prenorm_bwd(f"prenorm_bwd_mix_{tag}", dh1, sv["x"], row(p["g_pre_mix"]), row(mod[1]), dx1)
    d_scale_m, d_shift_m, gr["g_pre_mix"] = sums[0], sums[1], sums[2]

    d_mod = jnp.stack([d_shift_m, d_scale_m, d_gate_m, d_shift_f, d_scale_f, d_gate_f])
    return dx0, d_mod, gr


BIG = ("w_in", "w_glu", "w_pa", "w_pb", "w_o", "w_ffn_gate", "w_ffn_up", "w_ffn_down")
FIRST_USED = ("w_in", "w_glu")
EARLY_REDUCED = ("w_ffn_gate", "w_ffn_up", "w_ffn_down")
SMALL = ("b_ada", "g_pre_mix", "g_post_mix", "g_pre_ffn", "g_post_ffn", "lam_re", "lam_im", "log_dt", "b_re", "b_im",
         "c_re", "c_im", "d_skip", "b_glu", "b_f")
WEIGHTS = ("w_ada", "b_ada", "g_pre_mix", "g_post_mix", "g_pre_ffn", "g_post_ffn", "w_in", "lam_re", "lam_im", "log_dt",
           "b_re", "b_im", "c_re", "c_im", "d_skip", "w_glu", "b_glu", "b_f", "w_pa", "w_pb", "w_o", "w_ffn_gate",
           "w_ffn_up", "w_ffn_down")


def _prepare_layer(wts, small, l, seq):
    w_in = jnp.concatenate([wts["w_in"][j] for j in range(N_CHIPS)], axis=1)
    d = w_in.shape[0]
    heads = small["b_f"].shape[1]
    n_groups, n_state, group_ch = small["b_re"].shape[1:]
    w_ssm = n_groups * group_ch
    w_att = (w_in.shape[1] - w_ssm - heads - 2 * d) // 3
    n_main = w_ssm + 3 * w_att
    gpb = LANES // group_ch
    p = {"w_att": w_att}
    p["w_main"] = w_in[:, :n_main]
    p["w_gates"] = jnp.concatenate(
        [w_in[:, n_main + heads:], w_in[:, n_main:n_main + heads], jnp.zeros((d, F_PAD - heads), BF16)], axis=1)
    p["w_glu"] = wts["w_glu"].reshape(w_ssm, w_ssm)
    for n in ("g_pre_mix", "g_post_mix", "g_pre_ffn", "g_post_ffn", "d_skip", "b_glu", "b_f"):
        p[n] = small[n][l]
    ar, ai, br, bi = _discretize(small["lam_re"][l], small["lam_im"][l], small["log_dt"][l], small["b_re"][l], small["b_im"][l])
    n_steps = min(S5_ROWS, seq) // SUBLANES
    powers = jnp.cumprod(jnp.broadcast_to(lax.complex(ar, ai).reshape(1, -1), (n_steps, ar.size)), axis=0)
    p["a_f"] = jnp.concatenate([jnp.real(powers), jnp.imag(powers)], axis=1)
    p["a_r"] = jnp.concatenate([jnp.real(powers[::-1]), -jnp.imag(powers[::-1])], axis=1)
    p["tab_f"], p["tab_r"] = _scan_tables(jnp.real(powers[-1]), jnp.imag(powers[-1]))
    bre = _block_diag(br.transpose(0, 2, 1), gpb)
    bim = _block_diag(bi.transpose(0, 2, 1), gpb)
    p["b_blk"] = jnp.concatenate([bre, bim], axis=2).astype(BF16)
    cre = _block_diag(small["c_re"][l].transpose(0, 2, 1), gpb)
    cim = _block_diag(small["c_im"][l].transpose(0, 2, 1), gpb)
    p["c_blk"] = jnp.concatenate([cre, -cim], axis=1).astype(BF16)
    return p


def _compact_partials(gr, n_state, group_ch):
    gpb = LANES // group_ch
    half = gpb * n_state
    out = dict(gr)
    out["bbar_re"] = _block_diag_extract(gr["b_blk"][:, :, :half], gpb, group_ch, n_state).transpose(0, 2, 1)
    out["bbar_im"] = _block_diag_extract(gr["b_blk"][:, :, half:], gpb, group_ch, n_state).transpose(0, 2, 1)
    out["c_re"] = _block_diag_extract(gr["c_blk"][:, :half, :], gpb, n_state, group_ch).transpose(0, 2, 1)
    out["c_im"] = -_block_diag_extract(gr["c_blk"][:, half:, :], gpb, n_state, group_ch).transpose(0, 2, 1)
    return out


def _small_grads_from_partials(gr, small, l):
    n_groups, n_state, _ = small["b_re"].shape[1:]
    ns2 = n_groups * n_state
    d_abar = jnp.sum(gr["a_bar"], axis=0)
    dar, dai = d_abar[:ns2].reshape(n_groups, n_state), d_abar[ns2:].reshape(n_groups, n_state)
    args = (small["lam_re"][l], small["lam_im"][l], small["log_dt"][l], small["b_re"][l], small["b_im"][l])
    _, vjp = jax.vjp(_discretize, *args)
    d_lam_re, d_lam_im, d_log_dt, d_b_re, d_b_im = vjp((dar, dai, gr["bbar_re"], gr["bbar_im"]))
    return dict(lam_re=d_lam_re, lam_im=d_lam_im, log_dt=d_log_dt, b_re=d_b_re, b_im=d_b_im,
                c_re=gr["c_re"], c_im=gr["c_im"])


def _fwd_bwd(xs, target, mods, small, wts0, later, core=None, late0=None):
    depth = 1 + len(later)
    saved, layers, wts = [], [], [wts0]
    act = xs
    for l in range(depth):
        layers.append(_prepare_layer(wts[l], small, l, xs.shape[0]))
        shards = later[l] if l + 1 < depth and not isinstance(later[l], dict) else None
        late = late0 if l == 0 and late0 else ((), [])
        carried = (late[0], late[1], shards or []) if (late[0] or shards) else None
        act, sv, wts[l], gathered = _layer_fwd(str(l), act, mods[l], layers[l], wts[l], carried=carried)
        saved.append(sv)
        if l + 1 < depth:
            wts.append(dict(zip(BIG, _put_own_slabs(gathered, shards))) if shards is not None else later[l])
    dx, loss_blk = _loss_grad("loss", act, target)
    grads, d_mods = [None] * depth, [None] * depth
    pending, bufs = None, {}
    for l in reversed(range(depth)):
        early = (l, depth, core, bufs) if pending is not None else None
        dx, d_mods[l], grads[l] = _layer_bwd(str(l), dx, mods[l], layers[l], wts[l], saved[l], reduce_later=pending,
                                             early=early)
        if core is not None:
            names = [n for n in BIG if early is None or n not in EARLY_REDUCED]
            pending = _LayerReduce(str(l), l, depth, names, [grads[l][n] for n in names], core, bufs)
    if core is None:
        return loss_blk, dx, d_mods, grads, None
    pending.swap_and_add()
    pending.exchange_and_sum()
    pending.share()
    return loss_blk, dx, d_mods, grads, bufs


def kernel(x, c, w_ada, b_ada, g_pre_mix, g_post_mix, g_pre_ffn, g_post_ffn, w_in, lam_re, lam_im, log_dt, b_re, b_im, c_re, c_im, d_skip, w_glu, b_glu, b_f, w_pa, w_pb, w_o, w_ffn_gate, w_ffn_up, w_ffn_down, loss_target, m_w_ada, m_b_ada, m_g_pre_mix, m_g_post_mix, m_g_pre_ffn, m_g_post_ffn, m_w_in, m_lam_re, m_lam_im, m_log_dt, m_b_re, m_b_im, m_c_re, m_c_im, m_d_skip, m_w_glu, m_b_glu, m_b_f, m_w_pa, m_w_pb, m_w_o, m_w_ffn_gate, m_w_ffn_up, m_w_ffn_down, v_w_ada, v_b_ada, v_g_pre_mix, v_g_post_mix, v_g_pre_ffn, v_g_post_ffn, v_w_in, v_lam_re, v_lam_im, v_log_dt, v_b_re, v_b_im, v_c_re, v_c_im, v_d_skip, v_w_glu, v_b_glu, v_b_f, v_w_pa, v_w_pb, v_w_o, v_w_ffn_gate, v_w_ffn_up, v_w_ffn_down):
    local = dict(locals())
    weights = {n: local[n] for n in WEIGHTS}
    moments_m = {n: local["m_" + n] for n in WEIGHTS}
    moments_v = {n: local["v_" + n] for n in WEIGHTS}
    depth, d = g_pre_mix.shape
    n_mod = w_ada.shape[2] * N_CHIPS // d
    mx, my, mc = lax.axis_index("x"), lax.axis_index("y"), lax.axis_index("c")
    my_chip = 2 * mx + my
    my_dev = 4 * mx + 2 * my + mc
    xs = x[0]

    shards = [[weights[n][l].astype(BF16) for n in BIG] for l in range(depth)]
    early = [i for i, n in enumerate(BIG) if n in FIRST_USED]
    late = [i for i, n in enumerate(BIG) if n not in FIRST_USED]
    wts0 = dict(zip([BIG[i] for i in early], _gather_layer("gather_weights_0", [shards[0][i] for i in early])))
    late0 = ([BIG[i] for i in late], [shards[0][i] for i in late])
    small = {n: weights[n] for n in SMALL}

    c_pad = jnp.pad(c, ((0, SUBLANES - 1), (0, 0)))
    c_all = _all_gather("gather_cond", c_pad).reshape(N_DEV, SUBLANES, d)[:, 0, :]
    silu = lambda v: v * _sigmoid(v)
    n_cols = w_ada.shape[2]
    mod_shard = []
    for l in range(depth):
        bias = lax.dynamic_slice_in_dim(b_ada[l], my_chip * n_cols, n_cols)
        mod_shard.append(_mm_plain(f"ada_{l}", c_all, w_ada[l], "nn", F32, add=jnp.broadcast_to(bias, (N_DEV, n_cols)),
                                   a_fn=silu, tm=N_DEV, tn=512, tk=1024))
    mod_block = jnp.concatenate(mod_shard, axis=1)
    mod_all = _all_gather("gather_mod", mod_block).reshape(N_DEV, N_DEV, depth, n_cols)
    mod_rows = lax.dynamic_index_in_dim(mod_all[0::2], my_dev, axis=1, keepdims=False)
    mods = [mod_rows[:, l, :].reshape(n_mod, d) for l in range(depth)]

    loss_blk, dx, d_mods, grads, big_grads = _fwd_bwd(xs, loss_target[0], mods, small, wts0, shards[1:],
                                                      core=mc.astype(jnp.int32).reshape(1), late0=late0)
    loss = lax.psum(loss_blk[0, 0], ("x", "y", "c"))
    grad_x = dx[None]

    partial_names = ("g_pre_mix", "g_post_mix", "g_pre_ffn", "g_post_ffn", "d_skip", "b_glu", "b_f", "a_bar",
                     "bbar_re", "bbar_im", "c_re", "c_im")
    n_state, group_ch = b_re.shape[2:]
    contrib = list(d_mods)
    for l in range(depth):
        compact = _compact_partials(grads[l], n_state, group_ch)
        contrib += [compact[n] for n in partial_names]
    contrib_shapes = [a.shape for a in contrib]
    block = _pack(contrib, LANES, BF16_ROWS, F32)
    rows = block.shape[0]
    all_blocks = _all_gather("gather_small_grads", block).reshape(N_DEV, rows, LANES)
    summed = _unpack(_sum_blocks("sum_small_grads", all_blocks, F32), contrib_shapes)
    per_layer = len(partial_names)
    small_grads = {n: [] for n in SMALL}
    d_mod_all = []
    for l in range(depth):
        small_grads["b_ada"].append(summed[l].reshape(-1))
        gl = dict(zip(partial_names, summed[depth + l * per_layer:depth + (l + 1) * per_layer]))
        for n in ("g_pre_mix", "g_post_mix", "g_pre_ffn", "g_post_ffn", "d_skip", "b_glu", "b_f"):
            small_grads[n].append(gl[n])
        for n, gval in _small_grads_from_partials(gl, small, l).items():
            small_grads[n].append(gval)
        mod_rows_ = n_mod * d // LANES
        d_mod_all.append(all_blocks[:, l * mod_rows_:(l + 1) * mod_rows_, :].reshape(N_DEV, n_mod * d))
    small_grads = {n: jnp.stack(v) for n, v in small_grads.items()}

    g_w_ada = []
    for l in range(depth):
        cols = lax.dynamic_slice_in_dim(d_mod_all[l], my_chip * n_cols, n_cols, axis=1)
        g_w_ada.append(_mm_plain(f"dw_ada_{l}", c_all, cols, "tn", F32, a_fn=silu, tm=512, tn=512, tk=N_DEV))
    all_grads = dict(big_grads)
    all_grads.update(small_grads)
    all_grads["w_ada"] = jnp.stack(g_w_ada)

    delta, new_m, new_v = {}, {}, {}
    for n in ("w_ada",) + BIG:
        last = weights[n].shape[2]
        to_stored, from_stored = ((0, 1, 2),) * 2 if last % LANES == 0 else ((0, 2, 1),) * 2 if last % SUBLANES == 0 \
            else ((2, 0, 1), (1, 2, 0))
        view, back = (lambda a: a.transpose(to_stored)), (lambda a: a.transpose(from_stored))
        outs = _adamw(f"adamw_{n}", view(weights[n]), view(all_grads[n]), view(moments_m[n]), view(moments_v[n]))
        delta[n], new_m[n], new_v[n] = (back(o) for o in outs)
    small_shapes = [weights[n].shape for n in SMALL]
    packed = [_pack([src[n] for n in SMALL], LANES, SUBLANES, F32)[None] for src in (weights, all_grads, moments_m, moments_v)]
    outs = _adamw("adamw_small", *packed)
    for dst, buf in zip((delta, new_m, new_v), outs):
        dst.update(dict(zip(SMALL, _unpack(buf[0], small_shapes))))

    return (loss, grad_x, *[all_grads[n] for n in WEIGHTS], *[delta[n] for n in WEIGHTS],
            *[new_m[n] for n in WEIGHTS], *[new_v[n] for n in WEIGHTS])
```

```python
import math

import jax
import jax.numpy as jnp
from jax import lax
from jax.experimental import pallas as pl
from jax.experimental.pallas import tpu as pltpu

F32 = jnp.float32
BF16 = jnp.bfloat16
MESH = pl.DeviceIdType.MESH

RMS_EPS = 1e-6
EIG_CLIP = 1e-4
ADAM_LR, ADAM_B1, ADAM_B2, ADAM_EPS, ADAM_WD, ADAM_STEP = 0.001, 0.9, 0.999, 1e-08, 0.01, 10

LANES = 128
SUBLANES = 8
VMEM_LIMIT = 56 * 1024 * 1024
ROW_TILE_BYTES = 1 << 20
SUM_TILE_BYTES = 1 << 19
S5_ROWS = 256
S5_CHUNK = 1024
S5_UNROLL = 4
ATT_BLOCK = 512
F_PAD = 256
POSTNORM_ROWS = 1024
N_CHIPS = 4
N_DEV = 8

NN = (((1,), (0,)), ((), ()))
NT = (((1,), (1,)), ((), ()))
TN = (((0,), (0,)), ((), ()))
_DN = {"nn": NN, "nt": NT, "tn": TN}


def _cparams(**kw):
    return pltpu.CompilerParams(vmem_limit_bytes=VMEM_LIMIT, **kw)


def _pick(dim, target):
    best, t = None, LANES
    while t <= min(dim, target):
        if dim % t == 0:
            best = t
        t += LANES
    return best or dim


def _sigmoid(x):
    return 1.0 / (1.0 + jnp.exp(-x))


def _dot(a, b, dn):
    return lax.dot_general(a, b, dn, preferred_element_type=F32)


def _mm_raw(name, a, b, mode, grid, acc_shape, a_spec, b_spec, out_shapes, out_specs, epilogue,
            extra=(), extra_specs=(), a_fn=None, side=None):
    nk = grid[2]
    n_extra, n_out = len(extra), len(out_shapes)

    def body(*refs):
        a_ref, b_ref = refs[0], refs[1]
        extra_refs = refs[2:2 + n_extra]
        out_refs = refs[2 + n_extra:2 + n_extra + n_out]
        acc = refs[-1]
        k = pl.program_id(2)

        @pl.when(k == 0)
        def _():
            acc[...] = jnp.zeros_like(acc)

        av = a_ref[...]
        if a_fn is not None:
            av = a_fn(av.astype(F32))
        acc[...] += _dot(av.astype(BF16), b_ref[...].astype(BF16), _DN[mode])

        @pl.when(k == nk - 1)
        def _():
            epilogue(acc[...], extra_refs, out_refs)

    outs, side_outs = _hosted_call(body, side, name, grid, [a_spec, b_spec, *extra_specs], list(out_specs),
                                   list(out_shapes), [pltpu.VMEM(acc_shape, F32)], (a, b, *extra))
    return outs if side is None else (outs, side_outs)


def _mm(name, a, b, mode, out_shapes, out_specs, epilogue, extra=(), extra_specs=(),
        tm=512, tn=512, tk=512, a_fn=None):
    if mode == "nn":
        (m, kd), (_, n) = a.shape, b.shape
    elif mode == "nt":
        (m, kd), (n, _) = a.shape, b.shape
    else:
        (kd, m), (_, n) = a.shape, b.shape
    tm, tn, tk = _pick(m, tm), _pick(n, tn), _pick(kd, tk)
    if mode == "tn":
        a_spec = pl.BlockSpec((tk, tm), lambda i, j, k: (k, i))
    else:
        a_spec = pl.BlockSpec((tm, tk), lambda i, j, k: (i, k))
    if mode == "nt":
        b_spec = pl.BlockSpec((tn, tk), lambda i, j, k: (j, k))
    else:
        b_spec = pl.BlockSpec((tk, tn), lambda i, j, k: (k, j))
    res = _mm_raw(name, a, b, mode, (m // tm, n // tn, kd // tk), (tm, tn), a_spec, b_spec, out_shapes, out_specs,
                  epilogue, extra=extra, extra_specs=extra_specs, a_fn=a_fn)
    return res, (tm, tn, tk)


def _store(dtype):
    def epilogue(acc, extra_refs, out_refs):
        out_refs[0][...] = acc.astype(dtype)
    return epilogue


def _mm_sum(name, m, n, tm, tn, pairs, out_dtype, side=None):
    offs, total = [], 0
    for pr in pairs:
        offs.append(total)
        total += pr[6]
    n_p = len(pairs)

    def body(*refs):
        o_ref, acc = refs[2 * n_p], refs[2 * n_p + 1]
        k = pl.program_id(2)

        @pl.when(k == 0)
        def _():
            acc[...] = jnp.zeros_like(acc)

        for p_ in range(n_p):
            @pl.when((k >= offs[p_]) & (k < offs[p_] + pairs[p_][6]))
            def _(p_=p_):
                acc[...] += _dot(refs[2 * p_][...].astype(BF16), refs[2 * p_ + 1][...].astype(BF16), NT)

        @pl.when(k == total - 1)
        def _():
            o_ref[...] = acc[...].astype(out_dtype)

    in_specs, operands = [], []
    for (a, a_block, a_index, b, b_block, b_index, steps), off in zip(pairs, offs):
        local = lambda k, off=off, steps=steps: jnp.clip(k - off, 0, steps - 1)
        in_specs.append(pl.BlockSpec(a_block, lambda i, j, k, f=a_index, local=local: f(i, local(k))))
        in_specs.append(pl.BlockSpec(b_block, lambda i, j, k, f=b_index, local=local: f(j, local(k))))
        operands += [a, b]
    (out,), side_outs = _hosted_call(
        body, side, name, (m // tm, n // tn, total), in_specs, [pl.BlockSpec((tm, tn), lambda i, j, k: (i, j))],
        [jax.ShapeDtypeStruct((m, n), out_dtype)], [pltpu.VMEM((tm, tn), F32)], operands)
    return out if side is None else (out, side_outs)


class _SideJob:
    def __init__(self, arrays, out_shapes, aliases, n_sems, copies):
        self.arrays, self.out_shapes, self.aliases, self.n_sems, self.copies = arrays, out_shapes, aliases, n_sems, copies


def _hosted_call(body, side, name, grid, in_specs, out_specs, out_shape, scratch_shapes, operands):
    if side is None:
        outs = pl.pallas_call(body, name=name, grid=grid, in_specs=in_specs, out_specs=out_specs, out_shape=out_shape,
                              scratch_shapes=scratch_shapes, compiler_params=_cparams())(*operands)
        return outs, []
    n_in, n_out, ns_in, ns_out = len(in_specs), len(out_specs), len(side.arrays), len(side.out_shapes)

    def wrapped(*refs):
        main_in, side_in = refs[:n_in], refs[n_in:n_in + ns_in]
        rest = refs[n_in + ns_in:]
        main_out, side_out, rest = rest[:n_out], rest[n_out:n_out + ns_out], rest[n_out + ns_out:]
        scratch, send_sems, recv_sems = rest[:-2], rest[-2], rest[-1]
        first, last = None, None
        for axis, extent in enumerate(grid):
            at_start, at_end = pl.program_id(axis) == 0, pl.program_id(axis) == extent - 1
            first = at_start if first is None else first & at_start
            last = at_end if last is None else last & at_end

        @pl.when(first)
        def _():
            for cp in side.copies(side_in, side_out, send_sems, recv_sems):
                cp.start()

        body(*main_in, *main_out, *scratch)

        @pl.when(last)
        def _():
            for cp in side.copies(side_in, side_out, send_sems, recv_sems):
                cp.wait()

    hbm = pl.BlockSpec(memory_space=pl.ANY)
    outs = pl.pallas_call(
        wrapped, name=name, grid=grid, in_specs=list(in_specs) + [hbm] * ns_in,
        out_specs=list(out_specs) + [hbm] * ns_out, out_shape=list(out_shape) + list(side.out_shapes),
        scratch_shapes=list(scratch_shapes) + [pltpu.SemaphoreType.DMA((side.n_sems,))] * 2,
        input_output_aliases={n_in + i: n_out + o for i, o in side.aliases.items()},
        compiler_params=_cparams(),
    )(*operands, *side.arrays)
    return outs[:n_out], outs[n_out:]


def _ffn_up(name, h, wg, wu, side=None):
    s, d = h.shape
    nc, fs = wg.shape[0], wg.shape[2]
    tm, tk = _pick(s, 1024), _pick(d, 1024)
    nk = d // tk

    def body(h_ref, wg_ref, wu_ref, a_ref, b_ref, hid_ref, acc_g, acc_u):
        k = pl.program_id(2)

        @pl.when(k == 0)
        def _():
            acc_g[...] = jnp.zeros_like(acc_g)
            acc_u[...] = jnp.zeros_like(acc_u)

        hv = h_ref[...]
        acc_g[...] += _dot(hv, wg_ref[...], NN)
        acc_u[...] += _dot(hv, wu_ref[...], NN)

        @pl.when(k == nk - 1)
        def _():
            av, bv = acc_g[...], acc_u[...]
            a_ref[...] = av.astype(BF16)
            b_ref[...] = bv.astype(BF16)
            hid_ref[...] = (av * _sigmoid(av) * bv).astype(BF16)

    w_spec = pl.BlockSpec((None, tk, fs), lambda i, j, k: (j, k, 0))
    o_spec = pl.BlockSpec((None, tm, fs), lambda i, j, k: (j, i, 0))
    sh = jax.ShapeDtypeStruct((nc, s, fs), BF16)
    return _hosted_call(
        body, side, name, (s // tm, nc, nk), [pl.BlockSpec((tm, tk), lambda i, j, k: (i, k)), w_spec, w_spec],
        [o_spec] * 3, [sh] * 3, [pltpu.VMEM((tm, fs), F32), pltpu.VMEM((tm, fs), F32)], (h, wg, wu))


def _mm_plain(name, a, b, mode, out_dtype, add=None, a_fn=None, tm=512, tn=512, tk=512):
    if mode == "nn":
        m, n = a.shape[0], b.shape[1]
    elif mode == "nt":
        m, n = a.shape[0], b.shape[0]
    else:
        m, n = a.shape[1], b.shape[1]
    tm_, tn_ = _pick(m, tm), _pick(n, tn)
    spec = pl.BlockSpec((tm_, tn_), lambda i, j, k: (i, j))

    def epilogue(acc, extra_refs, out_refs):
        if add is not None:
            acc = acc + extra_refs[0][...]
        out_refs[0][...] = acc.astype(out_dtype)

    extra = () if add is None else (add,)
    (out,), _ = _mm(name, a, b, mode, [jax.ShapeDtypeStruct((m, n), out_dtype)], [spec], epilogue,
                    extra=extra, extra_specs=[spec] * len(extra), tm=tm, tn=tn, tk=tk, a_fn=a_fn)
    return out


def _row_tile(s, d):
    return _pick(s, max(SUBLANES, ROW_TILE_BYTES // (4 * d)))


def _prenorm_fwd(name, x, g, scale, shift):
    s, d = x.shape
    tr = _row_tile(s, d)

    def body(x_ref, g_ref, sc_ref, sh_ref, h_ref):
        xv = x_ref[...]
        r = lax.rsqrt(jnp.mean(xv * xv, axis=-1, keepdims=True) + RMS_EPS)
        h_ref[...] = ((xv * r * g_ref[...]) * (1.0 + sc_ref[...]) + sh_ref[...]).astype(BF16)

    row = pl.BlockSpec((tr, d), lambda i: (i, 0))
    vec = pl.BlockSpec((1, d), lambda i: (0, 0))
    return pl.pallas_call(body, name=name, grid=(s // tr,), in_specs=[row, vec, vec, vec], out_specs=row,
                          out_shape=jax.ShapeDtypeStruct((s, d), BF16), compiler_params=_cparams())(x, g, scale, shift)


def _prenorm_bwd(name, dh, x, g, scale, dx_res):
    s, d = x.shape
    tr = _row_tile(s, d)

    def body(dh_ref, x_ref, g_ref, sc_ref, dxr_ref, dx_ref, sums_ref):
        @pl.when(pl.program_id(0) == 0)
        def _():
            sums_ref[...] = jnp.zeros_like(sums_ref)

        xv, dhv, gv = x_ref[...], dh_ref[...].astype(F32), g_ref[...]
        r = lax.rsqrt(jnp.mean(xv * xv, axis=-1, keepdims=True) + RMS_EPS)
        xhat = xv * r
        dxn = dhv * (1.0 + sc_ref[...])
        dxhat = dxn * gv
        dx = r * (dxhat - xhat * jnp.mean(dxhat * xhat, axis=-1, keepdims=True))
        dx_ref[...] = dxr_ref[...] + dx
        sums_ref[0:1, :] += jnp.sum(dhv * (xhat * gv), axis=0, keepdims=True)
        sums_ref[1:2, :] += jnp.sum(dhv, axis=0, keepdims=True)
        sums_ref[2:3, :] += jnp.sum(dxn * xhat, axis=0, keepdims=True)

    row = pl.BlockSpec((tr, d), lambda i: (i, 0))
    vec = pl.BlockSpec((1, d), lambda i: (0, 0))
    acc = pl.BlockSpec((SUBLANES, d), lambda i: (0, 0))
    return pl.pallas_call(
        body, name=name, grid=(s // tr,), in_specs=[row, row, vec, vec, row], out_specs=[row, acc],
        out_shape=[jax.ShapeDtypeStruct((s, d), F32), jax.ShapeDtypeStruct((SUBLANES, d), F32)],
        compiler_params=_cparams())(dh, x, g, scale, dx_res)


def _postnorm_bwd(name, dxn, y, g, gate):
    s, d = y.shape
    tr = _row_tile(s, d)

    def body(dx_ref, y_ref, g_ref, gt_ref, dy_ref, sums_ref):
        @pl.when(pl.program_id(0) == 0)
        def _():
            sums_ref[...] = jnp.zeros_like(sums_ref)

        yv, dxv, gv = y_ref[...], dx_ref[...], g_ref[...]
        r = lax.rsqrt(jnp.mean(yv * yv, axis=-1, keepdims=True) + RMS_EPS)
        yhat = yv * r
        dn = dxv * gt_ref[...]
        dyhat = dn * gv
        dy_ref[...] = (r * (dyhat - yhat * jnp.mean(dyhat * yhat, axis=-1, keepdims=True))).astype(BF16)
        sums_ref[0:1, :] += jnp.sum(dxv * (yhat * gv), axis=0, keepdims=True)
        sums_ref[1:2, :] += jnp.sum(dn * yhat, axis=0, keepdims=True)

    row = pl.BlockSpec((tr, d), lambda i: (i, 0))
    vec = pl.BlockSpec((1, d), lambda i: (0, 0))
    acc = pl.BlockSpec((SUBLANES, d), lambda i: (0, 0))
    return pl.pallas_call(
        body, name=name, grid=(s // tr,), in_specs=[row, row, vec, vec], out_specs=[row, acc],
        out_shape=[jax.ShapeDtypeStruct((s, d), BF16), jax.ShapeDtypeStruct((SUBLANES, d), F32)],
        compiler_params=_cparams())(dxn, y, g, gate)


def _loss_grad(name, y, target):
    s, d = y.shape
    tr = _row_tile(s, d)

    def body(y_ref, t_ref, dy_ref, loss_ref):
        @pl.when(pl.program_id(0) == 0)
        def _():
            loss_ref[...] = jnp.zeros_like(loss_ref)

        err = y_ref[...] - t_ref[...]
        dy_ref[...] = err * (1.0 / d)
        part = jnp.sum(jnp.sum(err * err, axis=-1, keepdims=True), axis=0, keepdims=True) * (0.5 / d)
        loss_ref[...] += jnp.broadcast_to(part, loss_ref.shape)

    row = pl.BlockSpec((tr, d), lambda i: (i, 0))
    acc = pl.BlockSpec((SUBLANES, LANES), lambda i: (0, 0))
    return pl.pallas_call(
        body, name=name, grid=(s // tr,), in_specs=[row, row], out_specs=[row, acc],
        out_shape=[jax.ShapeDtypeStruct((s, d), F32), jax.ShapeDtypeStruct((SUBLANES, LANES), F32)],
        compiler_params=_cparams())(y, target)


def _gelu(y):
    c = math.sqrt(2.0 / math.pi)
    return 0.5 * y * (1.0 + jnp.tanh(c * (y + 0.044715 * (y * y * y))))


def _gelu_grad(y):
    c = math.sqrt(2.0 / math.pi)
    th = jnp.tanh(c * (y + 0.044715 * (y * y * y)))
    return 0.5 * (1.0 + th) + 0.5 * y * (1.0 - th * th) * c * (1.0 + 3.0 * 0.044715 * (y * y))


def _cmul_add(br, bi, ar, ai, xr, xi):
    return br + ar * xr - ai * xi, bi + ar * xi + ai * xr


def _scan_rows(x_ref, row0, n_steps, ns2, pow_ref, tab_ref, carry_ref, reverse, fold=None):
    assert n_steps % SUBLANES == 0
    wc = min(S5_CHUNK, ns2)
    sub = lax.broadcasted_iota(jnp.int32, (SUBLANES, wc), 0)
    unroll = S5_UNROLL if n_steps % S5_UNROLL == 0 else 1
    for c0 in range(0, ns2, wc):
        re = slice(c0, c0 + wc)
        im = slice(ns2 + c0, ns2 + c0 + wc)
        first_power = slice(n_steps - 1, n_steps) if reverse else slice(0, 1)
        ar = jnp.broadcast_to(pow_ref[first_power, re], (SUBLANES, wc))
        ai = jnp.broadcast_to(pow_ref[first_power, im], (SUBLANES, wc))
        rows = lambda r: pl.ds(pl.multiple_of(row0 + r * SUBLANES, SUBLANES), SUBLANES)
        step_of = lambda i: (n_steps - 1 - i) if reverse else i

        def local(i, carry, re=re, im=im, ar=ar, ai=ai):
            for u in range(unroll):
                r = step_of(i * unroll + u)
                carry = _cmul_add(x_ref[rows(r), re], x_ref[rows(r), im], ar, ai, *carry)
                x_ref[rows(r), re], x_ref[rows(r), im] = carry
            return carry

        zero = jnp.zeros((SUBLANES, wc), F32)
        lr, li = lax.fori_loop(0, n_steps // unroll, local, (zero, zero))

        tabs = [tab_ref[k, :, re] for k in range(8)]
        for lvl, k in enumerate((1, 2, 4)):
            sh = (SUBLANES - k) if reverse else k
            lr, li = _cmul_add(lr, li, tabs[2 * lvl], tabs[2 * lvl + 1], pltpu.roll(lr, sh, 0), pltpu.roll(li, sh, 0))
        cr, ci = carry_ref[0:1, re], carry_ref[0:1, im]
        lr, li = _cmul_add(lr, li, tabs[6], tabs[7], cr, ci)
        edge, away, last = (SUBLANES - 1, SUBLANES - 1, 0) if reverse else (0, 1, SUBLANES - 1)
        carry_ref[0:1, re] = lr[last:last + 1, :]
        carry_ref[0:1, im] = li[last:last + 1, :]
        er = jnp.where(sub == edge, cr, pltpu.roll(lr, away, 0))
        ei = jnp.where(sub == edge, ci, pltpu.roll(li, away, 0))

        def fix(j, acc, re=re, im=im, er=er, ei=ei, c0=c0):
            base = pl.ds(pl.multiple_of(j * SUBLANES, SUBLANES), SUBLANES)
            pw_r, pw_i = pow_ref[base, re], pow_ref[base, im]
            for i in range(SUBLANES):
                r = j * SUBLANES + i
                xr, xi = _cmul_add(x_ref[rows(r), re], x_ref[rows(r), im], pw_r[i:i + 1, :], pw_i[i:i + 1, :], er, ei)
                x_ref[rows(r), re], x_ref[rows(r), im] = xr, xi
                if fold is not None:
                    acc = fold(c0, r, xr, xi, acc)
            return acc

        acc = lax.fori_loop(0, n_steps // SUBLANES, fix, (zero, zero) if fold is not None else 0)
        if fold is not None:
            fold(c0, None, None, None, acc)


def _s5_fwd(name, u, b_blk, c_blk, a_f, tab_f, dskip, w_glu, b_glu):
    s, w = u.shape[0], w_glu.shape[0]
    nkb = w // LANES
    ns2 = b_blk.shape[2] // 2 * nkb
    half = ns2 // nkb
    t = min(S5_ROWS, s)
    nblk = s // t

    def body(u_ref, b_ref, c_ref, a_ref, tab_ref, ds_ref, wg_ref, bg_ref, y_ref, ys_ref, cs_ref, xs, carry):
        @pl.when(pl.program_id(0) == 0)
        def _():
            carry[...] = jnp.zeros_like(carry)

        cs_ref[0] = carry[...]
        for kb in range(nkb):
            bu = _dot(u_ref[:, kb * LANES:(kb + 1) * LANES], b_ref[kb], NN)
            xs[:, kb * half:(kb + 1) * half] = bu[:, :half]
            xs[:, ns2 + kb * half:ns2 + (kb + 1) * half] = bu[:, half:]
        _scan_rows(xs, 0, t // SUBLANES, ns2, a_ref, tab_ref, carry, reverse=False)
        for kb in range(nkb):
            cols = slice(kb * LANES, (kb + 1) * LANES)
            yk = _dot(xs[:, kb * half:(kb + 1) * half].astype(BF16), c_ref[kb, :half, :], NN)
            yk += _dot(xs[:, ns2 + kb * half:ns2 + (kb + 1) * half].astype(BF16), c_ref[kb, half:, :], NN)
            y_ref[:, cols] = yk + ds_ref[:, cols] * u_ref[:, cols].astype(F32)
        z = _gelu(y_ref[...])
        gate = _sigmoid(_dot(z.astype(BF16), wg_ref[...], NN) + bg_ref[...])
        ys_ref[...] = (z * gate).astype(BF16)

    row = pl.BlockSpec((t, w), lambda i: (i, 0))
    full = lambda shape: pl.BlockSpec(shape, lambda i: (0,) * len(shape))
    return pl.pallas_call(
        body, name=name, grid=(nblk,),
        in_specs=[row, full(b_blk.shape), full(c_blk.shape), full(a_f.shape), full(tab_f.shape), full(dskip.shape),
                  full(w_glu.shape), full(b_glu.shape)],
        out_specs=[row, row, pl.BlockSpec((1, 1, 2 * ns2), lambda i: (i, 0, 0))],
        out_shape=[jax.ShapeDtypeStruct((s, w), F32), jax.ShapeDtypeStruct((s, w), BF16),
                   jax.ShapeDtypeStruct((nblk, 1, 2 * ns2), F32)],
        scratch_shapes=[pltpu.VMEM((t, 2 * ns2), F32), pltpu.VMEM((1, 2 * ns2), F32)],
        compiler_params=_cparams(),
    )(u, b_blk, c_blk, a_f, tab_f, dskip, w_glu, b_glu)


def _s5_bwd(name, u, dys, y, carries, b_blk, c_blk, a_f, a_r, tab_f, tab_r, dskip, w_glu, b_glu):
    s, w = u.shape[0], w_glu.shape[0]
    nkb = w // LANES
    ns2 = b_blk.shape[2] // 2 * nkb
    half = ns2 // nkb
    t = min(S5_ROWS, s)
    nblk = s // t
    ng = t // SUBLANES

    def body(u_ref, dys_ref, y_ref, cs_ref, b_ref, c_ref, af_ref, ar_ref, tabf_ref, tabr_ref, ds_ref, wg_ref, bg_ref,
             du_ref, db_ref, dc_ref, da_ref, dwg_ref, vec_ref, xs, gs, dyv, fcarry, gcarry):
        @pl.when(pl.program_id(0) == 0)
        def _():
            db_ref[...] = jnp.zeros_like(db_ref)
            dc_ref[...] = jnp.zeros_like(dc_ref)
            da_ref[...] = jnp.zeros_like(da_ref)
            dwg_ref[...] = jnp.zeros_like(dwg_ref)
            vec_ref[...] = jnp.zeros_like(vec_ref)
            gcarry[...] = jnp.zeros_like(gcarry)

        yv = y_ref[...]
        z = _gelu(yv)
        zb = z.astype(BF16)
        gate = _sigmoid(_dot(zb, wg_ref[...], NN) + bg_ref[...])
        dout = dys_ref[...].astype(F32)
        dt = dout * z * gate * (1.0 - gate)
        dtb = dt.astype(BF16)
        dz = dout * gate + _dot(dtb, wg_ref[...], NT)
        dy = dz * _gelu_grad(yv)
        dyv[...] = dy
        dwg_ref[...] += _dot(zb, dtb, TN)
        vec_ref[0:1, :] += jnp.sum(dt, axis=0, keepdims=True)
        vec_ref[1:2, :] += jnp.sum(dy * u_ref[...].astype(F32), axis=0, keepdims=True)

        fcarry[...] = cs_ref[0]
        xs[0:SUBLANES, :] = jnp.broadcast_to(cs_ref[0], (SUBLANES, 2 * ns2))
        for kb in range(nkb):
            bu = _dot(u_ref[:, kb * LANES:(kb + 1) * LANES], b_ref[kb], NN)
            xs[SUBLANES:, kb * half:(kb + 1) * half] = bu[:, :half]
            xs[SUBLANES:, ns2 + kb * half:ns2 + (kb + 1) * half] = bu[:, half:]
        _scan_rows(xs, SUBLANES, ng, ns2, af_ref, tabf_ref, fcarry, reverse=False)
        first_segment = lax.broadcasted_iota(jnp.int32, (SUBLANES, 2 * ns2), 0) == 0
        xs[0:SUBLANES, :] = jnp.where(first_segment, xs[0:SUBLANES, :], pltpu.roll(xs[t:t + SUBLANES, :], 1, 0))

        for kb in range(nkb):
            dyk = dyv[:, kb * LANES:(kb + 1) * LANES].astype(BF16)
            re = slice(kb * half, (kb + 1) * half)
            im = slice(ns2 + kb * half, ns2 + (kb + 1) * half)
            gs[:, re] = _dot(dyk, c_ref[kb, :half, :], NT)
            gs[:, im] = _dot(dyk, c_ref[kb, half:, :], NT)
            dc_ref[kb, :half, :] += _dot(xs[SUBLANES:, re].astype(BF16), dyk, TN)
            dc_ref[kb, half:, :] += _dot(xs[SUBLANES:, im].astype(BF16), dyk, TN)

        def fold(c0, r, gr, gi, acc):
            wc = min(S5_CHUNK, ns2)
            re = slice(c0, c0 + wc)
            im = slice(ns2 + c0, ns2 + c0 + wc)
            if r is None:
                da_ref[:, re] += acc[0]
                da_ref[:, im] += acc[1]
                return acc
            before = pl.ds(pl.multiple_of(r * SUBLANES, SUBLANES), SUBLANES)
            xpr, xpi = xs[before, re], xs[before, im]
            return acc[0] + gr * xpr + gi * xpi, acc[1] - gr * xpi + gi * xpr

        _scan_rows(gs, 0, ng, ns2, ar_ref, tabr_ref, gcarry, reverse=True, fold=fold)

        for kb in range(nkb):
            cols = slice(kb * LANES, (kb + 1) * LANES)
            re = slice(kb * half, (kb + 1) * half)
            im = slice(ns2 + kb * half, ns2 + (kb + 1) * half)
            uk = u_ref[:, cols]
            gr = gs[:, re].astype(BF16)
            gi = gs[:, im].astype(BF16)
            db_ref[kb, :, :half] += _dot(uk, gr, TN)
            db_ref[kb, :, half:] += _dot(uk, gi, TN)
            duk = _dot(gr, b_ref[kb, :, :half], NT) + _dot(gi, b_ref[kb, :, half:], NT)
            du_ref[:, cols] = (duk + ds_ref[:, cols] * dyv[:, cols]).astype(BF16)

    rev = lambda i: (nblk - 1 - i, 0)
    row = pl.BlockSpec((t, w), rev)
    full = lambda shape: pl.BlockSpec(shape, lambda i: (0,) * len(shape))
    return pl.pallas_call(
        body, name=name, grid=(nblk,),
        in_specs=[row, row, row, pl.BlockSpec((1, 1, 2 * ns2), lambda i: (nblk - 1 - i, 0, 0)),
                  full(b_blk.shape), full(c_blk.shape), full(a_f.shape), full(a_r.shape), full(tab_f.shape),
                  full(tab_r.shape), full(dskip.shape), full(w_glu.shape), full(b_glu.shape)],
        out_specs=[row, full(b_blk.shape), full(c_blk.shape), full((SUBLANES, 2 * ns2)), full((w, w)),
                   full((SUBLANES, w))],
        out_shape=[jax.ShapeDtypeStruct((s, w), BF16), jax.ShapeDtypeStruct(b_blk.shape, F32),
                   jax.ShapeDtypeStruct(c_blk.shape, F32), jax.ShapeDtypeStruct((SUBLANES, 2 * ns2), F32),
                   jax.ShapeDtypeStruct((w, w), F32), jax.ShapeDtypeStruct((SUBLANES, w), F32)],
        scratch_shapes=[pltpu.VMEM((t + SUBLANES, 2 * ns2), F32), pltpu.VMEM((t, 2 * ns2), F32),
                        pltpu.VMEM((t, w), F32), pltpu.VMEM((1, 2 * ns2), F32), pltpu.VMEM((1, 2 * ns2), F32)],
        compiler_params=_cparams(),
    )(u, dys, y, carries, b_blk, c_blk, a_f, a_r, tab_f, tab_r, dskip, w_glu, b_glu)


def _log_sigmoid(x):
    return jnp.minimum(x, 0.0) - jnp.log(1.0 + jnp.exp(-jnp.abs(x)))


def _cum_fwd(name, f_t, b_f):
    h, s = f_t.shape
    tc = _pick(s, 512)
    nb = s // tc

    def body(f_ref, b_ref, c_ref, carry):
        @pl.when(pl.program_id(0) == 0)
        def _():
            carry[...] = jnp.zeros_like(carry)

        lf = _log_sigmoid(f_ref[...] + b_ref[...])
        upper = (lax.broadcasted_iota(jnp.int32, (tc, tc), 0) <= lax.broadcasted_iota(jnp.int32, (tc, tc), 1))
        cum = lax.dot_general(lf, upper.astype(F32), NN, precision=lax.Precision.HIGHEST,
                              preferred_element_type=F32) + carry[...]
        c_ref[...] = cum
        carry[...] += jnp.sum(lf, axis=1, keepdims=True)

    blk = pl.BlockSpec((h, tc), lambda i: (0, i))
    return pl.pallas_call(body, name=name, grid=(nb,), in_specs=[blk, pl.BlockSpec((h, 1), lambda i: (0, 0))],
                          out_specs=blk, out_shape=jax.ShapeDtypeStruct((h, s), F32),
                          scratch_shapes=[pltpu.VMEM((h, 1), F32)], compiler_params=_cparams())(f_t, b_f)


def _cum_bwd(name, dcq, dck, f_t, b_f):
    h, s = f_t.shape
    tc = _pick(s, 512)
    nb = s // tc

    def body(dcq_ref, dck_ref, f_ref, b_ref, df_ref, db_ref, carry):
        @pl.when(pl.program_id(0) == 0)
        def _():
            carry[...] = jnp.zeros_like(carry)
            db_ref[...] = jnp.zeros_like(db_ref)

        dc = dcq_ref[...] + dck_ref[...]
        lower = (lax.broadcasted_iota(jnp.int32, (tc, tc), 0) >= lax.broadcasted_iota(jnp.int32, (tc, tc), 1))
        dlf = lax.dot_general(dc, lower.astype(F32), NN, precision=lax.Precision.HIGHEST,
                              preferred_element_type=F32) + carry[...]
        carry[...] += jnp.sum(dc, axis=1, keepdims=True)
        df = dlf * _sigmoid(-(f_ref[...] + b_ref[...]))
        df_ref[...] = df
        db_ref[...] += jnp.broadcast_to(jnp.sum(df, axis=1, keepdims=True), db_ref.shape)

    blk = pl.BlockSpec((h, tc), lambda i: (0, nb - 1 - i))
    return pl.pallas_call(
        body, name=name, grid=(nb,), in_specs=[blk, blk, blk, pl.BlockSpec((h, 1), lambda i: (0, 0))],
        out_specs=[blk, pl.BlockSpec((h, LANES), lambda i: (0, 0))],
        out_shape=[jax.ShapeDtypeStruct((h, s), F32), jax.ShapeDtypeStruct((h, LANES), F32)],
        scratch_shapes=[pltpu.VMEM((h, 1), F32)], compiler_params=_cparams())(dcq, dck, f_t, b_f)


def _attn_fwd(name, qkv, q_blk, k_blk, v_blk, n_pairs, ck, side=None):
    s = qkv.shape[0]
    dh = LANES // 2
    t = min(ATT_BLOCK, s)
    nq = s // t
    scale = dh ** -0.5

    def body(q_ref, k_ref, v_ref, ck_ref, o_ref, lse_ref, m_s, acc_s):
        i = pl.program_id(1)
        low = lax.broadcasted_iota(jnp.int32, (1, LANES), 1) < dh
        qs = (q_ref[...].astype(F32) * scale).astype(BF16)
        zero = jnp.zeros_like(qs)
        qh = (jnp.where(low, qs, zero), jnp.where(low, zero, qs))
        m_s[...] = jnp.full(m_s.shape, -1e30, F32)
        acc_s[...] = jnp.zeros_like(acc_s)
        causal = (lax.broadcasted_iota(jnp.int32, (t, t), 1) <= lax.broadcasted_iota(jnp.int32, (t, t), 0))

        def step(j, diagonal):
            r0 = pl.multiple_of(j * t, t)
            kj = k_ref[pl.ds(r0, t), :]
            vj = v_ref[pl.ds(r0, t), :]
            one = jnp.ones_like(vj)
            vh = (jnp.where(low, vj, one), jnp.where(low, one, vj))
            for hd in range(2):
                sc = _dot(qh[hd], kj, NT) - ck_ref[hd, j]
                if diagonal:
                    sc = jnp.where(causal, sc, -1e30)
                m_old = m_s[hd]
                m_new = jnp.maximum(m_old, jnp.max(sc, axis=1, keepdims=True))
                p = jnp.exp(sc - m_new)
                acc_s[hd] = jnp.exp(m_old - m_new) * acc_s[hd] + _dot(p.astype(BF16), vh[hd], NN)
                m_s[hd] = m_new

        def full(j, _):
            step(j, False)
            return 0

        lax.fori_loop(0, i, full, 0)
        step(i, True)
        a0, a1 = acc_s[0], acc_s[1]
        o_ref[...] = jnp.where(low, a0 / pltpu.roll(a0, dh, 1), a1 / pltpu.roll(a1, dh, 1)).astype(BF16)
        lse_ref[0] = m_s[0] + jnp.log(a0[:, dh:dh + 1])
        lse_ref[1] = m_s[1] + jnp.log(a1[:, 0:1])

    return _hosted_call(
        body, side, name, (n_pairs, nq),
        [pl.BlockSpec((t, LANES), lambda hp, i: (i, q_blk + hp)),
         pl.BlockSpec((s, LANES), lambda hp, i: (0, k_blk + hp)),
         pl.BlockSpec((s, LANES), lambda hp, i: (0, v_blk + hp)),
         pl.BlockSpec((2, nq, 1, t), lambda hp, i: (hp, 0, 0, 0))],
        [pl.BlockSpec((t, LANES), lambda hp, i: (i, hp)), pl.BlockSpec((2, t, 1), lambda hp, i: (hp, i, 0))],
        [jax.ShapeDtypeStruct((s, LANES * n_pairs), BF16), jax.ShapeDtypeStruct((2 * n_pairs, s, 1), F32)],
        [pltpu.VMEM((2, t, 1), F32), pltpu.VMEM((2, t, LANES), F32)], (qkv, qkv, qkv, ck))


def _attn_bwd(name, qkv, q_blk, k_blk, v_blk, n_pairs, o, do, lse_rows, ck_cols, side=None):
    s = qkv.shape[0]
    dh = LANES // 2
    t = min(ATT_BLOCK, s)
    nk = s // t
    scale = dh ** -0.5

    def body(q_ref, k_ref, v_ref, o_ref, do_ref, lse_ref, ck_ref,
             dq_ref, dk_ref, dv_ref, dcq_ref, dck_ref, delta, dqt, dk_acc, dv_acc):
        j = pl.program_id(1)
        low = lax.broadcasted_iota(jnp.int32, (1, LANES), 1) < dh
        low_rows = lax.broadcasted_iota(jnp.int32, (LANES, 1), 0) < dh

        @pl.when(j == 0)
        def _():
            dqt[...] = jnp.zeros_like(dqt)
            sel = (jnp.broadcast_to(low, (SUBLANES, LANES)).astype(F32), jnp.broadcast_to(~low, (SUBLANES, LANES)).astype(F32))

            def fill(i, _):
                r0 = pl.multiple_of(i * t, t)
                prod = do_ref[pl.ds(r0, t), :].astype(F32) * o_ref[pl.ds(r0, t), :].astype(F32)
                for hd in range(2):
                    delta[hd, i] = lax.dot_general(sel[hd], prod, NT, precision=lax.Precision.HIGHEST,
                                                   preferred_element_type=F32)
                return 0

            lax.fori_loop(0, nk, fill, 0)

        kj, vj = k_ref[...], v_ref[...]
        zero, one = jnp.zeros_like(kj), jnp.ones_like(kj)
        kh = (jnp.where(low, kj, zero), jnp.where(low, zero, kj))
        vh = (jnp.where(low, vj, zero), jnp.where(low, zero, vj))
        kjt = kj.astype(F32).T.astype(BF16)
        one_t = jnp.ones_like(kjt)
        kht = (jnp.where(low_rows, kjt, one_t), jnp.where(low_rows, one_t, kjt))
        dk_acc[...] = jnp.zeros_like(dk_acc)
        dv_acc[...] = jnp.zeros_like(dv_acc)
        causal_t = (lax.broadcasted_iota(jnp.int32, (t, t), 0) <= lax.broadcasted_iota(jnp.int32, (t, t), 1))

        def step(i, diagonal):
            r0 = pl.multiple_of(i * t, t)
            qi = (q_ref[pl.ds(r0, t), :].astype(F32) * scale).astype(BF16)
            doi = do_ref[pl.ds(r0, t), :]
            qone, dzero = jnp.ones_like(qi), jnp.zeros_like(doi)
            qsel = (jnp.where(low, qi, qone), jnp.where(low, qone, qi))
            dosel = (jnp.where(low, doi, dzero), jnp.where(low, dzero, doi))
            for hd in range(2):
                st = _dot(kh[hd], qi, NT) - ck_ref[hd] - lse_ref[hd, i]
                pt = jnp.exp(st)
                if diagonal:
                    pt = jnp.where(causal_t, pt, 0.0)
                dst = pt * (_dot(vh[hd], doi, NT) - delta[hd, i, 0:1, :])
                dsb = dst.astype(BF16)
                dv_acc[...] += _dot(pt.astype(BF16), dosel[hd], NN)
                dk_acc[hd] += _dot(dsb, qsel[hd], NN)
                dqt[hd, i] += _dot(kht[hd], dsb, NN)

        step(j, True)

        def rest(i, _):
            step(i, False)
            return 0

        lax.fori_loop(j + 1, nk, rest, 0)
        dk_ref[...] = jnp.where(low, dk_acc[0], dk_acc[1]).astype(BF16)
        dv_ref[...] = dv_acc[...].astype(BF16)
        dck_ref[0] = -dk_acc[0][:, dh:dh + 1]
        dck_ref[1] = -dk_acc[1][:, 0:1]

        @pl.when(j == nk - 1)
        def _():
            def emit(i, _):
                r0 = pl.multiple_of(i * t, t)
                d0, d1 = dqt[0, i], dqt[1, i]
                dq_ref[pl.ds(r0, t), :] = (jnp.where(low_rows, d0, d1) * scale).T.astype(BF16)
                dcq_ref[0, i] = d0[dh:dh + 1, :]
                dcq_ref[1, i] = d1[0:1, :]
                return 0

            lax.fori_loop(0, nk, emit, 0)

    col_blk = lambda base: pl.BlockSpec((t, LANES), lambda hp, j: (j, base + hp))
    col_all = lambda base: pl.BlockSpec((s, LANES), lambda hp, j: (0, base + hp))
    rows_all = pl.BlockSpec((2, nk, 1, t), lambda hp, j: (hp, 0, 0, 0))
    return _hosted_call(
        body, side, name, (n_pairs, nk),
        [col_all(q_blk), col_blk(k_blk), col_blk(v_blk), col_all(0), col_all(0), rows_all,
         pl.BlockSpec((2, t, 1), lambda hp, j: (hp, j, 0))],
        [col_all(0), col_blk(0), col_blk(0), rows_all, pl.BlockSpec((2, t, 1), lambda hp, j: (hp, j, 0))],
        [jax.ShapeDtypeStruct((s, LANES * n_pairs), BF16), jax.ShapeDtypeStruct((s, LANES * n_pairs), BF16),
         jax.ShapeDtypeStruct((s, LANES * n_pairs), BF16), jax.ShapeDtypeStruct((2 * n_pairs, nk, 1, t), F32),
         jax.ShapeDtypeStruct((2 * n_pairs, s, 1), F32)],
        [pltpu.VMEM((2, nk, SUBLANES, t), F32), pltpu.VMEM((2, nk, LANES, t), F32),
         pltpu.VMEM((2, t, LANES), F32), pltpu.VMEM((t, LANES), F32)],
        (qkv, qkv, qkv, o, do, lse_rows, ck_cols))


def _adamw(name, w, g, m, v):
    n_l, r, c = w.shape
    by_rows = r % SUBLANES == 0
    tr = _pick8(r, max(SUBLANES, ROW_TILE_BYTES // (4 * c))) if by_rows else r
    tl = 1 if by_rows else max(t for t in range(1, n_l + 1) if n_l % t == 0 and t * r * c * 4 <= ROW_TILE_BYTES)

    def body(w_ref, g_ref, m_ref, v_ref, d_ref, mo_ref, vo_ref):
        gv = g_ref[...]
        m2 = ADAM_B1 * m_ref[...] + (1.0 - ADAM_B1) * gv
        v2 = ADAM_B2 * v_ref[...] + (1.0 - ADAM_B2) * (gv * gv)
        m_hat = m2 / (1.0 - ADAM_B1 ** ADAM_STEP)
        v_hat = v2 / (1.0 - ADAM_B2 ** ADAM_STEP)
        d_ref[...] = -ADAM_LR * (m_hat / (jnp.sqrt(v_hat) + ADAM_EPS) + ADAM_WD * w_ref[...])
        mo_ref[...] = m2
        vo_ref[...] = v2

    blk = pl.BlockSpec((None, tr, c) if by_rows else (tl, r, c), lambda l, i: (l, i, 0))
    sh = jax.ShapeDtypeStruct((n_l, r, c), F32)
    return pl.pallas_call(body, name=name, grid=(n_l // tl, r // tr), in_specs=[blk] * 4,
                          out_specs=[blk] * 3, out_shape=[sh, sh, sh], compiler_params=_cparams())(w, g, m, v)


def _pick8(dim, target, mult=SUBLANES):
    best, t = None, mult
    while t <= min(dim, target):
        if dim % t == 0:
            best = t
        t += mult
    return best or dim


BF16_ROWS = 16


def _sum_blocks(name, x, out_dtype):
    n, r, c = x.shape
    tr = _pick8(r, max(BF16_ROWS, SUM_TILE_BYTES // (4 * c)), BF16_ROWS)

    def body(x_ref, o_ref):
        acc = x_ref[0].astype(F32)
        for i in range(1, n):
            acc = acc + x_ref[i].astype(F32)
        o_ref[...] = acc.astype(out_dtype)

    return pl.pallas_call(body, name=name, grid=(r // tr,),
                          in_specs=[pl.BlockSpec((n, tr, c), lambda i: (0, i, 0))],
                          out_specs=pl.BlockSpec((tr, c), lambda i: (i, 0)),
                          out_shape=jax.ShapeDtypeStruct((r, c), out_dtype), compiler_params=_cparams())(x)


def _all_gather(name, x_shard):
    m_per, n = x_shard.shape

    def body(x_ref, out_ref, send_sems, recv_sems):
        x, y, c = lax.axis_index("x"), lax.axis_index("y"), lax.axis_index("c")
        me, sibling = (x, y, c), (x, y, 1 - c)
        chips = [(1 - x, y), (x, 1 - y), (1 - x, 1 - y)]

        def rows(px, py, pc):
            return out_ref.at[pl.ds((4 * px + 2 * py + pc) * m_per, m_per), :]

        def copy(k, block, to, src=None):
            return pltpu.make_async_remote_copy(
                src_ref=rows(*block) if src is None else src, dst_ref=rows(*block),
                send_sem=send_sems.at[k], recv_sem=recv_sems.at[k], device_id=to, device_id_type=MESH)

        first = [copy(0, me, sibling, src=x_ref)]
        first += [copy(1 + j, me, (*chip, c), src=x_ref) for j, chip in enumerate(chips)]
        for cp in first:
            cp.start()
        passed = [copy(4 + j, (*chip, c), sibling) for j, chip in enumerate(chips)]
        for j, chip in enumerate(chips):
            copy(1 + j, (*chip, c), me).wait_recv()
            passed[j].start()
        copy(0, sibling, me).wait_recv()
        for j, chip in enumerate(chips):
            copy(4 + j, (*chip, 1 - c), me).wait_recv()
        for cp in first + passed:
            cp.wait_send()

    out = pl.pallas_call(
        body, name=name, out_shape=jax.ShapeDtypeStruct((N_DEV * m_per, n), x_shard.dtype),
        in_specs=[pl.BlockSpec(memory_space=pl.ANY)], out_specs=pl.BlockSpec(memory_space=pl.ANY),
        scratch_shapes=[pltpu.SemaphoreType.DMA((7,)), pltpu.SemaphoreType.DMA((7,))],
    )(x_shard)
    my_dev = 4 * lax.axis_index("x") + 2 * lax.axis_index("y") + lax.axis_index("c")
    return lax.dynamic_update_slice(out, x_shard, (my_dev * m_per, 0))


def _put_own(out, own, index):
    start = tuple(index) + (0,) * own.ndim
    return lax.dynamic_update_slice(out, own.reshape((1,) * len(index) + own.shape), start)


def _gather_copies(stage, ins, outs, send_sems, recv_sems):
    x, y, c = lax.axis_index("x"), lax.axis_index("y"), lax.axis_index("c")
    my_chip = 2 * x + y
    copies = []
    for w, out in enumerate(outs):
        half = out.shape[1] // 2
        rows = pl.ds(c * half, half)
        for k, (cx, cy) in enumerate([(1 - x, y), (x, 1 - y), (1 - x, 1 - y)]):
            if stage == 0:
                src, dst, to = ins[w].at[rows], out.at[my_chip, rows], (cx, cy, c)
            else:
                src = dst = out.at[2 * cx + cy, rows]
                to = (x, y, 1 - c)
            copies.append(pltpu.make_async_remote_copy(
                src_ref=src, dst_ref=dst, send_sem=send_sems.at[3 * w + k], recv_sem=recv_sems.at[3 * w + k],
                device_id=to, device_id_type=MESH))
    return copies


def _gathered_shapes(shards):
    return [jax.ShapeDtypeStruct((N_CHIPS,) + s.shape, s.dtype) for s in shards]


def _put_own_slabs(gathered, shards):
    my_chip = 2 * lax.axis_index("x") + lax.axis_index("y")
    return [_put_own(o, s, (my_chip,)) for o, s in zip(gathered, shards)]


def _gather_layer(name, shards):
    n_w = len(shards)

    def body(*refs):
        ins, outs = refs[:n_w], refs[n_w:2 * n_w]
        for stage in (0, 1):
            copies = _gather_copies(stage, ins, outs, refs[2 * n_w + 2 * stage], refs[2 * n_w + 2 * stage + 1])
            for cp in copies:
                cp.start()
            for cp in copies:
                cp.wait()

    outs = pl.pallas_call(
        body, name=name, out_shape=_gathered_shapes(shards),
        in_specs=[pl.BlockSpec(memory_space=pl.ANY)] * n_w, out_specs=[pl.BlockSpec(memory_space=pl.ANY)] * n_w,
        scratch_shapes=[pltpu.SemaphoreType.DMA((3 * n_w,))] * 4,
    )(*shards)
    return _put_own_slabs(outs, shards)


def _gather_side_jobs(shards):
    between_chips = _SideJob(list(shards), _gathered_shapes(shards), {}, 3 * len(shards),
                             lambda ins, outs, send, recv: _gather_copies(0, ins, outs, send, recv))
    between_cores = lambda partial: _SideJob(
        list(partial), [jax.ShapeDtypeStruct(p.shape, p.dtype) for p in partial], {w: w for w in range(len(partial))},
        3 * len(partial), lambda ins, outs, send, recv: _gather_copies(1, ins, outs, send, recv))
    return between_chips, between_cores


def _run_job(name, job):
    n_in, n_out = len(job.arrays), len(job.out_shapes)

    def body(*refs):
        copies = job.copies(refs[:n_in], refs[n_in:n_in + n_out], refs[n_in + n_out], refs[n_in + n_out + 1])
        for cp in copies:
            cp.start()
        for cp in copies:
            cp.wait()

    hbm = pl.BlockSpec(memory_space=pl.ANY)
    return pl.pallas_call(
        body, name=name, out_shape=list(job.out_shapes), in_specs=[hbm] * n_in, out_specs=[hbm] * n_out,
        scratch_shapes=[pltpu.SemaphoreType.DMA((job.n_sems,))] * 2, input_output_aliases=dict(job.aliases),
    )(*job.arrays)


def _swap_job(grads):
    def copies(ins, outs, send_sems, recv_sems):
        x, y, c = lax.axis_index("x"), lax.axis_index("y"), lax.axis_index("c")
        return [pltpu.make_async_remote_copy(
            src_ref=g.at[:, pl.ds((1 - c) * (g.shape[1] // 2), g.shape[1] // 2)], dst_ref=outs[w],
            send_sem=send_sems.at[w], recv_sem=recv_sems.at[w], device_id=(x, y, 1 - c), device_id_type=MESH)
            for w, g in enumerate(ins)]

    shapes = [jax.ShapeDtypeStruct((g.shape[0], g.shape[1] // 2, g.shape[2]), g.dtype) for g in grads]
    return _SideJob(list(grads), shapes, {}, len(grads), copies)


def _exchange_job(parts):
    def copies(ins, outs, send_sems, recv_sems):
        x, y, c = lax.axis_index("x"), lax.axis_index("y"), lax.axis_index("c")
        return [pltpu.make_async_remote_copy(
            src_ref=ins[w].at[2 * cx + cy], dst_ref=outs[w].at[2 * x + y], send_sem=send_sems.at[3 * w + k],
            recv_sem=recv_sems.at[3 * w + k], device_id=(cx, cy, c), device_id_type=MESH)
            for w in range(len(ins)) for k, (cx, cy) in enumerate([(1 - x, y), (x, 1 - y), (1 - x, 1 - y)])]

    return _SideJob(list(parts), [jax.ShapeDtypeStruct(p.shape, p.dtype) for p in parts], {}, 3 * len(parts), copies)


def _share_job(bufs, layers):
    def copies(ins, outs, send_sems, recv_sems):
        x, y, c = lax.axis_index("x"), lax.axis_index("y"), lax.axis_index("c")
        mine = [o.at[layer, pl.ds(c * (o.shape[1] // 2), o.shape[1] // 2)] for o, ls in zip(outs, layers) for layer in ls]
        return [pltpu.make_async_remote_copy(src_ref=rows, dst_ref=rows, send_sem=send_sems.at[k], recv_sem=recv_sems.at[k],
                                             device_id=(x, y, 1 - c), device_id_type=MESH) for k, rows in enumerate(mine)]

    return _SideJob(list(bufs), [jax.ShapeDtypeStruct(b.shape, b.dtype) for b in bufs], {w: w for w in range(len(bufs))},
                    sum(len(ls) for ls in layers), copies)


def _sum_into(name, blocks, core, layer, depth, into):
    n, r, c = blocks.shape
    tr = _pick8(r, max(BF16_ROWS, SUM_TILE_BYTES // (4 * c)), BF16_ROWS)
    steps = r // tr

    def body(core_ref, x_ref, *rest):
        acc = x_ref[0].astype(F32)
        for i in range(1, n):
            acc = acc + x_ref[i].astype(F32)
        rest[-1][...] = acc

    grid_spec = pltpu.PrefetchScalarGridSpec(
        num_scalar_prefetch=1, grid=(steps,),
        in_specs=[pl.BlockSpec((n, tr, c), lambda i, core_ref: (0, i, 0))]
        + ([pl.BlockSpec(memory_space=pl.ANY)] if into is not None else []),
        out_specs=pl.BlockSpec((None, tr, c), lambda i, core_ref: (layer, core_ref[0] * steps + i, 0)))
    return pl.pallas_call(
        body, name=name, grid_spec=grid_spec, out_shape=jax.ShapeDtypeStruct((depth, 2 * r, c), F32),
        input_output_aliases={2: 0} if into is not None else {}, compiler_params=_cparams(),
    )(core, blocks, *([into] if into is not None else []))


def _add_rows(name, grads, recv, core):
    n, r, c = recv.shape
    tr = _pick8(r, max(BF16_ROWS, SUM_TILE_BYTES // (4 * c)), BF16_ROWS)
    steps = r // tr

    def body(core_ref, g_ref, r_ref, o_ref):
        o_ref[...] = (g_ref[...].astype(F32) + r_ref[...].astype(F32)).astype(BF16)

    grid_spec = pltpu.PrefetchScalarGridSpec(
        num_scalar_prefetch=1, grid=(steps,),
        in_specs=[pl.BlockSpec((n, tr, c), lambda i, core_ref: (0, core_ref[0] * steps + i, 0)),
                  pl.BlockSpec((n, tr, c), lambda i, core_ref: (0, i, 0))],
        out_specs=pl.BlockSpec((n, tr, c), lambda i, core_ref: (0, i, 0)))
    return pl.pallas_call(body, name=name, grid_spec=grid_spec,
                          out_shape=jax.ShapeDtypeStruct((n, r, c), BF16), compiler_params=_cparams())(core, grads, recv)


class _LayerReduce:
    def __init__(self, tag, layer, depth, names, grads, core, bufs):
        self.tag, self.layer, self.depth, self.names, self.core, self.bufs = tag, layer, depth, list(names), core, bufs
        self.state = list(grads)

    def _exchange(self, name, job, carry):
        if carry is None:
            return None, _run_job(f"{name}_{self.tag}", job)
        return carry(job)

    def swap_and_add(self, carry=None):
        grads = self.state
        results, recv = self._exchange("grads_swap_cores", _swap_job(grads), carry)
        self.state = [_add_rows(f"grads_add_{n}_{self.tag}", g, r, self.core) for n, g, r in zip(self.names, grads, recv)]
        return results

    def exchange_and_sum(self, carry=None, also=()):
        group = [self] + list(also)
        results, arrived = self._exchange("grads_exchange_chips", _exchange_job([p for r in group for p in r.state]), carry)
        my_chip = 2 * lax.axis_index("x") + lax.axis_index("y")
        for r in group:
            mine, arrived = arrived[:len(r.state)], arrived[len(r.state):]
            for n, a, p in zip(r.names, mine, r.state):
                a = _put_own(a, lax.dynamic_index_in_dim(p, my_chip, 0, keepdims=False), (my_chip,))
                r.bufs[n] = _sum_into(f"grads_sum_{n}_{r.tag}", a, r.core, r.layer, r.depth, r.bufs.get(n))
        return results

    def share(self, carry=None, also=()):
        layers = {}
        for r in [self] + list(also):
            for n in r.names:
                layers.setdefault(n, []).append(r.layer)
        names = list(layers)
        job = _share_job([self.bufs[n] for n in names], [layers[n] for n in names])
        results, outs = self._exchange("grads_share_cores", job, carry)
        self.bufs.update(zip(names, outs))
        return results


def _pack(arrays, cols, row_multiple, dtype):
    flat = jnp.concatenate([a.reshape(-1).astype(dtype) for a in arrays])
    unit = cols * row_multiple
    total = -(-flat.shape[0] // unit) * unit
    return jnp.pad(flat, (0, total - flat.shape[0])).reshape(total // cols, cols)


def _unpack(buf, shapes):
    flat, out, off = buf.reshape(-1), [], 0
    for sh in shapes:
        n = math.prod(sh)
        out.append(flat[off:off + n].reshape(sh))
        off += n
    return out


def _discretize(lam_re, lam_im, log_dt, b_re, b_im):
    lam = lax.complex(jnp.minimum(lam_re, -EIG_CLIP), lam_im)
    dt = jnp.exp(log_dt)[:, None]
    lam_bar = jnp.exp(lam * dt)
    b_bar = ((lam_bar - 1.0) / lam)[..., None] * lax.complex(b_re, b_im)
    return jnp.real(lam_bar), jnp.imag(lam_bar), jnp.real(b_bar), jnp.imag(b_bar)


def _scan_tables(ar, ai):
    a = lax.complex(ar, ai)
    pw = [a]
    for _ in range(7):
        pw.append(pw[-1] * a)
    rows = jnp.arange(SUBLANES)[:, None]

    def build(p, reverse):
        tabs = []
        for k in (1, 2, 4):
            keep = (rows <= SUBLANES - 1 - k) if reverse else (rows >= k)
            tk = jnp.where(keep, p[k - 1][None, :], 0.0)
            tabs += [jnp.real(tk), jnp.imag(tk)]
        stack = jnp.stack(p[::-1] if reverse else p)
        tabs += [jnp.real(stack), jnp.imag(stack)]
        return jnp.stack(tabs).astype(F32)

    return build(pw, False), build([jnp.conj(p) for p in pw], True)


def _interleave_rows(a, t):
    s, w = a.shape
    return a.reshape(s // t, SUBLANES, t // SUBLANES, w).transpose(0, 2, 1, 3).reshape(s, w)


def _deinterleave_rows(a, t):
    s, w = a.shape
    return a.reshape(s // t, t // SUBLANES, SUBLANES, w).transpose(0, 2, 1, 3).reshape(s, w)


def _block_diag(per_group, groups_per_block):
    g, a, b = per_group.shape
    x = per_group.reshape(g // groups_per_block, groups_per_block, a, b)
    eye = jnp.eye(groups_per_block, dtype=per_group.dtype)
    out = x[:, :, :, None, :] * eye[None, :, None, :, None]
    return out.reshape(g // groups_per_block, groups_per_block * a, groups_per_block * b)


def _block_diag_extract(dense, groups_per_block, a, b):
    nkb = dense.shape[0]
    x = dense.reshape(nkb, groups_per_block, a, groups_per_block, b)
    idx = jnp.arange(groups_per_block)
    return x[:, idx, :, idx, :].transpose(1, 0, 2, 3).reshape(nkb * groups_per_block, a, b)


def _layer_fwd(tag, x, mod, p, wts, carried=None):
    s, d = x.shape
    w_ssm, w_att = p["w_glu"].shape[0], p["w_att"]
    heads = p["b_f"].shape[0]
    dh = w_att // heads
    cs = d // N_CHIPS
    tm = _pick(s, 1024)
    row = lambda v: v.reshape(1, -1)
    sv = {}

    h = _prenorm_fwd(f"prenorm_mix_{tag}", x, row(p["g_pre_mix"]), row(mod[1]), row(mod[0]))
    uqkv = _mm_plain(f"proj_main_{tag}", h, p["w_main"], "nn", BF16, tm=1024, tn=1024, tk=1024)
    fg = _mm_plain(f"proj_gate_{tag}", h, p["w_gates"], "nn", F32, tm=1024, tn=1024, tk=1024)
    f_t = fg[:, 2 * d:2 * d + heads].T

    t5 = min(S5_ROWS, s)
    u_il = _interleave_rows(uqkv[:, :w_ssm], t5)
    y_s5, ys_il, carries = _s5_fwd(f"s5_fwd_{tag}", u_il, p["b_blk"], p["c_blk"], p["a_f"], p["tab_f"],
                                   row(p["d_skip"]), p["w_glu"], row(p["b_glu"]))
    ys = _deinterleave_rows(ys_il, t5)

    assert dh * 2 == LANES and w_ssm % LANES == 0 and w_att % LANES == 0
    n_pairs = w_att // LANES
    blocks = (w_ssm // LANES, w_ssm // LANES + n_pairs, w_ssm // LANES + 2 * n_pairs)
    cum = _cum_fwd(f"cum_fwd_{tag}", f_t, p["b_f"].reshape(heads, 1))
    t = min(ATT_BLOCK, s)
    ck_cols, ck_rows = cum.reshape(heads, s, 1), cum.reshape(heads, s // t, 1, t)
    late_names, late_shards, next_shards = carried if carried else ((), [], [])
    chips_job, cores_job = _gather_side_jobs(list(late_shards) + list(next_shards)) if carried else (None, None)
    (ya, lse), arrived = _attn_fwd(f"attn_fwd_{tag}", uqkv, *blocks, n_pairs, ck_rows, side=chips_job)
    if late_names:
        late = _run_job(f"gather_weights_late_{tag}", cores_job(arrived[:len(late_names)]))
        wts = {**wts, **dict(zip(late_names, _put_own_slabs(late, late_shards)))}
        arrived = arrived[len(late_names):]
    fs = wts["w_ffn_down"].shape[1]

    tile = pl.BlockSpec((tm, cs), lambda i, j, k: (i, j))
    slab = lambda rows: pl.BlockSpec((None, rows, cs), lambda i, j, k: (j, 0, 0))

    def merge(acc, extra_refs, out_refs):
        ya_ref, wpb_ref, ga_ref, gb_ref = extra_refs
        a_ref, b_ref, m_ref = out_refs
        bv = _dot(ya_ref[...], wpb_ref[...], NN)
        a_ref[...] = acc.astype(BF16)
        b_ref[...] = bv.astype(BF16)
        m_ref[...] = (_sigmoid(ga_ref[...]) * acc + _sigmoid(gb_ref[...]) * bv).astype(BF16)

    sd_bf = jax.ShapeDtypeStruct((s, d), BF16)
    pa, pb, merged = _mm_raw(
        f"merge_{tag}", ys, wts["w_pa"], "nn", (s // tm, N_CHIPS, 1), (tm, cs),
        pl.BlockSpec((tm, w_ssm), lambda i, j, k: (i, 0)), slab(w_ssm), [sd_bf] * 3, [tile] * 3, merge,
        extra=(ya, wts["w_pb"], fg, fg),
        extra_specs=[pl.BlockSpec((tm, w_att), lambda i, j, k: (i, 0)), slab(w_att), tile,
                     pl.BlockSpec((tm, cs), lambda i, j, k: (i, j + N_CHIPS))])

    tm2 = _pick(s, POSTNORM_ROWS)
    x1, y_mix = _mm_postnorm(
        f"out_proj_{tag}", merged, pl.BlockSpec((tm2, cs), lambda i, j, k: (i, k)), wts["w_o"],
        pl.BlockSpec((None, cs, d), lambda i, j, k: (k, 0, 0)), N_CHIPS, x, row(mod[2]), row(p["g_post_mix"]))

    h2 = _prenorm_fwd(f"prenorm_ffn_{tag}", x1, row(p["g_pre_ffn"]), row(mod[4]), row(mod[3]))
    (a4, b4, hid4), next_wts = _ffn_up(f"ffn_up_{tag}", h2, wts["w_ffn_gate"], wts["w_ffn_up"],
                                       side=cores_job(arrived) if carried and arrived else None)
    x2, y_ffn = _mm_postnorm(
        f"ffn_down_{tag}", hid4, pl.BlockSpec((None, tm2, fs), lambda i, j, k: (k, i, 0)), wts["w_ffn_down"],
        pl.BlockSpec((None, fs, d), lambda i, j, k: (k, 0, 0)), N_CHIPS, x1, row(mod[5]), row(p["g_post_ffn"]))

    sv.update(x=x, h=h, uqkv=uqkv, u_il=u_il, fg=fg, f_t=f_t, y_s5=y_s5, ys=ys, carries=carries, blocks=blocks,
              ck_cols=ck_cols, lse_rows=lse.reshape(heads, s // t, 1, t), ya=ya, pa=pa, pb=pb, merged=merged, x1=x1,
              y_mix=y_mix, h2=h2, a4=a4, b4=b4, hid4=hid4, y_ffn=y_ffn)
    return x2, sv, wts, next_wts


def _mm_postnorm(name, a, a_spec, w, w_spec, nk, x, gate, g):
    s, d = x.shape
    tm = _pick(s, POSTNORM_ROWS)
    rowspec = pl.BlockSpec((tm, d), lambda i, j, k: (i, 0))
    vec = pl.BlockSpec((1, d), lambda i, j, k: (0, 0))

    def epilogue(acc, extra_refs, out_refs):
        x_ref, gate_ref, g_ref = extra_refs
        r = lax.rsqrt(jnp.mean(acc * acc, axis=-1, keepdims=True) + RMS_EPS)
        out_refs[0][...] = x_ref[...] + gate_ref[...] * (acc * r * g_ref[...])
        out_refs[1][...] = acc

    sd = jax.ShapeDtypeStruct((s, d), F32)
    return _mm_raw(name, a, w, "nn", (s // tm, 1, nk), (tm, d), a_spec, w_spec, [sd, sd], [rowspec, rowspec], epilogue,
                   extra=(x, gate, g), extra_specs=[rowspec, vec, vec])


def _layer_bwd(tag, dx2, mod, p, wts, sv, reduce_later=None, early=None):
    s, d = dx2.shape
    w_ssm, w_att = p["w_glu"].shape[0], wts["w_pb"].shape[1]
    heads = p["b_f"].shape[0]
    cs = d // N_CHIPS
    fs = wts["w_ffn_down"].shape[1]
    tm, tk, td = _pick(s, 1024), _pick(s, 1024), d
    row = lambda v: v.reshape(1, -1)
    gr = {}

    def dw_slabs(name, act, act_spec, rows, dy, dy_spec, cols, grid_mn, out_index):
        return _mm_raw(name, act, dy, "tn", grid_mn + (s // tk,), (rows, cols), act_spec, dy_spec,
                       [jax.ShapeDtypeStruct((N_CHIPS,) + out_index[1], BF16)],
                       [pl.BlockSpec((None, rows, cols), out_index[0])], _store(BF16))[0]

    dy_ffn, sums = _postnorm_bwd(f"postnorm_bwd_ffn_{tag}", dx2, sv["y_ffn"], row(p["g_post_ffn"]), row(mod[5]))
    d_gate_f, gr["g_post_ffn"] = sums[0], sums[1]
    gr["w_ffn_down"] = dw_slabs(f"dw_down_{tag}", sv["hid4"], pl.BlockSpec((None, tk, fs), lambda i, j, k: (i, k, 0)), fs,
                                dy_ffn, pl.BlockSpec((tk, d), lambda i, j, k: (k, 0)), d, (N_CHIPS, 1),
                                (lambda i, j, k: (i, 0, 0), (fs, d)))

    def swiglu_bwd(acc, extra_refs, out_refs):
        av, bv = extra_refs[0][...].astype(F32), extra_refs[1][...].astype(F32)
        sg = _sigmoid(av)
        out_refs[0][...] = (acc * bv * (sg * (1.0 + av * (1.0 - sg)))).astype(BF16)
        out_refs[1][...] = (acc * (av * sg)).astype(BF16)

    blk4 = pl.BlockSpec((None, tm, fs), lambda i, j, k: (j, i, 0))
    sh4 = jax.ShapeDtypeStruct((N_CHIPS, s, fs), BF16)
    ffn_down_bwd = lambda side: _mm_raw(
        f"ffn_down_bwd_{tag}", dy_ffn, wts["w_ffn_down"], "nt", (s // tm, N_CHIPS, 1), (tm, fs),
        pl.BlockSpec((tm, d), lambda i, j, k: (i, 0)), pl.BlockSpec((None, fs, d), lambda i, j, k: (j, 0, 0)),
        [sh4, sh4], [blk4, blk4], swiglu_bwd, extra=(sv["a4"], sv["b4"]), extra_specs=[blk4, blk4], side=side)
    da4, db4 = reduce_later.swap_and_add(ffn_down_bwd) if reduce_later else ffn_down_bwd(None)
    for n, act4 in (("w_ffn_gate", da4), ("w_ffn_up", db4)):
        gr[n] = dw_slabs(f"d{n}_{tag}", sv["h2"], pl.BlockSpec((tk, td), lambda i, j, k: (k, i)), td,
                         act4, pl.BlockSpec((None, tk, fs), lambda i, j, k: (j, k, 0)), fs, (d // td, N_CHIPS),
                         (lambda i, j, k: (j, i, 0), (d, fs)))
    pairs = [(act4, (None, tm, fs), lambda i, kk: (kk, i, 0), wts[n], (None, td, fs), lambda j, kk: (kk, j, 0),
              N_CHIPS) for n, act4 in (("w_ffn_gate", da4), ("w_ffn_up", db4))]
    own_early = None
    if early is not None:
        own_early = _LayerReduce(f"{tag}e", early[0], early[1], EARLY_REDUCED, [gr[n] for n in EARLY_REDUCED], *early[2:])
    dh_ffn = lambda side: _mm_sum(f"dh_ffn_{tag}", s, d, tm, td, pairs, F32, side=side)
    dh2 = own_early.swap_and_add(dh_ffn) if own_early else dh_ffn(None)
    dx1, sums = _prenorm_bwd(f"prenorm_bwd_ffn_{tag}", dh2, sv["x1"], row(p["g_pre_ffn"]), row(mod[4]), dx2)
    d_scale_f, d_shift_f, gr["g_pre_ffn"] = sums[0], sums[1], sums[2]

    dy_mix, sums = _postnorm_bwd(f"postnorm_bwd_mix_{tag}", dx1, sv["y_mix"], row(p["g_post_mix"]), row(mod[2]))
    d_gate_m, gr["g_post_mix"] = sums[0], sums[1]
    gr["w_o"] = dw_slabs(f"dw_o_{tag}", sv["merged"], pl.BlockSpec((tk, cs), lambda i, j, k: (k, i)), cs,
                         dy_mix, pl.BlockSpec((tk, d), lambda i, j, k: (k, 0)), d, (N_CHIPS, 1),
                         (lambda i, j, k: (i, 0, 0), (cs, d)))

    tile = pl.BlockSpec((tm, cs), lambda i, j, k: (i, j))

    def merge_bwd(acc, extra_refs, out_refs):
        a_ref, b_ref, ga_ref, gb_ref = extra_refs
        sa, sb = _sigmoid(ga_ref[...]), _sigmoid(gb_ref[...])
        out_refs[0][...] = (acc * sa).astype(BF16)
        out_refs[1][...] = (acc * sb).astype(BF16)
        out_refs[2][...] = (acc * a_ref[...].astype(F32) * sa * (1.0 - sa)).astype(BF16)
        out_refs[3][...] = (acc * b_ref[...].astype(F32) * sb * (1.0 - sb)).astype(BF16)

    sd_bf = jax.ShapeDtypeStruct((s, d), BF16)
    d_pa, d_pb, d_ga, d_gb = _mm_raw(
        f"out_proj_bwd_{tag}", dy_mix, wts["w_o"], "nt", (s // tm, N_CHIPS, 1), (tm, cs),
        pl.BlockSpec((tm, d), lambda i, j, k: (i, 0)), pl.BlockSpec((None, cs, d), lambda i, j, k: (j, 0, 0)),
        [sd_bf] * 4, [tile] * 4, merge_bwd, extra=(sv["pa"], sv["pb"], sv["fg"], sv["fg"]),
        extra_specs=[tile, tile, tile, pl.BlockSpec((tm, cs), lambda i, j, k: (i, j + N_CHIPS))])
    branches = (("w_pa", sv["ys"], w_ssm, d_pa), ("w_pb", sv["ya"], w_att, d_pb))
    for n, act, width, d_p in branches:
        gr[n] = dw_slabs(f"d{n}_{tag}", act, pl.BlockSpec((tk, width), lambda i, j, k: (k, 0)), width,
                         d_p, pl.BlockSpec((tk, cs), lambda i, j, k: (k, j)), cs, (1, N_CHIPS),
                         (lambda i, j, k: (j, 0, 0), (width, cs)))
    own_mid = None
    if early is not None:
        own_mid = _LayerReduce(f"{tag}m", early[0], early[1], MID_REDUCED, [gr[n] for n in MID_REDUCED], *early[2:])
    d_branch = {}
    for n, act, width, d_p in branches:
        d_in = lambda side, n=n, width=width, d_p=d_p: _mm_raw(
            f"d_in_{n}_{tag}", d_p, wts[n], "nt", (s // tm, 1, N_CHIPS), (tm, width),
            pl.BlockSpec((tm, cs), lambda i, j, k: (i, k)), pl.BlockSpec((None, width, cs), lambda i, j, k: (k, 0, 0)),
            [jax.ShapeDtypeStruct((s, width), BF16)], [pl.BlockSpec((tm, width), lambda i, j, k: (i, 0))], _store(BF16),
            side=side)
        d_branch[n] = (own_mid.swap_and_add(d_in) if own_mid and n == branches[0][0] else d_in(None))[0]
    d_ys, d_ya = d_branch["w_pa"], d_branch["w_pb"]
    own = [r for r in (own_early, own_mid) if r is not None]

    attn_bwd = lambda side: _attn_bwd(f"attn_bwd_{tag}", sv["uqkv"], *sv["blocks"], w_att // LANES, sv["ya"], d_ya,
                                      sv["lse_rows"], sv["ck_cols"], side=side)
    dq, dk, dv, dcq, dck = reduce_later.exchange_and_sum(attn_bwd, also=own) if reduce_later else attn_bwd(None)[0]
    d_f_t, d_bf = _cum_bwd(f"cum_bwd_{tag}", dcq.reshape(heads, s), dck.reshape(heads, s), sv["f_t"],
                           p["b_f"].reshape(heads, 1))
    gr["b_f"] = d_bf[:, 0]

    t5 = min(S5_ROWS, s)
    du_il, d_bblk, d_cblk, d_abar, d_wglu, vec = _s5_bwd(
        f"s5_bwd_{tag}", sv["u_il"], _interleave_rows(d_ys, t5), sv["y_s5"], sv["carries"], p["b_blk"], p["c_blk"],
        p["a_f"], p["a_r"], p["tab_f"], p["tab_r"], row(p["d_skip"]), p["w_glu"], row(p["b_glu"]))
    du = _deinterleave_rows(du_il, t5)
    gr["w_glu"] = d_wglu.astype(BF16).reshape(N_CHIPS, w_ssm // N_CHIPS, w_ssm)
    gr["b_glu"], gr["d_skip"] = vec[0], vec[1]
    gr["b_blk"], gr["c_blk"], gr["a_bar"] = d_bblk, d_cblk, d_abar

    d_f = jnp.pad(d_f_t.T, ((0, 0), (0, F_PAD - heads))).astype(BF16)
    assert w_ssm % w_att == 0 and (2 * d) % F_PAD == 0
    first = w_ssm // w_att
    main_pieces = [(du, w_ssm, 0), (dq, w_att, first), (dk, w_att, first + 1), (dv, w_att, first + 2)]
    dw = [_mm_plain(f"dw_in{n}_{tag}", sv["h"], piece, "tn", BF16, tm=1024, tn=1024, tk=1024)
          for n, piece in enumerate([du, dq, dk, dv, d_f, d_ga, d_gb])]
    w_in_grad = jnp.concatenate(dw[:4] + [dw[4][:, :heads], dw[5], dw[6]], axis=1)
    gr["w_in"] = w_in_grad.reshape(d, N_CHIPS, w_in_grad.shape[1] // N_CHIPS).transpose(1, 0, 2)
    tmx, tkx = _pick(s, 1024), _pick(d, 512)
    pairs = [(piece, (tmx, width), lambda i, kk: (i, 0), p["w_main"], (d, width), lambda j, kk, blk=blk: (j, blk), 1)
             for piece, width, blk in main_pieces]
    steps = d // tkx
    pairs += [(piece, (tmx, tkx), lambda i, kk: (i, kk), p["w_gates"], (d, tkx), lambda j, kk, off=off: (j, off + kk), steps)
              for piece, off in ((d_ga, 0), (d_gb, steps))]
    pairs.append((d_f, (tmx, F_PAD), lambda i, kk: (i, 0), p["w_gates"], (d, F_PAD), lambda j, kk: (j, 2 * d // F_PAD), 1))
    dh_mix = lambda side: _mm_sum(f"dh_mix_{tag}", s, d, tmx, d, pairs, F32, side=side)
    dh1 = reduce_later.share(dh_mix, also=own) if reduce_later else dh_mix(None)
    dx0, sums = _prenorm_bwd(f"prenorm_bwd_mix_{tag}", dh1, sv["x"], row(p["g_pre_mix"]), row(mod[1]), dx1)
    d_scale_m, d_shift_m, gr["g_pre_mix"] = sums[0], sums[1], sums[2]

    d_mod = jnp.stack([d_shift_m, d_scale_m, d_gate_m, d_shift_f, d_scale_f, d_gate_f])
    return dx0, d_mod, gr


BIG = ("w_in", "w_glu", "w_pa", "w_pb", "w_o", "w_ffn_gate", "w_ffn_up", "w_ffn_down")
FIRST_USED = ("w_in", "w_glu")
EARLY_REDUCED = ("w_ffn_gate", "w_ffn_up", "w_ffn_down")
MID_REDUCED = ("w_pa", "w_pb", "w_o")
SMALL = ("b_ada", "g_pre_mix", "g_post_mix", "g_pre_ffn", "g_post_ffn", "lam_re", "lam_im", "log_dt", "b_re", "b_im",
         "c_re", "c_im", "d_skip", "b_glu", "b_f")
WEIGHTS = ("w_ada", "b_ada", "g_pre_mix", "g_post_mix", "g_pre_ffn", "g_post_ffn", "w_in", "lam_re", "lam_im", "log_dt",
           "b_re", "b_im", "c_re", "c_im", "d_skip", "w_glu", "b_glu", "b_f", "w_pa", "w_pb", "w_o", "w_ffn_gate",
           "w_ffn_up", "w_ffn_down")


def _prepare_layer(wts, small, l, seq):
    w_in = jnp.concatenate([wts["w_in"][j] for j in range(N_CHIPS)], axis=1)
    d = w_in.shape[0]
    heads = small["b_f"].shape[1]
    n_groups, n_state, group_ch = small["b_re"].shape[1:]
    w_ssm = n_groups * group_ch
    w_att = (w_in.shape[1] - w_ssm - heads - 2 * d) // 3
    n_main = w_ssm + 3 * w_att
    gpb = LANES // group_ch
    p = {"w_att": w_att}
    p["w_main"] = w_in[:, :n_main]
    p["w_gates"] = jnp.concatenate(
        [w_in[:, n_main + heads:], w_in[:, n_main:n_main + heads], jnp.zeros((d, F_PAD - heads), BF16)], axis=1)
    p["w_glu"] = wts["w_glu"].reshape(w_ssm, w_ssm)
    for n in ("g_pre_mix", "g_post_mix", "g_pre_ffn", "g_post_ffn", "d_skip", "b_glu", "b_f"):
        p[n] = small[n][l]
    ar, ai, br, bi = _discretize(small["lam_re"][l], small["lam_im"][l], small["log_dt"][l], small["b_re"][l], small["b_im"][l])
    n_steps = min(S5_ROWS, seq) // SUBLANES
    powers = jnp.cumprod(jnp.broadcast_to(lax.complex(ar, ai).reshape(1, -1), (n_steps, ar.size)), axis=0)
    p["a_f"] = jnp.concatenate([jnp.real(powers), jnp.imag(powers)], axis=1)
    p["a_r"] = jnp.concatenate([jnp.real(powers[::-1]), -jnp.imag(powers[::-1])], axis=1)
    p["tab_f"], p["tab_r"] = _scan_tables(jnp.real(powers[-1]), jnp.imag(powers[-1]))
    bre = _block_diag(br.transpose(0, 2, 1), gpb)
    bim = _block_diag(bi.transpose(0, 2, 1), gpb)
    p["b_blk"] = jnp.concatenate([bre, bim], axis=2).astype(BF16)
    cre = _block_diag(small["c_re"][l].transpose(0, 2, 1), gpb)
    cim = _block_diag(small["c_im"][l].transpose(0, 2, 1), gpb)
    p["c_blk"] = jnp.concatenate([cre, -cim], axis=1).astype(BF16)
    return p


def _compact_partials(gr, n_state, group_ch):
    gpb = LANES // group_ch
    half = gpb * n_state
    out = dict(gr)
    out["bbar_re"] = _block_diag_extract(gr["b_blk"][:, :, :half], gpb, group_ch, n_state).transpose(0, 2, 1)
    out["bbar_im"] = _block_diag_extract(gr["b_blk"][:, :, half:], gpb, group_ch, n_state).transpose(0, 2, 1)
    out["c_re"] = _block_diag_extract(gr["c_blk"][:, :half, :], gpb, n_state, group_ch).transpose(0, 2, 1)
    out["c_im"] = -_block_diag_extract(gr["c_blk"][:, half:, :], gpb, n_state, group_ch).transpose(0, 2, 1)
    return out


def _small_grads_from_partials(gr, small, l):
    n_groups, n_state, _ = small["b_re"].shape[1:]
    ns2 = n_groups * n_state
    d_abar = jnp.sum(gr["a_bar"], axis=0)
    dar, dai = d_abar[:ns2].reshape(n_groups, n_state), d_abar[ns2:].reshape(n_groups, n_state)
    args = (small["lam_re"][l], small["lam_im"][l], small["log_dt"][l], small["b_re"][l], small["b_im"][l])
    _, vjp = jax.vjp(_discretize, *args)
    d_lam_re, d_lam_im, d_log_dt, d_b_re, d_b_im = vjp((dar, dai, gr["bbar_re"], gr["bbar_im"]))
    return dict(lam_re=d_lam_re, lam_im=d_lam_im, log_dt=d_log_dt, b_re=d_b_re, b_im=d_b_im,
                c_re=gr["c_re"], c_im=gr["c_im"])


def _fwd_bwd(xs, target, mods, small, wts0, later, core=None, late0=None):
    depth = 1 + len(later)
    saved, layers, wts = [], [], [wts0]
    act = xs
    for l in range(depth):
        layers.append(_prepare_layer(wts[l], small, l, xs.shape[0]))
        shards = later[l] if l + 1 < depth and not isinstance(later[l], dict) else None
        late = late0 if l == 0 and late0 else ((), [])
        carried = (late[0], late[1], shards or []) if (late[0] or shards) else None
        act, sv, wts[l], gathered = _layer_fwd(str(l), act, mods[l], layers[l], wts[l], carried=carried)
        saved.append(sv)
        if l + 1 < depth:
            wts.append(dict(zip(BIG, _put_own_slabs(gathered, shards))) if shards is not None else later[l])
    dx, loss_blk = _loss_grad("loss", act, target)
    grads, d_mods = [None] * depth, [None] * depth
    pending, bufs = None, {}
    for l in reversed(range(depth)):
        early = (l, depth, core, bufs) if pending is not None else None
        dx, d_mods[l], grads[l] = _layer_bwd(str(l), dx, mods[l], layers[l], wts[l], saved[l], reduce_later=pending,
                                             early=early)
        if core is not None:
            names = [n for n in BIG if early is None or n not in EARLY_REDUCED + MID_REDUCED]
            pending = _LayerReduce(str(l), l, depth, names, [grads[l][n] for n in names], core, bufs)
    if core is None:
        return loss_blk, dx, d_mods, grads, None
    pending.swap_and_add()
    pending.exchange_and_sum()
    pending.share()
    return loss_blk, dx, d_mods, grads, bufs


def kernel(x, c, w_ada, b_ada, g_pre_mix, g_post_mix, g_pre_ffn, g_post_ffn, w_in, lam_re, lam_im, log_dt, b_re, b_im, c_re, c_im, d_skip, w_glu, b_glu, b_f, w_pa, w_pb, w_o, w_ffn_gate, w_ffn_up, w_ffn_down, loss_target, m_w_ada, m_b_ada, m_g_pre_mix, m_g_post_mix, m_g_pre_ffn, m_g_post_ffn, m_w_in, m_lam_re, m_lam_im, m_log_dt, m_b_re, m_b_im, m_c_re, m_c_im, m_d_skip, m_w_glu, m_b_glu, m_b_f, m_w_pa, m_w_pb, m_w_o, m_w_ffn_gate, m_w_ffn_up, m_w_ffn_down, v_w_ada, v_b_ada, v_g_pre_mix, v_g_post_mix, v_g_pre_ffn, v_g_post_ffn, v_w_in, v_lam_re, v_lam_im, v_log_dt, v_b_re, v_b_im, v_c_re, v_c_im, v_d_skip, v_w_glu, v_b_glu, v_b_f, v_w_pa, v_w_pb, v_w_o, v_w_ffn_gate, v_w_ffn_up, v_w_ffn_down):
    local = dict(locals())
    weights = {n: local[n] for n in WEIGHTS}
    moments_m = {n: local["m_" + n] for n in WEIGHTS}
    moments_v = {n: local["v_" + n] for n in WEIGHTS}
    depth, d = g_pre_mix.shape
    n_mod = w_ada.shape[2] * N_CHIPS // d
    mx, my, mc = lax.axis_index("x"), lax.axis_index("y"), lax.axis_index("c")
    my_chip = 2 * mx + my
    my_dev = 4 * mx + 2 * my + mc
    xs = x[0]

    shards = [[weights[n][l].astype(BF16) for n in BIG] for l in range(depth)]
    early = [i for i, n in enumerate(BIG) if n in FIRST_USED]
    late = [i for i, n in enumerate(BIG) if n not in FIRST_USED]
    wts0 = dict(zip([BIG[i] for i in early], _gather_layer("gather_weights_0", [shards[0][i] for i in early])))
    late0 = ([BIG[i] for i in late], [shards[0][i] for i in late])
    small = {n: weights[n] for n in SMALL}

    c_pad = jnp.pad(c, ((0, SUBLANES - 1), (0, 0)))
    c_all = _all_gather("gather_cond", c_pad).reshape(N_DEV, SUBLANES, d)[:, 0, :]
    silu = lambda v: v * _sigmoid(v)
    n_cols = w_ada.shape[2]
    mod_shard = []
    for l in range(depth):
        bias = lax.dynamic_slice_in_dim(b_ada[l], my_chip * n_cols, n_cols)
        mod_shard.append(_mm_plain(f"ada_{l}", c_all, w_ada[l], "nn", F32, add=jnp.broadcast_to(bias, (N_DEV, n_cols)),
                                   a_fn=silu, tm=N_DEV, tn=512, tk=1024))
    mod_block = jnp.concatenate(mod_shard, axis=1)
    mod_all = _all_gather("gather_mod", mod_block).reshape(N_DEV, N_DEV, depth, n_cols)
    mod_rows = lax.dynamic_index_in_dim(mod_all[0::2], my_dev, axis=1, keepdims=False)
    mods = [mod_rows[:, l, :].reshape(n_mod, d) for l in range(depth)]

    loss_blk, dx, d_mods, grads, big_grads = _fwd_bwd(xs, loss_target[0], mods, small, wts0, shards[1:],
                                                      core=mc.astype(jnp.int32).reshape(1), late0=late0)
    loss = lax.psum(loss_blk[0, 0], ("x", "y", "c"))
    grad_x = dx[None]

    partial_names = ("g_pre_mix", "g_post_mix", "g_pre_ffn", "g_post_ffn", "d_skip", "b_glu", "b_f", "a_bar",
                     "bbar_re", "bbar_im", "c_re", "c_im")
    n_state, group_ch = b_re.shape[2:]
    contrib = list(d_mods)
    for l in range(depth):
        compact = _compact_partials(grads[l], n_state, group_ch)
        contrib += [compact[n] for n in partial_names]
    contrib_shapes = [a.shape for a in contrib]
    block = _pack(contrib, LANES, BF16_ROWS, F32)
    rows = block.shape[0]
    all_blocks = _all_gather("gather_small_grads", block).reshape(N_DEV, rows, LANES)
    summed = _unpack(_sum_blocks("sum_small_grads", all_blocks, F32), contrib_shapes)
    per_layer = len(partial_names)
    small_grads = {n: [] for n in SMALL}
    d_mod_all = []
    for l in range(depth):
        small_grads["b_ada"].append(summed[l].reshape(-1))
        gl = dict(zip(partial_names, summed[depth + l * per_layer:depth + (l + 1) * per_layer]))
        for n in ("g_pre_mix", "g_post_mix", "g_pre_ffn", "g_post_ffn", "d_skip", "b_glu", "b_f"):
            small_grads[n].append(gl[n])
        for n, gval in _small_grads_from_partials(gl, small, l).items():
            small_grads[n].append(gval)
        mod_rows_ = n_mod * d // LANES
        d_mod_all.append(all_blocks[:, l * mod_rows_:(l + 1) * mod_rows_, :].reshape(N_DEV, n_mod * d))
    small_grads = {n: jnp.stack(v) for n, v in small_grads.items()}

    g_w_ada = []
    for l in range(depth):
        cols = lax.dynamic_slice_in_dim(d_mod_all[l], my_chip * n_cols, n_cols, axis=1)
        g_w_ada.append(_mm_plain(f"dw_ada_{l}", c_all, cols, "tn", F32, a_fn=silu, tm=512, tn=512, tk=N_DEV))
    all_grads = dict(big_grads)
    all_grads.update(small_grads)
    all_grads["w_ada"] = jnp.stack(g_w_ada)

    delta, new_m, new_v = {}, {}, {}
    for n in ("w_ada",) + BIG:
        last = weights[n].shape[2]
        to_stored, from_stored = ((0, 1, 2),) * 2 if last % LANES == 0 else ((0, 2, 1),) * 2 if last % SUBLANES == 0 \
            else ((2, 0, 1), (1, 2, 0))
        view, back = (lambda a: a.transpose(to_stored)), (lambda a: a.transpose(from_stored))
        outs = _adamw(f"adamw_{n}", view(weights[n]), view(all_grads[n]), view(moments_m[n]), view(moments_v[n]))
        delta[n], new_m[n], new_v[n] = (back(o) for o in outs)
    small_shapes = [weights[n].shape for n in SMALL]
    packed = [_pack([src[n] for n in SMALL], LANES, SUBLANES, F32)[None] for src in (weights, all_grads, moments_m, moments_v)]
    outs = _adamw("adamw_small", *packed)
    for dst, buf in zip((delta, new_m, new_v), outs):
        dst.update(dict(zip(SMALL, _unpack(buf[0], small_shapes))))

    return (loss, grad_x, *[all_grads[n] for n in WEIGHTS], *[delta[n] for n in WEIGHTS],
            *[new_m[n] for n in WEIGHTS], *[new_v[n] for n in WEIGHTS])
```

```python
import math

import jax
import jax.numpy as jnp
from jax import lax
from jax.experimental import pallas as pl
from jax.experimental.pallas import tpu as pltpu

F32 = jnp.float32
BF16 = jnp.bfloat16
MESH = pl.DeviceIdType.MESH

RMS_EPS = 1e-6
EIG_CLIP = 1e-4
ADAM_LR, ADAM_B1, ADAM_B2, ADAM_EPS, ADAM_WD, ADAM_STEP = 0.001, 0.9, 0.999, 1e-08, 0.01, 10

LANES = 128
SUBLANES = 8
VMEM_LIMIT = 56 * 1024 * 1024
ROW_TILE_BYTES = 1 << 20
SUM_TILE_BYTES = 1 << 19
S5_ROWS = 256
S5_CHUNK = 1024
S5_UNROLL = 4
ATT_BLOCK = 512
F_PAD = 256
POSTNORM_ROWS = 1024
N_CHIPS = 4
N_DEV = 8

NN = (((1,), (0,)), ((), ()))
NT = (((1,), (1,)), ((), ()))
TN = (((0,), (0,)), ((), ()))
_DN = {"nn": NN, "nt": NT, "tn": TN}


def _cparams(**kw):
    return pltpu.CompilerParams(vmem_limit_bytes=VMEM_LIMIT, **kw)


def _pick(dim, target):
    best, t = None, LANES
    while t <= min(dim, target):
        if dim % t == 0:
            best = t
        t += LANES
    return best or dim


def _sigmoid(x):
    return 1.0 / (1.0 + jnp.exp(-x))


def _dot(a, b, dn):
    return lax.dot_general(a, b, dn, preferred_element_type=F32)


def _mm_raw(name, a, b, mode, grid, acc_shape, a_spec, b_spec, out_shapes, out_specs, epilogue,
            extra=(), extra_specs=(), a_fn=None, side=None):
    nk = grid[2]
    n_extra, n_out = len(extra), len(out_shapes)

    def body(*refs):
        a_ref, b_ref = refs[0], refs[1]
        extra_refs = refs[2:2 + n_extra]
        out_refs = refs[2 + n_extra:2 + n_extra + n_out]
        acc = refs[-1]
        k = pl.program_id(2)

        @pl.when(k == 0)
        def _():
            acc[...] = jnp.zeros_like(acc)

        av = a_ref[...]
        if a_fn is not None:
            av = a_fn(av.astype(F32))
        acc[...] += _dot(av.astype(BF16), b_ref[...].astype(BF16), _DN[mode])

        @pl.when(k == nk - 1)
        def _():
            epilogue(acc[...], extra_refs, out_refs)

    outs, side_outs = _hosted_call(body, side, name, grid, [a_spec, b_spec, *extra_specs], list(out_specs),
                                   list(out_shapes), [pltpu.VMEM(acc_shape, F32)], (a, b, *extra))
    return outs if side is None else (outs, side_outs)


def _mm(name, a, b, mode, out_shapes, out_specs, epilogue, extra=(), extra_specs=(),
        tm=512, tn=512, tk=512, a_fn=None):
    if mode == "nn":
        (m, kd), (_, n) = a.shape, b.shape
    elif mode == "nt":
        (m, kd), (n, _) = a.shape, b.shape
    else:
        (kd, m), (_, n) = a.shape, b.shape
    tm, tn, tk = _pick(m, tm), _pick(n, tn), _pick(kd, tk)
    if mode == "tn":
        a_spec = pl.BlockSpec((tk, tm), lambda i, j, k: (k, i))
    else:
        a_spec = pl.BlockSpec((tm, tk), lambda i, j, k: (i, k))
    if mode == "nt":
        b_spec = pl.BlockSpec((tn, tk), lambda i, j, k: (j, k))
    else:
        b_spec = pl.BlockSpec((tk, tn), lambda i, j, k: (k, j))
    res = _mm_raw(name, a, b, mode, (m // tm, n // tn, kd // tk), (tm, tn), a_spec, b_spec, out_shapes, out_specs,
                  epilogue, extra=extra, extra_specs=extra_specs, a_fn=a_fn)
    return res, (tm, tn, tk)


def _store(dtype):
    def epilogue(acc, extra_refs, out_refs):
        out_refs[0][...] = acc.astype(dtype)
    return epilogue


def _mm_sum(name, m, n, tm, tn, pairs, out_dtype, side=None):
    offs, total = [], 0
    for pr in pairs:
        offs.append(total)
        total += pr[6]
    n_p = len(pairs)

    def body(*refs):
        o_ref, acc = refs[2 * n_p], refs[2 * n_p + 1]
        k = pl.program_id(2)

        @pl.when(k == 0)
        def _():
            acc[...] = jnp.zeros_like(acc)

        for p_ in range(n_p):
            @pl.when((k >= offs[p_]) & (k < offs[p_] + pairs[p_][6]))
            def _(p_=p_):
                acc[...] += _dot(refs[2 * p_][...].astype(BF16), refs[2 * p_ + 1][...].astype(BF16), NT)

        @pl.when(k == total - 1)
        def _():
            o_ref[...] = acc[...].astype(out_dtype)

    in_specs, operands = [], []
    for (a, a_block, a_index, b, b_block, b_index, steps), off in zip(pairs, offs):
        local = lambda k, off=off, steps=steps: jnp.clip(k - off, 0, steps - 1)
        in_specs.append(pl.BlockSpec(a_block, lambda i, j, k, f=a_index, local=local: f(i, local(k))))
        in_specs.append(pl.BlockSpec(b_block, lambda i, j, k, f=b_index, local=local: f(j, local(k))))
        operands += [a, b]
    (out,), side_outs = _hosted_call(
        body, side, name, (m // tm, n // tn, total), in_specs, [pl.BlockSpec((tm, tn), lambda i, j, k: (i, j))],
        [jax.ShapeDtypeStruct((m, n), out_dtype)], [pltpu.VMEM((tm, tn), F32)], operands)
    return out if side is None else (out, side_outs)


class _SideJob:
    def __init__(self, arrays, out_shapes, aliases, n_sems, copies):
        self.arrays, self.out_shapes, self.aliases, self.n_sems, self.copies = arrays, out_shapes, aliases, n_sems, copies


def _hosted_call(body, side, name, grid, in_specs, out_specs, out_shape, scratch_shapes, operands):
    if side is None:
        outs = pl.pallas_call(body, name=name, grid=grid, in_specs=in_specs, out_specs=out_specs, out_shape=out_shape,
                              scratch_shapes=scratch_shapes, compiler_params=_cparams())(*operands)
        return outs, []
    n_in, n_out, ns_in, ns_out = len(in_specs), len(out_specs), len(side.arrays), len(side.out_shapes)

    def wrapped(*refs):
        main_in, side_in = refs[:n_in], refs[n_in:n_in + ns_in]
        rest = refs[n_in + ns_in:]
        main_out, side_out, rest = rest[:n_out], rest[n_out:n_out + ns_out], rest[n_out + ns_out:]
        scratch, send_sems, recv_sems = rest[:-2], rest[-2], rest[-1]
        first, last = None, None
        for axis, extent in enumerate(grid):
            at_start, at_end = pl.program_id(axis) == 0, pl.program_id(axis) == extent - 1
            first = at_start if first is None else first & at_start
            last = at_end if last is None else last & at_end

        @pl.when(first)
        def _():
            for cp in side.copies(side_in, side_out, send_sems, recv_sems):
                cp.start()

        body(*main_in, *main_out, *scratch)

        @pl.when(last)
        def _():
            for cp in side.copies(side_in, side_out, send_sems, recv_sems):
                cp.wait()

    hbm = pl.BlockSpec(memory_space=pl.ANY)
    outs = pl.pallas_call(
        wrapped, name=name, grid=grid, in_specs=list(in_specs) + [hbm] * ns_in,
        out_specs=list(out_specs) + [hbm] * ns_out, out_shape=list(out_shape) + list(side.out_shapes),
        scratch_shapes=list(scratch_shapes) + [pltpu.SemaphoreType.DMA((side.n_sems,))] * 2,
        input_output_aliases={n_in + i: n_out + o for i, o in side.aliases.items()},
        compiler_params=_cparams(),
    )(*operands, *side.arrays)
    return outs[:n_out], outs[n_out:]


def _ffn_up(name, h, wg, wu, side=None):
    s, d = h.shape
    nc, fs = wg.shape[0], wg.shape[2]
    tm, tk = _pick(s, 1024), _pick(d, 1024)
    nk = d // tk

    def body(h_ref, wg_ref, wu_ref, a_ref, b_ref, hid_ref, acc_g, acc_u):
        k = pl.program_id(2)

        @pl.when(k == 0)
        def _():
            acc_g[...] = jnp.zeros_like(acc_g)
            acc_u[...] = jnp.zeros_like(acc_u)

        hv = h_ref[...]
        acc_g[...] += _dot(hv, wg_ref[...], NN)
        acc_u[...] += _dot(hv, wu_ref[...], NN)

        @pl.when(k == nk - 1)
        def _():
            av, bv = acc_g[...], acc_u[...]
            a_ref[...] = av.astype(BF16)
            b_ref[...] = bv.astype(BF16)
            hid_ref[...] = (av * _sigmoid(av) * bv).astype(BF16)

    w_spec = pl.BlockSpec((None, tk, fs), lambda i, j, k: (j, k, 0))
    o_spec = pl.BlockSpec((None, tm, fs), lambda i, j, k: (j, i, 0))
    sh = jax.ShapeDtypeStruct((nc, s, fs), BF16)
    return _hosted_call(
        body, side, name, (s // tm, nc, nk), [pl.BlockSpec((tm, tk), lambda i, j, k: (i, k)), w_spec, w_spec],
        [o_spec] * 3, [sh] * 3, [pltpu.VMEM((tm, fs), F32), pltpu.VMEM((tm, fs), F32)], (h, wg, wu))


def _mm_plain(name, a, b, mode, out_dtype, add=None, a_fn=None, tm=512, tn=512, tk=512):
    if mode == "nn":
        m, n = a.shape[0], b.shape[1]
    elif mode == "nt":
        m, n = a.shape[0], b.shape[0]
    else:
        m, n = a.shape[1], b.shape[1]
    tm_, tn_ = _pick(m, tm), _pick(n, tn)
    spec = pl.BlockSpec((tm_, tn_), lambda i, j, k: (i, j))

    def epilogue(acc, extra_refs, out_refs):
        if add is not None:
            acc = acc + extra_refs[0][...]
        out_refs[0][...] = acc.astype(out_dtype)

    extra = () if add is None else (add,)
    (out,), _ = _mm(name, a, b, mode, [jax.ShapeDtypeStruct((m, n), out_dtype)], [spec], epilogue,
                    extra=extra, extra_specs=[spec] * len(extra), tm=tm, tn=tn, tk=tk, a_fn=a_fn)
    return out


def _row_tile(s, d):
    return _pick(s, max(SUBLANES, ROW_TILE_BYTES // (4 * d)))


def _prenorm_fwd(name, x, g, scale, shift):
    s, d = x.shape
    tr = _row_tile(s, d)

    def body(x_ref, g_ref, sc_ref, sh_ref, h_ref):
        xv = x_ref[...]
        r = lax.rsqrt(jnp.mean(xv * xv, axis=-1, keepdims=True) + RMS_EPS)
        h_ref[...] = ((xv * r * g_ref[...]) * (1.0 + sc_ref[...]) + sh_ref[...]).astype(BF16)

    row = pl.BlockSpec((tr, d), lambda i: (i, 0))
    vec = pl.BlockSpec((1, d), lambda i: (0, 0))
    return pl.pallas_call(body, name=name, grid=(s // tr,), in_specs=[row, vec, vec, vec], out_specs=row,
                          out_shape=jax.ShapeDtypeStruct((s, d), BF16), compiler_params=_cparams())(x, g, scale, shift)


def _prenorm_bwd(name, dh, x, g, scale, dx_res):
    s, d = x.shape
    tr = _row_tile(s, d)

    def body(dh_ref, x_ref, g_ref, sc_ref, dxr_ref, dx_ref, sums_ref):
        @pl.when(pl.program_id(0) == 0)
        def _():
            sums_ref[...] = jnp.zeros_like(sums_ref)

        xv, dhv, gv = x_ref[...], dh_ref[...].astype(F32), g_ref[...]
        r = lax.rsqrt(jnp.mean(xv * xv, axis=-1, keepdims=True) + RMS_EPS)
        xhat = xv * r
        dxn = dhv * (1.0 + sc_ref[...])
        dxhat = dxn * gv
        dx = r * (dxhat - xhat * jnp.mean(dxhat * xhat, axis=-1, keepdims=True))
        dx_ref[...] = dxr_ref[...] + dx
        sums_ref[0:1, :] += jnp.sum(dhv * (xhat * gv), axis=0, keepdims=True)
        sums_ref[1:2, :] += jnp.sum(dhv, axis=0, keepdims=True)
        sums_ref[2:3, :] += jnp.sum(dxn * xhat, axis=0, keepdims=True)

    row = pl.BlockSpec((tr, d), lambda i: (i, 0))
    vec = pl.BlockSpec((1, d), lambda i: (0, 0))
    acc = pl.BlockSpec((SUBLANES, d), lambda i: (0, 0))
    return pl.pallas_call(
        body, name=name, grid=(s // tr,), in_specs=[row, row, vec, vec, row], out_specs=[row, acc],
        out_shape=[jax.ShapeDtypeStruct((s, d), F32), jax.ShapeDtypeStruct((SUBLANES, d), F32)],
        compiler_params=_cparams())(dh, x, g, scale, dx_res)


def _postnorm_bwd(name, dxn, y, g, gate):
    s, d = y.shape
    tr = _row_tile(s, d)

    def body(dx_ref, y_ref, g_ref, gt_ref, dy_ref, sums_ref):
        @pl.when(pl.program_id(0) == 0)
        def _():
            sums_ref[...] = jnp.zeros_like(sums_ref)

        yv, dxv, gv = y_ref[...], dx_ref[...], g_ref[...]
        r = lax.rsqrt(jnp.mean(yv * yv, axis=-1, keepdims=True) + RMS_EPS)
        yhat = yv * r
        dn = dxv * gt_ref[...]
        dyhat = dn * gv
        dy_ref[...] = (r * (dyhat - yhat * jnp.mean(dyhat * yhat, axis=-1, keepdims=True))).astype(BF16)
        sums_ref[0:1, :] += jnp.sum(dxv * (yhat * gv), axis=0, keepdims=True)
        sums_ref[1:2, :] += jnp.sum(dn * yhat, axis=0, keepdims=True)

    row = pl.BlockSpec((tr, d), lambda i: (i, 0))
    vec = pl.BlockSpec((1, d), lambda i: (0, 0))
    acc = pl.BlockSpec((SUBLANES, d), lambda i: (0, 0))
    return pl.pallas_call(
        body, name=name, grid=(s // tr,), in_specs=[row, row, vec, vec], out_specs=[row, acc],
        out_shape=[jax.ShapeDtypeStruct((s, d), BF16), jax.ShapeDtypeStruct((SUBLANES, d), F32)],
        compiler_params=_cparams())(dxn, y, g, gate)


def _loss_grad(name, y, target):
    s, d = y.shape
    tr = _row_tile(s, d)

    def body(y_ref, t_ref, dy_ref, loss_ref):
        @pl.when(pl.program_id(0) == 0)
        def _():
            loss_ref[...] = jnp.zeros_like(loss_ref)

        err = y_ref[...] - t_ref[...]
        dy_ref[...] = err * (1.0 / d)
        part = jnp.sum(jnp.sum(err * err, axis=-1, keepdims=True), axis=0, keepdims=True) * (0.5 / d)
        loss_ref[...] += jnp.broadcast_to(part, loss_ref.shape)

    row = pl.BlockSpec((tr, d), lambda i: (i, 0))
    acc = pl.BlockSpec((SUBLANES, LANES), lambda i: (0, 0))
    return pl.pallas_call(
        body, name=name, grid=(s // tr,), in_specs=[row, row], out_specs=[row, acc],
        out_shape=[jax.ShapeDtypeStruct((s, d), F32), jax.ShapeDtypeStruct((SUBLANES, LANES), F32)],
        compiler_params=_cparams())(y, target)


def _gelu(y):
    c = math.sqrt(2.0 / math.pi)
    return 0.5 * y * (1.0 + jnp.tanh(c * (y + 0.044715 * (y * y * y))))


def _gelu_grad(y):
    c = math.sqrt(2.0 / math.pi)
    th = jnp.tanh(c * (y + 0.044715 * (y * y * y)))
    return 0.5 * (1.0 + th) + 0.5 * y * (1.0 - th * th) * c * (1.0 + 3.0 * 0.044715 * (y * y))


def _cmul_add(br, bi, ar, ai, xr, xi):
    return br + ar * xr - ai * xi, bi + ar * xi + ai * xr


def _scan_rows(x_ref, row0, n_steps, ns2, pow_ref, tab_ref, carry_ref, reverse, fold=None):
    assert n_steps % SUBLANES == 0
    wc = min(S5_CHUNK, ns2)
    sub = lax.broadcasted_iota(jnp.int32, (SUBLANES, wc), 0)
    unroll = S5_UNROLL if n_steps % S5_UNROLL == 0 else 1
    for c0 in range(0, ns2, wc):
        re = slice(c0, c0 + wc)
        im = slice(ns2 + c0, ns2 + c0 + wc)
        first_power = slice(n_steps - 1, n_steps) if reverse else slice(0, 1)
        ar = jnp.broadcast_to(pow_ref[first_power, re], (SUBLANES, wc))
        ai = jnp.broadcast_to(pow_ref[first_power, im], (SUBLANES, wc))
        rows = lambda r: pl.ds(pl.multiple_of(row0 + r * SUBLANES, SUBLANES), SUBLANES)
        step_of = lambda i: (n_steps - 1 - i) if reverse else i

        def local(i, carry, re=re, im=im, ar=ar, ai=ai):
            for u in range(unroll):
                r = step_of(i * unroll + u)
                carry = _cmul_add(x_ref[rows(r), re], x_ref[rows(r), im], ar, ai, *carry)
                x_ref[rows(r), re], x_ref[rows(r), im] = carry
            return carry

        zero = jnp.zeros((SUBLANES, wc), F32)
        lr, li = lax.fori_loop(0, n_steps // unroll, local, (zero, zero))

        tabs = [tab_ref[k, :, re] for k in range(8)]
        for lvl, k in enumerate((1, 2, 4)):
            sh = (SUBLANES - k) if reverse else k
            lr, li = _cmul_add(lr, li, tabs[2 * lvl], tabs[2 * lvl + 1], pltpu.roll(lr, sh, 0), pltpu.roll(li, sh, 0))
        cr, ci = carry_ref[0:1, re], carry_ref[0:1, im]
        lr, li = _cmul_add(lr, li, tabs[6], tabs[7], cr, ci)
        edge, away, last = (SUBLANES - 1, SUBLANES - 1, 0) if reverse else (0, 1, SUBLANES - 1)
        carry_ref[0:1, re] = lr[last:last + 1, :]
        carry_ref[0:1, im] = li[last:last + 1, :]
        er = jnp.where(sub == edge, cr, pltpu.roll(lr, away, 0))
        ei = jnp.where(sub == edge, ci, pltpu.roll(li, away, 0))

        def fix(j, acc, re=re, im=im, er=er, ei=ei, c0=c0):
            base = pl.ds(pl.multiple_of(j * SUBLANES, SUBLANES), SUBLANES)
            pw_r, pw_i = pow_ref[base, re], pow_ref[base, im]
            for i in range(SUBLANES):
                r = j * SUBLANES + i
                xr, xi = _cmul_add(x_ref[rows(r), re], x_ref[rows(r), im], pw_r[i:i + 1, :], pw_i[i:i + 1, :], er, ei)
                x_ref[rows(r), re], x_ref[rows(r), im] = xr, xi
                if fold is not None:
                    acc = fold(c0, r, xr, xi, acc)
            return acc

        acc = lax.fori_loop(0, n_steps // SUBLANES, fix, (zero, zero) if fold is not None else 0)
        if fold is not None:
            fold(c0, None, None, None, acc)


def _s5_fwd(name, u, b_blk, c_blk, a_f, tab_f, dskip, w_glu, b_glu):
    s, w = u.shape[0], w_glu.shape[0]
    nkb = w // LANES
    ns2 = b_blk.shape[2] // 2 * nkb
    half = ns2 // nkb
    t = min(S5_ROWS, s)
    nblk = s // t

    def body(u_ref, b_ref, c_ref, a_ref, tab_ref, ds_ref, wg_ref, bg_ref, y_ref, ys_ref, cs_ref, xs, carry):
        @pl.when(pl.program_id(0) == 0)
        def _():
            carry[...] = jnp.zeros_like(carry)

        cs_ref[0] = carry[...]
        for kb in range(nkb):
            bu = _dot(u_ref[:, kb * LANES:(kb + 1) * LANES], b_ref[kb], NN)
            xs[:, kb * half:(kb + 1) * half] = bu[:, :half]
            xs[:, ns2 + kb * half:ns2 + (kb + 1) * half] = bu[:, half:]
        _scan_rows(xs, 0, t // SUBLANES, ns2, a_ref, tab_ref, carry, reverse=False)
        for kb in range(nkb):
            cols = slice(kb * LANES, (kb + 1) * LANES)
            yk = _dot(xs[:, kb * half:(kb + 1) * half].astype(BF16), c_ref[kb, :half, :], NN)
            yk += _dot(xs[:, ns2 + kb * half:ns2 + (kb + 1) * half].astype(BF16), c_ref[kb, half:, :], NN)
            y_ref[:, cols] = yk + ds_ref[:, cols] * u_ref[:, cols].astype(F32)
        z = _gelu(y_ref[...])
        gate = _sigmoid(_dot(z.astype(BF16), wg_ref[...], NN) + bg_ref[...])
        ys_ref[...] = (z * gate).astype(BF16)

    row = pl.BlockSpec((t, w), lambda i: (i, 0))
    full = lambda shape: pl.BlockSpec(shape, lambda i: (0,) * len(shape))
    return pl.pallas_call(
        body, name=name, grid=(nblk,),
        in_specs=[row, full(b_blk.shape), full(c_blk.shape), full(a_f.shape), full(tab_f.shape), full(dskip.shape),
                  full(w_glu.shape), full(b_glu.shape)],
        out_specs=[row, row, pl.BlockSpec((1, 1, 2 * ns2), lambda i: (i, 0, 0))],
        out_shape=[jax.ShapeDtypeStruct((s, w), F32), jax.ShapeDtypeStruct((s, w), BF16),
                   jax.ShapeDtypeStruct((nblk, 1, 2 * ns2), F32)],
        scratch_shapes=[pltpu.VMEM((t, 2 * ns2), F32), pltpu.VMEM((1, 2 * ns2), F32)],
        compiler_params=_cparams(),
    )(u, b_blk, c_blk, a_f, tab_f, dskip, w_glu, b_glu)


def _s5_bwd(name, u, dys, y, carries, b_blk, c_blk, a_f, a_r, tab_f, tab_r, dskip, w_glu, b_glu, side=None):
    s, w = u.shape[0], w_glu.shape[0]
    nkb = w // LANES
    ns2 = b_blk.shape[2] // 2 * nkb
    half = ns2 // nkb
    t = min(S5_ROWS, s)
    nblk = s // t
    ng = t // SUBLANES

    def body(u_ref, dys_ref, y_ref, cs_ref, b_ref, c_ref, af_ref, ar_ref, tabf_ref, tabr_ref, ds_ref, wg_ref, bg_ref,
             du_ref, db_ref, dc_ref, da_ref, dwg_ref, vec_ref, xs, gs, dyv, fcarry, gcarry):
        @pl.when(pl.program_id(0) == 0)
        def _():
            db_ref[...] = jnp.zeros_like(db_ref)
            dc_ref[...] = jnp.zeros_like(dc_ref)
            da_ref[...] = jnp.zeros_like(da_ref)
            dwg_ref[...] = jnp.zeros_like(dwg_ref)
            vec_ref[...] = jnp.zeros_like(vec_ref)
            gcarry[...] = jnp.zeros_like(gcarry)

        yv = y_ref[...]
        z = _gelu(yv)
        zb = z.astype(BF16)
        gate = _sigmoid(_dot(zb, wg_ref[...], NN) + bg_ref[...])
        dout = dys_ref[...].astype(F32)
        dt = dout * z * gate * (1.0 - gate)
        dtb = dt.astype(BF16)
        dz = dout * gate + _dot(dtb, wg_ref[...], NT)
        dy = dz * _gelu_grad(yv)
        dyv[...] = dy
        dwg_ref[...] += _dot(zb, dtb, TN)
        vec_ref[0:1, :] += jnp.sum(dt, axis=0, keepdims=True)
        vec_ref[1:2, :] += jnp.sum(dy * u_ref[...].astype(F32), axis=0, keepdims=True)

        fcarry[...] = cs_ref[0]
        xs[0:SUBLANES, :] = jnp.broadcast_to(cs_ref[0], (SUBLANES, 2 * ns2))
        for kb in range(nkb):
            bu = _dot(u_ref[:, kb * LANES:(kb + 1) * LANES], b_ref[kb], NN)
            xs[SUBLANES:, kb * half:(kb + 1) * half] = bu[:, :half]
            xs[SUBLANES:, ns2 + kb * half:ns2 + (kb + 1) * half] = bu[:, half:]
        _scan_rows(xs, SUBLANES, ng, ns2, af_ref, tabf_ref, fcarry, reverse=False)
        first_segment = lax.broadcasted_iota(jnp.int32, (SUBLANES, 2 * ns2), 0) == 0
        xs[0:SUBLANES, :] = jnp.where(first_segment, xs[0:SUBLANES, :], pltpu.roll(xs[t:t + SUBLANES, :], 1, 0))

        for kb in range(nkb):
            dyk = dyv[:, kb * LANES:(kb + 1) * LANES].astype(BF16)
            re = slice(kb * half, (kb + 1) * half)
            im = slice(ns2 + kb * half, ns2 + (kb + 1) * half)
            gs[:, re] = _dot(dyk, c_ref[kb, :half, :], NT)
            gs[:, im] = _dot(dyk, c_ref[kb, half:, :], NT)
            dc_ref[kb, :half, :] += _dot(xs[SUBLANES:, re].astype(BF16), dyk, TN)
            dc_ref[kb, half:, :] += _dot(xs[SUBLANES:, im].astype(BF16), dyk, TN)

        def fold(c0, r, gr, gi, acc):
            wc = min(S5_CHUNK, ns2)
            re = slice(c0, c0 + wc)
            im = slice(ns2 + c0, ns2 + c0 + wc)
            if r is None:
                da_ref[:, re] += acc[0]
                da_ref[:, im] += acc[1]
                return acc
            before = pl.ds(pl.multiple_of(r * SUBLANES, SUBLANES), SUBLANES)
            xpr, xpi = xs[before, re], xs[before, im]
            return acc[0] + gr * xpr + gi * xpi, acc[1] - gr * xpi + gi * xpr

        _scan_rows(gs, 0, ng, ns2, ar_ref, tabr_ref, gcarry, reverse=True, fold=fold)

        for kb in range(nkb):
            cols = slice(kb * LANES, (kb + 1) * LANES)
            re = slice(kb * half, (kb + 1) * half)
            im = slice(ns2 + kb * half, ns2 + (kb + 1) * half)
            uk = u_ref[:, cols]
            gr = gs[:, re].astype(BF16)
            gi = gs[:, im].astype(BF16)
            db_ref[kb, :, :half] += _dot(uk, gr, TN)
            db_ref[kb, :, half:] += _dot(uk, gi, TN)
            duk = _dot(gr, b_ref[kb, :, :half], NT) + _dot(gi, b_ref[kb, :, half:], NT)
            du_ref[:, cols] = (duk + ds_ref[:, cols] * dyv[:, cols]).astype(BF16)

    rev = lambda i: (nblk - 1 - i, 0)
    row = pl.BlockSpec((t, w), rev)
    full = lambda shape: pl.BlockSpec(shape, lambda i: (0,) * len(shape))
    return _hosted_call(
        body, side, name, (nblk,),
        [row, row, row, pl.BlockSpec((1, 1, 2 * ns2), lambda i: (nblk - 1 - i, 0, 0)),
         full(b_blk.shape), full(c_blk.shape), full(a_f.shape), full(a_r.shape), full(tab_f.shape),
         full(tab_r.shape), full(dskip.shape), full(w_glu.shape), full(b_glu.shape)],
        [row, full(b_blk.shape), full(c_blk.shape), full((SUBLANES, 2 * ns2)), full((w, w)), full((SUBLANES, w))],
        [jax.ShapeDtypeStruct((s, w), BF16), jax.ShapeDtypeStruct(b_blk.shape, F32),
         jax.ShapeDtypeStruct(c_blk.shape, F32), jax.ShapeDtypeStruct((SUBLANES, 2 * ns2), F32),
         jax.ShapeDtypeStruct((w, w), F32), jax.ShapeDtypeStruct((SUBLANES, w), F32)],
        [pltpu.VMEM((t + SUBLANES, 2 * ns2), F32), pltpu.VMEM((t, 2 * ns2), F32),
         pltpu.VMEM((t, w), F32), pltpu.VMEM((1, 2 * ns2), F32), pltpu.VMEM((1, 2 * ns2), F32)],
        (u, dys, y, carries, b_blk, c_blk, a_f, a_r, tab_f, tab_r, dskip, w_glu, b_glu))


def _log_sigmoid(x):
    return jnp.minimum(x, 0.0) - jnp.log(1.0 + jnp.exp(-jnp.abs(x)))


def _cum_fwd(name, f_t, b_f):
    h, s = f_t.shape
    tc = _pick(s, 512)
    nb = s // tc

    def body(f_ref, b_ref, c_ref, carry):
        @pl.when(pl.program_id(0) == 0)
        def _():
            carry[...] = jnp.zeros_like(carry)

        lf = _log_sigmoid(f_ref[...] + b_ref[...])
        upper = (lax.broadcasted_iota(jnp.int32, (tc, tc), 0) <= lax.broadcasted_iota(jnp.int32, (tc, tc), 1))
        cum = lax.dot_general(lf, upper.astype(F32), NN, precision=lax.Precision.HIGHEST,
                              preferred_element_type=F32) + carry[...]
        c_ref[...] = cum
        carry[...] += jnp.sum(lf, axis=1, keepdims=True)

    blk = pl.BlockSpec((h, tc), lambda i: (0, i))
    return pl.pallas_call(body, name=name, grid=(nb,), in_specs=[blk, pl.BlockSpec((h, 1), lambda i: (0, 0))],
                          out_specs=blk, out_shape=jax.ShapeDtypeStruct((h, s), F32),
                          scratch_shapes=[pltpu.VMEM((h, 1), F32)], compiler_params=_cparams())(f_t, b_f)


def _cum_bwd(name, dcq, dck, f_t, b_f):
    h, s = f_t.shape
    tc = _pick(s, 512)
    nb = s // tc

    def body(dcq_ref, dck_ref, f_ref, b_ref, df_ref, db_ref, carry):
        @pl.when(pl.program_id(0) == 0)
        def _():
            carry[...] = jnp.zeros_like(carry)
            db_ref[...] = jnp.zeros_like(db_ref)

        dc = dcq_ref[...] + dck_ref[...]
        lower = (lax.broadcasted_iota(jnp.int32, (tc, tc), 0) >= lax.broadcasted_iota(jnp.int32, (tc, tc), 1))
        dlf = lax.dot_general(dc, lower.astype(F32), NN, precision=lax.Precision.HIGHEST,
                              preferred_element_type=F32) + carry[...]
        carry[...] += jnp.sum(dc, axis=1, keepdims=True)
        df = dlf * _sigmoid(-(f_ref[...] + b_ref[...]))
        df_ref[...] = df
        db_ref[...] += jnp.broadcast_to(jnp.sum(df, axis=1, keepdims=True), db_ref.shape)

    blk = pl.BlockSpec((h, tc), lambda i: (0, nb - 1 - i))
    return pl.pallas_call(
        body, name=name, grid=(nb,), in_specs=[blk, blk, blk, pl.BlockSpec((h, 1), lambda i: (0, 0))],
        out_specs=[blk, pl.BlockSpec((h, LANES), lambda i: (0, 0))],
        out_shape=[jax.ShapeDtypeStruct((h, s), F32), jax.ShapeDtypeStruct((h, LANES), F32)],
        scratch_shapes=[pltpu.VMEM((h, 1), F32)], compiler_params=_cparams())(dcq, dck, f_t, b_f)


def _attn_fwd(name, qkv, q_blk, k_blk, v_blk, n_pairs, ck, side=None):
    s = qkv.shape[0]
    dh = LANES // 2
    t = min(ATT_BLOCK, s)
    nq = s // t
    scale = dh ** -0.5

    def body(q_ref, k_ref, v_ref, ck_ref, o_ref, lse_ref, m_s, acc_s):
        i = pl.program_id(1)
        low = lax.broadcasted_iota(jnp.int32, (1, LANES), 1) < dh
        qs = (q_ref[...].astype(F32) * scale).astype(BF16)
        zero = jnp.zeros_like(qs)
        qh = (jnp.where(low, qs, zero), jnp.where(low, zero, qs))
        m_s[...] = jnp.full(m_s.shape, -1e30, F32)
        acc_s[...] = jnp.zeros_like(acc_s)
        causal = (lax.broadcasted_iota(jnp.int32, (t, t), 1) <= lax.broadcasted_iota(jnp.int32, (t, t), 0))

        def step(j, diagonal):
            r0 = pl.multiple_of(j * t, t)
            kj = k_ref[pl.ds(r0, t), :]
            vj = v_ref[pl.ds(r0, t), :]
            one = jnp.ones_like(vj)
            vh = (jnp.where(low, vj, one), jnp.where(low, one, vj))
            for hd in range(2):
                sc = _dot(qh[hd], kj, NT) - ck_ref[hd, j]
                if diagonal:
                    sc = jnp.where(causal, sc, -1e30)
                m_old = m_s[hd]
                m_new = jnp.maximum(m_old, jnp.max(sc, axis=1, keepdims=True))
                p = jnp.exp(sc - m_new)
                acc_s[hd] = jnp.exp(m_old - m_new) * acc_s[hd] + _dot(p.astype(BF16), vh[hd], NN)
                m_s[hd] = m_new

        def full(j, _):
            step(j, False)
            return 0

        lax.fori_loop(0, i, full, 0)
        step(i, True)
        a0, a1 = acc_s[0], acc_s[1]
        o_ref[...] = jnp.where(low, a0 / pltpu.roll(a0, dh, 1), a1 / pltpu.roll(a1, dh, 1)).astype(BF16)
        lse_ref[0] = m_s[0] + jnp.log(a0[:, dh:dh + 1])
        lse_ref[1] = m_s[1] + jnp.log(a1[:, 0:1])

    return _hosted_call(
        body, side, name, (n_pairs, nq),
        [pl.BlockSpec((t, LANES), lambda hp, i: (i, q_blk + hp)),
         pl.BlockSpec((s, LANES), lambda hp, i: (0, k_blk + hp)),
         pl.BlockSpec((s, LANES), lambda hp, i: (0, v_blk + hp)),
         pl.BlockSpec((2, nq, 1, t), lambda hp, i: (hp, 0, 0, 0))],
        [pl.BlockSpec((t, LANES), lambda hp, i: (i, hp)), pl.BlockSpec((2, t, 1), lambda hp, i: (hp, i, 0))],
        [jax.ShapeDtypeStruct((s, LANES * n_pairs), BF16), jax.ShapeDtypeStruct((2 * n_pairs, s, 1), F32)],
        [pltpu.VMEM((2, t, 1), F32), pltpu.VMEM((2, t, LANES), F32)], (qkv, qkv, qkv, ck))


def _attn_bwd(name, qkv, q_blk, k_blk, v_blk, n_pairs, o, do, lse_rows, ck_cols, side=None):
    s = qkv.shape[0]
    dh = LANES // 2
    t = min(ATT_BLOCK, s)
    nk = s // t
    scale = dh ** -0.5

    def body(q_ref, k_ref, v_ref, o_ref, do_ref, lse_ref, ck_ref,
             dq_ref, dk_ref, dv_ref, dcq_ref, dck_ref, delta, dqt, dk_acc, dv_acc):
        j = pl.program_id(1)
        low = lax.broadcasted_iota(jnp.int32, (1, LANES), 1) < dh
        low_rows = lax.broadcasted_iota(jnp.int32, (LANES, 1), 0) < dh

        @pl.when(j == 0)
        def _():
            dqt[...] = jnp.zeros_like(dqt)
            sel = (jnp.broadcast_to(low, (SUBLANES, LANES)).astype(F32), jnp.broadcast_to(~low, (SUBLANES, LANES)).astype(F32))

            def fill(i, _):
                r0 = pl.multiple_of(i * t, t)
                prod = do_ref[pl.ds(r0, t), :].astype(F32) * o_ref[pl.ds(r0, t), :].astype(F32)
                for hd in range(2):
                    delta[hd, i] = lax.dot_general(sel[hd], prod, NT, precision=lax.Precision.HIGHEST,
                                                   preferred_element_type=F32)
                return 0

            lax.fori_loop(0, nk, fill, 0)

        kj, vj = k_ref[...], v_ref[...]
        zero, one = jnp.zeros_like(kj), jnp.ones_like(kj)
        kh = (jnp.where(low, kj, zero), jnp.where(low, zero, kj))
        vh = (jnp.where(low, vj, zero), jnp.where(low, zero, vj))
        kjt = kj.astype(F32).T.astype(BF16)
        one_t = jnp.ones_like(kjt)
        kht = (jnp.where(low_rows, kjt, one_t), jnp.where(low_rows, one_t, kjt))
        dk_acc[...] = jnp.zeros_like(dk_acc)
        dv_acc[...] = jnp.zeros_like(dv_acc)
        causal_t = (lax.broadcasted_iota(jnp.int32, (t, t), 0) <= lax.broadcasted_iota(jnp.int32, (t, t), 1))

        def step(i, diagonal):
            r0 = pl.multiple_of(i * t, t)
            qi = (q_ref[pl.ds(r0, t), :].astype(F32) * scale).astype(BF16)
            doi = do_ref[pl.ds(r0, t), :]
            qone, dzero = jnp.ones_like(qi), jnp.zeros_like(doi)
            qsel = (jnp.where(low, qi, qone), jnp.where(low, qone, qi))
            dosel = (jnp.where(low, doi, dzero), jnp.where(low, dzero, doi))
            for hd in range(2):
                st = _dot(kh[hd], qi, NT) - ck_ref[hd] - lse_ref[hd, i]
                pt = jnp.exp(st)
                if diagonal:
                    pt = jnp.where(causal_t, pt, 0.0)
                dst = pt * (_dot(vh[hd], doi, NT) - delta[hd, i, 0:1, :])
                dsb = dst.astype(BF16)
                dv_acc[...] += _dot(pt.astype(BF16), dosel[hd], NN)
                dk_acc[hd] += _dot(dsb, qsel[hd], NN)
                dqt[hd, i] += _dot(kht[hd], dsb, NN)

        step(j, True)

        def rest(i, _):
            step(i, False)
            return 0

        lax.fori_loop(j + 1, nk, rest, 0)
        dk_ref[...] = jnp.where(low, dk_acc[0], dk_acc[1]).astype(BF16)
        dv_ref[...] = dv_acc[...].astype(BF16)
        dck_ref[0] = -dk_acc[0][:, dh:dh + 1]
        dck_ref[1] = -dk_acc[1][:, 0:1]

        @pl.when(j == nk - 1)
        def _():
            def emit(i, _):
                r0 = pl.multiple_of(i * t, t)
                d0, d1 = dqt[0, i], dqt[1, i]
                dq_ref[pl.ds(r0, t), :] = (jnp.where(low_rows, d0, d1) * scale).T.astype(BF16)
                dcq_ref[0, i] = d0[dh:dh + 1, :]
                dcq_ref[1, i] = d1[0:1, :]
                return 0

            lax.fori_loop(0, nk, emit, 0)

    col_blk = lambda base: pl.BlockSpec((t, LANES), lambda hp, j: (j, base + hp))
    col_all = lambda base: pl.BlockSpec((s, LANES), lambda hp, j: (0, base + hp))
    rows_all = pl.BlockSpec((2, nk, 1, t), lambda hp, j: (hp, 0, 0, 0))
    return _hosted_call(
        body, side, name, (n_pairs, nk),
        [col_all(q_blk), col_blk(k_blk), col_blk(v_blk), col_all(0), col_all(0), rows_all,
         pl.BlockSpec((2, t, 1), lambda hp, j: (hp, j, 0))],
        [col_all(0), col_blk(0), col_blk(0), rows_all, pl.BlockSpec((2, t, 1), lambda hp, j: (hp, j, 0))],
        [jax.ShapeDtypeStruct((s, LANES * n_pairs), BF16), jax.ShapeDtypeStruct((s, LANES * n_pairs), BF16),
         jax.ShapeDtypeStruct((s, LANES * n_pairs), BF16), jax.ShapeDtypeStruct((2 * n_pairs, nk, 1, t), F32),
         jax.ShapeDtypeStruct((2 * n_pairs, s, 1), F32)],
        [pltpu.VMEM((2, nk, SUBLANES, t), F32), pltpu.VMEM((2, nk, LANES, t), F32),
         pltpu.VMEM((2, t, LANES), F32), pltpu.VMEM((t, LANES), F32)],
        (qkv, qkv, qkv, o, do, lse_rows, ck_cols))


def _adamw(name, w, g, m, v):
    n_l, r, c = w.shape
    by_rows = r % SUBLANES == 0
    tr = _pick8(r, max(SUBLANES, ROW_TILE_BYTES // (4 * c))) if by_rows else r
    tl = 1 if by_rows else max(t for t in range(1, n_l + 1) if n_l % t == 0 and t * r * c * 4 <= ROW_TILE_BYTES)

    def body(w_ref, g_ref, m_ref, v_ref, d_ref, mo_ref, vo_ref):
        gv = g_ref[...]
        m2 = ADAM_B1 * m_ref[...] + (1.0 - ADAM_B1) * gv
        v2 = ADAM_B2 * v_ref[...] + (1.0 - ADAM_B2) * (gv * gv)
        m_hat = m2 / (1.0 - ADAM_B1 ** ADAM_STEP)
        v_hat = v2 / (1.0 - ADAM_B2 ** ADAM_STEP)
        d_ref[...] = -ADAM_LR * (m_hat / (jnp.sqrt(v_hat) + ADAM_EPS) + ADAM_WD * w_ref[...])
        mo_ref[...] = m2
        vo_ref[...] = v2

    blk = pl.BlockSpec((None, tr, c) if by_rows else (tl, r, c), lambda l, i: (l, i, 0))
    sh = jax.ShapeDtypeStruct((n_l, r, c), F32)
    return pl.pallas_call(body, name=name, grid=(n_l // tl, r // tr), in_specs=[blk] * 4,
                          out_specs=[blk] * 3, out_shape=[sh, sh, sh], compiler_params=_cparams())(w, g, m, v)


def _pick8(dim, target, mult=SUBLANES):
    best, t = None, mult
    while t <= min(dim, target):
        if dim % t == 0:
            best = t
        t += mult
    return best or dim


BF16_ROWS = 16


def _sum_blocks(name, x, out_dtype):
    n, r, c = x.shape
    tr = _pick8(r, max(BF16_ROWS, SUM_TILE_BYTES // (4 * c)), BF16_ROWS)

    def body(x_ref, o_ref):
        acc = x_ref[0].astype(F32)
        for i in range(1, n):
            acc = acc + x_ref[i].astype(F32)
        o_ref[...] = acc.astype(out_dtype)

    return pl.pallas_call(body, name=name, grid=(r // tr,),
                          in_specs=[pl.BlockSpec((n, tr, c), lambda i: (0, i, 0))],
                          out_specs=pl.BlockSpec((tr, c), lambda i: (i, 0)),
                          out_shape=jax.ShapeDtypeStruct((r, c), out_dtype), compiler_params=_cparams())(x)


def _all_gather(name, x_shard):
    m_per, n = x_shard.shape

    def body(x_ref, out_ref, send_sems, recv_sems):
        x, y, c = lax.axis_index("x"), lax.axis_index("y"), lax.axis_index("c")
        me, sibling = (x, y, c), (x, y, 1 - c)
        chips = [(1 - x, y), (x, 1 - y), (1 - x, 1 - y)]

        def rows(px, py, pc):
            return out_ref.at[pl.ds((4 * px + 2 * py + pc) * m_per, m_per), :]

        def copy(k, block, to, src=None):
            return pltpu.make_async_remote_copy(
                src_ref=rows(*block) if src is None else src, dst_ref=rows(*block),
                send_sem=send_sems.at[k], recv_sem=recv_sems.at[k], device_id=to, device_id_type=MESH)

        first = [copy(0, me, sibling, src=x_ref)]
        first += [copy(1 + j, me, (*chip, c), src=x_ref) for j, chip in enumerate(chips)]
        for cp in first:
            cp.start()
        passed = [copy(4 + j, (*chip, c), sibling) for j, chip in enumerate(chips)]
        for j, chip in enumerate(chips):
            copy(1 + j, (*chip, c), me).wait_recv()
            passed[j].start()
        copy(0, sibling, me).wait_recv()
        for j, chip in enumerate(chips):
            copy(4 + j, (*chip, 1 - c), me).wait_recv()
        for cp in first + passed:
            cp.wait_send()

    out = pl.pallas_call(
        body, name=name, out_shape=jax.ShapeDtypeStruct((N_DEV * m_per, n), x_shard.dtype),
        in_specs=[pl.BlockSpec(memory_space=pl.ANY)], out_specs=pl.BlockSpec(memory_space=pl.ANY),
        scratch_shapes=[pltpu.SemaphoreType.DMA((7,)), pltpu.SemaphoreType.DMA((7,))],
    )(x_shard)
    my_dev = 4 * lax.axis_index("x") + 2 * lax.axis_index("y") + lax.axis_index("c")
    return lax.dynamic_update_slice(out, x_shard, (my_dev * m_per, 0))


def _put_own(out, own, index):
    start = tuple(index) + (0,) * own.ndim
    return lax.dynamic_update_slice(out, own.reshape((1,) * len(index) + own.shape), start)


def _gather_copies(stage, ins, outs, send_sems, recv_sems):
    x, y, c = lax.axis_index("x"), lax.axis_index("y"), lax.axis_index("c")
    my_chip = 2 * x + y
    copies = []
    for w, out in enumerate(outs):
        half = out.shape[1] // 2
        rows = pl.ds(c * half, half)
        for k, (cx, cy) in enumerate([(1 - x, y), (x, 1 - y), (1 - x, 1 - y)]):
            if stage == 0:
                src, dst, to = ins[w].at[rows], out.at[my_chip, rows], (cx, cy, c)
            else:
                src = dst = out.at[2 * cx + cy, rows]
                to = (x, y, 1 - c)
            copies.append(pltpu.make_async_remote_copy(
                src_ref=src, dst_ref=dst, send_sem=send_sems.at[3 * w + k], recv_sem=recv_sems.at[3 * w + k],
                device_id=to, device_id_type=MESH))
    return copies


def _gathered_shapes(shards):
    return [jax.ShapeDtypeStruct((N_CHIPS,) + s.shape, s.dtype) for s in shards]


def _put_own_slabs(gathered, shards):
    my_chip = 2 * lax.axis_index("x") + lax.axis_index("y")
    return [_put_own(o, s, (my_chip,)) for o, s in zip(gathered, shards)]


def _gather_layer(name, shards):
    n_w = len(shards)

    def body(*refs):
        ins, outs = refs[:n_w], refs[n_w:2 * n_w]
        for stage in (0, 1):
            copies = _gather_copies(stage, ins, outs, refs[2 * n_w + 2 * stage], refs[2 * n_w + 2 * stage + 1])
            for cp in copies:
                cp.start()
            for cp in copies:
                cp.wait()

    outs = pl.pallas_call(
        body, name=name, out_shape=_gathered_shapes(shards),
        in_specs=[pl.BlockSpec(memory_space=pl.ANY)] * n_w, out_specs=[pl.BlockSpec(memory_space=pl.ANY)] * n_w,
        scratch_shapes=[pltpu.SemaphoreType.DMA((3 * n_w,))] * 4,
    )(*shards)
    return _put_own_slabs(outs, shards)


def _gather_side_jobs(shards):
    between_chips = _SideJob(list(shards), _gathered_shapes(shards), {}, 3 * len(shards),
                             lambda ins, outs, send, recv: _gather_copies(0, ins, outs, send, recv))
    between_cores = lambda partial: _SideJob(
        list(partial), [jax.ShapeDtypeStruct(p.shape, p.dtype) for p in partial], {w: w for w in range(len(partial))},
        3 * len(partial), lambda ins, outs, send, recv: _gather_copies(1, ins, outs, send, recv))
    return between_chips, between_cores


def _run_job(name, job):
    n_in, n_out = len(job.arrays), len(job.out_shapes)

    def body(*refs):
        copies = job.copies(refs[:n_in], refs[n_in:n_in + n_out], refs[n_in + n_out], refs[n_in + n_out + 1])
        for cp in copies:
            cp.start()
        for cp in copies:
            cp.wait()

    hbm = pl.BlockSpec(memory_space=pl.ANY)
    return pl.pallas_call(
        body, name=name, out_shape=list(job.out_shapes), in_specs=[hbm] * n_in, out_specs=[hbm] * n_out,
        scratch_shapes=[pltpu.SemaphoreType.DMA((job.n_sems,))] * 2, input_output_aliases=dict(job.aliases),
    )(*job.arrays)


def _swap_job(grads):
    def copies(ins, outs, send_sems, recv_sems):
        x, y, c = lax.axis_index("x"), lax.axis_index("y"), lax.axis_index("c")
        return [pltpu.make_async_remote_copy(
            src_ref=g.at[:, pl.ds((1 - c) * (g.shape[1] // 2), g.shape[1] // 2)], dst_ref=outs[w],
            send_sem=send_sems.at[w], recv_sem=recv_sems.at[w], device_id=(x, y, 1 - c), device_id_type=MESH)
            for w, g in enumerate(ins)]

    shapes = [jax.ShapeDtypeStruct((g.shape[0], g.shape[1] // 2, g.shape[2]), g.dtype) for g in grads]
    return _SideJob(list(grads), shapes, {}, len(grads), copies)


def _exchange_job(parts):
    def copies(ins, outs, send_sems, recv_sems):
        x, y, c = lax.axis_index("x"), lax.axis_index("y"), lax.axis_index("c")
        return [pltpu.make_async_remote_copy(
            src_ref=ins[w].at[2 * cx + cy], dst_ref=outs[w].at[2 * x + y], send_sem=send_sems.at[3 * w + k],
            recv_sem=recv_sems.at[3 * w + k], device_id=(cx, cy, c), device_id_type=MESH)
            for w in range(len(ins)) for k, (cx, cy) in enumerate([(1 - x, y), (x, 1 - y), (1 - x, 1 - y)])]

    return _SideJob(list(parts), [jax.ShapeDtypeStruct(p.shape, p.dtype) for p in parts], {}, 3 * len(parts), copies)


def _share_job(bufs, layers):
    def copies(ins, outs, send_sems, recv_sems):
        x, y, c = lax.axis_index("x"), lax.axis_index("y"), lax.axis_index("c")
        mine = [o.at[layer, pl.ds(c * (o.shape[1] // 2), o.shape[1] // 2)] for o, ls in zip(outs, layers) for layer in ls]
        return [pltpu.make_async_remote_copy(src_ref=rows, dst_ref=rows, send_sem=send_sems.at[k], recv_sem=recv_sems.at[k],
                                             device_id=(x, y, 1 - c), device_id_type=MESH) for k, rows in enumerate(mine)]

    return _SideJob(list(bufs), [jax.ShapeDtypeStruct(b.shape, b.dtype) for b in bufs], {w: w for w in range(len(bufs))},
                    sum(len(ls) for ls in layers), copies)


def _sum_into(name, blocks, core, layer, depth, into):
    n, r, c = blocks.shape
    tr = _pick8(r, max(BF16_ROWS, SUM_TILE_BYTES // (4 * c)), BF16_ROWS)
    steps = r // tr

    def body(core_ref, x_ref, *rest):
        acc = x_ref[0].astype(F32)
        for i in range(1, n):
            acc = acc + x_ref[i].astype(F32)
        rest[-1][...] = acc

    grid_spec = pltpu.PrefetchScalarGridSpec(
        num_scalar_prefetch=1, grid=(steps,),
        in_specs=[pl.BlockSpec((n, tr, c), lambda i, core_ref: (0, i, 0))]
        + ([pl.BlockSpec(memory_space=pl.ANY)] if into is not None else []),
        out_specs=pl.BlockSpec((None, tr, c), lambda i, core_ref: (layer, core_ref[0] * steps + i, 0)))
    return pl.pallas_call(
        body, name=name, grid_spec=grid_spec, out_shape=jax.ShapeDtypeStruct((depth, 2 * r, c), F32),
        input_output_aliases={2: 0} if into is not None else {}, compiler_params=_cparams(),
    )(core, blocks, *([into] if into is not None else []))


def _add_rows(name, grads, recv, core):
    n, r, c = recv.shape
    tr = _pick8(r, max(BF16_ROWS, SUM_TILE_BYTES // (4 * c)), BF16_ROWS)
    steps = r // tr

    def body(core_ref, g_ref, r_ref, o_ref):
        o_ref[...] = (g_ref[...].astype(F32) + r_ref[...].astype(F32)).astype(BF16)

    grid_spec = pltpu.PrefetchScalarGridSpec(
        num_scalar_prefetch=1, grid=(steps,),
        in_specs=[pl.BlockSpec((n, tr, c), lambda i, core_ref: (0, core_ref[0] * steps + i, 0)),
                  pl.BlockSpec((n, tr, c), lambda i, core_ref: (0, i, 0))],
        out_specs=pl.BlockSpec((n, tr, c), lambda i, core_ref: (0, i, 0)))
    return pl.pallas_call(body, name=name, grid_spec=grid_spec,
                          out_shape=jax.ShapeDtypeStruct((n, r, c), BF16), compiler_params=_cparams())(core, grads, recv)


class _LayerReduce:
    def __init__(self, tag, layer, depth, names, grads, core, bufs):
        self.tag, self.layer, self.depth, self.names, self.core, self.bufs = tag, layer, depth, list(names), core, bufs
        self.state = list(grads)

    def _exchange(self, name, job, carry):
        if carry is None:
            return None, _run_job(f"{name}_{self.tag}", job)
        return carry(job)

    def swap_and_add(self, carry=None):
        grads = self.state
        results, recv = self._exchange("grads_swap_cores", _swap_job(grads), carry)
        self.state = [_add_rows(f"grads_add_{n}_{self.tag}", g, r, self.core) for n, g, r in zip(self.names, grads, recv)]
        return results

    def exchange_and_sum(self, carry=None, also=()):
        group = [self] + list(also)
        results, arrived = self._exchange("grads_exchange_chips", _exchange_job([p for r in group for p in r.state]), carry)
        my_chip = 2 * lax.axis_index("x") + lax.axis_index("y")
        for r in group:
            mine, arrived = arrived[:len(r.state)], arrived[len(r.state):]
            for n, a, p in zip(r.names, mine, r.state):
                a = _put_own(a, lax.dynamic_index_in_dim(p, my_chip, 0, keepdims=False), (my_chip,))
                r.bufs[n] = _sum_into(f"grads_sum_{n}_{r.tag}", a, r.core, r.layer, r.depth, r.bufs.get(n))
        return results

    def share(self, carry=None, also=()):
        layers = {}
        for r in [self] + list(also):
            for n in r.names:
                layers.setdefault(n, []).append(r.layer)
        names = list(layers)
        job = _share_job([self.bufs[n] for n in names], [layers[n] for n in names])
        results, outs = self._exchange("grads_share_cores", job, carry)
        self.bufs.update(zip(names, outs))
        return results


def _pack(arrays, cols, row_multiple, dtype):
    flat = jnp.concatenate([a.reshape(-1).astype(dtype) for a in arrays])
    unit = cols * row_multiple
    total = -(-flat.shape[0] // unit) * unit
    return jnp.pad(flat, (0, total - flat.shape[0])).reshape(total // cols, cols)


def _unpack(buf, shapes):
    flat, out, off = buf.reshape(-1), [], 0
    for sh in shapes:
        n = math.prod(sh)
        out.append(flat[off:off + n].reshape(sh))
        off += n
    return out


def _discretize(lam_re, lam_im, log_dt, b_re, b_im):
    lam = lax.complex(jnp.minimum(lam_re, -EIG_CLIP), lam_im)
    dt = jnp.exp(log_dt)[:, None]
    lam_bar = jnp.exp(lam * dt)
    b_bar = ((lam_bar - 1.0) / lam)[..., None] * lax.complex(b_re, b_im)
    return jnp.real(lam_bar), jnp.imag(lam_bar), jnp.real(b_bar), jnp.imag(b_bar)


def _scan_tables(ar, ai):
    a = lax.complex(ar, ai)
    pw = [a]
    for _ in range(7):
        pw.append(pw[-1] * a)
    rows = jnp.arange(SUBLANES)[:, None]

    def build(p, reverse):
        tabs = []
        for k in (1, 2, 4):
            keep = (rows <= SUBLANES - 1 - k) if reverse else (rows >= k)
            tk = jnp.where(keep, p[k - 1][None, :], 0.0)
            tabs += [jnp.real(tk), jnp.imag(tk)]
        stack = jnp.stack(p[::-1] if reverse else p)
        tabs += [jnp.real(stack), jnp.imag(stack)]
        return jnp.stack(tabs).astype(F32)

    return build(pw, False), build([jnp.conj(p) for p in pw], True)


def _interleave_rows(a, t):
    s, w = a.shape
    return a.reshape(s // t, SUBLANES, t // SUBLANES, w).transpose(0, 2, 1, 3).reshape(s, w)


def _deinterleave_rows(a, t):
    s, w = a.shape
    return a.reshape(s // t, t // SUBLANES, SUBLANES, w).transpose(0, 2, 1, 3).reshape(s, w)


def _block_diag(per_group, groups_per_block):
    g, a, b = per_group.shape
    x = per_group.reshape(g // groups_per_block, groups_per_block, a, b)
    eye = jnp.eye(groups_per_block, dtype=per_group.dtype)
    out = x[:, :, :, None, :] * eye[None, :, None, :, None]
    return out.reshape(g // groups_per_block, groups_per_block * a, groups_per_block * b)


def _block_diag_extract(dense, groups_per_block, a, b):
    nkb = dense.shape[0]
    x = dense.reshape(nkb, groups_per_block, a, groups_per_block, b)
    idx = jnp.arange(groups_per_block)
    return x[:, idx, :, idx, :].transpose(1, 0, 2, 3).reshape(nkb * groups_per_block, a, b)


def _layer_fwd(tag, x, mod, p, wts, carried=None):
    s, d = x.shape
    w_ssm, w_att = p["w_glu"].shape[0], p["w_att"]
    heads = p["b_f"].shape[0]
    dh = w_att // heads
    cs = d // N_CHIPS
    tm = _pick(s, 1024)
    row = lambda v: v.reshape(1, -1)
    sv = {}

    h = _prenorm_fwd(f"prenorm_mix_{tag}", x, row(p["g_pre_mix"]), row(mod[1]), row(mod[0]))
    uqkv = _mm_plain(f"proj_main_{tag}", h, p["w_main"], "nn", BF16, tm=1024, tn=1024, tk=1024)
    fg = _mm_plain(f"proj_gate_{tag}", h, p["w_gates"], "nn", F32, tm=1024, tn=1024, tk=1024)
    f_t = fg[:, 2 * d:2 * d + heads].T

    t5 = min(S5_ROWS, s)
    u_il = _interleave_rows(uqkv[:, :w_ssm], t5)
    y_s5, ys_il, carries = _s5_fwd(f"s5_fwd_{tag}", u_il, p["b_blk"], p["c_blk"], p["a_f"], p["tab_f"],
                                   row(p["d_skip"]), p["w_glu"], row(p["b_glu"]))
    ys = _deinterleave_rows(ys_il, t5)

    assert dh * 2 == LANES and w_ssm % LANES == 0 and w_att % LANES == 0
    n_pairs = w_att // LANES
    blocks = (w_ssm // LANES, w_ssm // LANES + n_pairs, w_ssm // LANES + 2 * n_pairs)
    cum = _cum_fwd(f"cum_fwd_{tag}", f_t, p["b_f"].reshape(heads, 1))
    t = min(ATT_BLOCK, s)
    ck_cols, ck_rows = cum.reshape(heads, s, 1), cum.reshape(heads, s // t, 1, t)
    late_names, late_shards, next_shards = carried if carried else ((), [], [])
    chips_job, cores_job = _gather_side_jobs(list(late_shards) + list(next_shards)) if carried else (None, None)
    (ya, lse), arrived = _attn_fwd(f"attn_fwd_{tag}", uqkv, *blocks, n_pairs, ck_rows, side=chips_job)
    if late_names:
        late = _run_job(f"gather_weights_late_{tag}", cores_job(arrived[:len(late_names)]))
        wts = {**wts, **dict(zip(late_names, _put_own_slabs(late, late_shards)))}
        arrived = arrived[len(late_names):]
    fs = wts["w_ffn_down"].shape[1]

    tile = pl.BlockSpec((tm, cs), lambda i, j, k: (i, j))
    slab = lambda rows: pl.BlockSpec((None, rows, cs), lambda i, j, k: (j, 0, 0))

    def merge(acc, extra_refs, out_refs):
        ya_ref, wpb_ref, ga_ref, gb_ref = extra_refs
        a_ref, b_ref, m_ref = out_refs
        bv = _dot(ya_ref[...], wpb_ref[...], NN)
        a_ref[...] = acc.astype(BF16)
        b_ref[...] = bv.astype(BF16)
        m_ref[...] = (_sigmoid(ga_ref[...]) * acc + _sigmoid(gb_ref[...]) * bv).astype(BF16)

    sd_bf = jax.ShapeDtypeStruct((s, d), BF16)
    pa, pb, merged = _mm_raw(
        f"merge_{tag}", ys, wts["w_pa"], "nn", (s // tm, N_CHIPS, 1), (tm, cs),
        pl.BlockSpec((tm, w_ssm), lambda i, j, k: (i, 0)), slab(w_ssm), [sd_bf] * 3, [tile] * 3, merge,
        extra=(ya, wts["w_pb"], fg, fg),
        extra_specs=[pl.BlockSpec((tm, w_att), lambda i, j, k: (i, 0)), slab(w_att), tile,
                     pl.BlockSpec((tm, cs), lambda i, j, k: (i, j + N_CHIPS))])

    tm2 = _pick(s, POSTNORM_ROWS)
    x1, y_mix = _mm_postnorm(
        f"out_proj_{tag}", merged, pl.BlockSpec((tm2, cs), lambda i, j, k: (i, k)), wts["w_o"],
        pl.BlockSpec((None, cs, d), lambda i, j, k: (k, 0, 0)), N_CHIPS, x, row(mod[2]), row(p["g_post_mix"]))

    h2 = _prenorm_fwd(f"prenorm_ffn_{tag}", x1, row(p["g_pre_ffn"]), row(mod[4]), row(mod[3]))
    (a4, b4, hid4), next_wts = _ffn_up(f"ffn_up_{tag}", h2, wts["w_ffn_gate"], wts["w_ffn_up"],
                                       side=cores_job(arrived) if carried and arrived else None)
    x2, y_ffn = _mm_postnorm(
        f"ffn_down_{tag}", hid4, pl.BlockSpec((None, tm2, fs), lambda i, j, k: (k, i, 0)), wts["w_ffn_down"],
        pl.BlockSpec((None, fs, d), lambda i, j, k: (k, 0, 0)), N_CHIPS, x1, row(mod[5]), row(p["g_post_ffn"]))

    sv.update(x=x, h=h, uqkv=uqkv, u_il=u_il, fg=fg, f_t=f_t, y_s5=y_s5, ys=ys, carries=carries, blocks=blocks,
              ck_cols=ck_cols, lse_rows=lse.reshape(heads, s // t, 1, t), ya=ya, pa=pa, pb=pb, merged=merged, x1=x1,
              y_mix=y_mix, h2=h2, a4=a4, b4=b4, hid4=hid4, y_ffn=y_ffn)
    return x2, sv, wts, next_wts


def _mm_postnorm(name, a, a_spec, w, w_spec, nk, x, gate, g):
    s, d = x.shape
    tm = _pick(s, POSTNORM_ROWS)
    rowspec = pl.BlockSpec((tm, d), lambda i, j, k: (i, 0))
    vec = pl.BlockSpec((1, d), lambda i, j, k: (0, 0))

    def epilogue(acc, extra_refs, out_refs):
        x_ref, gate_ref, g_ref = extra_refs
        r = lax.rsqrt(jnp.mean(acc * acc, axis=-1, keepdims=True) + RMS_EPS)
        out_refs[0][...] = x_ref[...] + gate_ref[...] * (acc * r * g_ref[...])
        out_refs[1][...] = acc

    sd = jax.ShapeDtypeStruct((s, d), F32)
    return _mm_raw(name, a, w, "nn", (s // tm, 1, nk), (tm, d), a_spec, w_spec, [sd, sd], [rowspec, rowspec], epilogue,
                   extra=(x, gate, g), extra_specs=[rowspec, vec, vec])


def _layer_bwd(tag, dx2, mod, p, wts, sv, reduce_later=None, early=None):
    s, d = dx2.shape
    w_ssm, w_att = p["w_glu"].shape[0], wts["w_pb"].shape[1]
    heads = p["b_f"].shape[0]
    cs = d // N_CHIPS
    fs = wts["w_ffn_down"].shape[1]
    tm, tk, td = _pick(s, 1024), _pick(s, 1024), d
    row = lambda v: v.reshape(1, -1)
    gr = {}

    def dw_slabs(name, act, act_spec, rows, dy, dy_spec, cols, grid_mn, out_index):
        return _mm_raw(name, act, dy, "tn", grid_mn + (s // tk,), (rows, cols), act_spec, dy_spec,
                       [jax.ShapeDtypeStruct((N_CHIPS,) + out_index[1], BF16)],
                       [pl.BlockSpec((None, rows, cols), out_index[0])], _store(BF16))[0]

    dy_ffn, sums = _postnorm_bwd(f"postnorm_bwd_ffn_{tag}", dx2, sv["y_ffn"], row(p["g_post_ffn"]), row(mod[5]))
    d_gate_f, gr["g_post_ffn"] = sums[0], sums[1]
    gr["w_ffn_down"] = dw_slabs(f"dw_down_{tag}", sv["hid4"], pl.BlockSpec((None, tk, fs), lambda i, j, k: (i, k, 0)), fs,
                                dy_ffn, pl.BlockSpec((tk, d), lambda i, j, k: (k, 0)), d, (N_CHIPS, 1),
                                (lambda i, j, k: (i, 0, 0), (fs, d)))

    def swiglu_bwd(acc, extra_refs, out_refs):
        av, bv = extra_refs[0][...].astype(F32), extra_refs[1][...].astype(F32)
        sg = _sigmoid(av)
        out_refs[0][...] = (acc * bv * (sg * (1.0 + av * (1.0 - sg)))).astype(BF16)
        out_refs[1][...] = (acc * (av * sg)).astype(BF16)

    blk4 = pl.BlockSpec((None, tm, fs), lambda i, j, k: (j, i, 0))
    sh4 = jax.ShapeDtypeStruct((N_CHIPS, s, fs), BF16)
    ffn_down_bwd = lambda side: _mm_raw(
        f"ffn_down_bwd_{tag}", dy_ffn, wts["w_ffn_down"], "nt", (s // tm, N_CHIPS, 1), (tm, fs),
        pl.BlockSpec((tm, d), lambda i, j, k: (i, 0)), pl.BlockSpec((None, fs, d), lambda i, j, k: (j, 0, 0)),
        [sh4, sh4], [blk4, blk4], swiglu_bwd, extra=(sv["a4"], sv["b4"]), extra_specs=[blk4, blk4], side=side)
    da4, db4 = reduce_later.swap_and_add(ffn_down_bwd) if reduce_later else ffn_down_bwd(None)
    for n, act4 in (("w_ffn_gate", da4), ("w_ffn_up", db4)):
        gr[n] = dw_slabs(f"d{n}_{tag}", sv["h2"], pl.BlockSpec((tk, td), lambda i, j, k: (k, i)), td,
                         act4, pl.BlockSpec((None, tk, fs), lambda i, j, k: (j, k, 0)), fs, (d // td, N_CHIPS),
                         (lambda i, j, k: (j, i, 0), (d, fs)))
    pairs = [(act4, (None, tm, fs), lambda i, kk: (kk, i, 0), wts[n], (None, td, fs), lambda j, kk: (kk, j, 0),
              N_CHIPS) for n, act4 in (("w_ffn_gate", da4), ("w_ffn_up", db4))]
    own_early = None
    if early is not None:
        own_early = _LayerReduce(f"{tag}e", early[0], early[1], EARLY_REDUCED, [gr[n] for n in EARLY_REDUCED], *early[2:])
    dh_ffn = lambda side: _mm_sum(f"dh_ffn_{tag}", s, d, tm, td, pairs, F32, side=side)
    dh2 = own_early.swap_and_add(dh_ffn) if own_early else dh_ffn(None)
    dx1, sums = _prenorm_bwd(f"prenorm_bwd_ffn_{tag}", dh2, sv["x1"], row(p["g_pre_ffn"]), row(mod[4]), dx2)
    d_scale_f, d_shift_f, gr["g_pre_ffn"] = sums[0], sums[1], sums[2]

    dy_mix, sums = _postnorm_bwd(f"postnorm_bwd_mix_{tag}", dx1, sv["y_mix"], row(p["g_post_mix"]), row(mod[2]))
    d_gate_m, gr["g_post_mix"] = sums[0], sums[1]
    gr["w_o"] = dw_slabs(f"dw_o_{tag}", sv["merged"], pl.BlockSpec((tk, cs), lambda i, j, k: (k, i)), cs,
                         dy_mix, pl.BlockSpec((tk, d), lambda i, j, k: (k, 0)), d, (N_CHIPS, 1),
                         (lambda i, j, k: (i, 0, 0), (cs, d)))

    tile = pl.BlockSpec((tm, cs), lambda i, j, k: (i, j))

    def merge_bwd(acc, extra_refs, out_refs):
        a_ref, b_ref, ga_ref, gb_ref = extra_refs
        sa, sb = _sigmoid(ga_ref[...]), _sigmoid(gb_ref[...])
        out_refs[0][...] = (acc * sa).astype(BF16)
        out_refs[1][...] = (acc * sb).astype(BF16)
        out_refs[2][...] = (acc * a_ref[...].astype(F32) * sa * (1.0 - sa)).astype(BF16)
        out_refs[3][...] = (acc * b_ref[...].astype(F32) * sb * (1.0 - sb)).astype(BF16)

    sd_bf = jax.ShapeDtypeStruct((s, d), BF16)
    d_pa, d_pb, d_ga, d_gb = _mm_raw(
        f"out_proj_bwd_{tag}", dy_mix, wts["w_o"], "nt", (s // tm, N_CHIPS, 1), (tm, cs),
        pl.BlockSpec((tm, d), lambda i, j, k: (i, 0)), pl.BlockSpec((None, cs, d), lambda i, j, k: (j, 0, 0)),
        [sd_bf] * 4, [tile] * 4, merge_bwd, extra=(sv["pa"], sv["pb"], sv["fg"], sv["fg"]),
        extra_specs=[tile, tile, tile, pl.BlockSpec((tm, cs), lambda i, j, k: (i, j + N_CHIPS))])
    branches = (("w_pa", sv["ys"], w_ssm, d_pa), ("w_pb", sv["ya"], w_att, d_pb))
    for n, act, width, d_p in branches:
        gr[n] = dw_slabs(f"d{n}_{tag}", act, pl.BlockSpec((tk, width), lambda i, j, k: (k, 0)), width,
                         d_p, pl.BlockSpec((tk, cs), lambda i, j, k: (k, j)), cs, (1, N_CHIPS),
                         (lambda i, j, k: (j, 0, 0), (width, cs)))
    own_mid = None
    if early is not None:
        own_mid = _LayerReduce(f"{tag}m", early[0], early[1], MID_REDUCED, [gr[n] for n in MID_REDUCED], *early[2:])
    d_branch = {}
    for n, act, width, d_p in branches:
        d_in = lambda side, n=n, width=width, d_p=d_p: _mm_raw(
            f"d_in_{n}_{tag}", d_p, wts[n], "nt", (s // tm, 1, N_CHIPS), (tm, width),
            pl.BlockSpec((tm, cs), lambda i, j, k: (i, k)), pl.BlockSpec((None, width, cs), lambda i, j, k: (k, 0, 0)),
            [jax.ShapeDtypeStruct((s, width), BF16)], [pl.BlockSpec((tm, width), lambda i, j, k: (i, 0))], _store(BF16),
            side=side)
        d_branch[n] = (own_mid.swap_and_add(d_in) if own_mid and n == branches[0][0] else d_in(None))[0]
    d_ys, d_ya = d_branch["w_pa"], d_branch["w_pb"]
    own = [r for r in (own_early, own_mid) if r is not None]

    attn_bwd = lambda side: _attn_bwd(f"attn_bwd_{tag}", sv["uqkv"], *sv["blocks"], w_att // LANES, sv["ya"], d_ya,
                                      sv["lse_rows"], sv["ck_cols"], side=side)
    dq, dk, dv, dcq, dck = (reduce_later.exchange_and_sum(attn_bwd, also=[own_mid] if own_mid else [])
                            if reduce_later else attn_bwd(None)[0])
    d_f_t, d_bf = _cum_bwd(f"cum_bwd_{tag}", dcq.reshape(heads, s), dck.reshape(heads, s), sv["f_t"],
                           p["b_f"].reshape(heads, 1))
    gr["b_f"] = d_bf[:, 0]

    t5 = min(S5_ROWS, s)
    s5_bwd = lambda side: _s5_bwd(
        f"s5_bwd_{tag}", sv["u_il"], _interleave_rows(d_ys, t5), sv["y_s5"], sv["carries"], p["b_blk"], p["c_blk"],
        p["a_f"], p["a_r"], p["tab_f"], p["tab_r"], row(p["d_skip"]), p["w_glu"], row(p["b_glu"]), side=side)
    du_il, d_bblk, d_cblk, d_abar, d_wglu, vec = own_early.exchange_and_sum(s5_bwd) if own_early else s5_bwd(None)[0]
    du = _deinterleave_rows(du_il, t5)
    gr["w_glu"] = d_wglu.astype(BF16).reshape(N_CHIPS, w_ssm // N_CHIPS, w_ssm)
    gr["b_glu"], gr["d_skip"] = vec[0], vec[1]
    gr["b_blk"], gr["c_blk"], gr["a_bar"] = d_bblk, d_cblk, d_abar

    d_f = jnp.pad(d_f_t.T, ((0, 0), (0, F_PAD - heads))).astype(BF16)
    assert w_ssm % w_att == 0 and (2 * d) % F_PAD == 0
    first = w_ssm // w_att
    main_pieces = [(du, w_ssm, 0), (dq, w_att, first), (dk, w_att, first + 1), (dv, w_att, first + 2)]
    dw = [_mm_plain(f"dw_in{n}_{tag}", sv["h"], piece, "tn", BF16, tm=1024, tn=1024, tk=1024)
          for n, piece in enumerate([du, dq, dk, dv, d_f, d_ga, d_gb])]
    w_in_grad = jnp.concatenate(dw[:4] + [dw[4][:, :heads], dw[5], dw[6]], axis=1)
    gr["w_in"] = w_in_grad.reshape(d, N_CHIPS, w_in_grad.shape[1] // N_CHIPS).transpose(1, 0, 2)
    tmx, tkx = _pick(s, 1024), _pick(d, 512)
    pairs = [(piece, (tmx, width), lambda i, kk: (i, 0), p["w_main"], (d, width), lambda j, kk, blk=blk: (j, blk), 1)
             for piece, width, blk in main_pieces]
    steps = d // tkx
    pairs += [(piece, (tmx, tkx), lambda i, kk: (i, kk), p["w_gates"], (d, tkx), lambda j, kk, off=off: (j, off + kk), steps)
              for piece, off in ((d_ga, 0), (d_gb, steps))]
    pairs.append((d_f, (tmx, F_PAD), lambda i, kk: (i, 0), p["w_gates"], (d, F_PAD), lambda j, kk: (j, 2 * d // F_PAD), 1))
    dh_mix = lambda side: _mm_sum(f"dh_mix_{tag}", s, d, tmx, d, pairs, F32, side=side)
    dh1 = reduce_later.share(dh_mix, also=own) if reduce_later else dh_mix(None)
    dx0, sums = _prenorm_bwd(f"prenorm_bwd_mix_{tag}", dh1, sv["x"], row(p["g_pre_mix"]), row(mod[1]), dx1)
    d_scale_m, d_shift_m, gr["g_pre_mix"] = sums[0], sums[1], sums[2]

    d_mod = jnp.stack([d_shift_m, d_scale_m, d_gate_m, d_shift_f, d_scale_f, d_gate_f])
    return dx0, d_mod, gr


BIG = ("w_in", "w_glu", "w_pa", "w_pb", "w_o", "w_ffn_gate", "w_ffn_up", "w_ffn_down")
FIRST_USED = ("w_in", "w_glu")
EARLY_REDUCED = ("w_ffn_gate", "w_ffn_up", "w_ffn_down")
MID_REDUCED = ("w_pa", "w_pb", "w_o")
SMALL = ("b_ada", "g_pre_mix", "g_post_mix", "g_pre_ffn", "g_post_ffn", "lam_re", "lam_im", "log_dt", "b_re", "b_im",
         "c_re", "c_im", "d_skip", "b_glu", "b_f")
WEIGHTS = ("w_ada", "b_ada", "g_pre_mix", "g_post_mix", "g_pre_ffn", "g_post_ffn", "w_in", "lam_re", "lam_im", "log_dt",
           "b_re", "b_im", "c_re", "c_im", "d_skip", "w_glu", "b_glu", "b_f", "w_pa", "w_pb", "w_o", "w_ffn_gate",
           "w_ffn_up", "w_ffn_down")


def _prepare_layer(wts, small, l, seq):
    w_in = jnp.concatenate([wts["w_in"][j] for j in range(N_CHIPS)], axis=1)
    d = w_in.shape[0]
    heads = small["b_f"].shape[1]
    n_groups, n_state, group_ch = small["b_re"].shape[1:]
    w_ssm = n_groups * group_ch
    w_att = (w_in.shape[1] - w_ssm - heads - 2 * d) // 3
    n_main = w_ssm + 3 * w_att
    gpb = LANES // group_ch
    p = {"w_att": w_att}
    p["w_main"] = w_in[:, :n_main]
    p["w_gates"] = jnp.concatenate(
        [w_in[:, n_main + heads:], w_in[:, n_main:n_main + heads], jnp.zeros((d, F_PAD - heads), BF16)], axis=1)
    p["w_glu"] = wts["w_glu"].reshape(w_ssm, w_ssm)
    for n in ("g_pre_mix", "g_post_mix", "g_pre_ffn", "g_post_ffn", "d_skip", "b_glu", "b_f"):
        p[n] = small[n][l]
    ar, ai, br, bi = _discretize(small["lam_re"][l], small["lam_im"][l], small["log_dt"][l], small["b_re"][l], small["b_im"][l])
    n_steps = min(S5_ROWS, seq) // SUBLANES
    powers = jnp.cumprod(jnp.broadcast_to(lax.complex(ar, ai).reshape(1, -1), (n_steps, ar.size)), axis=0)
    p["a_f"] = jnp.concatenate([jnp.real(powers), jnp.imag(powers)], axis=1)
    p["a_r"] = jnp.concatenate([jnp.real(powers[::-1]), -jnp.imag(powers[::-1])], axis=1)
    p["tab_f"], p["tab_r"] = _scan_tables(jnp.real(powers[-1]), jnp.imag(powers[-1]))
    bre = _block_diag(br.transpose(0, 2, 1), gpb)
    bim = _block_diag(bi.transpose(0, 2, 1), gpb)
    p["b_blk"] = jnp.concatenate([bre, bim], axis=2).astype(BF16)
    cre = _block_diag(small["c_re"][l].transpose(0, 2, 1), gpb)
    cim = _block_diag(small["c_im"][l].transpose(0, 2, 1), gpb)
    p["c_blk"] = jnp.concatenate([cre, -cim], axis=1).astype(BF16)
    return p


def _compact_partials(gr, n_state, group_ch):
    gpb = LANES // group_ch
    half = gpb * n_state
    out = dict(gr)
    out["bbar_re"] = _block_diag_extract(gr["b_blk"][:, :, :half], gpb, group_ch, n_state).transpose(0, 2, 1)
    out["bbar_im"] = _block_diag_extract(gr["b_blk"][:, :, half:], gpb, group_ch, n_state).transpose(0, 2, 1)
    out["c_re"] = _block_diag_extract(gr["c_blk"][:, :half, :], gpb, n_state, group_ch).transpose(0, 2, 1)
    out["c_im"] = -_block_diag_extract(gr["c_blk"][:, half:, :], gpb, n_state, group_ch).transpose(0, 2, 1)
    return out


def _small_grads_from_partials(gr, small, l):
    n_groups, n_state, _ = small["b_re"].shape[1:]
    ns2 = n_groups * n_state
    d_abar = jnp.sum(gr["a_bar"], axis=0)
    dar, dai = d_abar[:ns2].reshape(n_groups, n_state), d_abar[ns2:].reshape(n_groups, n_state)
    args = (small["lam_re"][l], small["lam_im"][l], small["log_dt"][l], small["b_re"][l], small["b_im"][l])
    _, vjp = jax.vjp(_discretize, *args)
    d_lam_re, d_lam_im, d_log_dt, d_b_re, d_b_im = vjp((dar, dai, gr["bbar_re"], gr["bbar_im"]))
    return dict(lam_re=d_lam_re, lam_im=d_lam_im, log_dt=d_log_dt, b_re=d_b_re, b_im=d_b_im,
                c_re=gr["c_re"], c_im=gr["c_im"])


def _fwd_bwd(xs, target, mods, small, wts0, later, core=None, late0=None):
    depth = 1 + len(later)
    saved, layers, wts = [], [], [wts0]
    act = xs
    for l in range(depth):
        layers.append(_prepare_layer(wts[l], small, l, xs.shape[0]))
        shards = later[l] if l + 1 < depth and not isinstance(later[l], dict) else None
        late = late0 if l == 0 and late0 else ((), [])
        carried = (late[0], late[1], shards or []) if (late[0] or shards) else None
        act, sv, wts[l], gathered = _layer_fwd(str(l), act, mods[l], layers[l], wts[l], carried=carried)
        saved.append(sv)
        if l + 1 < depth:
            wts.append(dict(zip(BIG, _put_own_slabs(gathered, shards))) if shards is not None else later[l])
    dx, loss_blk = _loss_grad("loss", act, target)
    grads, d_mods = [None] * depth, [None] * depth
    pending, bufs = None, {}
    for l in reversed(range(depth)):
        early = (l, depth, core, bufs) if pending is not None else None
        dx, d_mods[l], grads[l] = _layer_bwd(str(l), dx, mods[l], layers[l], wts[l], saved[l], reduce_later=pending,
                                             early=early)
        if core is not None:
            names = [n for n in BIG if early is None or n not in EARLY_REDUCED + MID_REDUCED]
            pending = _LayerReduce(str(l), l, depth, names, [grads[l][n] for n in names], core, bufs)
    if core is None:
        return loss_blk, dx, d_mods, grads, None
    pending.swap_and_add()
    pending.exchange_and_sum()
    pending.share()
    return loss_blk, dx, d_mods, grads, bufs


def kernel(x, c, w_ada, b_ada, g_pre_mix, g_post_mix, g_pre_ffn, g_post_ffn, w_in, lam_re, lam_im, log_dt, b_re, b_im, c_re, c_im, d_skip, w_glu, b_glu, b_f, w_pa, w_pb, w_o, w_ffn_gate, w_ffn_up, w_ffn_down, loss_target, m_w_ada, m_b_ada, m_g_pre_mix, m_g_post_mix, m_g_pre_ffn, m_g_post_ffn, m_w_in, m_lam_re, m_lam_im, m_log_dt, m_b_re, m_b_im, m_c_re, m_c_im, m_d_skip, m_w_glu, m_b_glu, m_b_f, m_w_pa, m_w_pb, m_w_o, m_w_ffn_gate, m_w_ffn_up, m_w_ffn_down, v_w_ada, v_b_ada, v_g_pre_mix, v_g_post_mix, v_g_pre_ffn, v_g_post_ffn, v_w_in, v_lam_re, v_lam_im, v_log_dt, v_b_re, v_b_im, v_c_re, v_c_im, v_d_skip, v_w_glu, v_b_glu, v_b_f, v_w_pa, v_w_pb, v_w_o, v_w_ffn_gate, v_w_ffn_up, v_w_ffn_down):
    local = dict(locals())
    weights = {n: local[n] for n in WEIGHTS}
    moments_m = {n: local["m_" + n] for n in WEIGHTS}
    moments_v = {n: local["v_" + n] for n in WEIGHTS}
    depth, d = g_pre_mix.shape
    n_mod = w_ada.shape[2] * N_CHIPS // d
    mx, my, mc = lax.axis_index("x"), lax.axis_index("y"), lax.axis_index("c")
    my_chip = 2 * mx + my
    my_dev = 4 * mx + 2 * my + mc
    xs = x[0]

    shards = [[weights[n][l].astype(BF16) for n in BIG] for l in range(depth)]
    early = [i for i, n in enumerate(BIG) if n in FIRST_USED]
    late = [i for i, n in enumerate(BIG) if n not in FIRST_USED]
    wts0 = dict(zip([BIG[i] for i in early], _gather_layer("gather_weights_0", [shards[0][i] for i in early])))
    late0 = ([BIG[i] for i in late], [shards[0][i] for i in late])
    small = {n: weights[n] for n in SMALL}

    c_pad = jnp.pad(c, ((0, SUBLANES - 1), (0, 0)))
    c_all = _all_gather("gather_cond", c_pad).reshape(N_DEV, SUBLANES, d)[:, 0, :]
    silu = lambda v: v * _sigmoid(v)
    n_cols = w_ada.shape[2]
    mod_shard = []
    for l in range(depth):
        bias = lax.dynamic_slice_in_dim(b_ada[l], my_chip * n_cols, n_cols)
        mod_shard.append(_mm_plain(f"ada_{l}", c_all, w_ada[l], "nn", F32, add=jnp.broadcast_to(bias, (N_DEV, n_cols)),
                                   a_fn=silu, tm=N_DEV, tn=512, tk=1024))
    mod_block = jnp.concatenate(mod_shard, axis=1)
    mod_all = _all_gather("gather_mod", mod_block).reshape(N_DEV, N_DEV, depth, n_cols)
    mod_rows = lax.dynamic_index_in_dim(mod_all[0::2], my_dev, axis=1, keepdims=False)
    mods = [mod_rows[:, l, :].reshape(n_mod, d) for l in range(depth)]

    loss_blk, dx, d_mods, grads, big_grads = _fwd_bwd(xs, loss_target[0], mods, small, wts0, shards[1:],
                                                      core=mc.astype(jnp.int32).reshape(1), late0=late0)
    loss = lax.psum(loss_blk[0, 0], ("x", "y", "c"))
    grad_x = dx[None]

    partial_names = ("g_pre_mix", "g_post_mix", "g_pre_ffn", "g_post_ffn", "d_skip", "b_glu", "b_f", "a_bar",
                     "bbar_re", "bbar_im", "c_re", "c_im")
    n_state, group_ch = b_re.shape[2:]
    contrib = list(d_mods)
    for l in range(depth):
        compact = _compact_partials(grads[l], n_state, group_ch)
        contrib += [compact[n] for n in partial_names]
    contrib_shapes = [a.shape for a in contrib]
    block = _pack(contrib, LANES, BF16_ROWS, F32)
    rows = block.shape[0]
    all_blocks = _all_gather("gather_small_grads", block).reshape(N_DEV, rows, LANES)
    summed = _unpack(_sum_blocks("sum_small_grads", all_blocks, F32), contrib_shapes)
    per_layer = len(partial_names)
    small_grads = {n: [] for n in SMALL}
    d_mod_all = []
    for l in range(depth):
        small_grads["b_ada"].append(summed[l].reshape(-1))
        gl = dict(zip(partial_names, summed[depth + l * per_layer:depth + (l + 1) * per_layer]))
        for n in ("g_pre_mix", "g_post_mix", "g_pre_ffn", "g_post_ffn", "d_skip", "b_glu", "b_f"):
            small_grads[n].append(gl[n])
        for n, gval in _small_grads_from_partials(gl, small, l).items():
            small_grads[n].append(gval)
        mod_rows_ = n_mod * d // LANES
        d_mod_all.append(all_blocks[:, l * mod_rows_:(l + 1) * mod_rows_, :].reshape(N_DEV, n_mod * d))
    small_grads = {n: jnp.stack(v) for n, v in small_grads.items()}

    g_w_ada = []
    for l in range(depth):
        cols = lax.dynamic_slice_in_dim(d_mod_all[l], my_chip * n_cols, n_cols, axis=1)
        g_w_ada.append(_mm_plain(f"dw_ada_{l}", c_all, cols, "tn", F32, a_fn=silu, tm=512, tn=512, tk=N_DEV))
    all_grads = dict(big_grads)
    all_grads.update(small_grads)
    all_grads["w_ada"] = jnp.stack(g_w_ada)

    delta, new_m, new_v = {}, {}, {}
    for n in ("w_ada",) + BIG:
        last = weights[n].shape[2]
        to_stored, from_stored = ((0, 1, 2),) * 2 if last % LANES == 0 else ((0, 2, 1),) * 2 if last % SUBLANES == 0 \
            else ((2, 0, 1), (1, 2, 0))
        view, back = (lambda a: a.transpose(to_stored)), (lambda a: a.transpose(from_stored))
        outs = _adamw(f"adamw_{n}", view(weights[n]), view(all_grads[n]), view(moments_m[n]), view(moments_v[n]))
        delta[n], new_m[n], new_v[n] = (back(o) for o in outs)
    small_shapes = [weights[n].shape for n in SMALL]
    packed = [_pack([src[n] for n in SMALL], LANES, SUBLANES, F32)[None] for src in (weights, all_grads, moments_m, moments_v)]
    outs = _adamw("adamw_small", *packed)
    for dst, buf in zip((delta, new_m, new_v), outs):
        dst.update(dict(zip(SMALL, _unpack(buf[0], small_shapes))))

    return (loss, grad_x, *[all_grads[n] for n in WEIGHTS], *[delta[n] for n in WEIGHTS],
            *[new_m[n] for n in WEIGHTS], *[new_v[n] for n in WEIGHTS])
```

```python
import math

import jax
import jax.numpy as jnp
from jax import lax
from jax.experimental import pallas as pl
from jax.experimental.pallas import tpu as pltpu

F32 = jnp.float32
BF16 = jnp.bfloat16
MESH = pl.DeviceIdType.MESH

RMS_EPS = 1e-6
EIG_CLIP = 1e-4
ADAM_LR, ADAM_B1, ADAM_B2, ADAM_EPS, ADAM_WD, ADAM_STEP = 0.001, 0.9, 0.999, 1e-08, 0.01, 10

LANES = 128
SUBLANES = 8
VMEM_LIMIT = 56 * 1024 * 1024
ROW_TILE_BYTES = 1 << 20
SUM_TILE_BYTES = 1 << 19
S5_ROWS = 256
S5_CHUNK = 1024
S5_UNROLL = 4
ATT_BLOCK = 512
F_PAD = 256
POSTNORM_ROWS = 1024
N_CHIPS = 4
N_DEV = 8

NN = (((1,), (0,)), ((), ()))
NT = (((1,), (1,)), ((), ()))
TN = (((0,), (0,)), ((), ()))
_DN = {"nn": NN, "nt": NT, "tn": TN}


def _cparams(**kw):
    return pltpu.CompilerParams(vmem_limit_bytes=VMEM_LIMIT, **kw)


def _pick(dim, target):
    best, t = None, LANES
    while t <= min(dim, target):
        if dim % t == 0:
            best = t
        t += LANES
    return best or dim


def _sigmoid(x):
    return 1.0 / (1.0 + jnp.exp(-x))


def _dot(a, b, dn):
    return lax.dot_general(a, b, dn, preferred_element_type=F32)


def _mm_raw(name, a, b, mode, grid, acc_shape, a_spec, b_spec, out_shapes, out_specs, epilogue,
            extra=(), extra_specs=(), a_fn=None, side=None):
    nk = grid[2]
    n_extra, n_out = len(extra), len(out_shapes)

    def body(*refs):
        a_ref, b_ref = refs[0], refs[1]
        extra_refs = refs[2:2 + n_extra]
        out_refs = refs[2 + n_extra:2 + n_extra + n_out]
        acc = refs[-1]
        k = pl.program_id(2)

        @pl.when(k == 0)
        def _():
            acc[...] = jnp.zeros_like(acc)

        av = a_ref[...]
        if a_fn is not None:
            av = a_fn(av.astype(F32))
        acc[...] += _dot(av.astype(BF16), b_ref[...].astype(BF16), _DN[mode])

        @pl.when(k == nk - 1)
        def _():
            epilogue(acc[...], extra_refs, out_refs)

    outs, side_outs = _hosted_call(body, side, name, grid, [a_spec, b_spec, *extra_specs], list(out_specs),
                                   list(out_shapes), [pltpu.VMEM(acc_shape, F32)], (a, b, *extra))
    return outs if side is None else (outs, side_outs)


def _mm(name, a, b, mode, out_shapes, out_specs, epilogue, extra=(), extra_specs=(),
        tm=512, tn=512, tk=512, a_fn=None):
    if mode == "nn":
        (m, kd), (_, n) = a.shape, b.shape
    elif mode == "nt":
        (m, kd), (n, _) = a.shape, b.shape
    else:
        (kd, m), (_, n) = a.shape, b.shape
    tm, tn, tk = _pick(m, tm), _pick(n, tn), _pick(kd, tk)
    if mode == "tn":
        a_spec = pl.BlockSpec((tk, tm), lambda i, j, k: (k, i))
    else:
        a_spec = pl.BlockSpec((tm, tk), lambda i, j, k: (i, k))
    if mode == "nt":
        b_spec = pl.BlockSpec((tn, tk), lambda i, j, k: (j, k))
    else:
        b_spec = pl.BlockSpec((tk, tn), lambda i, j, k: (k, j))
    res = _mm_raw(name, a, b, mode, (m // tm, n // tn, kd // tk), (tm, tn), a_spec, b_spec, out_shapes, out_specs,
                  epilogue, extra=extra, extra_specs=extra_specs, a_fn=a_fn)
    return res, (tm, tn, tk)


def _store(dtype):
    def epilogue(acc, extra_refs, out_refs):
        out_refs[0][...] = acc.astype(dtype)
    return epilogue


def _mm_sum(name, m, n, tm, tn, pairs, out_dtype, side=None):
    offs, total = [], 0
    for pr in pairs:
        offs.append(total)
        total += pr[6]
    n_p = len(pairs)

    def body(*refs):
        o_ref, acc = refs[2 * n_p], refs[2 * n_p + 1]
        k = pl.program_id(2)

        @pl.when(k == 0)
        def _():
            acc[...] = jnp.zeros_like(acc)

        for p_ in range(n_p):
            @pl.when((k >= offs[p_]) & (k < offs[p_] + pairs[p_][6]))
            def _(p_=p_):
                acc[...] += _dot(refs[2 * p_][...].astype(BF16), refs[2 * p_ + 1][...].astype(BF16), NT)

        @pl.when(k == total - 1)
        def _():
            o_ref[...] = acc[...].astype(out_dtype)

    in_specs, operands = [], []
    for (a, a_block, a_index, b, b_block, b_index, steps), off in zip(pairs, offs):
        local = lambda k, off=off, steps=steps: jnp.clip(k - off, 0, steps - 1)
        in_specs.append(pl.BlockSpec(a_block, lambda i, j, k, f=a_index, local=local: f(i, local(k))))
        in_specs.append(pl.BlockSpec(b_block, lambda i, j, k, f=b_index, local=local: f(j, local(k))))
        operands += [a, b]
    (out,), side_outs = _hosted_call(
        body, side, name, (m // tm, n // tn, total), in_specs, [pl.BlockSpec((tm, tn), lambda i, j, k: (i, j))],
        [jax.ShapeDtypeStruct((m, n), out_dtype)], [pltpu.VMEM((tm, tn), F32)], operands)
    return out if side is None else (out, side_outs)


class _SideJob:
    def __init__(self, arrays, out_shapes, aliases, n_sems, copies):
        self.arrays, self.out_shapes, self.aliases, self.n_sems, self.copies = arrays, out_shapes, aliases, n_sems, copies


def _hosted_call(body, side, name, grid, in_specs, out_specs, out_shape, scratch_shapes, operands):
    if side is None:
        outs = pl.pallas_call(body, name=name, grid=grid, in_specs=in_specs, out_specs=out_specs, out_shape=out_shape,
                              scratch_shapes=scratch_shapes, compiler_params=_cparams())(*operands)
        return outs, []
    n_in, n_out, ns_in, ns_out = len(in_specs), len(out_specs), len(side.arrays), len(side.out_shapes)

    def wrapped(*refs):
        main_in, side_in = refs[:n_in], refs[n_in:n_in + ns_in]
        rest = refs[n_in + ns_in:]
        main_out, side_out, rest = rest[:n_out], rest[n_out:n_out + ns_out], rest[n_out + ns_out:]
        scratch, send_sems, recv_sems = rest[:-2], rest[-2], rest[-1]
        first, last = None, None
        for axis, extent in enumerate(grid):
            at_start, at_end = pl.program_id(axis) == 0, pl.program_id(axis) == extent - 1
            first = at_start if first is None else first & at_start
            last = at_end if last is None else last & at_end

        @pl.when(first)
        def _():
            for cp in side.copies(side_in, side_out, send_sems, recv_sems):
                cp.start()

        body(*main_in, *main_out, *scratch)

        @pl.when(last)
        def _():
            for cp in side.copies(side_in, side_out, send_sems, recv_sems):
                cp.wait()

    hbm = pl.BlockSpec(memory_space=pl.ANY)
    outs = pl.pallas_call(
        wrapped, name=name, grid=grid, in_specs=list(in_specs) + [hbm] * ns_in,
        out_specs=list(out_specs) + [hbm] * ns_out, out_shape=list(out_shape) + list(side.out_shapes),
        scratch_shapes=list(scratch_shapes) + [pltpu.SemaphoreType.DMA((side.n_sems,))] * 2,
        input_output_aliases={n_in + i: n_out + o for i, o in side.aliases.items()},
        compiler_params=_cparams(),
    )(*operands, *side.arrays)
    return outs[:n_out], outs[n_out:]


def _ffn_up(name, h, wg, wu, side=None):
    s, d = h.shape
    nc, fs = wg.shape[0], wg.shape[2]
    tm, tk = _pick(s, 1024), _pick(d, 1024)
    nk = d // tk

    def body(h_ref, wg_ref, wu_ref, a_ref, b_ref, hid_ref, acc_g, acc_u):
        k = pl.program_id(2)

        @pl.when(k == 0)
        def _():
            acc_g[...] = jnp.zeros_like(acc_g)
            acc_u[...] = jnp.zeros_like(acc_u)

        hv = h_ref[...]
        acc_g[...] += _dot(hv, wg_ref[...], NN)
        acc_u[...] += _dot(hv, wu_ref[...], NN)

        @pl.when(k == nk - 1)
        def _():
            av, bv = acc_g[...], acc_u[...]
            a_ref[...] = av.astype(BF16)
            b_ref[...] = bv.astype(BF16)
            hid_ref[...] = (av * _sigmoid(av) * bv).astype(BF16)

    w_spec = pl.BlockSpec((None, tk, fs), lambda i, j, k: (j, k, 0))
    o_spec = pl.BlockSpec((None, tm, fs), lambda i, j, k: (j, i, 0))
    sh = jax.ShapeDtypeStruct((nc, s, fs), BF16)
    return _hosted_call(
        body, side, name, (s // tm, nc, nk), [pl.BlockSpec((tm, tk), lambda i, j, k: (i, k)), w_spec, w_spec],
        [o_spec] * 3, [sh] * 3, [pltpu.VMEM((tm, fs), F32), pltpu.VMEM((tm, fs), F32)], (h, wg, wu))


def _mm_plain(name, a, b, mode, out_dtype, add=None, a_fn=None, tm=512, tn=512, tk=512):
    if mode == "nn":
        m, n = a.shape[0], b.shape[1]
    elif mode == "nt":
        m, n = a.shape[0], b.shape[0]
    else:
        m, n = a.shape[1], b.shape[1]
    tm_, tn_ = _pick(m, tm), _pick(n, tn)
    spec = pl.BlockSpec((tm_, tn_), lambda i, j, k: (i, j))

    def epilogue(acc, extra_refs, out_refs):
        if add is not None:
            acc = acc + extra_refs[0][...]
        out_refs[0][...] = acc.astype(out_dtype)

    extra = () if add is None else (add,)
    (out,), _ = _mm(name, a, b, mode, [jax.ShapeDtypeStruct((m, n), out_dtype)], [spec], epilogue,
                    extra=extra, extra_specs=[spec] * len(extra), tm=tm, tn=tn, tk=tk, a_fn=a_fn)
    return out


def _row_tile(s, d):
    return _pick(s, max(SUBLANES, ROW_TILE_BYTES // (4 * d)))


def _prenorm_fwd(name, x, g, scale, shift):
    s, d = x.shape
    tr = _row_tile(s, d)

    def body(x_ref, g_ref, sc_ref, sh_ref, h_ref):
        xv = x_ref[...]
        r = lax.rsqrt(jnp.mean(xv * xv, axis=-1, keepdims=True) + RMS_EPS)
        h_ref[...] = ((xv * r * g_ref[...]) * (1.0 + sc_ref[...]) + sh_ref[...]).astype(BF16)

    row = pl.BlockSpec((tr, d), lambda i: (i, 0))
    vec = pl.BlockSpec((1, d), lambda i: (0, 0))
    return pl.pallas_call(body, name=name, grid=(s // tr,), in_specs=[row, vec, vec, vec], out_specs=row,
                          out_shape=jax.ShapeDtypeStruct((s, d), BF16), compiler_params=_cparams())(x, g, scale, shift)


def _prenorm_bwd(name, dh, x, g, scale, dx_res):
    s, d = x.shape
    tr = _row_tile(s, d)

    def body(dh_ref, x_ref, g_ref, sc_ref, dxr_ref, dx_ref, sums_ref):
        @pl.when(pl.program_id(0) == 0)
        def _():
            sums_ref[...] = jnp.zeros_like(sums_ref)

        xv, dhv, gv = x_ref[...], dh_ref[...].astype(F32), g_ref[...]
        r = lax.rsqrt(jnp.mean(xv * xv, axis=-1, keepdims=True) + RMS_EPS)
        xhat = xv * r
        dxn = dhv * (1.0 + sc_ref[...])
        dxhat = dxn * gv
        dx = r * (dxhat - xhat * jnp.mean(dxhat * xhat, axis=-1, keepdims=True))
        dx_ref[...] = dxr_ref[...] + dx
        sums_ref[0:1, :] += jnp.sum(dhv * (xhat * gv), axis=0, keepdims=True)
        sums_ref[1:2, :] += jnp.sum(dhv, axis=0, keepdims=True)
        sums_ref[2:3, :] += jnp.sum(dxn * xhat, axis=0, keepdims=True)

    row = pl.BlockSpec((tr, d), lambda i: (i, 0))
    vec = pl.BlockSpec((1, d), lambda i: (0, 0))
    acc = pl.BlockSpec((SUBLANES, d), lambda i: (0, 0))
    return pl.pallas_call(
        body, name=name, grid=(s // tr,), in_specs=[row, row, vec, vec, row], out_specs=[row, acc],
        out_shape=[jax.ShapeDtypeStruct((s, d), F32), jax.ShapeDtypeStruct((SUBLANES, d), F32)],
        compiler_params=_cparams())(dh, x, g, scale, dx_res)


def _postnorm_bwd(name, dxn, y, g, gate):
    s, d = y.shape
    tr = _row_tile(s, d)

    def body(dx_ref, y_ref, g_ref, gt_ref, dy_ref, sums_ref):
        @pl.when(pl.program_id(0) == 0)
        def _():
            sums_ref[...] = jnp.zeros_like(sums_ref)

        yv, dxv, gv = y_ref[...], dx_ref[...], g_ref[...]
        r = lax.rsqrt(jnp.mean(yv * yv, axis=-1, keepdims=True) + RMS_EPS)
        yhat = yv * r
        dn = dxv * gt_ref[...]
        dyhat = dn * gv
        dy_ref[...] = (r * (dyhat - yhat * jnp.mean(dyhat * yhat, axis=-1, keepdims=True))).astype(BF16)
        sums_ref[0:1, :] += jnp.sum(dxv * (yhat * gv), axis=0, keepdims=True)
        sums_ref[1:2, :] += jnp.sum(dn * yhat, axis=0, keepdims=True)

    row = pl.BlockSpec((tr, d), lambda i: (i, 0))
    vec = pl.BlockSpec((1, d), lambda i: (0, 0))
    acc = pl.BlockSpec((SUBLANES, d), lambda i: (0, 0))
    return pl.pallas_call(
        body, name=name, grid=(s // tr,), in_specs=[row, row, vec, vec], out_specs=[row, acc],
        out_shape=[jax.ShapeDtypeStruct((s, d), BF16), jax.ShapeDtypeStruct((SUBLANES, d), F32)],
        compiler_params=_cparams())(dxn, y, g, gate)


def _loss_grad(name, y, target):
    s, d = y.shape
    tr = _row_tile(s, d)

    def body(y_ref, t_ref, dy_ref, loss_ref):
        @pl.when(pl.program_id(0) == 0)
        def _():
            loss_ref[...] = jnp.zeros_like(loss_ref)

        err = y_ref[...] - t_ref[...]
        dy_ref[...] = err * (1.0 / d)
        part = jnp.sum(jnp.sum(err * err, axis=-1, keepdims=True), axis=0, keepdims=True) * (0.5 / d)
        loss_ref[...] += jnp.broadcast_to(part, loss_ref.shape)

    row = pl.BlockSpec((tr, d), lambda i: (i, 0))
    acc = pl.BlockSpec((SUBLANES, LANES), lambda i: (0, 0))
    return pl.pallas_call(
        body, name=name, grid=(s // tr,), in_specs=[row, row], out_specs=[row, acc],
        out_shape=[jax.ShapeDtypeStruct((s, d), F32), jax.ShapeDtypeStruct((SUBLANES, LANES), F32)],
        compiler_params=_cparams())(y, target)


def _gelu(y):
    c = math.sqrt(2.0 / math.pi)
    return 0.5 * y * (1.0 + jnp.tanh(c * (y + 0.044715 * (y * y * y))))


def _gelu_grad(y):
    c = math.sqrt(2.0 / math.pi)
    th = jnp.tanh(c * (y + 0.044715 * (y * y * y)))
    return 0.5 * (1.0 + th) + 0.5 * y * (1.0 - th * th) * c * (1.0 + 3.0 * 0.044715 * (y * y))


def _cmul_add(br, bi, ar, ai, xr, xi):
    return br + ar * xr - ai * xi, bi + ar * xi + ai * xr


def _scan_rows(x_ref, row0, n_steps, ns2, pow_ref, tab_ref, carry_ref, reverse, fold=None):
    assert n_steps % SUBLANES == 0
    wc = min(S5_CHUNK, ns2)
    sub = lax.broadcasted_iota(jnp.int32, (SUBLANES, wc), 0)
    unroll = S5_UNROLL if n_steps % S5_UNROLL == 0 else 1
    for c0 in range(0, ns2, wc):
        re = slice(c0, c0 + wc)
        im = slice(ns2 + c0, ns2 + c0 + wc)
        first_power = slice(n_steps - 1, n_steps) if reverse else slice(0, 1)
        ar = jnp.broadcast_to(pow_ref[first_power, re], (SUBLANES, wc))
        ai = jnp.broadcast_to(pow_ref[first_power, im], (SUBLANES, wc))
        rows = lambda r: pl.ds(pl.multiple_of(row0 + r * SUBLANES, SUBLANES), SUBLANES)
        step_of = lambda i: (n_steps - 1 - i) if reverse else i

        def local(i, carry, re=re, im=im, ar=ar, ai=ai):
            for u in range(unroll):
                r = step_of(i * unroll + u)
                carry = _cmul_add(x_ref[rows(r), re], x_ref[rows(r), im], ar, ai, *carry)
                x_ref[rows(r), re], x_ref[rows(r), im] = carry
            return carry

        zero = jnp.zeros((SUBLANES, wc), F32)
        lr, li = lax.fori_loop(0, n_steps // unroll, local, (zero, zero))

        tabs = [tab_ref[k, :, re] for k in range(8)]
        for lvl, k in enumerate((1, 2, 4)):
            sh = (SUBLANES - k) if reverse else k
            lr, li = _cmul_add(lr, li, tabs[2 * lvl], tabs[2 * lvl + 1], pltpu.roll(lr, sh, 0), pltpu.roll(li, sh, 0))
        cr, ci = carry_ref[0:1, re], carry_ref[0:1, im]
        lr, li = _cmul_add(lr, li, tabs[6], tabs[7], cr, ci)
        edge, away, last = (SUBLANES - 1, SUBLANES - 1, 0) if reverse else (0, 1, SUBLANES - 1)
        carry_ref[0:1, re] = lr[last:last + 1, :]
        carry_ref[0:1, im] = li[last:last + 1, :]
        er = jnp.where(sub == edge, cr, pltpu.roll(lr, away, 0))
        ei = jnp.where(sub == edge, ci, pltpu.roll(li, away, 0))

        def fix(j, acc, re=re, im=im, er=er, ei=ei, c0=c0):
            base = pl.ds(pl.multiple_of(j * SUBLANES, SUBLANES), SUBLANES)
            pw_r, pw_i = pow_ref[base, re], pow_ref[base, im]
            for i in range(SUBLANES):
                r = j * SUBLANES + i
                xr, xi = _cmul_add(x_ref[rows(r), re], x_ref[rows(r), im], pw_r[i:i + 1, :], pw_i[i:i + 1, :], er, ei)
                x_ref[rows(r), re], x_ref[rows(r), im] = xr, xi
                if fold is not None:
                    acc = fold(c0, r, xr, xi, acc)
            return acc

        acc = lax.fori_loop(0, n_steps // SUBLANES, fix, (zero, zero) if fold is not None else 0)
        if fold is not None:
            fold(c0, None, None, None, acc)


def _s5_fwd(name, u, b_blk, c_blk, a_f, tab_f, dskip, w_glu, b_glu):
    s, w = u.shape[0], w_glu.shape[0]
    nkb = w // LANES
    ns2 = b_blk.shape[2] // 2 * nkb
    half = ns2 // nkb
    t = min(S5_ROWS, s)
    nblk = s // t

    def body(u_ref, b_ref, c_ref, a_ref, tab_ref, ds_ref, wg_ref, bg_ref, y_ref, ys_ref, cs_ref, xs, carry):
        @pl.when(pl.program_id(0) == 0)
        def _():
            carry[...] = jnp.zeros_like(carry)

        cs_ref[0] = carry[...]
        for kb in range(nkb):
            bu = _dot(u_ref[:, kb * LANES:(kb + 1) * LANES], b_ref[kb], NN)
            xs[:, kb * half:(kb + 1) * half] = bu[:, :half]
            xs[:, ns2 + kb * half:ns2 + (kb + 1) * half] = bu[:, half:]
        _scan_rows(xs, 0, t // SUBLANES, ns2, a_ref, tab_ref, carry, reverse=False)
        for kb in range(nkb):
            cols = slice(kb * LANES, (kb + 1) * LANES)
            yk = _dot(xs[:, kb * half:(kb + 1) * half].astype(BF16), c_ref[kb, :half, :], NN)
            yk += _dot(xs[:, ns2 + kb * half:ns2 + (kb + 1) * half].astype(BF16), c_ref[kb, half:, :], NN)
            y_ref[:, cols] = yk + ds_ref[:, cols] * u_ref[:, cols].astype(F32)
        z = _gelu(y_ref[...])
        gate = _sigmoid(_dot(z.astype(BF16), wg_ref[...], NN) + bg_ref[...])
        ys_ref[...] = (z * gate).astype(BF16)

    row = pl.BlockSpec((t, w), lambda i: (i, 0))
    full = lambda shape: pl.BlockSpec(shape, lambda i: (0,) * len(shape))
    return pl.pallas_call(
        body, name=name, grid=(nblk,),
        in_specs=[row, full(b_blk.shape), full(c_blk.shape), full(a_f.shape), full(tab_f.shape), full(dskip.shape),
                  full(w_glu.shape), full(b_glu.shape)],
        out_specs=[row, row, pl.BlockSpec((1, 1, 2 * ns2), lambda i: (i, 0, 0))],
        out_shape=[jax.ShapeDtypeStruct((s, w), F32), jax.ShapeDtypeStruct((s, w), BF16),
                   jax.ShapeDtypeStruct((nblk, 1, 2 * ns2), F32)],
        scratch_shapes=[pltpu.VMEM((t, 2 * ns2), F32), pltpu.VMEM((1, 2 * ns2), F32)],
        compiler_params=_cparams(),
    )(u, b_blk, c_blk, a_f, tab_f, dskip, w_glu, b_glu)


def _s5_bwd(name, u, dys, y, carries, b_blk, c_blk, a_f, a_r, tab_f, tab_r, dskip, w_glu, b_glu, side=None):
    s, w = u.shape[0], w_glu.shape[0]
    nkb = w // LANES
    ns2 = b_blk.shape[2] // 2 * nkb
    half = ns2 // nkb
    t = min(S5_ROWS, s)
    nblk = s // t
    ng = t // SUBLANES

    def body(u_ref, dys_ref, y_ref, cs_ref, b_ref, c_ref, af_ref, ar_ref, tabf_ref, tabr_ref, ds_ref, wg_ref, bg_ref,
             du_ref, db_ref, dc_ref, da_ref, dwg_ref, vec_ref, xs, gs, dyv, fcarry, gcarry):
        @pl.when(pl.program_id(0) == 0)
        def _():
            db_ref[...] = jnp.zeros_like(db_ref)
            dc_ref[...] = jnp.zeros_like(dc_ref)
            da_ref[...] = jnp.zeros_like(da_ref)
            dwg_ref[...] = jnp.zeros_like(dwg_ref)
            vec_ref[...] = jnp.zeros_like(vec_ref)
            gcarry[...] = jnp.zeros_like(gcarry)

        yv = y_ref[...]
        z = _gelu(yv)
        zb = z.astype(BF16)
        gate = _sigmoid(_dot(zb, wg_ref[...], NN) + bg_ref[...])
        dout = dys_ref[...].astype(F32)
        dt = dout * z * gate * (1.0 - gate)
        dtb = dt.astype(BF16)
        dz = dout * gate + _dot(dtb, wg_ref[...], NT)
        dy = dz * _gelu_grad(yv)
        dyv[...] = dy
        dwg_ref[...] += _dot(zb, dtb, TN)
        vec_ref[0:1, :] += jnp.sum(dt, axis=0, keepdims=True)
        vec_ref[1:2, :] += jnp.sum(dy * u_ref[...].astype(F32), axis=0, keepdims=True)

        fcarry[...] = cs_ref[0]
        xs[0:SUBLANES, :] = jnp.broadcast_to(cs_ref[0], (SUBLANES, 2 * ns2))
        for kb in range(nkb):
            bu = _dot(u_ref[:, kb * LANES:(kb + 1) * LANES], b_ref[kb], NN)
            xs[SUBLANES:, kb * half:(kb + 1) * half] = bu[:, :half]
            xs[SUBLANES:, ns2 + kb * half:ns2 + (kb + 1) * half] = bu[:, half:]
        _scan_rows(xs, SUBLANES, ng, ns2, af_ref, tabf_ref, fcarry, reverse=False)
        first_segment = lax.broadcasted_iota(jnp.int32, (SUBLANES, 2 * ns2), 0) == 0
        xs[0:SUBLANES, :] = jnp.where(first_segment, xs[0:SUBLANES, :], pltpu.roll(xs[t:t + SUBLANES, :], 1, 0))

        for kb in range(nkb):
            dyk = dyv[:, kb * LANES:(kb + 1) * LANES].astype(BF16)
            re = slice(kb * half, (kb + 1) * half)
            im = slice(ns2 + kb * half, ns2 + (kb + 1) * half)
            gs[:, re] = _dot(dyk, c_ref[kb, :half, :], NT)
            gs[:, im] = _dot(dyk, c_ref[kb, half:, :], NT)
            dc_ref[kb, :half, :] += _dot(xs[SUBLANES:, re].astype(BF16), dyk, TN)
            dc_ref[kb, half:, :] += _dot(xs[SUBLANES:, im].astype(BF16), dyk, TN)

        def fold(c0, r, gr, gi, acc):
            wc = min(S5_CHUNK, ns2)
            re = slice(c0, c0 + wc)
            im = slice(ns2 + c0, ns2 + c0 + wc)
            if r is None:
                da_ref[:, re] += acc[0]
                da_ref[:, im] += acc[1]
                return acc
            before = pl.ds(pl.multiple_of(r * SUBLANES, SUBLANES), SUBLANES)
            xpr, xpi = xs[before, re], xs[before, im]
            return acc[0] + gr * xpr + gi * xpi, acc[1] - gr * xpi + gi * xpr

        _scan_rows(gs, 0, ng, ns2, ar_ref, tabr_ref, gcarry, reverse=True, fold=fold)

        for kb in range(nkb):
            cols = slice(kb * LANES, (kb + 1) * LANES)
            re = slice(kb * half, (kb + 1) * half)
            im = slice(ns2 + kb * half, ns2 + (kb + 1) * half)
            uk = u_ref[:, cols]
            gr = gs[:, re].astype(BF16)
            gi = gs[:, im].astype(BF16)
            db_ref[kb, :, :half] += _dot(uk, gr, TN)
            db_ref[kb, :, half:] += _dot(uk, gi, TN)
            duk = _dot(gr, b_ref[kb, :, :half], NT) + _dot(gi, b_ref[kb, :, half:], NT)
            du_ref[:, cols] = (duk + ds_ref[:, cols] * dyv[:, cols]).astype(BF16)

    rev = lambda i: (nblk - 1 - i, 0)
    row = pl.BlockSpec((t, w), rev)
    full = lambda shape: pl.BlockSpec(shape, lambda i: (0,) * len(shape))
    return _hosted_call(
        body, side, name, (nblk,),
        [row, row, row, pl.BlockSpec((1, 1, 2 * ns2), lambda i: (nblk - 1 - i, 0, 0)),
         full(b_blk.shape), full(c_blk.shape), full(a_f.shape), full(a_r.shape), full(tab_f.shape),
         full(tab_r.shape), full(dskip.shape), full(w_glu.shape), full(b_glu.shape)],
        [row, full(b_blk.shape), full(c_blk.shape), full((SUBLANES, 2 * ns2)), full((w, w)), full((SUBLANES, w))],
        [jax.ShapeDtypeStruct((s, w), BF16), jax.ShapeDtypeStruct(b_blk.shape, F32),
         jax.ShapeDtypeStruct(c_blk.shape, F32), jax.ShapeDtypeStruct((SUBLANES, 2 * ns2), F32),
         jax.ShapeDtypeStruct((w, w), F32), jax.ShapeDtypeStruct((SUBLANES, w), F32)],
        [pltpu.VMEM((t + SUBLANES, 2 * ns2), F32), pltpu.VMEM((t, 2 * ns2), F32),
         pltpu.VMEM((t, w), F32), pltpu.VMEM((1, 2 * ns2), F32), pltpu.VMEM((1, 2 * ns2), F32)],
        (u, dys, y, carries, b_blk, c_blk, a_f, a_r, tab_f, tab_r, dskip, w_glu, b_glu))


def _log_sigmoid(x):
    return jnp.minimum(x, 0.0) - jnp.log(1.0 + jnp.exp(-jnp.abs(x)))


def _cum_fwd(name, f_t, b_f):
    h, s = f_t.shape
    tc = _pick(s, 512)
    nb = s // tc

    def body(f_ref, b_ref, c_ref, carry):
        @pl.when(pl.program_id(0) == 0)
        def _():
            carry[...] = jnp.zeros_like(carry)

        lf = _log_sigmoid(f_ref[...] + b_ref[...])
        upper = (lax.broadcasted_iota(jnp.int32, (tc, tc), 0) <= lax.broadcasted_iota(jnp.int32, (tc, tc), 1))
        cum = lax.dot_general(lf, upper.astype(F32), NN, precision=lax.Precision.HIGHEST,
                              preferred_element_type=F32) + carry[...]
        c_ref[...] = cum
        carry[...] += jnp.sum(lf, axis=1, keepdims=True)

    blk = pl.BlockSpec((h, tc), lambda i: (0, i))
    return pl.pallas_call(body, name=name, grid=(nb,), in_specs=[blk, pl.BlockSpec((h, 1), lambda i: (0, 0))],
                          out_specs=blk, out_shape=jax.ShapeDtypeStruct((h, s), F32),
                          scratch_shapes=[pltpu.VMEM((h, 1), F32)], compiler_params=_cparams())(f_t, b_f)


def _cum_bwd(name, dcq, dck, f_t, b_f):
    h, s = f_t.shape
    tc = _pick(s, 512)
    nb = s // tc

    def body(dcq_ref, dck_ref, f_ref, b_ref, df_ref, db_ref, carry):
        @pl.when(pl.program_id(0) == 0)
        def _():
            carry[...] = jnp.zeros_like(carry)
            db_ref[...] = jnp.zeros_like(db_ref)

        dc = dcq_ref[...] + dck_ref[...]
        lower = (lax.broadcasted_iota(jnp.int32, (tc, tc), 0) >= lax.broadcasted_iota(jnp.int32, (tc, tc), 1))
        dlf = lax.dot_general(dc, lower.astype(F32), NN, precision=lax.Precision.HIGHEST,
                              preferred_element_type=F32) + carry[...]
        carry[...] += jnp.sum(dc, axis=1, keepdims=True)
        df = dlf * _sigmoid(-(f_ref[...] + b_ref[...]))
        df_ref[...] = df
        db_ref[...] += jnp.broadcast_to(jnp.sum(df, axis=1, keepdims=True), db_ref.shape)

    blk = pl.BlockSpec((h, tc), lambda i: (0, nb - 1 - i))
    return pl.pallas_call(
        body, name=name, grid=(nb,), in_specs=[blk, blk, blk, pl.BlockSpec((h, 1), lambda i: (0, 0))],
        out_specs=[blk, pl.BlockSpec((h, LANES), lambda i: (0, 0))],
        out_shape=[jax.ShapeDtypeStruct((h, s), F32), jax.ShapeDtypeStruct((h, LANES), F32)],
        scratch_shapes=[pltpu.VMEM((h, 1), F32)], compiler_params=_cparams())(dcq, dck, f_t, b_f)


def _attn_fwd(name, qkv, q_blk, k_blk, v_blk, n_pairs, ck, side=None):
    s = qkv.shape[0]
    dh = LANES // 2
    t = min(ATT_BLOCK, s)
    nq = s // t
    scale = dh ** -0.5

    def body(q_ref, k_ref, v_ref, ck_ref, o_ref, lse_ref, m_s, acc_s):
        i = pl.program_id(1)
        low = lax.broadcasted_iota(jnp.int32, (1, LANES), 1) < dh
        qs = (q_ref[...].astype(F32) * scale).astype(BF16)
        zero = jnp.zeros_like(qs)
        qh = (jnp.where(low, qs, zero), jnp.where(low, zero, qs))
        m_s[...] = jnp.full(m_s.shape, -1e30, F32)
        acc_s[...] = jnp.zeros_like(acc_s)
        causal = (lax.broadcasted_iota(jnp.int32, (t, t), 1) <= lax.broadcasted_iota(jnp.int32, (t, t), 0))

        def step(j, diagonal):
            r0 = pl.multiple_of(j * t, t)
            kj = k_ref[pl.ds(r0, t), :]
            vj = v_ref[pl.ds(r0, t), :]
            one = jnp.ones_like(vj)
            vh = (jnp.where(low, vj, one), jnp.where(low, one, vj))
            for hd in range(2):
                sc = _dot(qh[hd], kj, NT) - ck_ref[hd, j]
                if diagonal:
                    sc = jnp.where(causal, sc, -1e30)
                m_old = m_s[hd]
                m_new = jnp.maximum(m_old, jnp.max(sc, axis=1, keepdims=True))
                p = jnp.exp(sc - m_new)
                acc_s[hd] = jnp.exp(m_old - m_new) * acc_s[hd] + _dot(p.astype(BF16), vh[hd], NN)
                m_s[hd] = m_new

        def full(j, _):
            step(j, False)
            return 0

        lax.fori_loop(0, i, full, 0)
        step(i, True)
        a0, a1 = acc_s[0], acc_s[1]
        o_ref[...] = jnp.where(low, a0 / pltpu.roll(a0, dh, 1), a1 / pltpu.roll(a1, dh, 1)).astype(BF16)
        lse_ref[0] = m_s[0] + jnp.log(a0[:, dh:dh + 1])
        lse_ref[1] = m_s[1] + jnp.log(a1[:, 0:1])

    return _hosted_call(
        body, side, name, (n_pairs, nq),
        [pl.BlockSpec((t, LANES), lambda hp, i: (i, q_blk + hp)),
         pl.BlockSpec((s, LANES), lambda hp, i: (0, k_blk + hp)),
         pl.BlockSpec((s, LANES), lambda hp, i: (0, v_blk + hp)),
         pl.BlockSpec((2, nq, 1, t), lambda hp, i: (hp, 0, 0, 0))],
        [pl.BlockSpec((t, LANES), lambda hp, i: (i, hp)), pl.BlockSpec((2, t, 1), lambda hp, i: (hp, i, 0))],
        [jax.ShapeDtypeStruct((s, LANES * n_pairs), BF16), jax.ShapeDtypeStruct((2 * n_pairs, s, 1), F32)],
        [pltpu.VMEM((2, t, 1), F32), pltpu.VMEM((2, t, LANES), F32)], (qkv, qkv, qkv, ck))


def _attn_bwd(name, qkv, q_blk, k_blk, v_blk, n_pairs, o, do, lse_rows, ck_cols, side=None):
    s = qkv.shape[0]
    dh = LANES // 2
    t = min(ATT_BLOCK, s)
    nk = s // t
    scale = dh ** -0.5

    def body(q_ref, k_ref, v_ref, o_ref, do_ref, lse_ref, ck_ref,
             dq_ref, dk_ref, dv_ref, dcq_ref, dck_ref, delta, dqt, dk_acc, dv_acc):
        j = pl.program_id(1)
        low = lax.broadcasted_iota(jnp.int32, (1, LANES), 1) < dh
        low_rows = lax.broadcasted_iota(jnp.int32, (LANES, 1), 0) < dh

        @pl.when(j == 0)
        def _():
            dqt[...] = jnp.zeros_like(dqt)
            sel = (jnp.broadcast_to(low, (SUBLANES, LANES)).astype(F32), jnp.broadcast_to(~low, (SUBLANES, LANES)).astype(F32))

            def fill(i, _):
                r0 = pl.multiple_of(i * t, t)
                prod = do_ref[pl.ds(r0, t), :].astype(F32) * o_ref[pl.ds(r0, t), :].astype(F32)
                for hd in range(2):
                    delta[hd, i] = lax.dot_general(sel[hd], prod, NT, precision=lax.Precision.HIGHEST,
                                                   preferred_element_type=F32)
                return 0

            lax.fori_loop(0, nk, fill, 0)

        kj, vj = k_ref[...], v_ref[...]
        zero, one = jnp.zeros_like(kj), jnp.ones_like(kj)
        kh = (jnp.where(low, kj, zero), jnp.where(low, zero, kj))
        vh = (jnp.where(low, vj, zero), jnp.where(low, zero, vj))
        kjt = kj.astype(F32).T.astype(BF16)
        one_t = jnp.ones_like(kjt)
        kht = (jnp.where(low_rows, kjt, one_t), jnp.where(low_rows, one_t, kjt))
        dk_acc[...] = jnp.zeros_like(dk_acc)
        dv_acc[...] = jnp.zeros_like(dv_acc)
        causal_t = (lax.broadcasted_iota(jnp.int32, (t, t), 0) <= lax.broadcasted_iota(jnp.int32, (t, t), 1))

        def step(i, diagonal):
            r0 = pl.multiple_of(i * t, t)
            qi = (q_ref[pl.ds(r0, t), :].astype(F32) * scale).astype(BF16)
            doi = do_ref[pl.ds(r0, t), :]
            qone, dzero = jnp.ones_like(qi), jnp.zeros_like(doi)
            qsel = (jnp.where(low, qi, qone), jnp.where(low, qone, qi))
            dosel = (jnp.where(low, doi, dzero), jnp.where(low, dzero, doi))
            for hd in range(2):
                st = _dot(kh[hd], qi, NT) - ck_ref[hd] - lse_ref[hd, i]
                pt = jnp.exp(st)
                if diagonal:
                    pt = jnp.where(causal_t, pt, 0.0)
                dst = pt * (_dot(vh[hd], doi, NT) - delta[hd, i, 0:1, :])
                dsb = dst.astype(BF16)
                dv_acc[...] += _dot(pt.astype(BF16), dosel[hd], NN)
                dk_acc[hd] += _dot(dsb, qsel[hd], NN)
                dqt[hd, i] += _dot(kht[hd], dsb, NN)

        step(j, True)

        def rest(i, _):
            step(i, False)
            return 0

        lax.fori_loop(j + 1, nk, rest, 0)
        dk_ref[...] = jnp.where(low, dk_acc[0], dk_acc[1]).astype(BF16)
        dv_ref[...] = dv_acc[...].astype(BF16)
        dck_ref[0] = -dk_acc[0][:, dh:dh + 1]
        dck_ref[1] = -dk_acc[1][:, 0:1]

        @pl.when(j == nk - 1)
        def _():
            def emit(i, _):
                r0 = pl.multiple_of(i * t, t)
                d0, d1 = dqt[0, i], dqt[1, i]
                dq_ref[pl.ds(r0, t), :] = (jnp.where(low_rows, d0, d1) * scale).T.astype(BF16)
                dcq_ref[0, i] = d0[dh:dh + 1, :]
                dcq_ref[1, i] = d1[0:1, :]
                return 0

            lax.fori_loop(0, nk, emit, 0)

    col_blk = lambda base: pl.BlockSpec((t, LANES), lambda hp, j: (j, base + hp))
    col_all = lambda base: pl.BlockSpec((s, LANES), lambda hp, j: (0, base + hp))
    rows_all = pl.BlockSpec((2, nk, 1, t), lambda hp, j: (hp, 0, 0, 0))
    return _hosted_call(
        body, side, name, (n_pairs, nk),
        [col_all(q_blk), col_blk(k_blk), col_blk(v_blk), col_all(0), col_all(0), rows_all,
         pl.BlockSpec((2, t, 1), lambda hp, j: (hp, j, 0))],
        [col_all(0), col_blk(0), col_blk(0), rows_all, pl.BlockSpec((2, t, 1), lambda hp, j: (hp, j, 0))],
        [jax.ShapeDtypeStruct((s, LANES * n_pairs), BF16), jax.ShapeDtypeStruct((s, LANES * n_pairs), BF16),
         jax.ShapeDtypeStruct((s, LANES * n_pairs), BF16), jax.ShapeDtypeStruct((2 * n_pairs, nk, 1, t), F32),
         jax.ShapeDtypeStruct((2 * n_pairs, s, 1), F32)],
        [pltpu.VMEM((2, nk, SUBLANES, t), F32), pltpu.VMEM((2, nk, LANES, t), F32),
         pltpu.VMEM((2, t, LANES), F32), pltpu.VMEM((t, LANES), F32)],
        (qkv, qkv, qkv, o, do, lse_rows, ck_cols))


def _adamw(name, w, g, m, v, side=None):
    n_l, r, c = w.shape
    by_rows = r % SUBLANES == 0
    tr = _pick8(r, max(SUBLANES, ROW_TILE_BYTES // (4 * c))) if by_rows else r
    tl = 1 if by_rows else max(t for t in range(1, n_l + 1) if n_l % t == 0 and t * r * c * 4 <= ROW_TILE_BYTES)

    def body(w_ref, g_ref, m_ref, v_ref, d_ref, mo_ref, vo_ref):
        gv = g_ref[...]
        m2 = ADAM_B1 * m_ref[...] + (1.0 - ADAM_B1) * gv
        v2 = ADAM_B2 * v_ref[...] + (1.0 - ADAM_B2) * (gv * gv)
        m_hat = m2 / (1.0 - ADAM_B1 ** ADAM_STEP)
        v_hat = v2 / (1.0 - ADAM_B2 ** ADAM_STEP)
        d_ref[...] = -ADAM_LR * (m_hat / (jnp.sqrt(v_hat) + ADAM_EPS) + ADAM_WD * w_ref[...])
        mo_ref[...] = m2
        vo_ref[...] = v2

    blk = pl.BlockSpec((None, tr, c) if by_rows else (tl, r, c), lambda l, i: (l, i, 0))
    sh = jax.ShapeDtypeStruct((n_l, r, c), F32)
    outs, side_outs = _hosted_call(body, side, name, (n_l // tl, r // tr), [blk] * 4, [blk] * 3, [sh, sh, sh], [],
                                   (w, g, m, v))
    return outs if side is None else (outs, side_outs)


def _pick8(dim, target, mult=SUBLANES):
    best, t = None, mult
    while t <= min(dim, target):
        if dim % t == 0:
            best = t
        t += mult
    return best or dim


BF16_ROWS = 16


def _sum_blocks(name, x, out_dtype):
    n, r, c = x.shape
    tr = _pick8(r, max(BF16_ROWS, SUM_TILE_BYTES // (4 * c)), BF16_ROWS)

    def body(x_ref, o_ref):
        acc = x_ref[0].astype(F32)
        for i in range(1, n):
            acc = acc + x_ref[i].astype(F32)
        o_ref[...] = acc.astype(out_dtype)

    return pl.pallas_call(body, name=name, grid=(r // tr,),
                          in_specs=[pl.BlockSpec((n, tr, c), lambda i: (0, i, 0))],
                          out_specs=pl.BlockSpec((tr, c), lambda i: (i, 0)),
                          out_shape=jax.ShapeDtypeStruct((r, c), out_dtype), compiler_params=_cparams())(x)


def _all_gather(name, x_shard):
    m_per, n = x_shard.shape

    def body(x_ref, out_ref, send_sems, recv_sems):
        x, y, c = lax.axis_index("x"), lax.axis_index("y"), lax.axis_index("c")
        me, sibling = (x, y, c), (x, y, 1 - c)
        chips = [(1 - x, y), (x, 1 - y), (1 - x, 1 - y)]

        def rows(px, py, pc):
            return out_ref.at[pl.ds((4 * px + 2 * py + pc) * m_per, m_per), :]

        def copy(k, block, to, src=None):
            return pltpu.make_async_remote_copy(
                src_ref=rows(*block) if src is None else src, dst_ref=rows(*block),
                send_sem=send_sems.at[k], recv_sem=recv_sems.at[k], device_id=to, device_id_type=MESH)

        first = [copy(0, me, sibling, src=x_ref)]
        first += [copy(1 + j, me, (*chip, c), src=x_ref) for j, chip in enumerate(chips)]
        for cp in first:
            cp.start()
        passed = [copy(4 + j, (*chip, c), sibling) for j, chip in enumerate(chips)]
        for j, chip in enumerate(chips):
            copy(1 + j, (*chip, c), me).wait_recv()
            passed[j].start()
        copy(0, sibling, me).wait_recv()
        for j, chip in enumerate(chips):
            copy(4 + j, (*chip, 1 - c), me).wait_recv()
        for cp in first + passed:
            cp.wait_send()

    out = pl.pallas_call(
        body, name=name, out_shape=jax.ShapeDtypeStruct((N_DEV * m_per, n), x_shard.dtype),
        in_specs=[pl.BlockSpec(memory_space=pl.ANY)], out_specs=pl.BlockSpec(memory_space=pl.ANY),
        scratch_shapes=[pltpu.SemaphoreType.DMA((7,)), pltpu.SemaphoreType.DMA((7,))],
    )(x_shard)
    my_dev = 4 * lax.axis_index("x") + 2 * lax.axis_index("y") + lax.axis_index("c")
    return lax.dynamic_update_slice(out, x_shard, (my_dev * m_per, 0))


def _put_own(out, own, index):
    start = tuple(index) + (0,) * own.ndim
    return lax.dynamic_update_slice(out, own.reshape((1,) * len(index) + own.shape), start)


def _gather_copies(stage, ins, outs, send_sems, recv_sems):
    x, y, c = lax.axis_index("x"), lax.axis_index("y"), lax.axis_index("c")
    my_chip = 2 * x + y
    copies = []
    for w, out in enumerate(outs):
        half = out.shape[1] // 2
        rows = pl.ds(c * half, half)
        for k, (cx, cy) in enumerate([(1 - x, y), (x, 1 - y), (1 - x, 1 - y)]):
            if stage == 0:
                src, dst, to = ins[w].at[rows], out.at[my_chip, rows], (cx, cy, c)
            else:
                src = dst = out.at[2 * cx + cy, rows]
                to = (x, y, 1 - c)
            copies.append(pltpu.make_async_remote_copy(
                src_ref=src, dst_ref=dst, send_sem=send_sems.at[3 * w + k], recv_sem=recv_sems.at[3 * w + k],
                device_id=to, device_id_type=MESH))
    return copies


def _gathered_shapes(shards):
    return [jax.ShapeDtypeStruct((N_CHIPS,) + s.shape, s.dtype) for s in shards]


def _put_own_slabs(gathered, shards):
    my_chip = 2 * lax.axis_index("x") + lax.axis_index("y")
    return [_put_own(o, s, (my_chip,)) for o, s in zip(gathered, shards)]


def _gather_layer(name, shards):
    n_w = len(shards)

    def body(*refs):
        ins, outs = refs[:n_w], refs[n_w:2 * n_w]
        for stage in (0, 1):
            copies = _gather_copies(stage, ins, outs, refs[2 * n_w + 2 * stage], refs[2 * n_w + 2 * stage + 1])
            for cp in copies:
                cp.start()
            for cp in copies:
                cp.wait()

    outs = pl.pallas_call(
        body, name=name, out_shape=_gathered_shapes(shards),
        in_specs=[pl.BlockSpec(memory_space=pl.ANY)] * n_w, out_specs=[pl.BlockSpec(memory_space=pl.ANY)] * n_w,
        scratch_shapes=[pltpu.SemaphoreType.DMA((3 * n_w,))] * 4,
    )(*shards)
    return _put_own_slabs(outs, shards)


def _gather_side_jobs(shards):
    between_chips = _SideJob(list(shards), _gathered_shapes(shards), {}, 3 * len(shards),
                             lambda ins, outs, send, recv: _gather_copies(0, ins, outs, send, recv))
    between_cores = lambda partial: _SideJob(
        list(partial), [jax.ShapeDtypeStruct(p.shape, p.dtype) for p in partial], {w: w for w in range(len(partial))},
        3 * len(partial), lambda ins, outs, send, recv: _gather_copies(1, ins, outs, send, recv))
    return between_chips, between_cores


def _run_job(name, job):
    n_in, n_out = len(job.arrays), len(job.out_shapes)

    def body(*refs):
        copies = job.copies(refs[:n_in], refs[n_in:n_in + n_out], refs[n_in + n_out], refs[n_in + n_out + 1])
        for cp in copies:
            cp.start()
        for cp in copies:
            cp.wait()

    hbm = pl.BlockSpec(memory_space=pl.ANY)
    return pl.pallas_call(
        body, name=name, out_shape=list(job.out_shapes), in_specs=[hbm] * n_in, out_specs=[hbm] * n_out,
        scratch_shapes=[pltpu.SemaphoreType.DMA((job.n_sems,))] * 2, input_output_aliases=dict(job.aliases),
    )(*job.arrays)


def _swap_job(grads):
    def copies(ins, outs, send_sems, recv_sems):
        x, y, c = lax.axis_index("x"), lax.axis_index("y"), lax.axis_index("c")
        return [pltpu.make_async_remote_copy(
            src_ref=g.at[:, pl.ds((1 - c) * (g.shape[1] // 2), g.shape[1] // 2)], dst_ref=outs[w],
            send_sem=send_sems.at[w], recv_sem=recv_sems.at[w], device_id=(x, y, 1 - c), device_id_type=MESH)
            for w, g in enumerate(ins)]

    shapes = [jax.ShapeDtypeStruct((g.shape[0], g.shape[1] // 2, g.shape[2]), g.dtype) for g in grads]
    return _SideJob(list(grads), shapes, {}, len(grads), copies)


def _exchange_job(parts):
    def copies(ins, outs, send_sems, recv_sems):
        x, y, c = lax.axis_index("x"), lax.axis_index("y"), lax.axis_index("c")
        return [pltpu.make_async_remote_copy(
            src_ref=ins[w].at[2 * cx + cy], dst_ref=outs[w].at[2 * x + y], send_sem=send_sems.at[3 * w + k],
            recv_sem=recv_sems.at[3 * w + k], device_id=(cx, cy, c), device_id_type=MESH)
            for w in range(len(ins)) for k, (cx, cy) in enumerate([(1 - x, y), (x, 1 - y), (1 - x, 1 - y)])]

    return _SideJob(list(parts), [jax.ShapeDtypeStruct(p.shape, p.dtype) for p in parts], {}, 3 * len(parts), copies)


def _share_job(bufs, layers):
    def copies(ins, outs, send_sems, recv_sems):
        x, y, c = lax.axis_index("x"), lax.axis_index("y"), lax.axis_index("c")
        mine = [o.at[layer, pl.ds(c * (o.shape[1] // 2), o.shape[1] // 2)] for o, ls in zip(outs, layers) for layer in ls]
        return [pltpu.make_async_remote_copy(src_ref=rows, dst_ref=rows, send_sem=send_sems.at[k], recv_sem=recv_sems.at[k],
                                             device_id=(x, y, 1 - c), device_id_type=MESH) for k, rows in enumerate(mine)]

    return _SideJob(list(bufs), [jax.ShapeDtypeStruct(b.shape, b.dtype) for b in bufs], {w: w for w in range(len(bufs))},
                    sum(len(ls) for ls in layers), copies)


def _sum_into(name, blocks, core, layer, depth, into):
    n, r, c = blocks.shape
    tr = _pick8(r, max(BF16_ROWS, SUM_TILE_BYTES // (4 * c)), BF16_ROWS)
    steps = r // tr

    def body(core_ref, x_ref, *rest):
        acc = x_ref[0].astype(F32)
        for i in range(1, n):
            acc = acc + x_ref[i].astype(F32)
        rest[-1][...] = acc

    grid_spec = pltpu.PrefetchScalarGridSpec(
        num_scalar_prefetch=1, grid=(steps,),
        in_specs=[pl.BlockSpec((n, tr, c), lambda i, core_ref: (0, i, 0))]
        + ([pl.BlockSpec(memory_space=pl.ANY)] if into is not None else []),
        out_specs=pl.BlockSpec((None, tr, c), lambda i, core_ref: (layer, core_ref[0] * steps + i, 0)))
    return pl.pallas_call(
        body, name=name, grid_spec=grid_spec, out_shape=jax.ShapeDtypeStruct((depth, 2 * r, c), F32),
        input_output_aliases={2: 0} if into is not None else {}, compiler_params=_cparams(),
    )(core, blocks, *([into] if into is not None else []))


def _add_rows(name, grads, recv, core):
    n, r, c = recv.shape
    tr = _pick8(r, max(BF16_ROWS, SUM_TILE_BYTES // (4 * c)), BF16_ROWS)
    steps = r // tr

    def body(core_ref, g_ref, r_ref, o_ref):
        o_ref[...] = (g_ref[...].astype(F32) + r_ref[...].astype(F32)).astype(BF16)

    grid_spec = pltpu.PrefetchScalarGridSpec(
        num_scalar_prefetch=1, grid=(steps,),
        in_specs=[pl.BlockSpec((n, tr, c), lambda i, core_ref: (0, core_ref[0] * steps + i, 0)),
                  pl.BlockSpec((n, tr, c), lambda i, core_ref: (0, i, 0))],
        out_specs=pl.BlockSpec((n, tr, c), lambda i, core_ref: (0, i, 0)))
    return pl.pallas_call(body, name=name, grid_spec=grid_spec,
                          out_shape=jax.ShapeDtypeStruct((n, r, c), BF16), compiler_params=_cparams())(core, grads, recv)


class _LayerReduce:
    def __init__(self, tag, layer, depth, names, grads, core, bufs):
        self.tag, self.layer, self.depth, self.names, self.core, self.bufs = tag, layer, depth, list(names), core, bufs
        self.state = list(grads)

    def _exchange(self, name, job, carry):
        if carry is None:
            return None, _run_job(f"{name}_{self.tag}", job)
        return carry(job)

    def swap_and_add(self, carry=None):
        grads = self.state
        results, recv = self._exchange("grads_swap_cores", _swap_job(grads), carry)
        self.state = [_add_rows(f"grads_add_{n}_{self.tag}", g, r, self.core) for n, g, r in zip(self.names, grads, recv)]
        return results

    def exchange_and_sum(self, carry=None, also=()):
        group = [self] + list(also)
        results, arrived = self._exchange("grads_exchange_chips", _exchange_job([p for r in group for p in r.state]), carry)
        my_chip = 2 * lax.axis_index("x") + lax.axis_index("y")
        for r in group:
            mine, arrived = arrived[:len(r.state)], arrived[len(r.state):]
            for n, a, p in zip(r.names, mine, r.state):
                a = _put_own(a, lax.dynamic_index_in_dim(p, my_chip, 0, keepdims=False), (my_chip,))
                r.bufs[n] = _sum_into(f"grads_sum_{n}_{r.tag}", a, r.core, r.layer, r.depth, r.bufs.get(n))
        return results

    def share(self, carry=None, also=()):
        layers = {}
        for r in [self] + list(also):
            for n in r.names:
                layers.setdefault(n, []).append(r.layer)
        names = list(layers)
        job = _share_job([self.bufs[n] for n in names], [layers[n] for n in names])
        results, outs = self._exchange("grads_share_cores", job, carry)
        self.bufs.update(zip(names, outs))
        return results


def _pack(arrays, cols, row_multiple, dtype):
    flat = jnp.concatenate([a.reshape(-1).astype(dtype) for a in arrays])
    unit = cols * row_multiple
    total = -(-flat.shape[0] // unit) * unit
    return jnp.pad(flat, (0, total - flat.shape[0])).reshape(total // cols, cols)


def _unpack(buf, shapes):
    flat, out, off = buf.reshape(-1), [], 0
    for sh in shapes:
        n = math.prod(sh)
        out.append(flat[off:off + n].reshape(sh))
        off += n
    return out


def _discretize(lam_re, lam_im, log_dt, b_re, b_im):
    lam = lax.complex(jnp.minimum(lam_re, -EIG_CLIP), lam_im)
    dt = jnp.exp(log_dt)[:, None]
    lam_bar = jnp.exp(lam * dt)
    b_bar = ((lam_bar - 1.0) / lam)[..., None] * lax.complex(b_re, b_im)
    return jnp.real(lam_bar), jnp.imag(lam_bar), jnp.real(b_bar), jnp.imag(b_bar)


def _scan_tables(ar, ai):
    a = lax.complex(ar, ai)
    pw = [a]
    for _ in range(7):
        pw.append(pw[-1] * a)
    rows = jnp.arange(SUBLANES)[:, None]

    def build(p, reverse):
        tabs = []
        for k in (1, 2, 4):
            keep = (rows <= SUBLANES - 1 - k) if reverse else (rows >= k)
            tk = jnp.where(keep, p[k - 1][None, :], 0.0)
            tabs += [jnp.real(tk), jnp.imag(tk)]
        stack = jnp.stack(p[::-1] if reverse else p)
        tabs += [jnp.real(stack), jnp.imag(stack)]
        return jnp.stack(tabs).astype(F32)

    return build(pw, False), build([jnp.conj(p) for p in pw], True)


def _interleave_rows(a, t):
    s, w = a.shape
    return a.reshape(s // t, SUBLANES, t // SUBLANES, w).transpose(0, 2, 1, 3).reshape(s, w)


def _deinterleave_rows(a, t):
    s, w = a.shape
    return a.reshape(s // t, t // SUBLANES, SUBLANES, w).transpose(0, 2, 1, 3).reshape(s, w)


def _block_diag(per_group, groups_per_block):
    g, a, b = per_group.shape
    x = per_group.reshape(g // groups_per_block, groups_per_block, a, b)
    eye = jnp.eye(groups_per_block, dtype=per_group.dtype)
    out = x[:, :, :, None, :] * eye[None, :, None, :, None]
    return out.reshape(g // groups_per_block, groups_per_block * a, groups_per_block * b)


def _block_diag_extract(dense, groups_per_block, a, b):
    nkb = dense.shape[0]
    x = dense.reshape(nkb, groups_per_block, a, groups_per_block, b)
    idx = jnp.arange(groups_per_block)
    return x[:, idx, :, idx, :].transpose(1, 0, 2, 3).reshape(nkb * groups_per_block, a, b)


def _layer_fwd(tag, x, mod, p, wts, carried=None):
    s, d = x.shape
    w_ssm, w_att = p["w_glu"].shape[0], p["w_att"]
    heads = p["b_f"].shape[0]
    dh = w_att // heads
    cs = d // N_CHIPS
    tm = _pick(s, 1024)
    row = lambda v: v.reshape(1, -1)
    sv = {}

    h = _prenorm_fwd(f"prenorm_mix_{tag}", x, row(p["g_pre_mix"]), row(mod[1]), row(mod[0]))
    uqkv = _mm_plain(f"proj_main_{tag}", h, p["w_main"], "nn", BF16, tm=1024, tn=1024, tk=1024)
    fg = _mm_plain(f"proj_gate_{tag}", h, p["w_gates"], "nn", F32, tm=1024, tn=1024, tk=1024)
    f_t = fg[:, 2 * d:2 * d + heads].T

    t5 = min(S5_ROWS, s)
    u_il = _interleave_rows(uqkv[:, :w_ssm], t5)
    y_s5, ys_il, carries = _s5_fwd(f"s5_fwd_{tag}", u_il, p["b_blk"], p["c_blk"], p["a_f"], p["tab_f"],
                                   row(p["d_skip"]), p["w_glu"], row(p["b_glu"]))
    ys = _deinterleave_rows(ys_il, t5)

    assert dh * 2 == LANES and w_ssm % LANES == 0 and w_att % LANES == 0
    n_pairs = w_att // LANES
    blocks = (w_ssm // LANES, w_ssm // LANES + n_pairs, w_ssm // LANES + 2 * n_pairs)
    cum = _cum_fwd(f"cum_fwd_{tag}", f_t, p["b_f"].reshape(heads, 1))
    t = min(ATT_BLOCK, s)
    ck_cols, ck_rows = cum.reshape(heads, s, 1), cum.reshape(heads, s // t, 1, t)
    late_names, late_shards, next_shards = carried if carried else ((), [], [])
    chips_job, cores_job = _gather_side_jobs(list(late_shards) + list(next_shards)) if carried else (None, None)
    (ya, lse), arrived = _attn_fwd(f"attn_fwd_{tag}", uqkv, *blocks, n_pairs, ck_rows, side=chips_job)
    if late_names:
        late = _run_job(f"gather_weights_late_{tag}", cores_job(arrived[:len(late_names)]))
        wts = {**wts, **dict(zip(late_names, _put_own_slabs(late, late_shards)))}
        arrived = arrived[len(late_names):]
    fs = wts["w_ffn_down"].shape[1]

    tile = pl.BlockSpec((tm, cs), lambda i, j, k: (i, j))
    slab = lambda rows: pl.BlockSpec((None, rows, cs), lambda i, j, k: (j, 0, 0))

    def merge(acc, extra_refs, out_refs):
        ya_ref, wpb_ref, ga_ref, gb_ref = extra_refs
        a_ref, b_ref, m_ref = out_refs
        bv = _dot(ya_ref[...], wpb_ref[...], NN)
        a_ref[...] = acc.astype(BF16)
        b_ref[...] = bv.astype(BF16)
        m_ref[...] = (_sigmoid(ga_ref[...]) * acc + _sigmoid(gb_ref[...]) * bv).astype(BF16)

    sd_bf = jax.ShapeDtypeStruct((s, d), BF16)
    pa, pb, merged = _mm_raw(
        f"merge_{tag}", ys, wts["w_pa"], "nn", (s // tm, N_CHIPS, 1), (tm, cs),
        pl.BlockSpec((tm, w_ssm), lambda i, j, k: (i, 0)), slab(w_ssm), [sd_bf] * 3, [tile] * 3, merge,
        extra=(ya, wts["w_pb"], fg, fg),
        extra_specs=[pl.BlockSpec((tm, w_att), lambda i, j, k: (i, 0)), slab(w_att), tile,
                     pl.BlockSpec((tm, cs), lambda i, j, k: (i, j + N_CHIPS))])

    tm2 = _pick(s, POSTNORM_ROWS)
    x1, y_mix = _mm_postnorm(
        f"out_proj_{tag}", merged, pl.BlockSpec((tm2, cs), lambda i, j, k: (i, k)), wts["w_o"],
        pl.BlockSpec((None, cs, d), lambda i, j, k: (k, 0, 0)), N_CHIPS, x, row(mod[2]), row(p["g_post_mix"]))

    h2 = _prenorm_fwd(f"prenorm_ffn_{tag}", x1, row(p["g_pre_ffn"]), row(mod[4]), row(mod[3]))
    (a4, b4, hid4), next_wts = _ffn_up(f"ffn_up_{tag}", h2, wts["w_ffn_gate"], wts["w_ffn_up"],
                                       side=cores_job(arrived) if carried and arrived else None)
    x2, y_ffn = _mm_postnorm(
        f"ffn_down_{tag}", hid4, pl.BlockSpec((None, tm2, fs), lambda i, j, k: (k, i, 0)), wts["w_ffn_down"],
        pl.BlockSpec((None, fs, d), lambda i, j, k: (k, 0, 0)), N_CHIPS, x1, row(mod[5]), row(p["g_post_ffn"]))

    sv.update(x=x, h=h, uqkv=uqkv, u_il=u_il, fg=fg, f_t=f_t, y_s5=y_s5, ys=ys, carries=carries, blocks=blocks,
              ck_cols=ck_cols, lse_rows=lse.reshape(heads, s // t, 1, t), ya=ya, pa=pa, pb=pb, merged=merged, x1=x1,
              y_mix=y_mix, h2=h2, a4=a4, b4=b4, hid4=hid4, y_ffn=y_ffn)
    return x2, sv, wts, next_wts


def _mm_postnorm(name, a, a_spec, w, w_spec, nk, x, gate, g):
    s, d = x.shape
    tm = _pick(s, POSTNORM_ROWS)
    rowspec = pl.BlockSpec((tm, d), lambda i, j, k: (i, 0))
    vec = pl.BlockSpec((1, d), lambda i, j, k: (0, 0))

    def epilogue(acc, extra_refs, out_refs):
        x_ref, gate_ref, g_ref = extra_refs
        r = lax.rsqrt(jnp.mean(acc * acc, axis=-1, keepdims=True) + RMS_EPS)
        out_refs[0][...] = x_ref[...] + gate_ref[...] * (acc * r * g_ref[...])
        out_refs[1][...] = acc

    sd = jax.ShapeDtypeStruct((s, d), F32)
    return _mm_raw(name, a, w, "nn", (s // tm, 1, nk), (tm, d), a_spec, w_spec, [sd, sd], [rowspec, rowspec], epilogue,
                   extra=(x, gate, g), extra_specs=[rowspec, vec, vec])


def _layer_bwd(tag, dx2, mod, p, wts, sv, reduce_later=None, early=None):
    s, d = dx2.shape
    w_ssm, w_att = p["w_glu"].shape[0], wts["w_pb"].shape[1]
    heads = p["b_f"].shape[0]
    cs = d // N_CHIPS
    fs = wts["w_ffn_down"].shape[1]
    tm, tk, td = _pick(s, 1024), _pick(s, 1024), d
    row = lambda v: v.reshape(1, -1)
    gr = {}

    def dw_slabs(name, act, act_spec, rows, dy, dy_spec, cols, grid_mn, out_index):
        return _mm_raw(name, act, dy, "tn", grid_mn + (s // tk,), (rows, cols), act_spec, dy_spec,
                       [jax.ShapeDtypeStruct((N_CHIPS,) + out_index[1], BF16)],
                       [pl.BlockSpec((None, rows, cols), out_index[0])], _store(BF16))[0]

    dy_ffn, sums = _postnorm_bwd(f"postnorm_bwd_ffn_{tag}", dx2, sv["y_ffn"], row(p["g_post_ffn"]), row(mod[5]))
    d_gate_f, gr["g_post_ffn"] = sums[0], sums[1]
    gr["w_ffn_down"] = dw_slabs(f"dw_down_{tag}", sv["hid4"], pl.BlockSpec((None, tk, fs), lambda i, j, k: (i, k, 0)), fs,
                                dy_ffn, pl.BlockSpec((tk, d), lambda i, j, k: (k, 0)), d, (N_CHIPS, 1),
                                (lambda i, j, k: (i, 0, 0), (fs, d)))

    def swiglu_bwd(acc, extra_refs, out_refs):
        av, bv = extra_refs[0][...].astype(F32), extra_refs[1][...].astype(F32)
        sg = _sigmoid(av)
        out_refs[0][...] = (acc * bv * (sg * (1.0 + av * (1.0 - sg)))).astype(BF16)
        out_refs[1][...] = (acc * (av * sg)).astype(BF16)

    blk4 = pl.BlockSpec((None, tm, fs), lambda i, j, k: (j, i, 0))
    sh4 = jax.ShapeDtypeStruct((N_CHIPS, s, fs), BF16)
    ffn_down_bwd = lambda side: _mm_raw(
        f"ffn_down_bwd_{tag}", dy_ffn, wts["w_ffn_down"], "nt", (s // tm, N_CHIPS, 1), (tm, fs),
        pl.BlockSpec((tm, d), lambda i, j, k: (i, 0)), pl.BlockSpec((None, fs, d), lambda i, j, k: (j, 0, 0)),
        [sh4, sh4], [blk4, blk4], swiglu_bwd, extra=(sv["a4"], sv["b4"]), extra_specs=[blk4, blk4], side=side)
    da4, db4 = reduce_later.swap_and_add(ffn_down_bwd) if reduce_later else ffn_down_bwd(None)
    for n, act4 in (("w_ffn_gate", da4), ("w_ffn_up", db4)):
        gr[n] = dw_slabs(f"d{n}_{tag}", sv["h2"], pl.BlockSpec((tk, td), lambda i, j, k: (k, i)), td,
                         act4, pl.BlockSpec((None, tk, fs), lambda i, j, k: (j, k, 0)), fs, (d // td, N_CHIPS),
                         (lambda i, j, k: (j, i, 0), (d, fs)))
    pairs = [(act4, (None, tm, fs), lambda i, kk: (kk, i, 0), wts[n], (None, td, fs), lambda j, kk: (kk, j, 0),
              N_CHIPS) for n, act4 in (("w_ffn_gate", da4), ("w_ffn_up", db4))]
    own_early = None
    if early is not None:
        own_early = _LayerReduce(f"{tag}e", early[0], early[1], EARLY_REDUCED, [gr[n] for n in EARLY_REDUCED], *early[2:])
    dh_ffn = lambda side: _mm_sum(f"dh_ffn_{tag}", s, d, tm, td, pairs, F32, side=side)
    dh2 = own_early.swap_and_add(dh_ffn) if own_early else dh_ffn(None)
    dx1, sums = _prenorm_bwd(f"prenorm_bwd_ffn_{tag}", dh2, sv["x1"], row(p["g_pre_ffn"]), row(mod[4]), dx2)
    d_scale_f, d_shift_f, gr["g_pre_ffn"] = sums[0], sums[1], sums[2]

    dy_mix, sums = _postnorm_bwd(f"postnorm_bwd_mix_{tag}", dx1, sv["y_mix"], row(p["g_post_mix"]), row(mod[2]))
    d_gate_m, gr["g_post_mix"] = sums[0], sums[1]
    gr["w_o"] = dw_slabs(f"dw_o_{tag}", sv["merged"], pl.BlockSpec((tk, cs), lambda i, j, k: (k, i)), cs,
                         dy_mix, pl.BlockSpec((tk, d), lambda i, j, k: (k, 0)), d, (N_CHIPS, 1),
                         (lambda i, j, k: (i, 0, 0), (cs, d)))

    tile = pl.BlockSpec((tm, cs), lambda i, j, k: (i, j))

    def merge_bwd(acc, extra_refs, out_refs):
        a_ref, b_ref, ga_ref, gb_ref = extra_refs
        sa, sb = _sigmoid(ga_ref[...]), _sigmoid(gb_ref[...])
        out_refs[0][...] = (acc * sa).astype(BF16)
        out_refs[1][...] = (acc * sb).astype(BF16)
        out_refs[2][...] = (acc * a_ref[...].astype(F32) * sa * (1.0 - sa)).astype(BF16)
        out_refs[3][...] = (acc * b_ref[...].astype(F32) * sb * (1.0 - sb)).astype(BF16)

    sd_bf = jax.ShapeDtypeStruct((s, d), BF16)
    d_pa, d_pb, d_ga, d_gb = _mm_raw(
        f"out_proj_bwd_{tag}", dy_mix, wts["w_o"], "nt", (s // tm, N_CHIPS, 1), (tm, cs),
        pl.BlockSpec((tm, d), lambda i, j, k: (i, 0)), pl.BlockSpec((None, cs, d), lambda i, j, k: (j, 0, 0)),
        [sd_bf] * 4, [tile] * 4, merge_bwd, extra=(sv["pa"], sv["pb"], sv["fg"], sv["fg"]),
        extra_specs=[tile, tile, tile, pl.BlockSpec((tm, cs), lambda i, j, k: (i, j + N_CHIPS))])
    branches = (("w_pa", sv["ys"], w_ssm, d_pa), ("w_pb", sv["ya"], w_att, d_pb))
    for n, act, width, d_p in branches:
        gr[n] = dw_slabs(f"d{n}_{tag}", act, pl.BlockSpec((tk, width), lambda i, j, k: (k, 0)), width,
                         d_p, pl.BlockSpec((tk, cs), lambda i, j, k: (k, j)), cs, (1, N_CHIPS),
                         (lambda i, j, k: (j, 0, 0), (width, cs)))
    own_mid = None
    if early is not None:
        own_mid = _LayerReduce(f"{tag}m", early[0], early[1], MID_REDUCED, [gr[n] for n in MID_REDUCED], *early[2:])
    d_branch = {}
    for n, act, width, d_p in branches:
        d_in = lambda side, n=n, width=width, d_p=d_p: _mm_raw(
            f"d_in_{n}_{tag}", d_p, wts[n], "nt", (s // tm, 1, N_CHIPS), (tm, width),
            pl.BlockSpec((tm, cs), lambda i, j, k: (i, k)), pl.BlockSpec((None, width, cs), lambda i, j, k: (k, 0, 0)),
            [jax.ShapeDtypeStruct((s, width), BF16)], [pl.BlockSpec((tm, width), lambda i, j, k: (i, 0))], _store(BF16),
            side=side)
        d_branch[n] = (own_mid.swap_and_add(d_in) if own_mid and n == branches[0][0] else d_in(None))[0]
    d_ys, d_ya = d_branch["w_pa"], d_branch["w_pb"]
    own = [r for r in (own_early, own_mid) if r is not None]

    attn_bwd = lambda side: _attn_bwd(f"attn_bwd_{tag}", sv["uqkv"], *sv["blocks"], w_att // LANES, sv["ya"], d_ya,
                                      sv["lse_rows"], sv["ck_cols"], side=side)
    dq, dk, dv, dcq, dck = (reduce_later.exchange_and_sum(attn_bwd, also=[own_mid] if own_mid else [])
                            if reduce_later else attn_bwd(None)[0])
    d_f_t, d_bf = _cum_bwd(f"cum_bwd_{tag}", dcq.reshape(heads, s), dck.reshape(heads, s), sv["f_t"],
                           p["b_f"].reshape(heads, 1))
    gr["b_f"] = d_bf[:, 0]

    t5 = min(S5_ROWS, s)
    s5_bwd = lambda side: _s5_bwd(
        f"s5_bwd_{tag}", sv["u_il"], _interleave_rows(d_ys, t5), sv["y_s5"], sv["carries"], p["b_blk"], p["c_blk"],
        p["a_f"], p["a_r"], p["tab_f"], p["tab_r"], row(p["d_skip"]), p["w_glu"], row(p["b_glu"]), side=side)
    du_il, d_bblk, d_cblk, d_abar, d_wglu, vec = own_early.exchange_and_sum(s5_bwd) if own_early else s5_bwd(None)[0]
    du = _deinterleave_rows(du_il, t5)
    gr["w_glu"] = d_wglu.astype(BF16).reshape(N_CHIPS, w_ssm // N_CHIPS, w_ssm)
    gr["b_glu"], gr["d_skip"] = vec[0], vec[1]
    gr["b_blk"], gr["c_blk"], gr["a_bar"] = d_bblk, d_cblk, d_abar

    d_f = jnp.pad(d_f_t.T, ((0, 0), (0, F_PAD - heads))).astype(BF16)
    assert w_ssm % w_att == 0 and (2 * d) % F_PAD == 0
    first = w_ssm // w_att
    main_pieces = [(du, w_ssm, 0), (dq, w_att, first), (dk, w_att, first + 1), (dv, w_att, first + 2)]
    dw = [_mm_plain(f"dw_in{n}_{tag}", sv["h"], piece, "tn", BF16, tm=1024, tn=1024, tk=1024)
          for n, piece in enumerate([du, dq, dk, dv, d_f, d_ga, d_gb])]
    w_in_grad = jnp.concatenate(dw[:4] + [dw[4][:, :heads], dw[5], dw[6]], axis=1)
    gr["w_in"] = w_in_grad.reshape(d, N_CHIPS, w_in_grad.shape[1] // N_CHIPS).transpose(1, 0, 2)
    tmx, tkx = _pick(s, 1024), _pick(d, 512)
    pairs = [(piece, (tmx, width), lambda i, kk: (i, 0), p["w_main"], (d, width), lambda j, kk, blk=blk: (j, blk), 1)
             for piece, width, blk in main_pieces]
    steps = d // tkx
    pairs += [(piece, (tmx, tkx), lambda i, kk: (i, kk), p["w_gates"], (d, tkx), lambda j, kk, off=off: (j, off + kk), steps)
              for piece, off in ((d_ga, 0), (d_gb, steps))]
    pairs.append((d_f, (tmx, F_PAD), lambda i, kk: (i, 0), p["w_gates"], (d, F_PAD), lambda j, kk: (j, 2 * d // F_PAD), 1))
    dh_mix = lambda side: _mm_sum(f"dh_mix_{tag}", s, d, tmx, d, pairs, F32, side=side)
    dh1 = reduce_later.share(dh_mix, also=own) if reduce_later else dh_mix(None)
    dx0, sums = _prenorm_bwd(f"prenorm_bwd_mix_{tag}", dh1, sv["x"], row(p["g_pre_mix"]), row(mod[1]), dx1)
    d_scale_m, d_shift_m, gr["g_pre_mix"] = sums[0], sums[1], sums[2]

    d_mod = jnp.stack([d_shift_m, d_scale_m, d_gate_m, d_shift_f, d_scale_f, d_gate_f])
    return dx0, d_mod, gr


BIG = ("w_in", "w_glu", "w_pa", "w_pb", "w_o", "w_ffn_gate", "w_ffn_up", "w_ffn_down")
FIRST_USED = ("w_in", "w_glu")
EARLY_REDUCED = ("w_ffn_gate", "w_ffn_up", "w_ffn_down")
MID_REDUCED = ("w_pa", "w_pb", "w_o")
SMALL = ("b_ada", "g_pre_mix", "g_post_mix", "g_pre_ffn", "g_post_ffn", "lam_re", "lam_im", "log_dt", "b_re", "b_im",
         "c_re", "c_im", "d_skip", "b_glu", "b_f")
WEIGHTS = ("w_ada", "b_ada", "g_pre_mix", "g_post_mix", "g_pre_ffn", "g_post_ffn", "w_in", "lam_re", "lam_im", "log_dt",
           "b_re", "b_im", "c_re", "c_im", "d_skip", "w_glu", "b_glu", "b_f", "w_pa", "w_pb", "w_o", "w_ffn_gate",
           "w_ffn_up", "w_ffn_down")


def _prepare_layer(wts, small, l, seq):
    w_in = jnp.concatenate([wts["w_in"][j] for j in range(N_CHIPS)], axis=1)
    d = w_in.shape[0]
    heads = small["b_f"].shape[1]
    n_groups, n_state, group_ch = small["b_re"].shape[1:]
    w_ssm = n_groups * group_ch
    w_att = (w_in.shape[1] - w_ssm - heads - 2 * d) // 3
    n_main = w_ssm + 3 * w_att
    gpb = LANES // group_ch
    p = {"w_att": w_att}
    p["w_main"] = w_in[:, :n_main]
    p["w_gates"] = jnp.concatenate(
        [w_in[:, n_main + heads:], w_in[:, n_main:n_main + heads], jnp.zeros((d, F_PAD - heads), BF16)], axis=1)
    p["w_glu"] = wts["w_glu"].reshape(w_ssm, w_ssm)
    for n in ("g_pre_mix", "g_post_mix", "g_pre_ffn", "g_post_ffn", "d_skip", "b_glu", "b_f"):
        p[n] = small[n][l]
    ar, ai, br, bi = _discretize(small["lam_re"][l], small["lam_im"][l], small["log_dt"][l], small["b_re"][l], small["b_im"][l])
    n_steps = min(S5_ROWS, seq) // SUBLANES
    powers = jnp.cumprod(jnp.broadcast_to(lax.complex(ar, ai).reshape(1, -1), (n_steps, ar.size)), axis=0)
    p["a_f"] = jnp.concatenate([jnp.real(powers), jnp.imag(powers)], axis=1)
    p["a_r"] = jnp.concatenate([jnp.real(powers[::-1]), -jnp.imag(powers[::-1])], axis=1)
    p["tab_f"], p["tab_r"] = _scan_tables(jnp.real(powers[-1]), jnp.imag(powers[-1]))
    bre = _block_diag(br.transpose(0, 2, 1), gpb)
    bim = _block_diag(bi.transpose(0, 2, 1), gpb)
    p["b_blk"] = jnp.concatenate([bre, bim], axis=2).astype(BF16)
    cre = _block_diag(small["c_re"][l].transpose(0, 2, 1), gpb)
    cim = _block_diag(small["c_im"][l].transpose(0, 2, 1), gpb)
    p["c_blk"] = jnp.concatenate([cre, -cim], axis=1).astype(BF16)
    return p


def _compact_partials(gr, n_state, group_ch):
    gpb = LANES // group_ch
    half = gpb * n_state
    out = dict(gr)
    out["bbar_re"] = _block_diag_extract(gr["b_blk"][:, :, :half], gpb, group_ch, n_state).transpose(0, 2, 1)
    out["bbar_im"] = _block_diag_extract(gr["b_blk"][:, :, half:], gpb, group_ch, n_state).transpose(0, 2, 1)
    out["c_re"] = _block_diag_extract(gr["c_blk"][:, :half, :], gpb, n_state, group_ch).transpose(0, 2, 1)
    out["c_im"] = -_block_diag_extract(gr["c_blk"][:, half:, :], gpb, n_state, group_ch).transpose(0, 2, 1)
    return out


def _small_grads_from_partials(gr, small, l):
    n_groups, n_state, _ = small["b_re"].shape[1:]
    ns2 = n_groups * n_state
    d_abar = jnp.sum(gr["a_bar"], axis=0)
    dar, dai = d_abar[:ns2].reshape(n_groups, n_state), d_abar[ns2:].reshape(n_groups, n_state)
    args = (small["lam_re"][l], small["lam_im"][l], small["log_dt"][l], small["b_re"][l], small["b_im"][l])
    _, vjp = jax.vjp(_discretize, *args)
    d_lam_re, d_lam_im, d_log_dt, d_b_re, d_b_im = vjp((dar, dai, gr["bbar_re"], gr["bbar_im"]))
    return dict(lam_re=d_lam_re, lam_im=d_lam_im, log_dt=d_log_dt, b_re=d_b_re, b_im=d_b_im,
                c_re=gr["c_re"], c_im=gr["c_im"])


def _fwd_bwd(xs, target, mods, small, wts0, later, core=None, late0=None):
    depth = 1 + len(later)
    saved, layers, wts = [], [], [wts0]
    act = xs
    for l in range(depth):
        layers.append(_prepare_layer(wts[l], small, l, xs.shape[0]))
        shards = later[l] if l + 1 < depth and not isinstance(later[l], dict) else None
        late = late0 if l == 0 and late0 else ((), [])
        carried = (late[0], late[1], shards or []) if (late[0] or shards) else None
        act, sv, wts[l], gathered = _layer_fwd(str(l), act, mods[l], layers[l], wts[l], carried=carried)
        saved.append(sv)
        if l + 1 < depth:
            wts.append(dict(zip(BIG, _put_own_slabs(gathered, shards))) if shards is not None else later[l])
    dx, loss_blk = _loss_grad("loss", act, target)
    grads, d_mods = [None] * depth, [None] * depth
    pending, bufs = None, {}
    for l in reversed(range(depth)):
        early = (l, depth, core, bufs) if pending is not None else None
        dx, d_mods[l], grads[l] = _layer_bwd(str(l), dx, mods[l], layers[l], wts[l], saved[l], reduce_later=pending,
                                             early=early)
        if core is not None:
            names = [n for n in BIG if early is None or n not in EARLY_REDUCED + MID_REDUCED]
            pending = _LayerReduce(str(l), l, depth, names, [grads[l][n] for n in names], core, bufs)
    if core is None:
        return loss_blk, dx, d_mods, grads, None
    pending.swap_and_add()
    return loss_blk, dx, d_mods, grads, pending


def kernel(x, c, w_ada, b_ada, g_pre_mix, g_post_mix, g_pre_ffn, g_post_ffn, w_in, lam_re, lam_im, log_dt, b_re, b_im, c_re, c_im, d_skip, w_glu, b_glu, b_f, w_pa, w_pb, w_o, w_ffn_gate, w_ffn_up, w_ffn_down, loss_target, m_w_ada, m_b_ada, m_g_pre_mix, m_g_post_mix, m_g_pre_ffn, m_g_post_ffn, m_w_in, m_lam_re, m_lam_im, m_log_dt, m_b_re, m_b_im, m_c_re, m_c_im, m_d_skip, m_w_glu, m_b_glu, m_b_f, m_w_pa, m_w_pb, m_w_o, m_w_ffn_gate, m_w_ffn_up, m_w_ffn_down, v_w_ada, v_b_ada, v_g_pre_mix, v_g_post_mix, v_g_pre_ffn, v_g_post_ffn, v_w_in, v_lam_re, v_lam_im, v_log_dt, v_b_re, v_b_im, v_c_re, v_c_im, v_d_skip, v_w_glu, v_b_glu, v_b_f, v_w_pa, v_w_pb, v_w_o, v_w_ffn_gate, v_w_ffn_up, v_w_ffn_down):
    local = dict(locals())
    weights = {n: local[n] for n in WEIGHTS}
    moments_m = {n: local["m_" + n] for n in WEIGHTS}
    moments_v = {n: local["v_" + n] for n in WEIGHTS}
    depth, d = g_pre_mix.shape
    n_mod = w_ada.shape[2] * N_CHIPS // d
    mx, my, mc = lax.axis_index("x"), lax.axis_index("y"), lax.axis_index("c")
    my_chip = 2 * mx + my
    my_dev = 4 * mx + 2 * my + mc
    xs = x[0]

    shards = [[weights[n][l].astype(BF16) for n in BIG] for l in range(depth)]
    early = [i for i, n in enumerate(BIG) if n in FIRST_USED]
    late = [i for i, n in enumerate(BIG) if n not in FIRST_USED]
    wts0 = dict(zip([BIG[i] for i in early], _gather_layer("gather_weights_0", [shards[0][i] for i in early])))
    late0 = ([BIG[i] for i in late], [shards[0][i] for i in late])
    small = {n: weights[n] for n in SMALL}

    c_pad = jnp.pad(c, ((0, SUBLANES - 1), (0, 0)))
    c_all = _all_gather("gather_cond", c_pad).reshape(N_DEV, SUBLANES, d)[:, 0, :]
    silu = lambda v: v * _sigmoid(v)
    n_cols = w_ada.shape[2]
    mod_shard = []
    for l in range(depth):
        bias = lax.dynamic_slice_in_dim(b_ada[l], my_chip * n_cols, n_cols)
        mod_shard.append(_mm_plain(f"ada_{l}", c_all, w_ada[l], "nn", F32, add=jnp.broadcast_to(bias, (N_DEV, n_cols)),
                                   a_fn=silu, tm=N_DEV, tn=512, tk=1024))
    mod_block = jnp.concatenate(mod_shard, axis=1)
    mod_all = _all_gather("gather_mod", mod_block).reshape(N_DEV, N_DEV, depth, n_cols)
    mod_rows = lax.dynamic_index_in_dim(mod_all[0::2], my_dev, axis=1, keepdims=False)
    mods = [mod_rows[:, l, :].reshape(n_mod, d) for l in range(depth)]

    loss_blk, dx, d_mods, grads, last_reduce = _fwd_bwd(xs, loss_target[0], mods, small, wts0, shards[1:],
                                                      core=mc.astype(jnp.int32).reshape(1), late0=late0)
    loss = lax.psum(loss_blk[0, 0], ("x", "y", "c"))
    grad_x = dx[None]

    partial_names = ("g_pre_mix", "g_post_mix", "g_pre_ffn", "g_post_ffn", "d_skip", "b_glu", "b_f", "a_bar",
                     "bbar_re", "bbar_im", "c_re", "c_im")
    n_state, group_ch = b_re.shape[2:]
    contrib = list(d_mods)
    for l in range(depth):
        compact = _compact_partials(grads[l], n_state, group_ch)
        contrib += [compact[n] for n in partial_names]
    contrib_shapes = [a.shape for a in contrib]
    block = _pack(contrib, LANES, BF16_ROWS, F32)
    rows = block.shape[0]
    all_blocks = _all_gather("gather_small_grads", block).reshape(N_DEV, rows, LANES)
    summed = _unpack(_sum_blocks("sum_small_grads", all_blocks, F32), contrib_shapes)
    per_layer = len(partial_names)
    small_grads = {n: [] for n in SMALL}
    d_mod_all = []
    for l in range(depth):
        small_grads["b_ada"].append(summed[l].reshape(-1))
        gl = dict(zip(partial_names, summed[depth + l * per_layer:depth + (l + 1) * per_layer]))
        for n in ("g_pre_mix", "g_post_mix", "g_pre_ffn", "g_post_ffn", "d_skip", "b_glu", "b_f"):
            small_grads[n].append(gl[n])
        for n, gval in _small_grads_from_partials(gl, small, l).items():
            small_grads[n].append(gval)
        mod_rows_ = n_mod * d // LANES
        d_mod_all.append(all_blocks[:, l * mod_rows_:(l + 1) * mod_rows_, :].reshape(N_DEV, n_mod * d))
    small_grads = {n: jnp.stack(v) for n, v in small_grads.items()}

    g_w_ada = []
    for l in range(depth):
        cols = lax.dynamic_slice_in_dim(d_mod_all[l], my_chip * n_cols, n_cols, axis=1)
        g_w_ada.append(_mm_plain(f"dw_ada_{l}", c_all, cols, "tn", F32, a_fn=silu, tm=512, tn=512, tk=N_DEV))
    all_grads = dict(small_grads)
    all_grads["w_ada"] = jnp.stack(g_w_ada)

    delta, new_m, new_v = {}, {}, {}
    ada_step = lambda side: _adamw("adamw_w_ada", w_ada, all_grads["w_ada"], m_w_ada, v_w_ada, side=side)
    delta["w_ada"], new_m["w_ada"], new_v["w_ada"] = last_reduce.exchange_and_sum(ada_step)
    last_reduce.share()
    all_grads.update(last_reduce.bufs)
    for n in BIG:
        last = weights[n].shape[2]
        to_stored, from_stored = ((0, 1, 2),) * 2 if last % LANES == 0 else ((0, 2, 1),) * 2 if last % SUBLANES == 0 \
            else ((2, 0, 1), (1, 2, 0))
        view, back = (lambda a: a.transpose(to_stored)), (lambda a: a.transpose(from_stored))
        outs = _adamw(f"adamw_{n}", view(weights[n]), view(all_grads[n]), view(moments_m[n]), view(moments_v[n]))
        delta[n], new_m[n], new_v[n] = (back(o) for o in outs)
    small_shapes = [weights[n].shape for n in SMALL]
    packed = [_pack([src[n] for n in SMALL], LANES, SUBLANES, F32)[None] for src in (weights, all_grads, moments_m, moments_v)]
    outs = _adamw("adamw_small", *packed)
    for dst, buf in zip((delta, new_m, new_v), outs):
        dst.update(dict(zip(SMALL, _unpack(buf[0], small_shapes))))

    return (loss, grad_x, *[all_grads[n] for n in WEIGHTS], *[delta[n] for n in WEIGHTS],
            *[new_m[n] for n in WEIGHTS], *[new_v[n] for n in WEIGHTS])
```
